```python
import math
import jax, jax.numpy as jnp
from jax import lax
import numpy as np

D_MODEL = 1024
BATCH = 32
SEQ = 2048
DEPTH = 1

CHUNK = 64
Q_BLOCK = 128
PLE_DIM = 256
GDN_HEADS = 4
GDN_DK = 128
GDN_DV = 128
GDN_CONV = 4
MLA_HEADS = 4
MLA_NOPE = 128
MLA_ROPE = 64
MLA_V = 128
MLA_Q_LORA = 384
MLA_KV_LORA = 256
ROPE_THETA = 10000.0
D_FF = 2816
FFN_CONV = 3
ALPHA = (2.0 * DEPTH) ** 0.25
BETA = (8.0 * DEPTH) ** -0.25
NORM_EPS = 1e-6
GDN_QK = GDN_HEADS * GDN_DK
GDN_VW = GDN_HEADS * GDN_DV
D_IN = 2 * GDN_QK + 2 * GDN_VW + 2 * GDN_HEADS + MLA_Q_LORA + MLA_KV_LORA + MLA_ROPE
D_MIX = GDN_VW + MLA_HEADS * MLA_V

kernel_name = "hybrid_gdn_mla_convffn_deepnorm"


def _rmsnorm(x, g):
    xf = x.astype(jnp.float32)
    y = xf * lax.rsqrt(jnp.mean(xf * xf, axis=-1, keepdims=True) + NORM_EPS)
    return (y * g.astype(jnp.float32)).astype(x.dtype)


def _layernorm(x, g, b):
    xf = x.astype(jnp.float32)
    mu = jnp.mean(xf, axis=-1, keepdims=True)
    xc = xf - mu
    var = jnp.mean(xc * xc, axis=-1, keepdims=True)
    y = xc * lax.rsqrt(var + NORM_EPS) * g.astype(jnp.float32) + b.astype(jnp.float32)
    return y.astype(x.dtype)


def _l2norm(x):
    return x * lax.rsqrt(jnp.sum(x * x, axis=-1, keepdims=True) + NORM_EPS)


def _causal_dwconv(x, w):
    k = w.shape[0]
    return lax.conv_general_dilated(
        x, w[:, None, :].astype(x.dtype), window_strides=(1,), padding=[(k - 1, 0)],
        dimension_numbers=("NWC", "WIO", "NWC"), feature_group_count=x.shape[-1])


def _rope_tables(seq):
    inv = ROPE_THETA ** (-jnp.arange(0, MLA_ROPE, 2, dtype=jnp.float32) / MLA_ROPE)
    ang = jnp.arange(seq, dtype=jnp.float32)[:, None] * inv[None, :]
    return jnp.cos(ang), jnp.sin(ang)


def _apply_rope(x, cos, sin):
    xf = x.astype(jnp.float32)
    x1, x2 = jnp.split(xf, 2, axis=-1)
    return jnp.concatenate([x1 * cos - x2 * sin, x1 * sin + x2 * cos], axis=-1).astype(x.dtype)


def _chunk_gated_delta_rule(q, k, v, g, beta):
    bsz, seq, nh, dk = q.shape
    dv = v.shape[-1]
    n = seq // CHUNK

    def blocks(t):
        t = t.reshape((bsz, n, CHUNK, nh) + t.shape[3:])
        return jnp.moveaxis(jnp.swapaxes(t, 2, 3), 1, 0)

    qc, kc, vc = blocks(q), blocks(k), blocks(v)
    bc = blocks(beta)
    gc = jnp.cumsum(blocks(g), axis=-1)
    idx = jnp.arange(CHUNK)
    incl = idx[:, None] >= idx[None, :]
    strict = idx[:, None] > idx[None, :]
    decay = jnp.exp(jnp.where(incl, gc[..., :, None] - gc[..., None, :], -jnp.inf))
    kb = kc * bc[..., None]
    lower = jnp.where(strict, jnp.einsum("nbhcd,nbhsd->nbhcs", kb, kc) * decay, 0.0)
    tri = lower + jnp.eye(CHUNK, dtype=jnp.float32)
    w = lax.linalg.triangular_solve(tri, kb * jnp.exp(gc)[..., None], left_side=True, lower=True,
                                    unit_diagonal=True)
    u = lax.linalg.triangular_solve(tri, vc * bc[..., None], left_side=True, lower=True,
                                    unit_diagonal=True)
    qk = jnp.einsum("nbhcd,nbhsd->nbhcs", qc, kc) * decay
    qg = qc * jnp.exp(gc)[..., None]
    kd = kc * jnp.exp(gc[..., -1:] - gc)[..., None]
    glast = jnp.exp(gc[..., -1])

    def step(state, xs):
        w_n, u_n, qk_n, qg_n, kd_n, gl_n = xs
        v_new = u_n - jnp.einsum("bhck,bhkv->bhcv", w_n, state)
        o_n = jnp.einsum("bhck,bhkv->bhcv", qg_n, state) + jnp.einsum("bhcs,bhsv->bhcv", qk_n, v_new)
        state = state * gl_n[..., None, None] + jnp.einsum("bhck,bhcv->bhkv", kd_n, v_new)
        return state, o_n

    s0 = jnp.zeros((bsz, nh, dk, dv), jnp.float32)
    _, o = lax.scan(step, s0, (w, u, qk, qg, kd, glast))
    return jnp.swapaxes(jnp.moveaxis(o, 0, 1), 2, 3).reshape(bsz, seq, nh, dv)


def _gated_deltanet(qkv, z, a, b, conv_w, a_log, dt_bias, norm_g):
    bsz, seq, _ = qkv.shape
    h = jax.nn.silu(_causal_dwconv(qkv, conv_w)).astype(jnp.float32)
    q, k, v = jnp.split(h, [GDN_QK, 2 * GDN_QK], axis=-1)
    q = _l2norm(q.reshape(bsz, seq, GDN_HEADS, GDN_DK)) * (GDN_DK ** -0.5)
    k = _l2norm(k.reshape(bsz, seq, GDN_HEADS, GDN_DK))
    v = v.reshape(bsz, seq, GDN_HEADS, GDN_DV)
    beta = jax.nn.sigmoid(b.astype(jnp.float32))
    g = -jnp.exp(a_log.astype(jnp.float32)) * jax.nn.softplus(
        a.astype(jnp.float32) + dt_bias.astype(jnp.float32))
    o = _chunk_gated_delta_rule(q, k, v, g, beta)
    o = _rmsnorm(o, norm_g) * jax.nn.silu(z.astype(jnp.float32).reshape(bsz, seq, GDN_HEADS, GDN_DV))
    return o.reshape(bsz, seq, GDN_VW).astype(z.dtype)


def _mla(cq, ckv, k_rope, q_norm_g, w_q_up, kv_norm_g, w_kv_up):
    bsz, seq, _ = cq.shape
    q = (_rmsnorm(cq, q_norm_g) @ w_q_up).reshape(bsz, seq, MLA_HEADS, MLA_NOPE + MLA_ROPE)
    q_nope, q_rope = q[..., :MLA_NOPE], q[..., MLA_NOPE:]
    kv = (_rmsnorm(ckv, kv_norm_g) @ w_kv_up).reshape(bsz, seq, MLA_HEADS, MLA_NOPE + MLA_V)
    k_nope, v = kv[..., :MLA_NOPE], kv[..., MLA_NOPE:]
    cos, sin = _rope_tables(seq)
    q_rope = _apply_rope(q_rope, cos[:, None, :], sin[:, None, :])
    k_rope = _apply_rope(k_rope, cos, sin)
    nqb = seq // Q_BLOCK

    def qblocks(t):
        return jnp.moveaxis(t.reshape(bsz, nqb, Q_BLOCK, MLA_HEADS, t.shape[-1]), 1, 0)

    key_chunk = jnp.arange(seq) // CHUNK
    scale = (MLA_NOPE + MLA_ROPE) ** -0.5

    def attend(xs):
        qn, qr, blk = xs
        s = jnp.einsum("bqhd,bkhd->bhqk", qn, k_nope) + jnp.einsum("bqhd,bkd->bhqk", qr, k_rope)
        q_chunk = (blk * Q_BLOCK + jnp.arange(Q_BLOCK)) // CHUNK
        allowed = key_chunk[None, :] <= q_chunk[:, None]
        s = jnp.where(allowed, s.astype(jnp.float32) * scale, -jnp.inf)
        pr = jax.nn.softmax(s, axis=-1).astype(v.dtype)
        return jnp.einsum("bhqk,bkhd->bqhd", pr, v)

    o = lax.map(attend, (qblocks(q_nope), qblocks(q_rope), jnp.arange(nqb)))
    return jnp.moveaxis(o, 0, 1).reshape(bsz, seq, MLA_HEADS * MLA_V)


def _conv_ffn(h, w_up, conv_w, conv_b, w_down):
    u = _causal_dwconv(h @ w_up, conv_w) + conv_b
    gate, up = jnp.split(u, 2, axis=-1)
    return (jax.nn.silu(gate) * up) @ w_down


def _fwd_setup_inputs(seed: int = 0) -> dict:
    key = jax.random.key(seed)
    ks = jax.random.split(key, 24)
    f32 = jnp.float32
    L = DEPTH

    def nrm(k, shape, scale):
        return jax.random.normal(k, shape, f32) * scale

    dt = jnp.exp(jax.random.uniform(ks[5], (L, GDN_HEADS), f32, math.log(1e-3), math.log(1e-1)))
    return {
        "x": nrm(ks[0], (BATCH, SEQ, D_MODEL), 1.0),
        "p": nrm(ks[1], (L, BATCH, SEQ, PLE_DIM), 1.0),
        "w_in": nrm(ks[2], (L, D_MODEL, D_IN), D_MODEL ** -0.5),
        "gdn_conv_w": nrm(ks[3], (L, GDN_CONV, 2 * GDN_QK + GDN_VW), GDN_CONV ** -0.5),
        "gdn_a_log": jnp.log(jax.random.uniform(ks[4], (L, GDN_HEADS), f32, 1.0, 16.0)),
        "gdn_dt_bias": dt + jnp.log(-jnp.expm1(-dt)),
        "gdn_norm_g": 1.0 + nrm(ks[6], (L, GDN_DV), 0.02),
        "mla_q_norm_g": 1.0 + nrm(ks[7], (L, MLA_Q_LORA), 0.02),
        "mla_w_q_up": nrm(ks[8], (L, MLA_Q_LORA, MLA_HEADS * (MLA_NOPE + MLA_ROPE)), MLA_Q_LORA ** -0.5),
        "mla_kv_norm_g": 1.0 + nrm(ks[9], (L, MLA_KV_LORA), 0.02),
        "mla_w_kv_up": nrm(ks[10], (L, MLA_KV_LORA, MLA_HEADS * (MLA_NOPE + MLA_V)), MLA_KV_LORA ** -0.5),
        "w_out": nrm(ks[11], (L, D_MIX, D_MODEL), BETA * D_MIX ** -0.5),
        "ln1_g": 1.0 + nrm(ks[12], (L, D_MODEL), 0.02),
        "ln1_b": nrm(ks[13], (L, D_MODEL), 0.02),
        "ffn_w_up": nrm(ks[14], (L, D_MODEL, 2 * D_FF), D_MODEL ** -0.5),
        "ffn_conv_w": nrm(ks[15], (L, FFN_CONV, 2 * D_FF), FFN_CONV ** -0.5),
        "ffn_conv_b": nrm(ks[16], (L, 2 * D_FF), 0.01),
        "ffn_w_down": nrm(ks[17], (L, D_FF, D_MODEL), BETA * D_FF ** -0.5),
        "ple_w_gate": nrm(ks[18], (L, D_MODEL, D_MODEL), D_MODEL ** -0.5),
        "ple_b_gate": nrm(ks[19], (L, D_MODEL), 0.01),
        "ple_w_proj": nrm(ks[20], (L, PLE_DIM, D_MODEL), BETA * PLE_DIM ** -0.5),
        "ln2_g": 1.0 + nrm(ks[21], (L, D_MODEL), 0.02),
        "ln2_b": nrm(ks[22], (L, D_MODEL), 0.02),
    }


def _fwd_reference(x, p, w_in, gdn_conv_w, gdn_a_log, gdn_dt_bias, gdn_norm_g, mla_q_norm_g, mla_w_q_up,
              mla_kv_norm_g, mla_w_kv_up, w_out, ln1_g, ln1_b, ffn_w_up, ffn_conv_w, ffn_conv_b,
              ffn_w_down, ple_w_gate, ple_b_gate, ple_w_proj, ln2_g, ln2_b):
    o_qkv = 2 * GDN_QK + GDN_VW
    o_z = o_qkv + GDN_VW
    o_a = o_z + GDN_HEADS
    o_b = o_a + GDN_HEADS
    o_cq = o_b + MLA_Q_LORA
    o_ckv = o_cq + MLA_KV_LORA
    h = x
    for i in range(DEPTH):
        proj = h @ w_in[i]
        qkv, z, a, b, cq, ckv, k_rope = jnp.split(proj, [o_qkv, o_z, o_a, o_b, o_cq, o_ckv], axis=-1)
        out_a = _gated_deltanet(qkv, z, a, b, gdn_conv_w[i], gdn_a_log[i], gdn_dt_bias[i], gdn_norm_g[i])
        out_b = _mla(cq, ckv, k_rope, mla_q_norm_g[i], mla_w_q_up[i], mla_kv_norm_g[i], mla_w_kv_up[i])
        mix = jnp.concatenate([out_a, out_b], axis=-1) @ w_out[i]
        h = _layernorm(ALPHA * h + mix, ln1_g[i], ln1_b[i])
        ffn = _conv_ffn(h, ffn_w_up[i], ffn_conv_w[i], ffn_conv_b[i], ffn_w_down[i])
        ple = jax.nn.sigmoid(h @ ple_w_gate[i] + ple_b_gate[i]) * (p[i] @ ple_w_proj[i])
        h = _layernorm(ALPHA * h + ffn + ple, ln2_g[i], ln2_b[i])
    return h


import jax as _jax
import jax.numpy as _jnp

TWIN_FORMAT = 'train_step'
FWD_PARAMS = ['x', 'p', 'w_in', 'gdn_conv_w', 'gdn_a_log', 'gdn_dt_bias', 'gdn_norm_g', 'mla_q_norm_g', 'mla_w_q_up', 'mla_kv_norm_g', 'mla_w_kv_up', 'w_out', 'ln1_g', 'ln1_b', 'ffn_w_up', 'ffn_conv_w', 'ffn_conv_b', 'ffn_w_down', 'ple_w_gate', 'ple_b_gate', 'ple_w_proj', 'ln2_g', 'ln2_b']
TWIN_WEIGHTS = ['w_in', 'gdn_conv_w', 'gdn_a_log', 'gdn_dt_bias', 'gdn_norm_g', 'mla_q_norm_g', 'mla_w_q_up', 'mla_kv_norm_g', 'mla_w_kv_up', 'w_out', 'ln1_g', 'ln1_b', 'ffn_w_up', 'ffn_conv_w', 'ffn_conv_b', 'ffn_w_down', 'ple_w_gate', 'ple_b_gate', 'ple_w_proj', 'ln2_g', 'ln2_b']
TWIN_DIFF_INPUT = 'x'
TWIN_INPUTS = ['x', 'p', 'w_in', 'gdn_conv_w', 'gdn_a_log', 'gdn_dt_bias', 'gdn_norm_g', 'mla_q_norm_g', 'mla_w_q_up', 'mla_kv_norm_g', 'mla_w_kv_up', 'w_out', 'ln1_g', 'ln1_b', 'ffn_w_up', 'ffn_conv_w', 'ffn_conv_b', 'ffn_w_down', 'ple_w_gate', 'ple_b_gate', 'ple_w_proj', 'ln2_g', 'ln2_b', 'loss_target', 'm_w_in', 'm_gdn_conv_w', 'm_gdn_a_log', 'm_gdn_dt_bias', 'm_gdn_norm_g', 'm_mla_q_norm_g', 'm_mla_w_q_up', 'm_mla_kv_norm_g', 'm_mla_w_kv_up', 'm_w_out', 'm_ln1_g', 'm_ln1_b', 'm_ffn_w_up', 'm_ffn_conv_w', 'm_ffn_conv_b', 'm_ffn_w_down', 'm_ple_w_gate', 'm_ple_b_gate', 'm_ple_w_proj', 'm_ln2_g', 'm_ln2_b', 'v_w_in', 'v_gdn_conv_w', 'v_gdn_a_log', 'v_gdn_dt_bias', 'v_gdn_norm_g', 'v_mla_q_norm_g', 'v_mla_w_q_up', 'v_mla_kv_norm_g', 'v_mla_w_kv_up', 'v_w_out', 'v_ln1_g', 'v_ln1_b', 'v_ffn_w_up', 'v_ffn_conv_w', 'v_ffn_conv_b', 'v_ffn_w_down', 'v_ple_w_gate', 'v_ple_b_gate', 'v_ple_w_proj', 'v_ln2_g', 'v_ln2_b']
TWIN_OUTPUTS = ['loss', 'grad_x', 'grad_w_in', 'grad_gdn_conv_w', 'grad_gdn_a_log', 'grad_gdn_dt_bias', 'grad_gdn_norm_g', 'grad_mla_q_norm_g', 'grad_mla_w_q_up', 'grad_mla_kv_norm_g', 'grad_mla_w_kv_up', 'grad_w_out', 'grad_ln1_g', 'grad_ln1_b', 'grad_ffn_w_up', 'grad_ffn_conv_w', 'grad_ffn_conv_b', 'grad_ffn_w_down', 'grad_ple_w_gate', 'grad_ple_b_gate', 'grad_ple_w_proj', 'grad_ln2_g', 'grad_ln2_b', 'delta_w_in', 'delta_gdn_conv_w', 'delta_gdn_a_log', 'delta_gdn_dt_bias', 'delta_gdn_norm_g', 'delta_mla_q_norm_g', 'delta_mla_w_q_up', 'delta_mla_kv_norm_g', 'delta_mla_w_kv_up', 'delta_w_out', 'delta_ln1_g', 'delta_ln1_b', 'delta_ffn_w_up', 'delta_ffn_conv_w', 'delta_ffn_conv_b', 'delta_ffn_w_down', 'delta_ple_w_gate', 'delta_ple_b_gate', 'delta_ple_w_proj', 'delta_ln2_g', 'delta_ln2_b', 'new_m_w_in', 'new_m_gdn_conv_w', 'new_m_gdn_a_log', 'new_m_gdn_dt_bias', 'new_m_gdn_norm_g', 'new_m_mla_q_norm_g', 'new_m_mla_w_q_up', 'new_m_mla_kv_norm_g', 'new_m_mla_w_kv_up', 'new_m_w_out', 'new_m_ln1_g', 'new_m_ln1_b', 'new_m_ffn_w_up', 'new_m_ffn_conv_w', 'new_m_ffn_conv_b', 'new_m_ffn_w_down', 'new_m_ple_w_gate', 'new_m_ple_b_gate', 'new_m_ple_w_proj', 'new_m_ln2_g', 'new_m_ln2_b', 'new_v_w_in', 'new_v_gdn_conv_w', 'new_v_gdn_a_log', 'new_v_gdn_dt_bias', 'new_v_gdn_norm_g', 'new_v_mla_q_norm_g', 'new_v_mla_w_q_up', 'new_v_mla_kv_norm_g', 'new_v_mla_w_kv_up', 'new_v_w_out', 'new_v_ln1_g', 'new_v_ln1_b', 'new_v_ffn_w_up', 'new_v_ffn_conv_w', 'new_v_ffn_conv_b', 'new_v_ffn_w_down', 'new_v_ple_w_gate', 'new_v_ple_b_gate', 'new_v_ple_w_proj', 'new_v_ln2_g', 'new_v_ln2_b']
TWIN_LEAF_KINDS = {'loss': 'loss', 'grad_x': 'grad_x', 'grad_w_in': 'grad_w', 'grad_gdn_conv_w': 'grad_w', 'grad_gdn_a_log': 'grad_w', 'grad_gdn_dt_bias': 'grad_w', 'grad_gdn_norm_g': 'grad_w', 'grad_mla_q_norm_g': 'grad_w', 'grad_mla_w_q_up': 'grad_w', 'grad_mla_kv_norm_g': 'grad_w', 'grad_mla_w_kv_up': 'grad_w', 'grad_w_out': 'grad_w', 'grad_ln1_g': 'grad_w', 'grad_ln1_b': 'grad_w', 'grad_ffn_w_up': 'grad_w', 'grad_ffn_conv_w': 'grad_w', 'grad_ffn_conv_b': 'grad_w', 'grad_ffn_w_down': 'grad_w', 'grad_ple_w_gate': 'grad_w', 'grad_ple_b_gate': 'grad_w', 'grad_ple_w_proj': 'grad_w', 'grad_ln2_g': 'grad_w', 'grad_ln2_b': 'grad_w', 'delta_w_in': 'delta_w', 'delta_gdn_conv_w': 'delta_w', 'delta_gdn_a_log': 'delta_w', 'delta_gdn_dt_bias': 'delta_w', 'delta_gdn_norm_g': 'delta_w', 'delta_mla_q_norm_g': 'delta_w', 'delta_mla_w_q_up': 'delta_w', 'delta_mla_kv_norm_g': 'delta_w', 'delta_mla_w_kv_up': 'delta_w', 'delta_w_out': 'delta_w', 'delta_ln1_g': 'delta_w', 'delta_ln1_b': 'delta_w', 'delta_ffn_w_up': 'delta_w', 'delta_ffn_conv_w': 'delta_w', 'delta_ffn_conv_b': 'delta_w', 'delta_ffn_w_down': 'delta_w', 'delta_ple_w_gate': 'delta_w', 'delta_ple_b_gate': 'delta_w', 'delta_ple_w_proj': 'delta_w', 'delta_ln2_g': 'delta_w', 'delta_ln2_b': 'delta_w', 'new_m_w_in': 'new_m', 'new_m_gdn_conv_w': 'new_m', 'new_m_gdn_a_log': 'new_m', 'new_m_gdn_dt_bias': 'new_m', 'new_m_gdn_norm_g': 'new_m', 'new_m_mla_q_norm_g': 'new_m', 'new_m_mla_w_q_up': 'new_m', 'new_m_mla_kv_norm_g': 'new_m', 'new_m_mla_w_kv_up': 'new_m', 'new_m_w_out': 'new_m', 'new_m_ln1_g': 'new_m', 'new_m_ln1_b': 'new_m', 'new_m_ffn_w_up': 'new_m', 'new_m_ffn_conv_w': 'new_m', 'new_m_ffn_conv_b': 'new_m', 'new_m_ffn_w_down': 'new_m', 'new_m_ple_w_gate': 'new_m', 'new_m_ple_b_gate': 'new_m', 'new_m_ple_w_proj': 'new_m', 'new_m_ln2_g': 'new_m', 'new_m_ln2_b': 'new_m', 'new_v_w_in': 'new_v', 'new_v_gdn_conv_w': 'new_v', 'new_v_gdn_a_log': 'new_v', 'new_v_gdn_dt_bias': 'new_v', 'new_v_gdn_norm_g': 'new_v', 'new_v_mla_q_norm_g': 'new_v', 'new_v_mla_w_q_up': 'new_v', 'new_v_mla_kv_norm_g': 'new_v', 'new_v_mla_w_kv_up': 'new_v', 'new_v_w_out': 'new_v', 'new_v_ln1_g': 'new_v', 'new_v_ln1_b': 'new_v', 'new_v_ffn_w_up': 'new_v', 'new_v_ffn_conv_w': 'new_v', 'new_v_ffn_conv_b': 'new_v', 'new_v_ffn_w_down': 'new_v', 'new_v_ple_w_gate': 'new_v', 'new_v_ple_b_gate': 'new_v', 'new_v_ple_w_proj': 'new_v', 'new_v_ln2_g': 'new_v', 'new_v_ln2_b': 'new_v'}


def _forward(args):
    return _fwd_reference(*[args[k] for k in FWD_PARAMS])


def _output_shape():
    out = _jax.eval_shape(lambda: _forward(_fwd_setup_inputs(0)))
    return out.shape, out.dtype

N_MICROBATCH = 1
ADAM_LR = 0.001
ADAM_B1 = 0.9
ADAM_B2 = 0.999
ADAM_EPS = 1e-08
ADAM_WD = 0.01
ADAM_STEP = 10
PER_EXAMPLE_BATCH_AXIS = {'x': 0, 'p': 1, 'loss_target': 0}
SHARED_INPUTS = []
_WEIGHT_DTYPES = {'w_in': _jnp.float32, 'gdn_conv_w': _jnp.float32, 'gdn_a_log': _jnp.float32, 'gdn_dt_bias': _jnp.float32, 'gdn_norm_g': _jnp.float32, 'mla_q_norm_g': _jnp.float32, 'mla_w_q_up': _jnp.float32, 'mla_kv_norm_g': _jnp.float32, 'mla_w_kv_up': _jnp.float32, 'w_out': _jnp.float32, 'ln1_g': _jnp.float32, 'ln1_b': _jnp.float32, 'ffn_w_up': _jnp.float32, 'ffn_conv_w': _jnp.float32, 'ffn_conv_b': _jnp.float32, 'ffn_w_down': _jnp.float32, 'ple_w_gate': _jnp.float32, 'ple_b_gate': _jnp.float32, 'ple_w_proj': _jnp.float32, 'ln2_g': _jnp.float32, 'ln2_b': _jnp.float32}
MOMENT_SCALE = {'w_in': 5.427040e-02, 'gdn_conv_w': 5.659225e-02, 'gdn_a_log': 7.399032e-01, 'gdn_dt_bias': 7.009002e-01, 'gdn_norm_g': 1.974827e-01, 'mla_q_norm_g': 2.161232e-02, 'mla_w_q_up': 1.505088e-02, 'mla_kv_norm_g': 3.593010e-02, 'mla_w_kv_up': 1.750497e-02, 'w_out': 8.934116e-02, 'ln1_g': 1.634380e+00, 'ln1_b': 6.716073e-01, 'ffn_w_up': 4.411378e-02, 'ffn_conv_w': 4.347174e-02, 'ffn_conv_b': 5.196954e-02, 'ffn_w_down': 1.210841e-01, 'ple_w_gate': 2.445376e-02, 'ple_b_gate': 3.760530e-02, 'ple_w_proj': 1.057349e-01, 'ln2_g': 6.401584e+01, 'ln2_b': 1.981352e+00}


def _to_microbatches(a, axis):
    t = _jnp.moveaxis(a, axis, 0)
    t = t.reshape((N_MICROBATCH, t.shape[0] // N_MICROBATCH) + t.shape[1:])
    return _jnp.moveaxis(t, 1, axis + 1)


def setup_inputs(seed: int = 0) -> dict:
    inp = _fwd_setup_inputs(seed)
    key = _jax.random.fold_in(_jax.random.key(seed), 7919)
    shape, _ = _output_shape()
    out = dict(inp)
    out["loss_target"] = _jax.random.normal(_jax.random.fold_in(key, 0), shape, _jnp.float32)
    for i, name in enumerate(TWIN_WEIGHTS):
        w = inp[name].astype(_jnp.float32)
        if MOMENT_SCALE is None:
            s = _jnp.sqrt(_jnp.mean(_jnp.square(w)) + 1e-30)
        else:
            s = MOMENT_SCALE[name]
        km, kv = _jax.random.split(_jax.random.fold_in(key, i + 1))
        out[name] = w
        out["m_" + name] = s * _jax.random.normal(km, w.shape, _jnp.float32)
        out["v_" + name] = (s * s) * _jax.random.uniform(kv, w.shape, _jnp.float32, 0.5, 1.5)
    if N_MICROBATCH > 1:
        for name, axis in PER_EXAMPLE_BATCH_AXIS.items():
            out[name] = _to_microbatches(out[name], axis)
    return {'x': out['x'], 'p': out['p'], 'w_in': out['w_in'], 'gdn_conv_w': out['gdn_conv_w'], 'gdn_a_log': out['gdn_a_log'], 'gdn_dt_bias': out['gdn_dt_bias'], 'gdn_norm_g': out['gdn_norm_g'], 'mla_q_norm_g': out['mla_q_norm_g'], 'mla_w_q_up': out['mla_w_q_up'], 'mla_kv_norm_g': out['mla_kv_norm_g'], 'mla_w_kv_up': out['mla_w_kv_up'], 'w_out': out['w_out'], 'ln1_g': out['ln1_g'], 'ln1_b': out['ln1_b'], 'ffn_w_up': out['ffn_w_up'], 'ffn_conv_w': out['ffn_conv_w'], 'ffn_conv_b': out['ffn_conv_b'], 'ffn_w_down': out['ffn_w_down'], 'ple_w_gate': out['ple_w_gate'], 'ple_b_gate': out['ple_b_gate'], 'ple_w_proj': out['ple_w_proj'], 'ln2_g': out['ln2_g'], 'ln2_b': out['ln2_b'], 'loss_target': out['loss_target'], 'm_w_in': out['m_w_in'], 'm_gdn_conv_w': out['m_gdn_conv_w'], 'm_gdn_a_log': out['m_gdn_a_log'], 'm_gdn_dt_bias': out['m_gdn_dt_bias'], 'm_gdn_norm_g': out['m_gdn_norm_g'], 'm_mla_q_norm_g': out['m_mla_q_norm_g'], 'm_mla_w_q_up': out['m_mla_w_q_up'], 'm_mla_kv_norm_g': out['m_mla_kv_norm_g'], 'm_mla_w_kv_up': out['m_mla_w_kv_up'], 'm_w_out': out['m_w_out'], 'm_ln1_g': out['m_ln1_g'], 'm_ln1_b': out['m_ln1_b'], 'm_ffn_w_up': out['m_ffn_w_up'], 'm_ffn_conv_w': out['m_ffn_conv_w'], 'm_ffn_conv_b': out['m_ffn_conv_b'], 'm_ffn_w_down': out['m_ffn_w_down'], 'm_ple_w_gate': out['m_ple_w_gate'], 'm_ple_b_gate': out['m_ple_b_gate'], 'm_ple_w_proj': out['m_ple_w_proj'], 'm_ln2_g': out['m_ln2_g'], 'm_ln2_b': out['m_ln2_b'], 'v_w_in': out['v_w_in'], 'v_gdn_conv_w': out['v_gdn_conv_w'], 'v_gdn_a_log': out['v_gdn_a_log'], 'v_gdn_dt_bias': out['v_gdn_dt_bias'], 'v_gdn_norm_g': out['v_gdn_norm_g'], 'v_mla_q_norm_g': out['v_mla_q_norm_g'], 'v_mla_w_q_up': out['v_mla_w_q_up'], 'v_mla_kv_norm_g': out['v_mla_kv_norm_g'], 'v_mla_w_kv_up': out['v_mla_w_kv_up'], 'v_w_out': out['v_w_out'], 'v_ln1_g': out['v_ln1_g'], 'v_ln1_b': out['v_ln1_b'], 'v_ffn_w_up': out['v_ffn_w_up'], 'v_ffn_conv_w': out['v_ffn_conv_w'], 'v_ffn_conv_b': out['v_ffn_conv_b'], 'v_ffn_w_down': out['v_ffn_w_down'], 'v_ple_w_gate': out['v_ple_w_gate'], 'v_ple_b_gate': out['v_ple_b_gate'], 'v_ple_w_proj': out['v_ple_w_proj'], 'v_ln2_g': out['v_ln2_g'], 'v_ln2_b': out['v_ln2_b']}


def _loss(weights, diff, rest, loss_target):
    with _jax.named_scope("forward"):
        args = {**rest, TWIN_DIFF_INPUT: diff, **{k: w.astype(_WEIGHT_DTYPES[k]) for k, w in weights.items()}}
        y = _forward(args)
    with _jax.named_scope("loss_head"):
        err = _jnp.square(y.astype(_jnp.float32) - loss_target)
        return 0.5 * _jnp.sum(_jnp.mean(err, axis=-1)) if err.ndim else 0.5 * err


def _adamw(w, g, m, v):
    m = ADAM_B1 * m + (1.0 - ADAM_B1) * g
    v = ADAM_B2 * v + (1.0 - ADAM_B2) * _jnp.square(g)
    m_hat = m / (1.0 - ADAM_B1 ** ADAM_STEP)
    v_hat = v / (1.0 - ADAM_B2 ** ADAM_STEP)
    delta = -ADAM_LR * (m_hat / (_jnp.sqrt(v_hat) + ADAM_EPS) + ADAM_WD * w)
    return delta, m, v


def reference(x, p, w_in, gdn_conv_w, gdn_a_log, gdn_dt_bias, gdn_norm_g, mla_q_norm_g, mla_w_q_up, mla_kv_norm_g, mla_w_kv_up, w_out, ln1_g, ln1_b, ffn_w_up, ffn_conv_w, ffn_conv_b, ffn_w_down, ple_w_gate, ple_b_gate, ple_w_proj, ln2_g, ln2_b, loss_target, m_w_in, m_gdn_conv_w, m_gdn_a_log, m_gdn_dt_bias, m_gdn_norm_g, m_mla_q_norm_g, m_mla_w_q_up, m_mla_kv_norm_g, m_mla_w_kv_up, m_w_out, m_ln1_g, m_ln1_b, m_ffn_w_up, m_ffn_conv_w, m_ffn_conv_b, m_ffn_w_down, m_ple_w_gate, m_ple_b_gate, m_ple_w_proj, m_ln2_g, m_ln2_b, v_w_in, v_gdn_conv_w, v_gdn_a_log, v_gdn_dt_bias, v_gdn_norm_g, v_mla_q_norm_g, v_mla_w_q_up, v_mla_kv_norm_g, v_mla_w_kv_up, v_w_out, v_ln1_g, v_ln1_b, v_ffn_w_up, v_ffn_conv_w, v_ffn_conv_b, v_ffn_w_down, v_ple_w_gate, v_ple_b_gate, v_ple_w_proj, v_ln2_g, v_ln2_b):
    given = dict(x=x, p=p, w_in=w_in, gdn_conv_w=gdn_conv_w, gdn_a_log=gdn_a_log, gdn_dt_bias=gdn_dt_bias, gdn_norm_g=gdn_norm_g, mla_q_norm_g=mla_q_norm_g, mla_w_q_up=mla_w_q_up, mla_kv_norm_g=mla_kv_norm_g, mla_w_kv_up=mla_w_kv_up, w_out=w_out, ln1_g=ln1_g, ln1_b=ln1_b, ffn_w_up=ffn_w_up, ffn_conv_w=ffn_conv_w, ffn_conv_b=ffn_conv_b, ffn_w_down=ffn_w_down, ple_w_gate=ple_w_gate, ple_b_gate=ple_b_gate, ple_w_proj=ple_w_proj, ln2_g=ln2_g, ln2_b=ln2_b, loss_target=loss_target, m_w_in=m_w_in, m_gdn_conv_w=m_gdn_conv_w, m_gdn_a_log=m_gdn_a_log, m_gdn_dt_bias=m_gdn_dt_bias, m_gdn_norm_g=m_gdn_norm_g, m_mla_q_norm_g=m_mla_q_norm_g, m_mla_w_q_up=m_mla_w_q_up, m_mla_kv_norm_g=m_mla_kv_norm_g, m_mla_w_kv_up=m_mla_w_kv_up, m_w_out=m_w_out, m_ln1_g=m_ln1_g, m_ln1_b=m_ln1_b, m_ffn_w_up=m_ffn_w_up, m_ffn_conv_w=m_ffn_conv_w, m_ffn_conv_b=m_ffn_conv_b, m_ffn_w_down=m_ffn_w_down, m_ple_w_gate=m_ple_w_gate, m_ple_b_gate=m_ple_b_gate, m_ple_w_proj=m_ple_w_proj, m_ln2_g=m_ln2_g, m_ln2_b=m_ln2_b, v_w_in=v_w_in, v_gdn_conv_w=v_gdn_conv_w, v_gdn_a_log=v_gdn_a_log, v_gdn_dt_bias=v_gdn_dt_bias, v_gdn_norm_g=v_gdn_norm_g, v_mla_q_norm_g=v_mla_q_norm_g, v_mla_w_q_up=v_mla_w_q_up, v_mla_kv_norm_g=v_mla_kv_norm_g, v_mla_w_kv_up=v_mla_w_kv_up, v_w_out=v_w_out, v_ln1_g=v_ln1_g, v_ln1_b=v_ln1_b, v_ffn_w_up=v_ffn_w_up, v_ffn_conv_w=v_ffn_conv_w, v_ffn_conv_b=v_ffn_conv_b, v_ffn_w_down=v_ffn_w_down, v_ple_w_gate=v_ple_w_gate, v_ple_b_gate=v_ple_b_gate, v_ple_w_proj=v_ple_w_proj, v_ln2_g=v_ln2_g, v_ln2_b=v_ln2_b)
    weights = {n: given[n] for n in TWIN_WEIGHTS}
    shared = {n: given[n] for n in SHARED_INPUTS}
    per_example = {n: given[n] for n in ['x', 'p']}
    grad_fn = _jax.value_and_grad(_loss, argnums=(0, 1))

    def one_microbatch(ex, loss_target):
        ex = dict(ex)
        diff = ex.pop(TWIN_DIFF_INPUT)
        return grad_fn(weights, diff, {**shared, **ex}, loss_target)

    if N_MICROBATCH == 1:
        loss, (grad_w, grad_x) = one_microbatch(per_example, given["loss_target"])
    else:
        def body(carry, xs):
            loss_sum, grad_sum = carry
            l_k, (gw_k, gx_k) = one_microbatch(xs[0], xs[1])
            with _jax.named_scope("update"):
                return (loss_sum + l_k, _jax.tree.map(_jnp.add, grad_sum, gw_k)), gx_k

        init = (_jnp.zeros((), _jnp.float32), _jax.tree.map(_jnp.zeros_like, weights))
        (loss, grad_w), grad_x = _jax.lax.scan(body, init, (per_example, given["loss_target"]))
    with _jax.named_scope("update"):
        delta_w, new_m, new_v = {}, {}, {}
        for n in TWIN_WEIGHTS:
            delta_w[n], new_m[n], new_v[n] = _adamw(weights[n], grad_w[n], given["m_" + n], given["v_" + n])
    return (loss, grad_x, *[grad_w[n] for n in TWIN_WEIGHTS], *[delta_w[n] for n in TWIN_WEIGHTS],
            *[new_m[n] for n in TWIN_WEIGHTS], *[new_v[n] for n in TWIN_WEIGHTS])
```

```python
import functools
import math

import jax
import jax.numpy as jnp
from jax import lax
from jax.experimental import pallas as pl
from jax.experimental.pallas import tpu as pltpu

F32 = jnp.float32
BF = jnp.bfloat16
_MXU = jnp.bfloat16
_INTERPRET = None
_VMEM_LIMIT = 56 * 1024 * 1024
HI = lax.Precision.HIGHEST

N_DEV = 8
CHUNK = 64
GDN_HEADS = 4
GDN_DK = 128
MLA_HEADS = 4
MLA_NOPE = 128
MLA_ROPE = 64
ROPE_THETA = 10000.0
ALPHA = 2.0 ** 0.25
NORM_EPS = 1e-6
ADAM_LR, ADAM_B1, ADAM_B2, ADAM_EPS, ADAM_WD, ADAM_STEP = 0.001, 0.9, 0.999, 1e-08, 0.01, 10
MESH = pl.DeviceIdType.MESH
ANY = pl.BlockSpec(memory_space=pl.ANY)


def _pc(body, *, name, out_shape, grid=None, in_specs=None, out_specs=None, scratch=(), sem=None, aliases=None):
    kw = {}
    if _INTERPRET is not None:
        kw["interpret"] = _INTERPRET
    if grid is not None:
        kw["grid"] = grid
    if in_specs is not None:
        kw["in_specs"] = in_specs
    if out_specs is not None:
        kw["out_specs"] = out_specs
    cp = dict(vmem_limit_bytes=_VMEM_LIMIT)
    if sem is not None:
        cp["dimension_semantics"] = sem
    return pl.pallas_call(body, name=name, out_shape=out_shape, scratch_shapes=list(scratch),
                          input_output_aliases=aliases or {}, compiler_params=pltpu.CompilerParams(**cp), **kw)


def _sds(shape, dtype=F32):
    return jax.ShapeDtypeStruct(tuple(shape), dtype)


def _pick(n, cands):
    for c in cands:
        if n % c == 0:
            return c
    return n


def _exchange(arrs, modes, name):
    n = len(arrs)
    outs = []
    for a, m in zip(arrs, modes):
        outs.append(_sds((N_DEV,) + a.shape if m == "ag" else a.shape, a.dtype))

    def body(*refs):
        ins, os_ = refs[:n], refs[n:2 * n]
        send, recv, loc = refs[2 * n:]
        x, y, c = lax.axis_index("x"), lax.axis_index("y"), lax.axis_index("c")
        me = 4 * x + 2 * y + c
        copies = []
        for ai in range(n):
            ag = modes[ai] == "ag"
            own = pltpu.make_async_copy(ins[ai] if ag else ins[ai].at[me], os_[ai].at[me], loc.at[ai])
            own.start()
            copies.append(own)
            for r in range(1, N_DEV):
                px = 1 - x if (r >> 2) & 1 else x
                py = 1 - y if (r >> 1) & 1 else y
                pc = 1 - c if r & 1 else c
                peer = 4 * px + 2 * py + pc
                cp = pltpu.make_async_remote_copy(
                    src_ref=ins[ai] if ag else ins[ai].at[peer], dst_ref=os_[ai].at[me],
                    send_sem=send.at[ai, r - 1], recv_sem=recv.at[ai, r - 1],
                    device_id=(px, py, pc), device_id_type=MESH)
                cp.start()
                copies.append(cp)
        for cp in copies:
            cp.wait()

    return _pc(body, name=name, out_shape=tuple(outs), in_specs=[ANY] * n, out_specs=tuple([ANY] * n),
               scratch=[pltpu.SemaphoreType.DMA((n, N_DEV - 1)), pltpu.SemaphoreType.DMA((n, N_DEV - 1)),
                        pltpu.SemaphoreType.DMA((n,))])(*arrs)


def _mm(a, b, mode, *, name, add=None, out_dtype=F32):
    if mode == "nn":
        (M, K), (K2, N) = a.shape, b.shape
    elif mode == "nt":
        (M, K), (N, K2) = a.shape, b.shape
    else:
        (K, M), (K2, N) = a.shape, b.shape
    assert K == K2, (a.shape, b.shape, mode)
    tm = _pick(M, (512, 256, 128))
    tn = _pick(N, (512, 256, 128))
    tk = K if K <= 1536 else _pick(K, (1024, 1408, 512, 256, 128))
    nk = K // tk
    if mode == "nn":
        a_spec = pl.BlockSpec((tm, tk), lambda i, j, k: (i, k))
        b_spec = pl.BlockSpec((tk, tn), lambda i, j, k: (k, j))
        dims = (((1,), (0,)), ((), ()))
    elif mode == "nt":
        a_spec = pl.BlockSpec((tm, tk), lambda i, j, k: (i, k))
        b_spec = pl.BlockSpec((tn, tk), lambda i, j, k: (j, k))
        dims = (((1,), (1,)), ((), ()))
    else:
        a_spec = pl.BlockSpec((tk, tm), lambda i, j, k: (k, i))
        b_spec = pl.BlockSpec((tk, tn), lambda i, j, k: (k, j))
        dims = (((0,), (0,)), ((), ()))
    o_spec = pl.BlockSpec((tm, tn), lambda i, j, k: (i, j))
    has_add = add is not None

    def body(a_ref, b_ref, *rest):
        if has_add:
            add_ref, o_ref, acc = rest
        else:
            o_ref, acc = rest
        k = pl.program_id(2)

        @pl.when(k == 0)
        def _():
            acc[...] = jnp.zeros_like(acc)

        acc[...] += lax.dot_general(a_ref[...].astype(_MXU), b_ref[...].astype(_MXU), dims,
                                    preferred_element_type=F32)

        @pl.when(k == nk - 1)
        def _():
            r = acc[...]
            if has_add:
                r = r + add_ref[...].astype(F32)
            o_ref[...] = r.astype(out_dtype)

    ins = [a, b] + ([add] if has_add else [])
    specs = [a_spec, b_spec] + ([o_spec] if has_add else [])
    return _pc(body, name=name, out_shape=_sds((M, N), out_dtype), grid=(M // tm, N // tn, nk),
               in_specs=specs, out_specs=o_spec, scratch=[pltpu.VMEM((tm, tn), F32)],
               sem=("parallel", "parallel", "arbitrary"))(*ins)


def _dot(a, b, dims):
    return lax.dot_general(a.astype(_MXU), b.astype(_MXU), dims, preferred_element_type=F32)


_NN = (((1,), (0,)), ((), ()))
_NT = (((1,), (1,)), ((), ()))
_TN = (((0,), (0,)), ((), ()))


def _split(a):
    hi = a.astype(BF)
    lo = (a - hi.astype(F32)).astype(BF)
    return hi, lo


def _dot3(a, b, dims):
    ah, al = _split(a)
    bh, bl = _split(b)
    d = lambda u, v: lax.dot_general(u, v, dims, preferred_element_type=F32)
    return d(ah, bh) + (d(ah, bl) + d(al, bh))


def _softplus(x):
    return jnp.maximum(x, 0.0) + jnp.log1p(jnp.exp(-jnp.abs(x)))


def _silu(x):
    return x * jax.nn.sigmoid(x)


def _rmsnorm(x, g):
    return x * lax.rsqrt(jnp.mean(x * x, axis=-1, keepdims=True) + NORM_EPS) * g


def _layernorm(x, g, b):
    mu = jnp.mean(x, axis=-1, keepdims=True)
    xc = x - mu
    var = jnp.mean(xc * xc, axis=-1, keepdims=True)
    return xc * lax.rsqrt(var + NORM_EPS) * g + b


def _shift_dn(x, s):
    if s == 0:
        return x
    t = lax.broadcasted_iota(jnp.int32, x.shape, 0)
    return jnp.where(t >= s, pltpu.roll(x, s, 0), 0.0)


def _shift_up(x, s):
    if s == 0:
        return x
    n = x.shape[0]
    t = lax.broadcasted_iota(jnp.int32, x.shape, 0)
    return jnp.where(t < n - s, pltpu.roll(x, n - s, 0), 0.0)


def _conv(x, w):
    kk = w.shape[0]
    y = w[kk - 1:kk, :] * x
    for j in range(kk - 1):
        y = y + w[j:j + 1, :] * _shift_dn(x, kk - 1 - j)
    return y


def _conv_bwd_x(dy, w):
    kk = w.shape[0]
    dx = w[kk - 1:kk, :] * dy
    for j in range(kk - 1):
        dx = dx + w[j:j + 1, :] * _shift_up(dy, kk - 1 - j)
    return dx


def _conv_bwd_w(x, dy, dw_ref, first):
    kk = dw_ref.shape[0]
    for j in range(kk):
        r = jnp.sum(dy * _shift_dn(x, kk - 1 - j), axis=0, keepdims=True)
        prev = jnp.where(first, 0.0, dw_ref[j:j + 1, :])
        dw_ref[j:j + 1, :] = prev + r


def _gdn_post_conv(c, j):
    h = _silu(c)
    hn = h * lax.rsqrt(jnp.sum(h * h, axis=-1, keepdims=True) + NORM_EPS)
    return jnp.where(j < GDN_HEADS, hn * (GDN_DK ** -0.5), jnp.where(j < 2 * GDN_HEADS, hn, h))


def _gdn_pre_fwd(qkv, conv_w, Bl, S):
    T, C = qkv.shape
    nj = C // 128

    def body(x_ref, w_ref, o_ref):
        o_ref[...] = _gdn_post_conv(_conv(x_ref[...], w_ref[...]), pl.program_id(1))

    blk = pl.BlockSpec((S, 128), lambda b, j: (b, j))
    return _pc(body, name="gdn_pre_fwd", out_shape=_sds((T, C)), grid=(Bl, nj),
               in_specs=[blk, pl.BlockSpec((conv_w.shape[0], 128), lambda b, j: (0, j))], out_specs=blk,
               sem=("parallel", "parallel"))(qkv, conv_w)


def _gdn_pre_bwd(qkv, conv_w, dout, Bl, S):
    T, C = qkv.shape
    nj = C // 128
    kk = conv_w.shape[0]

    def body(x_ref, w_ref, d_ref, dx_ref, dw_ref):
        j, b = pl.program_id(0), pl.program_id(1)
        x, w = x_ref[...], w_ref[...]
        c = _conv(x, w)
        _, vjp = jax.vjp(lambda u: _gdn_post_conv(u, j), c)
        (dc,) = vjp(d_ref[...])
        dx_ref[...] = _conv_bwd_x(dc, w)
        _conv_bwd_w(x, dc, dw_ref, b == 0)

    blk = pl.BlockSpec((S, 128), lambda j, b: (b, j))
    wblk = pl.BlockSpec((kk, 128), lambda j, b: (0, j))
    return _pc(body, name="gdn_pre_bwd", out_shape=(_sds((T, C)), _sds((kk, C))), grid=(nj, Bl),
               in_specs=[blk, wblk, blk], out_specs=(blk, wblk), sem=("parallel", "arbitrary"))(qkv, conv_w, dout)


_H = GDN_HEADS
_HC = _H * CHUNK


def _st(x):
    return jnp.concatenate([x[:, h * 128:(h + 1) * 128] for h in range(_H)], axis=0)


def _unst(x):
    return jnp.concatenate([x[h * CHUNK:(h + 1) * CHUNK] for h in range(_H)], axis=1)


def _stc(t):
    return jnp.concatenate([t[:, h:h + 1] for h in range(_H)], axis=0)


def _untile(col):
    lane = lax.broadcasted_iota(jnp.int32, (CHUNK, 128), 1)
    out = jnp.zeros((CHUNK, 128), F32)
    for h in range(_H):
        out = out + jnp.where(lane == h, col[h * CHUNK:(h + 1) * CHUNK], 0.0)
    return out


def _rowform(col):
    return jnp.broadcast_to(col, (_HC, 128)).T[0:1, :]


def _tri(n, upper=False):
    i = lax.broadcasted_iota(jnp.int32, (n, n), 0)
    j = lax.broadcasted_iota(jnp.int32, (n, n), 1)
    return jnp.where((j >= i) if upper else (j <= i), 1.0, 0.0).astype(F32)


def _gdn_chunk_common(q, k, v, gc, beta):
    Q, K, V = _st(q), _st(k), _st(v)
    B, GC = _stc(beta), _stc(gc)
    GL = jnp.concatenate([jnp.broadcast_to(gc[CHUNK - 1:CHUNK, h:h + 1], (CHUNK, 1)) for h in range(_H)], axis=0)
    ii = lax.broadcasted_iota(jnp.int32, (_HC, _HC), 0)
    jj = lax.broadcasted_iota(jnp.int32, (_HC, _HC), 1)
    same = (ii >> 6) == (jj >> 6)
    incl = same & (ii >= jj)
    strict = same & (ii > jj)
    diff = GC - _rowform(GC)
    D = jnp.where(incl, jnp.exp(jnp.where(incl, diff, 0.0)), 0.0)
    KB = K * B
    A = jnp.where(strict, _dot(KB, K, _NT) * D, 0.0)
    P = jnp.where(incl, _dot(Q, K, _NT) * D, 0.0)
    EG = jnp.exp(GC)
    ED = jnp.exp(GL - GC)
    return dict(Q=Q, K=K, V=V, B=B, GC=GC, GL=GL, incl=incl, strict=strict, D=D, KB=KB, A=A, P=P, EG=EG, ED=ED,
                QG=Q * EG, KD=K * ED)


def _neumann_inv(A):
    n = A.shape[0]
    i = lax.broadcasted_iota(jnp.int32, (n, n), 0)
    j = lax.broadcasted_iota(jnp.int32, (n, n), 1)
    N = -A
    acc = jnp.where(i == j, 1.0, 0.0) + N
    Pw = N
    for _ in range(5):
        Pw = _dot3(Pw, Pw, _NN)
        acc = acc + _dot3(acc, Pw, _NN)
    return acc


def _gates(a, alog, dtb):
    return -jnp.exp(alog) * _softplus(a + dtb)


def _hs(x, h, n=CHUNK):
    return x[h * n:(h + 1) * n]


def _gdn_fwd(qkvn, sm, alog, dtb, Bl, S):
    T = Bl * S
    nC = S // CHUNK

    def body(q_ref, k_ref, v_ref, a_ref, b_ref, alog_ref, dtb_ref,
             o_ref, tinv_ref, sst_ref, w_ref, u_ref, gc_ref, beta_ref, S_ref):
        @pl.when(pl.program_id(1) == 0)
        def _():
            S_ref[...] = jnp.zeros_like(S_ref)

        g = _gates(a_ref[...], alog_ref[...], dtb_ref[...])
        beta = jax.nn.sigmoid(b_ref[...])
        gc = jnp.dot(_tri(CHUNK), g, precision=HI, preferred_element_type=F32)
        cm = _gdn_chunk_common(q_ref[...], k_ref[...], v_ref[...], gc, beta)
        tinv = _neumann_inv(cm["A"])
        W = _dot3(tinv, cm["KB"] * cm["EG"], _NN)
        U = _dot3(tinv, cm["V"] * cm["B"], _NN)
        s_old = [S_ref[h * 128:(h + 1) * 128, :] for h in range(_H)]
        vn = [_hs(U, h) - _dot(_hs(W, h), s_old[h], _NN) for h in range(_H)]
        o_intra = _dot(cm["P"], jnp.concatenate(vn, axis=0), _NN)
        outs = []
        for h in range(_H):
            outs.append(_dot(_hs(cm["QG"], h), s_old[h], _NN) + _hs(o_intra, h))
            gl = jnp.exp(gc[CHUNK - 1:CHUNK, h:h + 1])
            S_ref[h * 128:(h + 1) * 128, :] = gl * s_old[h] + _dot(_hs(cm["KD"], h), vn[h], _TN)
            sst_ref[h * 128:(h + 1) * 128, :] = s_old[h]
        o_ref[...] = jnp.concatenate(outs, axis=1)
        tinv_ref[...] = tinv
        w_ref[...] = _unst(W)
        u_ref[...] = _unst(U)
        gc_ref[...] = gc
        beta_ref[...] = beta

    row = lambda b, n: b * nC + n
    hd = lambda c: pl.BlockSpec((CHUNK, 512), lambda b, n: (row(b, n), c))
    tile = lambda c: pl.BlockSpec((CHUNK, 128), lambda b, n: (row(b, n), c))
    par = pl.BlockSpec((1, 128), lambda b, n: (0, 0))
    out_shape = (_sds((T, 512)), _sds((T // CHUNK * _HC, _HC)), _sds((T // CHUNK * 512, 128)),
                 _sds((T, 512)), _sds((T, 512)), _sds((T, 128)), _sds((T, 128)))
    out_specs = (hd(0), pl.BlockSpec((_HC, _HC), lambda b, n: (row(b, n), 0)),
                 pl.BlockSpec((512, 128), lambda b, n: (row(b, n), 0)), hd(0), hd(0), tile(0), tile(0))
    return _pc(body, name="gdn_chunk_fwd", out_shape=out_shape, grid=(Bl, nC),
               in_specs=[hd(0), hd(1), hd(2), tile(6), tile(7), par, par], out_specs=out_specs,
               scratch=[pltpu.VMEM((512, 128), F32)], sem=("parallel", "arbitrary"))(
        qkvn, qkvn, qkvn, sm, sm, alog, dtb)


def _gdn_bwd(qkvn, sm, alog, dtb, gc_s, beta_s, tinv_s, sst_s, w_s, u_s, do, Bl, S):
    T = Bl * S
    nC = S // CHUNK

    def body(q_ref, k_ref, v_ref, a_ref, alog_ref, dtb_ref, gc_ref, beta_ref, tinv_ref, sst_ref, w_ref, u_ref,
             do_ref, dq_ref, dk_ref, dv_ref, da_ref, db_ref, acc_ref, dS_ref):
        first_chunk = pl.program_id(1) == 0

        @pl.when(first_chunk)
        def _():
            dS_ref[...] = jnp.zeros_like(dS_ref)

        @pl.when(first_chunk & (pl.program_id(0) == 0))
        def _():
            acc_ref[...] = jnp.zeros_like(acc_ref)

        gc, beta = gc_ref[...], beta_ref[...]
        cm = _gdn_chunk_common(q_ref[...], k_ref[...], v_ref[...], gc, beta)
        Q, K, V, B, D, KB, A, P = (cm[n] for n in ("Q", "K", "V", "B", "D", "KB", "A", "P"))
        EG, ED, QG, KD = cm["EG"], cm["ED"], cm["QG"], cm["KD"]
        tinv = tinv_ref[...]
        W, U, DO = _st(w_ref[...]), _st(u_ref[...]), _st(do_ref[...])
        s_old = [sst_ref[h * 128:(h + 1) * 128, :] for h in range(_H)]
        ds_new = [dS_ref[h * 128:(h + 1) * 128, :] for h in range(_H)]
        VN = jnp.concatenate([_hs(U, h) - _dot(_hs(W, h), s_old[h], _NN) for h in range(_H)], axis=0)
        dP = jnp.where(cm["incl"], _dot(DO, VN, _NT), 0.0)
        dVN0 = _dot(P, DO, _TN)
        dVN, dQG, dKD, dW, TL = [], [], [], [], []
        for h in range(_H):
            gl = jnp.exp(gc[CHUNK - 1:CHUNK, h:h + 1])
            dvn = _hs(dVN0, h) + _dot(_hs(KD, h), ds_new[h], _NN)
            dkd = _dot(_hs(VN, h), ds_new[h], _NT)
            dVN.append(dvn)
            dQG.append(_dot(_hs(DO, h), s_old[h], _NT))
            dKD.append(dkd)
            dW.append(-_dot(dvn, s_old[h], _NT))
            dgl = jnp.sum(jnp.sum(ds_new[h] * s_old[h], axis=1, keepdims=True), axis=0, keepdims=True)
            tl = jnp.sum(jnp.sum(dkd * _hs(KD, h), axis=1, keepdims=True), axis=0, keepdims=True) + dgl * gl
            TL.append(jnp.broadcast_to(tl, (CHUNK, 1)))
            dS_ref[h * 128:(h + 1) * 128, :] = (_dot(_hs(QG, h), _hs(DO, h), _TN) + gl * ds_new[h]
                                                - _dot(_hs(W, h), dvn, _TN))
        dVN, dQG, dKD, dW, TL = (jnp.concatenate(z, axis=0) for z in (dVN, dQG, dKD, dW, TL))
        dVB = _dot3(tinv, dVN, _TN)
        dKBE = _dot3(tinv, dW, _TN)
        dA = jnp.where(cm["strict"], -(_dot3(dVB, U, _NT) + _dot3(dKBE, W, _NT)), 0.0)
        dG = dA * D
        dQK = dP * D
        dKB = _dot(dG, K, _NN) + dKBE * EG
        dK = _dot(dG, KB, _TN) + _dot(dQK, Q, _TN) + dKB * B + dKD * ED
        dQ = _dot(dQK, K, _NN) + dQG * EG
        Mx = dA * A + dP * P
        rs = lambda z: jnp.sum(z, axis=1, keepdims=True)
        ri = lax.broadcasted_iota(jnp.int32, (_HC, 1), 0)
        dGC = (rs(Mx) - rs(Mx.T) + rs(dKBE * KB * EG) + rs(dQG * QG) - rs(dKD * KD)
               + jnp.where((ri & (CHUNK - 1)) == CHUNK - 1, TL, 0.0))
        dBeta = rs(dVB * V) + rs(dKB * K)
        dq_ref[...] = _unst(dQ)
        dk_ref[...] = _unst(dK)
        dv_ref[...] = _unst(dVB * B)
        dg = jnp.dot(_tri(CHUNK, upper=True), _untile(dGC), precision=HI, preferred_element_type=F32)
        a, alog_v, dtb_v = a_ref[...], alog_ref[...], dtb_ref[...]
        g = _gates(a, alog_v, dtb_v)
        lane = lax.broadcasted_iota(jnp.int32, (CHUNK, 128), 1)
        valid = lane < _H
        da = jnp.where(valid, dg * (-jnp.exp(alog_v)) * jax.nn.sigmoid(a + dtb_v), 0.0)
        da_ref[...] = da
        db_ref[...] = jnp.where(valid, _untile(dBeta) * beta * (1.0 - beta), 0.0)
        acc_ref[0:1, :] += jnp.sum(jnp.where(valid, dg * g, 0.0), axis=0, keepdims=True)
        acc_ref[1:2, :] += jnp.sum(da, axis=0, keepdims=True)

    row = lambda b, n: b * nC + (nC - 1 - n)
    hd = lambda c: pl.BlockSpec((CHUNK, 512), lambda b, n: (row(b, n), c))
    tile = lambda c: pl.BlockSpec((CHUNK, 128), lambda b, n: (row(b, n), c))
    par = pl.BlockSpec((1, 128), lambda b, n: (0, 0))
    in_specs = [hd(0), hd(1), hd(2), tile(6), par, par, tile(0), tile(0),
                pl.BlockSpec((_HC, _HC), lambda b, n: (row(b, n), 0)),
                pl.BlockSpec((512, 128), lambda b, n: (row(b, n), 0)), hd(0), hd(0), hd(0)]
    out_shape = (_sds((T, 512)), _sds((T, 512)), _sds((T, 512)), _sds((T, 128)), _sds((T, 128)), _sds((8, 128)))
    out_specs = (hd(0), hd(0), hd(0), tile(0), tile(0), pl.BlockSpec((8, 128), lambda b, n: (0, 0)))
    return _pc(body, name="gdn_chunk_bwd", out_shape=out_shape, grid=(Bl, nC), in_specs=in_specs,
               out_specs=out_specs, scratch=[pltpu.VMEM((512, 128), F32)], sem=("arbitrary", "arbitrary"))(
        qkvn, qkvn, qkvn, sm, alog, dtb, gc_s, beta_s, tinv_s, sst_s, w_s, u_s, do)


def _gdn_post(o, z, g, T):
    tt = _pick(T, (512, 256))

    def body(o_ref, z_ref, g_ref, out_ref):
        out_ref[...] = (_rmsnorm(o_ref[...], g_ref[...]) * _silu(z_ref[...])).astype(out_ref.dtype)

    blk = pl.BlockSpec((tt, 128), lambda i, h: (i, h))
    return _pc(body, name="gdn_post_fwd", out_shape=_sds((T, 1024), BF), grid=(T // tt, _H),
               in_specs=[blk, blk, pl.BlockSpec((1, 128), lambda i, h: (0, 0))], out_specs=blk,
               sem=("parallel", "parallel"))(o, z, g)


def _gdn_post_bwd(o, z, g, dmixin, T):
    tt = _pick(T, (512, 256))

    def body(o_ref, z_ref, g_ref, d_ref, do_ref, dz_ref, dg_ref):
        @pl.when((pl.program_id(0) == 0) & (pl.program_id(1) == 0))
        def _():
            dg_ref[...] = jnp.zeros_like(dg_ref)

        _, vjp = jax.vjp(lambda a, b, c: _rmsnorm(a, c) * _silu(b), o_ref[...], z_ref[...], g_ref[...])
        do, dz, dg = vjp(d_ref[...])
        do_ref[...] = do
        dz_ref[...] = dz
        dg_ref[0:1, :] += dg

    blk = pl.BlockSpec((tt, 128), lambda i, h: (i, h))
    return _pc(body, name="gdn_post_bwd", out_shape=(_sds((T, 512)), _sds((T, 512)), _sds((8, 128))),
               grid=(T // tt, _H), in_specs=[blk, blk, pl.BlockSpec((1, 128), lambda i, h: (0, 0)), blk],
               out_specs=(blk, blk, pl.BlockSpec((8, 128), lambda i, h: (0, 0))),
               sem=("arbitrary", "arbitrary"))(o, z, g, dmixin)


def _rms_fwd(sm, g, col_blk, name):
    T = sm.shape[0]
    d = g.shape[1]
    tt = _pick(T, (512, 256))

    def body(x_ref, g_ref, o_ref):
        o_ref[...] = _rmsnorm(x_ref[...], g_ref[...]).astype(o_ref.dtype)

    return _pc(body, name=name, out_shape=_sds((T, d), BF), grid=(T // tt,),
               in_specs=[pl.BlockSpec((tt, d), lambda i: (i, col_blk)), pl.BlockSpec((1, d), lambda i: (0, 0))],
               out_specs=pl.BlockSpec((tt, d), lambda i: (i, 0)), sem=("parallel",))(sm, g)


def _rms_bwd(sm, g, dy, col_blk, name):
    T = sm.shape[0]
    d = g.shape[1]
    tt = _pick(T, (512, 256))

    def body(x_ref, g_ref, d_ref, dx_ref, dg_ref):
        @pl.when(pl.program_id(0) == 0)
        def _():
            dg_ref[...] = jnp.zeros_like(dg_ref)

        _, vjp = jax.vjp(_rmsnorm, x_ref[...], g_ref[...])
        dx, dg = vjp(d_ref[...])
        dx_ref[...] = dx
        dg_ref[0:1, :] += dg

    return _pc(body, name=name, out_shape=(_sds((T, d)), _sds((8, d))), grid=(T // tt,),
               in_specs=[pl.BlockSpec((tt, d), lambda i: (i, col_blk)), pl.BlockSpec((1, d), lambda i: (0, 0)),
                         pl.BlockSpec((tt, d), lambda i: (i, 0))],
               out_specs=(pl.BlockSpec((tt, d), lambda i: (i, 0)), pl.BlockSpec((8, d), lambda i: (0, 0))),
               sem=("arbitrary",))(sm, g, dy)


def _rope_tables(S):
    inv = ROPE_THETA ** (-jnp.arange(0, MLA_ROPE, 2, dtype=F32) / MLA_ROPE)
    ang = jnp.arange(S, dtype=F32)[:, None] * inv[None, :]
    cos, sin = jnp.cos(ang), jnp.sin(ang)
    z = jnp.zeros((S, 64), F32)
    return jnp.concatenate([cos, cos, z], axis=1), jnp.concatenate([-sin, sin, z], axis=1)


def _swap_halves(x):
    lane = lax.broadcasted_iota(jnp.int32, x.shape, 1)
    return jnp.where(lane < 32, pltpu.roll(x, 96, 1), jnp.where(lane < 64, pltpu.roll(x, 32, 1), 0.0))


def _rope_fwd(qraw, sm, cos, sin, S):
    T = qraw.shape[0]
    tt = _pick(S, (512, 256))
    nps = S // tt

    def body(q_ref, kr_ref, c_ref, s_ref, qo_ref, ko_ref):
        c, s = c_ref[...], s_ref[...]
        q = q_ref[...]
        qo_ref[...] = (q * c + _swap_halves(q) * s).astype(qo_ref.dtype)

        @pl.when(pl.program_id(1) == 0)
        def _():
            k = kr_ref[...]
            ko_ref[...] = (k * c + _swap_halves(k) * s).astype(ko_ref.dtype)

    tab = pl.BlockSpec((tt, 128), lambda i, h: (i % nps, 0))
    return _pc(body, name="rope_fwd", out_shape=(_sds((T, 512), BF), _sds((T, 128), BF)), grid=(T // tt, MLA_HEADS),
               in_specs=[pl.BlockSpec((tt, 128), lambda i, h: (i, 2 * h + 1)),
                         pl.BlockSpec((tt, 128), lambda i, h: (i, 3)), tab, tab],
               out_specs=(pl.BlockSpec((tt, 128), lambda i, h: (i, h)), pl.BlockSpec((tt, 128), lambda i, h: (i, 0))),
               sem=("parallel", "arbitrary"))(qraw, sm, cos, sin)


def _rope_bwd(dqraw, dqr, dkrr, cos, sin, S):
    T = dqraw.shape[0]
    tt = _pick(S, (512, 256))
    nps = S // tt

    def body(alias_ref, dq_ref, dk_ref, c_ref, s_ref, qo_ref, ko_ref):
        c, s = c_ref[...], s_ref[...]
        d = dq_ref[...]
        qo_ref[...] = d * c + _swap_halves(d * s)

        @pl.when(pl.program_id(1) == 0)
        def _():
            k = dk_ref[...]
            ko_ref[...] = k * c + _swap_halves(k * s)

    tab = pl.BlockSpec((tt, 128), lambda i, h: (i % nps, 0))
    return _pc(body, name="rope_bwd", out_shape=(_sds(dqraw.shape), _sds((T, 128))), grid=(T // tt, MLA_HEADS),
               in_specs=[ANY, pl.BlockSpec((tt, 128), lambda i, h: (i, h)),
                         pl.BlockSpec((tt, 128), lambda i, h: (i, 0)), tab, tab],
               out_specs=(pl.BlockSpec((tt, 128), lambda i, h: (i, 2 * h + 1)),
                          pl.BlockSpec((tt, 128), lambda i, h: (i, 0))),
               aliases={0: 0}, sem=("parallel", "arbitrary"))(dqraw, dqr, dkrr, cos, sin)


def _attn_probs(qn, qr, kn, kr, i, tq):
    scale = (MLA_NOPE + MLA_ROPE) ** -0.5
    s = (_dot(qn, kn, _NT) + _dot(qr, kr, _NT)) * scale
    qpos = i * tq + lax.broadcasted_iota(jnp.int32, s.shape, 0)
    kpos = lax.broadcasted_iota(jnp.int32, s.shape, 1)
    s = jnp.where((kpos >> 6) <= (qpos >> 6), s, -jnp.inf)
    e = jnp.exp(s - jnp.max(s, axis=-1, keepdims=True))
    return e / jnp.sum(e, axis=-1, keepdims=True), scale


def _attn_fwd(mixin, qraw, qr, kv, krr, Bl, S):
    T = Bl * S
    tq = _pick(S, (256, 128))
    nq = S // tq

    def body(alias_ref, qn_ref, qr_ref, kn_ref, v_ref, kr_ref, o_ref):
        p, _ = _attn_probs(qn_ref[...], qr_ref[...], kn_ref[...], kr_ref[...], pl.program_id(2), tq)
        o_ref[...] = _dot(p, v_ref[...], _NN).astype(o_ref.dtype)

    qrow = lambda b, h, i: b * nq + i
    in_specs = [ANY,
                pl.BlockSpec((tq, 128), lambda b, h, i: (qrow(b, h, i), 2 * h)),
                pl.BlockSpec((tq, 128), lambda b, h, i: (qrow(b, h, i), h)),
                pl.BlockSpec((S, 128), lambda b, h, i: (b, 2 * h)),
                pl.BlockSpec((S, 128), lambda b, h, i: (b, 2 * h + 1)),
                pl.BlockSpec((S, 128), lambda b, h, i: (b, 0))]
    return _pc(body, name="mla_attn_fwd", out_shape=_sds(mixin.shape, mixin.dtype), grid=(Bl, MLA_HEADS, nq),
               in_specs=in_specs, out_specs=pl.BlockSpec((tq, 128), lambda b, h, i: (qrow(b, h, i), 4 + h)),
               aliases={0: 0}, sem=("parallel", "parallel", "parallel"))(mixin, qraw, qr, kv, kv, krr)


def _attn_bwd(qraw, qr, kv, krr, dmixin, Bl, S):
    T = Bl * S
    tq = _pick(S, (256, 128))
    nq = S // tq

    def body(qn_ref, qr_ref, kn_ref, v_ref, kr_ref, do_ref, dqn_ref, dqr_ref, dkn_ref, dv_ref, dkr_ref):
        h, i = pl.program_id(1), pl.program_id(2)
        qn, qrv, kn, v, kr, do = (r[...] for r in (qn_ref, qr_ref, kn_ref, v_ref, kr_ref, do_ref))
        p, scale = _attn_probs(qn, qrv, kn, kr, i, tq)
        dp = _dot(do, v, _NT)
        ds = p * (dp - jnp.sum(dp * p, axis=-1, keepdims=True)) * scale
        dqn_ref[...] = _dot(ds, kn, _NN)
        dqr_ref[...] = _dot(ds, kr, _NN)
        first = i == 0
        dv_ref[...] = jnp.where(first, 0.0, dv_ref[...]) + _dot(p, do, _TN)
        dkn_ref[...] = jnp.where(first, 0.0, dkn_ref[...]) + _dot(ds, qn, _TN)
        dkr_ref[...] = jnp.where(first & (h == 0), 0.0, dkr_ref[...]) + _dot(ds, qrv, _TN)

    qrow = lambda b, h, i: b * nq + i
    in_specs = [pl.BlockSpec((tq, 128), lambda b, h, i: (qrow(b, h, i), 2 * h)),
                pl.BlockSpec((tq, 128), lambda b, h, i: (qrow(b, h, i), h)),
                pl.BlockSpec((S, 128), lambda b, h, i: (b, 2 * h)),
                pl.BlockSpec((S, 128), lambda b, h, i: (b, 2 * h + 1)),
                pl.BlockSpec((S, 128), lambda b, h, i: (b, 0)),
                pl.BlockSpec((tq, 128), lambda b, h, i: (qrow(b, h, i), 4 + h))]
    out_shape = (_sds((T, 1024)), _sds((T, 512)), _sds((T, 512)), _sds((T, 512)), _sds((T, 128)))
    out_specs = (pl.BlockSpec((tq, 128), lambda b, h, i: (qrow(b, h, i), 2 * h)),
                 pl.BlockSpec((tq, 128), lambda b, h, i: (qrow(b, h, i), h)),
                 pl.BlockSpec((S, 128), lambda b, h, i: (b, h)),
                 pl.BlockSpec((S, 128), lambda b, h, i: (b, h)),
                 pl.BlockSpec((S, 128), lambda b, h, i: (b, 0)))
    return _pc(body, name="mla_attn_bwd", out_shape=out_shape, grid=(Bl, MLA_HEADS, nq), in_specs=in_specs,
               out_specs=out_specs, sem=("arbitrary", "arbitrary", "arbitrary"))(qraw, qr, kv, kv, krr, dmixin)


def _ln1_fwd(x, mix, g, b):
    T, D = x.shape
    tt = _pick(T, (512, 256))

    def body(x_ref, m_ref, g_ref, b_ref, o_ref):
        o_ref[...] = _layernorm(ALPHA * x_ref[...] + m_ref[...], g_ref[...], b_ref[...])

    blk = pl.BlockSpec((tt, D), lambda i: (i, 0))
    par = pl.BlockSpec((1, D), lambda i: (0, 0))
    return _pc(body, name="ln1_fwd", out_shape=_sds((T, D)), grid=(T // tt,), in_specs=[blk, blk, par, par],
               out_specs=blk, sem=("parallel",))(x, mix, g, b)


def _ln1_bwd(x, mix, g, b, dh):
    T, D = x.shape
    tt = _pick(T, (256,))

    def body(x_ref, m_ref, g_ref, b_ref, d_ref, dx_ref, dm_ref, acc_ref):
        @pl.when(pl.program_id(0) == 0)
        def _():
            acc_ref[...] = jnp.zeros_like(acc_ref)

        f = lambda xx, mm, gg, bb: _layernorm(ALPHA * xx + mm, gg, bb)
        _, vjp = jax.vjp(f, x_ref[...], m_ref[...], g_ref[...], b_ref[...])
        dx, dm, dg, db = vjp(d_ref[...])
        dx_ref[...] = dx
        dm_ref[...] = dm.astype(dm_ref.dtype)
        acc_ref[0:1, :] += dg
        acc_ref[1:2, :] += db

    blk = pl.BlockSpec((tt, D), lambda i: (i, 0))
    par = pl.BlockSpec((1, D), lambda i: (0, 0))
    return _pc(body, name="ln1_bwd", out_shape=(_sds((T, D)), _sds((T, D), BF), _sds((8, D))), grid=(T // tt,),
               in_specs=[blk, blk, par, par, blk], out_specs=(blk, blk, pl.BlockSpec((8, D), lambda i: (0, 0))),
               sem=("arbitrary",))(x, mix, g, b, dh)


def _ffn_act_fwd(u0, conv_w, conv_b, Bl, S):
    T, C2 = u0.shape
    C = C2 // 2
    cb = _pick(C, (256, 128))
    nj = C // cb

    def body(g_ref, u_ref, wg_ref, wu_ref, bg_ref, bu_ref, o_ref):
        cg = _conv(g_ref[...], wg_ref[...]) + bg_ref[...]
        cu = _conv(u_ref[...], wu_ref[...]) + bu_ref[...]
        o_ref[...] = (_silu(cg) * cu).astype(o_ref.dtype)

    kk = conv_w.shape[0]
    in_specs = [pl.BlockSpec((S, cb), lambda b, j: (b, j)), pl.BlockSpec((S, cb), lambda b, j: (b, j + nj)),
                pl.BlockSpec((kk, cb), lambda b, j: (0, j)), pl.BlockSpec((kk, cb), lambda b, j: (0, j + nj)),
                pl.BlockSpec((1, cb), lambda b, j: (0, j)), pl.BlockSpec((1, cb), lambda b, j: (0, j + nj))]
    return _pc(body, name="ffn_act_fwd", out_shape=_sds((T, C), BF), grid=(Bl, nj), in_specs=in_specs,
               out_specs=pl.BlockSpec((S, cb), lambda b, j: (b, j)), sem=("parallel", "parallel"))(
        u0, u0, conv_w, conv_w, conv_b, conv_b)


def _ffn_act_bwd(u0, conv_w, conv_b, dact, Bl, S):
    T, C2 = u0.shape
    C = C2 // 2
    cb = _pick(C, (256, 128))
    nj = C // cb
    kk = conv_w.shape[0]

    def body(g_ref, u_ref, wg_ref, wu_ref, bg_ref, bu_ref, d_ref,
             dg_ref, du_ref, dwg_ref, dwu_ref, dbg_ref, dbu_ref):
        first = pl.program_id(1) == 0
        xg, xu, wg, wu = g_ref[...], u_ref[...], wg_ref[...], wu_ref[...]
        cg = _conv(xg, wg) + bg_ref[...]
        cu = _conv(xu, wu) + bu_ref[...]
        _, vjp = jax.vjp(lambda a, b: _silu(a) * b, cg, cu)
        dcg, dcu = vjp(d_ref[...].astype(F32))
        dg_ref[...] = _conv_bwd_x(dcg, wg).astype(dg_ref.dtype)
        du_ref[...] = _conv_bwd_x(dcu, wu).astype(du_ref.dtype)
        _conv_bwd_w(xg, dcg, dwg_ref, first)
        _conv_bwd_w(xu, dcu, dwu_ref, first)
        dbg_ref[...] = jnp.where(first, 0.0, dbg_ref[...]) + jnp.sum(dcg, axis=0, keepdims=True)
        dbu_ref[...] = jnp.where(first, 0.0, dbu_ref[...]) + jnp.sum(dcu, axis=0, keepdims=True)

    gs = pl.BlockSpec((S, cb), lambda j, b: (b, j))
    us = pl.BlockSpec((S, cb), lambda j, b: (b, j + nj))
    wgs = pl.BlockSpec((kk, cb), lambda j, b: (0, j))
    wus = pl.BlockSpec((kk, cb), lambda j, b: (0, j + nj))
    bgs = pl.BlockSpec((1, cb), lambda j, b: (0, j))
    bus = pl.BlockSpec((1, cb), lambda j, b: (0, j + nj))
    out_shape = (_sds((T, C), BF), _sds((T, C), BF), _sds((kk, C)), _sds((kk, C)), _sds((1, C)), _sds((1, C)))
    half = pl.BlockSpec((S, cb), lambda j, b: (b, j))
    whalf = pl.BlockSpec((kk, cb), lambda j, b: (0, j))
    bhalf = pl.BlockSpec((1, cb), lambda j, b: (0, j))
    return _pc(body, name="ffn_act_bwd", out_shape=out_shape, grid=(nj, Bl),
               in_specs=[gs, us, wgs, wus, bgs, bus, half],
               out_specs=(half, half, whalf, whalf, bhalf, bhalf), sem=("parallel", "arbitrary"))(
        u0, u0, conv_w, conv_w, conv_b, conv_b, dact)


def _head(h1, ffn, gpre, pp, tgt, bg, g2, b2):
    T, D = h1.shape
    tt = _pick(T, (256,))

    def body(h_ref, f_ref, gp_ref, pp_ref, t_ref, bg_ref, g2_ref, b2_ref, df_ref, dgp_ref, dpp_ref, acc_ref):
        @pl.when(pl.program_id(0) == 0)
        def _():
            acc_ref[...] = jnp.zeros_like(acc_ref)

        h, tg = h_ref[...], t_ref[...]

        def loss_fn(f, gp, p_, bgv, g2v, b2v):
            pre = ALPHA * h + f + jax.nn.sigmoid(gp + bgv) * p_
            err = _layernorm(pre, g2v, b2v) - tg
            return 0.5 * jnp.sum(jnp.mean(err * err, axis=-1, keepdims=True))

        loss, grads = jax.value_and_grad(loss_fn, argnums=(0, 1, 2, 3, 4, 5))(
            f_ref[...], gp_ref[...], pp_ref[...], bg_ref[...], g2_ref[...], b2_ref[...])
        df_ref[...] = grads[0]
        dgp_ref[...] = grads[1].astype(dgp_ref.dtype)
        dpp_ref[...] = grads[2].astype(dpp_ref.dtype)
        acc_ref[0:1, :] += grads[3]
        acc_ref[1:2, :] += grads[4]
        acc_ref[2:3, :] += grads[5]
        acc_ref[3:4, :] += jnp.broadcast_to(loss, (1, D))

    blk = pl.BlockSpec((tt, D), lambda i: (i, 0))
    par = pl.BlockSpec((1, D), lambda i: (0, 0))
    return _pc(body, name="loss_head", out_shape=(_sds((T, D)), _sds((T, D), BF), _sds((T, D), BF), _sds((8, D))),
               grid=(T // tt,), in_specs=[blk, blk, blk, blk, blk, par, par, par],
               out_specs=(blk, blk, blk, pl.BlockSpec((8, D), lambda i: (0, 0))), sem=("arbitrary",))(
        h1, ffn, gpre, pp, tgt, bg, g2, b2)


def _adamw(parts, w, m, v, name):
    R, C = w.shape
    tr = R if R <= 512 else _pick(R, (256,))

    def body(p_ref, w_ref, m_ref, v_ref, g_ref, d_ref, nm_ref, nv_ref):
        g = p_ref[0]
        for j in range(1, N_DEV):
            g = g + p_ref[j]
        mm = ADAM_B1 * m_ref[...] + (1.0 - ADAM_B1) * g
        vv = ADAM_B2 * v_ref[...] + (1.0 - ADAM_B2) * jnp.square(g)
        m_hat = mm / (1.0 - ADAM_B1 ** ADAM_STEP)
        v_hat = vv / (1.0 - ADAM_B2 ** ADAM_STEP)
        g_ref[...] = g
        d_ref[...] = -ADAM_LR * (m_hat / (jnp.sqrt(v_hat) + ADAM_EPS) + ADAM_WD * w_ref[...])
        nm_ref[...] = mm
        nv_ref[...] = vv

    blk = pl.BlockSpec((tr, C), lambda i: (i, 0))
    o = _sds((R, C))
    return _pc(body, name=name, out_shape=(o, o, o, o), grid=(R // tr,),
               in_specs=[pl.BlockSpec((N_DEV, tr, C), lambda i: (0, i, 0)), blk, blk, blk],
               out_specs=(blk, blk, blk, blk), sem=("parallel",))(parts, w, m, v)


SMALL_ORDER = ("gdn_a_log", "gdn_dt_bias", "gdn_norm_g", "mla_q_norm_g", "mla_kv_norm_g", "ln1_g", "ln1_b",
               "ffn_conv_b", "ple_b_gate", "ln2_g", "ln2_b")


def _pad_lanes(v, n=128):
    return jnp.pad(v, ((0, 0), (0, n - v.shape[1])))


def _local_step(x, p, tgt, W, sp, Bl, S):
    T = Bl * S
    w_in = W["w_in"]
    wqkv, wz = w_in[:, :1536], w_in[:, 1536:2048]
    z64 = jnp.zeros((w_in.shape[0], 64), w_in.dtype)
    z124 = jnp.zeros((w_in.shape[0], 124), w_in.dtype)
    wsm = jnp.concatenate([w_in[:, 2056:2440], w_in[:, 2696:2760], z64, w_in[:, 2440:2696],
                           w_in[:, 2048:2052], z124, w_in[:, 2052:2056], z124], axis=1)
    wq = jnp.pad(W["mla_w_q_up"].reshape(-1, MLA_HEADS, 192), ((0, 0), (0, 0), (0, 64))).reshape(-1, 1024)
    wkv = W["mla_w_kv_up"]
    alog, dtb = _pad_lanes(sp["gdn_a_log"]), _pad_lanes(sp["gdn_dt_bias"])
    cos, sin = _rope_tables(S)

    qkv = _mm(x, wqkv, "nn", name="mm_qkv")
    z = _mm(x, wz, "nn", name="mm_z")
    sm = _mm(x, wsm, "nn", name="mm_sm")
    qkvn = _gdn_pre_fwd(qkv, W["gdn_conv_w"], Bl, S)
    o, tinv_s, sst_s, w_s, u_s, gc_s, beta_s = _gdn_fwd(qkvn, sm, alog, dtb, Bl, S)
    mixin = _gdn_post(o, z, sp["gdn_norm_g"], T)
    cqn = _rms_fwd(sm, sp["mla_q_norm_g"], 0, "rms_q_fwd")
    ckvn = _rms_fwd(sm, sp["mla_kv_norm_g"], 2, "rms_kv_fwd")
    qraw = _mm(cqn, wq, "nn", name="mm_qup")
    kv = _mm(ckvn, wkv, "nn", name="mm_kvup")
    qr, krr = _rope_fwd(qraw, sm, cos, sin, S)
    mixin = _attn_fwd(mixin, qraw, qr, kv, krr, Bl, S)
    mix = _mm(mixin, W["w_out"], "nn", name="mm_out")
    h1 = _ln1_fwd(x, mix, sp["ln1_g"], sp["ln1_b"])
    u0 = _mm(h1, W["ffn_w_up"], "nn", name="mm_up")
    act = _ffn_act_fwd(u0, W["ffn_conv_w"], sp["ffn_conv_b"], Bl, S)
    ffn = _mm(act, W["ffn_w_down"], "nn", name="mm_down")
    gpre = _mm(h1, W["ple_w_gate"], "nn", name="mm_gate")
    pp = _mm(p, W["ple_w_proj"], "nn", name="mm_proj")

    dffn, dgpre, dpp, hacc = _head(h1, ffn, gpre, pp, tgt, sp["ple_b_gate"], sp["ln2_g"], sp["ln2_b"])
    loss = hacc[3, 0]
    gW, gs = {}, {}
    gs["ple_b_gate"], gs["ln2_g"], gs["ln2_b"] = hacc[0:1], hacc[1:2], hacc[2:3]
    gW["ple_w_proj"] = _mm(p, dpp, "tn", name="mm_dproj")
    gW["ple_w_gate"] = _mm(h1, dgpre, "tn", name="mm_dgate")
    gW["ffn_w_down"] = _mm(act, dffn, "tn", name="mm_ddown")
    dact = _mm(dffn, W["ffn_w_down"], "nt", name="mm_dact")
    du_g, du_u, dcw_g, dcw_u, dcb_g, dcb_u = _ffn_act_bwd(u0, W["ffn_conv_w"], sp["ffn_conv_b"], dact, Bl, S)
    du0 = jnp.concatenate([du_g, du_u], axis=1)
    gW["ffn_conv_w"] = jnp.concatenate([dcw_g, dcw_u], axis=1)
    gs["ffn_conv_b"] = jnp.concatenate([dcb_g, dcb_u], axis=1)
    gW["ffn_w_up"] = _mm(h1, du0, "tn", name="mm_dup")
    dh1 = _mm(dgpre, W["ple_w_gate"], "nt", name="mm_dh1_gate", add=ALPHA * dffn)
    dh1 = _mm(du0, W["ffn_w_up"], "nt", name="mm_dh1_up", add=dh1)
    dxa, dmix, acc1 = _ln1_bwd(x, mix, sp["ln1_g"], sp["ln1_b"], dh1)
    gs["ln1_g"], gs["ln1_b"] = acc1[0:1], acc1[1:2]
    gW["w_out"] = _mm(mixin, dmix, "tn", name="mm_dwout")
    dmixin = _mm(dmix, W["w_out"], "nt", name="mm_dmixin")
    do, dz, gacc = _gdn_post_bwd(o, z, sp["gdn_norm_g"], dmixin, T)
    gs["gdn_norm_g"] = gacc[0:1]
    dqn, dkn, dvn, da, db, cacc = _gdn_bwd(qkvn, sm, alog, dtb, gc_s, beta_s, tinv_s, sst_s, w_s, u_s, do, Bl, S)
    gs["gdn_a_log"], gs["gdn_dt_bias"] = cacc[0:1, :GDN_HEADS], cacc[1:2, :GDN_HEADS]
    dqkv, gW["gdn_conv_w"] = _gdn_pre_bwd(qkv, W["gdn_conv_w"], jnp.concatenate([dqn, dkn, dvn], axis=1), Bl, S)
    dqraw, dqr, dkn_m, dv_m, dkrr = _attn_bwd(qraw, qr, kv, krr, dmixin, Bl, S)
    dqraw, dkr = _rope_bwd(dqraw, dqr, dkrr, cos, sin, S)
    dkv = jnp.stack([dkn_m.reshape(T, MLA_HEADS, 128), dv_m.reshape(T, MLA_HEADS, 128)], axis=2).reshape(T, 1024)
    gwq = _mm(cqn, dqraw, "tn", name="mm_dwq")
    gW["mla_w_q_up"] = gwq.reshape(-1, MLA_HEADS, 256)[:, :, :192].reshape(-1, MLA_HEADS * 192)
    gW["mla_w_kv_up"] = _mm(ckvn, dkv, "tn", name="mm_dwkv")
    dcqn = _mm(dqraw, wq, "nt", name="mm_dcqn")
    dckvn = _mm(dkv, wkv, "nt", name="mm_dckvn")
    dcq, qacc = _rms_bwd(sm, sp["mla_q_norm_g"], dcqn, 0, "rms_q_bwd")
    dckv, kacc = _rms_bwd(sm, sp["mla_kv_norm_g"], dckvn, 2, "rms_kv_bwd")
    gs["mla_q_norm_g"], gs["mla_kv_norm_g"] = qacc[0:1], kacc[0:1]
    dsm = jnp.concatenate([dcq, dkr, dckv, da, db], axis=1)
    gqkv = _mm(x, dqkv, "tn", name="mm_dwqkv")
    gz = _mm(x, dz, "tn", name="mm_dwz")
    gsm = _mm(x, dsm, "tn", name="mm_dwsm")
    gW["w_in"] = jnp.concatenate([gqkv, gz, gsm[:, 768:772], gsm[:, 896:900], gsm[:, 0:384], gsm[:, 512:768],
                                  gsm[:, 384:448]], axis=1)
    dx = _mm(dqkv, wqkv, "nt", name="mm_dx_qkv", add=dxa)
    dx = _mm(dz, wz, "nt", name="mm_dx_z", add=dx)
    dx = _mm(dsm, wsm, "nt", name="mm_dx_sm", add=dx)
    return loss, dx, gW, gs


COL_SHARDED = ("w_in", "mla_w_q_up", "mla_w_kv_up", "ffn_w_up", "ple_w_proj", "gdn_conv_w", "ffn_conv_w")
ROW_SHARDED = ("w_out", "ffn_w_down", "ple_w_gate")
SHARDED = ("w_in", "gdn_conv_w", "mla_w_q_up", "mla_w_kv_up", "w_out", "ffn_w_up", "ffn_conv_w", "ffn_w_down",
           "ple_w_gate", "ple_w_proj")
WEIGHTS = ("w_in", "gdn_conv_w", "gdn_a_log", "gdn_dt_bias", "gdn_norm_g", "mla_q_norm_g", "mla_w_q_up",
           "mla_kv_norm_g", "mla_w_kv_up", "w_out", "ln1_g", "ln1_b", "ffn_w_up", "ffn_conv_w", "ffn_conv_b",
           "ffn_w_down", "ple_w_gate", "ple_b_gate", "ple_w_proj", "ln2_g", "ln2_b")
SMALL_ROWS, SMALL_LANES = 96, 128


def _gathered_to_full(name, g):
    if name in COL_SHARDED:
        return jnp.transpose(g, (1, 0, 2)).reshape(g.shape[1], -1)
    return g.reshape(-1, g.shape[-1])


def _full_to_blocks(name, gfull, shard_shape):
    r, c = shard_shape
    if name in COL_SHARDED:
        return jnp.transpose(gfull.reshape(r, N_DEV, c), (1, 0, 2))
    return gfull.reshape(N_DEV, r, c)


def _pack_small(d):
    flat = jnp.concatenate([d[n].reshape(-1) for n in SMALL_ORDER])
    return jnp.pad(flat, (0, SMALL_ROWS * SMALL_LANES - flat.shape[0])).reshape(SMALL_ROWS, SMALL_LANES)


def _unpack_small(packed, shapes):
    flat = packed.reshape(-1)
    out, off = {}, 0
    for n in SMALL_ORDER:
        sz = math.prod(shapes[n])
        out[n] = flat[off:off + sz].reshape(shapes[n])
        off += sz
    return out


def kernel(x, p, w_in, gdn_conv_w, gdn_a_log, gdn_dt_bias, gdn_norm_g, mla_q_norm_g, mla_w_q_up, mla_kv_norm_g, mla_w_kv_up, w_out, ln1_g, ln1_b, ffn_w_up, ffn_conv_w, ffn_conv_b, ffn_w_down, ple_w_gate, ple_b_gate, ple_w_proj, ln2_g, ln2_b, loss_target, m_w_in, m_gdn_conv_w, m_gdn_a_log, m_gdn_dt_bias, m_gdn_norm_g, m_mla_q_norm_g, m_mla_w_q_up, m_mla_kv_norm_g, m_mla_w_kv_up, m_w_out, m_ln1_g, m_ln1_b, m_ffn_w_up, m_ffn_conv_w, m_ffn_conv_b, m_ffn_w_down, m_ple_w_gate, m_ple_b_gate, m_ple_w_proj, m_ln2_g, m_ln2_b, v_w_in, v_gdn_conv_w, v_gdn_a_log, v_gdn_dt_bias, v_gdn_norm_g, v_mla_q_norm_g, v_mla_w_q_up, v_mla_kv_norm_g, v_mla_w_kv_up, v_w_out, v_ln1_g, v_ln1_b, v_ffn_w_up, v_ffn_conv_w, v_ffn_conv_b, v_ffn_w_down, v_ple_w_gate, v_ple_b_gate, v_ple_w_proj, v_ln2_g, v_ln2_b):
    loc = dict(locals())
    wts = {n: loc[n] for n in WEIGHTS}
    ms = {n: loc["m_" + n] for n in WEIGHTS}
    vs = {n: loc["v_" + n] for n in WEIGHTS}
    Bl, S, D = x.shape
    T = Bl * S

    conv = ("gdn_conv_w", "ffn_conv_w")
    send = [wts[n][0] if n in conv else wts[n][0].astype(BF) for n in SHARDED]
    got = _exchange(send, ["ag"] * len(send), "ag_weights")
    W = {n: _gathered_to_full(n, g) for n, g in zip(SHARDED, got)}
    sp = {n: wts[n].reshape(1, -1) for n in SMALL_ORDER}

    loss, dx, gW, gs = _local_step(x.reshape(T, D), p.reshape(T, -1), loss_target.reshape(T, D), W, sp, Bl, S)

    blocks = [_full_to_blocks(n, gW[n], wts[n].shape[1:]) for n in SHARDED]
    recv = _exchange(blocks + [_pack_small(gs)], ["a2a"] * len(blocks) + ["ag"], "exchange_grads")

    res = {}
    for n, parts in zip(SHARDED, recv[:-1]):
        res[n] = _adamw(parts, wts[n][0], ms[n][0], vs[n][0], "adamw_" + n)
    small = _adamw(recv[-1], _pack_small(wts), _pack_small(ms), _pack_small(vs), "adamw_small")
    shapes = {n: wts[n].shape for n in SMALL_ORDER}
    small = [_unpack_small(t, shapes) for t in small]
    for n in SMALL_ORDER:
        res[n] = tuple(t[n] for t in small)
    for n in SHARDED:
        res[n] = tuple(t[None] for t in res[n])

    loss = lax.psum(loss, ("x", "y", "c"))
    outs = [loss, dx.reshape(Bl, S, D)]
    for k in range(4):
        outs += [res[n][k] for n in WEIGHTS]
    return tuple(outs)
```

```python
import math

import jax
import jax.numpy as jnp
from jax import lax
from jax.experimental import pallas as pl
from jax.experimental.pallas import tpu as pltpu

F32 = jnp.float32
BF = jnp.bfloat16
_MXU = jnp.bfloat16
_INTERPRET = None
_VMEM_LIMIT = 56 * 1024 * 1024
HI = lax.Precision.HIGHEST

N_DEV = 8
CHUNK = 64
GDN_HEADS = 4
GDN_DK = 128
MLA_HEADS = 4
MLA_NOPE = 128
MLA_ROPE = 64
ROPE_THETA = 10000.0
ALPHA = 2.0 ** 0.25
NORM_EPS = 1e-6
ADAM_LR, ADAM_B1, ADAM_B2, ADAM_EPS, ADAM_WD, ADAM_STEP = 0.001, 0.9, 0.999, 1e-08, 0.01, 10
MESH = pl.DeviceIdType.MESH
ANY = pl.BlockSpec(memory_space=pl.ANY)
_NN = (((1,), (0,)), ((), ()))
_NT = (((1,), (1,)), ((), ()))
_TN = (((0,), (0,)), ((), ()))


def _sds(shape, dtype=F32):
    return jax.ShapeDtypeStruct(tuple(shape), dtype)


def _pick(n, cands):
    for c in cands:
        if n % c == 0:
            return c
    return n


def _xchg_out_shapes(arrs, modes):
    return [_sds((N_DEV,) + a.shape if m == "ag" else a.shape, a.dtype) for a, m in zip(arrs, modes)]


def _xchg_scratch(n):
    return [pltpu.SemaphoreType.DMA((n, N_DEV - 1)), pltpu.SemaphoreType.DMA((n, N_DEV - 1)),
            pltpu.SemaphoreType.DMA((n,))]


def _xchg_copies(ins, outs, send, recv, loc, modes):
    x, y, c = lax.axis_index("x"), lax.axis_index("y"), lax.axis_index("c")
    me = 4 * x + 2 * y + c
    copies = []
    for ai, mode in enumerate(modes):
        ag = mode == "ag"
        copies.append(pltpu.make_async_copy(ins[ai] if ag else ins[ai].at[me], outs[ai].at[me], loc.at[ai]))
        for r in range(1, N_DEV):
            px = 1 - x if (r >> 2) & 1 else x
            py = 1 - y if (r >> 1) & 1 else y
            pc = 1 - c if r & 1 else c
            peer = 4 * px + 2 * py + pc
            copies.append(pltpu.make_async_remote_copy(
                src_ref=ins[ai] if ag else ins[ai].at[peer], dst_ref=outs[ai].at[me],
                send_sem=send.at[ai, r - 1], recv_sem=recv.at[ai, r - 1],
                device_id=(px, py, pc), device_id_type=MESH))
    return copies


def _pc(body, *, name, out_shape, grid=None, in_specs=None, out_specs=None, scratch=(), sem=None, aliases=None,
        xchg=None):
    kw = {}
    if _INTERPRET is not None:
        kw["interpret"] = _INTERPRET
    single = not isinstance(out_shape, (tuple, list))
    out_shape = [out_shape] if single else list(out_shape)
    if out_specs is not None:
        out_specs = [out_specs] if single else list(out_specs)
    scratch = list(scratch)
    if xchg is not None:
        xarrs, xmodes = xchg
        n_in, n_out, n_scr, nx = len(in_specs), len(out_shape), len(scratch), len(xarrs)
        single = False
        inner = body

        def body(*refs):
            ins, xins = refs[:n_in], refs[n_in:n_in + nx]
            outs = refs[n_in + nx:n_in + nx + n_out]
            xouts = refs[n_in + nx + n_out:n_in + 2 * nx + n_out]
            scr = refs[n_in + 2 * nx + n_out:n_in + 2 * nx + n_out + n_scr]
            send, recv, loc = refs[n_in + 2 * nx + n_out + n_scr:]
            first = last = None
            for d, g in enumerate(grid):
                f, l = pl.program_id(d) == 0, pl.program_id(d) == g - 1
                first, last = (f, l) if first is None else (first & f, last & l)

            @pl.when(first)
            def _():
                for cp in _xchg_copies(xins, xouts, send, recv, loc, xmodes):
                    cp.start()

            inner(*ins, *outs, *scr)

            @pl.when(last)
            def _():
                for cp in _xchg_copies(xins, xouts, send, recv, loc, xmodes):
                    cp.wait()

        in_specs = list(in_specs) + [ANY] * nx
        out_specs = out_specs + [ANY] * nx
        out_shape = out_shape + _xchg_out_shapes(xarrs, xmodes)
        scratch = scratch + _xchg_scratch(nx)
        sem = ("arbitrary",) * len(grid)
    if grid is not None:
        kw["grid"] = grid
    if in_specs is not None:
        kw["in_specs"] = in_specs
    if out_specs is not None:
        kw["out_specs"] = out_specs[0] if single else tuple(out_specs)
    cp = dict(vmem_limit_bytes=_VMEM_LIMIT)
    if sem is not None:
        cp["dimension_semantics"] = sem
    call = pl.pallas_call(body, name=name, out_shape=out_shape[0] if single else tuple(out_shape),
                          scratch_shapes=scratch, input_output_aliases=aliases or {},
                          compiler_params=pltpu.CompilerParams(**cp), **kw)
    if xchg is None:
        return call
    return lambda *ins: call(*ins, *xchg[0])


def _exchange(arrs, modes, name):
    n = len(arrs)

    def body(*refs):
        copies = _xchg_copies(refs[:n], refs[n:2 * n], *refs[2 * n:], modes)
        for cp in copies:
            cp.start()
        for cp in copies:
            cp.wait()

    return _pc(body, name=name, out_shape=tuple(_xchg_out_shapes(arrs, modes)), in_specs=[ANY] * n,
               out_specs=tuple([ANY] * n), scratch=_xchg_scratch(n))(*arrs)


_FULL_K_MAX = 2816


def _mm(a, b, mode, *, name, add=None, add_scale=1.0, out_dtype=F32):
    (M, K) = a.shape
    (K2, N) = b.shape if mode == "nn" else b.shape[::-1]
    assert K == K2, (a.shape, b.shape, mode)
    tm = _pick(M, (1024, 1408, 512, 384, 256, 128))
    tn = _pick(N, (1024, 1408, 768, 512, 384, 256, 128))
    tk = K if K <= _FULL_K_MAX else _pick(K, (2048, 1408, 1024, 512))
    nk = K // tk
    dims = _NN if mode == "nn" else _NT
    has_add = add is not None

    def finish(r, add_ref, o_ref):
        if has_add:
            r = r + add_scale * add_ref[...].astype(F32)
        o_ref[...] = r.astype(out_dtype)

    if nk == 1:
        def body(a_ref, b_ref, *rest):
            r = lax.dot_general(a_ref[...].astype(_MXU), b_ref[...].astype(_MXU), dims, preferred_element_type=F32)
            finish(r, rest[0] if has_add else None, rest[-1])

        a_spec = pl.BlockSpec((tm, K), lambda i, j: (i, 0))
        b_spec = (pl.BlockSpec((K, tn), lambda i, j: (0, j)) if mode == "nn"
                  else pl.BlockSpec((tn, K), lambda i, j: (j, 0)))
        o_spec = pl.BlockSpec((tm, tn), lambda i, j: (i, j))
        grid, sem, scratch = (M // tm, N // tn), ("parallel", "parallel"), []
    else:
        def body(a_ref, b_ref, *rest):
            o_ref, acc = rest[-2], rest[-1]
            k = pl.program_id(2)

            @pl.when(k == 0)
            def _():
                acc[...] = jnp.zeros_like(acc)

            acc[...] += lax.dot_general(a_ref[...].astype(_MXU), b_ref[...].astype(_MXU), dims,
                                        preferred_element_type=F32)

            @pl.when(k == nk - 1)
            def _():
                finish(acc[...], rest[0] if has_add else None, o_ref)

        a_spec = pl.BlockSpec((tm, tk), lambda i, j, k: (i, k))
        b_spec = (pl.BlockSpec((tk, tn), lambda i, j, k: (k, j)) if mode == "nn"
                  else pl.BlockSpec((tn, tk), lambda i, j, k: (j, k)))
        o_spec = pl.BlockSpec((tm, tn), lambda i, j, k: (i, j))
        grid, sem, scratch = (M // tm, N // tn, nk), ("parallel", "parallel", "arbitrary"), [pltpu.VMEM((tm, tn), F32)]

    ins = [a, b] + ([add] if has_add else [])
    specs = [a_spec, b_spec] + ([o_spec] if has_add else [])
    return _pc(body, name=name, out_shape=_sds((M, N), out_dtype), grid=grid, in_specs=specs, out_specs=o_spec,
               scratch=scratch, sem=sem)(*ins)


def _dot(a, b, dims):
    return lax.dot_general(a.astype(_MXU), b.astype(_MXU), dims, preferred_element_type=F32)


def _split(a):
    hi = a.astype(BF)
    lo = (a - hi.astype(F32)).astype(BF)
    return hi, lo


def _dot3(a, b, dims):
    ah, al = _split(a)
    bh, bl = _split(b)
    d = lambda u, v: lax.dot_general(u, v, dims, preferred_element_type=F32)
    return d(ah, bh) + (d(ah, bl) + d(al, bh))


def _softplus(x):
    return jnp.maximum(x, 0.0) + jnp.log1p(jnp.exp(-jnp.abs(x)))


def _silu(x):
    return x * jax.nn.sigmoid(x)


def _rmsnorm(x, g):
    return x * lax.rsqrt(jnp.mean(x * x, axis=-1, keepdims=True) + NORM_EPS) * g


def _layernorm(x, g, b):
    mu = jnp.mean(x, axis=-1, keepdims=True)
    xc = x - mu
    var = jnp.mean(xc * xc, axis=-1, keepdims=True)
    return xc * lax.rsqrt(var + NORM_EPS) * g + b


def _shift_dn(x, s):
    if s == 0:
        return x
    t = lax.broadcasted_iota(jnp.int32, x.shape, 0)
    return jnp.where(t >= s, pltpu.roll(x, s, 0), 0.0)


def _shift_up(x, s):
    if s == 0:
        return x
    n = x.shape[0]
    t = lax.broadcasted_iota(jnp.int32, x.shape, 0)
    return jnp.where(t < n - s, pltpu.roll(x, n - s, 0), 0.0)


def _conv(x, w):
    kk = w.shape[0]
    y = w[kk - 1:kk, :] * x
    for j in range(kk - 1):
        y = y + w[j:j + 1, :] * _shift_dn(x, kk - 1 - j)
    return y


def _conv_bwd_x(dy, w):
    kk = w.shape[0]
    dx = w[kk - 1:kk, :] * dy
    for j in range(kk - 1):
        dx = dx + w[j:j + 1, :] * _shift_up(dy, kk - 1 - j)
    return dx


def _conv_bwd_w(x, dy, dw_ref, first):
    kk = dw_ref.shape[0]
    for j in range(kk):
        r = jnp.sum(dy * _shift_dn(x, kk - 1 - j), axis=0, keepdims=True)
        prev = jnp.where(first, 0.0, dw_ref[j:j + 1, :])
        dw_ref[j:j + 1, :] = prev + r


def _gdn_post_conv(c, j):
    h = _silu(c)
    hn = h * lax.rsqrt(jnp.sum(h * h, axis=-1, keepdims=True) + NORM_EPS)
    return jnp.where(j < GDN_HEADS, hn * (GDN_DK ** -0.5), jnp.where(j < 2 * GDN_HEADS, hn, h))


def _gdn_pre_fwd(qkv, conv_w, Bl, S):
    T, C = qkv.shape
    nj = C // 128

    def body(x_ref, w_ref, o_ref):
        o_ref[...] = _gdn_post_conv(_conv(x_ref[...], w_ref[...]), pl.program_id(1))

    blk = pl.BlockSpec((S, 128), lambda b, j: (b, j))
    return _pc(body, name="gdn_pre_fwd", out_shape=_sds((T, C)), grid=(Bl, nj),
               in_specs=[blk, pl.BlockSpec((conv_w.shape[0], 128), lambda b, j: (0, j))], out_specs=blk,
               sem=("parallel", "parallel"))(qkv, conv_w)


def _gdn_pre_bwd(qkv, conv_w, dout, Bl, S):
    T, C = qkv.shape
    nj = C // 128
    kk = conv_w.shape[0]

    def body(x_ref, w_ref, d_ref, dx_ref, dw_ref):
        j, b = pl.program_id(0), pl.program_id(1)
        x, w = x_ref[...], w_ref[...]
        c = _conv(x, w)
        _, vjp = jax.vjp(lambda u: _gdn_post_conv(u, j), c)
        (dc,) = vjp(d_ref[...])
        dx_ref[...] = _conv_bwd_x(dc, w).astype(dx_ref.dtype)
        _conv_bwd_w(x, dc, dw_ref, b == 0)

    blk = pl.BlockSpec((S, 128), lambda j, b: (b, j))
    wblk = pl.BlockSpec((kk, 128), lambda j, b: (0, j))
    return _pc(body, name="gdn_pre_bwd", out_shape=(_sds((T, C), BF), _sds((kk, C))), grid=(nj, Bl),
               in_specs=[blk, wblk, blk], out_specs=(blk, wblk), sem=("parallel", "arbitrary"))(qkv, conv_w, dout)


_H = GDN_HEADS
_HC = _H * CHUNK


def _st(x):
    return jnp.concatenate([x[:, h * 128:(h + 1) * 128] for h in range(_H)], axis=0)


def _unst(x):
    return jnp.concatenate([x[h * CHUNK:(h + 1) * CHUNK] for h in range(_H)], axis=1)


def _stc(t):
    return jnp.concatenate([t[:, h:h + 1] for h in range(_H)], axis=0)


def _untile(col):
    lane = lax.broadcasted_iota(jnp.int32, (CHUNK, 128), 1)
    out = jnp.zeros((CHUNK, 128), F32)
    for h in range(_H):
        out = out + jnp.where(lane == h, col[h * CHUNK:(h + 1) * CHUNK], 0.0)
    return out


def _rowform(col):
    return jnp.broadcast_to(col, (_HC, 128)).T[0:1, :]


def _tri(n, upper=False):
    i = lax.broadcasted_iota(jnp.int32, (n, n), 0)
    j = lax.broadcasted_iota(jnp.int32, (n, n), 1)
    return jnp.where((j >= i) if upper else (j <= i), 1.0, 0.0).astype(F32)


def _gdn_chunk_common(q, k, v, gc, beta):
    Q, K, V = _st(q), _st(k), _st(v)
    B, GC = _stc(beta), _stc(gc)
    GL = jnp.concatenate([jnp.broadcast_to(gc[CHUNK - 1:CHUNK, h:h + 1], (CHUNK, 1)) for h in range(_H)], axis=0)
    ii = lax.broadcasted_iota(jnp.int32, (_HC, _HC), 0)
    jj = lax.broadcasted_iota(jnp.int32, (_HC, _HC), 1)
    same = (ii >> 6) == (jj >> 6)
    incl = same & (ii >= jj)
    strict = same & (ii > jj)
    diff = GC - _rowform(GC)
    D = jnp.where(incl, jnp.exp(jnp.where(incl, diff, 0.0)), 0.0)
    KB = K * B
    A = jnp.where(strict, _dot(KB, K, _NT) * D, 0.0)
    P = jnp.where(incl, _dot(Q, K, _NT) * D, 0.0)
    EG = jnp.exp(GC)
    ED = jnp.exp(GL - GC)
    return dict(Q=Q, K=K, V=V, B=B, GC=GC, GL=GL, incl=incl, strict=strict, D=D, KB=KB, A=A, P=P, EG=EG, ED=ED,
                QG=Q * EG, KD=K * ED)


def _neumann_inv(A):
    n = A.shape[0]
    i = lax.broadcasted_iota(jnp.int32, (n, n), 0)
    j = lax.broadcasted_iota(jnp.int32, (n, n), 1)
    N = -A
    acc = jnp.where(i == j, 1.0, 0.0) + N
    Pw = N
    for _ in range(5):
        Pw = _dot3(Pw, Pw, _NN)
        acc = acc + _dot3(acc, Pw, _NN)
    return acc


def _gates(a, alog, dtb):
    return -jnp.exp(alog) * _softplus(a + dtb)


def _hs(x, h, n=CHUNK):
    return x[h * n:(h + 1) * n]


def _gdn_fwd(qkvn, sm, alog, dtb, Bl, S, xchg=None):
    T = Bl * S
    nC = S // CHUNK

    def body(q_ref, k_ref, v_ref, a_ref, b_ref, alog_ref, dtb_ref,
             o_ref, tinv_ref, sst_ref, w_ref, u_ref, gc_ref, beta_ref, S_ref):
        @pl.when(pl.program_id(1) == 0)
        def _():
            S_ref[...] = jnp.zeros_like(S_ref)

        g = _gates(a_ref[...], alog_ref[...], dtb_ref[...])
        beta = jax.nn.sigmoid(b_ref[...])
        gc = jnp.dot(_tri(CHUNK), g, precision=HI, preferred_element_type=F32)
        cm = _gdn_chunk_common(q_ref[...], k_ref[...], v_ref[...], gc, beta)
        tinv = _neumann_inv(cm["A"])
        W = _dot3(tinv, cm["KB"] * cm["EG"], _NN)
        U = _dot3(tinv, cm["V"] * cm["B"], _NN)
        s_old = [S_ref[h * 128:(h + 1) * 128, :] for h in range(_H)]
        vn = [_hs(U, h) - _dot(_hs(W, h), s_old[h], _NN) for h in range(_H)]
        o_intra = _dot(cm["P"], jnp.concatenate(vn, axis=0), _NN)
        outs = []
        for h in range(_H):
            outs.append(_dot(_hs(cm["QG"], h), s_old[h], _NN) + _hs(o_intra, h))
            gl = jnp.exp(gc[CHUNK - 1:CHUNK, h:h + 1])
            S_ref[h * 128:(h + 1) * 128, :] = gl * s_old[h] + _dot(_hs(cm["KD"], h), vn[h], _TN)
            sst_ref[h * 128:(h + 1) * 128, :] = s_old[h]
        o_ref[...] = jnp.concatenate(outs, axis=1)
        tinv_ref[...] = tinv
        w_ref[...] = _unst(W)
        u_ref[...] = _unst(U)
        gc_ref[...] = gc
        beta_ref[...] = beta

    row = lambda b, n: b * nC + n
    hd = lambda c: pl.BlockSpec((CHUNK, 512), lambda b, n: (row(b, n), c))
    tile = lambda c: pl.BlockSpec((CHUNK, 128), lambda b, n: (row(b, n), c))
    par = pl.BlockSpec((1, 128), lambda b, n: (0, 0))
    out_shape = (_sds((T, 512)), _sds((T // CHUNK * _HC, _HC)), _sds((T // CHUNK * 512, 128)),
                 _sds((T, 512)), _sds((T, 512)), _sds((T, 128)), _sds((T, 128)))
    out_specs = (hd(0), pl.BlockSpec((_HC, _HC), lambda b, n: (row(b, n), 0)),
                 pl.BlockSpec((512, 128), lambda b, n: (row(b, n), 0)), hd(0), hd(0), tile(0), tile(0))
    return _pc(body, name="gdn_chunk_fwd", out_shape=out_shape, grid=(Bl, nC),
               in_specs=[hd(0), hd(1), hd(2), tile(6), tile(7), par, par], out_specs=out_specs,
               scratch=[pltpu.VMEM((512, 128), F32)], sem=("parallel", "arbitrary"), xchg=xchg)(
        qkvn, qkvn, qkvn, sm, sm, alog, dtb)


def _gdn_bwd(qkvn, sm, alog, dtb, gc_s, beta_s, tinv_s, sst_s, w_s, u_s, do, Bl, S, xchg=None):
    T = Bl * S
    nC = S // CHUNK

    def body(q_ref, k_ref, v_ref, a_ref, alog_ref, dtb_ref, gc_ref, beta_ref, tinv_ref, sst_ref, w_ref, u_ref,
             do_ref, dqkv_ref, dsm_ref, acc_ref, dS_ref):
        first_chunk = pl.program_id(1) == 0

        @pl.when(first_chunk)
        def _():
            dS_ref[...] = jnp.zeros_like(dS_ref)

        @pl.when(first_chunk & (pl.program_id(0) == 0))
        def _():
            acc_ref[...] = jnp.zeros_like(acc_ref)

        gc, beta = gc_ref[...], beta_ref[...]
        cm = _gdn_chunk_common(q_ref[...], k_ref[...], v_ref[...], gc, beta)
        Q, K, V, B, D, KB, A, P = (cm[n] for n in ("Q", "K", "V", "B", "D", "KB", "A", "P"))
        EG, ED, QG, KD = cm["EG"], cm["ED"], cm["QG"], cm["KD"]
        tinv = tinv_ref[...]
        W, U, DO = _st(w_ref[...]), _st(u_ref[...]), _st(do_ref[...])
        s_old = [sst_ref[h * 128:(h + 1) * 128, :] for h in range(_H)]
        ds_new = [dS_ref[h * 128:(h + 1) * 128, :] for h in range(_H)]
        VN = jnp.concatenate([_hs(U, h) - _dot(_hs(W, h), s_old[h], _NN) for h in range(_H)], axis=0)
        dP = jnp.where(cm["incl"], _dot(DO, VN, _NT), 0.0)
        dVN0 = _dot(P, DO, _TN)
        dVN, dQG, dKD, dW, TL = [], [], [], [], []
        for h in range(_H):
            gl = jnp.exp(gc[CHUNK - 1:CHUNK, h:h + 1])
            dvn = _hs(dVN0, h) + _dot(_hs(KD, h), ds_new[h], _NN)
            dkd = _dot(_hs(VN, h), ds_new[h], _NT)
            dVN.append(dvn)
            dQG.append(_dot(_hs(DO, h), s_old[h], _NT))
            dKD.append(dkd)
            dW.append(-_dot(dvn, s_old[h], _NT))
            dgl = jnp.sum(jnp.sum(ds_new[h] * s_old[h], axis=1, keepdims=True), axis=0, keepdims=True)
            tl = jnp.sum(jnp.sum(dkd * _hs(KD, h), axis=1, keepdims=True), axis=0, keepdims=True) + dgl * gl
            TL.append(jnp.broadcast_to(tl, (CHUNK, 1)))
            dS_ref[h * 128:(h + 1) * 128, :] = (_dot(_hs(QG, h), _hs(DO, h), _TN) + gl * ds_new[h]
                                                - _dot(_hs(W, h), dvn, _TN))
        dVN, dQG, dKD, dW, TL = (jnp.concatenate(z, axis=0) for z in (dVN, dQG, dKD, dW, TL))
        dVB = _dot3(tinv, dVN, _TN)
        dKBE = _dot3(tinv, dW, _TN)
        dA = jnp.where(cm["strict"], -(_dot3(dVB, U, _NT) + _dot3(dKBE, W, _NT)), 0.0)
        dG = dA * D
        dQK = dP * D
        dKB = _dot(dG, K, _NN) + dKBE * EG
        dK = _dot(dG, KB, _TN) + _dot(dQK, Q, _TN) + dKB * B + dKD * ED
        dQ = _dot(dQK, K, _NN) + dQG * EG
        Mx = dA * A + dP * P
        rs = lambda z: jnp.sum(z, axis=1, keepdims=True)
        ri = lax.broadcasted_iota(jnp.int32, (_HC, 1), 0)
        dGC = (rs(Mx) - rs(Mx.T) + rs(dKBE * KB * EG) + rs(dQG * QG) - rs(dKD * KD)
               + jnp.where((ri & (CHUNK - 1)) == CHUNK - 1, TL, 0.0))
        dBeta = rs(dVB * V) + rs(dKB * K)
        dqkv_ref[:, 0:512] = _unst(dQ)
        dqkv_ref[:, 512:1024] = _unst(dK)
        dqkv_ref[:, 1024:1536] = _unst(dVB * B)
        dg = jnp.dot(_tri(CHUNK, upper=True), _untile(dGC), precision=HI, preferred_element_type=F32)
        a, alog_v, dtb_v = a_ref[...], alog_ref[...], dtb_ref[...]
        g = _gates(a, alog_v, dtb_v)
        lane = lax.broadcasted_iota(jnp.int32, (CHUNK, 128), 1)
        valid = lane < _H
        da = jnp.where(valid, dg * (-jnp.exp(alog_v)) * jax.nn.sigmoid(a + dtb_v), 0.0)
        dsm_ref[:, 0:128] = da.astype(dsm_ref.dtype)
        dsm_ref[:, 128:256] = jnp.where(valid, _untile(dBeta) * beta * (1.0 - beta), 0.0).astype(dsm_ref.dtype)
        acc_ref[0:1, :] += jnp.sum(jnp.where(valid, dg * g, 0.0), axis=0, keepdims=True)
        acc_ref[1:2, :] += jnp.sum(da, axis=0, keepdims=True)

    row = lambda b, n: b * nC + (nC - 1 - n)
    hd = lambda c: pl.BlockSpec((CHUNK, 512), lambda b, n: (row(b, n), c))
    tile = lambda c: pl.BlockSpec((CHUNK, 128), lambda b, n: (row(b, n), c))
    par = pl.BlockSpec((1, 128), lambda b, n: (0, 0))
    in_specs = [hd(0), hd(1), hd(2), tile(6), par, par, tile(0), tile(0),
                pl.BlockSpec((_HC, _HC), lambda b, n: (row(b, n), 0)),
                pl.BlockSpec((512, 128), lambda b, n: (row(b, n), 0)), hd(0), hd(0), hd(0)]
    out_shape = (_sds((T, 1536)), _sds((T, 1024), BF), _sds((8, 128)))
    out_specs = (pl.BlockSpec((CHUNK, 1536), lambda b, n: (row(b, n), 0)),
                 pl.BlockSpec((CHUNK, 256), lambda b, n: (row(b, n), 3)),
                 pl.BlockSpec((8, 128), lambda b, n: (0, 0)))
    return _pc(body, name="gdn_chunk_bwd", out_shape=out_shape, grid=(Bl, nC), in_specs=in_specs,
               out_specs=out_specs, scratch=[pltpu.VMEM((512, 128), F32)], sem=("arbitrary", "arbitrary"),
               xchg=xchg)(qkvn, qkvn, qkvn, sm, alog, dtb, gc_s, beta_s, tinv_s, sst_s, w_s, u_s, do)


def _gdn_post(o, z, g, T):
    tt = _pick(T, (512, 256))

    def body(o_ref, z_ref, g_ref, out_ref):
        out_ref[...] = (_rmsnorm(o_ref[...], g_ref[...]) * _silu(z_ref[...])).astype(out_ref.dtype)

    blk = pl.BlockSpec((tt, 128), lambda i, h: (i, h))
    return _pc(body, name="gdn_post_fwd", out_shape=_sds((T, 1024), BF), grid=(T // tt, _H),
               in_specs=[blk, blk, pl.BlockSpec((1, 128), lambda i, h: (0, 0))], out_specs=blk,
               sem=("parallel", "parallel"))(o, z, g)


def _gdn_post_bwd(o, z, g, dmixin, T):
    tt = _pick(T, (512, 256))

    def body(o_ref, z_ref, g_ref, d_ref, do_ref, dz_ref, dg_ref):
        @pl.when((pl.program_id(0) == 0) & (pl.program_id(1) == 0))
        def _():
            dg_ref[...] = jnp.zeros_like(dg_ref)

        _, vjp = jax.vjp(lambda a, b, c: _rmsnorm(a, c) * _silu(b), o_ref[...], z_ref[...], g_ref[...])
        do, dz, dg = vjp(d_ref[...])
        do_ref[...] = do
        dz_ref[...] = dz.astype(dz_ref.dtype)
        dg_ref[0:1, :] += dg

    blk = pl.BlockSpec((tt, 128), lambda i, h: (i, h))
    return _pc(body, name="gdn_post_bwd", out_shape=(_sds((T, 512)), _sds((T, 512), BF), _sds((8, 128))),
               grid=(T // tt, _H), in_specs=[blk, blk, pl.BlockSpec((1, 128), lambda i, h: (0, 0)), blk],
               out_specs=(blk, blk, pl.BlockSpec((8, 128), lambda i, h: (0, 0))),
               sem=("arbitrary", "arbitrary"))(o, z, g, dmixin)


def _rms_fwd(sm, g, col_blk, name):
    T = sm.shape[0]
    d = g.shape[1]
    tt = _pick(T, (512, 256))

    def body(x_ref, g_ref, o_ref):
        o_ref[...] = _rmsnorm(x_ref[...], g_ref[...]).astype(o_ref.dtype)

    return _pc(body, name=name, out_shape=_sds((T, d), BF), grid=(T // tt,),
               in_specs=[pl.BlockSpec((tt, d), lambda i: (i, col_blk)), pl.BlockSpec((1, d), lambda i: (0, 0))],
               out_specs=pl.BlockSpec((tt, d), lambda i: (i, 0)), sem=("parallel",))(sm, g)


def _rms_bwd(sm, g, dy, dsm, col_blk, name):
    T = sm.shape[0]
    d = g.shape[1]
    tt = _pick(T, (512, 256))

    def body(x_ref, g_ref, d_ref, alias_ref, dx_ref, dg_ref):
        @pl.when(pl.program_id(0) == 0)
        def _():
            dg_ref[...] = jnp.zeros_like(dg_ref)

        _, vjp = jax.vjp(_rmsnorm, x_ref[...], g_ref[...])
        dx, dg = vjp(d_ref[...])
        dx_ref[...] = dx.astype(dx_ref.dtype)
        dg_ref[0:1, :] += dg

    grp = pl.BlockSpec((tt, d), lambda i: (i, col_blk))
    return _pc(body, name=name, out_shape=(_sds(dsm.shape, dsm.dtype), _sds((8, d))), grid=(T // tt,),
               in_specs=[grp, pl.BlockSpec((1, d), lambda i: (0, 0)), pl.BlockSpec((tt, d), lambda i: (i, 0)), ANY],
               out_specs=(grp, pl.BlockSpec((8, d), lambda i: (0, 0))), aliases={3: 0},
               sem=("arbitrary",))(sm, g, dy, dsm)


def _rope_tables(S):
    inv = ROPE_THETA ** (-jnp.arange(0, MLA_ROPE, 2, dtype=F32) / MLA_ROPE)
    ang = jnp.arange(S, dtype=F32)[:, None] * inv[None, :]
    cos, sin = jnp.cos(ang), jnp.sin(ang)
    z = jnp.zeros((S, 64), F32)
    return jnp.concatenate([cos, cos, z], axis=1), jnp.concatenate([-sin, sin, z], axis=1)


def _swap_halves(x):
    lane = lax.broadcasted_iota(jnp.int32, x.shape, 1)
    return jnp.where(lane < 32, pltpu.roll(x, 96, 1), jnp.where(lane < 64, pltpu.roll(x, 32, 1), 0.0))


def _rope_fwd(qraw, sm, cos, sin, S):
    T = qraw.shape[0]
    tt = _pick(S, (512, 256))
    nps = S // tt

    def body(q_ref, kr_ref, c_ref, s_ref, qo_ref, ko_ref):
        c, s = c_ref[...], s_ref[...]
        q = q_ref[...]
        qo_ref[...] = (q * c + _swap_halves(q) * s).astype(qo_ref.dtype)

        @pl.when(pl.program_id(1) == 0)
        def _():
            k = kr_ref[...]
            ko_ref[...] = (k * c + _swap_halves(k) * s).astype(ko_ref.dtype)

    tab = pl.BlockSpec((tt, 128), lambda i, h: (i % nps, 0))
    return _pc(body, name="rope_fwd", out_shape=(_sds((T, 512), BF), _sds((T, 128), BF)), grid=(T // tt, MLA_HEADS),
               in_specs=[pl.BlockSpec((tt, 128), lambda i, h: (i, 2 * h + 1)),
                         pl.BlockSpec((tt, 128), lambda i, h: (i, 3)), tab, tab],
               out_specs=(pl.BlockSpec((tt, 128), lambda i, h: (i, h)), pl.BlockSpec((tt, 128), lambda i, h: (i, 0))),
               sem=("parallel", "arbitrary"))(qraw, sm, cos, sin)


def _rope_bwd(dqraw, dsm, dqr, dkrr, cos, sin, S):
    T = dqraw.shape[0]
    tt = _pick(S, (512, 256))
    nps = S // tt

    def body(alias_q, alias_s, dq_ref, dk_ref, c_ref, s_ref, qo_ref, ko_ref):
        c, s = c_ref[...], s_ref[...]
        d = dq_ref[...]
        qo_ref[...] = (d * c + _swap_halves(d * s)).astype(qo_ref.dtype)

        @pl.when(pl.program_id(1) == 0)
        def _():
            k = dk_ref[...]
            ko_ref[...] = (k * c + _swap_halves(k * s)).astype(ko_ref.dtype)

    tab = pl.BlockSpec((tt, 128), lambda i, h: (i % nps, 0))
    return _pc(body, name="rope_bwd", out_shape=(_sds(dqraw.shape, dqraw.dtype), _sds(dsm.shape, dsm.dtype)),
               grid=(T // tt, MLA_HEADS),
               in_specs=[ANY, ANY, pl.BlockSpec((tt, 128), lambda i, h: (i, h)),
                         pl.BlockSpec((tt, 128), lambda i, h: (i, 0)), tab, tab],
               out_specs=(pl.BlockSpec((tt, 128), lambda i, h: (i, 2 * h + 1)),
                          pl.BlockSpec((tt, 128), lambda i, h: (i, 3))),
               aliases={0: 0, 1: 1}, sem=("parallel", "arbitrary"))(dqraw, dsm, dqr, dkrr, cos, sin)


def _attn_probs(qn, qr, kn, kr, i, tq):
    scale = (MLA_NOPE + MLA_ROPE) ** -0.5
    s = (_dot(qn, kn, _NT) + _dot(qr, kr, _NT)) * scale
    qpos = i * tq + lax.broadcasted_iota(jnp.int32, s.shape, 0)
    kpos = lax.broadcasted_iota(jnp.int32, s.shape, 1)
    s = jnp.where((kpos >> 6) <= (qpos >> 6), s, -jnp.inf)
    e = jnp.exp(s - jnp.max(s, axis=-1, keepdims=True))
    return e / jnp.sum(e, axis=-1, keepdims=True), scale


def _attn_fwd(mixin, qraw, qr, kv, krr, Bl, S):
    tq = _pick(S, (256, 128))
    nq = S // tq

    def body(alias_ref, qn_ref, qr_ref, kn_ref, v_ref, kr_ref, o_ref):
        p, _ = _attn_probs(qn_ref[...], qr_ref[...], kn_ref[...], kr_ref[...], pl.program_id(2), tq)
        o_ref[...] = _dot(p, v_ref[...], _NN).astype(o_ref.dtype)

    qrow = lambda b, h, i: b * nq + i
    in_specs = [ANY,
                pl.BlockSpec((tq, 128), lambda b, h, i: (qrow(b, h, i), 2 * h)),
                pl.BlockSpec((tq, 128), lambda b, h, i: (qrow(b, h, i), h)),
                pl.BlockSpec((S, 128), lambda b, h, i: (b, 2 * h)),
                pl.BlockSpec((S, 128), lambda b, h, i: (b, 2 * h + 1)),
                pl.BlockSpec((S, 128), lambda b, h, i: (b, 0))]
    return _pc(body, name="mla_attn_fwd", out_shape=_sds(mixin.shape, mixin.dtype), grid=(Bl, MLA_HEADS, nq),
               in_specs=in_specs, out_specs=pl.BlockSpec((tq, 128), lambda b, h, i: (qrow(b, h, i), 4 + h)),
               aliases={0: 0}, sem=("parallel", "parallel", "parallel"))(mixin, qraw, qr, kv, kv, krr)


def _attn_bwd(qraw, qr, kv, krr, dmixin, Bl, S):
    T = Bl * S
    tq = _pick(S, (256, 128))
    nq = S // tq

    def body(qn_ref, qr_ref, kn_ref, v_ref, kr_ref, do_ref, dqn_ref, dqr_ref, dkv_ref, dkr_ref, acc_ref):
        h, i = pl.program_id(1), pl.program_id(2)
        qn, qrv, kn, v, kr, do = (r[...] for r in (qn_ref, qr_ref, kn_ref, v_ref, kr_ref, do_ref))
        p, scale = _attn_probs(qn, qrv, kn, kr, i, tq)
        dp = _dot(do, v, _NT)
        ds = p * (dp - jnp.sum(dp * p, axis=-1, keepdims=True)) * scale
        dqn_ref[...] = _dot(ds, kn, _NN).astype(dqn_ref.dtype)
        dqr_ref[...] = _dot(ds, kr, _NN)
        first = i == 0
        acc_ref[:, 0:128] = jnp.where(first, 0.0, acc_ref[:, 0:128]) + _dot(ds, qn, _TN)
        acc_ref[:, 128:256] = jnp.where(first, 0.0, acc_ref[:, 128:256]) + _dot(p, do, _TN)
        dkr_ref[...] = jnp.where(first & (h == 0), 0.0, dkr_ref[...]) + _dot(ds, qrv, _TN)

        @pl.when(i == nq - 1)
        def _():
            dkv_ref[...] = acc_ref[...].astype(dkv_ref.dtype)

    qrow = lambda b, h, i: b * nq + i
    in_specs = [pl.BlockSpec((tq, 128), lambda b, h, i: (qrow(b, h, i), 2 * h)),
                pl.BlockSpec((tq, 128), lambda b, h, i: (qrow(b, h, i), h)),
                pl.BlockSpec((S, 128), lambda b, h, i: (b, 2 * h)),
                pl.BlockSpec((S, 128), lambda b, h, i: (b, 2 * h + 1)),
                pl.BlockSpec((S, 128), lambda b, h, i: (b, 0)),
                pl.BlockSpec((tq, 128), lambda b, h, i: (qrow(b, h, i), 4 + h))]
    out_shape = (_sds((T, 1024), BF), _sds((T, 512)), _sds((T, 1024), BF), _sds((T, 128)))
    out_specs = (pl.BlockSpec((tq, 128), lambda b, h, i: (qrow(b, h, i), 2 * h)),
                 pl.BlockSpec((tq, 128), lambda b, h, i: (qrow(b, h, i), h)),
                 pl.BlockSpec((S, 256), lambda b, h, i: (b, h)),
                 pl.BlockSpec((S, 128), lambda b, h, i: (b, 0)))
    return _pc(body, name="mla_attn_bwd", out_shape=out_shape, grid=(Bl, MLA_HEADS, nq), in_specs=in_specs,
               out_specs=out_specs, scratch=[pltpu.VMEM((S, 256), F32)],
               sem=("arbitrary", "arbitrary", "arbitrary"))(qraw, qr, kv, kv, krr, dmixin)


def _ln1_fwd(x, mix, g, b):
    T, D = x.shape
    tt = _pick(T, (512, 256))

    def body(x_ref, m_ref, g_ref, b_ref, o_ref, ob_ref):
        h = _layernorm(ALPHA * x_ref[...] + m_ref[...], g_ref[...], b_ref[...])
        o_ref[...] = h
        ob_ref[...] = h.astype(ob_ref.dtype)

    blk = pl.BlockSpec((tt, D), lambda i: (i, 0))
    par = pl.BlockSpec((1, D), lambda i: (0, 0))
    return _pc(body, name="ln1_fwd", out_shape=(_sds((T, D)), _sds((T, D), BF)), grid=(T // tt,),
               in_specs=[blk, blk, par, par], out_specs=(blk, blk), sem=("parallel",))(x, mix, g, b)


def _ln1_bwd(x, mix, g, b, dh):
    T, D = x.shape
    tt = _pick(T, (256,))

    def body(x_ref, m_ref, g_ref, b_ref, d_ref, dx_ref, dm_ref, acc_ref):
        @pl.when(pl.program_id(0) == 0)
        def _():
            acc_ref[...] = jnp.zeros_like(acc_ref)

        f = lambda xx, mm, gg, bb: _layernorm(ALPHA * xx + mm, gg, bb)
        _, vjp = jax.vjp(f, x_ref[...], m_ref[...], g_ref[...], b_ref[...])
        dx, dm, dg, db = vjp(d_ref[...])
        dx_ref[...] = dx
        dm_ref[...] = dm.astype(dm_ref.dtype)
        acc_ref[0:1, :] += dg
        acc_ref[1:2, :] += db

    blk = pl.BlockSpec((tt, D), lambda i: (i, 0))
    par = pl.BlockSpec((1, D), lambda i: (0, 0))
    return _pc(body, name="ln1_bwd", out_shape=(_sds((T, D)), _sds((T, D), BF), _sds((8, D))), grid=(T // tt,),
               in_specs=[blk, blk, par, par, blk], out_specs=(blk, blk, pl.BlockSpec((8, D), lambda i: (0, 0))),
               sem=("arbitrary",))(x, mix, g, b, dh)


def _ffn_act_fwd(ug, uu, wg, wu, bg, bu, Bl, S):
    T, C = ug.shape
    cb = _pick(C, (256, 128))
    kk = wg.shape[0]

    def body(g_ref, u_ref, wg_ref, wu_ref, bg_ref, bu_ref, o_ref):
        cg = _conv(g_ref[...], wg_ref[...]) + bg_ref[...]
        cu = _conv(u_ref[...], wu_ref[...]) + bu_ref[...]
        o_ref[...] = (_silu(cg) * cu).astype(o_ref.dtype)

    blk = pl.BlockSpec((S, cb), lambda b, j: (b, j))
    wblk = pl.BlockSpec((kk, cb), lambda b, j: (0, j))
    bblk = pl.BlockSpec((1, cb), lambda b, j: (0, j))
    return _pc(body, name="ffn_act_fwd", out_shape=_sds((T, C), BF), grid=(Bl, C // cb),
               in_specs=[blk, blk, wblk, wblk, bblk, bblk], out_specs=blk, sem=("parallel", "parallel"))(
        ug, uu, wg, wu, bg, bu)


def _ffn_act_bwd(ug, uu, wg, wu, bg, bu, dact, Bl, S):
    T, C = ug.shape
    cb = _pick(C, (256, 128))
    kk = wg.shape[0]

    def body(g_ref, u_ref, wg_ref, wu_ref, bg_ref, bu_ref, d_ref,
             dg_ref, du_ref, dwg_ref, dwu_ref, dbg_ref, dbu_ref):
        first = pl.program_id(1) == 0
        xg, xu, wgv, wuv = g_ref[...], u_ref[...], wg_ref[...], wu_ref[...]
        cg = _conv(xg, wgv) + bg_ref[...]
        cu = _conv(xu, wuv) + bu_ref[...]
        _, vjp = jax.vjp(lambda a, b: _silu(a) * b, cg, cu)
        dcg, dcu = vjp(d_ref[...].astype(F32))
        dg_ref[...] = _conv_bwd_x(dcg, wgv).astype(dg_ref.dtype)
        du_ref[...] = _conv_bwd_x(dcu, wuv).astype(du_ref.dtype)
        _conv_bwd_w(xg, dcg, dwg_ref, first)
        _conv_bwd_w(xu, dcu, dwu_ref, first)
        dbg_ref[...] = jnp.where(first, 0.0, dbg_ref[...]) + jnp.sum(dcg, axis=0, keepdims=True)
        dbu_ref[...] = jnp.where(first, 0.0, dbu_ref[...]) + jnp.sum(dcu, axis=0, keepdims=True)

    blk = pl.BlockSpec((S, cb), lambda j, b: (b, j))
    wblk = pl.BlockSpec((kk, cb), lambda j, b: (0, j))
    bblk = pl.BlockSpec((1, cb), lambda j, b: (0, j))
    out_shape = (_sds((T, C), BF), _sds((T, C), BF), _sds((kk, C)), _sds((kk, C)), _sds((1, C)), _sds((1, C)))
    return _pc(body, name="ffn_act_bwd", out_shape=out_shape, grid=(C // cb, Bl),
               in_specs=[blk, blk, wblk, wblk, bblk, bblk, blk],
               out_specs=(blk, blk, wblk, wblk, bblk, bblk), sem=("parallel", "arbitrary"))(
        ug, uu, wg, wu, bg, bu, dact)


def _head(h1, ffn, gpre, pp, tgt, bg, g2, b2):
    T, D = h1.shape
    tt = _pick(T, (256,))

    def body(h_ref, f_ref, gp_ref, pp_ref, t_ref, bg_ref, g2_ref, b2_ref, df_ref, dfb_ref, dgp_ref, dpp_ref, acc_ref):
        @pl.when(pl.program_id(0) == 0)
        def _():
            acc_ref[...] = jnp.zeros_like(acc_ref)

        h, tg = h_ref[...], t_ref[...]

        def loss_fn(f, gp, p_, bgv, g2v, b2v):
            pre = ALPHA * h + f + jax.nn.sigmoid(gp + bgv) * p_
            err = _layernorm(pre, g2v, b2v) - tg
            return 0.5 * jnp.sum(jnp.mean(err * err, axis=-1, keepdims=True))

        loss, grads = jax.value_and_grad(loss_fn, argnums=(0, 1, 2, 3, 4, 5))(
            f_ref[...], gp_ref[...], pp_ref[...], bg_ref[...], g2_ref[...], b2_ref[...])
        df_ref[...] = grads[0]
        dfb_ref[...] = grads[0].astype(dfb_ref.dtype)
        dgp_ref[...] = grads[1].astype(dgp_ref.dtype)
        dpp_ref[...] = grads[2].astype(dpp_ref.dtype)
        acc_ref[0:1, :] += grads[3]
        acc_ref[1:2, :] += grads[4]
        acc_ref[2:3, :] += grads[5]
        acc_ref[3:4, :] += jnp.broadcast_to(loss, (1, D))

    blk = pl.BlockSpec((tt, D), lambda i: (i, 0))
    par = pl.BlockSpec((1, D), lambda i: (0, 0))
    o_bf = _sds((T, D), BF)
    return _pc(body, name="loss_head", out_shape=(_sds((T, D)), o_bf, o_bf, o_bf, _sds((8, D))),
               grid=(T // tt,), in_specs=[blk, blk, blk, blk, blk, par, par, par],
               out_specs=(blk, blk, blk, blk, pl.BlockSpec((8, D), lambda i: (0, 0))), sem=("arbitrary",))(
        h1, ffn, gpre, pp, tgt, bg, g2, b2)


def _adamw(parts, w, m, v, name):
    R, C = w.shape
    tr = R if R <= 512 else _pick(R, (256,))

    def body(p_ref, w_ref, m_ref, v_ref, g_ref, d_ref, nm_ref, nv_ref):
        g = p_ref[0].astype(F32)
        for j in range(1, N_DEV):
            g = g + p_ref[j].astype(F32)
        mm = ADAM_B1 * m_ref[...] + (1.0 - ADAM_B1) * g
        vv = ADAM_B2 * v_ref[...] + (1.0 - ADAM_B2) * jnp.square(g)
        m_hat = mm / (1.0 - ADAM_B1 ** ADAM_STEP)
        v_hat = vv / (1.0 - ADAM_B2 ** ADAM_STEP)
        g_ref[...] = g
        d_ref[...] = -ADAM_LR * (m_hat / (jnp.sqrt(v_hat) + ADAM_EPS) + ADAM_WD * w_ref[...])
        nm_ref[...] = mm
        nv_ref[...] = vv

    blk = pl.BlockSpec((tr, C), lambda i: (i, 0))
    o = _sds((R, C))
    return _pc(body, name=name, out_shape=(o, o, o, o), grid=(R // tr,),
               in_specs=[pl.BlockSpec((N_DEV, tr, C), lambda i: (0, i, 0)), blk, blk, blk],
               out_specs=(blk, blk, blk, blk), sem=("parallel",))(parts, w, m, v)


SMALL_ORDER = ("gdn_a_log", "gdn_dt_bias", "gdn_norm_g", "mla_q_norm_g", "mla_kv_norm_g", "ln1_g", "ln1_b",
               "ffn_conv_b", "ple_b_gate", "ln2_g", "ln2_b")
EARLY = ("w_in", "gdn_conv_w")
LATE = ("mla_w_q_up", "mla_w_kv_up", "w_out", "ffn_w_up", "ffn_conv_w", "ffn_w_down", "ple_w_gate", "ple_w_proj")
GRADS_EARLY = ("w_out", "ffn_w_up", "ffn_conv_w", "ffn_w_down", "ple_w_gate", "ple_w_proj")
GRADS_LATE = ("w_in", "gdn_conv_w", "mla_w_q_up", "mla_w_kv_up")


def _pad_lanes(v, n=128):
    return jnp.pad(v, ((0, 0), (0, n - v.shape[1])))


def _local_step(x, p, tgt, W, sp, Bl, S, late_weights=None, early_grads=None):
    T = Bl * S
    W = dict(W)
    w_in = W["w_in"]
    wqkv, wz = w_in[:, :1536], w_in[:, 1536:2048]
    z64 = jnp.zeros((w_in.shape[0], 64), w_in.dtype)
    z124 = jnp.zeros((w_in.shape[0], 124), w_in.dtype)
    wsm = jnp.concatenate([w_in[:, 2056:2440], w_in[:, 2696:2760], z64, w_in[:, 2440:2696],
                           w_in[:, 2048:2052], z124, w_in[:, 2052:2056], z124], axis=1)
    alog, dtb = _pad_lanes(sp["gdn_a_log"]), _pad_lanes(sp["gdn_dt_bias"])
    cos, sin = _rope_tables(S)
    xb, pb = x.astype(BF), p.astype(BF)
    xT, pT = xb.T, pb.T

    qkv = _mm(xb, wqkv, "nn", name="mm_qkv")
    z = _mm(xb, wz, "nn", name="mm_z")
    sm = _mm(xb, wsm, "nn", name="mm_sm")
    qkvn = _gdn_pre_fwd(qkv, W["gdn_conv_w"], Bl, S)
    gdn_out = _gdn_fwd(qkvn, sm, alog, dtb, Bl, S, xchg=None if late_weights is None else late_weights[:2])
    o, tinv_s, sst_s, w_s, u_s, gc_s, beta_s = gdn_out[:7]
    if late_weights is not None:
        W.update(late_weights[2](gdn_out[7:]))
    wq = jnp.pad(W["mla_w_q_up"].reshape(-1, MLA_HEADS, 192), ((0, 0), (0, 0), (0, 64))).reshape(-1, 1024)
    wkv = W["mla_w_kv_up"]
    C = W["ffn_w_down"].shape[0]
    wup_g, wup_u = W["ffn_w_up"][:, :C], W["ffn_w_up"][:, C:]
    cw_g, cw_u = W["ffn_conv_w"][:, :C], W["ffn_conv_w"][:, C:]
    cb_g, cb_u = sp["ffn_conv_b"][:, :C], sp["ffn_conv_b"][:, C:]
    mixin = _gdn_post(o, z, sp["gdn_norm_g"], T)
    cqn = _rms_fwd(sm, sp["mla_q_norm_g"], 0, "rms_q_fwd")
    ckvn = _rms_fwd(sm, sp["mla_kv_norm_g"], 2, "rms_kv_fwd")
    qraw = _mm(cqn, wq, "nn", name="mm_qup")
    kv = _mm(ckvn, wkv, "nn", name="mm_kvup")
    qr, krr = _rope_fwd(qraw, sm, cos, sin, S)
    mixin = _attn_fwd(mixin, qraw, qr, kv, krr, Bl, S)
    mix = _mm(mixin, W["w_out"], "nn", name="mm_out")
    h1, h1b = _ln1_fwd(x, mix, sp["ln1_g"], sp["ln1_b"])
    h1T = h1b.T
    ug = _mm(h1b, wup_g, "nn", name="mm_up_gate")
    uu = _mm(h1b, wup_u, "nn", name="mm_up_up")
    act = _ffn_act_fwd(ug, uu, cw_g, cw_u, cb_g, cb_u, Bl, S)
    ffn = _mm(act, W["ffn_w_down"], "nn", name="mm_down")
    gpre = _mm(h1b, W["ple_w_gate"], "nn", name="mm_gate")
    pp = _mm(pb, W["ple_w_proj"], "nn", name="mm_proj")

    dffn, dffnb, dgpre, dpp, hacc = _head(h1, ffn, gpre, pp, tgt, sp["ple_b_gate"], sp["ln2_g"], sp["ln2_b"])
    loss = hacc[3, 0]
    gW, gs = {}, {}
    gs["ple_b_gate"], gs["ln2_g"], gs["ln2_b"] = hacc[0:1], hacc[1:2], hacc[2:3]
    gW["ple_w_proj"] = _mm(pT, dpp, "nn", name="mm_dproj", out_dtype=BF)
    gW["ple_w_gate"] = _mm(h1T, dgpre, "nn", name="mm_dgate", out_dtype=BF)
    gW["ffn_w_down"] = _mm(act.T, dffnb, "nn", name="mm_ddown", out_dtype=BF)
    dact = _mm(dffnb, W["ffn_w_down"], "nt", name="mm_dact")
    du_g, du_u, dcw_g, dcw_u, dcb_g, dcb_u = _ffn_act_bwd(ug, uu, cw_g, cw_u, cb_g, cb_u, dact, Bl, S)
    gW["ffn_conv_w"] = jnp.concatenate([dcw_g, dcw_u], axis=1)
    gs["ffn_conv_b"] = jnp.concatenate([dcb_g, dcb_u], axis=1)
    gW["ffn_w_up"] = jnp.concatenate([_mm(h1T, du_g, "nn", name="mm_dup_gate", out_dtype=BF),
                                      _mm(h1T, du_u, "nn", name="mm_dup_up", out_dtype=BF)], axis=1)
    dh1 = _mm(dgpre, W["ple_w_gate"], "nt", name="mm_dh1_gate", add=dffn, add_scale=ALPHA)
    dh1 = _mm(du_g, wup_g, "nt", name="mm_dh1_upg", add=dh1)
    dh1 = _mm(du_u, wup_u, "nt", name="mm_dh1_upu", add=dh1)
    dxa, dmix, acc1 = _ln1_bwd(x, mix, sp["ln1_g"], sp["ln1_b"], dh1)
    gs["ln1_g"], gs["ln1_b"] = acc1[0:1], acc1[1:2]
    gW["w_out"] = _mm(mixin.T, dmix, "nn", name="mm_dwout", out_dtype=BF)
    dmixin = _mm(dmix, W["w_out"], "nt", name="mm_dmixin")
    do, dz, gacc = _gdn_post_bwd(o, z, sp["gdn_norm_g"], dmixin, T)
    gs["gdn_norm_g"] = gacc[0:1]
    bwd_out = _gdn_bwd(qkvn, sm, alog, dtb, gc_s, beta_s, tinv_s, sst_s, w_s, u_s, do, Bl, S,
                       xchg=None if early_grads is None else early_grads({n: gW[n] for n in GRADS_EARLY}))
    dqkvn, dsm, cacc = bwd_out[:3]
    early_recv = bwd_out[3:]
    gs["gdn_a_log"], gs["gdn_dt_bias"] = cacc[0:1, :GDN_HEADS], cacc[1:2, :GDN_HEADS]
    dqkv, gW["gdn_conv_w"] = _gdn_pre_bwd(qkv, W["gdn_conv_w"], dqkvn, Bl, S)
    dqraw, dqr, dkv, dkrr = _attn_bwd(qraw, qr, kv, krr, dmixin, Bl, S)
    dqraw, dsm = _rope_bwd(dqraw, dsm, dqr, dkrr, cos, sin, S)
    gwq = _mm(cqn.T, dqraw, "nn", name="mm_dwq", out_dtype=BF)
    gW["mla_w_q_up"] = gwq.reshape(-1, MLA_HEADS, 256)[:, :, :192].reshape(-1, MLA_HEADS * 192)
    gW["mla_w_kv_up"] = _mm(ckvn.T, dkv, "nn", name="mm_dwkv", out_dtype=BF)
    dcqn = _mm(dqraw, wq, "nt", name="mm_dcqn")
    dckvn = _mm(dkv, wkv, "nt", name="mm_dckvn")
    dsm, qacc = _rms_bwd(sm, sp["mla_q_norm_g"], dcqn, dsm, 0, "rms_q_bwd")
    dsm, kacc = _rms_bwd(sm, sp["mla_kv_norm_g"], dckvn, dsm, 2, "rms_kv_bwd")
    gs["mla_q_norm_g"], gs["mla_kv_norm_g"] = qacc[0:1], kacc[0:1]
    gqkv = _mm(xT, dqkv, "nn", name="mm_dwqkv", out_dtype=BF)
    gz = _mm(xT, dz, "nn", name="mm_dwz", out_dtype=BF)
    gsm = _mm(xT, dsm, "nn", name="mm_dwsm", out_dtype=BF)
    gW["w_in"] = jnp.concatenate([gqkv, gz, gsm[:, 768:772], gsm[:, 896:900], gsm[:, 0:384], gsm[:, 512:768],
                                  gsm[:, 384:448]], axis=1)
    dx = _mm(dqkv, wqkv, "nt", name="mm_dx_qkv", add=dxa)
    dx = _mm(dz, wz, "nt", name="mm_dx_z", add=dx)
    dx = _mm(dsm, wsm, "nt", name="mm_dx_sm", add=dx)
    return loss, dx, gW, gs, early_recv


COL_SHARDED = ("w_in", "mla_w_q_up", "mla_w_kv_up", "ffn_w_up", "ple_w_proj", "gdn_conv_w", "ffn_conv_w")
SHARDED = EARLY + LATE
WEIGHTS = ("w_in", "gdn_conv_w", "gdn_a_log", "gdn_dt_bias", "gdn_norm_g", "mla_q_norm_g", "mla_w_q_up",
           "mla_kv_norm_g", "mla_w_kv_up", "w_out", "ln1_g", "ln1_b", "ffn_w_up", "ffn_conv_w", "ffn_conv_b",
           "ffn_w_down", "ple_w_gate", "ple_b_gate", "ple_w_proj", "ln2_g", "ln2_b")
CONV = ("gdn_conv_w", "ffn_conv_w")
SMALL_ROWS, SMALL_LANES = 96, 128


def _gathered_to_full(name, g):
    if name in COL_SHARDED:
        return jnp.transpose(g, (1, 0, 2)).reshape(g.shape[1], -1)
    return g.reshape(-1, g.shape[-1])


def _full_to_blocks(name, gfull, shard_shape):
    r, c = shard_shape
    if name in COL_SHARDED:
        return jnp.transpose(gfull.reshape(r, N_DEV, c), (1, 0, 2))
    return gfull.reshape(N_DEV, r, c)


def _pack_small(d):
    flat = jnp.concatenate([d[n].reshape(-1) for n in SMALL_ORDER])
    return jnp.pad(flat, (0, SMALL_ROWS * SMALL_LANES - flat.shape[0])).reshape(SMALL_ROWS, SMALL_LANES)


def _unpack_small(packed, shapes):
    flat = packed.reshape(-1)
    out, off = {}, 0
    for n in SMALL_ORDER:
        sz = math.prod(shapes[n])
        out[n] = flat[off:off + sz].reshape(shapes[n])
        off += sz
    return out


def kernel(x, p, w_in, gdn_conv_w, gdn_a_log, gdn_dt_bias, gdn_norm_g, mla_q_norm_g, mla_w_q_up, mla_kv_norm_g, mla_w_kv_up, w_out, ln1_g, ln1_b, ffn_w_up, ffn_conv_w, ffn_conv_b, ffn_w_down, ple_w_gate, ple_b_gate, ple_w_proj, ln2_g, ln2_b, loss_target, m_w_in, m_gdn_conv_w, m_gdn_a_log, m_gdn_dt_bias, m_gdn_norm_g, m_mla_q_norm_g, m_mla_w_q_up, m_mla_kv_norm_g, m_mla_w_kv_up, m_w_out, m_ln1_g, m_ln1_b, m_ffn_w_up, m_ffn_conv_w, m_ffn_conv_b, m_ffn_w_down, m_ple_w_gate, m_ple_b_gate, m_ple_w_proj, m_ln2_g, m_ln2_b, v_w_in, v_gdn_conv_w, v_gdn_a_log, v_gdn_dt_bias, v_gdn_norm_g, v_mla_q_norm_g, v_mla_w_q_up, v_mla_kv_norm_g, v_mla_w_kv_up, v_w_out, v_ln1_g, v_ln1_b, v_ffn_w_up, v_ffn_conv_w, v_ffn_conv_b, v_ffn_w_down, v_ple_w_gate, v_ple_b_gate, v_ple_w_proj, v_ln2_g, v_ln2_b):
    loc = dict(locals())
    wts = {n: loc[n] for n in WEIGHTS}
    ms = {n: loc["m_" + n] for n in WEIGHTS}
    vs = {n: loc["v_" + n] for n in WEIGHTS}
    Bl, S, D = x.shape
    T = Bl * S

    wire = lambda n: wts[n][0] if n in CONV else wts[n][0].astype(BF)
    got = _exchange([wire(n) for n in EARLY], ["ag"] * len(EARLY), "ag_weights_early")
    W = {n: _gathered_to_full(n, g) for n, g in zip(EARLY, got)}
    late = ([wire(n) for n in LATE], ["ag"] * len(LATE),
            lambda res: {n: _gathered_to_full(n, g) for n, g in zip(LATE, res)})
    pack = lambda g: ([_full_to_blocks(n, g[n], wts[n].shape[1:]) for n in GRADS_EARLY], ["a2a"] * len(GRADS_EARLY))
    sp = {n: wts[n].reshape(1, -1) for n in SMALL_ORDER}

    loss, dx, gW, gs, early_recv = _local_step(x.reshape(T, D), p.reshape(T, -1), loss_target.reshape(T, D), W, sp,
                                               Bl, S, late_weights=late, early_grads=pack)

    blocks = [_full_to_blocks(n, gW[n], wts[n].shape[1:]) for n in GRADS_LATE]
    late_recv = _exchange(blocks + [_pack_small(gs)], ["a2a"] * len(blocks) + ["ag"], "exchange_grads_late")

    res = {}
    for n, parts in list(zip(GRADS_EARLY, early_recv)) + list(zip(GRADS_LATE, late_recv[:-1])):
        res[n] = tuple(t[None] for t in _adamw(parts, wts[n][0], ms[n][0], vs[n][0], "adamw_" + n))
    small = _adamw(late_recv[-1], _pack_small(wts), _pack_small(ms), _pack_small(vs), "adamw_small")
    shapes = {n: wts[n].shape for n in SMALL_ORDER}
    small = [_unpack_small(t, shapes) for t in small]
    for n in SMALL_ORDER:
        res[n] = tuple(t[n] for t in small)

    loss = lax.psum(loss, ("x", "y", "c"))
    outs = [loss, dx.reshape(Bl, S, D)]
    for k in range(4):
        outs += [res[n][k] for n in WEIGHTS]
    return tuple(outs)
```

```python
import math

import jax
import jax.numpy as jnp
from jax import lax
from jax.experimental import pallas as pl
from jax.experimental.pallas import tpu as pltpu

F32 = jnp.float32
BF = jnp.bfloat16
_MXU = jnp.bfloat16
_INTERPRET = None
_VMEM_LIMIT = 56 * 1024 * 1024
HI = lax.Precision.HIGHEST

N_DEV = 8
CHUNK = 64
GDN_HEADS = 4
GDN_DK = 128
MLA_HEADS = 4
MLA_NOPE = 128
MLA_ROPE = 64
ROPE_THETA = 10000.0
ALPHA = 2.0 ** 0.25
NORM_EPS = 1e-6
ATTN_SCALE = (MLA_NOPE + MLA_ROPE) ** -0.5
ADAM_LR, ADAM_B1, ADAM_B2, ADAM_EPS, ADAM_WD, ADAM_STEP = 0.001, 0.9, 0.999, 1e-08, 0.01, 10
MESH = pl.DeviceIdType.MESH
ANY = pl.BlockSpec(memory_space=pl.ANY)
_NN = (((1,), (0,)), ((), ()))
_NT = (((1,), (1,)), ((), ()))
_TN = (((0,), (0,)), ((), ()))


def _sds(shape, dtype=F32):
    return jax.ShapeDtypeStruct(tuple(shape), dtype)


def _pick(n, cands):
    for c in cands:
        if n % c == 0:
            return c
    return n


def _xchg_out_shapes(arrs, modes):
    return [_sds((N_DEV,) + a.shape if m == "ag" else a.shape, a.dtype) for a, m in zip(arrs, modes)]


def _xchg_scratch(n):
    return [pltpu.SemaphoreType.DMA((n, N_DEV - 1)), pltpu.SemaphoreType.DMA((n, N_DEV - 1)),
            pltpu.SemaphoreType.DMA((n,))]


def _xchg_copies(ins, outs, send, recv, loc, modes):
    x, y, c = lax.axis_index("x"), lax.axis_index("y"), lax.axis_index("c")
    me = 4 * x + 2 * y + c
    copies = []
    for ai, mode in enumerate(modes):
        ag = mode == "ag"
        copies.append(pltpu.make_async_copy(ins[ai] if ag else ins[ai].at[me], outs[ai].at[me], loc.at[ai]))
        for r in range(1, N_DEV):
            px = 1 - x if (r >> 2) & 1 else x
            py = 1 - y if (r >> 1) & 1 else y
            pc = 1 - c if r & 1 else c
            peer = 4 * px + 2 * py + pc
            copies.append(pltpu.make_async_remote_copy(
                src_ref=ins[ai] if ag else ins[ai].at[peer], dst_ref=outs[ai].at[me],
                send_sem=send.at[ai, r - 1], recv_sem=recv.at[ai, r - 1],
                device_id=(px, py, pc), device_id_type=MESH))
    return copies


def _pc(body, *, name, out_shape, grid=None, in_specs=None, out_specs=None, scratch=(), sem=None, aliases=None,
        xchg=None):
    kw = {}
    if _INTERPRET is not None:
        kw["interpret"] = _INTERPRET
    single = not isinstance(out_shape, (tuple, list))
    out_shape = [out_shape] if single else list(out_shape)
    if out_specs is not None:
        out_specs = [out_specs] if single else list(out_specs)
    scratch = list(scratch)
    if xchg is not None:
        xarrs, xmodes = xchg
        n_in, n_out, n_scr, nx = len(in_specs), len(out_shape), len(scratch), len(xarrs)
        single = False
        inner = body

        def body(*refs):
            ins, xins = refs[:n_in], refs[n_in:n_in + nx]
            outs = refs[n_in + nx:n_in + nx + n_out]
            xouts = refs[n_in + nx + n_out:n_in + 2 * nx + n_out]
            scr = refs[n_in + 2 * nx + n_out:n_in + 2 * nx + n_out + n_scr]
            send, recv, loc = refs[n_in + 2 * nx + n_out + n_scr:]
            first = last = None
            for d, g in enumerate(grid):
                f, l = pl.program_id(d) == 0, pl.program_id(d) == g - 1
                first, last = (f, l) if first is None else (first & f, last & l)

            @pl.when(first)
            def _():
                for cp in _xchg_copies(xins, xouts, send, recv, loc, xmodes):
                    cp.start()

            inner(*ins, *outs, *scr)

            @pl.when(last)
            def _():
                for cp in _xchg_copies(xins, xouts, send, recv, loc, xmodes):
                    cp.wait()

        in_specs = list(in_specs) + [ANY] * nx
        out_specs = out_specs + [ANY] * nx
        out_shape = out_shape + _xchg_out_shapes(xarrs, xmodes)
        scratch = scratch + _xchg_scratch(nx)
        sem = ("arbitrary",) * len(grid)
    if grid is not None:
        kw["grid"] = grid
    if in_specs is not None:
        kw["in_specs"] = in_specs
    if out_specs is not None:
        kw["out_specs"] = out_specs[0] if single else tuple(out_specs)
    cp = dict(vmem_limit_bytes=_VMEM_LIMIT)
    if sem is not None:
        cp["dimension_semantics"] = sem
    call = pl.pallas_call(body, name=name, out_shape=out_shape[0] if single else tuple(out_shape),
                          scratch_shapes=scratch, input_output_aliases=aliases or {},
                          compiler_params=pltpu.CompilerParams(**cp), **kw)
    if xchg is None:
        return call
    return lambda *ins: call(*ins, *xchg[0])


def _exchange(arrs, modes, name):
    n = len(arrs)

    def body(*refs):
        copies = _xchg_copies(refs[:n], refs[n:2 * n], *refs[2 * n:], modes)
        for cp in copies:
            cp.start()
        for cp in copies:
            cp.wait()

    return _pc(body, name=name, out_shape=tuple(_xchg_out_shapes(arrs, modes)), in_specs=[ANY] * n,
               out_specs=tuple([ANY] * n), scratch=_xchg_scratch(n))(*arrs)


_FULL_K_MAX = 2816


def _mm(a, b, mode, *, name, add=None, add_scale=1.0, out_dtype=F32):
    (M, K) = a.shape
    (K2, N) = b.shape if mode == "nn" else b.shape[::-1]
    assert K == K2, (a.shape, b.shape, mode)
    tm = _pick(M, (1024, 1408, 512, 384, 256, 128))
    tn = _pick(N, (1024, 1408, 768, 512, 384, 256, 128))
    tk = K if K <= _FULL_K_MAX else _pick(K, (2048, 1408, 1024, 512))
    nk = K // tk
    dims = _NN if mode == "nn" else _NT
    has_add = add is not None

    def finish(r, add_ref, o_ref):
        if has_add:
            r = r + add_scale * add_ref[...].astype(F32)
        o_ref[...] = r.astype(out_dtype)

    if nk == 1:
        def body(a_ref, b_ref, *rest):
            r = lax.dot_general(a_ref[...].astype(_MXU), b_ref[...].astype(_MXU), dims, preferred_element_type=F32)
            finish(r, rest[0] if has_add else None, rest[-1])

        a_spec = pl.BlockSpec((tm, K), lambda i, j: (i, 0))
        b_spec = (pl.BlockSpec((K, tn), lambda i, j: (0, j)) if mode == "nn"
                  else pl.BlockSpec((tn, K), lambda i, j: (j, 0)))
        o_spec = pl.BlockSpec((tm, tn), lambda i, j: (i, j))
        grid, sem, scratch = (M // tm, N // tn), ("parallel", "parallel"), []
    else:
        def body(a_ref, b_ref, *rest):
            o_ref, acc = rest[-2], rest[-1]
            k = pl.program_id(2)

            @pl.when(k == 0)
            def _():
                acc[...] = jnp.zeros_like(acc)

            acc[...] += lax.dot_general(a_ref[...].astype(_MXU), b_ref[...].astype(_MXU), dims,
                                        preferred_element_type=F32)

            @pl.when(k == nk - 1)
            def _():
                finish(acc[...], rest[0] if has_add else None, o_ref)

        a_spec = pl.BlockSpec((tm, tk), lambda i, j, k: (i, k))
        b_spec = (pl.BlockSpec((tk, tn), lambda i, j, k: (k, j)) if mode == "nn"
                  else pl.BlockSpec((tn, tk), lambda i, j, k: (j, k)))
        o_spec = pl.BlockSpec((tm, tn), lambda i, j, k: (i, j))
        grid, sem, scratch = (M // tm, N // tn, nk), ("parallel", "parallel", "arbitrary"), [pltpu.VMEM((tm, tn), F32)]

    ins = [a, b] + ([add] if has_add else [])
    specs = [a_spec, b_spec] + ([o_spec] if has_add else [])
    return _pc(body, name=name, out_shape=_sds((M, N), out_dtype), grid=grid, in_specs=specs, out_specs=o_spec,
               scratch=scratch, sem=sem)(*ins)


def _dot(a, b, dims):
    return lax.dot_general(a.astype(_MXU), b.astype(_MXU), dims, preferred_element_type=F32)


def _split(a):
    hi = a.astype(BF)
    lo = (a - hi.astype(F32)).astype(BF)
    return hi, lo


def _dot3(a, b, dims):
    ah, al = _split(a)
    bh, bl = _split(b)
    d = lambda u, v: lax.dot_general(u, v, dims, preferred_element_type=F32)
    return d(ah, bh) + (d(ah, bl) + d(al, bh))


def _softplus(x):
    return jnp.maximum(x, 0.0) + jnp.log1p(jnp.exp(-jnp.abs(x)))


def _silu(x):
    return x * jax.nn.sigmoid(x)


def _rmsnorm(x, g):
    return x * lax.rsqrt(jnp.mean(x * x, axis=-1, keepdims=True) + NORM_EPS) * g


def _layernorm(x, g, b):
    mu = jnp.mean(x, axis=-1, keepdims=True)
    xc = x - mu
    var = jnp.mean(xc * xc, axis=-1, keepdims=True)
    return xc * lax.rsqrt(var + NORM_EPS) * g + b


def _shift_dn(x, s):
    if s == 0:
        return x
    t = lax.broadcasted_iota(jnp.int32, x.shape, 0)
    return jnp.where(t >= s, pltpu.roll(x, s, 0), 0.0)


def _shift_up(x, s):
    if s == 0:
        return x
    n = x.shape[0]
    t = lax.broadcasted_iota(jnp.int32, x.shape, 0)
    return jnp.where(t < n - s, pltpu.roll(x, n - s, 0), 0.0)


def _conv(x, w):
    kk = w.shape[0]
    y = w[kk - 1:kk, :] * x
    for j in range(kk - 1):
        y = y + w[j:j + 1, :] * _shift_dn(x, kk - 1 - j)
    return y


def _conv_bwd_x(dy, w):
    kk = w.shape[0]
    dx = w[kk - 1:kk, :] * dy
    for j in range(kk - 1):
        dx = dx + w[j:j + 1, :] * _shift_up(dy, kk - 1 - j)
    return dx


def _conv_bwd_w(x, dy, dw_ref, first):
    kk = dw_ref.shape[0]
    for j in range(kk):
        r = jnp.sum(dy * _shift_dn(x, kk - 1 - j), axis=0, keepdims=True)
        prev = jnp.where(first, 0.0, dw_ref[j:j + 1, :])
        dw_ref[j:j + 1, :] = prev + r


def _gdn_post_conv(c, j):
    h = _silu(c)
    hn = h * lax.rsqrt(jnp.sum(h * h, axis=-1, keepdims=True) + NORM_EPS)
    return jnp.where(j < GDN_HEADS, hn * (GDN_DK ** -0.5), jnp.where(j < 2 * GDN_HEADS, hn, h))


def _gdn_pre_fwd(qkv, conv_w, Bl, S):
    T, C = qkv.shape
    nj = C // 128

    def body(x_ref, w_ref, o_ref):
        o_ref[...] = _gdn_post_conv(_conv(x_ref[...], w_ref[...]), pl.program_id(1))

    blk = pl.BlockSpec((S, 128), lambda b, j: (b, j))
    return _pc(body, name="gdn_pre_fwd", out_shape=_sds((T, C)), grid=(Bl, nj),
               in_specs=[blk, pl.BlockSpec((conv_w.shape[0], 128), lambda b, j: (0, j))], out_specs=blk,
               sem=("parallel", "parallel"))(qkv, conv_w)


def _gdn_pre_bwd(qkv, conv_w, dout, Bl, S):
    T, C = qkv.shape
    nj = C // 128
    kk = conv_w.shape[0]

    def body(x_ref, w_ref, d_ref, dx_ref, dw_ref):
        j, b = pl.program_id(0), pl.program_id(1)
        x, w = x_ref[...], w_ref[...]
        c = _conv(x, w)
        _, vjp = jax.vjp(lambda u: _gdn_post_conv(u, j), c)
        (dc,) = vjp(d_ref[...])
        dx_ref[...] = _conv_bwd_x(dc, w).astype(dx_ref.dtype)
        _conv_bwd_w(x, dc, dw_ref, b == 0)

    blk = pl.BlockSpec((S, 128), lambda j, b: (b, j))
    wblk = pl.BlockSpec((kk, 128), lambda j, b: (0, j))
    return _pc(body, name="gdn_pre_bwd", out_shape=(_sds((T, C), BF), _sds((kk, C))), grid=(nj, Bl),
               in_specs=[blk, wblk, blk], out_specs=(blk, wblk), sem=("parallel", "arbitrary"))(qkv, conv_w, dout)


_H = GDN_HEADS
_HC = _H * CHUNK


def _st(x):
    return jnp.concatenate([x[:, h * 128:(h + 1) * 128] for h in range(_H)], axis=0)


def _unst(x):
    return jnp.concatenate([x[h * CHUNK:(h + 1) * CHUNK] for h in range(_H)], axis=1)


def _stc(t):
    return jnp.concatenate([t[:, h:h + 1] for h in range(_H)], axis=0)


def _untile(col):
    lane = lax.broadcasted_iota(jnp.int32, (CHUNK, 128), 1)
    out = jnp.zeros((CHUNK, 128), F32)
    for h in range(_H):
        out = out + jnp.where(lane == h, col[h * CHUNK:(h + 1) * CHUNK], 0.0)
    return out


def _rowform(col):
    return jnp.broadcast_to(col, (_HC, 128)).T[0:1, :]


def _tri(n, upper=False):
    i = lax.broadcasted_iota(jnp.int32, (n, n), 0)
    j = lax.broadcasted_iota(jnp.int32, (n, n), 1)
    return jnp.where((j >= i) if upper else (j <= i), 1.0, 0.0).astype(F32)


def _gdn_chunk_common(q, k, v, gc, beta):
    Q, K, V = _st(q), _st(k), _st(v)
    B, GC = _stc(beta), _stc(gc)
    GL = jnp.concatenate([jnp.broadcast_to(gc[CHUNK - 1:CHUNK, h:h + 1], (CHUNK, 1)) for h in range(_H)], axis=0)
    ii = lax.broadcasted_iota(jnp.int32, (_HC, _HC), 0)
    jj = lax.broadcasted_iota(jnp.int32, (_HC, _HC), 1)
    same = (ii >> 6) == (jj >> 6)
    incl = same & (ii >= jj)
    strict = same & (ii > jj)
    diff = GC - _rowform(GC)
    D = jnp.where(incl, jnp.exp(jnp.where(incl, diff, 0.0)), 0.0)
    KB = K * B
    A = jnp.where(strict, _dot(KB, K, _NT) * D, 0.0)
    P = jnp.where(incl, _dot(Q, K, _NT) * D, 0.0)
    EG = jnp.exp(GC)
    ED = jnp.exp(GL - GC)
    return dict(Q=Q, K=K, V=V, B=B, GC=GC, GL=GL, incl=incl, strict=strict, D=D, KB=KB, A=A, P=P, EG=EG, ED=ED,
                QG=Q * EG, KD=K * ED)


def _neumann_inv(A):
    n = A.shape[0]
    i = lax.broadcasted_iota(jnp.int32, (n, n), 0)
    j = lax.broadcasted_iota(jnp.int32, (n, n), 1)
    N = -A
    acc = jnp.where(i == j, 1.0, 0.0) + N
    Pw = N
    for _ in range(5):
        Pw = _dot3(Pw, Pw, _NN)
        acc = acc + _dot3(acc, Pw, _NN)
    return acc


def _gates(a, alog, dtb):
    return -jnp.exp(alog) * _softplus(a + dtb)


def _hs(x, h, n=CHUNK):
    return x[h * n:(h + 1) * n]


def _per_step(Bl):
    return 2 if Bl % 2 == 0 else 1


def _gdn_fwd(qkvn, sm, alog, dtb, Bl, S, xchg=None):
    T = Bl * S
    nC = S // CHUNK
    E = _per_step(Bl)

    def body(*refs):
        S_ref = refs[-1]

        @pl.when(pl.program_id(1) == 0)
        def _():
            S_ref[...] = jnp.zeros_like(S_ref)

        for e in range(E):
            one(*[r if i in (5, 6) else r.at[e] for i, r in enumerate(refs)])

    def one(q_ref, k_ref, v_ref, a_ref, b_ref, alog_ref, dtb_ref,
            o_ref, tinv_ref, sst_ref, w_ref, u_ref, gc_ref, beta_ref, S_ref):
        g = _gates(a_ref[...], alog_ref[...], dtb_ref[...])
        beta = jax.nn.sigmoid(b_ref[...])
        gc = jnp.dot(_tri(CHUNK), g, precision=HI, preferred_element_type=F32)
        cm = _gdn_chunk_common(q_ref[...], k_ref[...], v_ref[...], gc, beta)
        tinv = _neumann_inv(cm["A"])
        W = _dot3(tinv, cm["KB"] * cm["EG"], _NN)
        U = _dot3(tinv, cm["V"] * cm["B"], _NN)
        s_old = [S_ref[h * 128:(h + 1) * 128, :] for h in range(_H)]
        vn = [_hs(U, h) - _dot(_hs(W, h), s_old[h], _NN) for h in range(_H)]
        o_intra = _dot(cm["P"], jnp.concatenate(vn, axis=0), _NN)
        outs = []
        for h in range(_H):
            outs.append(_dot(_hs(cm["QG"], h), s_old[h], _NN) + _hs(o_intra, h))
            gl = jnp.exp(gc[CHUNK - 1:CHUNK, h:h + 1])
            S_ref[h * 128:(h + 1) * 128, :] = gl * s_old[h] + _dot(_hs(cm["KD"], h), vn[h], _TN)
            sst_ref[h * 128:(h + 1) * 128, :] = s_old[h]
        o_ref[...] = jnp.concatenate(outs, axis=1)
        tinv_ref[...] = tinv
        w_ref[...] = _unst(W)
        u_ref[...] = _unst(U)
        gc_ref[...] = gc
        beta_ref[...] = beta

    blk = lambda r, w, c: pl.BlockSpec((E, r, w), lambda b, n: (b, n, c))
    par = pl.BlockSpec((1, 128), lambda b, n: (0, 0))
    out_shape = (_sds((Bl, S, 512)), _sds((Bl, nC * _HC, _HC)), _sds((Bl, nC * 512, 128)),
                 _sds((Bl, S, 512)), _sds((Bl, S, 512)), _sds((Bl, S, 128)), _sds((Bl, S, 128)))
    out_specs = (blk(CHUNK, 512, 0), blk(_HC, _HC, 0), blk(512, 128, 0), blk(CHUNK, 512, 0), blk(CHUNK, 512, 0),
                 blk(CHUNK, 128, 0), blk(CHUNK, 128, 0))
    q3, s3 = qkvn.reshape(Bl, S, -1), sm.reshape(Bl, S, -1)
    res = _pc(body, name="gdn_chunk_fwd", out_shape=out_shape, grid=(Bl // E, nC),
              in_specs=[blk(CHUNK, 512, 0), blk(CHUNK, 512, 1), blk(CHUNK, 512, 2), blk(CHUNK, 128, 6),
                        blk(CHUNK, 128, 7), par, par],
              out_specs=out_specs, scratch=[pltpu.VMEM((E, 512, 128), F32)], sem=("parallel", "arbitrary"),
              xchg=xchg)(q3, q3, q3, s3, s3, alog, dtb)
    return (res[0].reshape(T, 512),) + tuple(res[1:])


def _gdn_bwd(qkvn, sm, alog, dtb, gc_s, beta_s, tinv_s, sst_s, w_s, u_s, do, Bl, S, xchg=None):
    T = Bl * S
    nC = S // CHUNK
    E = _per_step(Bl)

    def body(*refs):
        acc_ref, dS_ref = refs[-2], refs[-1]
        first_chunk = pl.program_id(1) == 0

        @pl.when(first_chunk)
        def _():
            dS_ref[...] = jnp.zeros_like(dS_ref)

        @pl.when(first_chunk & (pl.program_id(0) == 0))
        def _():
            acc_ref[...] = jnp.zeros_like(acc_ref)

        for e in range(E):
            one(*[r if i in (4, 5, 15) else r.at[e] for i, r in enumerate(refs)])

    def one(q_ref, k_ref, v_ref, a_ref, alog_ref, dtb_ref, gc_ref, beta_ref, tinv_ref, sst_ref, w_ref, u_ref,
            do_ref, dqkv_ref, dsm_ref, acc_ref, dS_ref):
        gc, beta = gc_ref[...], beta_ref[...]
        cm = _gdn_chunk_common(q_ref[...], k_ref[...], v_ref[...], gc, beta)
        Q, K, V, B, D, KB, A, P = (cm[n] for n in ("Q", "K", "V", "B", "D", "KB", "A", "P"))
        EG, ED, QG, KD = cm["EG"], cm["ED"], cm["QG"], cm["KD"]
        tinv = tinv_ref[...]
        W, U, DO = _st(w_ref[...]), _st(u_ref[...]), _st(do_ref[...])
        s_old = [sst_ref[h * 128:(h + 1) * 128, :] for h in range(_H)]
        ds_new = [dS_ref[h * 128:(h + 1) * 128, :] for h in range(_H)]
        VN = jnp.concatenate([_hs(U, h) - _dot(_hs(W, h), s_old[h], _NN) for h in range(_H)], axis=0)
        dP = jnp.where(cm["incl"], _dot(DO, VN, _NT), 0.0)
        dVN0 = _dot(P, DO, _TN)
        dVN, dQG, dKD, dW, TL = [], [], [], [], []
        for h in range(_H):
            gl = jnp.exp(gc[CHUNK - 1:CHUNK, h:h + 1])
            dvn = _hs(dVN0, h) + _dot(_hs(KD, h), ds_new[h], _NN)
            dkd = _dot(_hs(VN, h), ds_new[h], _NT)
            dVN.append(dvn)
            dQG.append(_dot(_hs(DO, h), s_old[h], _NT))
            dKD.append(dkd)
            dW.append(-_dot(dvn, s_old[h], _NT))
            dgl = jnp.sum(jnp.sum(ds_new[h] * s_old[h], axis=1, keepdims=True), axis=0, keepdims=True)
            tl = jnp.sum(jnp.sum(dkd * _hs(KD, h), axis=1, keepdims=True), axis=0, keepdims=True) + dgl * gl
            TL.append(jnp.broadcast_to(tl, (CHUNK, 1)))
            dS_ref[h * 128:(h + 1) * 128, :] = (_dot(_hs(QG, h), _hs(DO, h), _TN) + gl * ds_new[h]
                                                - _dot(_hs(W, h), dvn, _TN))
        dVN, dQG, dKD, dW, TL = (jnp.concatenate(z, axis=0) for z in (dVN, dQG, dKD, dW, TL))
        dVB = _dot3(tinv, dVN, _TN)
        dKBE = _dot3(tinv, dW, _TN)
        dA = jnp.where(cm["strict"], -(_dot3(dVB, U, _NT) + _dot3(dKBE, W, _NT)), 0.0)
        dG = dA * D
        dQK = dP * D
        dKB = _dot(dG, K, _NN) + dKBE * EG
        dK = _dot(dG, KB, _TN) + _dot(dQK, Q, _TN) + dKB * B + dKD * ED
        dQ = _dot(dQK, K, _NN) + dQG * EG
        Mx = dA * A + dP * P
        rs = lambda z: jnp.sum(z, axis=1, keepdims=True)
        ri = lax.broadcasted_iota(jnp.int32, (_HC, 1), 0)
        dGC = (rs(Mx) - rs(Mx.T) + rs(dKBE * KB * EG) + rs(dQG * QG) - rs(dKD * KD)
               + jnp.where((ri & (CHUNK - 1)) == CHUNK - 1, TL, 0.0))
        dBeta = rs(dVB * V) + rs(dKB * K)
        dqkv_ref[:, 0:512] = _unst(dQ)
        dqkv_ref[:, 512:1024] = _unst(dK)
        dqkv_ref[:, 1024:1536] = _unst(dVB * B)
        dg = jnp.dot(_tri(CHUNK, upper=True), _untile(dGC), precision=HI, preferred_element_type=F32)
        a, alog_v, dtb_v = a_ref[...], alog_ref[...], dtb_ref[...]
        g = _gates(a, alog_v, dtb_v)
        lane = lax.broadcasted_iota(jnp.int32, (CHUNK, 128), 1)
        valid = lane < _H
        da = jnp.where(valid, dg * (-jnp.exp(alog_v)) * jax.nn.sigmoid(a + dtb_v), 0.0)
        dsm_ref[:, 0:128] = da.astype(dsm_ref.dtype)
        dsm_ref[:, 128:256] = jnp.where(valid, _untile(dBeta) * beta * (1.0 - beta), 0.0).astype(dsm_ref.dtype)
        acc_ref[0:1, :] += jnp.sum(jnp.where(valid, dg * g, 0.0), axis=0, keepdims=True)
        acc_ref[1:2, :] += jnp.sum(da, axis=0, keepdims=True)

    blk = lambda r, w, c: pl.BlockSpec((E, r, w), lambda b, n: (b, nC - 1 - n, c))
    par = pl.BlockSpec((1, 128), lambda b, n: (0, 0))
    in_specs = [blk(CHUNK, 512, 0), blk(CHUNK, 512, 1), blk(CHUNK, 512, 2), blk(CHUNK, 128, 6), par, par,
                blk(CHUNK, 128, 0), blk(CHUNK, 128, 0), blk(_HC, _HC, 0), blk(512, 128, 0),
                blk(CHUNK, 512, 0), blk(CHUNK, 512, 0), blk(CHUNK, 512, 0)]
    out_shape = (_sds((Bl, S, 1536)), _sds((Bl, S, 1024), BF), _sds((8, 128)))
    out_specs = (blk(CHUNK, 1536, 0), blk(CHUNK, 256, 3), pl.BlockSpec((8, 128), lambda b, n: (0, 0)))
    q3, s3 = qkvn.reshape(Bl, S, -1), sm.reshape(Bl, S, -1)
    res = _pc(body, name="gdn_chunk_bwd", out_shape=out_shape, grid=(Bl // E, nC), in_specs=in_specs,
              out_specs=out_specs, scratch=[pltpu.VMEM((E, 512, 128), F32)], sem=("arbitrary", "arbitrary"),
              xchg=xchg)(q3, q3, q3, s3, alog, dtb, gc_s, beta_s, tinv_s, sst_s, w_s, u_s, do.reshape(Bl, S, -1))
    return (res[0].reshape(T, 1536), res[1].reshape(T, 1024)) + tuple(res[2:])


def _gdn_post(o, z, g, T):
    tt = _pick(T, (2048, 1024, 512, 256))

    def body(o_ref, z_ref, g_ref, out_ref):
        out_ref[...] = (_rmsnorm(o_ref[...], g_ref[...]) * _silu(z_ref[...])).astype(out_ref.dtype)

    blk = pl.BlockSpec((tt, 128), lambda i, h: (i, h))
    return _pc(body, name="gdn_post_fwd", out_shape=_sds((T, 1024), BF), grid=(T // tt, _H),
               in_specs=[blk, blk, pl.BlockSpec((1, 128), lambda i, h: (0, 0))], out_specs=blk,
               sem=("parallel", "parallel"))(o, z, g)


def _gdn_post_bwd(o, z, g, dmixin, T):
    tt = _pick(T, (2048, 1024, 512, 256))

    def body(o_ref, z_ref, g_ref, d_ref, do_ref, dz_ref, dg_ref):
        @pl.when((pl.program_id(0) == 0) & (pl.program_id(1) == 0))
        def _():
            dg_ref[...] = jnp.zeros_like(dg_ref)

        _, vjp = jax.vjp(lambda a, b, c: _rmsnorm(a, c) * _silu(b), o_ref[...], z_ref[...], g_ref[...])
        do, dz, dg = vjp(d_ref[...])
        do_ref[...] = do
        dz_ref[...] = dz.astype(dz_ref.dtype)
        dg_ref[0:1, :] += dg

    blk = pl.BlockSpec((tt, 128), lambda i, h: (i, h))
    return _pc(body, name="gdn_post_bwd", out_shape=(_sds((T, 512)), _sds((T, 512), BF), _sds((8, 128))),
               grid=(T // tt, _H), in_specs=[blk, blk, pl.BlockSpec((1, 128), lambda i, h: (0, 0)), blk],
               out_specs=(blk, blk, pl.BlockSpec((8, 128), lambda i, h: (0, 0))),
               sem=("arbitrary", "arbitrary"))(o, z, g, dmixin)


def _rms_fwd(sm, g, col_blk, name):
    T = sm.shape[0]
    d = g.shape[1]
    tt = _pick(T, (1024, 512, 256))

    def body(x_ref, g_ref, o_ref):
        o_ref[...] = _rmsnorm(x_ref[...], g_ref[...]).astype(o_ref.dtype)

    return _pc(body, name=name, out_shape=_sds((T, d), BF), grid=(T // tt,),
               in_specs=[pl.BlockSpec((tt, d), lambda i: (i, col_blk)), pl.BlockSpec((1, d), lambda i: (0, 0))],
               out_specs=pl.BlockSpec((tt, d), lambda i: (i, 0)), sem=("parallel",))(sm, g)


def _rms_bwd(sm, g, dy, dsm, col_blk, name):
    T = sm.shape[0]
    d = g.shape[1]
    tt = _pick(T, (1024, 512, 256))

    def body(x_ref, g_ref, d_ref, alias_ref, dx_ref, dg_ref):
        @pl.when(pl.program_id(0) == 0)
        def _():
            dg_ref[...] = jnp.zeros_like(dg_ref)

        _, vjp = jax.vjp(_rmsnorm, x_ref[...], g_ref[...])
        dx, dg = vjp(d_ref[...])
        dx_ref[...] = dx.astype(dx_ref.dtype)
        dg_ref[0:1, :] += dg

    grp = pl.BlockSpec((tt, d), lambda i: (i, col_blk))
    return _pc(body, name=name, out_shape=(_sds(dsm.shape, dsm.dtype), _sds((8, d))), grid=(T // tt,),
               in_specs=[grp, pl.BlockSpec((1, d), lambda i: (0, 0)), pl.BlockSpec((tt, d), lambda i: (i, 0)), ANY],
               out_specs=(grp, pl.BlockSpec((8, d), lambda i: (0, 0))), aliases={3: 0},
               sem=("arbitrary",))(sm, g, dy, dsm)


def _rope_tables(S):
    inv = ROPE_THETA ** (-jnp.arange(0, MLA_ROPE, 2, dtype=F32) / MLA_ROPE)
    ang = jnp.arange(S, dtype=F32)[:, None] * inv[None, :]
    cos, sin = jnp.cos(ang), jnp.sin(ang)
    z = jnp.zeros((S, 64), F32)
    return jnp.concatenate([cos, cos, z], axis=1), jnp.concatenate([-sin, sin, z], axis=1)


def _swap_halves(x):
    lane = lax.broadcasted_iota(jnp.int32, x.shape, 1)
    return jnp.where(lane < 32, pltpu.roll(x, 96, 1), jnp.where(lane < 64, pltpu.roll(x, 32, 1), 0.0))


def _rope_fwd(qraw, sm, cos, sin, S):
    T = qraw.shape[0]
    tt = _pick(S, (2048, 1024, 512, 256))
    nps = S // tt

    def body(q_ref, kr_ref, c_ref, s_ref, qo_ref, ko_ref):
        c, s = c_ref[...], s_ref[...]
        q = q_ref[...]
        qo_ref[...] = ((q * c + _swap_halves(q) * s) * ATTN_SCALE).astype(qo_ref.dtype)

        @pl.when(pl.program_id(1) == 0)
        def _():
            k = kr_ref[...]
            ko_ref[...] = (k * c + _swap_halves(k) * s).astype(ko_ref.dtype)

    tab = pl.BlockSpec((tt, 128), lambda i, h: (i % nps, 0))
    return _pc(body, name="rope_fwd", out_shape=(_sds((T, 512), BF), _sds((T, 128), BF)), grid=(T // tt, MLA_HEADS),
               in_specs=[pl.BlockSpec((tt, 128), lambda i, h: (i, 2 * h + 1)),
                         pl.BlockSpec((tt, 128), lambda i, h: (i, 3)), tab, tab],
               out_specs=(pl.BlockSpec((tt, 128), lambda i, h: (i, h)), pl.BlockSpec((tt, 128), lambda i, h: (i, 0))),
               sem=("parallel", "arbitrary"))(qraw, sm, cos, sin)


def _rope_bwd(dqraw, dsm, dqr, dkrr, cos, sin, S):
    T = dqraw.shape[0]
    tt = _pick(S, (2048, 1024, 512, 256))
    nps = S // tt

    def body(alias_q, alias_s, dq_ref, dk_ref, c_ref, s_ref, qo_ref, ko_ref):
        c, s = c_ref[...], s_ref[...]
        d = dq_ref[...]
        qo_ref[...] = (d * c + _swap_halves(d * s)).astype(qo_ref.dtype)

        @pl.when(pl.program_id(1) == 0)
        def _():
            k = dk_ref[...]
            ko_ref[...] = (k * c + _swap_halves(k * s)).astype(ko_ref.dtype)

    tab = pl.BlockSpec((tt, 128), lambda i, h: (i % nps, 0))
    return _pc(body, name="rope_bwd", out_shape=(_sds(dqraw.shape, dqraw.dtype), _sds(dsm.shape, dsm.dtype)),
               grid=(T // tt, MLA_HEADS),
               in_specs=[ANY, ANY, pl.BlockSpec((tt, 128), lambda i, h: (i, h)),
                         pl.BlockSpec((tt, 128), lambda i, h: (i, 0)), tab, tab],
               out_specs=(pl.BlockSpec((tt, 128), lambda i, h: (i, 2 * h + 1)),
                          pl.BlockSpec((tt, 128), lambda i, h: (i, 3))),
               aliases={0: 0, 1: 1}, sem=("parallel", "arbitrary"))(dqraw, dsm, dqr, dkrr, cos, sin)


def _attn_scores(qn, qr, kn_ref, kr_ref, L, tq):
    def sc(lo, hi):
        return _dot(qn, kn_ref[lo:hi, :], _NT) + _dot(qr, kr_ref[lo:hi, :], _NT)

    sd = sc(L - tq, L)
    qc = lax.broadcasted_iota(jnp.int32, sd.shape, 0) >> 6
    kc = lax.broadcasted_iota(jnp.int32, sd.shape, 1) >> 6
    sd = jnp.where(kc <= qc, sd, -jnp.inf)
    return sd if L == tq else jnp.concatenate([sc(0, L - tq), sd], axis=1)


def _attn_fwd(mixin, qraw, qr, kv, krr, Bl, S):
    T = Bl * S
    tq = _pick(S, (256, 128))
    nq = S // tq

    def body(alias_ref, qn_ref, qr_ref, kn_ref, v_ref, kr_ref, o_ref, lse_ref):
        i = pl.program_id(2)
        qn = (qn_ref[...] * ATTN_SCALE).astype(BF)
        qrv = qr_ref[...]
        for ii in range(nq):
            @pl.when(i == ii)
            def _(L=(ii + 1) * tq):
                s = _attn_scores(qn, qrv, kn_ref, kr_ref, L, tq)
                m = jnp.max(s, axis=-1, keepdims=True)
                e = jnp.exp(s - m)
                l = jnp.sum(e, axis=-1, keepdims=True)
                o_ref[...] = (_dot(e, v_ref[0:L, :], _NN) / l).astype(o_ref.dtype)
                lse_ref[...] = jnp.broadcast_to(m + jnp.log(l), lse_ref.shape)

    qrow = lambda b, h, i: b * nq + i
    in_specs = [ANY,
                pl.BlockSpec((tq, 128), lambda b, h, i: (qrow(b, h, i), 2 * h)),
                pl.BlockSpec((tq, 128), lambda b, h, i: (qrow(b, h, i), h)),
                pl.BlockSpec((S, 128), lambda b, h, i: (b, 2 * h)),
                pl.BlockSpec((S, 128), lambda b, h, i: (b, 2 * h + 1)),
                pl.BlockSpec((S, 128), lambda b, h, i: (b, 0))]
    return _pc(body, name="mla_attn_fwd", out_shape=(_sds(mixin.shape, mixin.dtype), _sds((T, 512))),
               grid=(Bl, MLA_HEADS, nq), in_specs=in_specs,
               out_specs=(pl.BlockSpec((tq, 128), lambda b, h, i: (qrow(b, h, i), 4 + h)),
                          pl.BlockSpec((tq, 128), lambda b, h, i: (qrow(b, h, i), h))),
               aliases={0: 0}, sem=("parallel", "parallel", "parallel"))(mixin, qraw, qr, kv, kv, krr)


def _attn_bwd(qraw, qr, kv, krr, lse, dmixin, Bl, S):
    T = Bl * S
    tq = _pick(S, (256, 128))
    nq = S // tq

    def body(qn_ref, qr_ref, kn_ref, v_ref, kr_ref, do_ref, lse_ref, dqn_ref, dqr_ref, dkv_ref, dkr_ref, acc_ref):
        h, i = pl.program_id(1), pl.program_id(2)
        qn = (qn_ref[...] * ATTN_SCALE).astype(BF)
        qrv = qr_ref[...]
        do32 = do_ref[...]
        do = do32.astype(BF)
        lse_col = lse_ref[:, 0:1]

        @pl.when(i == 0)
        def _():
            acc_ref[...] = jnp.zeros_like(acc_ref)

        @pl.when((i == 0) & (h == 0))
        def _():
            dkr_ref[...] = jnp.zeros_like(dkr_ref)

        for ii in range(nq):
            @pl.when(i == ii)
            def _(L=(ii + 1) * tq):
                p = jnp.exp(_attn_scores(qn, qrv, kn_ref, kr_ref, L, tq) - lse_col)
                pb = p.astype(BF)
                v = v_ref[0:L, :]
                delta = jnp.sum(do32 * _dot(pb, v, _NN), axis=-1, keepdims=True)
                ds = (p * (_dot(do, v, _NT) - delta)).astype(BF)
                dqn_ref[...] = (_dot(ds, kn_ref[0:L, :], _NN) * ATTN_SCALE).astype(dqn_ref.dtype)
                dqr_ref[...] = _dot(ds, kr_ref[0:L, :], _NN) * ATTN_SCALE
                acc_ref[0:L, 0:128] += _dot(ds, qn, _TN)
                acc_ref[0:L, 128:256] += _dot(pb, do, _TN)
                dkr_ref[0:L, :] += _dot(ds, qrv, _TN)

        @pl.when(i == nq - 1)
        def _():
            dkv_ref[...] = acc_ref[...].astype(dkv_ref.dtype)

    qrow = lambda b, h, i: b * nq + i
    in_specs = [pl.BlockSpec((tq, 128), lambda b, h, i: (qrow(b, h, i), 2 * h)),
                pl.BlockSpec((tq, 128), lambda b, h, i: (qrow(b, h, i), h)),
                pl.BlockSpec((S, 128), lambda b, h, i: (b, 2 * h)),
                pl.BlockSpec((S, 128), lambda b, h, i: (b, 2 * h + 1)),
                pl.BlockSpec((S, 128), lambda b, h, i: (b, 0)),
                pl.BlockSpec((tq, 128), lambda b, h, i: (qrow(b, h, i), 4 + h)),
                pl.BlockSpec((tq, 128), lambda b, h, i: (qrow(b, h, i), h))]
    out_shape = (_sds((T, 1024), BF), _sds((T, 512)), _sds((T, 1024), BF), _sds((T, 128)))
    out_specs = (pl.BlockSpec((tq, 128), lambda b, h, i: (qrow(b, h, i), 2 * h)),
                 pl.BlockSpec((tq, 128), lambda b, h, i: (qrow(b, h, i), h)),
                 pl.BlockSpec((S, 256), lambda b, h, i: (b, h)),
                 pl.BlockSpec((S, 128), lambda b, h, i: (b, 0)))
    return _pc(body, name="mla_attn_bwd", out_shape=out_shape, grid=(Bl, MLA_HEADS, nq), in_specs=in_specs,
               out_specs=out_specs, scratch=[pltpu.VMEM((S, 256), F32)],
               sem=("arbitrary", "arbitrary", "arbitrary"))(qraw, qr, kv, kv, krr, dmixin, lse)


def _ln1_fwd(x, mix, g, b):
    T, D = x.shape
    tt = _pick(T, (512, 256))

    def body(x_ref, m_ref, g_ref, b_ref, o_ref, ob_ref):
        h = _layernorm(ALPHA * x_ref[...] + m_ref[...], g_ref[...], b_ref[...])
        o_ref[...] = h
        ob_ref[...] = h.astype(ob_ref.dtype)

    blk = pl.BlockSpec((tt, D), lambda i: (i, 0))
    par = pl.BlockSpec((1, D), lambda i: (0, 0))
    return _pc(body, name="ln1_fwd", out_shape=(_sds((T, D)), _sds((T, D), BF)), grid=(T // tt,),
               in_specs=[blk, blk, par, par], out_specs=(blk, blk), sem=("parallel",))(x, mix, g, b)


def _ln1_bwd(x, mix, g, b, dh):
    T, D = x.shape
    tt = _pick(T, (256,))

    def body(x_ref, m_ref, g_ref, b_ref, d_ref, dx_ref, dm_ref, acc_ref):
        @pl.when(pl.program_id(0) == 0)
        def _():
            acc_ref[...] = jnp.zeros_like(acc_ref)

        f = lambda xx, mm, gg, bb: _layernorm(ALPHA * xx + mm, gg, bb)
        _, vjp = jax.vjp(f, x_ref[...], m_ref[...], g_ref[...], b_ref[...])
        dx, dm, dg, db = vjp(d_ref[...])
        dx_ref[...] = dx
        dm_ref[...] = dm.astype(dm_ref.dtype)
        acc_ref[0:1, :] += dg
        acc_ref[1:2, :] += db

    blk = pl.BlockSpec((tt, D), lambda i: (i, 0))
    par = pl.BlockSpec((1, D), lambda i: (0, 0))
    return _pc(body, name="ln1_bwd", out_shape=(_sds((T, D)), _sds((T, D), BF), _sds((8, D))), grid=(T // tt,),
               in_specs=[blk, blk, par, par, blk], out_specs=(blk, blk, pl.BlockSpec((8, D), lambda i: (0, 0))),
               sem=("arbitrary",))(x, mix, g, b, dh)


def _ffn_act_fwd(ug, uu, wg, wu, bg, bu, Bl, S):
    T, C = ug.shape
    cb = _pick(C, (256, 128))
    kk = wg.shape[0]

    def body(g_ref, u_ref, wg_ref, wu_ref, bg_ref, bu_ref, o_ref):
        cg = _conv(g_ref[...], wg_ref[...]) + bg_ref[...]
        cu = _conv(u_ref[...], wu_ref[...]) + bu_ref[...]
        o_ref[...] = (_silu(cg) * cu).astype(o_ref.dtype)

    blk = pl.BlockSpec((S, cb), lambda b, j: (b, j))
    wblk = pl.BlockSpec((kk, cb), lambda b, j: (0, j))
    bblk = pl.BlockSpec((1, cb), lambda b, j: (0, j))
    return _pc(body, name="ffn_act_fwd", out_shape=_sds((T, C), BF), grid=(Bl, C // cb),
               in_specs=[blk, blk, wblk, wblk, bblk, bblk], out_specs=blk, sem=("parallel", "parallel"))(
        ug, uu, wg, wu, bg, bu)


def _ffn_act_bwd(ug, uu, wg, wu, bg, bu, dact, Bl, S):
    T, C = ug.shape
    cb = _pick(C, (256, 128))
    kk = wg.shape[0]

    def body(g_ref, u_ref, wg_ref, wu_ref, bg_ref, bu_ref, d_ref,
             dg_ref, du_ref, dwg_ref, dwu_ref, dbg_ref, dbu_ref):
        first = pl.program_id(1) == 0
        xg, xu, wgv, wuv = g_ref[...], u_ref[...], wg_ref[...], wu_ref[...]
        cg = _conv(xg, wgv) + bg_ref[...]
        cu = _conv(xu, wuv) + bu_ref[...]
        _, vjp = jax.vjp(lambda a, b: _silu(a) * b, cg, cu)
        dcg, dcu = vjp(d_ref[...].astype(F32))
        dg_ref[...] = _conv_bwd_x(dcg, wgv).astype(dg_ref.dtype)
        du_ref[...] = _conv_bwd_x(dcu, wuv).astype(du_ref.dtype)
        _conv_bwd_w(xg, dcg, dwg_ref, first)
        _conv_bwd_w(xu, dcu, dwu_ref, first)
        dbg_ref[...] = jnp.where(first, 0.0, dbg_ref[...]) + jnp.sum(dcg, axis=0, keepdims=True)
        dbu_ref[...] = jnp.where(first, 0.0, dbu_ref[...]) + jnp.sum(dcu, axis=0, keepdims=True)

    blk = pl.BlockSpec((S, cb), lambda j, b: (b, j))
    wblk = pl.BlockSpec((kk, cb), lambda j, b: (0, j))
    bblk = pl.BlockSpec((1, cb), lambda j, b: (0, j))
    out_shape = (_sds((T, C), BF), _sds((T, C), BF), _sds((kk, C)), _sds((kk, C)), _sds((1, C)), _sds((1, C)))
    return _pc(body, name="ffn_act_bwd", out_shape=out_shape, grid=(C // cb, Bl),
               in_specs=[blk, blk, wblk, wblk, bblk, bblk, blk],
               out_specs=(blk, blk, wblk, wblk, bblk, bblk), sem=("parallel", "arbitrary"))(
        ug, uu, wg, wu, bg, bu, dact)


def _head(h1, ffn, gpre, pp, tgt, bg, g2, b2):
    T, D = h1.shape
    tt = _pick(T, (256,))

    def body(h_ref, f_ref, gp_ref, pp_ref, t_ref, bg_ref, g2_ref, b2_ref, df_ref, dfb_ref, dgp_ref, dpp_ref, acc_ref):
        @pl.when(pl.program_id(0) == 0)
        def _():
            acc_ref[...] = jnp.zeros_like(acc_ref)

        h, tg = h_ref[...], t_ref[...]

        def loss_fn(f, gp, p_, bgv, g2v, b2v):
            pre = ALPHA * h + f + jax.nn.sigmoid(gp + bgv) * p_
            err = _layernorm(pre, g2v, b2v) - tg
            return 0.5 * jnp.sum(jnp.mean(err * err, axis=-1, keepdims=True))

        loss, grads = jax.value_and_grad(loss_fn, argnums=(0, 1, 2, 3, 4, 5))(
            f_ref[...], gp_ref[...], pp_ref[...], bg_ref[...], g2_ref[...], b2_ref[...])
        df_ref[...] = grads[0]
        dfb_ref[...] = grads[0].astype(dfb_ref.dtype)
        dgp_ref[...] = grads[1].astype(dgp_ref.dtype)
        dpp_ref[...] = grads[2].astype(dpp_ref.dtype)
        acc_ref[0:1, :] += grads[3]
        acc_ref[1:2, :] += grads[4]
        acc_ref[2:3, :] += grads[5]
        acc_ref[3:4, :] += jnp.broadcast_to(loss, (1, D))

    blk = pl.BlockSpec((tt, D), lambda i: (i, 0))
    par = pl.BlockSpec((1, D), lambda i: (0, 0))
    o_bf = _sds((T, D), BF)
    return _pc(body, name="loss_head", out_shape=(_sds((T, D)), o_bf, o_bf, o_bf, _sds((8, D))),
               grid=(T // tt,), in_specs=[blk, blk, blk, blk, blk, par, par, par],
               out_specs=(blk, blk, blk, blk, pl.BlockSpec((8, D), lambda i: (0, 0))), sem=("arbitrary",))(
        h1, ffn, gpre, pp, tgt, bg, g2, b2)


def _adamw(parts, w, m, v, name):
    R, C = w.shape
    tr = R if R <= 512 else _pick(R, (256,))

    def body(p_ref, w_ref, m_ref, v_ref, g_ref, d_ref, nm_ref, nv_ref):
        g = p_ref[0].astype(F32)
        for j in range(1, N_DEV):
            g = g + p_ref[j].astype(F32)
        mm = ADAM_B1 * m_ref[...] + (1.0 - ADAM_B1) * g
        vv = ADAM_B2 * v_ref[...] + (1.0 - ADAM_B2) * jnp.square(g)
        m_hat = mm / (1.0 - ADAM_B1 ** ADAM_STEP)
        v_hat = vv / (1.0 - ADAM_B2 ** ADAM_STEP)
        g_ref[...] = g
        d_ref[...] = -ADAM_LR * (m_hat / (jnp.sqrt(v_hat) + ADAM_EPS) + ADAM_WD * w_ref[...])
        nm_ref[...] = mm
        nv_ref[...] = vv

    blk = pl.BlockSpec((tr, C), lambda i: (i, 0))
    o = _sds((R, C))
    return _pc(body, name=name, out_shape=(o, o, o, o), grid=(R // tr,),
               in_specs=[pl.BlockSpec((N_DEV, tr, C), lambda i: (0, i, 0)), blk, blk, blk],
               out_specs=(blk, blk, blk, blk), sem=("parallel",))(parts, w, m, v)


SMALL_ORDER = ("gdn_a_log", "gdn_dt_bias", "gdn_norm_g", "mla_q_norm_g", "mla_kv_norm_g", "ln1_g", "ln1_b",
               "ffn_conv_b", "ple_b_gate", "ln2_g", "ln2_b")
EARLY = ("w_in", "gdn_conv_w")
LATE = ("mla_w_q_up", "mla_w_kv_up", "w_out", "ffn_w_up", "ffn_conv_w", "ffn_w_down", "ple_w_gate", "ple_w_proj")
GRADS_EARLY = ("w_out", "ffn_w_up", "ffn_conv_w", "ffn_w_down", "ple_w_gate", "ple_w_proj")
GRADS_LATE = ("w_in", "gdn_conv_w", "mla_w_q_up", "mla_w_kv_up")


def _pad_lanes(v, n=128):
    return jnp.pad(v, ((0, 0), (0, n - v.shape[1])))


def _local_step(x, p, tgt, W, sp, Bl, S, late_weights=None, early_grads=None):
    T = Bl * S
    W = dict(W)
    w_in = W["w_in"]
    wqkv, wz = w_in[:, :1536], w_in[:, 1536:2048]
    z64 = jnp.zeros((w_in.shape[0], 64), w_in.dtype)
    z124 = jnp.zeros((w_in.shape[0], 124), w_in.dtype)
    wsm = jnp.concatenate([w_in[:, 2056:2440], w_in[:, 2696:2760], z64, w_in[:, 2440:2696],
                           w_in[:, 2048:2052], z124, w_in[:, 2052:2056], z124], axis=1)
    alog, dtb = _pad_lanes(sp["gdn_a_log"]), _pad_lanes(sp["gdn_dt_bias"])
    cos, sin = _rope_tables(S)
    xb, pb = x.astype(BF), p.astype(BF)
    xT, pT = xb.T, pb.T

    qkv = _mm(xb, wqkv, "nn", name="mm_qkv")
    z = _mm(xb, wz, "nn", name="mm_z")
    sm = _mm(xb, wsm, "nn", name="mm_sm")
    qkvn = _gdn_pre_fwd(qkv, W["gdn_conv_w"], Bl, S)
    gdn_out = _gdn_fwd(qkvn, sm, alog, dtb, Bl, S, xchg=None if late_weights is None else late_weights[:2])
    o, tinv_s, sst_s, w_s, u_s, gc_s, beta_s = gdn_out[:7]
    if late_weights is not None:
        W.update(late_weights[2](gdn_out[7:]))
    wq = jnp.pad(W["mla_w_q_up"].reshape(-1, MLA_HEADS, 192), ((0, 0), (0, 0), (0, 64))).reshape(-1, 1024)
    wkv = W["mla_w_kv_up"]
    C = W["ffn_w_down"].shape[0]
    wup_g, wup_u = W["ffn_w_up"][:, :C], W["ffn_w_up"][:, C:]
    cw_g, cw_u = W["ffn_conv_w"][:, :C], W["ffn_conv_w"][:, C:]
    cb_g, cb_u = sp["ffn_conv_b"][:, :C], sp["ffn_conv_b"][:, C:]
    mixin = _gdn_post(o, z, sp["gdn_norm_g"], T)
    cqn = _rms_fwd(sm, sp["mla_q_norm_g"], 0, "rms_q_fwd")
    ckvn = _rms_fwd(sm, sp["mla_kv_norm_g"], 2, "rms_kv_fwd")
    qraw = _mm(cqn, wq, "nn", name="mm_qup")
    kv = _mm(ckvn, wkv, "nn", name="mm_kvup", out_dtype=BF)
    qr, krr = _rope_fwd(qraw, sm, cos, sin, S)
    mixin, lse = _attn_fwd(mixin, qraw, qr, kv, krr, Bl, S)
    mix = _mm(mixin, W["w_out"], "nn", name="mm_out")
    h1, h1b = _ln1_fwd(x, mix, sp["ln1_g"], sp["ln1_b"])
    h1T = h1b.T
    ug = _mm(h1b, wup_g, "nn", name="mm_up_gate")
    uu = _mm(h1b, wup_u, "nn", name="mm_up_up")
    act = _ffn_act_fwd(ug, uu, cw_g, cw_u, cb_g, cb_u, Bl, S)
    ffn = _mm(act, W["ffn_w_down"], "nn", name="mm_down")
    gpre = _mm(h1b, W["ple_w_gate"], "nn", name="mm_gate")
    pp = _mm(pb, W["ple_w_proj"], "nn", name="mm_proj")

    dffn, dffnb, dgpre, dpp, hacc = _head(h1, ffn, gpre, pp, tgt, sp["ple_b_gate"], sp["ln2_g"], sp["ln2_b"])
    loss = hacc[3, 0]
    gW, gs = {}, {}
    gs["ple_b_gate"], gs["ln2_g"], gs["ln2_b"] = hacc[0:1], hacc[1:2], hacc[2:3]
    gW["ple_w_proj"] = _mm(pT, dpp, "nn", name="mm_dproj", out_dtype=BF)
    gW["ple_w_gate"] = _mm(h1T, dgpre, "nn", name="mm_dgate", out_dtype=BF)
    gW["ffn_w_down"] = _mm(act.T, dffnb, "nn", name="mm_ddown", out_dtype=BF)
    dact = _mm(dffnb, W["ffn_w_down"], "nt", name="mm_dact")
    du_g, du_u, dcw_g, dcw_u, dcb_g, dcb_u = _ffn_act_bwd(ug, uu, cw_g, cw_u, cb_g, cb_u, dact, Bl, S)
    gW["ffn_conv_w"] = jnp.concatenate([dcw_g, dcw_u], axis=1)
    gs["ffn_conv_b"] = jnp.concatenate([dcb_g, dcb_u], axis=1)
    gW["ffn_w_up"] = jnp.concatenate([_mm(h1T, du_g, "nn", name="mm_dup_gate", out_dtype=BF),
                                      _mm(h1T, du_u, "nn", name="mm_dup_up", out_dtype=BF)], axis=1)
    dh1 = _mm(dgpre, W["ple_w_gate"], "nt", name="mm_dh1_gate", add=dffn, add_scale=ALPHA)
    dh1 = _mm(du_g, wup_g, "nt", name="mm_dh1_upg", add=dh1)
    dh1 = _mm(du_u, wup_u, "nt", name="mm_dh1_upu", add=dh1)
    dxa, dmix, acc1 = _ln1_bwd(x, mix, sp["ln1_g"], sp["ln1_b"], dh1)
    gs["ln1_g"], gs["ln1_b"] = acc1[0:1], acc1[1:2]
    gW["w_out"] = _mm(mixin.T, dmix, "nn", name="mm_dwout", out_dtype=BF)
    dmixin = _mm(dmix, W["w_out"], "nt", name="mm_dmixin")
    do, dz, gacc = _gdn_post_bwd(o, z, sp["gdn_norm_g"], dmixin, T)
    gs["gdn_norm_g"] = gacc[0:1]
    bwd_out = _gdn_bwd(qkvn, sm, alog, dtb, gc_s, beta_s, tinv_s, sst_s, w_s, u_s, do, Bl, S,
                       xchg=None if early_grads is None else early_grads({n: gW[n] for n in GRADS_EARLY}))
    dqkvn, dsm, cacc = bwd_out[:3]
    early_recv = bwd_out[3:]
    gs["gdn_a_log"], gs["gdn_dt_bias"] = cacc[0:1, :GDN_HEADS], cacc[1:2, :GDN_HEADS]
    dqkv, gW["gdn_conv_w"] = _gdn_pre_bwd(qkv, W["gdn_conv_w"], dqkvn, Bl, S)
    dqraw, dqr, dkv, dkrr = _attn_bwd(qraw, qr, kv, krr, lse, dmixin, Bl, S)
    dqraw, dsm = _rope_bwd(dqraw, dsm, dqr, dkrr, cos, sin, S)
    gwq = _mm(cqn.T, dqraw, "nn", name="mm_dwq", out_dtype=BF)
    gW["mla_w_q_up"] = gwq.reshape(-1, MLA_HEADS, 256)[:, :, :192].reshape(-1, MLA_HEADS * 192)
    gW["mla_w_kv_up"] = _mm(ckvn.T, dkv, "nn", name="mm_dwkv", out_dtype=BF)
    dcqn = _mm(dqraw, wq, "nt", name="mm_dcqn")
    dckvn = _mm(dkv, wkv, "nt", name="mm_dckvn")
    dsm, qacc = _rms_bwd(sm, sp["mla_q_norm_g"], dcqn, dsm, 0, "rms_q_bwd")
    dsm, kacc = _rms_bwd(sm, sp["mla_kv_norm_g"], dckvn, dsm, 2, "rms_kv_bwd")
    gs["mla_q_norm_g"], gs["mla_kv_norm_g"] = qacc[0:1], kacc[0:1]
    gqkv = _mm(xT, dqkv, "nn", name="mm_dwqkv", out_dtype=BF)
    gz = _mm(xT, dz, "nn", name="mm_dwz", out_dtype=BF)
    gsm = _mm(xT, dsm, "nn", name="mm_dwsm", out_dtype=BF)
    gW["w_in"] = jnp.concatenate([gqkv, gz, gsm[:, 768:772], gsm[:, 896:900], gsm[:, 0:384], gsm[:, 512:768],
                                  gsm[:, 384:448]], axis=1)
    dx = _mm(dqkv, wqkv, "nt", name="mm_dx_qkv", add=dxa)
    dx = _mm(dz, wz, "nt", name="mm_dx_z", add=dx)
    dx = _mm(dsm, wsm, "nt", name="mm_dx_sm", add=dx)
    return loss, dx, gW, gs, early_recv


COL_SHARDED = ("w_in", "mla_w_q_up", "mla_w_kv_up", "ffn_w_up", "ple_w_proj", "gdn_conv_w", "ffn_conv_w")
SHARDED = EARLY + LATE
WEIGHTS = ("w_in", "gdn_conv_w", "gdn_a_log", "gdn_dt_bias", "gdn_norm_g", "mla_q_norm_g", "mla_w_q_up",
           "mla_kv_norm_g", "mla_w_kv_up", "w_out", "ln1_g", "ln1_b", "ffn_w_up", "ffn_conv_w", "ffn_conv_b",
           "ffn_w_down", "ple_w_gate", "ple_b_gate", "ple_w_proj", "ln2_g", "ln2_b")
CONV = ("gdn_conv_w", "ffn_conv_w")
SMALL_ROWS, SMALL_LANES = 96, 128


def _gathered_to_full(name, g):
    if name in COL_SHARDED:
        return jnp.transpose(g, (1, 0, 2)).reshape(g.shape[1], -1)
    return g.reshape(-1, g.shape[-1])


def _full_to_blocks(name, gfull, shard_shape):
    r, c = shard_shape
    if name in COL_SHARDED:
        return jnp.transpose(gfull.reshape(r, N_DEV, c), (1, 0, 2))
    return gfull.reshape(N_DEV, r, c)


def _pack_small(d):
    flat = jnp.concatenate([d[n].reshape(-1) for n in SMALL_ORDER])
    return jnp.pad(flat, (0, SMALL_ROWS * SMALL_LANES - flat.shape[0])).reshape(SMALL_ROWS, SMALL_LANES)


def _unpack_small(packed, shapes):
    flat = packed.reshape(-1)
    out, off = {}, 0
    for n in SMALL_ORDER:
        sz = math.prod(shapes[n])
        out[n] = flat[off:off + sz].reshape(shapes[n])
        off += sz
    return out


def kernel(x, p, w_in, gdn_conv_w, gdn_a_log, gdn_dt_bias, gdn_norm_g, mla_q_norm_g, mla_w_q_up, mla_kv_norm_g, mla_w_kv_up, w_out, ln1_g, ln1_b, ffn_w_up, ffn_conv_w, ffn_conv_b, ffn_w_down, ple_w_gate, ple_b_gate, ple_w_proj, ln2_g, ln2_b, loss_target, m_w_in, m_gdn_conv_w, m_gdn_a_log, m_gdn_dt_bias, m_gdn_norm_g, m_mla_q_norm_g, m_mla_w_q_up, m_mla_kv_norm_g, m_mla_w_kv_up, m_w_out, m_ln1_g, m_ln1_b, m_ffn_w_up, m_ffn_conv_w, m_ffn_conv_b, m_ffn_w_down, m_ple_w_gate, m_ple_b_gate, m_ple_w_proj, m_ln2_g, m_ln2_b, v_w_in, v_gdn_conv_w, v_gdn_a_log, v_gdn_dt_bias, v_gdn_norm_g, v_mla_q_norm_g, v_mla_w_q_up, v_mla_kv_norm_g, v_mla_w_kv_up, v_w_out, v_ln1_g, v_ln1_b, v_ffn_w_up, v_ffn_conv_w, v_ffn_conv_b, v_ffn_w_down, v_ple_w_gate, v_ple_b_gate, v_ple_w_proj, v_ln2_g, v_ln2_b):
    loc = dict(locals())
    wts = {n: loc[n] for n in WEIGHTS}
    ms = {n: loc["m_" + n] for n in WEIGHTS}
    vs = {n: loc["v_" + n] for n in WEIGHTS}
    Bl, S, D = x.shape
    T = Bl * S

    wire = lambda n: wts[n][0] if n in CONV else wts[n][0].astype(BF)
    got = _exchange([wire(n) for n in EARLY], ["ag"] * len(EARLY), "ag_weights_early")
    W = {n: _gathered_to_full(n, g) for n, g in zip(EARLY, got)}
    late = ([wire(n) for n in LATE], ["ag"] * len(LATE),
            lambda res: {n: _gathered_to_full(n, g) for n, g in zip(LATE, res)})
    pack = lambda g: ([_full_to_blocks(n, g[n], wts[n].shape[1:]) for n in GRADS_EARLY], ["a2a"] * len(GRADS_EARLY))
    sp = {n: wts[n].reshape(1, -1) for n in SMALL_ORDER}

    loss, dx, gW, gs, early_recv = _local_step(x.reshape(T, D), p.reshape(T, -1), loss_target.reshape(T, D), W, sp,
                                               Bl, S, late_weights=late, early_grads=pack)

    blocks = [_full_to_blocks(n, gW[n], wts[n].shape[1:]) for n in GRADS_LATE]
    late_recv = _exchange(blocks + [_pack_small(gs)], ["a2a"] * len(blocks) + ["ag"], "exchange_grads_late")

    res = {}
    for n, parts in list(zip(GRADS_EARLY, early_recv)) + list(zip(GRADS_LATE, late_recv[:-1])):
        res[n] = tuple(t[None] for t in _adamw(parts, wts[n][0], ms[n][0], vs[n][0], "adamw_" + n))
    small = _adamw(late_recv[-1], _pack_small(wts), _pack_small(ms), _pack_small(vs), "adamw_small")
    shapes = {n: wts[n].shape for n in SMALL_ORDER}
    small = [_unpack_small(t, shapes) for t in small]
    for n in SMALL_ORDER:
        res[n] = tuple(t[n] for t in small)

    loss = lax.psum(loss, ("x", "y", "c"))
    outs = [loss, dx.reshape(Bl, S, D)]
    for k in range(4):
        outs += [res[n][k] for n in WEIGHTS]
    return tuple(outs)
```

```python
import math

import jax
import jax.numpy as jnp
from jax import lax
from jax.experimental import pallas as pl
from jax.experimental.pallas import tpu as pltpu

F32 = jnp.float32
BF = jnp.bfloat16
_MXU = jnp.bfloat16
_INTERPRET = None
_VMEM_LIMIT = 56 * 1024 * 1024
HI = lax.Precision.HIGHEST

N_DEV = 8
CHUNK = 64
GDN_HEADS = 4
GDN_DK = 128
MLA_HEADS = 4
MLA_NOPE = 128
MLA_ROPE = 64
ROPE_THETA = 10000.0
ALPHA = 2.0 ** 0.25
NORM_EPS = 1e-6
ATTN_SCALE = (MLA_NOPE + MLA_ROPE) ** -0.5
ADAM_LR, ADAM_B1, ADAM_B2, ADAM_EPS, ADAM_WD, ADAM_STEP = 0.001, 0.9, 0.999, 1e-08, 0.01, 10
MESH = pl.DeviceIdType.MESH
ANY = pl.BlockSpec(memory_space=pl.ANY)
_NN = (((1,), (0,)), ((), ()))
_NT = (((1,), (1,)), ((), ()))
_TN = (((0,), (0,)), ((), ()))


def _sds(shape, dtype=F32):
    return jax.ShapeDtypeStruct(tuple(shape), dtype)


def _pick(n, cands):
    for c in cands:
        if n % c == 0:
            return c
    return n


def _xchg_out_shapes(arrs, modes):
    return [_sds((N_DEV,) + a.shape if m == "ag" else a.shape, a.dtype) for a, m in zip(arrs, modes)]


def _xchg_scratch(n):
    return [pltpu.SemaphoreType.DMA((n, N_DEV - 1)), pltpu.SemaphoreType.DMA((n, N_DEV - 1)),
            pltpu.SemaphoreType.DMA((n,))]


def _xchg_copies(ins, outs, send, recv, loc, modes):
    x, y, c = lax.axis_index("x"), lax.axis_index("y"), lax.axis_index("c")
    me = 4 * x + 2 * y + c
    copies = []
    for ai, mode in enumerate(modes):
        ag = mode == "ag"
        copies.append(pltpu.make_async_copy(ins[ai] if ag else ins[ai].at[me], outs[ai].at[me], loc.at[ai]))
        for r in range(1, N_DEV):
            px = 1 - x if (r >> 2) & 1 else x
            py = 1 - y if (r >> 1) & 1 else y
            pc = 1 - c if r & 1 else c
            peer = 4 * px + 2 * py + pc
            copies.append(pltpu.make_async_remote_copy(
                src_ref=ins[ai] if ag else ins[ai].at[peer], dst_ref=outs[ai].at[me],
                send_sem=send.at[ai, r - 1], recv_sem=recv.at[ai, r - 1],
                device_id=(px, py, pc), device_id_type=MESH))
    return copies


def _pc(body, *, name, out_shape, grid=None, in_specs=None, out_specs=None, scratch=(), sem=None, aliases=None,
        xchg=None):
    kw = {}
    if _INTERPRET is not None:
        kw["interpret"] = _INTERPRET
    single = not isinstance(out_shape, (tuple, list))
    out_shape = [out_shape] if single else list(out_shape)
    if out_specs is not None:
        out_specs = [out_specs] if single else list(out_specs)
    scratch = list(scratch)
    if xchg is not None:
        xarrs, xmodes = xchg
        n_in, n_out, n_scr, nx = len(in_specs), len(out_shape), len(scratch), len(xarrs)
        single = False
        inner = body

        def body(*refs):
            ins, xins = refs[:n_in], refs[n_in:n_in + nx]
            outs = refs[n_in + nx:n_in + nx + n_out]
            xouts = refs[n_in + nx + n_out:n_in + 2 * nx + n_out]
            scr = refs[n_in + 2 * nx + n_out:n_in + 2 * nx + n_out + n_scr]
            send, recv, loc = refs[n_in + 2 * nx + n_out + n_scr:]
            first = last = None
            for d, g in enumerate(grid):
                f, l = pl.program_id(d) == 0, pl.program_id(d) == g - 1
                first, last = (f, l) if first is None else (first & f, last & l)

            @pl.when(first)
            def _():
                for cp in _xchg_copies(xins, xouts, send, recv, loc, xmodes):
                    cp.start()

            inner(*ins, *outs, *scr)

            @pl.when(last)
            def _():
                for cp in _xchg_copies(xins, xouts, send, recv, loc, xmodes):
                    cp.wait()

        in_specs = list(in_specs) + [ANY] * nx
        out_specs = out_specs + [ANY] * nx
        out_shape = out_shape + _xchg_out_shapes(xarrs, xmodes)
        scratch = scratch + _xchg_scratch(nx)
        sem = ("arbitrary",) * len(grid)
    if grid is not None:
        kw["grid"] = grid
    if in_specs is not None:
        kw["in_specs"] = in_specs
    if out_specs is not None:
        kw["out_specs"] = out_specs[0] if single else tuple(out_specs)
    cp = dict(vmem_limit_bytes=_VMEM_LIMIT)
    if sem is not None:
        cp["dimension_semantics"] = sem
    call = pl.pallas_call(body, name=name, out_shape=out_shape[0] if single else tuple(out_shape),
                          scratch_shapes=scratch, input_output_aliases=aliases or {},
                          compiler_params=pltpu.CompilerParams(**cp), **kw)
    if xchg is None:
        return call
    return lambda *ins: call(*ins, *xchg[0])


def _exchange(arrs, modes, name):
    n = len(arrs)

    def body(*refs):
        copies = _xchg_copies(refs[:n], refs[n:2 * n], *refs[2 * n:], modes)
        for cp in copies:
            cp.start()
        for cp in copies:
            cp.wait()

    return _pc(body, name=name, out_shape=tuple(_xchg_out_shapes(arrs, modes)), in_specs=[ANY] * n,
               out_specs=tuple([ANY] * n), scratch=_xchg_scratch(n))(*arrs)


_FULL_K_MAX = 2816


def _mm(a, b, mode, *, name, add=None, add_scale=1.0, out_dtype=F32, xchg=None):
    (M, K) = a.shape
    (K2, N) = b.shape if mode == "nn" else b.shape[::-1]
    assert K == K2, (a.shape, b.shape, mode)
    tm = _pick(M, (1024, 1408, 512, 384, 256, 128))
    tn = _pick(N, (1024, 1408, 768, 512, 384, 256, 128))
    tk = K if K <= _FULL_K_MAX else _pick(K, (2048, 1408, 1024, 512))
    nk = K // tk
    dims = _NN if mode == "nn" else _NT
    has_add = add is not None

    def finish(r, add_ref, o_ref):
        if has_add:
            r = r + add_scale * add_ref[...].astype(F32)
        o_ref[...] = r.astype(out_dtype)

    if nk == 1:
        def body(a_ref, b_ref, *rest):
            r = lax.dot_general(a_ref[...].astype(_MXU), b_ref[...].astype(_MXU), dims, preferred_element_type=F32)
            finish(r, rest[0] if has_add else None, rest[-1])

        a_spec = pl.BlockSpec((tm, K), lambda i, j: (i, 0))
        b_spec = (pl.BlockSpec((K, tn), lambda i, j: (0, j)) if mode == "nn"
                  else pl.BlockSpec((tn, K), lambda i, j: (j, 0)))
        o_spec = pl.BlockSpec((tm, tn), lambda i, j: (i, j))
        grid, sem, scratch = (M // tm, N // tn), ("parallel", "parallel"), []
    else:
        def body(a_ref, b_ref, *rest):
            o_ref, acc = rest[-2], rest[-1]
            k = pl.program_id(2)

            @pl.when(k == 0)
            def _():
                acc[...] = jnp.zeros_like(acc)

            acc[...] += lax.dot_general(a_ref[...].astype(_MXU), b_ref[...].astype(_MXU), dims,
                                        preferred_element_type=F32)

            @pl.when(k == nk - 1)
            def _():
                finish(acc[...], rest[0] if has_add else None, o_ref)

        a_spec = pl.BlockSpec((tm, tk), lambda i, j, k: (i, k))
        b_spec = (pl.BlockSpec((tk, tn), lambda i, j, k: (k, j)) if mode == "nn"
                  else pl.BlockSpec((tn, tk), lambda i, j, k: (j, k)))
        o_spec = pl.BlockSpec((tm, tn), lambda i, j, k: (i, j))
        grid, sem, scratch = (M // tm, N // tn, nk), ("parallel", "parallel", "arbitrary"), [pltpu.VMEM((tm, tn), F32)]

    ins = [a, b] + ([add] if has_add else [])
    specs = [a_spec, b_spec] + ([o_spec] if has_add else [])
    return _pc(body, name=name, out_shape=_sds((M, N), out_dtype), grid=grid, in_specs=specs, out_specs=o_spec,
               scratch=scratch, sem=sem, xchg=xchg)(*ins)


def _dot(a, b, dims):
    return lax.dot_general(a.astype(_MXU), b.astype(_MXU), dims, preferred_element_type=F32)


def _split(a):
    hi = a.astype(BF)
    lo = (a - hi.astype(F32)).astype(BF)
    return hi, lo


def _dot3(a, b, dims):
    ah, al = _split(a)
    bh, bl = _split(b)
    d = lambda u, v: lax.dot_general(u, v, dims, preferred_element_type=F32)
    return d(ah, bh) + (d(ah, bl) + d(al, bh))


def _softplus(x):
    return jnp.maximum(x, 0.0) + jnp.log1p(jnp.exp(-jnp.abs(x)))


def _silu(x):
    return x * jax.nn.sigmoid(x)


def _rmsnorm(x, g):
    return x * lax.rsqrt(jnp.mean(x * x, axis=-1, keepdims=True) + NORM_EPS) * g


def _layernorm(x, g, b):
    mu = jnp.mean(x, axis=-1, keepdims=True)
    xc = x - mu
    var = jnp.mean(xc * xc, axis=-1, keepdims=True)
    return xc * lax.rsqrt(var + NORM_EPS) * g + b


def _shift_dn(x, s):
    if s == 0:
        return x
    t = lax.broadcasted_iota(jnp.int32, x.shape, 0)
    return jnp.where(t >= s, pltpu.roll(x, s, 0), 0.0)


def _shift_up(x, s):
    if s == 0:
        return x
    n = x.shape[0]
    t = lax.broadcasted_iota(jnp.int32, x.shape, 0)
    return jnp.where(t < n - s, pltpu.roll(x, n - s, 0), 0.0)


def _conv(x, w):
    kk = w.shape[0]
    y = w[kk - 1:kk, :] * x
    for j in range(kk - 1):
        y = y + w[j:j + 1, :] * _shift_dn(x, kk - 1 - j)
    return y


def _conv_bwd_x(dy, w):
    kk = w.shape[0]
    dx = w[kk - 1:kk, :] * dy
    for j in range(kk - 1):
        dx = dx + w[j:j + 1, :] * _shift_up(dy, kk - 1 - j)
    return dx


def _conv_bwd_w(x, dy, dw_ref, first):
    kk = dw_ref.shape[0]
    for j in range(kk):
        r = jnp.sum(dy * _shift_dn(x, kk - 1 - j), axis=0, keepdims=True)
        prev = jnp.where(first, 0.0, dw_ref[j:j + 1, :])
        dw_ref[j:j + 1, :] = prev + r


def _gdn_post_conv(c, j):
    h = _silu(c)
    hn = h * lax.rsqrt(jnp.sum(h * h, axis=-1, keepdims=True) + NORM_EPS)
    return jnp.where(j < GDN_HEADS, hn * (GDN_DK ** -0.5), jnp.where(j < 2 * GDN_HEADS, hn, h))


def _gdn_pre_fwd(qkv, conv_w, Bl, S):
    T, C = qkv.shape
    nj = C // 128

    def body(x_ref, w_ref, o_ref):
        o_ref[...] = _gdn_post_conv(_conv(x_ref[...], w_ref[...]), pl.program_id(1))

    blk = pl.BlockSpec((S, 128), lambda b, j: (b, j))
    return _pc(body, name="gdn_pre_fwd", out_shape=_sds((T, C)), grid=(Bl, nj),
               in_specs=[blk, pl.BlockSpec((conv_w.shape[0], 128), lambda b, j: (0, j))], out_specs=blk,
               sem=("parallel", "parallel"))(qkv, conv_w)


def _gdn_pre_bwd(qkv, conv_w, dout, Bl, S):
    T, C = qkv.shape
    nj = C // 128
    kk = conv_w.shape[0]

    def body(x_ref, w_ref, d_ref, dx_ref, dw_ref):
        j, b = pl.program_id(0), pl.program_id(1)
        x, w = x_ref[...], w_ref[...]
        c = _conv(x, w)
        _, vjp = jax.vjp(lambda u: _gdn_post_conv(u, j), c)
        (dc,) = vjp(d_ref[...])
        dx_ref[...] = _conv_bwd_x(dc, w).astype(dx_ref.dtype)
        _conv_bwd_w(x, dc, dw_ref, b == 0)

    blk = pl.BlockSpec((S, 128), lambda j, b: (b, j))
    wblk = pl.BlockSpec((kk, 128), lambda j, b: (0, j))
    return _pc(body, name="gdn_pre_bwd", out_shape=(_sds((T, C), BF), _sds((kk, C))), grid=(nj, Bl),
               in_specs=[blk, wblk, blk], out_specs=(blk, wblk), sem=("parallel", "arbitrary"))(qkv, conv_w, dout)


_H = GDN_HEADS
_HC = _H * CHUNK


def _st(x):
    return jnp.concatenate([x[:, h * 128:(h + 1) * 128] for h in range(_H)], axis=0)


def _unst(x):
    return jnp.concatenate([x[h * CHUNK:(h + 1) * CHUNK] for h in range(_H)], axis=1)


def _stc(t):
    return jnp.concatenate([t[:, h:h + 1] for h in range(_H)], axis=0)


def _untile(col):
    lane = lax.broadcasted_iota(jnp.int32, (CHUNK, 128), 1)
    out = jnp.zeros((CHUNK, 128), F32)
    for h in range(_H):
        out = out + jnp.where(lane == h, col[h * CHUNK:(h + 1) * CHUNK], 0.0)
    return out


def _rowform(col):
    return jnp.broadcast_to(col, (_HC, 128)).T[0:1, :]


def _tri(n, upper=False):
    i = lax.broadcasted_iota(jnp.int32, (n, n), 0)
    j = lax.broadcasted_iota(jnp.int32, (n, n), 1)
    return jnp.where((j >= i) if upper else (j <= i), 1.0, 0.0).astype(F32)


def _gdn_chunk_common(q, k, v, gc, beta):
    Q, K, V = _st(q), _st(k), _st(v)
    B, GC = _stc(beta), _stc(gc)
    GL = jnp.concatenate([jnp.broadcast_to(gc[CHUNK - 1:CHUNK, h:h + 1], (CHUNK, 1)) for h in range(_H)], axis=0)
    ii = lax.broadcasted_iota(jnp.int32, (_HC, _HC), 0)
    jj = lax.broadcasted_iota(jnp.int32, (_HC, _HC), 1)
    same = (ii >> 6) == (jj >> 6)
    incl = same & (ii >= jj)
    strict = same & (ii > jj)
    diff = GC - _rowform(GC)
    D = jnp.where(incl, jnp.exp(jnp.where(incl, diff, 0.0)), 0.0)
    KB = K * B
    A = jnp.where(strict, _dot(KB, K, _NT) * D, 0.0)
    P = jnp.where(incl, _dot(Q, K, _NT) * D, 0.0)
    EG = jnp.exp(GC)
    ED = jnp.exp(GL - GC)
    return dict(Q=Q, K=K, V=V, B=B, GC=GC, GL=GL, incl=incl, strict=strict, D=D, KB=KB, A=A, P=P, EG=EG, ED=ED,
                QG=Q * EG, KD=K * ED)


def _neumann_inv(A):
    n = A.shape[0]
    i = lax.broadcasted_iota(jnp.int32, (n, n), 0)
    j = lax.broadcasted_iota(jnp.int32, (n, n), 1)
    N = -A
    acc = jnp.where(i == j, 1.0, 0.0) + N
    Pw = N
    for _ in range(5):
        Pw = _dot3(Pw, Pw, _NN)
        acc = acc + _dot3(acc, Pw, _NN)
    return acc


def _gates(a, alog, dtb):
    return -jnp.exp(alog) * _softplus(a + dtb)


def _hs(x, h, n=CHUNK):
    return x[h * n:(h + 1) * n]


def _per_step(Bl):
    return 2 if Bl % 2 == 0 else 1


def _gdn_fwd(qkvn, sm, alog, dtb, Bl, S, xchg=None):
    T = Bl * S
    nC = S // CHUNK
    E = _per_step(Bl)

    def body(*refs):
        S_ref = refs[-1]

        @pl.when(pl.program_id(1) == 0)
        def _():
            S_ref[...] = jnp.zeros_like(S_ref)

        for e in range(E):
            one(*[r if i in (5, 6) else r.at[e] for i, r in enumerate(refs)])

    def one(q_ref, k_ref, v_ref, a_ref, b_ref, alog_ref, dtb_ref,
            o_ref, tinv_ref, sst_ref, w_ref, u_ref, gc_ref, beta_ref, S_ref):
        g = _gates(a_ref[...], alog_ref[...], dtb_ref[...])
        beta = jax.nn.sigmoid(b_ref[...])
        gc = jnp.dot(_tri(CHUNK), g, precision=HI, preferred_element_type=F32)
        cm = _gdn_chunk_common(q_ref[...], k_ref[...], v_ref[...], gc, beta)
        tinv = _neumann_inv(cm["A"])
        W = _dot3(tinv, cm["KB"] * cm["EG"], _NN)
        U = _dot3(tinv, cm["V"] * cm["B"], _NN)
        s_old = [S_ref[h * 128:(h + 1) * 128, :] for h in range(_H)]
        vn = [_hs(U, h) - _dot(_hs(W, h), s_old[h], _NN) for h in range(_H)]
        o_intra = _dot(cm["P"], jnp.concatenate(vn, axis=0), _NN)
        outs = []
        for h in range(_H):
            outs.append(_dot(_hs(cm["QG"], h), s_old[h], _NN) + _hs(o_intra, h))
            gl = jnp.exp(gc[CHUNK - 1:CHUNK, h:h + 1])
            S_ref[h * 128:(h + 1) * 128, :] = gl * s_old[h] + _dot(_hs(cm["KD"], h), vn[h], _TN)
            sst_ref[h * 128:(h + 1) * 128, :] = s_old[h]
        o_ref[...] = jnp.concatenate(outs, axis=1)
        tinv_ref[...] = tinv
        w_ref[...] = _unst(W)
        u_ref[...] = _unst(U)
        gc_ref[...] = gc
        beta_ref[...] = beta

    blk = lambda r, w, c: pl.BlockSpec((E, r, w), lambda b, n: (b, n, c))
    par = pl.BlockSpec((1, 128), lambda b, n: (0, 0))
    out_shape = (_sds((Bl, S, 512)), _sds((Bl, nC * _HC, _HC)), _sds((Bl, nC * 512, 128)),
                 _sds((Bl, S, 512)), _sds((Bl, S, 512)), _sds((Bl, S, 128)), _sds((Bl, S, 128)))
    out_specs = (blk(CHUNK, 512, 0), blk(_HC, _HC, 0), blk(512, 128, 0), blk(CHUNK, 512, 0), blk(CHUNK, 512, 0),
                 blk(CHUNK, 128, 0), blk(CHUNK, 128, 0))
    q3, s3 = qkvn.reshape(Bl, S, -1), sm.reshape(Bl, S, -1)
    res = _pc(body, name="gdn_chunk_fwd", out_shape=out_shape, grid=(Bl // E, nC),
              in_specs=[blk(CHUNK, 512, 0), blk(CHUNK, 512, 1), blk(CHUNK, 512, 2), blk(CHUNK, 128, 6),
                        blk(CHUNK, 128, 7), par, par],
              out_specs=out_specs, scratch=[pltpu.VMEM((E, 512, 128), F32)], sem=("parallel", "arbitrary"),
              xchg=xchg)(q3, q3, q3, s3, s3, alog, dtb)
    return (res[0].reshape(T, 512),) + tuple(res[1:])


def _gdn_bwd(qkvn, sm, alog, dtb, gc_s, beta_s, tinv_s, sst_s, w_s, u_s, do, Bl, S, xchg=None):
    T = Bl * S
    nC = S // CHUNK
    E = _per_step(Bl)

    def body(*refs):
        acc_ref, dS_ref = refs[-2], refs[-1]
        first_chunk = pl.program_id(1) == 0

        @pl.when(first_chunk)
        def _():
            dS_ref[...] = jnp.zeros_like(dS_ref)

        @pl.when(first_chunk & (pl.program_id(0) == 0))
        def _():
            acc_ref[...] = jnp.zeros_like(acc_ref)

        for e in range(E):
            one(*[r if i in (4, 5, 15) else r.at[e] for i, r in enumerate(refs)])

    def one(q_ref, k_ref, v_ref, a_ref, alog_ref, dtb_ref, gc_ref, beta_ref, tinv_ref, sst_ref, w_ref, u_ref,
            do_ref, dqkv_ref, dsm_ref, acc_ref, dS_ref):
        gc, beta = gc_ref[...], beta_ref[...]
        cm = _gdn_chunk_common(q_ref[...], k_ref[...], v_ref[...], gc, beta)
        Q, K, V, B, D, KB, A, P = (cm[n] for n in ("Q", "K", "V", "B", "D", "KB", "A", "P"))
        EG, ED, QG, KD = cm["EG"], cm["ED"], cm["QG"], cm["KD"]
        tinv = tinv_ref[...]
        W, U, DO = _st(w_ref[...]), _st(u_ref[...]), _st(do_ref[...])
        s_old = [sst_ref[h * 128:(h + 1) * 128, :] for h in range(_H)]
        ds_new = [dS_ref[h * 128:(h + 1) * 128, :] for h in range(_H)]
        VN = jnp.concatenate([_hs(U, h) - _dot(_hs(W, h), s_old[h], _NN) for h in range(_H)], axis=0)
        dP = jnp.where(cm["incl"], _dot(DO, VN, _NT), 0.0)
        dVN0 = _dot(P, DO, _TN)
        dVN, dQG, dKD, dW, TL = [], [], [], [], []
        for h in range(_H):
            gl = jnp.exp(gc[CHUNK - 1:CHUNK, h:h + 1])
            dvn = _hs(dVN0, h) + _dot(_hs(KD, h), ds_new[h], _NN)
            dkd = _dot(_hs(VN, h), ds_new[h], _NT)
            dVN.append(dvn)
            dQG.append(_dot(_hs(DO, h), s_old[h], _NT))
            dKD.append(dkd)
            dW.append(-_dot(dvn, s_old[h], _NT))
            dgl = jnp.sum(jnp.sum(ds_new[h] * s_old[h], axis=1, keepdims=True), axis=0, keepdims=True)
            tl = jnp.sum(jnp.sum(dkd * _hs(KD, h), axis=1, keepdims=True), axis=0, keepdims=True) + dgl * gl
            TL.append(jnp.broadcast_to(tl, (CHUNK, 1)))
            dS_ref[h * 128:(h + 1) * 128, :] = (_dot(_hs(QG, h), _hs(DO, h), _TN) + gl * ds_new[h]
                                                - _dot(_hs(W, h), dvn, _TN))
        dVN, dQG, dKD, dW, TL = (jnp.concatenate(z, axis=0) for z in (dVN, dQG, dKD, dW, TL))
        dVB = _dot3(tinv, dVN, _TN)
        dKBE = _dot3(tinv, dW, _TN)
        dA = jnp.where(cm["strict"], -(_dot3(dVB, U, _NT) + _dot3(dKBE, W, _NT)), 0.0)
        dG = dA * D
        dQK = dP * D
        dKB = _dot(dG, K, _NN) + dKBE * EG
        dK = _dot(dG, KB, _TN) + _dot(dQK, Q, _TN) + dKB * B + dKD * ED
        dQ = _dot(dQK, K, _NN) + dQG * EG
        Mx = dA * A + dP * P
        rs = lambda z: jnp.sum(z, axis=1, keepdims=True)
        ri = lax.broadcasted_iota(jnp.int32, (_HC, 1), 0)
        dGC = (rs(Mx) - rs(Mx.T) + rs(dKBE * KB * EG) + rs(dQG * QG) - rs(dKD * KD)
               + jnp.where((ri & (CHUNK - 1)) == CHUNK - 1, TL, 0.0))
        dBeta = rs(dVB * V) + rs(dKB * K)
        dqkv_ref[:, 0:512] = _unst(dQ)
        dqkv_ref[:, 512:1024] = _unst(dK)
        dqkv_ref[:, 1024:1536] = _unst(dVB * B)
        dg = jnp.dot(_tri(CHUNK, upper=True), _untile(dGC), precision=HI, preferred_element_type=F32)
        a, alog_v, dtb_v = a_ref[...], alog_ref[...], dtb_ref[...]
        g = _gates(a, alog_v, dtb_v)
        lane = lax.broadcasted_iota(jnp.int32, (CHUNK, 128), 1)
        valid = lane < _H
        da = jnp.where(valid, dg * (-jnp.exp(alog_v)) * jax.nn.sigmoid(a + dtb_v), 0.0)
        dsm_ref[:, 0:128] = da.astype(dsm_ref.dtype)
        dsm_ref[:, 128:256] = jnp.where(valid, _untile(dBeta) * beta * (1.0 - beta), 0.0).astype(dsm_ref.dtype)
        acc_ref[0:1, :] += jnp.sum(jnp.where(valid, dg * g, 0.0), axis=0, keepdims=True)
        acc_ref[1:2, :] += jnp.sum(da, axis=0, keepdims=True)

    blk = lambda r, w, c: pl.BlockSpec((E, r, w), lambda b, n: (b, nC - 1 - n, c))
    par = pl.BlockSpec((1, 128), lambda b, n: (0, 0))
    in_specs = [blk(CHUNK, 512, 0), blk(CHUNK, 512, 1), blk(CHUNK, 512, 2), blk(CHUNK, 128, 6), par, par,
                blk(CHUNK, 128, 0), blk(CHUNK, 128, 0), blk(_HC, _HC, 0), blk(512, 128, 0),
                blk(CHUNK, 512, 0), blk(CHUNK, 512, 0), blk(CHUNK, 512, 0)]
    out_shape = (_sds((Bl, S, 1536)), _sds((Bl, S, 1024), BF), _sds((8, 128)))
    out_specs = (blk(CHUNK, 1536, 0), blk(CHUNK, 256, 3), pl.BlockSpec((8, 128), lambda b, n: (0, 0)))
    q3, s3 = qkvn.reshape(Bl, S, -1), sm.reshape(Bl, S, -1)
    res = _pc(body, name="gdn_chunk_bwd", out_shape=out_shape, grid=(Bl // E, nC), in_specs=in_specs,
              out_specs=out_specs, scratch=[pltpu.VMEM((E, 512, 128), F32)], sem=("arbitrary", "arbitrary"),
              xchg=xchg)(q3, q3, q3, s3, alog, dtb, gc_s, beta_s, tinv_s, sst_s, w_s, u_s, do.reshape(Bl, S, -1))
    return (res[0].reshape(T, 1536), res[1].reshape(T, 1024)) + tuple(res[2:])


def _gdn_post(o, z, g, T):
    tt = _pick(T, (2048, 1024, 512, 256))

    def body(o_ref, z_ref, g_ref, out_ref, outt_ref):
        y = (_rmsnorm(o_ref[...], g_ref[...]) * _silu(z_ref[...])).astype(out_ref.dtype)
        out_ref[...] = y
        outt_ref[...] = y.T

    blk = pl.BlockSpec((tt, 128), lambda i, h: (i, h))
    return _pc(body, name="gdn_post_fwd", out_shape=(_sds((T, 1024), BF), _sds((1024, T), BF)), grid=(T // tt, _H),
               in_specs=[blk, blk, pl.BlockSpec((1, 128), lambda i, h: (0, 0))],
               out_specs=(blk, pl.BlockSpec((128, tt), lambda i, h: (h, i))),
               sem=("parallel", "parallel"))(o, z, g)


def _gdn_post_bwd(o, z, g, dmixin, T):
    tt = _pick(T, (2048, 1024, 512, 256))

    def body(o_ref, z_ref, g_ref, d_ref, do_ref, dz_ref, dg_ref):
        @pl.when((pl.program_id(0) == 0) & (pl.program_id(1) == 0))
        def _():
            dg_ref[...] = jnp.zeros_like(dg_ref)

        _, vjp = jax.vjp(lambda a, b, c: _rmsnorm(a, c) * _silu(b), o_ref[...], z_ref[...], g_ref[...])
        do, dz, dg = vjp(d_ref[...])
        do_ref[...] = do
        dz_ref[...] = dz.astype(dz_ref.dtype)
        dg_ref[0:1, :] += dg

    blk = pl.BlockSpec((tt, 128), lambda i, h: (i, h))
    return _pc(body, name="gdn_post_bwd", out_shape=(_sds((T, 512)), _sds((T, 512), BF), _sds((8, 128))),
               grid=(T // tt, _H), in_specs=[blk, blk, pl.BlockSpec((1, 128), lambda i, h: (0, 0)), blk],
               out_specs=(blk, blk, pl.BlockSpec((8, 128), lambda i, h: (0, 0))),
               sem=("arbitrary", "arbitrary"))(o, z, g, dmixin)


def _rms_fwd(sm, g, col_blk, name):
    T = sm.shape[0]
    d = g.shape[1]
    tt = _pick(T, (1024, 512, 256))

    def body(x_ref, g_ref, o_ref, ot_ref):
        y = _rmsnorm(x_ref[...], g_ref[...]).astype(o_ref.dtype)
        o_ref[...] = y
        ot_ref[...] = y.T

    return _pc(body, name=name, out_shape=(_sds((T, d), BF), _sds((d, T), BF)), grid=(T // tt,),
               in_specs=[pl.BlockSpec((tt, d), lambda i: (i, col_blk)), pl.BlockSpec((1, d), lambda i: (0, 0))],
               out_specs=(pl.BlockSpec((tt, d), lambda i: (i, 0)), pl.BlockSpec((d, tt), lambda i: (0, i))),
               sem=("parallel",))(sm, g)


def _rms_bwd(sm, g, dy, dsm, col_blk, name):
    T = sm.shape[0]
    d = g.shape[1]
    tt = _pick(T, (1024, 512, 256))

    def body(x_ref, g_ref, d_ref, alias_ref, dx_ref, dg_ref):
        @pl.when(pl.program_id(0) == 0)
        def _():
            dg_ref[...] = jnp.zeros_like(dg_ref)

        _, vjp = jax.vjp(_rmsnorm, x_ref[...], g_ref[...])
        dx, dg = vjp(d_ref[...])
        dx_ref[...] = dx.astype(dx_ref.dtype)
        dg_ref[0:1, :] += dg

    grp = pl.BlockSpec((tt, d), lambda i: (i, col_blk))
    return _pc(body, name=name, out_shape=(_sds(dsm.shape, dsm.dtype), _sds((8, d))), grid=(T // tt,),
               in_specs=[grp, pl.BlockSpec((1, d), lambda i: (0, 0)), pl.BlockSpec((tt, d), lambda i: (i, 0)), ANY],
               out_specs=(grp, pl.BlockSpec((8, d), lambda i: (0, 0))), aliases={3: 0},
               sem=("arbitrary",))(sm, g, dy, dsm)


def _rope_tables(S):
    inv = ROPE_THETA ** (-jnp.arange(0, MLA_ROPE, 2, dtype=F32) / MLA_ROPE)
    ang = jnp.arange(S, dtype=F32)[:, None] * inv[None, :]
    cos, sin = jnp.cos(ang), jnp.sin(ang)
    z = jnp.zeros((S, 64), F32)
    return jnp.concatenate([cos, cos, z], axis=1), jnp.concatenate([-sin, sin, z], axis=1)


def _swap_halves(x):
    lane = lax.broadcasted_iota(jnp.int32, x.shape, 1)
    return jnp.where(lane < 32, pltpu.roll(x, 96, 1), jnp.where(lane < 64, pltpu.roll(x, 32, 1), 0.0))


def _rope_fwd(qraw, sm, cos, sin, S):
    T = qraw.shape[0]
    tt = _pick(S, (2048, 1024, 512, 256))
    nps = S // tt

    def body(q_ref, kr_ref, c_ref, s_ref, qo_ref, ko_ref):
        c, s = c_ref[...], s_ref[...]
        q = q_ref[...]
        qo_ref[...] = ((q * c + _swap_halves(q) * s) * ATTN_SCALE).astype(qo_ref.dtype)

        @pl.when(pl.program_id(1) == 0)
        def _():
            k = kr_ref[...]
            ko_ref[...] = (k * c + _swap_halves(k) * s).astype(ko_ref.dtype)

    tab = pl.BlockSpec((tt, 128), lambda i, h: (i % nps, 0))
    return _pc(body, name="rope_fwd", out_shape=(_sds((T, 512), BF), _sds((T, 128), BF)), grid=(T // tt, MLA_HEADS),
               in_specs=[pl.BlockSpec((tt, 128), lambda i, h: (i, 2 * h + 1)),
                         pl.BlockSpec((tt, 128), lambda i, h: (i, 3)), tab, tab],
               out_specs=(pl.BlockSpec((tt, 128), lambda i, h: (i, h)), pl.BlockSpec((tt, 128), lambda i, h: (i, 0))),
               sem=("parallel", "arbitrary"))(qraw, sm, cos, sin)


def _rope_bwd(dqraw, dsm, dqr, dkrr, cos, sin, S):
    T = dqraw.shape[0]
    tt = _pick(S, (2048, 1024, 512, 256))
    nps = S // tt

    def body(alias_q, alias_s, dq_ref, dk_ref, c_ref, s_ref, qo_ref, ko_ref):
        c, s = c_ref[...], s_ref[...]
        d = dq_ref[...]
        qo_ref[...] = (d * c + _swap_halves(d * s)).astype(qo_ref.dtype)

        @pl.when(pl.program_id(1) == 0)
        def _():
            k = dk_ref[...]
            ko_ref[...] = (k * c + _swap_halves(k * s)).astype(ko_ref.dtype)

    tab = pl.BlockSpec((tt, 128), lambda i, h: (i % nps, 0))
    return _pc(body, name="rope_bwd", out_shape=(_sds(dqraw.shape, dqraw.dtype), _sds(dsm.shape, dsm.dtype)),
               grid=(T // tt, MLA_HEADS),
               in_specs=[ANY, ANY, pl.BlockSpec((tt, 128), lambda i, h: (i, h)),
                         pl.BlockSpec((tt, 128), lambda i, h: (i, 0)), tab, tab],
               out_specs=(pl.BlockSpec((tt, 128), lambda i, h: (i, 2 * h + 1)),
                          pl.BlockSpec((tt, 128), lambda i, h: (i, 3))),
               aliases={0: 0, 1: 1}, sem=("parallel", "arbitrary"))(dqraw, dsm, dqr, dkrr, cos, sin)


def _attn_scores(qn, qr, kn_ref, kr_ref, L, tq):
    def sc(lo, hi):
        return _dot(qn, kn_ref[lo:hi, :], _NT) + _dot(qr, kr_ref[lo:hi, :], _NT)

    sd = sc(L - tq, L)
    qc = lax.broadcasted_iota(jnp.int32, sd.shape, 0) >> 6
    kc = lax.broadcasted_iota(jnp.int32, sd.shape, 1) >> 6
    sd = jnp.where(kc <= qc, sd, -jnp.inf)
    return sd if L == tq else jnp.concatenate([sc(0, L - tq), sd], axis=1)


def _attn_fwd(mixin, mixin_t, qraw, qr, kv, krr, Bl, S):
    T = Bl * S
    tq = _pick(S, (256, 128))
    nq = S // tq

    def body(alias_ref, alias_t, qn_ref, qr_ref, kn_ref, v_ref, kr_ref, o_ref, lse_ref, ot_ref):
        i = pl.program_id(2)
        qn = (qn_ref[...] * ATTN_SCALE).astype(BF)
        qrv = qr_ref[...]
        for ii in range(nq):
            @pl.when(i == ii)
            def _(L=(ii + 1) * tq):
                s = _attn_scores(qn, qrv, kn_ref, kr_ref, L, tq)
                m = jnp.max(s, axis=-1, keepdims=True)
                e = jnp.exp(s - m)
                l = jnp.sum(e, axis=-1, keepdims=True)
                o = (_dot(e, v_ref[0:L, :], _NN) / l).astype(o_ref.dtype)
                o_ref[...] = o
                ot_ref[...] = o.T
                lse_ref[...] = jnp.broadcast_to(m + jnp.log(l), lse_ref.shape)

    qrow = lambda b, h, i: b * nq + i
    in_specs = [ANY, ANY,
                pl.BlockSpec((tq, 128), lambda b, h, i: (qrow(b, h, i), 2 * h)),
                pl.BlockSpec((tq, 128), lambda b, h, i: (qrow(b, h, i), h)),
                pl.BlockSpec((S, 128), lambda b, h, i: (b, 2 * h)),
                pl.BlockSpec((S, 128), lambda b, h, i: (b, 2 * h + 1)),
                pl.BlockSpec((S, 128), lambda b, h, i: (b, 0))]
    return _pc(body, name="mla_attn_fwd",
               out_shape=(_sds(mixin.shape, mixin.dtype), _sds((T, 512)), _sds(mixin_t.shape, mixin_t.dtype)),
               grid=(Bl, MLA_HEADS, nq), in_specs=in_specs,
               out_specs=(pl.BlockSpec((tq, 128), lambda b, h, i: (qrow(b, h, i), 4 + h)),
                          pl.BlockSpec((tq, 128), lambda b, h, i: (qrow(b, h, i), h)),
                          pl.BlockSpec((128, tq), lambda b, h, i: (4 + h, qrow(b, h, i)))),
               aliases={0: 0, 1: 2}, sem=("parallel", "parallel", "parallel"))(mixin, mixin_t, qraw, qr, kv, kv, krr)


def _attn_bwd(qraw, qr, kv, krr, lse, dmixin, Bl, S):
    T = Bl * S
    tq = _pick(S, (256, 128))
    nq = S // tq

    def body(qn_ref, qr_ref, kn_ref, v_ref, kr_ref, do_ref, lse_ref, dqn_ref, dqr_ref, dkv_ref, dkr_ref, acc_ref):
        h, i = pl.program_id(1), pl.program_id(2)
        qn = (qn_ref[...] * ATTN_SCALE).astype(BF)
        qrv = qr_ref[...]
        do32 = do_ref[...]
        do = do32.astype(BF)
        lse_col = lse_ref[:, 0:1]

        @pl.when(i == 0)
        def _():
            acc_ref[...] = jnp.zeros_like(acc_ref)

        @pl.when((i == 0) & (h == 0))
        def _():
            dkr_ref[...] = jnp.zeros_like(dkr_ref)

        for ii in range(nq):
            @pl.when(i == ii)
            def _(L=(ii + 1) * tq):
                p = jnp.exp(_attn_scores(qn, qrv, kn_ref, kr_ref, L, tq) - lse_col)
                pb = p.astype(BF)
                v = v_ref[0:L, :]
                delta = jnp.sum(do32 * _dot(pb, v, _NN), axis=-1, keepdims=True)
                ds = (p * (_dot(do, v, _NT) - delta)).astype(BF)
                dqn_ref[...] = (_dot(ds, kn_ref[0:L, :], _NN) * ATTN_SCALE).astype(dqn_ref.dtype)
                dqr_ref[...] = _dot(ds, kr_ref[0:L, :], _NN) * ATTN_SCALE
                acc_ref[0:L, 0:128] += _dot(ds, qn, _TN)
                acc_ref[0:L, 128:256] += _dot(pb, do, _TN)
                dkr_ref[0:L, :] += _dot(ds, qrv, _TN)

        @pl.when(i == nq - 1)
        def _():
            dkv_ref[...] = acc_ref[...].astype(dkv_ref.dtype)

    qrow = lambda b, h, i: b * nq + i
    in_specs = [pl.BlockSpec((tq, 128), lambda b, h, i: (qrow(b, h, i), 2 * h)),
                pl.BlockSpec((tq, 128), lambda b, h, i: (qrow(b, h, i), h)),
                pl.BlockSpec((S, 128), lambda b, h, i: (b, 2 * h)),
                pl.BlockSpec((S, 128), lambda b, h, i: (b, 2 * h + 1)),
                pl.BlockSpec((S, 128), lambda b, h, i: (b, 0)),
                pl.BlockSpec((tq, 128), lambda b, h, i: (qrow(b, h, i), 4 + h)),
                pl.BlockSpec((tq, 128), lambda b, h, i: (qrow(b, h, i), h))]
    out_shape = (_sds((T, 1024), BF), _sds((T, 512)), _sds((T, 1024), BF), _sds((T, 128)))
    out_specs = (pl.BlockSpec((tq, 128), lambda b, h, i: (qrow(b, h, i), 2 * h)),
                 pl.BlockSpec((tq, 128), lambda b, h, i: (qrow(b, h, i), h)),
                 pl.BlockSpec((S, 256), lambda b, h, i: (b, h)),
                 pl.BlockSpec((S, 128), lambda b, h, i: (b, 0)))
    return _pc(body, name="mla_attn_bwd", out_shape=out_shape, grid=(Bl, MLA_HEADS, nq), in_specs=in_specs,
               out_specs=out_specs, scratch=[pltpu.VMEM((S, 256), F32)],
               sem=("arbitrary", "arbitrary", "arbitrary"))(qraw, qr, kv, kv, krr, dmixin, lse)


def _ln1_fwd(x, mix, g, b):
    T, D = x.shape
    tt = _pick(T, (512, 256))

    def body(x_ref, m_ref, g_ref, b_ref, o_ref, ob_ref, ot_ref):
        h = _layernorm(ALPHA * x_ref[...] + m_ref[...], g_ref[...], b_ref[...])
        o_ref[...] = h
        hb = h.astype(ob_ref.dtype)
        ob_ref[...] = hb
        ot_ref[...] = hb.T

    blk = pl.BlockSpec((tt, D), lambda i: (i, 0))
    par = pl.BlockSpec((1, D), lambda i: (0, 0))
    return _pc(body, name="ln1_fwd", out_shape=(_sds((T, D)), _sds((T, D), BF), _sds((D, T), BF)), grid=(T // tt,),
               in_specs=[blk, blk, par, par], out_specs=(blk, blk, pl.BlockSpec((D, tt), lambda i: (0, i))),
               sem=("parallel",))(x, mix, g, b)


def _ln1_bwd(x, mix, g, b, dh):
    T, D = x.shape
    tt = _pick(T, (256,))

    def body(x_ref, m_ref, g_ref, b_ref, d_ref, dx_ref, dm_ref, acc_ref):
        @pl.when(pl.program_id(0) == 0)
        def _():
            acc_ref[...] = jnp.zeros_like(acc_ref)

        f = lambda xx, mm, gg, bb: _layernorm(ALPHA * xx + mm, gg, bb)
        _, vjp = jax.vjp(f, x_ref[...], m_ref[...], g_ref[...], b_ref[...])
        dx, dm, dg, db = vjp(d_ref[...])
        dx_ref[...] = dx
        dm_ref[...] = dm.astype(dm_ref.dtype)
        acc_ref[0:1, :] += dg
        acc_ref[1:2, :] += db

    blk = pl.BlockSpec((tt, D), lambda i: (i, 0))
    par = pl.BlockSpec((1, D), lambda i: (0, 0))
    return _pc(body, name="ln1_bwd", out_shape=(_sds((T, D)), _sds((T, D), BF), _sds((8, D))), grid=(T // tt,),
               in_specs=[blk, blk, par, par, blk], out_specs=(blk, blk, pl.BlockSpec((8, D), lambda i: (0, 0))),
               sem=("arbitrary",))(x, mix, g, b, dh)


def _ffn_act_fwd(ug, uu, wg, wu, bg, bu, Bl, S):
    T, C = ug.shape
    cb = _pick(C, (256, 128))
    kk = wg.shape[0]

    def body(g_ref, u_ref, wg_ref, wu_ref, bg_ref, bu_ref, o_ref, ot_ref):
        cg = _conv(g_ref[...], wg_ref[...]) + bg_ref[...]
        cu = _conv(u_ref[...], wu_ref[...]) + bu_ref[...]
        a = (_silu(cg) * cu).astype(o_ref.dtype)
        o_ref[...] = a
        ot_ref[...] = a.T

    blk = pl.BlockSpec((S, cb), lambda b, j: (b, j))
    wblk = pl.BlockSpec((kk, cb), lambda b, j: (0, j))
    bblk = pl.BlockSpec((1, cb), lambda b, j: (0, j))
    return _pc(body, name="ffn_act_fwd", out_shape=(_sds((T, C), BF), _sds((C, T), BF)), grid=(Bl, C // cb),
               in_specs=[blk, blk, wblk, wblk, bblk, bblk],
               out_specs=(blk, pl.BlockSpec((cb, S), lambda b, j: (j, b))), sem=("parallel", "parallel"))(
        ug, uu, wg, wu, bg, bu)


def _ffn_act_bwd(ug, uu, wg, wu, bg, bu, dact, Bl, S):
    T, C = ug.shape
    cb = _pick(C, (256, 128))
    kk = wg.shape[0]

    def body(g_ref, u_ref, wg_ref, wu_ref, bg_ref, bu_ref, d_ref,
             dg_ref, du_ref, dwg_ref, dwu_ref, dbg_ref, dbu_ref):
        first = pl.program_id(1) == 0
        xg, xu, wgv, wuv = g_ref[...], u_ref[...], wg_ref[...], wu_ref[...]
        cg = _conv(xg, wgv) + bg_ref[...]
        cu = _conv(xu, wuv) + bu_ref[...]
        _, vjp = jax.vjp(lambda a, b: _silu(a) * b, cg, cu)
        dcg, dcu = vjp(d_ref[...].astype(F32))
        dg_ref[...] = _conv_bwd_x(dcg, wgv).astype(dg_ref.dtype)
        du_ref[...] = _conv_bwd_x(dcu, wuv).astype(du_ref.dtype)
        _conv_bwd_w(xg, dcg, dwg_ref, first)
        _conv_bwd_w(xu, dcu, dwu_ref, first)
        dbg_ref[...] = jnp.where(first, 0.0, dbg_ref[...]) + jnp.sum(dcg, axis=0, keepdims=True)
        dbu_ref[...] = jnp.where(first, 0.0, dbu_ref[...]) + jnp.sum(dcu, axis=0, keepdims=True)

    blk = pl.BlockSpec((S, cb), lambda j, b: (b, j))
    wblk = pl.BlockSpec((kk, cb), lambda j, b: (0, j))
    bblk = pl.BlockSpec((1, cb), lambda j, b: (0, j))
    out_shape = (_sds((T, C), BF), _sds((T, C), BF), _sds((kk, C)), _sds((kk, C)), _sds((1, C)), _sds((1, C)))
    return _pc(body, name="ffn_act_bwd", out_shape=out_shape, grid=(C // cb, Bl),
               in_specs=[blk, blk, wblk, wblk, bblk, bblk, blk],
               out_specs=(blk, blk, wblk, wblk, bblk, bblk), sem=("parallel", "arbitrary"))(
        ug, uu, wg, wu, bg, bu, dact)


def _head(h1, ffn, gpre, pp, tgt, bg, g2, b2):
    T, D = h1.shape
    tt = _pick(T, (256,))

    def body(h_ref, f_ref, gp_ref, pp_ref, t_ref, bg_ref, g2_ref, b2_ref, df_ref, dfb_ref, dgp_ref, dpp_ref, acc_ref):
        @pl.when(pl.program_id(0) == 0)
        def _():
            acc_ref[...] = jnp.zeros_like(acc_ref)

        h, tg = h_ref[...], t_ref[...]

        def loss_fn(f, gp, p_, bgv, g2v, b2v):
            pre = ALPHA * h + f + jax.nn.sigmoid(gp + bgv) * p_
            err = _layernorm(pre, g2v, b2v) - tg
            return 0.5 * jnp.sum(jnp.mean(err * err, axis=-1, keepdims=True))

        loss, grads = jax.value_and_grad(loss_fn, argnums=(0, 1, 2, 3, 4, 5))(
            f_ref[...], gp_ref[...], pp_ref[...], bg_ref[...], g2_ref[...], b2_ref[...])
        df_ref[...] = grads[0]
        dfb_ref[...] = grads[0].astype(dfb_ref.dtype)
        dgp_ref[...] = grads[1].astype(dgp_ref.dtype)
        dpp_ref[...] = grads[2].astype(dpp_ref.dtype)
        acc_ref[0:1, :] += grads[3]
        acc_ref[1:2, :] += grads[4]
        acc_ref[2:3, :] += grads[5]
        acc_ref[3:4, :] += jnp.broadcast_to(loss, (1, D))

    blk = pl.BlockSpec((tt, D), lambda i: (i, 0))
    par = pl.BlockSpec((1, D), lambda i: (0, 0))
    o_bf = _sds((T, D), BF)
    return _pc(body, name="loss_head", out_shape=(_sds((T, D)), o_bf, o_bf, o_bf, _sds((8, D))),
               grid=(T // tt,), in_specs=[blk, blk, blk, blk, blk, par, par, par],
               out_specs=(blk, blk, blk, blk, pl.BlockSpec((8, D), lambda i: (0, 0))), sem=("arbitrary",))(
        h1, ffn, gpre, pp, tgt, bg, g2, b2)


def _adamw(parts, w, m, v, name):
    R, C = w.shape
    tr = R if R <= 512 else _pick(R, (256,))

    def body(p_ref, w_ref, m_ref, v_ref, g_ref, d_ref, nm_ref, nv_ref):
        g = p_ref[0].astype(F32)
        for j in range(1, N_DEV):
            g = g + p_ref[j].astype(F32)
        mm = ADAM_B1 * m_ref[...] + (1.0 - ADAM_B1) * g
        vv = ADAM_B2 * v_ref[...] + (1.0 - ADAM_B2) * jnp.square(g)
        m_hat = mm / (1.0 - ADAM_B1 ** ADAM_STEP)
        v_hat = vv / (1.0 - ADAM_B2 ** ADAM_STEP)
        g_ref[...] = g
        d_ref[...] = -ADAM_LR * (m_hat / (jnp.sqrt(v_hat) + ADAM_EPS) + ADAM_WD * w_ref[...])
        nm_ref[...] = mm
        nv_ref[...] = vv

    blk = pl.BlockSpec((tr, C), lambda i: (i, 0))
    o = _sds((R, C))
    return _pc(body, name=name, out_shape=(o, o, o, o), grid=(R // tr,),
               in_specs=[pl.BlockSpec((N_DEV, tr, C), lambda i: (0, i, 0)), blk, blk, blk],
               out_specs=(blk, blk, blk, blk), sem=("parallel",))(parts, w, m, v)


def _prep(x, p, xchg=None):
    T, D = x.shape
    Dp = p.shape[1]
    tt = _pick(T, (512, 256))

    def body(x_ref, p_ref, xb_ref, xt_ref, pb_ref, pt_ref):
        xb = x_ref[...].astype(BF)
        xb_ref[...] = xb
        xt_ref[...] = xb.T
        pb = p_ref[...].astype(BF)
        pb_ref[...] = pb
        pt_ref[...] = pb.T

    row = lambda w: pl.BlockSpec((tt, w), lambda i: (i, 0))
    col = lambda w: pl.BlockSpec((w, tt), lambda i: (0, i))
    return _pc(body, name="prep_inputs",
               out_shape=(_sds((T, D), BF), _sds((D, T), BF), _sds((T, Dp), BF), _sds((Dp, T), BF)),
               grid=(T // tt,), in_specs=[row(D), row(Dp)], out_specs=(row(D), col(D), row(Dp), col(Dp)),
               sem=("parallel",), xchg=xchg)(x, p)


SMALL_ORDER = ("gdn_a_log", "gdn_dt_bias", "gdn_norm_g", "mla_q_norm_g", "mla_kv_norm_g", "ln1_g", "ln1_b",
               "ffn_conv_b", "ple_b_gate", "ln2_g", "ln2_b")
EARLY = ("w_in", "gdn_conv_w")
LATE = ("mla_w_q_up", "mla_w_kv_up", "w_out", "ffn_w_up", "ffn_conv_w", "ffn_w_down", "ple_w_gate", "ple_w_proj")
GRADS_EARLY = ("w_out", "ffn_w_up", "ffn_conv_w", "ffn_w_down", "ple_w_gate", "ple_w_proj")
GRADS_LATE = ("w_in", "gdn_conv_w", "mla_w_q_up", "mla_w_kv_up")


def _pad_lanes(v, n=128):
    return jnp.pad(v, ((0, 0), (0, n - v.shape[1])))


def _local_step(x, p, tgt, W, sp, Bl, S, early_weights=None, late_weights=None, early_grads=None, late_grads=None):
    T = Bl * S
    W = dict(W)
    prep = _prep(x, p, xchg=None if early_weights is None else early_weights[:2])
    xb, xT, pb, pT = prep[:4]
    if early_weights is not None:
        W.update(early_weights[2](prep[4:]))
    w_in = W["w_in"]
    wqkv, wz = w_in[:, :1536], w_in[:, 1536:2048]
    z64 = jnp.zeros((w_in.shape[0], 64), w_in.dtype)
    z124 = jnp.zeros((w_in.shape[0], 124), w_in.dtype)
    wsm = jnp.concatenate([w_in[:, 2056:2440], w_in[:, 2696:2760], z64, w_in[:, 2440:2696],
                           w_in[:, 2048:2052], z124, w_in[:, 2052:2056], z124], axis=1)
    alog, dtb = _pad_lanes(sp["gdn_a_log"]), _pad_lanes(sp["gdn_dt_bias"])
    cos, sin = _rope_tables(S)

    qkv = _mm(xb, wqkv, "nn", name="mm_qkv")
    z = _mm(xb, wz, "nn", name="mm_z")
    sm = _mm(xb, wsm, "nn", name="mm_sm")
    qkvn = _gdn_pre_fwd(qkv, W["gdn_conv_w"], Bl, S)
    gdn_out = _gdn_fwd(qkvn, sm, alog, dtb, Bl, S, xchg=None if late_weights is None else late_weights[:2])
    o, tinv_s, sst_s, w_s, u_s, gc_s, beta_s = gdn_out[:7]
    if late_weights is not None:
        W.update(late_weights[2](gdn_out[7:]))
    wq = jnp.pad(W["mla_w_q_up"].reshape(-1, MLA_HEADS, 192), ((0, 0), (0, 0), (0, 64))).reshape(-1, 1024)
    wkv = W["mla_w_kv_up"]
    C = W["ffn_w_down"].shape[0]
    wup_g, wup_u = W["ffn_w_up"][:, :C], W["ffn_w_up"][:, C:]
    cw_g, cw_u = W["ffn_conv_w"][:, :C], W["ffn_conv_w"][:, C:]
    cb_g, cb_u = sp["ffn_conv_b"][:, :C], sp["ffn_conv_b"][:, C:]
    mixin, mixin_t = _gdn_post(o, z, sp["gdn_norm_g"], T)
    cqn, cqn_t = _rms_fwd(sm, sp["mla_q_norm_g"], 0, "rms_q_fwd")
    ckvn, ckvn_t = _rms_fwd(sm, sp["mla_kv_norm_g"], 2, "rms_kv_fwd")
    qraw = _mm(cqn, wq, "nn", name="mm_qup")
    kv = _mm(ckvn, wkv, "nn", name="mm_kvup", out_dtype=BF)
    qr, krr = _rope_fwd(qraw, sm, cos, sin, S)
    mixin, lse, mixin_t = _attn_fwd(mixin, mixin_t, qraw, qr, kv, krr, Bl, S)
    mix = _mm(mixin, W["w_out"], "nn", name="mm_out")
    h1, h1b, h1T = _ln1_fwd(x, mix, sp["ln1_g"], sp["ln1_b"])
    ug = _mm(h1b, wup_g, "nn", name="mm_up_gate")
    uu = _mm(h1b, wup_u, "nn", name="mm_up_up")
    act, act_t = _ffn_act_fwd(ug, uu, cw_g, cw_u, cb_g, cb_u, Bl, S)
    ffn = _mm(act, W["ffn_w_down"], "nn", name="mm_down")
    gpre = _mm(h1b, W["ple_w_gate"], "nn", name="mm_gate")
    pp = _mm(pb, W["ple_w_proj"], "nn", name="mm_proj")

    dffn, dffnb, dgpre, dpp, hacc = _head(h1, ffn, gpre, pp, tgt, sp["ple_b_gate"], sp["ln2_g"], sp["ln2_b"])
    loss = hacc[3, 0]
    gW, gs = {}, {}
    gs["ple_b_gate"], gs["ln2_g"], gs["ln2_b"] = hacc[0:1], hacc[1:2], hacc[2:3]
    gW["ple_w_proj"] = _mm(pT, dpp, "nn", name="mm_dproj", out_dtype=BF)
    gW["ple_w_gate"] = _mm(h1T, dgpre, "nn", name="mm_dgate", out_dtype=BF)
    gW["ffn_w_down"] = _mm(act_t, dffnb, "nn", name="mm_ddown", out_dtype=BF)
    dact = _mm(dffnb, W["ffn_w_down"], "nt", name="mm_dact")
    du_g, du_u, dcw_g, dcw_u, dcb_g, dcb_u = _ffn_act_bwd(ug, uu, cw_g, cw_u, cb_g, cb_u, dact, Bl, S)
    gW["ffn_conv_w"] = jnp.concatenate([dcw_g, dcw_u], axis=1)
    gs["ffn_conv_b"] = jnp.concatenate([dcb_g, dcb_u], axis=1)
    gW["ffn_w_up"] = jnp.concatenate([_mm(h1T, du_g, "nn", name="mm_dup_gate", out_dtype=BF),
                                      _mm(h1T, du_u, "nn", name="mm_dup_up", out_dtype=BF)], axis=1)
    dh1 = _mm(dgpre, W["ple_w_gate"], "nt", name="mm_dh1_gate", add=dffn, add_scale=ALPHA)
    dh1 = _mm(du_g, wup_g, "nt", name="mm_dh1_upg", add=dh1)
    dh1 = _mm(du_u, wup_u, "nt", name="mm_dh1_upu", add=dh1)
    dxa, dmix, acc1 = _ln1_bwd(x, mix, sp["ln1_g"], sp["ln1_b"], dh1)
    gs["ln1_g"], gs["ln1_b"] = acc1[0:1], acc1[1:2]
    gW["w_out"] = _mm(mixin_t, dmix, "nn", name="mm_dwout", out_dtype=BF)
    dmixin = _mm(dmix, W["w_out"], "nt", name="mm_dmixin")
    do, dz, gacc = _gdn_post_bwd(o, z, sp["gdn_norm_g"], dmixin, T)
    gs["gdn_norm_g"] = gacc[0:1]
    bwd_out = _gdn_bwd(qkvn, sm, alog, dtb, gc_s, beta_s, tinv_s, sst_s, w_s, u_s, do, Bl, S,
                       xchg=None if early_grads is None else early_grads({n: gW[n] for n in GRADS_EARLY}))
    dqkvn, dsm, cacc = bwd_out[:3]
    early_recv = bwd_out[3:]
    gs["gdn_a_log"], gs["gdn_dt_bias"] = cacc[0:1, :GDN_HEADS], cacc[1:2, :GDN_HEADS]
    dqkv, gW["gdn_conv_w"] = _gdn_pre_bwd(qkv, W["gdn_conv_w"], dqkvn, Bl, S)
    dqraw, dqr, dkv, dkrr = _attn_bwd(qraw, qr, kv, krr, lse, dmixin, Bl, S)
    dqraw, dsm = _rope_bwd(dqraw, dsm, dqr, dkrr, cos, sin, S)
    gwq = _mm(cqn_t, dqraw, "nn", name="mm_dwq", out_dtype=BF)
    gW["mla_w_q_up"] = gwq.reshape(-1, MLA_HEADS, 256)[:, :, :192].reshape(-1, MLA_HEADS * 192)
    gW["mla_w_kv_up"] = _mm(ckvn_t, dkv, "nn", name="mm_dwkv", out_dtype=BF)
    dcqn = _mm(dqraw, wq, "nt", name="mm_dcqn")
    dckvn = _mm(dkv, wkv, "nt", name="mm_dckvn")
    dsm, qacc = _rms_bwd(sm, sp["mla_q_norm_g"], dcqn, dsm, 0, "rms_q_bwd")
    dsm, kacc = _rms_bwd(sm, sp["mla_kv_norm_g"], dckvn, dsm, 2, "rms_kv_bwd")
    gs["mla_q_norm_g"], gs["mla_kv_norm_g"] = qacc[0:1], kacc[0:1]
    gqkv = _mm(xT, dqkv, "nn", name="mm_dwqkv", out_dtype=BF)
    gz = _mm(xT, dz, "nn", name="mm_dwz", out_dtype=BF)
    gsm = _mm(xT, dsm, "nn", name="mm_dwsm", out_dtype=BF)
    gW["w_in"] = jnp.concatenate([gqkv, gz, gsm[:, 768:772], gsm[:, 896:900], gsm[:, 0:384], gsm[:, 512:768],
                                  gsm[:, 384:448]], axis=1)
    dx = _mm(dqkv, wqkv, "nt", name="mm_dx_qkv", add=dxa, xchg=None if late_grads is None else late_grads(gW, gs))
    late_recv = ()
    if late_grads is not None:
        dx, late_recv = dx[0], dx[1:]
    dx = _mm(dz, wz, "nt", name="mm_dx_z", add=dx)
    dx = _mm(dsm, wsm, "nt", name="mm_dx_sm", add=dx)
    return loss, dx, gW, gs, early_recv, late_recv


COL_SHARDED = ("w_in", "mla_w_q_up", "mla_w_kv_up", "ffn_w_up", "ple_w_proj", "gdn_conv_w", "ffn_conv_w")
SHARDED = EARLY + LATE
WEIGHTS = ("w_in", "gdn_conv_w", "gdn_a_log", "gdn_dt_bias", "gdn_norm_g", "mla_q_norm_g", "mla_w_q_up",
           "mla_kv_norm_g", "mla_w_kv_up", "w_out", "ln1_g", "ln1_b", "ffn_w_up", "ffn_conv_w", "ffn_conv_b",
           "ffn_w_down", "ple_w_gate", "ple_b_gate", "ple_w_proj", "ln2_g", "ln2_b")
CONV = ("gdn_conv_w", "ffn_conv_w")
SMALL_ROWS, SMALL_LANES = 96, 128


def _gathered_to_full(name, g):
    if name in COL_SHARDED:
        return jnp.transpose(g, (1, 0, 2)).reshape(g.shape[1], -1)
    return g.reshape(-1, g.shape[-1])


def _full_to_blocks(name, gfull, shard_shape):
    r, c = shard_shape
    if name in COL_SHARDED:
        return jnp.transpose(gfull.reshape(r, N_DEV, c), (1, 0, 2))
    return gfull.reshape(N_DEV, r, c)


def _pack_small(d):
    flat = jnp.concatenate([d[n].reshape(-1) for n in SMALL_ORDER])
    return jnp.pad(flat, (0, SMALL_ROWS * SMALL_LANES - flat.shape[0])).reshape(SMALL_ROWS, SMALL_LANES)


def _unpack_small(packed, shapes):
    flat = packed.reshape(-1)
    out, off = {}, 0
    for n in SMALL_ORDER:
        sz = math.prod(shapes[n])
        out[n] = flat[off:off + sz].reshape(shapes[n])
        off += sz
    return out


def kernel(x, p, w_in, gdn_conv_w, gdn_a_log, gdn_dt_bias, gdn_norm_g, mla_q_norm_g, mla_w_q_up, mla_kv_norm_g, mla_w_kv_up, w_out, ln1_g, ln1_b, ffn_w_up, ffn_conv_w, ffn_conv_b, ffn_w_down, ple_w_gate, ple_b_gate, ple_w_proj, ln2_g, ln2_b, loss_target, m_w_in, m_gdn_conv_w, m_gdn_a_log, m_gdn_dt_bias, m_gdn_norm_g, m_mla_q_norm_g, m_mla_w_q_up, m_mla_kv_norm_g, m_mla_w_kv_up, m_w_out, m_ln1_g, m_ln1_b, m_ffn_w_up, m_ffn_conv_w, m_ffn_conv_b, m_ffn_w_down, m_ple_w_gate, m_ple_b_gate, m_ple_w_proj, m_ln2_g, m_ln2_b, v_w_in, v_gdn_conv_w, v_gdn_a_log, v_gdn_dt_bias, v_gdn_norm_g, v_mla_q_norm_g, v_mla_w_q_up, v_mla_kv_norm_g, v_mla_w_kv_up, v_w_out, v_ln1_g, v_ln1_b, v_ffn_w_up, v_ffn_conv_w, v_ffn_conv_b, v_ffn_w_down, v_ple_w_gate, v_ple_b_gate, v_ple_w_proj, v_ln2_g, v_ln2_b):
    loc = dict(locals())
    wts = {n: loc[n] for n in WEIGHTS}
    ms = {n: loc["m_" + n] for n in WEIGHTS}
    vs = {n: loc["v_" + n] for n in WEIGHTS}
    Bl, S, D = x.shape
    T = Bl * S

    wire = lambda n: wts[n][0] if n in CONV else wts[n][0].astype(BF)
    gather = lambda names: ([wire(n) for n in names], ["ag"] * len(names),
                            lambda res: {n: _gathered_to_full(n, g) for n, g in zip(names, res)})
    blocks = lambda g, names: [_full_to_blocks(n, g[n], wts[n].shape[1:]) for n in names]
    pack_early = lambda g: (blocks(g, GRADS_EARLY), ["a2a"] * len(GRADS_EARLY))
    pack_late = lambda g, gs: (blocks(g, GRADS_LATE) + [_pack_small(gs)], ["a2a"] * len(GRADS_LATE) + ["ag"])
    sp = {n: wts[n].reshape(1, -1) for n in SMALL_ORDER}

    loss, dx, gW, gs, early_recv, late_recv = _local_step(
        x.reshape(T, D), p.reshape(T, -1), loss_target.reshape(T, D), {}, sp, Bl, S,
        early_weights=gather(EARLY), late_weights=gather(LATE), early_grads=pack_early, late_grads=pack_late)

    res = {}
    for n, parts in list(zip(GRADS_EARLY, early_recv)) + list(zip(GRADS_LATE, late_recv[:-1])):
        res[n] = tuple(t[None] for t in _adamw(parts, wts[n][0], ms[n][0], vs[n][0], "adamw_" + n))
    small = _adamw(late_recv[-1], _pack_small(wts), _pack_small(ms), _pack_small(vs), "adamw_small")
    shapes = {n: wts[n].shape for n in SMALL_ORDER}
    small = [_unpack_small(t, shapes) for t in small]
    for n in SMALL_ORDER:
        res[n] = tuple(t[n] for t in small)

    loss = lax.psum(loss, ("x", "y", "c"))
    outs = [loss, dx.reshape(Bl, S, D)]
    for k in range(4):
        outs += [res[n][k] for n in WEIGHTS]
    return tuple(outs)
```

```python
import math

import jax
import jax.numpy as jnp
from jax import lax
from jax.experimental import pallas as pl
from jax.experimental.pallas import tpu as pltpu

F32 = jnp.float32
BF = jnp.bfloat16
_MXU = jnp.bfloat16
_INTERPRET = None
_VMEM_LIMIT = 56 * 1024 * 1024
HI = lax.Precision.HIGHEST

N_DEV = 8
CHUNK = 64
GDN_HEADS = 4
GDN_DK = 128
MLA_HEADS = 4
MLA_NOPE = 128
MLA_ROPE = 64
ROPE_THETA = 10000.0
ALPHA = 2.0 ** 0.25
NORM_EPS = 1e-6
ATTN_SCALE = (MLA_NOPE + MLA_ROPE) ** -0.5
DCAT_QKV, DCAT_Z, DCAT_W = 1024, 2560, 3072
ADAM_LR, ADAM_B1, ADAM_B2, ADAM_EPS, ADAM_WD, ADAM_STEP = 0.001, 0.9, 0.999, 1e-08, 0.01, 10
MESH = pl.DeviceIdType.MESH
ANY = pl.BlockSpec(memory_space=pl.ANY)
_NN = (((1,), (0,)), ((), ()))
_NT = (((1,), (1,)), ((), ()))
_TN = (((0,), (0,)), ((), ()))


def _sds(shape, dtype=F32):
    return jax.ShapeDtypeStruct(tuple(shape), dtype)


def _pick(n, cands):
    for c in cands:
        if n % c == 0:
            return c
    return n


def _xchg_out_shapes(arrs, modes):
    return [_sds((N_DEV,) + a.shape if m == "ag" else a.shape, a.dtype) for a, m in zip(arrs, modes)]


def _xchg_scratch(n):
    return [pltpu.SemaphoreType.DMA((n, N_DEV - 1)), pltpu.SemaphoreType.DMA((n, N_DEV - 1)),
            pltpu.SemaphoreType.DMA((n,))]


def _xchg_copies(ins, outs, send, recv, loc, modes):
    x, y, c = lax.axis_index("x"), lax.axis_index("y"), lax.axis_index("c")
    me = 4 * x + 2 * y + c
    copies = []
    for ai, mode in enumerate(modes):
        ag = mode == "ag"
        copies.append(pltpu.make_async_copy(ins[ai] if ag else ins[ai].at[me], outs[ai].at[me], loc.at[ai]))
        for r in range(1, N_DEV):
            px = 1 - x if (r >> 2) & 1 else x
            py = 1 - y if (r >> 1) & 1 else y
            pc = 1 - c if r & 1 else c
            peer = 4 * px + 2 * py + pc
            copies.append(pltpu.make_async_remote_copy(
                src_ref=ins[ai] if ag else ins[ai].at[peer], dst_ref=outs[ai].at[me],
                send_sem=send.at[ai, r - 1], recv_sem=recv.at[ai, r - 1],
                device_id=(px, py, pc), device_id_type=MESH))
    return copies


def _pc(body, *, name, out_shape, grid=None, in_specs=None, out_specs=None, scratch=(), sem=None, aliases=None,
        xchg=None):
    kw = {}
    if _INTERPRET is not None:
        kw["interpret"] = _INTERPRET
    single = not isinstance(out_shape, (tuple, list))
    out_shape = [out_shape] if single else list(out_shape)
    if out_specs is not None:
        out_specs = [out_specs] if single else list(out_specs)
    scratch = list(scratch)
    if xchg is not None:
        xarrs, xmodes = xchg
        n_in, n_out, n_scr, nx = len(in_specs), len(out_shape), len(scratch), len(xarrs)
        single = False
        inner = body

        def body(*refs):
            ins, xins = refs[:n_in], refs[n_in:n_in + nx]
            outs = refs[n_in + nx:n_in + nx + n_out]
            xouts = refs[n_in + nx + n_out:n_in + 2 * nx + n_out]
            scr = refs[n_in + 2 * nx + n_out:n_in + 2 * nx + n_out + n_scr]
            send, recv, loc = refs[n_in + 2 * nx + n_out + n_scr:]
            first = last = None
            for d, g in enumerate(grid):
                f, l = pl.program_id(d) == 0, pl.program_id(d) == g - 1
                first, last = (f, l) if first is None else (first & f, last & l)

            @pl.when(first)
            def _():
                for cp in _xchg_copies(xins, xouts, send, recv, loc, xmodes):
                    cp.start()

            inner(*ins, *outs, *scr)

            @pl.when(last)
            def _():
                for cp in _xchg_copies(xins, xouts, send, recv, loc, xmodes):
                    cp.wait()

        in_specs = list(in_specs) + [ANY] * nx
        out_specs = out_specs + [ANY] * nx
        out_shape = out_shape + _xchg_out_shapes(xarrs, xmodes)
        scratch = scratch + _xchg_scratch(nx)
        sem = ("arbitrary",) * len(grid)
    if grid is not None:
        kw["grid"] = grid
    if in_specs is not None:
        kw["in_specs"] = in_specs
    if out_specs is not None:
        kw["out_specs"] = out_specs[0] if single else tuple(out_specs)
    cp = dict(vmem_limit_bytes=_VMEM_LIMIT)
    if sem is not None:
        cp["dimension_semantics"] = sem
    call = pl.pallas_call(body, name=name, out_shape=out_shape[0] if single else tuple(out_shape),
                          scratch_shapes=scratch, input_output_aliases=aliases or {},
                          compiler_params=pltpu.CompilerParams(**cp), **kw)
    if xchg is None:
        return call
    return lambda *ins: call(*ins, *xchg[0])


def _exchange(arrs, modes, name):
    n = len(arrs)

    def body(*refs):
        copies = _xchg_copies(refs[:n], refs[n:2 * n], *refs[2 * n:], modes)
        for cp in copies:
            cp.start()
        for cp in copies:
            cp.wait()

    return _pc(body, name=name, out_shape=tuple(_xchg_out_shapes(arrs, modes)), in_specs=[ANY] * n,
               out_specs=tuple([ANY] * n), scratch=_xchg_scratch(n))(*arrs)


_FULL_K_MAX = 3072


def _mm(a, b, mode, *, name, add=None, add_scale=1.0, out_dtype=F32, xchg=None):
    (M, K) = a.shape
    (K2, N) = b.shape if mode == "nn" else b.shape[::-1]
    assert K == K2, (a.shape, b.shape, mode)
    tm = _pick(M, (1024, 1408, 512, 384, 256, 128))
    tn = _pick(N, (1024, 2816, 1408, 768, 512, 384, 256, 128))
    tk = K if K <= _FULL_K_MAX else _pick(K, (2048, 1408, 1024, 512) if tn <= 1408 else (1024, 512))
    nk = K // tk
    dims = _NN if mode == "nn" else _NT
    has_add = add is not None

    def finish(r, add_ref, o_ref):
        if has_add:
            r = r + add_scale * add_ref[...].astype(F32)
        o_ref[...] = r.astype(out_dtype)

    if nk == 1:
        def body(a_ref, b_ref, *rest):
            r = lax.dot_general(a_ref[...].astype(_MXU), b_ref[...].astype(_MXU), dims, preferred_element_type=F32)
            finish(r, rest[0] if has_add else None, rest[-1])

        a_spec = pl.BlockSpec((tm, K), lambda i, j: (i, 0))
        b_spec = (pl.BlockSpec((K, tn), lambda i, j: (0, j)) if mode == "nn"
                  else pl.BlockSpec((tn, K), lambda i, j: (j, 0)))
        o_spec = pl.BlockSpec((tm, tn), lambda i, j: (i, j))
        grid, sem, scratch = (M // tm, N // tn), ("parallel", "parallel"), []
    else:
        def body(a_ref, b_ref, *rest):
            o_ref, acc = rest[-2], rest[-1]
            k = pl.program_id(2)

            @pl.when(k == 0)
            def _():
                acc[...] = jnp.zeros_like(acc)

            acc[...] += lax.dot_general(a_ref[...].astype(_MXU), b_ref[...].astype(_MXU), dims,
                                        preferred_element_type=F32)

            @pl.when(k == nk - 1)
            def _():
                finish(acc[...], rest[0] if has_add else None, o_ref)

        a_spec = pl.BlockSpec((tm, tk), lambda i, j, k: (i, k))
        b_spec = (pl.BlockSpec((tk, tn), lambda i, j, k: (k, j)) if mode == "nn"
                  else pl.BlockSpec((tn, tk), lambda i, j, k: (j, k)))
        o_spec = pl.BlockSpec((tm, tn), lambda i, j, k: (i, j))
        grid, sem, scratch = (M // tm, N // tn, nk), ("parallel", "parallel", "arbitrary"), [pltpu.VMEM((tm, tn), F32)]

    ins = [a, b] + ([add] if has_add else [])
    specs = [a_spec, b_spec] + ([o_spec] if has_add else [])
    return _pc(body, name=name, out_shape=_sds((M, N), out_dtype), grid=grid, in_specs=specs, out_specs=o_spec,
               scratch=scratch, sem=sem, xchg=xchg)(*ins)


def _dot(a, b, dims):
    return lax.dot_general(a.astype(_MXU), b.astype(_MXU), dims, preferred_element_type=F32)


def _split(a):
    hi = a.astype(BF)
    lo = (a - hi.astype(F32)).astype(BF)
    return hi, lo


def _dot3(a, b, dims):
    ah, al = _split(a)
    bh, bl = _split(b)
    d = lambda u, v: lax.dot_general(u, v, dims, preferred_element_type=F32)
    return d(ah, bh) + (d(ah, bl) + d(al, bh))


def _softplus(x):
    return jnp.maximum(x, 0.0) + jnp.log1p(jnp.exp(-jnp.abs(x)))


def _silu(x):
    return x * jax.nn.sigmoid(x)


def _rmsnorm(x, g):
    return x * lax.rsqrt(jnp.mean(x * x, axis=-1, keepdims=True) + NORM_EPS) * g


def _layernorm(x, g, b):
    mu = jnp.mean(x, axis=-1, keepdims=True)
    xc = x - mu
    var = jnp.mean(xc * xc, axis=-1, keepdims=True)
    return xc * lax.rsqrt(var + NORM_EPS) * g + b


def _shift_dn(x, s):
    if s == 0:
        return x
    t = lax.broadcasted_iota(jnp.int32, x.shape, 0)
    return jnp.where(t >= s, pltpu.roll(x, s, 0), 0.0)


def _shift_up(x, s):
    if s == 0:
        return x
    n = x.shape[0]
    t = lax.broadcasted_iota(jnp.int32, x.shape, 0)
    return jnp.where(t < n - s, pltpu.roll(x, n - s, 0), 0.0)


def _taps(x, kk):
    return [_shift_dn(x, kk - 1 - j) for j in range(kk)]


def _conv(x, w, taps=None):
    kk = w.shape[0]
    taps = _taps(x, kk) if taps is None else taps
    y = w[kk - 1:kk, :] * taps[kk - 1]
    for j in range(kk - 1):
        y = y + w[j:j + 1, :] * taps[j]
    return y


def _conv_bwd_x(dy, w):
    kk = w.shape[0]
    dx = w[kk - 1:kk, :] * dy
    for j in range(kk - 1):
        dx = dx + w[j:j + 1, :] * _shift_up(dy, kk - 1 - j)
    return dx


def _conv_bwd_w(taps, dy, dw_ref, first):
    kk = dw_ref.shape[0]
    for j in range(kk):
        r = jnp.sum(dy * taps[j], axis=0, keepdims=True)
        prev = jnp.where(first, 0.0, dw_ref[j:j + 1, :])
        dw_ref[j:j + 1, :] = prev + r


def _gdn_post_conv(c, j):
    h = _silu(c)
    hn = h * lax.rsqrt(jnp.sum(h * h, axis=-1, keepdims=True) + NORM_EPS)
    return jnp.where(j < GDN_HEADS, hn * (GDN_DK ** -0.5), jnp.where(j < 2 * GDN_HEADS, hn, h))


def _gdn_pre_fwd(qkv, conv_w, Bl, S):
    T, C = qkv.shape
    nj = C // 128

    def body(x_ref, w_ref, o_ref):
        o_ref[...] = _gdn_post_conv(_conv(x_ref[...], w_ref[...]), pl.program_id(1))

    blk = pl.BlockSpec((S, 128), lambda b, j: (b, j))
    return _pc(body, name="gdn_pre_fwd", out_shape=_sds((T, C)), grid=(Bl, nj),
               in_specs=[blk, pl.BlockSpec((conv_w.shape[0], 128), lambda b, j: (0, j))], out_specs=blk,
               sem=("parallel", "parallel"))(qkv, conv_w)


def _gdn_pre_bwd(qkv, conv_w, dout, dcat, Bl, S):
    T, C = qkv.shape
    nj = C // 128
    kk = conv_w.shape[0]

    def body(x_ref, w_ref, d_ref, alias_ref, dx_ref, dw_ref):
        j, b = pl.program_id(0), pl.program_id(1)
        w = w_ref[...]
        taps = _taps(x_ref[...], kk)
        c = _conv(None, w, taps)
        _, vjp = jax.vjp(lambda u: _gdn_post_conv(u, j), c)
        (dc,) = vjp(d_ref[...])
        dx_ref[...] = _conv_bwd_x(dc, w).astype(dx_ref.dtype)
        _conv_bwd_w(taps, dc, dw_ref, b == 0)

    blk = pl.BlockSpec((S, 128), lambda j, b: (b, j))
    wblk = pl.BlockSpec((kk, 128), lambda j, b: (0, j))
    return _pc(body, name="gdn_pre_bwd", out_shape=(_sds(dcat.shape, dcat.dtype), _sds((kk, C))), grid=(nj, Bl),
               in_specs=[blk, wblk, blk, ANY],
               out_specs=(pl.BlockSpec((S, 128), lambda j, b: (b, DCAT_QKV // 128 + j)), wblk),
               aliases={3: 0}, sem=("parallel", "arbitrary"))(qkv, conv_w, dout, dcat)


_H = GDN_HEADS
_HC = _H * CHUNK


def _st(x):
    return jnp.concatenate([x[:, h * 128:(h + 1) * 128] for h in range(_H)], axis=0)


def _unst(x):
    return jnp.concatenate([x[h * CHUNK:(h + 1) * CHUNK] for h in range(_H)], axis=1)


def _stc(t):
    return jnp.concatenate([t[:, h:h + 1] for h in range(_H)], axis=0)


def _untile(col):
    lane = lax.broadcasted_iota(jnp.int32, (CHUNK, 128), 1)
    out = jnp.zeros((CHUNK, 128), F32)
    for h in range(_H):
        out = out + jnp.where(lane == h, col[h * CHUNK:(h + 1) * CHUNK], 0.0)
    return out


def _rowform(col):
    return jnp.broadcast_to(col, (_HC, 128)).T[0:1, :]


def _tri(n, upper=False):
    i = lax.broadcasted_iota(jnp.int32, (n, n), 0)
    j = lax.broadcasted_iota(jnp.int32, (n, n), 1)
    return jnp.where((j >= i) if upper else (j <= i), 1.0, 0.0).astype(F32)


def _gdn_chunk_common(q, k, v, gc, beta):
    Q, K, V = _st(q), _st(k), _st(v)
    B, GC = _stc(beta), _stc(gc)
    GL = jnp.concatenate([jnp.broadcast_to(gc[CHUNK - 1:CHUNK, h:h + 1], (CHUNK, 1)) for h in range(_H)], axis=0)
    ii = lax.broadcasted_iota(jnp.int32, (_HC, _HC), 0)
    jj = lax.broadcasted_iota(jnp.int32, (_HC, _HC), 1)
    same = (ii >> 6) == (jj >> 6)
    incl = same & (ii >= jj)
    strict = same & (ii > jj)
    diff = GC - _rowform(GC)
    D = jnp.where(incl, jnp.exp(jnp.where(incl, diff, 0.0)), 0.0)
    KB = K * B
    A = jnp.where(strict, _dot(KB, K, _NT) * D, 0.0)
    P = jnp.where(incl, _dot(Q, K, _NT) * D, 0.0)
    EG = jnp.exp(GC)
    ED = jnp.exp(GL - GC)
    return dict(Q=Q, K=K, V=V, B=B, GC=GC, GL=GL, incl=incl, strict=strict, D=D, KB=KB, A=A, P=P, EG=EG, ED=ED,
                QG=Q * EG, KD=K * ED)


def _neumann_inv(A):
    n = A.shape[0]
    i = lax.broadcasted_iota(jnp.int32, (n, n), 0)
    j = lax.broadcasted_iota(jnp.int32, (n, n), 1)
    N = -A
    acc = jnp.where(i == j, 1.0, 0.0) + N
    Pw = N
    for _ in range(5):
        Pw = _dot3(Pw, Pw, _NN)
        acc = acc + _dot3(acc, Pw, _NN)
    return acc


def _gates(a, alog, dtb):
    return -jnp.exp(alog) * _softplus(a + dtb)


def _hs(x, h, n=CHUNK):
    return x[h * n:(h + 1) * n]


def _per_step(Bl):
    return 2 if Bl % 2 == 0 else 1


def _gdn_fwd(qkvn, sm, alog, dtb, Bl, S, xchg=None):
    T = Bl * S
    nC = S // CHUNK
    E = _per_step(Bl)

    def body(*refs):
        S_ref = refs[-1]

        @pl.when(pl.program_id(1) == 0)
        def _():
            S_ref[...] = jnp.zeros_like(S_ref)

        for e in range(E):
            one(*[r if i in (5, 6) else r.at[e] for i, r in enumerate(refs)])

    def one(q_ref, k_ref, v_ref, a_ref, b_ref, alog_ref, dtb_ref,
            o_ref, tinv_ref, sst_ref, w_ref, u_ref, gc_ref, beta_ref, S_ref):
        g = _gates(a_ref[...], alog_ref[...], dtb_ref[...])
        beta = jax.nn.sigmoid(b_ref[...])
        gc = jnp.dot(_tri(CHUNK), g, precision=HI, preferred_element_type=F32)
        cm = _gdn_chunk_common(q_ref[...], k_ref[...], v_ref[...], gc, beta)
        tinv = _neumann_inv(cm["A"])
        W = _dot3(tinv, cm["KB"] * cm["EG"], _NN)
        U = _dot3(tinv, cm["V"] * cm["B"], _NN)
        s_old = [S_ref[h * 128:(h + 1) * 128, :] for h in range(_H)]
        vn = [_hs(U, h) - _dot(_hs(W, h), s_old[h], _NN) for h in range(_H)]
        o_intra = _dot(cm["P"], jnp.concatenate(vn, axis=0), _NN)
        outs = []
        for h in range(_H):
            outs.append(_dot(_hs(cm["QG"], h), s_old[h], _NN) + _hs(o_intra, h))
            gl = jnp.exp(gc[CHUNK - 1:CHUNK, h:h + 1])
            S_ref[h * 128:(h + 1) * 128, :] = gl * s_old[h] + _dot(_hs(cm["KD"], h), vn[h], _TN)
            sst_ref[h * 128:(h + 1) * 128, :] = s_old[h]
        o_ref[...] = jnp.concatenate(outs, axis=1)
        tinv_ref[...] = tinv
        w_ref[...] = _unst(W)
        u_ref[...] = _unst(U)
        gc_ref[...] = gc
        beta_ref[...] = beta

    blk = lambda r, w, c: pl.BlockSpec((E, r, w), lambda b, n: (b, n, c))
    par = pl.BlockSpec((1, 128), lambda b, n: (0, 0))
    out_shape = (_sds((Bl, S, 512)), _sds((Bl, nC * _HC, _HC)), _sds((Bl, nC * 512, 128)),
                 _sds((Bl, S, 512)), _sds((Bl, S, 512)), _sds((Bl, S, 128)), _sds((Bl, S, 128)))
    out_specs = (blk(CHUNK, 512, 0), blk(_HC, _HC, 0), blk(512, 128, 0), blk(CHUNK, 512, 0), blk(CHUNK, 512, 0),
                 blk(CHUNK, 128, 0), blk(CHUNK, 128, 0))
    q3, s3 = qkvn.reshape(Bl, S, -1), sm.reshape(Bl, S, -1)
    res = _pc(body, name="gdn_chunk_fwd", out_shape=out_shape, grid=(Bl // E, nC),
              in_specs=[blk(CHUNK, 512, 0), blk(CHUNK, 512, 1), blk(CHUNK, 512, 2), blk(CHUNK, 128, 6),
                        blk(CHUNK, 128, 7), par, par],
              out_specs=out_specs, scratch=[pltpu.VMEM((E, 512, 128), F32)], sem=("parallel", "arbitrary"),
              xchg=xchg)(q3, q3, q3, s3, s3, alog, dtb)
    return (res[0].reshape(T, 512),) + tuple(res[1:])


def _gdn_bwd(qkvn, sm, alog, dtb, gc_s, beta_s, tinv_s, sst_s, w_s, u_s, do, dcat, Bl, S, xchg=None):
    T = Bl * S
    nC = S // CHUNK
    E = _per_step(Bl)

    def body(*refs):
        acc_ref, dS_ref = refs[-2], refs[-1]
        first_chunk = pl.program_id(1) == 0

        @pl.when(first_chunk)
        def _():
            dS_ref[...] = jnp.zeros_like(dS_ref)

        @pl.when(first_chunk & (pl.program_id(0) == 0))
        def _():
            acc_ref[...] = jnp.zeros_like(acc_ref)

        for e in range(E):
            one(*[r if i in (4, 5, 13, 16) else r.at[e] for i, r in enumerate(refs)])

    def one(q_ref, k_ref, v_ref, a_ref, alog_ref, dtb_ref, gc_ref, beta_ref, tinv_ref, sst_ref, w_ref, u_ref,
            do_ref, alias_ref, dqkv_ref, dsm_ref, acc_ref, dS_ref):
        gc, beta = gc_ref[...], beta_ref[...]
        cm = _gdn_chunk_common(q_ref[...], k_ref[...], v_ref[...], gc, beta)
        Q, K, V, B, D, KB, A, P = (cm[n] for n in ("Q", "K", "V", "B", "D", "KB", "A", "P"))
        EG, ED, QG, KD = cm["EG"], cm["ED"], cm["QG"], cm["KD"]
        tinv = tinv_ref[...]
        W, U, DO = _st(w_ref[...]), _st(u_ref[...]), _st(do_ref[...])
        s_old = [sst_ref[h * 128:(h + 1) * 128, :] for h in range(_H)]
        ds_new = [dS_ref[h * 128:(h + 1) * 128, :] for h in range(_H)]
        VN = jnp.concatenate([_hs(U, h) - _dot(_hs(W, h), s_old[h], _NN) for h in range(_H)], axis=0)
        dP = jnp.where(cm["incl"], _dot(DO, VN, _NT), 0.0)
        dVN0 = _dot(P, DO, _TN)
        dVN, dQG, dKD, dW, TL = [], [], [], [], []
        for h in range(_H):
            gl = jnp.exp(gc[CHUNK - 1:CHUNK, h:h + 1])
            dvn = _hs(dVN0, h) + _dot(_hs(KD, h), ds_new[h], _NN)
            dkd = _dot(_hs(VN, h), ds_new[h], _NT)
            dVN.append(dvn)
            dQG.append(_dot(_hs(DO, h), s_old[h], _NT))
            dKD.append(dkd)
            dW.append(-_dot(dvn, s_old[h], _NT))
            dgl = jnp.sum(jnp.sum(ds_new[h] * s_old[h], axis=1, keepdims=True), axis=0, keepdims=True)
            tl = jnp.sum(jnp.sum(dkd * _hs(KD, h), axis=1, keepdims=True), axis=0, keepdims=True) + dgl * gl
            TL.append(jnp.broadcast_to(tl, (CHUNK, 1)))
            dS_ref[h * 128:(h + 1) * 128, :] = (_dot(_hs(QG, h), _hs(DO, h), _TN) + gl * ds_new[h]
                                                - _dot(_hs(W, h), dvn, _TN))
        dVN, dQG, dKD, dW, TL = (jnp.concatenate(z, axis=0) for z in (dVN, dQG, dKD, dW, TL))
        dVB = _dot3(tinv, dVN, _TN)
        dKBE = _dot3(tinv, dW, _TN)
        dA = jnp.where(cm["strict"], -(_dot3(dVB, U, _NT) + _dot3(dKBE, W, _NT)), 0.0)
        dG = dA * D
        dQK = dP * D
        dKB = _dot(dG, K, _NN) + dKBE * EG
        dK = _dot(dG, KB, _TN) + _dot(dQK, Q, _TN) + dKB * B + dKD * ED
        dQ = _dot(dQK, K, _NN) + dQG * EG
        Mx = dA * A + dP * P
        rs = lambda z: jnp.sum(z, axis=1, keepdims=True)
        ri = lax.broadcasted_iota(jnp.int32, (_HC, 1), 0)
        dGC = (rs(Mx) - rs(Mx.T) + rs(dKBE * KB * EG) + rs(dQG * QG) - rs(dKD * KD)
               + jnp.where((ri & (CHUNK - 1)) == CHUNK - 1, TL, 0.0))
        dBeta = rs(dVB * V) + rs(dKB * K)
        dqkv_ref[:, 0:512] = _unst(dQ)
        dqkv_ref[:, 512:1024] = _unst(dK)
        dqkv_ref[:, 1024:1536] = _unst(dVB * B)
        dg = jnp.dot(_tri(CHUNK, upper=True), _untile(dGC), precision=HI, preferred_element_type=F32)
        a, alog_v, dtb_v = a_ref[...], alog_ref[...], dtb_ref[...]
        g = _gates(a, alog_v, dtb_v)
        lane = lax.broadcasted_iota(jnp.int32, (CHUNK, 128), 1)
        valid = lane < _H
        da = jnp.where(valid, dg * (-jnp.exp(alog_v)) * jax.nn.sigmoid(a + dtb_v), 0.0)
        dsm_ref[:, 0:128] = da.astype(dsm_ref.dtype)
        dsm_ref[:, 128:256] = jnp.where(valid, _untile(dBeta) * beta * (1.0 - beta), 0.0).astype(dsm_ref.dtype)
        acc_ref[0:1, :] += jnp.sum(jnp.where(valid, dg * g, 0.0), axis=0, keepdims=True)
        acc_ref[1:2, :] += jnp.sum(da, axis=0, keepdims=True)

    blk = lambda r, w, c: pl.BlockSpec((E, r, w), lambda b, n: (b, nC - 1 - n, c))
    par = pl.BlockSpec((1, 128), lambda b, n: (0, 0))
    in_specs = [blk(CHUNK, 512, 0), blk(CHUNK, 512, 1), blk(CHUNK, 512, 2), blk(CHUNK, 128, 6), par, par,
                blk(CHUNK, 128, 0), blk(CHUNK, 128, 0), blk(_HC, _HC, 0), blk(512, 128, 0),
                blk(CHUNK, 512, 0), blk(CHUNK, 512, 0), blk(CHUNK, 512, 0), ANY]
    out_shape = (_sds((Bl, S, 1536)), _sds((Bl, S, DCAT_W), BF), _sds((8, 128)))
    out_specs = (blk(CHUNK, 1536, 0), blk(CHUNK, 256, 3), pl.BlockSpec((8, 128), lambda b, n: (0, 0)))
    q3, s3 = qkvn.reshape(Bl, S, -1), sm.reshape(Bl, S, -1)
    res = _pc(body, name="gdn_chunk_bwd", out_shape=out_shape, grid=(Bl // E, nC), in_specs=in_specs,
              out_specs=out_specs, scratch=[pltpu.VMEM((E, 512, 128), F32)], sem=("arbitrary", "arbitrary"),
              aliases={13: 1}, xchg=xchg)(q3, q3, q3, s3, alog, dtb, gc_s, beta_s, tinv_s, sst_s, w_s, u_s,
                                          do.reshape(Bl, S, -1), dcat.reshape(Bl, S, DCAT_W))
    return (res[0].reshape(T, 1536), res[1].reshape(T, DCAT_W)) + tuple(res[2:])


def _gdn_post(o, z, g, T):
    tt = _pick(T, (2048, 1024, 512, 256))

    def body(o_ref, z_ref, g_ref, out_ref, outt_ref):
        y = (_rmsnorm(o_ref[...], g_ref[...]) * _silu(z_ref[...])).astype(out_ref.dtype)
        out_ref[...] = y
        outt_ref[...] = y.T

    blk = pl.BlockSpec((tt, 128), lambda i, h: (i, h))
    return _pc(body, name="gdn_post_fwd", out_shape=(_sds((T, 1024), BF), _sds((1024, T), BF)), grid=(T // tt, _H),
               in_specs=[blk, blk, pl.BlockSpec((1, 128), lambda i, h: (0, 0))],
               out_specs=(blk, pl.BlockSpec((128, tt), lambda i, h: (h, i))),
               sem=("parallel", "parallel"))(o, z, g)


def _gdn_post_bwd(o, z, g, dmixin, T):
    tt = _pick(T, (2048, 1024, 512, 256))

    def body(o_ref, z_ref, g_ref, d_ref, do_ref, dz_ref, dg_ref):
        @pl.when((pl.program_id(0) == 0) & (pl.program_id(1) == 0))
        def _():
            dg_ref[...] = jnp.zeros_like(dg_ref)

        _, vjp = jax.vjp(lambda a, b, c: _rmsnorm(a, c) * _silu(b), o_ref[...], z_ref[...], g_ref[...])
        do, dz, dg = vjp(d_ref[...])
        do_ref[...] = do
        dz_ref[...] = dz.astype(dz_ref.dtype)
        dg_ref[0:1, :] += dg

    blk = pl.BlockSpec((tt, 128), lambda i, h: (i, h))
    return _pc(body, name="gdn_post_bwd", out_shape=(_sds((T, 512)), _sds((T, DCAT_W), BF), _sds((8, 128))),
               grid=(T // tt, _H), in_specs=[blk, blk, pl.BlockSpec((1, 128), lambda i, h: (0, 0)), blk],
               out_specs=(blk, pl.BlockSpec((tt, 128), lambda i, h: (i, DCAT_Z // 128 + h)),
                          pl.BlockSpec((8, 128), lambda i, h: (0, 0))),
               sem=("arbitrary", "arbitrary"))(o, z, g, dmixin)


def _rms_fwd(sm, g, col_blk, name):
    T = sm.shape[0]
    d = g.shape[1]
    tt = _pick(T, (1024, 512, 256))

    def body(x_ref, g_ref, o_ref, ot_ref):
        y = _rmsnorm(x_ref[...], g_ref[...]).astype(o_ref.dtype)
        o_ref[...] = y
        ot_ref[...] = y.T

    return _pc(body, name=name, out_shape=(_sds((T, d), BF), _sds((d, T), BF)), grid=(T // tt,),
               in_specs=[pl.BlockSpec((tt, d), lambda i: (i, col_blk)), pl.BlockSpec((1, d), lambda i: (0, 0))],
               out_specs=(pl.BlockSpec((tt, d), lambda i: (i, 0)), pl.BlockSpec((d, tt), lambda i: (0, i))),
               sem=("parallel",))(sm, g)


def _rms_bwd(sm, g, dy, dsm, col_blk, name):
    T = sm.shape[0]
    d = g.shape[1]
    tt = _pick(T, (1024, 512, 256))

    def body(x_ref, g_ref, d_ref, alias_ref, dx_ref, dg_ref):
        @pl.when(pl.program_id(0) == 0)
        def _():
            dg_ref[...] = jnp.zeros_like(dg_ref)

        _, vjp = jax.vjp(_rmsnorm, x_ref[...], g_ref[...])
        dx, dg = vjp(d_ref[...])
        dx_ref[...] = dx.astype(dx_ref.dtype)
        dg_ref[0:1, :] += dg

    grp = pl.BlockSpec((tt, d), lambda i: (i, col_blk))
    return _pc(body, name=name, out_shape=(_sds(dsm.shape, dsm.dtype), _sds((8, d))), grid=(T // tt,),
               in_specs=[grp, pl.BlockSpec((1, d), lambda i: (0, 0)), pl.BlockSpec((tt, d), lambda i: (i, 0)), ANY],
               out_specs=(grp, pl.BlockSpec((8, d), lambda i: (0, 0))), aliases={3: 0},
               sem=("arbitrary",))(sm, g, dy, dsm)


def _rope_tables(S):
    inv = ROPE_THETA ** (-jnp.arange(0, MLA_ROPE, 2, dtype=F32) / MLA_ROPE)
    ang = jnp.arange(S, dtype=F32)[:, None] * inv[None, :]
    cos, sin = jnp.cos(ang), jnp.sin(ang)
    z = jnp.zeros((S, 64), F32)
    return jnp.concatenate([cos, cos, z], axis=1), jnp.concatenate([-sin, sin, z], axis=1)


def _swap_halves(x):
    lane = lax.broadcasted_iota(jnp.int32, x.shape, 1)
    return jnp.where(lane < 32, pltpu.roll(x, 96, 1), jnp.where(lane < 64, pltpu.roll(x, 32, 1), 0.0))


def _rope_fwd(qraw, sm, cos, sin, S):
    T = qraw.shape[0]
    tt = _pick(S, (2048, 1024, 512, 256))
    nps = S // tt

    def body(q_ref, kr_ref, c_ref, s_ref, qo_ref, ko_ref):
        c, s = c_ref[...], s_ref[...]
        q = q_ref[...]
        qo_ref[...] = ((q * c + _swap_halves(q) * s) * ATTN_SCALE).astype(qo_ref.dtype)

        @pl.when(pl.program_id(1) == 0)
        def _():
            k = kr_ref[...]
            ko_ref[...] = (k * c + _swap_halves(k) * s).astype(ko_ref.dtype)

    tab = pl.BlockSpec((tt, 128), lambda i, h: (i % nps, 0))
    return _pc(body, name="rope_fwd", out_shape=(_sds((T, 512), BF), _sds((T, 128), BF)), grid=(T // tt, MLA_HEADS),
               in_specs=[pl.BlockSpec((tt, 128), lambda i, h: (i, 2 * h + 1)),
                         pl.BlockSpec((tt, 128), lambda i, h: (i, 3)), tab, tab],
               out_specs=(pl.BlockSpec((tt, 128), lambda i, h: (i, h)), pl.BlockSpec((tt, 128), lambda i, h: (i, 0))),
               sem=("parallel", "arbitrary"))(qraw, sm, cos, sin)


def _rope_bwd(dqraw, dsm, dqr, dkrr, cos, sin, S):
    T = dqraw.shape[0]
    tt = _pick(S, (2048, 1024, 512, 256))
    nps = S // tt

    def body(alias_q, alias_s, dq_ref, dk_ref, c_ref, s_ref, qo_ref, ko_ref):
        c, s = c_ref[...], s_ref[...]
        d = dq_ref[...]
        qo_ref[...] = (d * c + _swap_halves(d * s)).astype(qo_ref.dtype)

        @pl.when(pl.program_id(1) == 0)
        def _():
            k = dk_ref[...]
            ko_ref[...] = (k * c + _swap_halves(k * s)).astype(ko_ref.dtype)

    tab = pl.BlockSpec((tt, 128), lambda i, h: (i % nps, 0))
    return _pc(body, name="rope_bwd", out_shape=(_sds(dqraw.shape, dqraw.dtype), _sds(dsm.shape, dsm.dtype)),
               grid=(T // tt, MLA_HEADS),
               in_specs=[ANY, ANY, pl.BlockSpec((tt, 128), lambda i, h: (i, h)),
                         pl.BlockSpec((tt, 128), lambda i, h: (i, 0)), tab, tab],
               out_specs=(pl.BlockSpec((tt, 128), lambda i, h: (i, 2 * h + 1)),
                          pl.BlockSpec((tt, 128), lambda i, h: (i, 3))),
               aliases={0: 0, 1: 1}, sem=("parallel", "arbitrary"))(dqraw, dsm, dqr, dkrr, cos, sin)


def _attn_scores(qn, qr, kn_ref, kr_ref, L, tq):
    def sc(lo, hi):
        return _dot(qn, kn_ref[lo:hi, :], _NT) + _dot(qr, kr_ref[lo:hi, :], _NT)

    sd = sc(L - tq, L)
    qc = lax.broadcasted_iota(jnp.int32, sd.shape, 0) >> 6
    kc = lax.broadcasted_iota(jnp.int32, sd.shape, 1) >> 6
    sd = jnp.where(kc <= qc, sd, -jnp.inf)
    return sd if L == tq else jnp.concatenate([sc(0, L - tq), sd], axis=1)


def _attn_fwd(mixin, mixin_t, qraw, qr, kv, krr, Bl, S):
    T = Bl * S
    tq = _pick(S, (256, 128))
    nq = S // tq

    def body(alias_ref, alias_t, qn_ref, qr_ref, kn_ref, v_ref, kr_ref, o_ref, lse_ref, ot_ref):
        i = pl.program_id(2)
        qn = (qn_ref[...] * ATTN_SCALE).astype(BF)
        qrv = qr_ref[...]
        for ii in range(nq):
            @pl.when(i == ii)
            def _(L=(ii + 1) * tq):
                s = _attn_scores(qn, qrv, kn_ref, kr_ref, L, tq)
                m = jnp.max(s, axis=-1, keepdims=True)
                e = jnp.exp(s - m)
                l = jnp.sum(e, axis=-1, keepdims=True)
                o = (_dot(e, v_ref[0:L, :], _NN) / l).astype(o_ref.dtype)
                o_ref[...] = o
                ot_ref[...] = o.T
                lse_ref[...] = jnp.broadcast_to(m + jnp.log(l), lse_ref.shape)

    qrow = lambda b, h, i: b * nq + i
    in_specs = [ANY, ANY,
                pl.BlockSpec((tq, 128), lambda b, h, i: (qrow(b, h, i), 2 * h)),
                pl.BlockSpec((tq, 128), lambda b, h, i: (qrow(b, h, i), h)),
                pl.BlockSpec((S, 128), lambda b, h, i: (b, 2 * h)),
                pl.BlockSpec((S, 128), lambda b, h, i: (b, 2 * h + 1)),
                pl.BlockSpec((S, 128), lambda b, h, i: (b, 0))]
    return _pc(body, name="mla_attn_fwd",
               out_shape=(_sds(mixin.shape, mixin.dtype), _sds((T, 512)), _sds(mixin_t.shape, mixin_t.dtype)),
               grid=(Bl, MLA_HEADS, nq), in_specs=in_specs,
               out_specs=(pl.BlockSpec((tq, 128), lambda b, h, i: (qrow(b, h, i), 4 + h)),
                          pl.BlockSpec((tq, 128), lambda b, h, i: (qrow(b, h, i), h)),
                          pl.BlockSpec((128, tq), lambda b, h, i: (4 + h, qrow(b, h, i)))),
               aliases={0: 0, 1: 2}, sem=("parallel", "parallel", "parallel"))(mixin, mixin_t, qraw, qr, kv, kv, krr)


def _attn_bwd(qraw, qr, kv, krr, lse, dmixin, Bl, S):
    T = Bl * S
    tq = _pick(S, (256, 128))
    nq = S // tq

    def body(qn_ref, qr_ref, kn_ref, v_ref, kr_ref, do_ref, lse_ref, dqn_ref, dqr_ref, dkv_ref, dkr_ref, acc_ref):
        h, i = pl.program_id(1), pl.program_id(2)
        qn = (qn_ref[...] * ATTN_SCALE).astype(BF)
        qrv = qr_ref[...]
        do32 = do_ref[...]
        do = do32.astype(BF)
        lse_col = lse_ref[:, 0:1]

        @pl.when(i == 0)
        def _():
            acc_ref[...] = jnp.zeros_like(acc_ref)

        @pl.when((i == 0) & (h == 0))
        def _():
            dkr_ref[...] = jnp.zeros_like(dkr_ref)

        for ii in range(nq):
            @pl.when(i == ii)
            def _(L=(ii + 1) * tq):
                p = jnp.exp(_attn_scores(qn, qrv, kn_ref, kr_ref, L, tq) - lse_col)
                pb = p.astype(BF)
                v = v_ref[0:L, :]
                delta = jnp.sum(do32 * _dot(pb, v, _NN), axis=-1, keepdims=True)
                ds = (p * (_dot(do, v, _NT) - delta)).astype(BF)
                dqn_ref[...] = (_dot(ds, kn_ref[0:L, :], _NN) * ATTN_SCALE).astype(dqn_ref.dtype)
                dqr_ref[...] = _dot(ds, kr_ref[0:L, :], _NN) * ATTN_SCALE
                acc_ref[0:L, 0:128] += _dot(ds, qn, _TN)
                acc_ref[0:L, 128:256] += _dot(pb, do, _TN)
                dkr_ref[0:L, :] += _dot(ds, qrv, _TN)

        @pl.when(i == nq - 1)
        def _():
            dkv_ref[...] = acc_ref[...].astype(dkv_ref.dtype)

    qrow = lambda b, h, i: b * nq + i
    in_specs = [pl.BlockSpec((tq, 128), lambda b, h, i: (qrow(b, h, i), 2 * h)),
                pl.BlockSpec((tq, 128), lambda b, h, i: (qrow(b, h, i), h)),
                pl.BlockSpec((S, 128), lambda b, h, i: (b, 2 * h)),
                pl.BlockSpec((S, 128), lambda b, h, i: (b, 2 * h + 1)),
                pl.BlockSpec((S, 128), lambda b, h, i: (b, 0)),
                pl.BlockSpec((tq, 128), lambda b, h, i: (qrow(b, h, i), 4 + h)),
                pl.BlockSpec((tq, 128), lambda b, h, i: (qrow(b, h, i), h))]
    out_shape = (_sds((T, 1024), BF), _sds((T, 512)), _sds((T, 1024), BF), _sds((T, 128)))
    out_specs = (pl.BlockSpec((tq, 128), lambda b, h, i: (qrow(b, h, i), 2 * h)),
                 pl.BlockSpec((tq, 128), lambda b, h, i: (qrow(b, h, i), h)),
                 pl.BlockSpec((S, 256), lambda b, h, i: (b, h)),
                 pl.BlockSpec((S, 128), lambda b, h, i: (b, 0)))
    return _pc(body, name="mla_attn_bwd", out_shape=out_shape, grid=(Bl, MLA_HEADS, nq), in_specs=in_specs,
               out_specs=out_specs, scratch=[pltpu.VMEM((S, 256), F32)],
               sem=("arbitrary", "arbitrary", "arbitrary"))(qraw, qr, kv, kv, krr, dmixin, lse)


def _ln1_fwd(x, mix, g, b):
    T, D = x.shape
    tt = _pick(T, (512, 256))

    def body(x_ref, m_ref, g_ref, b_ref, o_ref, ob_ref, ot_ref):
        h = _layernorm(ALPHA * x_ref[...] + m_ref[...], g_ref[...], b_ref[...])
        o_ref[...] = h
        hb = h.astype(ob_ref.dtype)
        ob_ref[...] = hb
        ot_ref[...] = hb.T

    blk = pl.BlockSpec((tt, D), lambda i: (i, 0))
    par = pl.BlockSpec((1, D), lambda i: (0, 0))
    return _pc(body, name="ln1_fwd", out_shape=(_sds((T, D)), _sds((T, D), BF), _sds((D, T), BF)), grid=(T // tt,),
               in_specs=[blk, blk, par, par], out_specs=(blk, blk, pl.BlockSpec((D, tt), lambda i: (0, i))),
               sem=("parallel",))(x, mix, g, b)


def _ln1_bwd(x, mix, g, b, dh):
    T, D = x.shape
    tt = _pick(T, (256,))

    def body(x_ref, m_ref, g_ref, b_ref, d_ref, dx_ref, dm_ref, acc_ref):
        @pl.when(pl.program_id(0) == 0)
        def _():
            acc_ref[...] = jnp.zeros_like(acc_ref)

        f = lambda xx, mm, gg, bb: _layernorm(ALPHA * xx + mm, gg, bb)
        _, vjp = jax.vjp(f, x_ref[...], m_ref[...], g_ref[...], b_ref[...])
        dx, dm, dg, db = vjp(d_ref[...])
        dx_ref[...] = dx
        dm_ref[...] = dm.astype(dm_ref.dtype)
        acc_ref[0:1, :] += dg
        acc_ref[1:2, :] += db

    blk = pl.BlockSpec((tt, D), lambda i: (i, 0))
    par = pl.BlockSpec((1, D), lambda i: (0, 0))
    return _pc(body, name="ln1_bwd", out_shape=(_sds((T, D)), _sds((T, D), BF), _sds((8, D))), grid=(T // tt,),
               in_specs=[blk, blk, par, par, blk], out_specs=(blk, blk, pl.BlockSpec((8, D), lambda i: (0, 0))),
               sem=("arbitrary",))(x, mix, g, b, dh)


def _ffn_act_fwd(ug, uu, wg, wu, bg, bu, Bl, S):
    T, C = ug.shape
    cb = _pick(C, (256, 128))
    kk = wg.shape[0]

    def body(g_ref, u_ref, wg_ref, wu_ref, bg_ref, bu_ref, o_ref, ot_ref):
        cg = _conv(g_ref[...], wg_ref[...]) + bg_ref[...]
        cu = _conv(u_ref[...], wu_ref[...]) + bu_ref[...]
        a = (_silu(cg) * cu).astype(o_ref.dtype)
        o_ref[...] = a
        ot_ref[...] = a.T

    blk = pl.BlockSpec((S, cb), lambda b, j: (b, j))
    wblk = pl.BlockSpec((kk, cb), lambda b, j: (0, j))
    bblk = pl.BlockSpec((1, cb), lambda b, j: (0, j))
    return _pc(body, name="ffn_act_fwd", out_shape=(_sds((T, C), BF), _sds((C, T), BF)), grid=(Bl, C // cb),
               in_specs=[blk, blk, wblk, wblk, bblk, bblk],
               out_specs=(blk, pl.BlockSpec((cb, S), lambda b, j: (j, b))), sem=("parallel", "parallel"))(
        ug, uu, wg, wu, bg, bu)


def _ffn_act_bwd(ug, uu, wg, wu, bg, bu, dact, Bl, S):
    T, C = ug.shape
    cb = _pick(C, (256, 128))
    kk = wg.shape[0]

    def body(g_ref, u_ref, wg_ref, wu_ref, bg_ref, bu_ref, d_ref,
             dg_ref, du_ref, dwg_ref, dwu_ref, dbg_ref, dbu_ref):
        first = pl.program_id(1) == 0
        wgv, wuv = wg_ref[...], wu_ref[...]
        tg, tu = _taps(g_ref[...], kk), _taps(u_ref[...], kk)
        cg = _conv(None, wgv, tg) + bg_ref[...]
        cu = _conv(None, wuv, tu) + bu_ref[...]
        _, vjp = jax.vjp(lambda a, b: _silu(a) * b, cg, cu)
        dcg, dcu = vjp(d_ref[...].astype(F32))
        dg_ref[...] = _conv_bwd_x(dcg, wgv).astype(dg_ref.dtype)
        du_ref[...] = _conv_bwd_x(dcu, wuv).astype(du_ref.dtype)
        _conv_bwd_w(tg, dcg, dwg_ref, first)
        _conv_bwd_w(tu, dcu, dwu_ref, first)
        dbg_ref[...] = jnp.where(first, 0.0, dbg_ref[...]) + jnp.sum(dcg, axis=0, keepdims=True)
        dbu_ref[...] = jnp.where(first, 0.0, dbu_ref[...]) + jnp.sum(dcu, axis=0, keepdims=True)

    blk = pl.BlockSpec((S, cb), lambda j, b: (b, j))
    wblk = pl.BlockSpec((kk, cb), lambda j, b: (0, j))
    bblk = pl.BlockSpec((1, cb), lambda j, b: (0, j))
    out_shape = (_sds((T, C), BF), _sds((T, C), BF), _sds((kk, C)), _sds((kk, C)), _sds((1, C)), _sds((1, C)))
    return _pc(body, name="ffn_act_bwd", out_shape=out_shape, grid=(C // cb, Bl),
               in_specs=[blk, blk, wblk, wblk, bblk, bblk, blk],
               out_specs=(blk, blk, wblk, wblk, bblk, bblk), sem=("parallel", "arbitrary"))(
        ug, uu, wg, wu, bg, bu, dact)


def _head(h1, ffn, gpre, pp, tgt, bg, g2, b2):
    T, D = h1.shape
    tt = _pick(T, (256,))

    def body(h_ref, f_ref, gp_ref, pp_ref, t_ref, bg_ref, g2_ref, b2_ref, df_ref, dfb_ref, dgp_ref, dpp_ref, acc_ref):
        @pl.when(pl.program_id(0) == 0)
        def _():
            acc_ref[...] = jnp.zeros_like(acc_ref)

        h, tg = h_ref[...], t_ref[...]

        def loss_fn(f, gp, p_, bgv, g2v, b2v):
            pre = ALPHA * h + f + jax.nn.sigmoid(gp + bgv) * p_
            err = _layernorm(pre, g2v, b2v) - tg
            return 0.5 * jnp.sum(jnp.mean(err * err, axis=-1, keepdims=True))

        loss, grads = jax.value_and_grad(loss_fn, argnums=(0, 1, 2, 3, 4, 5))(
            f_ref[...], gp_ref[...], pp_ref[...], bg_ref[...], g2_ref[...], b2_ref[...])
        df_ref[...] = grads[0]
        dfb_ref[...] = grads[0].astype(dfb_ref.dtype)
        dgp_ref[...] = grads[1].astype(dgp_ref.dtype)
        dpp_ref[...] = grads[2].astype(dpp_ref.dtype)
        acc_ref[0:1, :] += grads[3]
        acc_ref[1:2, :] += grads[4]
        acc_ref[2:3, :] += grads[5]
        acc_ref[3:4, :] += jnp.broadcast_to(loss, (1, D))

    blk = pl.BlockSpec((tt, D), lambda i: (i, 0))
    par = pl.BlockSpec((1, D), lambda i: (0, 0))
    o_bf = _sds((T, D), BF)
    return _pc(body, name="loss_head", out_shape=(_sds((T, D)), o_bf, o_bf, o_bf, _sds((8, D))),
               grid=(T // tt,), in_specs=[blk, blk, blk, blk, blk, par, par, par],
               out_specs=(blk, blk, blk, blk, pl.BlockSpec((8, D), lambda i: (0, 0))), sem=("arbitrary",))(
        h1, ffn, gpre, pp, tgt, bg, g2, b2)


def _adamw(parts, w, m, v, name):
    R, C = w.shape
    tr = R if R <= 512 else _pick(R, (256,))

    def body(p_ref, w_ref, m_ref, v_ref, g_ref, d_ref, nm_ref, nv_ref):
        g = p_ref[0].astype(F32)
        for j in range(1, N_DEV):
            g = g + p_ref[j].astype(F32)
        mm = ADAM_B1 * m_ref[...] + (1.0 - ADAM_B1) * g
        vv = ADAM_B2 * v_ref[...] + (1.0 - ADAM_B2) * jnp.square(g)
        m_hat = mm / (1.0 - ADAM_B1 ** ADAM_STEP)
        v_hat = vv / (1.0 - ADAM_B2 ** ADAM_STEP)
        g_ref[...] = g
        d_ref[...] = -ADAM_LR * (m_hat / (jnp.sqrt(v_hat) + ADAM_EPS) + ADAM_WD * w_ref[...])
        nm_ref[...] = mm
        nv_ref[...] = vv

    blk = pl.BlockSpec((tr, C), lambda i: (i, 0))
    o = _sds((R, C))
    return _pc(body, name=name, out_shape=(o, o, o, o), grid=(R // tr,),
               in_specs=[pl.BlockSpec((N_DEV, tr, C), lambda i: (0, i, 0)), blk, blk, blk],
               out_specs=(blk, blk, blk, blk), sem=("parallel",))(parts, w, m, v)


def _prep(x, p, xchg=None):
    T, D = x.shape
    Dp = p.shape[1]
    tt = _pick(T, (512, 256))

    def body(x_ref, p_ref, xb_ref, xt_ref, pb_ref, pt_ref):
        xb = x_ref[...].astype(BF)
        xb_ref[...] = xb
        xt_ref[...] = xb.T
        pb = p_ref[...].astype(BF)
        pb_ref[...] = pb
        pt_ref[...] = pb.T

    row = lambda w: pl.BlockSpec((tt, w), lambda i: (i, 0))
    col = lambda w: pl.BlockSpec((w, tt), lambda i: (0, i))
    return _pc(body, name="prep_inputs",
               out_shape=(_sds((T, D), BF), _sds((D, T), BF), _sds((T, Dp), BF), _sds((Dp, T), BF)),
               grid=(T // tt,), in_specs=[row(D), row(Dp)], out_specs=(row(D), col(D), row(Dp), col(Dp)),
               sem=("parallel",), xchg=xchg)(x, p)


SMALL_ORDER = ("gdn_a_log", "gdn_dt_bias", "gdn_norm_g", "mla_q_norm_g", "mla_kv_norm_g", "ln1_g", "ln1_b",
               "ffn_conv_b", "ple_b_gate", "ln2_g", "ln2_b")
EARLY = ("w_in", "gdn_conv_w")
LATE = ("mla_w_q_up", "mla_w_kv_up", "w_out", "ffn_w_up", "ffn_conv_w", "ffn_w_down", "ple_w_gate", "ple_w_proj")
GRADS_EARLY = ("w_out", "ffn_w_up", "ffn_conv_w", "ffn_w_down", "ple_w_gate", "ple_w_proj")
GRADS_LATE = ("w_in", "gdn_conv_w", "mla_w_q_up", "mla_w_kv_up")


def _pad_lanes(v, n=128):
    return jnp.pad(v, ((0, 0), (0, n - v.shape[1])))


def _local_step(x, p, tgt, W, sp, Bl, S, early_weights=None, late_weights=None, early_grads=None, late_grads=None):
    T = Bl * S
    W = dict(W)
    prep = _prep(x, p, xchg=None if early_weights is None else early_weights[:2])
    xb, xT, pb, pT = prep[:4]
    if early_weights is not None:
        W.update(early_weights[2](prep[4:]))
    w_in = W["w_in"]
    wqkv, wz = w_in[:, :1536], w_in[:, 1536:2048]
    z64 = jnp.zeros((w_in.shape[0], 64), w_in.dtype)
    z124 = jnp.zeros((w_in.shape[0], 124), w_in.dtype)
    wsm = jnp.concatenate([w_in[:, 2056:2440], w_in[:, 2696:2760], z64, w_in[:, 2440:2696],
                           w_in[:, 2048:2052], z124, w_in[:, 2052:2056], z124], axis=1)
    alog, dtb = _pad_lanes(sp["gdn_a_log"]), _pad_lanes(sp["gdn_dt_bias"])
    cos, sin = _rope_tables(S)

    qkv = _mm(xb, wqkv, "nn", name="mm_qkv")
    z = _mm(xb, wz, "nn", name="mm_z")
    sm = _mm(xb, wsm, "nn", name="mm_sm")
    qkvn = _gdn_pre_fwd(qkv, W["gdn_conv_w"], Bl, S)
    gdn_out = _gdn_fwd(qkvn, sm, alog, dtb, Bl, S, xchg=None if late_weights is None else late_weights[:2])
    o, tinv_s, sst_s, w_s, u_s, gc_s, beta_s = gdn_out[:7]
    if late_weights is not None:
        W.update(late_weights[2](gdn_out[7:]))
    wq = jnp.pad(W["mla_w_q_up"].reshape(-1, MLA_HEADS, 192), ((0, 0), (0, 0), (0, 64))).reshape(-1, 1024)
    wkv = W["mla_w_kv_up"]
    C = W["ffn_w_down"].shape[0]
    wup_g, wup_u = W["ffn_w_up"][:, :C], W["ffn_w_up"][:, C:]
    cw_g, cw_u = W["ffn_conv_w"][:, :C], W["ffn_conv_w"][:, C:]
    cb_g, cb_u = sp["ffn_conv_b"][:, :C], sp["ffn_conv_b"][:, C:]
    mixin, mixin_t = _gdn_post(o, z, sp["gdn_norm_g"], T)
    cqn, cqn_t = _rms_fwd(sm, sp["mla_q_norm_g"], 0, "rms_q_fwd")
    ckvn, ckvn_t = _rms_fwd(sm, sp["mla_kv_norm_g"], 2, "rms_kv_fwd")
    qraw = _mm(cqn, wq, "nn", name="mm_qup")
    kv = _mm(ckvn, wkv, "nn", name="mm_kvup", out_dtype=BF)
    qr, krr = _rope_fwd(qraw, sm, cos, sin, S)
    mixin, lse, mixin_t = _attn_fwd(mixin, mixin_t, qraw, qr, kv, krr, Bl, S)
    mix = _mm(mixin, W["w_out"], "nn", name="mm_out")
    h1, h1b, h1T = _ln1_fwd(x, mix, sp["ln1_g"], sp["ln1_b"])
    ug = _mm(h1b, wup_g, "nn", name="mm_up_gate")
    uu = _mm(h1b, wup_u, "nn", name="mm_up_up")
    act, act_t = _ffn_act_fwd(ug, uu, cw_g, cw_u, cb_g, cb_u, Bl, S)
    ffn = _mm(act, W["ffn_w_down"], "nn", name="mm_down")
    gpre = _mm(h1b, W["ple_w_gate"], "nn", name="mm_gate")
    pp = _mm(pb, W["ple_w_proj"], "nn", name="mm_proj")

    dffn, dffnb, dgpre, dpp, hacc = _head(h1, ffn, gpre, pp, tgt, sp["ple_b_gate"], sp["ln2_g"], sp["ln2_b"])
    loss = hacc[3, 0]
    gW, gs = {}, {}
    gs["ple_b_gate"], gs["ln2_g"], gs["ln2_b"] = hacc[0:1], hacc[1:2], hacc[2:3]
    gW["ple_w_proj"] = _mm(pT, dpp, "nn", name="mm_dproj", out_dtype=BF)
    gW["ple_w_gate"] = _mm(h1T, dgpre, "nn", name="mm_dgate", out_dtype=BF)
    gW["ffn_w_down"] = _mm(act_t, dffnb, "nn", name="mm_ddown", out_dtype=BF)
    dact = _mm(dffnb, W["ffn_w_down"], "nt", name="mm_dact")
    du_g, du_u, dcw_g, dcw_u, dcb_g, dcb_u = _ffn_act_bwd(ug, uu, cw_g, cw_u, cb_g, cb_u, dact, Bl, S)
    gW["ffn_conv_w"] = jnp.concatenate([dcw_g, dcw_u], axis=1)
    gs["ffn_conv_b"] = jnp.concatenate([dcb_g, dcb_u], axis=1)
    gW["ffn_w_up"] = jnp.concatenate([_mm(h1T, du_g, "nn", name="mm_dup_gate", out_dtype=BF),
                                      _mm(h1T, du_u, "nn", name="mm_dup_up", out_dtype=BF)], axis=1)
    dh1 = _mm(dgpre, W["ple_w_gate"], "nt", name="mm_dh1_gate", add=dffn, add_scale=ALPHA)
    dh1 = _mm(du_g, wup_g, "nt", name="mm_dh1_upg", add=dh1)
    dh1 = _mm(du_u, wup_u, "nt", name="mm_dh1_upu", add=dh1)
    dxa, dmix, acc1 = _ln1_bwd(x, mix, sp["ln1_g"], sp["ln1_b"], dh1)
    gs["ln1_g"], gs["ln1_b"] = acc1[0:1], acc1[1:2]
    gW["w_out"] = _mm(mixin_t, dmix, "nn", name="mm_dwout", out_dtype=BF)
    dmixin = _mm(dmix, W["w_out"], "nt", name="mm_dmixin")
    do, dcat, gacc = _gdn_post_bwd(o, z, sp["gdn_norm_g"], dmixin, T)
    gs["gdn_norm_g"] = gacc[0:1]
    bwd_out = _gdn_bwd(qkvn, sm, alog, dtb, gc_s, beta_s, tinv_s, sst_s, w_s, u_s, do, dcat, Bl, S,
                       xchg=None if early_grads is None else early_grads({n: gW[n] for n in GRADS_EARLY}))
    dqkvn, dcat, cacc = bwd_out[:3]
    early_recv = bwd_out[3:]
    gs["gdn_a_log"], gs["gdn_dt_bias"] = cacc[0:1, :GDN_HEADS], cacc[1:2, :GDN_HEADS]
    dcat, gW["gdn_conv_w"] = _gdn_pre_bwd(qkv, W["gdn_conv_w"], dqkvn, dcat, Bl, S)
    dqraw, dqr, dkv, dkrr = _attn_bwd(qraw, qr, kv, krr, lse, dmixin, Bl, S)
    dqraw, dcat = _rope_bwd(dqraw, dcat, dqr, dkrr, cos, sin, S)
    gwq = _mm(cqn_t, dqraw, "nn", name="mm_dwq", out_dtype=BF)
    gW["mla_w_q_up"] = gwq.reshape(-1, MLA_HEADS, 256)[:, :, :192].reshape(-1, MLA_HEADS * 192)
    gW["mla_w_kv_up"] = _mm(ckvn_t, dkv, "nn", name="mm_dwkv", out_dtype=BF)
    dcqn = _mm(dqraw, wq, "nt", name="mm_dcqn")
    dckvn = _mm(dkv, wkv, "nt", name="mm_dckvn")
    dcat, qacc = _rms_bwd(sm, sp["mla_q_norm_g"], dcqn, dcat, 0, "rms_q_bwd")
    dcat, kacc = _rms_bwd(sm, sp["mla_kv_norm_g"], dckvn, dcat, 2, "rms_kv_bwd")
    gs["mla_q_norm_g"], gs["mla_kv_norm_g"] = qacc[0:1], kacc[0:1]
    gcat = _mm(xT, dcat, "nn", name="mm_dwin", out_dtype=BF)
    gsm, gqkv, gz = gcat[:, :DCAT_QKV], gcat[:, DCAT_QKV:DCAT_Z], gcat[:, DCAT_Z:]
    gW["w_in"] = jnp.concatenate([gqkv, gz, gsm[:, 768:772], gsm[:, 896:900], gsm[:, 0:384], gsm[:, 512:768],
                                  gsm[:, 384:448]], axis=1)
    wcat = jnp.concatenate([wsm, wqkv, wz], axis=1)
    dx = _mm(dcat, wcat, "nt", name="mm_dx", add=dxa, xchg=None if late_grads is None else late_grads(gW, gs))
    late_recv = ()
    if late_grads is not None:
        dx, late_recv = dx[0], dx[1:]
    return loss, dx, gW, gs, early_recv, late_recv


COL_SHARDED = ("w_in", "mla_w_q_up", "mla_w_kv_up", "ffn_w_up", "ple_w_proj", "gdn_conv_w", "ffn_conv_w")
SHARDED = EARLY + LATE
WEIGHTS = ("w_in", "gdn_conv_w", "gdn_a_log", "gdn_dt_bias", "gdn_norm_g", "mla_q_norm_g", "mla_w_q_up",
           "mla_kv_norm_g", "mla_w_kv_up", "w_out", "ln1_g", "ln1_b", "ffn_w_up", "ffn_conv_w", "ffn_conv_b",
           "ffn_w_down", "ple_w_gate", "ple_b_gate", "ple_w_proj", "ln2_g", "ln2_b")
CONV = ("gdn_conv_w", "ffn_conv_w")
SMALL_ROWS, SMALL_LANES = 96, 128


def _gathered_to_full(name, g):
    if name in COL_SHARDED:
        return jnp.transpose(g, (1, 0, 2)).reshape(g.shape[1], -1)
    return g.reshape(-1, g.shape[-1])


def _full_to_blocks(name, gfull, shard_shape):
    r, c = shard_shape
    if name in COL_SHARDED:
        return jnp.transpose(gfull.reshape(r, N_DEV, c), (1, 0, 2))
    return gfull.reshape(N_DEV, r, c)


def _pack_small(d):
    flat = jnp.concatenate([d[n].reshape(-1) for n in SMALL_ORDER])
    return jnp.pad(flat, (0, SMALL_ROWS * SMALL_LANES - flat.shape[0])).reshape(SMALL_ROWS, SMALL_LANES)


def _unpack_small(packed, shapes):
    flat = packed.reshape(-1)
    out, off = {}, 0
    for n in SMALL_ORDER:
        sz = math.prod(shapes[n])
        out[n] = flat[off:off + sz].reshape(shapes[n])
        off += sz
    return out


def kernel(x, p, w_in, gdn_conv_w, gdn_a_log, gdn_dt_bias, gdn_norm_g, mla_q_norm_g, mla_w_q_up, mla_kv_norm_g, mla_w_kv_up, w_out, ln1_g, ln1_b, ffn_w_up, ffn_conv_w, ffn_conv_b, ffn_w_down, ple_w_gate, ple_b_gate, ple_w_proj, ln2_g, ln2_b, loss_target, m_w_in, m_gdn_conv_w, m_gdn_a_log, m_gdn_dt_bias, m_gdn_norm_g, m_mla_q_norm_g, m_mla_w_q_up, m_mla_kv_norm_g, m_mla_w_kv_up, m_w_out, m_ln1_g, m_ln1_b, m_ffn_w_up, m_ffn_conv_w, m_ffn_conv_b, m_ffn_w_down, m_ple_w_gate, m_ple_b_gate, m_ple_w_proj, m_ln2_g, m_ln2_b, v_w_in, v_gdn_conv_w, v_gdn_a_log, v_gdn_dt_bias, v_gdn_norm_g, v_mla_q_norm_g, v_mla_w_q_up, v_mla_kv_norm_g, v_mla_w_kv_up, v_w_out, v_ln1_g, v_ln1_b, v_ffn_w_up, v_ffn_conv_w, v_ffn_conv_b, v_ffn_w_down, v_ple_w_gate, v_ple_b_gate, v_ple_w_proj, v_ln2_g, v_ln2_b):
    loc = dict(locals())
    wts = {n: loc[n] for n in WEIGHTS}
    ms = {n: loc["m_" + n] for n in WEIGHTS}
    vs = {n: loc["v_" + n] for n in WEIGHTS}
    Bl, S, D = x.shape
    T = Bl * S

    wire = lambda n: wts[n][0] if n in CONV else wts[n][0].astype(BF)
    gather = lambda names: ([wire(n) for n in names], ["ag"] * len(names),
                            lambda res: {n: _gathered_to_full(n, g) for n, g in zip(names, res)})
    blocks = lambda g, names: [_full_to_blocks(n, g[n], wts[n].shape[1:]) for n in names]
    pack_early = lambda g: (blocks(g, GRADS_EARLY), ["a2a"] * len(GRADS_EARLY))
    pack_late = lambda g, gs: (blocks(g, GRADS_LATE) + [_pack_small(gs)], ["a2a"] * len(GRADS_LATE) + ["ag"])
    sp = {n: wts[n].reshape(1, -1) for n in SMALL_ORDER}

    loss, dx, gW, gs, early_recv, late_recv = _local_step(
        x.reshape(T, D), p.reshape(T, -1), loss_target.reshape(T, D), {}, sp, Bl, S,
        early_weights=gather(EARLY), late_weights=gather(LATE), early_grads=pack_early, late_grads=pack_late)

    res = {}
    for n, parts in list(zip(GRADS_EARLY, early_recv)) + list(zip(GRADS_LATE, late_recv[:-1])):
        res[n] = tuple(t[None] for t in _adamw(parts, wts[n][0], ms[n][0], vs[n][0], "adamw_" + n))
    small = _adamw(late_recv[-1], _pack_small(wts), _pack_small(ms), _pack_small(vs), "adamw_small")
    shapes = {n: wts[n].shape for n in SMALL_ORDER}
    small = [_unpack_small(t, shapes) for t in small]
    for n in SMALL_ORDER:
        res[n] = tuple(t[n] for t in small)

    loss = lax.psum(loss, ("x", "y", "c"))
    outs = [loss, dx.reshape(Bl, S, D)]
    for k in range(4):
        outs += [res[n][k] for n in WEIGHTS]
    return tuple(outs)
```

```python
import math

import jax
import jax.numpy as jnp
from jax import lax
from jax.experimental import pallas as pl
from jax.experimental.pallas import tpu as pltpu

F32 = jnp.float32
BF = jnp.bfloat16
_MXU = jnp.bfloat16
_INTERPRET = None
_VMEM_LIMIT = 56 * 1024 * 1024
HI = lax.Precision.HIGHEST

N_DEV = 8
CHUNK = 64
GDN_HEADS = 4
GDN_DK = 128
MLA_HEADS = 4
MLA_NOPE = 128
MLA_ROPE = 64
ROPE_THETA = 10000.0
ALPHA = 2.0 ** 0.25
NORM_EPS = 1e-6
ATTN_SCALE = (MLA_NOPE + MLA_ROPE) ** -0.5
DCAT_QKV, DCAT_Z, DCAT_W = 1024, 2560, 3072
ADAM_LR, ADAM_B1, ADAM_B2, ADAM_EPS, ADAM_WD, ADAM_STEP = 0.001, 0.9, 0.999, 1e-08, 0.01, 10
MESH = pl.DeviceIdType.MESH
ANY = pl.BlockSpec(memory_space=pl.ANY)
_NN = (((1,), (0,)), ((), ()))
_NT = (((1,), (1,)), ((), ()))
_TN = (((0,), (0,)), ((), ()))


def _sds(shape, dtype=F32):
    return jax.ShapeDtypeStruct(tuple(shape), dtype)


def _pick(n, cands):
    for c in cands:
        if n % c == 0:
            return c
    return n


def _xchg_out_shapes(arrs, modes):
    return [_sds(a.shape if m == "a2a" else (N_DEV,) + a.shape, a.dtype) for a, m in zip(arrs, modes)]


def _xchg_scratch(n):
    return [pltpu.SemaphoreType.DMA((n, N_DEV - 1)), pltpu.SemaphoreType.DMA((n, N_DEV - 1)),
            pltpu.SemaphoreType.DMA((n,))]


def _xchg_plan(ins, outs, send, recv, loc, modes):
    x, y, c = lax.axis_index("x"), lax.axis_index("y"), lax.axis_index("c")
    me = 4 * x + 2 * y + c
    starts, relays, waits = [], [], []
    for ai, mode in enumerate(modes):
        src_all, out = ins[ai], outs[ai]

        def remote(k, src, dst, dev):
            return pltpu.make_async_remote_copy(src_ref=src, dst_ref=dst, send_sem=send.at[ai, k], recv_sem=recv.at[ai, k],
                                                device_id=dev, device_id_type=MESH)

        own = pltpu.make_async_copy(src_all.at[me] if mode == "a2a" else src_all, out.at[me], loc.at[ai])
        starts.append(own)
        waits.append(own.wait)
        if mode == "ag2":
            sib = (x, y, 1 - c)
            chips = [(1 - x, y), (x, 1 - y), (1 - x, 1 - y)]
            first = [remote(0, src_all, out.at[me], sib)]
            first += [remote(1 + k, src_all, out.at[me], (px, py, c)) for k, (px, py) in enumerate(chips)]
            starts += first
            waits += [cp.wait_send for cp in first]
            waits.append(remote(0, src_all, out.at[4 * x + 2 * y + 1 - c], sib).wait_recv)
            for k, (px, py) in enumerate(chips):
                same, other = out.at[4 * px + 2 * py + c], out.at[4 * px + 2 * py + 1 - c]
                relay = remote(4 + k, same, same, sib)
                relays.append((remote(1 + k, src_all, same, (px, py, c)), relay))
                waits += [relay.wait_send, remote(4 + k, src_all, other, sib).wait_recv]
            continue
        for r in range(1, N_DEV):
            px = 1 - x if (r >> 2) & 1 else x
            py = 1 - y if (r >> 1) & 1 else y
            pc = 1 - c if r & 1 else c
            cp = remote(r - 1, src_all.at[4 * px + 2 * py + pc] if mode == "a2a" else src_all, out.at[me], (px, py, pc))
            starts.append(cp)
            waits.append(cp.wait)
    return starts, relays, waits


def _pc(body, *, name, out_shape, grid=None, in_specs=None, out_specs=None, scratch=(), sem=None, aliases=None,
        xchg=None):
    kw = {}
    if _INTERPRET is not None:
        kw["interpret"] = _INTERPRET
    single = not isinstance(out_shape, (tuple, list))
    out_shape = [out_shape] if single else list(out_shape)
    if out_specs is not None:
        out_specs = [out_specs] if single else list(out_specs)
    scratch = list(scratch)
    if xchg is not None:
        xarrs, xmodes = xchg
        n_in, n_out, n_scr, nx = len(in_specs), len(out_shape), len(scratch), len(xarrs)
        single = False
        inner = body

        def body(*refs):
            ins, xins = refs[:n_in], refs[n_in:n_in + nx]
            outs = refs[n_in + nx:n_in + nx + n_out]
            xouts = refs[n_in + nx + n_out:n_in + 2 * nx + n_out]
            scr = refs[n_in + 2 * nx + n_out:n_in + 2 * nx + n_out + n_scr]
            send, recv, loc = refs[n_in + 2 * nx + n_out + n_scr:]
            first = last = None
            for d, g in enumerate(grid):
                f, l = pl.program_id(d) == 0, pl.program_id(d) == g - 1
                first, last = (f, l) if first is None else (first & f, last & l)

            @pl.when(first)
            def _():
                for cp in _xchg_plan(xins, xouts, send, recv, loc, xmodes)[0]:
                    cp.start()

            inner(*ins, *outs, *scr)

            @pl.when(last)
            def _():
                _, relays, waits = _xchg_plan(xins, xouts, send, recv, loc, xmodes)
                for arrival, relay in relays:
                    arrival.wait_recv()
                    relay.start()
                for wait in waits:
                    wait()

        in_specs = list(in_specs) + [ANY] * nx
        out_specs = out_specs + [ANY] * nx
        out_shape = out_shape + _xchg_out_shapes(xarrs, xmodes)
        scratch = scratch + _xchg_scratch(nx)
        sem = ("arbitrary",) * len(grid)
    if grid is not None:
        kw["grid"] = grid
    if in_specs is not None:
        kw["in_specs"] = in_specs
    if out_specs is not None:
        kw["out_specs"] = out_specs[0] if single else tuple(out_specs)
    cp = dict(vmem_limit_bytes=_VMEM_LIMIT)
    if sem is not None:
        cp["dimension_semantics"] = sem
    call = pl.pallas_call(body, name=name, out_shape=out_shape[0] if single else tuple(out_shape),
                          scratch_shapes=scratch, input_output_aliases=aliases or {},
                          compiler_params=pltpu.CompilerParams(**cp), **kw)
    if xchg is None:
        return call
    return lambda *ins: call(*ins, *xchg[0])


_FULL_K_MAX = 3072


def _mm(a, b, mode, *, name, add=None, add_scale=1.0, out_dtype=F32, xchg=None):
    (M, K) = a.shape
    (K2, N) = b.shape if mode == "nn" else b.shape[::-1]
    assert K == K2, (a.shape, b.shape, mode)
    tm = _pick(M, (1024, 1408, 512, 384, 256, 128))
    tn = _pick(N, (1024, 2816, 1408, 768, 512, 384, 256, 128))
    tk = K if K <= _FULL_K_MAX else _pick(K, (2048, 1408, 1024, 512) if tn <= 1408 else (1024, 512))
    nk = K // tk
    dims = _NN if mode == "nn" else _NT
    has_add = add is not None

    def finish(r, add_ref, o_ref):
        if has_add:
            r = r + add_scale * add_ref[...].astype(F32)
        o_ref[...] = r.astype(out_dtype)

    if nk == 1:
        def body(a_ref, b_ref, *rest):
            r = lax.dot_general(a_ref[...].astype(_MXU), b_ref[...].astype(_MXU), dims, preferred_element_type=F32)
            finish(r, rest[0] if has_add else None, rest[-1])

        a_spec = pl.BlockSpec((tm, K), lambda i, j: (i, 0))
        b_spec = (pl.BlockSpec((K, tn), lambda i, j: (0, j)) if mode == "nn"
                  else pl.BlockSpec((tn, K), lambda i, j: (j, 0)))
        o_spec = pl.BlockSpec((tm, tn), lambda i, j: (i, j))
        grid, sem, scratch = (M // tm, N // tn), ("parallel", "parallel"), []
    else:
        def body(a_ref, b_ref, *rest):
            o_ref, acc = rest[-2], rest[-1]
            k = pl.program_id(2)

            @pl.when(k == 0)
            def _():
                acc[...] = jnp.zeros_like(acc)

            acc[...] += lax.dot_general(a_ref[...].astype(_MXU), b_ref[...].astype(_MXU), dims,
                                        preferred_element_type=F32)

            @pl.when(k == nk - 1)
            def _():
                finish(acc[...], rest[0] if has_add else None, o_ref)

        a_spec = pl.BlockSpec((tm, tk), lambda i, j, k: (i, k))
        b_spec = (pl.BlockSpec((tk, tn), lambda i, j, k: (k, j)) if mode == "nn"
                  else pl.BlockSpec((tn, tk), lambda i, j, k: (j, k)))
        o_spec = pl.BlockSpec((tm, tn), lambda i, j, k: (i, j))
        grid, sem, scratch = (M // tm, N // tn, nk), ("parallel", "parallel", "arbitrary"), [pltpu.VMEM((tm, tn), F32)]

    ins = [a, b] + ([add] if has_add else [])
    specs = [a_spec, b_spec] + ([o_spec] if has_add else [])
    return _pc(body, name=name, out_shape=_sds((M, N), out_dtype), grid=grid, in_specs=specs, out_specs=o_spec,
               scratch=scratch, sem=sem, xchg=xchg)(*ins)


def _dot(a, b, dims):
    return lax.dot_general(a.astype(_MXU), b.astype(_MXU), dims, preferred_element_type=F32)


def _split(a):
    hi = a.astype(BF)
    lo = (a - hi.astype(F32)).astype(BF)
    return hi, lo


def _dot3(a, b, dims):
    ah, al = _split(a)
    bh, bl = _split(b)
    d = lambda u, v: lax.dot_general(u, v, dims, preferred_element_type=F32)
    return d(ah, bh) + (d(ah, bl) + d(al, bh))


def _softplus(x):
    return jnp.maximum(x, 0.0) + jnp.log1p(jnp.exp(-jnp.abs(x)))


def _silu(x):
    return x * jax.nn.sigmoid(x)


def _rmsnorm(x, g):
    return x * lax.rsqrt(jnp.mean(x * x, axis=-1, keepdims=True) + NORM_EPS) * g


def _layernorm(x, g, b):
    mu = jnp.mean(x, axis=-1, keepdims=True)
    xc = x - mu
    var = jnp.mean(xc * xc, axis=-1, keepdims=True)
    return xc * lax.rsqrt(var + NORM_EPS) * g + b


def _shift_dn(x, s):
    if s == 0:
        return x
    t = lax.broadcasted_iota(jnp.int32, x.shape, 0)
    return jnp.where(t >= s, pltpu.roll(x, s, 0), 0.0)


def _shift_up(x, s):
    if s == 0:
        return x
    n = x.shape[0]
    t = lax.broadcasted_iota(jnp.int32, x.shape, 0)
    return jnp.where(t < n - s, pltpu.roll(x, n - s, 0), 0.0)


def _taps(x, kk):
    return [_shift_dn(x, kk - 1 - j) for j in range(kk)]


def _conv(x, w, taps=None):
    kk = w.shape[0]
    taps = _taps(x, kk) if taps is None else taps
    y = w[kk - 1:kk, :] * taps[kk - 1]
    for j in range(kk - 1):
        y = y + w[j:j + 1, :] * taps[j]
    return y


def _conv_bwd_x(dy, w):
    kk = w.shape[0]
    dx = w[kk - 1:kk, :] * dy
    for j in range(kk - 1):
        dx = dx + w[j:j + 1, :] * _shift_up(dy, kk - 1 - j)
    return dx


def _conv_bwd_w(taps, dy, dw_ref, first):
    kk = dw_ref.shape[0]
    for j in range(kk):
        r = jnp.sum(dy * taps[j], axis=0, keepdims=True)
        prev = jnp.where(first, 0.0, dw_ref[j:j + 1, :])
        dw_ref[j:j + 1, :] = prev + r


def _gdn_post_conv(c, j):
    h = _silu(c)
    hn = h * lax.rsqrt(jnp.sum(h * h, axis=-1, keepdims=True) + NORM_EPS)
    return jnp.where(j < GDN_HEADS, hn * (GDN_DK ** -0.5), jnp.where(j < 2 * GDN_HEADS, hn, h))


def _gdn_pre_fwd(qkv, conv_w, Bl, S):
    T, C = qkv.shape
    nj = C // 128

    def body(x_ref, w_ref, o_ref):
        o_ref[...] = _gdn_post_conv(_conv(x_ref[...], w_ref[...]), pl.program_id(1))

    blk = pl.BlockSpec((S, 128), lambda b, j: (b, j))
    return _pc(body, name="gdn_pre_fwd", out_shape=_sds((T, C)), grid=(Bl, nj),
               in_specs=[blk, pl.BlockSpec((conv_w.shape[0], 128), lambda b, j: (0, j))], out_specs=blk,
               sem=("parallel", "parallel"))(qkv, conv_w)


def _gdn_pre_bwd(qkv, conv_w, dout, dcat, Bl, S):
    T, C = qkv.shape
    nj = C // 128
    kk = conv_w.shape[0]

    def body(x_ref, w_ref, d_ref, alias_ref, dx_ref, dw_ref):
        j, b = pl.program_id(0), pl.program_id(1)
        w = w_ref[...]
        taps = _taps(x_ref[...], kk)
        c = _conv(None, w, taps)
        _, vjp = jax.vjp(lambda u: _gdn_post_conv(u, j), c)
        (dc,) = vjp(d_ref[...])
        dx_ref[...] = _conv_bwd_x(dc, w).astype(dx_ref.dtype)
        _conv_bwd_w(taps, dc, dw_ref, b == 0)

    blk = pl.BlockSpec((S, 128), lambda j, b: (b, j))
    wblk = pl.BlockSpec((kk, 128), lambda j, b: (0, j))
    return _pc(body, name="gdn_pre_bwd", out_shape=(_sds(dcat.shape, dcat.dtype), _sds((kk, C))), grid=(nj, Bl),
               in_specs=[blk, wblk, blk, ANY],
               out_specs=(pl.BlockSpec((S, 128), lambda j, b: (b, DCAT_QKV // 128 + j)), wblk),
               aliases={3: 0}, sem=("parallel", "arbitrary"))(qkv, conv_w, dout, dcat)


_H = GDN_HEADS
_HC = _H * CHUNK


def _st(x):
    return jnp.concatenate([x[:, h * 128:(h + 1) * 128] for h in range(_H)], axis=0)


def _unst(x):
    return jnp.concatenate([x[h * CHUNK:(h + 1) * CHUNK] for h in range(_H)], axis=1)


def _stc(t):
    return jnp.concatenate([t[:, h:h + 1] for h in range(_H)], axis=0)


def _untile(col):
    lane = lax.broadcasted_iota(jnp.int32, (CHUNK, 128), 1)
    out = jnp.zeros((CHUNK, 128), F32)
    for h in range(_H):
        out = out + jnp.where(lane == h, col[h * CHUNK:(h + 1) * CHUNK], 0.0)
    return out


def _rowform(col):
    return jnp.broadcast_to(col, (_HC, 128)).T[0:1, :]


def _tri(n, upper=False):
    i = lax.broadcasted_iota(jnp.int32, (n, n), 0)
    j = lax.broadcasted_iota(jnp.int32, (n, n), 1)
    return jnp.where((j >= i) if upper else (j <= i), 1.0, 0.0).astype(F32)


def _gdn_chunk_common(q, k, v, gc, beta):
    Q, K, V = _st(q), _st(k), _st(v)
    B, GC = _stc(beta), _stc(gc)
    GL = jnp.concatenate([jnp.broadcast_to(gc[CHUNK - 1:CHUNK, h:h + 1], (CHUNK, 1)) for h in range(_H)], axis=0)
    ii = lax.broadcasted_iota(jnp.int32, (_HC, _HC), 0)
    jj = lax.broadcasted_iota(jnp.int32, (_HC, _HC), 1)
    same = (ii >> 6) == (jj >> 6)
    incl = same & (ii >= jj)
    strict = same & (ii > jj)
    diff = GC - _rowform(GC)
    D = jnp.where(incl, jnp.exp(jnp.where(incl, diff, 0.0)), 0.0)
    KB = K * B
    A = jnp.where(strict, _dot(KB, K, _NT) * D, 0.0)
    P = jnp.where(incl, _dot(Q, K, _NT) * D, 0.0)
    EG = jnp.exp(GC)
    ED = jnp.exp(GL - GC)
    return dict(Q=Q, K=K, V=V, B=B, GC=GC, GL=GL, incl=incl, strict=strict, D=D, KB=KB, A=A, P=P, EG=EG, ED=ED,
                QG=Q * EG, KD=K * ED)


def _neumann_inv(A):
    n = A.shape[0]
    i = lax.broadcasted_iota(jnp.int32, (n, n), 0)
    j = lax.broadcasted_iota(jnp.int32, (n, n), 1)
    N = -A
    acc = jnp.where(i == j, 1.0, 0.0) + N
    Pw = N
    for _ in range(5):
        Pw = _dot3(Pw, Pw, _NN)
        acc = acc + _dot3(acc, Pw, _NN)
    return acc


def _gates(a, alog, dtb):
    return -jnp.exp(alog) * _softplus(a + dtb)


def _hs(x, h, n=CHUNK):
    return x[h * n:(h + 1) * n]


def _per_step(Bl):
    return 2 if Bl % 2 == 0 else 1


def _gdn_fwd(qkvn, sm, alog, dtb, Bl, S, xchg=None):
    T = Bl * S
    nC = S // CHUNK
    E = _per_step(Bl)

    def body(*refs):
        S_ref = refs[-1]

        @pl.when(pl.program_id(1) == 0)
        def _():
            S_ref[...] = jnp.zeros_like(S_ref)

        for e in range(E):
            one(*[r if i in (5, 6) else r.at[e] for i, r in enumerate(refs)])

    def one(q_ref, k_ref, v_ref, a_ref, b_ref, alog_ref, dtb_ref,
            o_ref, tinv_ref, sst_ref, w_ref, u_ref, gc_ref, beta_ref, S_ref):
        g = _gates(a_ref[...], alog_ref[...], dtb_ref[...])
        beta = jax.nn.sigmoid(b_ref[...])
        gc = jnp.dot(_tri(CHUNK), g, precision=HI, preferred_element_type=F32)
        cm = _gdn_chunk_common(q_ref[...], k_ref[...], v_ref[...], gc, beta)
        tinv = _neumann_inv(cm["A"])
        W = _dot3(tinv, cm["KB"] * cm["EG"], _NN)
        U = _dot3(tinv, cm["V"] * cm["B"], _NN)
        s_old = [S_ref[h * 128:(h + 1) * 128, :] for h in range(_H)]
        vn = [_hs(U, h) - _dot(_hs(W, h), s_old[h], _NN) for h in range(_H)]
        o_intra = _dot(cm["P"], jnp.concatenate(vn, axis=0), _NN)
        outs = []
        for h in range(_H):
            outs.append(_dot(_hs(cm["QG"], h), s_old[h], _NN) + _hs(o_intra, h))
            gl = jnp.exp(gc[CHUNK - 1:CHUNK, h:h + 1])
            S_ref[h * 128:(h + 1) * 128, :] = gl * s_old[h] + _dot(_hs(cm["KD"], h), vn[h], _TN)
            sst_ref[h * 128:(h + 1) * 128, :] = s_old[h]
        o_ref[...] = jnp.concatenate(outs, axis=1)
        tinv_ref[...] = tinv
        w_ref[...] = _unst(W)
        u_ref[...] = _unst(U)
        gc_ref[...] = gc
        beta_ref[...] = beta

    blk = lambda r, w, c: pl.BlockSpec((E, r, w), lambda b, n: (b, n, c))
    par = pl.BlockSpec((1, 128), lambda b, n: (0, 0))
    out_shape = (_sds((Bl, S, 512)), _sds((Bl, nC * _HC, _HC)), _sds((Bl, nC * 512, 128)),
                 _sds((Bl, S, 512)), _sds((Bl, S, 512)), _sds((Bl, S, 128)), _sds((Bl, S, 128)))
    out_specs = (blk(CHUNK, 512, 0), blk(_HC, _HC, 0), blk(512, 128, 0), blk(CHUNK, 512, 0), blk(CHUNK, 512, 0),
                 blk(CHUNK, 128, 0), blk(CHUNK, 128, 0))
    q3, s3 = qkvn.reshape(Bl, S, -1), sm.reshape(Bl, S, -1)
    res = _pc(body, name="gdn_chunk_fwd", out_shape=out_shape, grid=(Bl // E, nC),
              in_specs=[blk(CHUNK, 512, 0), blk(CHUNK, 512, 1), blk(CHUNK, 512, 2), blk(CHUNK, 128, 6),
                        blk(CHUNK, 128, 7), par, par],
              out_specs=out_specs, scratch=[pltpu.VMEM((E, 512, 128), F32)], sem=("parallel", "arbitrary"),
              xchg=xchg)(q3, q3, q3, s3, s3, alog, dtb)
    return (res[0].reshape(T, 512),) + tuple(res[1:])


def _gdn_bwd(qkvn, sm, alog, dtb, gc_s, beta_s, tinv_s, sst_s, w_s, u_s, do, dcat, Bl, S, xchg=None):
    T = Bl * S
    nC = S // CHUNK
    E = _per_step(Bl)

    def body(*refs):
        acc_ref, dS_ref = refs[-2], refs[-1]
        first_chunk = pl.program_id(1) == 0

        @pl.when(first_chunk)
        def _():
            dS_ref[...] = jnp.zeros_like(dS_ref)

        @pl.when(first_chunk & (pl.program_id(0) == 0))
        def _():
            acc_ref[...] = jnp.zeros_like(acc_ref)

        for e in range(E):
            one(*[r if i in (4, 5, 13, 16) else r.at[e] for i, r in enumerate(refs)])

    def one(q_ref, k_ref, v_ref, a_ref, alog_ref, dtb_ref, gc_ref, beta_ref, tinv_ref, sst_ref, w_ref, u_ref,
            do_ref, alias_ref, dqkv_ref, dsm_ref, acc_ref, dS_ref):
        gc, beta = gc_ref[...], beta_ref[...]
        cm = _gdn_chunk_common(q_ref[...], k_ref[...], v_ref[...], gc, beta)
        Q, K, V, B, D, KB, A, P = (cm[n] for n in ("Q", "K", "V", "B", "D", "KB", "A", "P"))
        EG, ED, QG, KD = cm["EG"], cm["ED"], cm["QG"], cm["KD"]
        tinv = tinv_ref[...]
        W, U, DO = _st(w_ref[...]), _st(u_ref[...]), _st(do_ref[...])
        s_old = [sst_ref[h * 128:(h + 1) * 128, :] for h in range(_H)]
        ds_new = [dS_ref[h * 128:(h + 1) * 128, :] for h in range(_H)]
        VN = jnp.concatenate([_hs(U, h) - _dot(_hs(W, h), s_old[h], _NN) for h in range(_H)], axis=0)
        dP = jnp.where(cm["incl"], _dot(DO, VN, _NT), 0.0)
        dVN0 = _dot(P, DO, _TN)
        dVN, dQG, dKD, dW, TL = [], [], [], [], []
        for h in range(_H):
            gl = jnp.exp(gc[CHUNK - 1:CHUNK, h:h + 1])
            dvn = _hs(dVN0, h) + _dot(_hs(KD, h), ds_new[h], _NN)
            dkd = _dot(_hs(VN, h), ds_new[h], _NT)
            dVN.append(dvn)
            dQG.append(_dot(_hs(DO, h), s_old[h], _NT))
            dKD.append(dkd)
            dW.append(-_dot(dvn, s_old[h], _NT))
            dgl = jnp.sum(jnp.sum(ds_new[h] * s_old[h], axis=1, keepdims=True), axis=0, keepdims=True)
            tl = jnp.sum(jnp.sum(dkd * _hs(KD, h), axis=1, keepdims=True), axis=0, keepdims=True) + dgl * gl
            TL.append(jnp.broadcast_to(tl, (CHUNK, 1)))
            dS_ref[h * 128:(h + 1) * 128, :] = (_dot(_hs(QG, h), _hs(DO, h), _TN) + gl * ds_new[h]
                                                - _dot(_hs(W, h), dvn, _TN))
        dVN, dQG, dKD, dW, TL = (jnp.concatenate(z, axis=0) for z in (dVN, dQG, dKD, dW, TL))
        dVB = _dot3(tinv, dVN, _TN)
        dKBE = _dot3(tinv, dW, _TN)
        dA = jnp.where(cm["strict"], -(_dot3(dVB, U, _NT) + _dot3(dKBE, W, _NT)), 0.0)
        dG = dA * D
        dQK = dP * D
        dKB = _dot(dG, K, _NN) + dKBE * EG
        dK = _dot(dG, KB, _TN) + _dot(dQK, Q, _TN) + dKB * B + dKD * ED
        dQ = _dot(dQK, K, _NN) + dQG * EG
        Mx = dA * A + dP * P
        rs = lambda z: jnp.sum(z, axis=1, keepdims=True)
        ri = lax.broadcasted_iota(jnp.int32, (_HC, 1), 0)
        dGC = (rs(Mx) - rs(Mx.T) + rs(dKBE * KB * EG) + rs(dQG * QG) - rs(dKD * KD)
               + jnp.where((ri & (CHUNK - 1)) == CHUNK - 1, TL, 0.0))
        dBeta = rs(dVB * V) + rs(dKB * K)
        dqkv_ref[:, 0:512] = _unst(dQ)
        dqkv_ref[:, 512:1024] = _unst(dK)
        dqkv_ref[:, 1024:1536] = _unst(dVB * B)
        dg = jnp.dot(_tri(CHUNK, upper=True), _untile(dGC), precision=HI, preferred_element_type=F32)
        a, alog_v, dtb_v = a_ref[...], alog_ref[...], dtb_ref[...]
        g = _gates(a, alog_v, dtb_v)
        lane = lax.broadcasted_iota(jnp.int32, (CHUNK, 128), 1)
        valid = lane < _H
        da = jnp.where(valid, dg * (-jnp.exp(alog_v)) * jax.nn.sigmoid(a + dtb_v), 0.0)
        dsm_ref[:, 0:128] = da.astype(dsm_ref.dtype)
        dsm_ref[:, 128:256] = jnp.where(valid, _untile(dBeta) * beta * (1.0 - beta), 0.0).astype(dsm_ref.dtype)
        acc_ref[0:1, :] += jnp.sum(jnp.where(valid, dg * g, 0.0), axis=0, keepdims=True)
        acc_ref[1:2, :] += jnp.sum(da, axis=0, keepdims=True)

    blk = lambda r, w, c: pl.BlockSpec((E, r, w), lambda b, n: (b, nC - 1 - n, c))
    par = pl.BlockSpec((1, 128), lambda b, n: (0, 0))
    in_specs = [blk(CHUNK, 512, 0), blk(CHUNK, 512, 1), blk(CHUNK, 512, 2), blk(CHUNK, 128, 6), par, par,
                blk(CHUNK, 128, 0), blk(CHUNK, 128, 0), blk(_HC, _HC, 0), blk(512, 128, 0),
                blk(CHUNK, 512, 0), blk(CHUNK, 512, 0), blk(CHUNK, 512, 0), ANY]
    out_shape = (_sds((Bl, S, 1536)), _sds((Bl, S, DCAT_W), BF), _sds((8, 128)))
    out_specs = (blk(CHUNK, 1536, 0), blk(CHUNK, 256, 3), pl.BlockSpec((8, 128), lambda b, n: (0, 0)))
    q3, s3 = qkvn.reshape(Bl, S, -1), sm.reshape(Bl, S, -1)
    res = _pc(body, name="gdn_chunk_bwd", out_shape=out_shape, grid=(Bl // E, nC), in_specs=in_specs,
              out_specs=out_specs, scratch=[pltpu.VMEM((E, 512, 128), F32)], sem=("arbitrary", "arbitrary"),
              aliases={13: 1}, xchg=xchg)(q3, q3, q3, s3, alog, dtb, gc_s, beta_s, tinv_s, sst_s, w_s, u_s,
                                          do.reshape(Bl, S, -1), dcat.reshape(Bl, S, DCAT_W))
    return (res[0].reshape(T, 1536), res[1].reshape(T, DCAT_W)) + tuple(res[2:])


def _gdn_post(o, z, g, T):
    tt = _pick(T, (2048, 1024, 512, 256))

    def body(o_ref, z_ref, g_ref, out_ref, outt_ref):
        y = (_rmsnorm(o_ref[...], g_ref[...]) * _silu(z_ref[...])).astype(out_ref.dtype)
        out_ref[...] = y
        outt_ref[...] = y.T

    blk = pl.BlockSpec((tt, 128), lambda i, h: (i, h))
    return _pc(body, name="gdn_post_fwd", out_shape=(_sds((T, 1024), BF), _sds((1024, T), BF)), grid=(T // tt, _H),
               in_specs=[blk, blk, pl.BlockSpec((1, 128), lambda i, h: (0, 0))],
               out_specs=(blk, pl.BlockSpec((128, tt), lambda i, h: (h, i))),
               sem=("parallel", "parallel"))(o, z, g)


def _gdn_post_bwd(o, z, g, dmixin, T):
    tt = _pick(T, (2048, 1024, 512, 256))

    def body(o_ref, z_ref, g_ref, d_ref, do_ref, dz_ref, dg_ref):
        @pl.when((pl.program_id(0) == 0) & (pl.program_id(1) == 0))
        def _():
            dg_ref[...] = jnp.zeros_like(dg_ref)

        _, vjp = jax.vjp(lambda a, b, c: _rmsnorm(a, c) * _silu(b), o_ref[...], z_ref[...], g_ref[...])
        do, dz, dg = vjp(d_ref[...])
        do_ref[...] = do
        dz_ref[...] = dz.astype(dz_ref.dtype)
        dg_ref[0:1, :] += dg

    blk = pl.BlockSpec((tt, 128), lambda i, h: (i, h))
    return _pc(body, name="gdn_post_bwd", out_shape=(_sds((T, 512)), _sds((T, DCAT_W), BF), _sds((8, 128))),
               grid=(T // tt, _H), in_specs=[blk, blk, pl.BlockSpec((1, 128), lambda i, h: (0, 0)), blk],
               out_specs=(blk, pl.BlockSpec((tt, 128), lambda i, h: (i, DCAT_Z // 128 + h)),
                          pl.BlockSpec((8, 128), lambda i, h: (0, 0))),
               sem=("arbitrary", "arbitrary"))(o, z, g, dmixin)


def _rms_fwd(sm, g, col_blk, name):
    T = sm.shape[0]
    d = g.shape[1]
    tt = _pick(T, (1024, 512, 256))

    def body(x_ref, g_ref, o_ref, ot_ref):
        y = _rmsnorm(x_ref[...], g_ref[...]).astype(o_ref.dtype)
        o_ref[...] = y
        ot_ref[...] = y.T

    return _pc(body, name=name, out_shape=(_sds((T, d), BF), _sds((d, T), BF)), grid=(T // tt,),
               in_specs=[pl.BlockSpec((tt, d), lambda i: (i, col_blk)), pl.BlockSpec((1, d), lambda i: (0, 0))],
               out_specs=(pl.BlockSpec((tt, d), lambda i: (i, 0)), pl.BlockSpec((d, tt), lambda i: (0, i))),
               sem=("parallel",))(sm, g)


def _rms_bwd(sm, g, dy, dsm, col_blk, name):
    T = sm.shape[0]
    d = g.shape[1]
    tt = _pick(T, (1024, 512, 256))

    def body(x_ref, g_ref, d_ref, alias_ref, dx_ref, dg_ref):
        @pl.when(pl.program_id(0) == 0)
        def _():
            dg_ref[...] = jnp.zeros_like(dg_ref)

        _, vjp = jax.vjp(_rmsnorm, x_ref[...], g_ref[...])
        dx, dg = vjp(d_ref[...])
        dx_ref[...] = dx.astype(dx_ref.dtype)
        dg_ref[0:1, :] += dg

    grp = pl.BlockSpec((tt, d), lambda i: (i, col_blk))
    return _pc(body, name=name, out_shape=(_sds(dsm.shape, dsm.dtype), _sds((8, d))), grid=(T // tt,),
               in_specs=[grp, pl.BlockSpec((1, d), lambda i: (0, 0)), pl.BlockSpec((tt, d), lambda i: (i, 0)), ANY],
               out_specs=(grp, pl.BlockSpec((8, d), lambda i: (0, 0))), aliases={3: 0},
               sem=("arbitrary",))(sm, g, dy, dsm)


def _rope_tables(S):
    inv = ROPE_THETA ** (-jnp.arange(0, MLA_ROPE, 2, dtype=F32) / MLA_ROPE)
    ang = jnp.arange(S, dtype=F32)[:, None] * inv[None, :]
    cos, sin = jnp.cos(ang), jnp.sin(ang)
    z = jnp.zeros((S, 64), F32)
    return jnp.concatenate([cos, cos, z], axis=1), jnp.concatenate([-sin, sin, z], axis=1)


def _swap_halves(x):
    lane = lax.broadcasted_iota(jnp.int32, x.shape, 1)
    return jnp.where(lane < 32, pltpu.roll(x, 96, 1), jnp.where(lane < 64, pltpu.roll(x, 32, 1), 0.0))


def _rope_fwd(qraw, kv, sm, cos, sin, S):
    T = qraw.shape[0]
    tt = _pick(S, (2048, 1024, 512, 256))
    nps = S // tt

    def body(q_ref, kn_ref, kr_ref, c_ref, s_ref, qo_ref, ko_ref):
        c, s = c_ref[...], s_ref[...]
        q = q_ref[:, 128:256]
        qo_ref[:, 0:128] = (q_ref[:, 0:128] * ATTN_SCALE).astype(qo_ref.dtype)
        qo_ref[:, 128:256] = ((q * c + _swap_halves(q) * s) * ATTN_SCALE).astype(qo_ref.dtype)
        k = kr_ref[...]
        ko_ref[:, 0:128] = kn_ref[...]
        ko_ref[:, 128:256] = (k * c + _swap_halves(k) * s).astype(ko_ref.dtype)

    tab = pl.BlockSpec((tt, 128), lambda i, h: (i % nps, 0))
    head = pl.BlockSpec((tt, 256), lambda i, h: (i, h))
    return _pc(body, name="rope_fwd", out_shape=(_sds((T, 1024), BF), _sds((T, 1024), BF)), grid=(T // tt, MLA_HEADS),
               in_specs=[head, pl.BlockSpec((tt, 128), lambda i, h: (i, 2 * h)),
                         pl.BlockSpec((tt, 128), lambda i, h: (i, 3)), tab, tab],
               out_specs=(head, head), sem=("parallel", "parallel"))(qraw, kv, sm, cos, sin)


def _rope_bwd(dqc, dcat, dkrr, cos, sin, S):
    T = dqc.shape[0]
    tt = _pick(S, (2048, 1024, 512, 256))
    nps = S // tt

    def body(alias_s, dq_ref, dk_ref, c_ref, s_ref, qo_ref, ko_ref):
        c, s = c_ref[...], s_ref[...]
        d = dq_ref[:, 128:256]
        qo_ref[:, 0:128] = dq_ref[:, 0:128].astype(qo_ref.dtype)
        qo_ref[:, 128:256] = (d * c + _swap_halves(d * s)).astype(qo_ref.dtype)

        @pl.when(pl.program_id(1) == 0)
        def _():
            k = dk_ref[...]
            ko_ref[...] = (k * c + _swap_halves(k * s)).astype(ko_ref.dtype)

    tab = pl.BlockSpec((tt, 128), lambda i, h: (i % nps, 0))
    head = pl.BlockSpec((tt, 256), lambda i, h: (i, h))
    return _pc(body, name="rope_bwd", out_shape=(_sds((T, 1024), BF), _sds(dcat.shape, dcat.dtype)),
               grid=(T // tt, MLA_HEADS),
               in_specs=[ANY, head, pl.BlockSpec((tt, 128), lambda i, h: (i, 0)), tab, tab],
               out_specs=(head, pl.BlockSpec((tt, 128), lambda i, h: (i, 3))),
               aliases={0: 1}, sem=("parallel", "arbitrary"))(dcat, dqc, dkrr, cos, sin)


def _attn_scores(q, k_ref, L, tq):
    def sc(lo, hi):
        return _dot(q, k_ref[lo:hi, :], _NT)

    sd = sc(L - tq, L)
    qc = lax.broadcasted_iota(jnp.int32, sd.shape, 0) >> 6
    kc = lax.broadcasted_iota(jnp.int32, sd.shape, 1) >> 6
    sd = jnp.where(kc <= qc, sd, -jnp.inf)
    return sd if L == tq else jnp.concatenate([sc(0, L - tq), sd], axis=1)


def _attn_fwd(mixin, mixin_t, qcat, kcat, kv, Bl, S):
    T = Bl * S
    tq = _pick(S, (256, 128))
    nq = S // tq

    def body(alias_ref, alias_t, q_ref, k_ref, v_ref, o_ref, lse_ref, ot_ref):
        i = pl.program_id(2)
        q = q_ref[...]
        for ii in range(nq):
            @pl.when(i == ii)
            def _(L=(ii + 1) * tq):
                s = _attn_scores(q, k_ref, L, tq)
                m = jnp.max(s, axis=-1, keepdims=True)
                e = jnp.exp(s - m)
                l = jnp.sum(e, axis=-1, keepdims=True)
                o = (_dot(e, v_ref[0:L, :], _NN) / l).astype(o_ref.dtype)
                o_ref[...] = o
                ot_ref[...] = o.T
                lse_ref[...] = jnp.broadcast_to(m + jnp.log(l), lse_ref.shape)

    qrow = lambda b, h, i: b * nq + i
    in_specs = [ANY, ANY,
                pl.BlockSpec((tq, 256), lambda b, h, i: (qrow(b, h, i), h)),
                pl.BlockSpec((S, 256), lambda b, h, i: (b, h)),
                pl.BlockSpec((S, 128), lambda b, h, i: (b, 2 * h + 1))]
    return _pc(body, name="mla_attn_fwd",
               out_shape=(_sds(mixin.shape, mixin.dtype), _sds((T, 512)), _sds(mixin_t.shape, mixin_t.dtype)),
               grid=(Bl, MLA_HEADS, nq), in_specs=in_specs,
               out_specs=(pl.BlockSpec((tq, 128), lambda b, h, i: (qrow(b, h, i), 4 + h)),
                          pl.BlockSpec((tq, 128), lambda b, h, i: (qrow(b, h, i), h)),
                          pl.BlockSpec((128, tq), lambda b, h, i: (4 + h, qrow(b, h, i)))),
               aliases={0: 0, 1: 2}, sem=("parallel", "parallel", "parallel"))(mixin, mixin_t, qcat, kcat, kv)


def _attn_bwd(qcat, kcat, kv, lse, dmixin, Bl, S):
    T = Bl * S
    tq = _pick(S, (256, 128))
    nq = S // tq

    def body(q_ref, k_ref, v_ref, do_ref, lse_ref, dq_ref, dkv_ref, dkr_ref, acck_ref, accv_ref):
        h, i = pl.program_id(1), pl.program_id(2)
        q = q_ref[...]
        do32 = do_ref[...]
        do = do32.astype(BF)
        lse_col = lse_ref[:, 0:1]

        @pl.when(i == 0)
        def _():
            acck_ref[...] = jnp.zeros_like(acck_ref)
            accv_ref[...] = jnp.zeros_like(accv_ref)

        for ii in range(nq):
            @pl.when(i == ii)
            def _(L=(ii + 1) * tq):
                p = jnp.exp(_attn_scores(q, k_ref, L, tq) - lse_col)
                pb = p.astype(BF)
                v = v_ref[0:L, :]
                delta = jnp.sum(do32 * _dot(pb, v, _NN), axis=-1, keepdims=True)
                ds = (p * (_dot(do, v, _NT) - delta)).astype(BF)
                dq_ref[...] = _dot(ds, k_ref[0:L, :], _NN) * ATTN_SCALE
                acck_ref[0:L, :] += _dot(ds, q, _TN)
                accv_ref[0:L, :] += _dot(pb, do, _TN)

        @pl.when(i == nq - 1)
        def _():
            dkv_ref[:, 0:128] = acck_ref[:, 0:128].astype(dkv_ref.dtype)
            dkv_ref[:, 128:256] = accv_ref[...].astype(dkv_ref.dtype)
            dkr_ref[...] = jnp.where(h == 0, 0.0, dkr_ref[...]) + acck_ref[:, 128:256]

    qrow = lambda b, h, i: b * nq + i
    in_specs = [pl.BlockSpec((tq, 256), lambda b, h, i: (qrow(b, h, i), h)),
                pl.BlockSpec((S, 256), lambda b, h, i: (b, h)),
                pl.BlockSpec((S, 128), lambda b, h, i: (b, 2 * h + 1)),
                pl.BlockSpec((tq, 128), lambda b, h, i: (qrow(b, h, i), 4 + h)),
                pl.BlockSpec((tq, 128), lambda b, h, i: (qrow(b, h, i), h))]
    out_shape = (_sds((T, 1024)), _sds((T, 1024), BF), _sds((T, 128)))
    out_specs = (pl.BlockSpec((tq, 256), lambda b, h, i: (qrow(b, h, i), h)),
                 pl.BlockSpec((S, 256), lambda b, h, i: (b, h)),
                 pl.BlockSpec((S, 128), lambda b, h, i: (b, 0)))
    return _pc(body, name="mla_attn_bwd", out_shape=out_shape, grid=(Bl, MLA_HEADS, nq), in_specs=in_specs,
               out_specs=out_specs, scratch=[pltpu.VMEM((S, 256), F32), pltpu.VMEM((S, 128), F32)],
               sem=("arbitrary", "arbitrary", "arbitrary"))(qcat, kcat, kv, dmixin, lse)


def _ln1_fwd(x, mix, g, b):
    T, D = x.shape
    tt = _pick(T, (512, 256))

    def body(x_ref, m_ref, g_ref, b_ref, o_ref, ob_ref, ot_ref):
        h = _layernorm(ALPHA * x_ref[...] + m_ref[...], g_ref[...], b_ref[...])
        o_ref[...] = h
        hb = h.astype(ob_ref.dtype)
        ob_ref[...] = hb
        ot_ref[...] = hb.T

    blk = pl.BlockSpec((tt, D), lambda i: (i, 0))
    par = pl.BlockSpec((1, D), lambda i: (0, 0))
    return _pc(body, name="ln1_fwd", out_shape=(_sds((T, D)), _sds((T, D), BF), _sds((D, T), BF)), grid=(T // tt,),
               in_specs=[blk, blk, par, par], out_specs=(blk, blk, pl.BlockSpec((D, tt), lambda i: (0, i))),
               sem=("parallel",))(x, mix, g, b)


def _ln1_bwd(x, mix, g, b, dh):
    T, D = x.shape
    tt = _pick(T, (256,))

    def body(x_ref, m_ref, g_ref, b_ref, d_ref, dx_ref, dm_ref, acc_ref):
        @pl.when(pl.program_id(0) == 0)
        def _():
            acc_ref[...] = jnp.zeros_like(acc_ref)

        f = lambda xx, mm, gg, bb: _layernorm(ALPHA * xx + mm, gg, bb)
        _, vjp = jax.vjp(f, x_ref[...], m_ref[...], g_ref[...], b_ref[...])
        dx, dm, dg, db = vjp(d_ref[...])
        dx_ref[...] = dx
        dm_ref[...] = dm.astype(dm_ref.dtype)
        acc_ref[0:1, :] += dg
        acc_ref[1:2, :] += db

    blk = pl.BlockSpec((tt, D), lambda i: (i, 0))
    par = pl.BlockSpec((1, D), lambda i: (0, 0))
    return _pc(body, name="ln1_bwd", out_shape=(_sds((T, D)), _sds((T, D), BF), _sds((8, D))), grid=(T // tt,),
               in_specs=[blk, blk, par, par, blk], out_specs=(blk, blk, pl.BlockSpec((8, D), lambda i: (0, 0))),
               sem=("arbitrary",))(x, mix, g, b, dh)


def _ffn_act_fwd(ug, uu, wg, wu, bg, bu, Bl, S):
    T, C = ug.shape
    cb = _pick(C, (256, 128))
    kk = wg.shape[0]

    def body(g_ref, u_ref, wg_ref, wu_ref, bg_ref, bu_ref, o_ref, ot_ref):
        cg = _conv(g_ref[...], wg_ref[...]) + bg_ref[...]
        cu = _conv(u_ref[...], wu_ref[...]) + bu_ref[...]
        a = (_silu(cg) * cu).astype(o_ref.dtype)
        o_ref[...] = a
        ot_ref[...] = a.T

    blk = pl.BlockSpec((S, cb), lambda b, j: (b, j))
    wblk = pl.BlockSpec((kk, cb), lambda b, j: (0, j))
    bblk = pl.BlockSpec((1, cb), lambda b, j: (0, j))
    return _pc(body, name="ffn_act_fwd", out_shape=(_sds((T, C), BF), _sds((C, T), BF)), grid=(Bl, C // cb),
               in_specs=[blk, blk, wblk, wblk, bblk, bblk],
               out_specs=(blk, pl.BlockSpec((cb, S), lambda b, j: (j, b))),
               sem=("parallel", "parallel"))(ug, uu, wg, wu, bg, bu)


def _ffn_act_bwd(ug, uu, wg, wu, bg, bu, dact, Bl, S):
    T, C = ug.shape
    cb = _pick(C, (256, 128))
    kk = wg.shape[0]

    def body(g_ref, u_ref, wg_ref, wu_ref, bg_ref, bu_ref, d_ref,
             dg_ref, du_ref, dwg_ref, dwu_ref, dbg_ref, dbu_ref):
        first = pl.program_id(1) == 0
        wgv, wuv = wg_ref[...], wu_ref[...]
        tg, tu = _taps(g_ref[...], kk), _taps(u_ref[...], kk)
        cg = _conv(None, wgv, tg) + bg_ref[...]
        cu = _conv(None, wuv, tu) + bu_ref[...]
        _, vjp = jax.vjp(lambda a, b: _silu(a) * b, cg, cu)
        dcg, dcu = vjp(d_ref[...].astype(F32))
        dg_ref[...] = _conv_bwd_x(dcg, wgv).astype(dg_ref.dtype)
        du_ref[...] = _conv_bwd_x(dcu, wuv).astype(du_ref.dtype)
        _conv_bwd_w(tg, dcg, dwg_ref, first)
        _conv_bwd_w(tu, dcu, dwu_ref, first)
        dbg_ref[...] = jnp.where(first, 0.0, dbg_ref[...]) + jnp.sum(dcg, axis=0, keepdims=True)
        dbu_ref[...] = jnp.where(first, 0.0, dbu_ref[...]) + jnp.sum(dcu, axis=0, keepdims=True)

    blk = pl.BlockSpec((S, cb), lambda j, b: (b, j))
    wblk = pl.BlockSpec((kk, cb), lambda j, b: (0, j))
    bblk = pl.BlockSpec((1, cb), lambda j, b: (0, j))
    out_shape = (_sds((T, C), BF), _sds((T, C), BF), _sds((kk, C)), _sds((kk, C)), _sds((1, C)), _sds((1, C)))
    return _pc(body, name="ffn_act_bwd", out_shape=out_shape, grid=(C // cb, Bl),
               in_specs=[blk, blk, wblk, wblk, bblk, bblk, blk],
               out_specs=(blk, blk, wblk, wblk, bblk, bblk), sem=("parallel", "arbitrary"))(
        ug, uu, wg, wu, bg, bu, dact)


def _head(h1, ffn, gpre, pp, tgt, bg, g2, b2):
    T, D = h1.shape
    tt = _pick(T, (256,))

    def body(h_ref, f_ref, gp_ref, pp_ref, t_ref, bg_ref, g2_ref, b2_ref, df_ref, dfb_ref, dgp_ref, dpp_ref, acc_ref):
        @pl.when(pl.program_id(0) == 0)
        def _():
            acc_ref[...] = jnp.zeros_like(acc_ref)

        h, tg = h_ref[...], t_ref[...]

        def loss_fn(f, gp, p_, bgv, g2v, b2v):
            pre = ALPHA * h + f + jax.nn.sigmoid(gp + bgv) * p_
            err = _layernorm(pre, g2v, b2v) - tg
            return 0.5 * jnp.sum(jnp.mean(err * err, axis=-1, keepdims=True))

        loss, grads = jax.value_and_grad(loss_fn, argnums=(0, 1, 2, 3, 4, 5))(
            f_ref[...], gp_ref[...], pp_ref[...], bg_ref[...], g2_ref[...], b2_ref[...])
        df_ref[...] = grads[0]
        dfb_ref[...] = grads[0].astype(dfb_ref.dtype)
        dgp_ref[...] = grads[1].astype(dgp_ref.dtype)
        dpp_ref[...] = grads[2].astype(dpp_ref.dtype)
        acc_ref[0:1, :] += grads[3]
        acc_ref[1:2, :] += grads[4]
        acc_ref[2:3, :] += grads[5]
        acc_ref[3:4, :] += jnp.broadcast_to(loss, (1, D))

    blk = pl.BlockSpec((tt, D), lambda i: (i, 0))
    par = pl.BlockSpec((1, D), lambda i: (0, 0))
    o_bf = _sds((T, D), BF)
    return _pc(body, name="loss_head", out_shape=(_sds((T, D)), o_bf, o_bf, o_bf, _sds((8, D))),
               grid=(T // tt,), in_specs=[blk, blk, blk, blk, blk, par, par, par],
               out_specs=(blk, blk, blk, blk, pl.BlockSpec((8, D), lambda i: (0, 0))), sem=("arbitrary",))(
        h1, ffn, gpre, pp, tgt, bg, g2, b2)


def _adamw(parts, w, m, v, name):
    R, C = w.shape
    tr = R if R <= 512 else _pick(R, (256,))

    def body(p_ref, w_ref, m_ref, v_ref, g_ref, d_ref, nm_ref, nv_ref):
        g = p_ref[0].astype(F32)
        for j in range(1, N_DEV):
            g = g + p_ref[j].astype(F32)
        g_ref[...] = g
        d_ref[...], nm_ref[...], nv_ref[...] = _adam_math(g, w_ref[...], m_ref[...], v_ref[...])

    blk = pl.BlockSpec((tr, C), lambda i: (i, 0))
    o = _sds((R, C))
    return _pc(body, name=name, out_shape=(o, o, o, o), grid=(R // tr,),
               in_specs=[pl.BlockSpec((N_DEV, tr, C), lambda i: (0, i, 0)), blk, blk, blk],
               out_specs=(blk, blk, blk, blk), sem=("parallel",))(parts, w, m, v)


def _adam_math(g, w, m, v):
    mm = ADAM_B1 * m + (1.0 - ADAM_B1) * g
    vv = ADAM_B2 * v + (1.0 - ADAM_B2) * jnp.square(g)
    m_hat = mm / (1.0 - ADAM_B1 ** ADAM_STEP)
    v_hat = vv / (1.0 - ADAM_B2 ** ADAM_STEP)
    return -ADAM_LR * (m_hat / (jnp.sqrt(v_hat) + ADAM_EPS) + ADAM_WD * w), mm, vv


def _small_layout(sizes):
    offs, off = {}, 0
    for n in SMALL_ORDER:
        offs[n] = off
        off += -(-sizes[n] // 128) * 128
    return offs, off


def _pack_small(d):
    return jnp.concatenate([_pad_lanes(d[n].reshape(1, -1), -(-d[n].size // 128) * 128) for n in SMALL_ORDER], axis=1)


def _adamw_small(parts, ws, ms, vs):
    k = len(SMALL_ORDER)
    sizes = {n: ws[n].shape[1] for n in SMALL_ORDER}
    offs, _ = _small_layout(sizes)

    def body(p_ref, *refs):
        ins, outs = refs[:3 * k], refs[3 * k:]
        for i, n in enumerate(SMALL_ORDER):
            lo, hi = offs[n], offs[n] + sizes[n]
            g = p_ref[0, :, lo:hi]
            for j in range(1, N_DEV):
                g = g + p_ref[j, :, lo:hi]
            d, mm, vv = _adam_math(g, ins[i][...], ins[k + i][...], ins[2 * k + i][...])
            outs[4 * i][...], outs[4 * i + 1][...], outs[4 * i + 2][...], outs[4 * i + 3][...] = g, d, mm, vv

    out_shape = tuple(_sds((1, sizes[n])) for n in SMALL_ORDER for _ in range(4))
    args = [ws[n] for n in SMALL_ORDER] + [ms[n] for n in SMALL_ORDER] + [vs[n] for n in SMALL_ORDER]
    res = _pc(body, name="adamw_small", out_shape=out_shape)(parts, *args)
    return {n: tuple(res[4 * i:4 * i + 4]) for i, n in enumerate(SMALL_ORDER)}


def _prep(x, p, xchg=None):
    T, D = x.shape
    Dp = p.shape[1]
    tt = _pick(T, (512, 256))

    def body(x_ref, p_ref, xb_ref, xt_ref, pb_ref, pt_ref):
        xb = x_ref[...].astype(BF)
        xb_ref[...] = xb
        xt_ref[...] = xb.T
        pb = p_ref[...].astype(BF)
        pb_ref[...] = pb
        pt_ref[...] = pb.T

    row = lambda w: pl.BlockSpec((tt, w), lambda i: (i, 0))
    col = lambda w: pl.BlockSpec((w, tt), lambda i: (0, i))
    return _pc(body, name="prep_inputs",
               out_shape=(_sds((T, D), BF), _sds((D, T), BF), _sds((T, Dp), BF), _sds((Dp, T), BF)),
               grid=(T // tt,), in_specs=[row(D), row(Dp)], out_specs=(row(D), col(D), row(Dp), col(Dp)),
               sem=("parallel",), xchg=xchg)(x, p)


SMALL_ORDER = ("gdn_a_log", "gdn_dt_bias", "gdn_norm_g", "mla_q_norm_g", "mla_kv_norm_g", "ln1_g", "ln1_b",
               "ffn_conv_b", "ple_b_gate", "ln2_g", "ln2_b")
EARLY = ("w_in", "gdn_conv_w")
LATE = ("mla_w_q_up", "mla_w_kv_up", "w_out", "ffn_w_up", "ffn_conv_w", "ffn_w_down", "ple_w_gate", "ple_w_proj")
GRADS_EARLY = ("w_out", "ffn_w_up", "ffn_conv_w", "ffn_w_down", "ple_w_gate", "ple_w_proj")
GRADS_LATE = ("w_in", "gdn_conv_w", "mla_w_q_up", "mla_w_kv_up")


def _pad_lanes(v, n=128):
    return jnp.pad(v, ((0, 0), (0, n - v.shape[1])))


def _local_step(x, p, tgt, W, sp, Bl, S, early_weights=None, late_weights=None, early_grads=None, late_grads=None):
    T = Bl * S
    W = dict(W)
    prep = _prep(x, p, xchg=None if early_weights is None else early_weights[:2])
    xb, xT, pb, pT = prep[:4]
    if early_weights is not None:
        W.update(early_weights[2](prep[4:]))
    w_in = W["w_in"]
    wqkv, wz = w_in[:, :1536], w_in[:, 1536:2048]
    z64 = jnp.zeros((w_in.shape[0], 64), w_in.dtype)
    z124 = jnp.zeros((w_in.shape[0], 124), w_in.dtype)
    wsm = jnp.concatenate([w_in[:, 2056:2440], w_in[:, 2696:2760], z64, w_in[:, 2440:2696],
                           w_in[:, 2048:2052], z124, w_in[:, 2052:2056], z124], axis=1)
    alog, dtb = _pad_lanes(sp["gdn_a_log"]), _pad_lanes(sp["gdn_dt_bias"])
    cos, sin = _rope_tables(S)

    qkv = _mm(xb, wqkv, "nn", name="mm_qkv")
    z = _mm(xb, wz, "nn", name="mm_z")
    sm = _mm(xb, wsm, "nn", name="mm_sm")
    qkvn = _gdn_pre_fwd(qkv, W["gdn_conv_w"], Bl, S)
    gdn_out = _gdn_fwd(qkvn, sm, alog, dtb, Bl, S, xchg=None if late_weights is None else late_weights[:2])
    o, tinv_s, sst_s, w_s, u_s, gc_s, beta_s = gdn_out[:7]
    if late_weights is not None:
        W.update(late_weights[2](gdn_out[7:]))
    wq = jnp.pad(W["mla_w_q_up"].reshape(-1, MLA_HEADS, 192), ((0, 0), (0, 0), (0, 64))).reshape(-1, 1024)
    wkv = W["mla_w_kv_up"]
    C = W["ffn_w_down"].shape[0]
    wup_g, wup_u = W["ffn_w_up"][:, :C], W["ffn_w_up"][:, C:]
    cw_g, cw_u = W["ffn_conv_w"][:, :C], W["ffn_conv_w"][:, C:]
    cb_g, cb_u = sp["ffn_conv_b"][:, :C], sp["ffn_conv_b"][:, C:]
    mixin, mixin_t = _gdn_post(o, z, sp["gdn_norm_g"], T)
    cqn, cqn_t = _rms_fwd(sm, sp["mla_q_norm_g"], 0, "rms_q_fwd")
    ckvn, ckvn_t = _rms_fwd(sm, sp["mla_kv_norm_g"], 2, "rms_kv_fwd")
    qraw = _mm(cqn, wq, "nn", name="mm_qup")
    kv = _mm(ckvn, wkv, "nn", name="mm_kvup", out_dtype=BF)
    qcat, kcat = _rope_fwd(qraw, kv, sm, cos, sin, S)
    mixin, lse, mixin_t = _attn_fwd(mixin, mixin_t, qcat, kcat, kv, Bl, S)
    mix = _mm(mixin, W["w_out"], "nn", name="mm_out")
    h1, h1b, h1T = _ln1_fwd(x, mix, sp["ln1_g"], sp["ln1_b"])
    ug = _mm(h1b, wup_g, "nn", name="mm_up_gate")
    uu = _mm(h1b, wup_u, "nn", name="mm_up_up")
    act, act_t = _ffn_act_fwd(ug, uu, cw_g, cw_u, cb_g, cb_u, Bl, S)
    ffn = _mm(act, W["ffn_w_down"], "nn", name="mm_down")
    gpre = _mm(h1b, W["ple_w_gate"], "nn", name="mm_gate")
    pp = _mm(pb, W["ple_w_proj"], "nn", name="mm_proj")

    dffn, dffnb, dgpre, dpp, hacc = _head(h1, ffn, gpre, pp, tgt, sp["ple_b_gate"], sp["ln2_g"], sp["ln2_b"])
    loss = hacc[3, 0]
    gW, gs = {}, {}
    gs["ple_b_gate"], gs["ln2_g"], gs["ln2_b"] = hacc[0:1], hacc[1:2], hacc[2:3]
    gW["ple_w_proj"] = _mm(pT, dpp, "nn", name="mm_dproj", out_dtype=BF)
    gW["ple_w_gate"] = _mm(h1T, dgpre, "nn", name="mm_dgate", out_dtype=BF)
    gW["ffn_w_down"] = _mm(act_t, dffnb, "nn", name="mm_ddown", out_dtype=BF)
    dact = _mm(dffnb, W["ffn_w_down"], "nt", name="mm_dact")
    du_g, du_u, dcw_g, dcw_u, dcb_g, dcb_u = _ffn_act_bwd(ug, uu, cw_g, cw_u, cb_g, cb_u, dact, Bl, S)
    gW["ffn_conv_w"] = jnp.concatenate([dcw_g, dcw_u], axis=1)
    gs["ffn_conv_b"] = jnp.concatenate([dcb_g, dcb_u], axis=1)
    gW["ffn_w_up"] = jnp.concatenate([_mm(h1T, du_g, "nn", name="mm_dup_gate", out_dtype=BF),
                                      _mm(h1T, du_u, "nn", name="mm_dup_up", out_dtype=BF)], axis=1)
    dh1 = _mm(dgpre, W["ple_w_gate"], "nt", name="mm_dh1_gate", add=dffn, add_scale=ALPHA)
    dh1 = _mm(du_g, wup_g, "nt", name="mm_dh1_upg", add=dh1)
    dh1 = _mm(du_u, wup_u, "nt", name="mm_dh1_upu", add=dh1)
    dxa, dmix, acc1 = _ln1_bwd(x, mix, sp["ln1_g"], sp["ln1_b"], dh1)
    gs["ln1_g"], gs["ln1_b"] = acc1[0:1], acc1[1:2]
    gW["w_out"] = _mm(mixin_t, dmix, "nn", name="mm_dwout", out_dtype=BF)
    dmixin = _mm(dmix, W["w_out"], "nt", name="mm_dmixin")
    do, dcat, gacc = _gdn_post_bwd(o, z, sp["gdn_norm_g"], dmixin, T)
    gs["gdn_norm_g"] = gacc[0:1]
    bwd_out = _gdn_bwd(qkvn, sm, alog, dtb, gc_s, beta_s, tinv_s, sst_s, w_s, u_s, do, dcat, Bl, S,
                       xchg=None if early_grads is None else early_grads({n: gW[n] for n in GRADS_EARLY}))
    dqkvn, dcat, cacc = bwd_out[:3]
    early_recv = bwd_out[3:]
    gs["gdn_a_log"], gs["gdn_dt_bias"] = cacc[0:1, :GDN_HEADS], cacc[1:2, :GDN_HEADS]
    dcat, gW["gdn_conv_w"] = _gdn_pre_bwd(qkv, W["gdn_conv_w"], dqkvn, dcat, Bl, S)
    dqc, dkv, dkrr = _attn_bwd(qcat, kcat, kv, lse, dmixin, Bl, S)
    dqraw, dcat = _rope_bwd(dqc, dcat, dkrr, cos, sin, S)
    gwq = _mm(cqn_t, dqraw, "nn", name="mm_dwq", out_dtype=BF)
    gW["mla_w_q_up"] = gwq.reshape(-1, MLA_HEADS, 256)[:, :, :192].reshape(-1, MLA_HEADS * 192)
    gW["mla_w_kv_up"] = _mm(ckvn_t, dkv, "nn", name="mm_dwkv", out_dtype=BF)
    dcqn = _mm(dqraw, wq, "nt", name="mm_dcqn")
    dckvn = _mm(dkv, wkv, "nt", name="mm_dckvn")
    dcat, qacc = _rms_bwd(sm, sp["mla_q_norm_g"], dcqn, dcat, 0, "rms_q_bwd")
    dcat, kacc = _rms_bwd(sm, sp["mla_kv_norm_g"], dckvn, dcat, 2, "rms_kv_bwd")
    gs["mla_q_norm_g"], gs["mla_kv_norm_g"] = qacc[0:1], kacc[0:1]
    gcat = _mm(xT, dcat, "nn", name="mm_dwin", out_dtype=BF)
    gsm, gqkv, gz = gcat[:, :DCAT_QKV], gcat[:, DCAT_QKV:DCAT_Z], gcat[:, DCAT_Z:]
    gW["w_in"] = jnp.concatenate([gqkv, gz, gsm[:, 768:772], gsm[:, 896:900], gsm[:, 0:384], gsm[:, 512:768],
                                  gsm[:, 384:448]], axis=1)
    wcat = jnp.concatenate([wsm, wqkv, wz], axis=1)
    dx = _mm(dcat, wcat, "nt", name="mm_dx", add=dxa, xchg=None if late_grads is None else late_grads(gW, gs))
    late_recv = ()
    if late_grads is not None:
        dx, late_recv = dx[0], dx[1:]
    return loss, dx, gW, gs, early_recv, late_recv


COL_SHARDED = ("w_in", "mla_w_q_up", "mla_w_kv_up", "ffn_w_up", "ple_w_proj", "gdn_conv_w", "ffn_conv_w")
SHARDED = EARLY + LATE
WEIGHTS = ("w_in", "gdn_conv_w", "gdn_a_log", "gdn_dt_bias", "gdn_norm_g", "mla_q_norm_g", "mla_w_q_up",
           "mla_kv_norm_g", "mla_w_kv_up", "w_out", "ln1_g", "ln1_b", "ffn_w_up", "ffn_conv_w", "ffn_conv_b",
           "ffn_w_down", "ple_w_gate", "ple_b_gate", "ple_w_proj", "ln2_g", "ln2_b")
CONV = ("gdn_conv_w", "ffn_conv_w")


def _gathered_to_full(name, g):
    if name in COL_SHARDED:
        return jnp.transpose(g, (1, 0, 2)).reshape(g.shape[1], -1)
    return g.reshape(-1, g.shape[-1])


def _full_to_blocks(name, gfull, shard_shape):
    r, c = shard_shape
    if name in COL_SHARDED:
        return jnp.transpose(gfull.reshape(r, N_DEV, c), (1, 0, 2))
    return gfull.reshape(N_DEV, r, c)


def kernel(x, p, w_in, gdn_conv_w, gdn_a_log, gdn_dt_bias, gdn_norm_g, mla_q_norm_g, mla_w_q_up, mla_kv_norm_g, mla_w_kv_up, w_out, ln1_g, ln1_b, ffn_w_up, ffn_conv_w, ffn_conv_b, ffn_w_down, ple_w_gate, ple_b_gate, ple_w_proj, ln2_g, ln2_b, loss_target, m_w_in, m_gdn_conv_w, m_gdn_a_log, m_gdn_dt_bias, m_gdn_norm_g, m_mla_q_norm_g, m_mla_w_q_up, m_mla_kv_norm_g, m_mla_w_kv_up, m_w_out, m_ln1_g, m_ln1_b, m_ffn_w_up, m_ffn_conv_w, m_ffn_conv_b, m_ffn_w_down, m_ple_w_gate, m_ple_b_gate, m_ple_w_proj, m_ln2_g, m_ln2_b, v_w_in, v_gdn_conv_w, v_gdn_a_log, v_gdn_dt_bias, v_gdn_norm_g, v_mla_q_norm_g, v_mla_w_q_up, v_mla_kv_norm_g, v_mla_w_kv_up, v_w_out, v_ln1_g, v_ln1_b, v_ffn_w_up, v_ffn_conv_w, v_ffn_conv_b, v_ffn_w_down, v_ple_w_gate, v_ple_b_gate, v_ple_w_proj, v_ln2_g, v_ln2_b):
    loc = dict(locals())
    wts = {n: loc[n] for n in WEIGHTS}
    ms = {n: loc["m_" + n] for n in WEIGHTS}
    vs = {n: loc["v_" + n] for n in WEIGHTS}
    Bl, S, D = x.shape
    T = Bl * S

    wire = lambda n: wts[n][0] if n in CONV else wts[n][0].astype(BF)
    gather = lambda names, mode: ([wire(n) for n in names], [mode] * len(names),
                                  lambda res: {n: _gathered_to_full(n, g) for n, g in zip(names, res)})
    blocks = lambda g, names: [_full_to_blocks(n, g[n], wts[n].shape[1:]) for n in names]
    pack_early = lambda g: (blocks(g, GRADS_EARLY), ["a2a"] * len(GRADS_EARLY))
    pack_late = lambda g, gs: (blocks(g, GRADS_LATE) + [_pack_small(gs)], ["a2a"] * len(GRADS_LATE) + ["ag"])
    sp = {n: wts[n].reshape(1, -1) for n in SMALL_ORDER}

    loss, dx, gW, gs, early_recv, late_recv = _local_step(
        x.reshape(T, D), p.reshape(T, -1), loss_target.reshape(T, D), {}, sp, Bl, S,
        early_weights=gather(EARLY, "ag2"), late_weights=gather(LATE, "ag"), early_grads=pack_early,
        late_grads=pack_late)

    res = {}
    for n, parts in list(zip(GRADS_EARLY, early_recv)) + list(zip(GRADS_LATE, late_recv[:-1])):
        res[n] = tuple(t[None] for t in _adamw(parts, wts[n][0], ms[n][0], vs[n][0], "adamw_" + n))
    res.update(_adamw_small(late_recv[-1], wts, ms, vs))

    loss = lax.psum(loss, ("x", "y", "c"))
    outs = [loss, dx.reshape(Bl, S, D)]
    for k in range(4):
        outs += [res[n][k] for n in WEIGHTS]
    return tuple(outs)
```

```python
import math

import jax
import jax.numpy as jnp
from jax import lax
from jax.experimental import pallas as pl
from jax.experimental.pallas import tpu as pltpu

F32 = jnp.float32
BF = jnp.bfloat16
_MXU = jnp.bfloat16
_INTERPRET = None
_VMEM_LIMIT = 56 * 1024 * 1024
HI = lax.Precision.HIGHEST

N_DEV = 8
CHUNK = 64
GDN_HEADS = 4
GDN_DK = 128
MLA_HEADS = 4
MLA_NOPE = 128
MLA_ROPE = 64
ROPE_THETA = 10000.0
ALPHA = 2.0 ** 0.25
NORM_EPS = 1e-6
ATTN_SCALE = (MLA_NOPE + MLA_ROPE) ** -0.5
DCAT_QKV, DCAT_Z, DCAT_W = 1024, 2560, 3072
ADAM_LR, ADAM_B1, ADAM_B2, ADAM_EPS, ADAM_WD, ADAM_STEP = 0.001, 0.9, 0.999, 1e-08, 0.01, 10
MESH = pl.DeviceIdType.MESH
ANY = pl.BlockSpec(memory_space=pl.ANY)
_NN = (((1,), (0,)), ((), ()))
_NT = (((1,), (1,)), ((), ()))
_TN = (((0,), (0,)), ((), ()))


def _sds(shape, dtype=F32):
    return jax.ShapeDtypeStruct(tuple(shape), dtype)


def _pick(n, cands):
    for c in cands:
        if n % c == 0:
            return c
    return n


def _xchg_out_shapes(arrs, modes):
    return [_sds(a.shape if m == "a2a" else (N_DEV,) + a.shape, a.dtype) for a, m in zip(arrs, modes)]


def _xchg_scratch(n):
    return [pltpu.SemaphoreType.DMA((n, N_DEV - 1)), pltpu.SemaphoreType.DMA((n, N_DEV - 1)),
            pltpu.SemaphoreType.DMA((n,))]


def _xchg_plan(ins, outs, send, recv, loc, modes):
    x, y, c = lax.axis_index("x"), lax.axis_index("y"), lax.axis_index("c")
    me = 4 * x + 2 * y + c
    starts, relays, waits = [], [], []
    for ai, mode in enumerate(modes):
        src_all, out = ins[ai], outs[ai]

        def remote(k, src, dst, dev):
            return pltpu.make_async_remote_copy(src_ref=src, dst_ref=dst, send_sem=send.at[ai, k], recv_sem=recv.at[ai, k],
                                                device_id=dev, device_id_type=MESH)

        own = pltpu.make_async_copy(src_all.at[me] if mode == "a2a" else src_all, out.at[me], loc.at[ai])
        starts.append(own)
        waits.append(own.wait)
        if mode == "ag2":
            sib = (x, y, 1 - c)
            chips = [(1 - x, y), (x, 1 - y), (1 - x, 1 - y)]
            first = [remote(0, src_all, out.at[me], sib)]
            first += [remote(1 + k, src_all, out.at[me], (px, py, c)) for k, (px, py) in enumerate(chips)]
            starts += first
            waits += [cp.wait_send for cp in first]
            waits.append(remote(0, src_all, out.at[4 * x + 2 * y + 1 - c], sib).wait_recv)
            for k, (px, py) in enumerate(chips):
                same, other = out.at[4 * px + 2 * py + c], out.at[4 * px + 2 * py + 1 - c]
                relay = remote(4 + k, same, same, sib)
                relays.append((remote(1 + k, src_all, same, (px, py, c)), relay))
                waits += [relay.wait_send, remote(4 + k, src_all, other, sib).wait_recv]
            continue
        for r in range(1, N_DEV):
            px = 1 - x if (r >> 2) & 1 else x
            py = 1 - y if (r >> 1) & 1 else y
            pc = 1 - c if r & 1 else c
            cp = remote(r - 1, src_all.at[4 * px + 2 * py + pc] if mode == "a2a" else src_all, out.at[me], (px, py, pc))
            starts.append(cp)
            waits.append(cp.wait)
    return starts, relays, waits


def _pc(body, *, name, out_shape, grid=None, in_specs=None, out_specs=None, scratch=(), sem=None, aliases=None,
        xchg=None):
    kw = {}
    if _INTERPRET is not None:
        kw["interpret"] = _INTERPRET
    single = not isinstance(out_shape, (tuple, list))
    out_shape = [out_shape] if single else list(out_shape)
    if out_specs is not None:
        out_specs = [out_specs] if single else list(out_specs)
    scratch = list(scratch)
    if xchg is not None:
        xarrs, xmodes = xchg
        n_in, n_out, n_scr, nx = len(in_specs), len(out_shape), len(scratch), len(xarrs)
        single = False
        inner = body

        def body(*refs):
            ins, xins = refs[:n_in], refs[n_in:n_in + nx]
            outs = refs[n_in + nx:n_in + nx + n_out]
            xouts = refs[n_in + nx + n_out:n_in + 2 * nx + n_out]
            scr = refs[n_in + 2 * nx + n_out:n_in + 2 * nx + n_out + n_scr]
            send, recv, loc = refs[n_in + 2 * nx + n_out + n_scr:]
            first = last = None
            for d, g in enumerate(grid):
                f, l = pl.program_id(d) == 0, pl.program_id(d) == g - 1
                first, last = (f, l) if first is None else (first & f, last & l)

            @pl.when(first)
            def _():
                for cp in _xchg_plan(xins, xouts, send, recv, loc, xmodes)[0]:
                    cp.start()

            inner(*ins, *outs, *scr)

            @pl.when(last)
            def _():
                _, relays, waits = _xchg_plan(xins, xouts, send, recv, loc, xmodes)
                for arrival, relay in relays:
                    arrival.wait_recv()
                    relay.start()
                for wait in waits:
                    wait()

        in_specs = list(in_specs) + [ANY] * nx
        out_specs = out_specs + [ANY] * nx
        out_shape = out_shape + _xchg_out_shapes(xarrs, xmodes)
        scratch = scratch + _xchg_scratch(nx)
        sem = ("arbitrary",) * len(grid)
    if grid is not None:
        kw["grid"] = grid
    if in_specs is not None:
        kw["in_specs"] = in_specs
    if out_specs is not None:
        kw["out_specs"] = out_specs[0] if single else tuple(out_specs)
    cp = dict(vmem_limit_bytes=_VMEM_LIMIT)
    if sem is not None:
        cp["dimension_semantics"] = sem
    call = pl.pallas_call(body, name=name, out_shape=out_shape[0] if single else tuple(out_shape),
                          scratch_shapes=scratch, input_output_aliases=aliases or {},
                          compiler_params=pltpu.CompilerParams(**cp), **kw)
    if xchg is None:
        return call
    return lambda *ins: call(*ins, *xchg[0])


_FULL_K_MAX = 3072


def _mm(a, b, mode, *, name, add=None, add_scale=1.0, out_dtype=F32, xchg=None):
    (M, K) = a.shape
    (K2, N) = b.shape if mode == "nn" else b.shape[::-1]
    assert K == K2, (a.shape, b.shape, mode)
    tm = _pick(M, (1024, 1408, 512, 384, 256, 128))
    tn = _pick(N, (1024, 2816, 1408, 768, 512, 384, 256, 128))
    tk = K if K <= _FULL_K_MAX else _pick(K, (2048, 1408, 1024, 512) if tn <= 1408 else (1024, 512))
    nk = K // tk
    dims = _NN if mode == "nn" else _NT
    has_add = add is not None

    def finish(r, add_ref, o_ref):
        if has_add:
            r = r + add_scale * add_ref[...].astype(F32)
        o_ref[...] = r.astype(out_dtype)

    if nk == 1:
        def body(a_ref, b_ref, *rest):
            r = lax.dot_general(a_ref[...].astype(_MXU), b_ref[...].astype(_MXU), dims, preferred_element_type=F32)
            finish(r, rest[0] if has_add else None, rest[-1])

        a_spec = pl.BlockSpec((tm, K), lambda i, j: (i, 0))
        b_spec = (pl.BlockSpec((K, tn), lambda i, j: (0, j)) if mode == "nn"
                  else pl.BlockSpec((tn, K), lambda i, j: (j, 0)))
        o_spec = pl.BlockSpec((tm, tn), lambda i, j: (i, j))
        grid, sem, scratch = (M // tm, N // tn), ("parallel", "parallel"), []
    else:
        def body(a_ref, b_ref, *rest):
            o_ref, acc = rest[-2], rest[-1]
            k = pl.program_id(2)

            @pl.when(k == 0)
            def _():
                acc[...] = jnp.zeros_like(acc)

            acc[...] += lax.dot_general(a_ref[...].astype(_MXU), b_ref[...].astype(_MXU), dims,
                                        preferred_element_type=F32)

            @pl.when(k == nk - 1)
            def _():
                finish(acc[...], rest[0] if has_add else None, o_ref)

        a_spec = pl.BlockSpec((tm, tk), lambda i, j, k: (i, k))
        b_spec = (pl.BlockSpec((tk, tn), lambda i, j, k: (k, j)) if mode == "nn"
                  else pl.BlockSpec((tn, tk), lambda i, j, k: (j, k)))
        o_spec = pl.BlockSpec((tm, tn), lambda i, j, k: (i, j))
        grid, sem, scratch = (M // tm, N // tn, nk), ("parallel", "parallel", "arbitrary"), [pltpu.VMEM((tm, tn), F32)]

    ins = [a, b] + ([add] if has_add else [])
    specs = [a_spec, b_spec] + ([o_spec] if has_add else [])
    return _pc(body, name=name, out_shape=_sds((M, N), out_dtype), grid=grid, in_specs=specs, out_specs=o_spec,
               scratch=scratch, sem=sem, xchg=xchg)(*ins)


def _dot(a, b, dims):
    return lax.dot_general(a.astype(_MXU), b.astype(_MXU), dims, preferred_element_type=F32)


def _split(a):
    hi = a.astype(BF)
    lo = (a - hi.astype(F32)).astype(BF)
    return hi, lo


def _dot3(a, b, dims):
    ah, al = _split(a)
    bh, bl = _split(b)
    d = lambda u, v: lax.dot_general(u, v, dims, preferred_element_type=F32)
    return d(ah, bh) + (d(ah, bl) + d(al, bh))


def _softplus(x):
    return jnp.maximum(x, 0.0) + jnp.log1p(jnp.exp(-jnp.abs(x)))


def _silu(x):
    return x * jax.nn.sigmoid(x)


def _rmsnorm(x, g):
    return x * lax.rsqrt(jnp.mean(x * x, axis=-1, keepdims=True) + NORM_EPS) * g


def _layernorm(x, g, b):
    mu = jnp.mean(x, axis=-1, keepdims=True)
    xc = x - mu
    var = jnp.mean(xc * xc, axis=-1, keepdims=True)
    return xc * lax.rsqrt(var + NORM_EPS) * g + b


def _shift_dn(x, s):
    if s == 0:
        return x
    t = lax.broadcasted_iota(jnp.int32, x.shape, 0)
    return jnp.where(t >= s, pltpu.roll(x, s, 0), 0.0)


def _shift_up(x, s):
    if s == 0:
        return x
    n = x.shape[0]
    t = lax.broadcasted_iota(jnp.int32, x.shape, 0)
    return jnp.where(t < n - s, pltpu.roll(x, n - s, 0), 0.0)


def _taps(x, kk):
    return [_shift_dn(x, kk - 1 - j) for j in range(kk)]


def _conv(x, w, taps=None):
    kk = w.shape[0]
    taps = _taps(x, kk) if taps is None else taps
    y = w[kk - 1:kk, :] * taps[kk - 1]
    for j in range(kk - 1):
        y = y + w[j:j + 1, :] * taps[j]
    return y


def _conv_bwd_x(dy, w):
    kk = w.shape[0]
    dx = w[kk - 1:kk, :] * dy
    for j in range(kk - 1):
        dx = dx + w[j:j + 1, :] * _shift_up(dy, kk - 1 - j)
    return dx


def _conv_bwd_w(taps, dy, dw_ref, first):
    kk = dw_ref.shape[0]
    for j in range(kk):
        r = jnp.sum(dy * taps[j], axis=0, keepdims=True)
        prev = jnp.where(first, 0.0, dw_ref[j:j + 1, :])
        dw_ref[j:j + 1, :] = prev + r


def _gdn_post_conv(c, j):
    h = _silu(c)
    hn = h * lax.rsqrt(jnp.sum(h * h, axis=-1, keepdims=True) + NORM_EPS)
    return jnp.where(j < GDN_HEADS, hn * (GDN_DK ** -0.5), jnp.where(j < 2 * GDN_HEADS, hn, h))


def _gdn_pre_fwd(qkv, conv_w, Bl, S):
    T, C = qkv.shape
    nj = C // 128

    def body(x_ref, w_ref, o_ref):
        o_ref[...] = _gdn_post_conv(_conv(x_ref[...], w_ref[...]), pl.program_id(1))

    blk = pl.BlockSpec((S, 128), lambda b, j: (b, j))
    return _pc(body, name="gdn_pre_fwd", out_shape=_sds((T, C)), grid=(Bl, nj),
               in_specs=[blk, pl.BlockSpec((conv_w.shape[0], 128), lambda b, j: (0, j))], out_specs=blk,
               sem=("parallel", "parallel"))(qkv, conv_w)


def _gdn_pre_bwd(qkv, conv_w, dout, dcat, Bl, S):
    T, C = qkv.shape
    nj = C // 128
    kk = conv_w.shape[0]

    def body(x_ref, w_ref, d_ref, alias_ref, dx_ref, dw_ref):
        j, b = pl.program_id(0), pl.program_id(1)
        w = w_ref[...]
        taps = _taps(x_ref[...], kk)
        c = _conv(None, w, taps)
        _, vjp = jax.vjp(lambda u: _gdn_post_conv(u, j), c)
        (dc,) = vjp(d_ref[...])
        dx_ref[...] = _conv_bwd_x(dc, w).astype(dx_ref.dtype)
        _conv_bwd_w(taps, dc, dw_ref, b == 0)

    blk = pl.BlockSpec((S, 128), lambda j, b: (b, j))
    wblk = pl.BlockSpec((kk, 128), lambda j, b: (0, j))
    return _pc(body, name="gdn_pre_bwd", out_shape=(_sds(dcat.shape, dcat.dtype), _sds((kk, C))), grid=(nj, Bl),
               in_specs=[blk, wblk, blk, ANY],
               out_specs=(pl.BlockSpec((S, 128), lambda j, b: (b, DCAT_QKV // 128 + j)), wblk),
               aliases={3: 0}, sem=("parallel", "arbitrary"))(qkv, conv_w, dout, dcat)


_H = GDN_HEADS
_HC = _H * CHUNK


def _st(x):
    return jnp.concatenate([x[:, h * 128:(h + 1) * 128] for h in range(_H)], axis=0)


def _unst(x):
    return jnp.concatenate([x[h * CHUNK:(h + 1) * CHUNK] for h in range(_H)], axis=1)


def _stc(t):
    return jnp.concatenate([t[:, h:h + 1] for h in range(_H)], axis=0)


def _untile(col):
    lane = lax.broadcasted_iota(jnp.int32, (CHUNK, 128), 1)
    out = jnp.zeros((CHUNK, 128), F32)
    for h in range(_H):
        out = out + jnp.where(lane == h, col[h * CHUNK:(h + 1) * CHUNK], 0.0)
    return out


def _rowform(col):
    return jnp.broadcast_to(col, (_HC, 128)).T[0:1, :]


def _tri(n, upper=False):
    i = lax.broadcasted_iota(jnp.int32, (n, n), 0)
    j = lax.broadcasted_iota(jnp.int32, (n, n), 1)
    return jnp.where((j >= i) if upper else (j <= i), 1.0, 0.0).astype(F32)


def _gdn_chunk_common(q, k, v, gc, beta):
    Q, K, V = _st(q), _st(k), _st(v)
    B, GC = _stc(beta), _stc(gc)
    GL = jnp.concatenate([jnp.broadcast_to(gc[CHUNK - 1:CHUNK, h:h + 1], (CHUNK, 1)) for h in range(_H)], axis=0)
    ii = lax.broadcasted_iota(jnp.int32, (_HC, _HC), 0)
    jj = lax.broadcasted_iota(jnp.int32, (_HC, _HC), 1)
    same = (ii >> 6) == (jj >> 6)
    incl = same & (ii >= jj)
    strict = same & (ii > jj)
    diff = GC - _rowform(GC)
    D = jnp.where(incl, jnp.exp(jnp.where(incl, diff, 0.0)), 0.0)
    KB = K * B
    A = jnp.where(strict, _dot(KB, K, _NT) * D, 0.0)
    P = jnp.where(incl, _dot(Q, K, _NT) * D, 0.0)
    EG = jnp.exp(GC)
    ED = jnp.exp(GL - GC)
    return dict(Q=Q, K=K, V=V, B=B, GC=GC, GL=GL, incl=incl, strict=strict, D=D, KB=KB, A=A, P=P, EG=EG, ED=ED,
                QG=Q * EG, KD=K * ED)


def _neumann_inv(A):
    n = A.shape[0]
    i = lax.broadcasted_iota(jnp.int32, (n, n), 0)
    j = lax.broadcasted_iota(jnp.int32, (n, n), 1)
    N = -A
    acc = jnp.where(i == j, 1.0, 0.0) + N
    Pw = N
    for _ in range(5):
        Pw = _dot3(Pw, Pw, _NN)
        acc = acc + _dot3(acc, Pw, _NN)
    return acc


def _gates(a, alog, dtb):
    return -jnp.exp(alog) * _softplus(a + dtb)


def _hs(x, h, n=CHUNK):
    return x[h * n:(h + 1) * n]


def _per_step(Bl):
    return 4 if Bl % 4 == 0 else 2 if Bl % 2 == 0 else 1


def _gdn_fwd(qkvn, sm, alog, dtb, Bl, S, xchg=None):
    T = Bl * S
    nC = S // CHUNK
    E = _per_step(Bl)

    def body(*refs):
        S_ref = refs[-1]

        @pl.when(pl.program_id(1) == 0)
        def _():
            S_ref[...] = jnp.zeros_like(S_ref)

        for e in range(E):
            one(*[r if i in (5, 6) else r.at[e] for i, r in enumerate(refs)])

    def one(q_ref, k_ref, v_ref, a_ref, b_ref, alog_ref, dtb_ref,
            o_ref, tinv_ref, sst_ref, w_ref, u_ref, gc_ref, beta_ref, S_ref):
        g = _gates(a_ref[...], alog_ref[...], dtb_ref[...])
        beta = jax.nn.sigmoid(b_ref[...])
        gc = jnp.dot(_tri(CHUNK), g, precision=HI, preferred_element_type=F32)
        cm = _gdn_chunk_common(q_ref[...], k_ref[...], v_ref[...], gc, beta)
        tinv = _neumann_inv(cm["A"])
        WU = _dot3(tinv, jnp.concatenate([cm["KB"] * cm["EG"], cm["V"] * cm["B"]], axis=1), _NN)
        W, U = WU[:, :128], WU[:, 128:]
        s_old = [S_ref[h * 128:(h + 1) * 128, :] for h in range(_H)]
        vn = [_hs(U, h) - _dot(_hs(W, h), s_old[h], _NN) for h in range(_H)]
        o_intra = _dot(cm["P"], jnp.concatenate(vn, axis=0), _NN)
        outs = []
        for h in range(_H):
            outs.append(_dot(_hs(cm["QG"], h), s_old[h], _NN) + _hs(o_intra, h))
            gl = jnp.exp(gc[CHUNK - 1:CHUNK, h:h + 1])
            S_ref[h * 128:(h + 1) * 128, :] = gl * s_old[h] + _dot(_hs(cm["KD"], h), vn[h], _TN)
            sst_ref[h * 128:(h + 1) * 128, :] = s_old[h]
        o_ref[...] = jnp.concatenate(outs, axis=1)
        tinv_ref[...] = tinv
        w_ref[...] = _unst(W)
        u_ref[...] = _unst(U)
        gc_ref[...] = gc
        beta_ref[...] = beta

    blk = lambda r, w, c: pl.BlockSpec((E, r, w), lambda b, n: (b, n, c))
    par = pl.BlockSpec((1, 128), lambda b, n: (0, 0))
    out_shape = (_sds((Bl, S, 512)), _sds((Bl, nC * _HC, _HC)), _sds((Bl, nC * 512, 128)),
                 _sds((Bl, S, 512)), _sds((Bl, S, 512)), _sds((Bl, S, 128)), _sds((Bl, S, 128)))
    out_specs = (blk(CHUNK, 512, 0), blk(_HC, _HC, 0), blk(512, 128, 0), blk(CHUNK, 512, 0), blk(CHUNK, 512, 0),
                 blk(CHUNK, 128, 0), blk(CHUNK, 128, 0))
    q3, s3 = qkvn.reshape(Bl, S, -1), sm.reshape(Bl, S, -1)
    res = _pc(body, name="gdn_chunk_fwd", out_shape=out_shape, grid=(Bl // E, nC),
              in_specs=[blk(CHUNK, 512, 0), blk(CHUNK, 512, 1), blk(CHUNK, 512, 2), blk(CHUNK, 128, 6),
                        blk(CHUNK, 128, 7), par, par],
              out_specs=out_specs, scratch=[pltpu.VMEM((E, 512, 128), F32)], sem=("parallel", "arbitrary"),
              xchg=xchg)(q3, q3, q3, s3, s3, alog, dtb)
    return (res[0].reshape(T, 512),) + tuple(res[1:])


def _gdn_bwd(qkvn, sm, alog, dtb, gc_s, beta_s, tinv_s, sst_s, w_s, u_s, do, dcat, Bl, S, xchg=None):
    T = Bl * S
    nC = S // CHUNK
    E = _per_step(Bl)

    def body(*refs):
        acc_ref, dS_ref = refs[-2], refs[-1]
        first_chunk = pl.program_id(1) == 0

        @pl.when(first_chunk)
        def _():
            dS_ref[...] = jnp.zeros_like(dS_ref)

        @pl.when(first_chunk & (pl.program_id(0) == 0))
        def _():
            acc_ref[...] = jnp.zeros_like(acc_ref)

        for e in range(E):
            one(*[r if i in (4, 5, 13, 16) else r.at[e] for i, r in enumerate(refs)])

    def one(q_ref, k_ref, v_ref, a_ref, alog_ref, dtb_ref, gc_ref, beta_ref, tinv_ref, sst_ref, w_ref, u_ref,
            do_ref, alias_ref, dqkv_ref, dsm_ref, acc_ref, dS_ref):
        gc, beta = gc_ref[...], beta_ref[...]
        cm = _gdn_chunk_common(q_ref[...], k_ref[...], v_ref[...], gc, beta)
        Q, K, V, B, D, KB, A, P = (cm[n] for n in ("Q", "K", "V", "B", "D", "KB", "A", "P"))
        EG, ED, QG, KD = cm["EG"], cm["ED"], cm["QG"], cm["KD"]
        tinv = tinv_ref[...]
        W, U, DO = _st(w_ref[...]), _st(u_ref[...]), _st(do_ref[...])
        s_old = [sst_ref[h * 128:(h + 1) * 128, :] for h in range(_H)]
        ds_new = [dS_ref[h * 128:(h + 1) * 128, :] for h in range(_H)]
        VN = jnp.concatenate([_hs(U, h) - _dot(_hs(W, h), s_old[h], _NN) for h in range(_H)], axis=0)
        dP = jnp.where(cm["incl"], _dot(DO, VN, _NT), 0.0)
        dVN0 = _dot(P, DO, _TN)
        dVN, dQG, dKD, dW, TL = [], [], [], [], []
        for h in range(_H):
            gl = jnp.exp(gc[CHUNK - 1:CHUNK, h:h + 1])
            dvn = _hs(dVN0, h) + _dot(_hs(KD, h), ds_new[h], _NN)
            dkd = _dot(_hs(VN, h), ds_new[h], _NT)
            dVN.append(dvn)
            dQG.append(_dot(_hs(DO, h), s_old[h], _NT))
            dKD.append(dkd)
            dW.append(-_dot(dvn, s_old[h], _NT))
            dgl = jnp.sum(jnp.sum(ds_new[h] * s_old[h], axis=1, keepdims=True), axis=0, keepdims=True)
            tl = jnp.sum(jnp.sum(dkd * _hs(KD, h), axis=1, keepdims=True), axis=0, keepdims=True) + dgl * gl
            TL.append(jnp.broadcast_to(tl, (CHUNK, 1)))
            dS_ref[h * 128:(h + 1) * 128, :] = (_dot(_hs(QG, h), _hs(DO, h), _TN) + gl * ds_new[h]
                                                - _dot(_hs(W, h), dvn, _TN))
        dVN, dQG, dKD, dW, TL = (jnp.concatenate(z, axis=0) for z in (dVN, dQG, dKD, dW, TL))
        dB = _dot3(tinv, jnp.concatenate([dVN, dW], axis=1), _TN)
        dVB, dKBE = dB[:, :128], dB[:, 128:]
        dA = jnp.where(cm["strict"], -_dot3(dB, jnp.concatenate([U, W], axis=1), _NT), 0.0)
        dG = dA * D
        dQK = dP * D
        dKB = _dot(dG, K, _NN) + dKBE * EG
        dK = (_dot(jnp.concatenate([dG, dQK], axis=0), jnp.concatenate([KB, Q], axis=0), _TN)
              + dKB * B + dKD * ED)
        dQ = _dot(dQK, K, _NN) + dQG * EG
        Mx = dA * A + dP * P
        rs = lambda z: jnp.sum(z, axis=1, keepdims=True)
        ri = lax.broadcasted_iota(jnp.int32, (_HC, 1), 0)
        dGC = (rs(Mx) - rs(Mx.T) + rs(dKBE * KB * EG) + rs(dQG * QG) - rs(dKD * KD)
               + jnp.where((ri & (CHUNK - 1)) == CHUNK - 1, TL, 0.0))
        dBeta = rs(dVB * V) + rs(dKB * K)
        dqkv_ref[:, 0:512] = _unst(dQ)
        dqkv_ref[:, 512:1024] = _unst(dK)
        dqkv_ref[:, 1024:1536] = _unst(dVB * B)
        dg = jnp.dot(_tri(CHUNK, upper=True), _untile(dGC), precision=HI, preferred_element_type=F32)
        a, alog_v, dtb_v = a_ref[...], alog_ref[...], dtb_ref[...]
        g = _gates(a, alog_v, dtb_v)
        lane = lax.broadcasted_iota(jnp.int32, (CHUNK, 128), 1)
        valid = lane < _H
        da = jnp.where(valid, dg * (-jnp.exp(alog_v)) * jax.nn.sigmoid(a + dtb_v), 0.0)
        dsm_ref[:, 0:128] = da.astype(dsm_ref.dtype)
        dsm_ref[:, 128:256] = jnp.where(valid, _untile(dBeta) * beta * (1.0 - beta), 0.0).astype(dsm_ref.dtype)
        acc_ref[0:1, :] += jnp.sum(jnp.where(valid, dg * g, 0.0), axis=0, keepdims=True)
        acc_ref[1:2, :] += jnp.sum(da, axis=0, keepdims=True)

    blk = lambda r, w, c: pl.BlockSpec((E, r, w), lambda b, n: (b, nC - 1 - n, c))
    par = pl.BlockSpec((1, 128), lambda b, n: (0, 0))
    in_specs = [blk(CHUNK, 512, 0), blk(CHUNK, 512, 1), blk(CHUNK, 512, 2), blk(CHUNK, 128, 6), par, par,
                blk(CHUNK, 128, 0), blk(CHUNK, 128, 0), blk(_HC, _HC, 0), blk(512, 128, 0),
                blk(CHUNK, 512, 0), blk(CHUNK, 512, 0), blk(CHUNK, 512, 0), ANY]
    out_shape = (_sds((Bl, S, 1536)), _sds((Bl, S, DCAT_W), BF), _sds((8, 128)))
    out_specs = (blk(CHUNK, 1536, 0), blk(CHUNK, 256, 3), pl.BlockSpec((8, 128), lambda b, n: (0, 0)))
    q3, s3 = qkvn.reshape(Bl, S, -1), sm.reshape(Bl, S, -1)
    res = _pc(body, name="gdn_chunk_bwd", out_shape=out_shape, grid=(Bl // E, nC), in_specs=in_specs,
              out_specs=out_specs, scratch=[pltpu.VMEM((E, 512, 128), F32)], sem=("arbitrary", "arbitrary"),
              aliases={13: 1}, xchg=xchg)(q3, q3, q3, s3, alog, dtb, gc_s, beta_s, tinv_s, sst_s, w_s, u_s,
                                          do.reshape(Bl, S, -1), dcat.reshape(Bl, S, DCAT_W))
    return (res[0].reshape(T, 1536), res[1].reshape(T, DCAT_W)) + tuple(res[2:])


def _gdn_post(o, z, g, T):
    tt = _pick(T, (2048, 1024, 512, 256))

    def body(o_ref, z_ref, g_ref, out_ref, outt_ref):
        y = (_rmsnorm(o_ref[...], g_ref[...]) * _silu(z_ref[...])).astype(out_ref.dtype)
        out_ref[...] = y
        outt_ref[...] = y.T

    blk = pl.BlockSpec((tt, 128), lambda i, h: (i, h))
    return _pc(body, name="gdn_post_fwd", out_shape=(_sds((T, 1024), BF), _sds((1024, T), BF)), grid=(T // tt, _H),
               in_specs=[blk, blk, pl.BlockSpec((1, 128), lambda i, h: (0, 0))],
               out_specs=(blk, pl.BlockSpec((128, tt), lambda i, h: (h, i))),
               sem=("parallel", "parallel"))(o, z, g)


def _gdn_post_bwd(o, z, g, dmixin, T):
    tt = _pick(T, (2048, 1024, 512, 256))

    def body(o_ref, z_ref, g_ref, d_ref, do_ref, dz_ref, dg_ref):
        @pl.when((pl.program_id(0) == 0) & (pl.program_id(1) == 0))
        def _():
            dg_ref[...] = jnp.zeros_like(dg_ref)

        _, vjp = jax.vjp(lambda a, b, c: _rmsnorm(a, c) * _silu(b), o_ref[...], z_ref[...], g_ref[...])
        do, dz, dg = vjp(d_ref[...])
        do_ref[...] = do
        dz_ref[...] = dz.astype(dz_ref.dtype)
        dg_ref[0:1, :] += dg

    blk = pl.BlockSpec((tt, 128), lambda i, h: (i, h))
    return _pc(body, name="gdn_post_bwd", out_shape=(_sds((T, 512)), _sds((T, DCAT_W), BF), _sds((8, 128))),
               grid=(T // tt, _H), in_specs=[blk, blk, pl.BlockSpec((1, 128), lambda i, h: (0, 0)), blk],
               out_specs=(blk, pl.BlockSpec((tt, 128), lambda i, h: (i, DCAT_Z // 128 + h)),
                          pl.BlockSpec((8, 128), lambda i, h: (0, 0))),
               sem=("arbitrary", "arbitrary"))(o, z, g, dmixin)


def _rms_fwd(sm, g, col_blk, name):
    T = sm.shape[0]
    d = g.shape[1]
    tt = _pick(T, (1024, 512, 256))

    def body(x_ref, g_ref, o_ref, ot_ref):
        y = _rmsnorm(x_ref[...], g_ref[...]).astype(o_ref.dtype)
        o_ref[...] = y
        ot_ref[...] = y.T

    return _pc(body, name=name, out_shape=(_sds((T, d), BF), _sds((d, T), BF)), grid=(T // tt,),
               in_specs=[pl.BlockSpec((tt, d), lambda i: (i, col_blk)), pl.BlockSpec((1, d), lambda i: (0, 0))],
               out_specs=(pl.BlockSpec((tt, d), lambda i: (i, 0)), pl.BlockSpec((d, tt), lambda i: (0, i))),
               sem=("parallel",))(sm, g)


def _rms_bwd(sm, g, dy, dsm, col_blk, name):
    T = sm.shape[0]
    d = g.shape[1]
    tt = _pick(T, (1024, 512, 256))

    def body(x_ref, g_ref, d_ref, alias_ref, dx_ref, dg_ref):
        @pl.when(pl.program_id(0) == 0)
        def _():
            dg_ref[...] = jnp.zeros_like(dg_ref)

        _, vjp = jax.vjp(_rmsnorm, x_ref[...], g_ref[...])
        dx, dg = vjp(d_ref[...])
        dx_ref[...] = dx.astype(dx_ref.dtype)
        dg_ref[0:1, :] += dg

    grp = pl.BlockSpec((tt, d), lambda i: (i, col_blk))
    return _pc(body, name=name, out_shape=(_sds(dsm.shape, dsm.dtype), _sds((8, d))), grid=(T // tt,),
               in_specs=[grp, pl.BlockSpec((1, d), lambda i: (0, 0)), pl.BlockSpec((tt, d), lambda i: (i, 0)), ANY],
               out_specs=(grp, pl.BlockSpec((8, d), lambda i: (0, 0))), aliases={3: 0},
               sem=("arbitrary",))(sm, g, dy, dsm)


def _rope_tables(S):
    inv = ROPE_THETA ** (-jnp.arange(0, MLA_ROPE, 2, dtype=F32) / MLA_ROPE)
    ang = jnp.arange(S, dtype=F32)[:, None] * inv[None, :]
    cos, sin = jnp.cos(ang), jnp.sin(ang)
    z = jnp.zeros((S, 64), F32)
    return jnp.concatenate([cos, cos, z], axis=1), jnp.concatenate([-sin, sin, z], axis=1)


def _swap_halves(x):
    lane = lax.broadcasted_iota(jnp.int32, x.shape, 1)
    return jnp.where(lane < 32, pltpu.roll(x, 96, 1), jnp.where(lane < 64, pltpu.roll(x, 32, 1), 0.0))


def _rope_fwd(qraw, kv, sm, cos, sin, S):
    T = qraw.shape[0]
    tt = _pick(S, (2048, 1024, 512, 256))
    nps = S // tt

    def body(q_ref, kn_ref, kr_ref, c_ref, s_ref, qo_ref, ko_ref):
        c, s = c_ref[...], s_ref[...]
        q = q_ref[:, 128:256]
        qo_ref[:, 0:128] = (q_ref[:, 0:128] * ATTN_SCALE).astype(qo_ref.dtype)
        qo_ref[:, 128:256] = ((q * c + _swap_halves(q) * s) * ATTN_SCALE).astype(qo_ref.dtype)
        k = kr_ref[...]
        ko_ref[:, 0:128] = kn_ref[...]
        ko_ref[:, 128:256] = (k * c + _swap_halves(k) * s).astype(ko_ref.dtype)

    tab = pl.BlockSpec((tt, 128), lambda i, h: (i % nps, 0))
    head = pl.BlockSpec((tt, 256), lambda i, h: (i, h))
    return _pc(body, name="rope_fwd", out_shape=(_sds((T, 1024), BF), _sds((T, 1024), BF)), grid=(T // tt, MLA_HEADS),
               in_specs=[head, pl.BlockSpec((tt, 128), lambda i, h: (i, 2 * h)),
                         pl.BlockSpec((tt, 128), lambda i, h: (i, 3)), tab, tab],
               out_specs=(head, head), sem=("parallel", "parallel"))(qraw, kv, sm, cos, sin)


def _rope_bwd(dqc, dcat, dkrr, cos, sin, S):
    T = dqc.shape[0]
    tt = _pick(S, (2048, 1024, 512, 256))
    nps = S // tt

    def body(alias_s, dq_ref, dk_ref, c_ref, s_ref, qo_ref, ko_ref):
        c, s = c_ref[...], s_ref[...]
        d = dq_ref[:, 128:256]
        qo_ref[:, 0:128] = dq_ref[:, 0:128].astype(qo_ref.dtype)
        qo_ref[:, 128:256] = (d * c + _swap_halves(d * s)).astype(qo_ref.dtype)

        @pl.when(pl.program_id(1) == 0)
        def _():
            k = dk_ref[...]
            ko_ref[...] = (k * c + _swap_halves(k * s)).astype(ko_ref.dtype)

    tab = pl.BlockSpec((tt, 128), lambda i, h: (i % nps, 0))
    head = pl.BlockSpec((tt, 256), lambda i, h: (i, h))
    return _pc(body, name="rope_bwd", out_shape=(_sds((T, 1024), BF), _sds(dcat.shape, dcat.dtype)),
               grid=(T // tt, MLA_HEADS),
               in_specs=[ANY, head, pl.BlockSpec((tt, 128), lambda i, h: (i, 0)), tab, tab],
               out_specs=(head, pl.BlockSpec((tt, 128), lambda i, h: (i, 3))),
               aliases={0: 1}, sem=("parallel", "arbitrary"))(dcat, dqc, dkrr, cos, sin)


def _attn_scores(q, k_ref, L, tq):
    def sc(lo, hi):
        return _dot(q, k_ref[lo:hi, :], _NT)

    sd = sc(L - tq, L)
    qc = lax.broadcasted_iota(jnp.int32, sd.shape, 0) >> 6
    kc = lax.broadcasted_iota(jnp.int32, sd.shape, 1) >> 6
    sd = jnp.where(kc <= qc, sd, -jnp.inf)
    return sd if L == tq else jnp.concatenate([sc(0, L - tq), sd], axis=1)


def _attn_fwd(mixin, mixin_t, qcat, kcat, kv, Bl, S):
    T = Bl * S
    tq = _pick(S, (256, 128))
    nq = S // tq

    def body(alias_ref, alias_t, q_ref, k_ref, v_ref, o_ref, lse_ref, ot_ref):
        i = pl.program_id(2)
        q = q_ref[...]
        for ii in range(nq):
            @pl.when(i == ii)
            def _(L=(ii + 1) * tq):
                s = _attn_scores(q, k_ref, L, tq)
                m = jnp.max(s, axis=-1, keepdims=True)
                e = jnp.exp(s - m)
                l = jnp.sum(e, axis=-1, keepdims=True)
                o = (_dot(e, v_ref[0:L, :], _NN) / l).astype(o_ref.dtype)
                o_ref[...] = o
                ot_ref[...] = o.T
                lse_ref[...] = jnp.broadcast_to(m + jnp.log(l), lse_ref.shape)

    qrow = lambda b, h, i: b * nq + i
    in_specs = [ANY, ANY,
                pl.BlockSpec((tq, 256), lambda b, h, i: (qrow(b, h, i), h)),
                pl.BlockSpec((S, 256), lambda b, h, i: (b, h)),
                pl.BlockSpec((S, 128), lambda b, h, i: (b, 2 * h + 1))]
    return _pc(body, name="mla_attn_fwd",
               out_shape=(_sds(mixin.shape, mixin.dtype), _sds((T, 512)), _sds(mixin_t.shape, mixin_t.dtype)),
               grid=(Bl, MLA_HEADS, nq), in_specs=in_specs,
               out_specs=(pl.BlockSpec((tq, 128), lambda b, h, i: (qrow(b, h, i), 4 + h)),
                          pl.BlockSpec((tq, 128), lambda b, h, i: (qrow(b, h, i), h)),
                          pl.BlockSpec((128, tq), lambda b, h, i: (4 + h, qrow(b, h, i)))),
               aliases={0: 0, 1: 2}, sem=("parallel", "parallel", "parallel"))(mixin, mixin_t, qcat, kcat, kv)


def _attn_bwd(qcat, kcat, kv, lse, dmixin, Bl, S):
    T = Bl * S
    tq = _pick(S, (256, 128))
    nq = S // tq

    def body(q_ref, k_ref, v_ref, do_ref, lse_ref, dq_ref, dkv_ref, dkr_ref, acck_ref, accv_ref):
        h, i = pl.program_id(1), pl.program_id(2)
        q = q_ref[...]
        do32 = do_ref[...]
        do = do32.astype(BF)
        lse_col = lse_ref[:, 0:1]

        @pl.when(i == 0)
        def _():
            acck_ref[...] = jnp.zeros_like(acck_ref)
            accv_ref[...] = jnp.zeros_like(accv_ref)

        for ii in range(nq):
            @pl.when(i == ii)
            def _(L=(ii + 1) * tq):
                p = jnp.exp(_attn_scores(q, k_ref, L, tq) - lse_col)
                pb = p.astype(BF)
                v = v_ref[0:L, :]
                delta = jnp.sum(do32 * _dot(pb, v, _NN), axis=-1, keepdims=True)
                ds = (p * (_dot(do, v, _NT) - delta)).astype(BF)
                dq_ref[...] = _dot(ds, k_ref[0:L, :], _NN) * ATTN_SCALE
                acck_ref[0:L, :] += _dot(ds, q, _TN)
                accv_ref[0:L, :] += _dot(pb, do, _TN)

        @pl.when(i == nq - 1)
        def _():
            dkv_ref[:, 0:128] = acck_ref[:, 0:128].astype(dkv_ref.dtype)
            dkv_ref[:, 128:256] = accv_ref[...].astype(dkv_ref.dtype)
            dkr_ref[...] = jnp.where(h == 0, 0.0, dkr_ref[...]) + acck_ref[:, 128:256]

    qrow = lambda b, h, i: b * nq + i
    in_specs = [pl.BlockSpec((tq, 256), lambda b, h, i: (qrow(b, h, i), h)),
                pl.BlockSpec((S, 256), lambda b, h, i: (b, h)),
                pl.BlockSpec((S, 128), lambda b, h, i: (b, 2 * h + 1)),
                pl.BlockSpec((tq, 128), lambda b, h, i: (qrow(b, h, i), 4 + h)),
                pl.BlockSpec((tq, 128), lambda b, h, i: (qrow(b, h, i), h))]
    out_shape = (_sds((T, 1024)), _sds((T, 1024), BF), _sds((T, 128)))
    out_specs = (pl.BlockSpec((tq, 256), lambda b, h, i: (qrow(b, h, i), h)),
                 pl.BlockSpec((S, 256), lambda b, h, i: (b, h)),
                 pl.BlockSpec((S, 128), lambda b, h, i: (b, 0)))
    return _pc(body, name="mla_attn_bwd", out_shape=out_shape, grid=(Bl, MLA_HEADS, nq), in_specs=in_specs,
               out_specs=out_specs, scratch=[pltpu.VMEM((S, 256), F32), pltpu.VMEM((S, 128), F32)],
               sem=("arbitrary", "arbitrary", "arbitrary"))(qcat, kcat, kv, dmixin, lse)


def _ln1_fwd(x, mix, g, b):
    T, D = x.shape
    tt = _pick(T, (512, 256))

    def body(x_ref, m_ref, g_ref, b_ref, o_ref, ob_ref, ot_ref):
        h = _layernorm(ALPHA * x_ref[...] + m_ref[...], g_ref[...], b_ref[...])
        o_ref[...] = h
        hb = h.astype(ob_ref.dtype)
        ob_ref[...] = hb
        ot_ref[...] = hb.T

    blk = pl.BlockSpec((tt, D), lambda i: (i, 0))
    par = pl.BlockSpec((1, D), lambda i: (0, 0))
    return _pc(body, name="ln1_fwd", out_shape=(_sds((T, D)), _sds((T, D), BF), _sds((D, T), BF)), grid=(T // tt,),
               in_specs=[blk, blk, par, par], out_specs=(blk, blk, pl.BlockSpec((D, tt), lambda i: (0, i))),
               sem=("parallel",))(x, mix, g, b)


def _ln1_bwd(x, mix, g, b, dh):
    T, D = x.shape
    tt = _pick(T, (256,))

    def body(x_ref, m_ref, g_ref, b_ref, d_ref, dx_ref, dm_ref, acc_ref):
        @pl.when(pl.program_id(0) == 0)
        def _():
            acc_ref[...] = jnp.zeros_like(acc_ref)

        f = lambda xx, mm, gg, bb: _layernorm(ALPHA * xx + mm, gg, bb)
        _, vjp = jax.vjp(f, x_ref[...], m_ref[...], g_ref[...], b_ref[...])
        dx, dm, dg, db = vjp(d_ref[...])
        dx_ref[...] = dx
        dm_ref[...] = dm.astype(dm_ref.dtype)
        acc_ref[0:1, :] += dg
        acc_ref[1:2, :] += db

    blk = pl.BlockSpec((tt, D), lambda i: (i, 0))
    par = pl.BlockSpec((1, D), lambda i: (0, 0))
    return _pc(body, name="ln1_bwd", out_shape=(_sds((T, D)), _sds((T, D), BF), _sds((8, D))), grid=(T // tt,),
               in_specs=[blk, blk, par, par, blk], out_specs=(blk, blk, pl.BlockSpec((8, D), lambda i: (0, 0))),
               sem=("arbitrary",))(x, mix, g, b, dh)


def _ffn_act_fwd(ug, uu, wg, wu, bg, bu, Bl, S):
    T, C = ug.shape
    cb = _pick(C, (256, 128))
    kk = wg.shape[0]

    def body(g_ref, u_ref, wg_ref, wu_ref, bg_ref, bu_ref, o_ref, ot_ref):
        cg = _conv(g_ref[...], wg_ref[...]) + bg_ref[...]
        cu = _conv(u_ref[...], wu_ref[...]) + bu_ref[...]
        a = (_silu(cg) * cu).astype(o_ref.dtype)
        o_ref[...] = a
        ot_ref[...] = a.T

    blk = pl.BlockSpec((S, cb), lambda b, j: (b, j))
    wblk = pl.BlockSpec((kk, cb), lambda b, j: (0, j))
    bblk = pl.BlockSpec((1, cb), lambda b, j: (0, j))
    return _pc(body, name="ffn_act_fwd", out_shape=(_sds((T, C), BF), _sds((C, T), BF)), grid=(Bl, C // cb),
               in_specs=[blk, blk, wblk, wblk, bblk, bblk],
               out_specs=(blk, pl.BlockSpec((cb, S), lambda b, j: (j, b))),
               sem=("parallel", "parallel"))(ug, uu, wg, wu, bg, bu)


def _ffn_act_bwd(ug, uu, wg, wu, bg, bu, dact, Bl, S):
    T, C = ug.shape
    cb = _pick(C, (256, 128))
    kk = wg.shape[0]

    def body(g_ref, u_ref, wg_ref, wu_ref, bg_ref, bu_ref, d_ref,
             dg_ref, du_ref, dwg_ref, dwu_ref, dbg_ref, dbu_ref):
        first = pl.program_id(1) == 0
        wgv, wuv = wg_ref[...], wu_ref[...]
        tg, tu = _taps(g_ref[...], kk), _taps(u_ref[...], kk)
        cg = _conv(None, wgv, tg) + bg_ref[...]
        cu = _conv(None, wuv, tu) + bu_ref[...]
        _, vjp = jax.vjp(lambda a, b: _silu(a) * b, cg, cu)
        dcg, dcu = vjp(d_ref[...].astype(F32))
        dg_ref[...] = _conv_bwd_x(dcg, wgv).astype(dg_ref.dtype)
        du_ref[...] = _conv_bwd_x(dcu, wuv).astype(du_ref.dtype)
        _conv_bwd_w(tg, dcg, dwg_ref, first)
        _conv_bwd_w(tu, dcu, dwu_ref, first)
        dbg_ref[...] = jnp.where(first, 0.0, dbg_ref[...]) + jnp.sum(dcg, axis=0, keepdims=True)
        dbu_ref[...] = jnp.where(first, 0.0, dbu_ref[...]) + jnp.sum(dcu, axis=0, keepdims=True)

    blk = pl.BlockSpec((S, cb), lambda j, b: (b, j))
    wblk = pl.BlockSpec((kk, cb), lambda j, b: (0, j))
    bblk = pl.BlockSpec((1, cb), lambda j, b: (0, j))
    out_shape = (_sds((T, C), BF), _sds((T, C), BF), _sds((kk, C)), _sds((kk, C)), _sds((1, C)), _sds((1, C)))
    return _pc(body, name="ffn_act_bwd", out_shape=out_shape, grid=(C // cb, Bl),
               in_specs=[blk, blk, wblk, wblk, bblk, bblk, blk],
               out_specs=(blk, blk, wblk, wblk, bblk, bblk), sem=("parallel", "arbitrary"))(
        ug, uu, wg, wu, bg, bu, dact)


def _head(h1, ffn, gpre, pp, tgt, bg, g2, b2):
    T, D = h1.shape
    tt = _pick(T, (256,))

    def body(h_ref, f_ref, gp_ref, pp_ref, t_ref, bg_ref, g2_ref, b2_ref, df_ref, dfb_ref, dgp_ref, dpp_ref, acc_ref):
        @pl.when(pl.program_id(0) == 0)
        def _():
            acc_ref[...] = jnp.zeros_like(acc_ref)

        h, tg = h_ref[...], t_ref[...]

        def loss_fn(f, gp, p_, bgv, g2v, b2v):
            pre = ALPHA * h + f + jax.nn.sigmoid(gp + bgv) * p_
            err = _layernorm(pre, g2v, b2v) - tg
            return 0.5 * jnp.sum(jnp.mean(err * err, axis=-1, keepdims=True))

        loss, grads = jax.value_and_grad(loss_fn, argnums=(0, 1, 2, 3, 4, 5))(
            f_ref[...], gp_ref[...], pp_ref[...], bg_ref[...], g2_ref[...], b2_ref[...])
        df_ref[...] = grads[0]
        dfb_ref[...] = grads[0].astype(dfb_ref.dtype)
        dgp_ref[...] = grads[1].astype(dgp_ref.dtype)
        dpp_ref[...] = grads[2].astype(dpp_ref.dtype)
        acc_ref[0:1, :] += grads[3]
        acc_ref[1:2, :] += grads[4]
        acc_ref[2:3, :] += grads[5]
        acc_ref[3:4, :] += jnp.broadcast_to(loss, (1, D))

    blk = pl.BlockSpec((tt, D), lambda i: (i, 0))
    par = pl.BlockSpec((1, D), lambda i: (0, 0))
    o_bf = _sds((T, D), BF)
    return _pc(body, name="loss_head", out_shape=(_sds((T, D)), o_bf, o_bf, o_bf, _sds((8, D))),
               grid=(T // tt,), in_specs=[blk, blk, blk, blk, blk, par, par, par],
               out_specs=(blk, blk, blk, blk, pl.BlockSpec((8, D), lambda i: (0, 0))), sem=("arbitrary",))(
        h1, ffn, gpre, pp, tgt, bg, g2, b2)


def _adamw(parts, w, m, v, name):
    R, C = w.shape
    tr = R if R <= 512 else _pick(R, (256,))

    def body(p_ref, w_ref, m_ref, v_ref, g_ref, d_ref, nm_ref, nv_ref):
        g = p_ref[0].astype(F32)
        for j in range(1, N_DEV):
            g = g + p_ref[j].astype(F32)
        g_ref[...] = g
        d_ref[...], nm_ref[...], nv_ref[...] = _adam_math(g, w_ref[...], m_ref[...], v_ref[...])

    blk = pl.BlockSpec((tr, C), lambda i: (i, 0))
    o = _sds((R, C))
    return _pc(body, name=name, out_shape=(o, o, o, o), grid=(R // tr,),
               in_specs=[pl.BlockSpec((N_DEV, tr, C), lambda i: (0, i, 0)), blk, blk, blk],
               out_specs=(blk, blk, blk, blk), sem=("parallel",))(parts, w, m, v)


def _adam_math(g, w, m, v):
    mm = ADAM_B1 * m + (1.0 - ADAM_B1) * g
    vv = ADAM_B2 * v + (1.0 - ADAM_B2) * jnp.square(g)
    m_hat = mm / (1.0 - ADAM_B1 ** ADAM_STEP)
    v_hat = vv / (1.0 - ADAM_B2 ** ADAM_STEP)
    return -ADAM_LR * (m_hat / (jnp.sqrt(v_hat) + ADAM_EPS) + ADAM_WD * w), mm, vv


def _small_layout(sizes):
    offs, off = {}, 0
    for n in SMALL_ORDER:
        offs[n] = off
        off += -(-sizes[n] // 128) * 128
    return offs, off


def _pack_small(d):
    return jnp.concatenate([_pad_lanes(d[n].reshape(1, -1), -(-d[n].size // 128) * 128) for n in SMALL_ORDER], axis=1)


def _adamw_small(parts, ws, ms, vs):
    k = len(SMALL_ORDER)
    sizes = {n: ws[n].shape[1] for n in SMALL_ORDER}
    offs, _ = _small_layout(sizes)

    def body(p_ref, *refs):
        ins, outs = refs[:3 * k], refs[3 * k:]
        for i, n in enumerate(SMALL_ORDER):
            lo, hi = offs[n], offs[n] + sizes[n]
            g = p_ref[0, :, lo:hi]
            for j in range(1, N_DEV):
                g = g + p_ref[j, :, lo:hi]
            d, mm, vv = _adam_math(g, ins[i][...], ins[k + i][...], ins[2 * k + i][...])
            outs[4 * i][...], outs[4 * i + 1][...], outs[4 * i + 2][...], outs[4 * i + 3][...] = g, d, mm, vv

    out_shape = tuple(_sds((1, sizes[n])) for n in SMALL_ORDER for _ in range(4))
    args = [ws[n] for n in SMALL_ORDER] + [ms[n] for n in SMALL_ORDER] + [vs[n] for n in SMALL_ORDER]
    res = _pc(body, name="adamw_small", out_shape=out_shape)(parts, *args)
    return {n: tuple(res[4 * i:4 * i + 4]) for i, n in enumerate(SMALL_ORDER)}


def _prep(x, p, xchg=None):
    T, D = x.shape
    Dp = p.shape[1]
    tt = _pick(T, (512, 256))

    def body(x_ref, p_ref, xb_ref, xt_ref, pb_ref, pt_ref):
        xb = x_ref[...].astype(BF)
        xb_ref[...] = xb
        xt_ref[...] = xb.T
        pb = p_ref[...].astype(BF)
        pb_ref[...] = pb
        pt_ref[...] = pb.T

    row = lambda w: pl.BlockSpec((tt, w), lambda i: (i, 0))
    col = lambda w: pl.BlockSpec((w, tt), lambda i: (0, i))
    return _pc(body, name="prep_inputs",
               out_shape=(_sds((T, D), BF), _sds((D, T), BF), _sds((T, Dp), BF), _sds((Dp, T), BF)),
               grid=(T // tt,), in_specs=[row(D), row(Dp)], out_specs=(row(D), col(D), row(Dp), col(Dp)),
               sem=("parallel",), xchg=xchg)(x, p)


SMALL_ORDER = ("gdn_a_log", "gdn_dt_bias", "gdn_norm_g", "mla_q_norm_g", "mla_kv_norm_g", "ln1_g", "ln1_b",
               "ffn_conv_b", "ple_b_gate", "ln2_g", "ln2_b")
EARLY = ("w_in", "gdn_conv_w")
LATE = ("mla_w_q_up", "mla_w_kv_up", "w_out", "ffn_w_up", "ffn_conv_w", "ffn_w_down", "ple_w_gate", "ple_w_proj")
GRADS_EARLY = ("w_out", "ffn_w_up", "ffn_conv_w", "ffn_w_down", "ple_w_gate", "ple_w_proj")
GRADS_LATE = ("w_in", "gdn_conv_w", "mla_w_q_up", "mla_w_kv_up")


def _pad_lanes(v, n=128):
    return jnp.pad(v, ((0, 0), (0, n - v.shape[1])))


def _local_step(x, p, tgt, W, sp, Bl, S, early_weights=None, late_weights=None, early_grads=None, late_grads=None):
    T = Bl * S
    W = dict(W)
    prep = _prep(x, p, xchg=None if early_weights is None else early_weights[:2])
    xb, xT, pb, pT = prep[:4]
    if early_weights is not None:
        W.update(early_weights[2](prep[4:]))
    w_in = W["w_in"]
    wqkv, wz = w_in[:, :1536], w_in[:, 1536:2048]
    z64 = jnp.zeros((w_in.shape[0], 64), w_in.dtype)
    z124 = jnp.zeros((w_in.shape[0], 124), w_in.dtype)
    wsm = jnp.concatenate([w_in[:, 2056:2440], w_in[:, 2696:2760], z64, w_in[:, 2440:2696],
                           w_in[:, 2048:2052], z124, w_in[:, 2052:2056], z124], axis=1)
    alog, dtb = _pad_lanes(sp["gdn_a_log"]), _pad_lanes(sp["gdn_dt_bias"])
    cos, sin = _rope_tables(S)

    qkv = _mm(xb, wqkv, "nn", name="mm_qkv")
    z = _mm(xb, wz, "nn", name="mm_z")
    sm = _mm(xb, wsm, "nn", name="mm_sm")
    qkvn = _gdn_pre_fwd(qkv, W["gdn_conv_w"], Bl, S)
    gdn_out = _gdn_fwd(qkvn, sm, alog, dtb, Bl, S, xchg=None if late_weights is None else late_weights[:2])
    o, tinv_s, sst_s, w_s, u_s, gc_s, beta_s = gdn_out[:7]
    if late_weights is not None:
        W.update(late_weights[2](gdn_out[7:]))
    wq = jnp.pad(W["mla_w_q_up"].reshape(-1, MLA_HEADS, 192), ((0, 0), (0, 0), (0, 64))).reshape(-1, 1024)
    wkv = W["mla_w_kv_up"]
    C = W["ffn_w_down"].shape[0]
    wup_g, wup_u = W["ffn_w_up"][:, :C], W["ffn_w_up"][:, C:]
    cw_g, cw_u = W["ffn_conv_w"][:, :C], W["ffn_conv_w"][:, C:]
    cb_g, cb_u = sp["ffn_conv_b"][:, :C], sp["ffn_conv_b"][:, C:]
    mixin, mixin_t = _gdn_post(o, z, sp["gdn_norm_g"], T)
    cqn, cqn_t = _rms_fwd(sm, sp["mla_q_norm_g"], 0, "rms_q_fwd")
    ckvn, ckvn_t = _rms_fwd(sm, sp["mla_kv_norm_g"], 2, "rms_kv_fwd")
    qraw = _mm(cqn, wq, "nn", name="mm_qup")
    kv = _mm(ckvn, wkv, "nn", name="mm_kvup", out_dtype=BF)
    qcat, kcat = _rope_fwd(qraw, kv, sm, cos, sin, S)
    mixin, lse, mixin_t = _attn_fwd(mixin, mixin_t, qcat, kcat, kv, Bl, S)
    mix = _mm(mixin, W["w_out"], "nn", name="mm_out")
    h1, h1b, h1T = _ln1_fwd(x, mix, sp["ln1_g"], sp["ln1_b"])
    ug = _mm(h1b, wup_g, "nn", name="mm_up_gate")
    uu = _mm(h1b, wup_u, "nn", name="mm_up_up")
    act, act_t = _ffn_act_fwd(ug, uu, cw_g, cw_u, cb_g, cb_u, Bl, S)
    ffn = _mm(act, W["ffn_w_down"], "nn", name="mm_down")
    gpre = _mm(h1b, W["ple_w_gate"], "nn", name="mm_gate")
    pp = _mm(pb, W["ple_w_proj"], "nn", name="mm_proj")

    dffn, dffnb, dgpre, dpp, hacc = _head(h1, ffn, gpre, pp, tgt, sp["ple_b_gate"], sp["ln2_g"], sp["ln2_b"])
    loss = hacc[3, 0]
    gW, gs = {}, {}
    gs["ple_b_gate"], gs["ln2_g"], gs["ln2_b"] = hacc[0:1], hacc[1:2], hacc[2:3]
    gW["ple_w_proj"] = _mm(pT, dpp, "nn", name="mm_dproj", out_dtype=BF)
    gW["ple_w_gate"] = _mm(h1T, dgpre, "nn", name="mm_dgate", out_dtype=BF)
    gW["ffn_w_down"] = _mm(act_t, dffnb, "nn", name="mm_ddown", out_dtype=BF)
    dact = _mm(dffnb, W["ffn_w_down"], "nt", name="mm_dact")
    du_g, du_u, dcw_g, dcw_u, dcb_g, dcb_u = _ffn_act_bwd(ug, uu, cw_g, cw_u, cb_g, cb_u, dact, Bl, S)
    gW["ffn_conv_w"] = jnp.concatenate([dcw_g, dcw_u], axis=1)
    gs["ffn_conv_b"] = jnp.concatenate([dcb_g, dcb_u], axis=1)
    gW["ffn_w_up"] = jnp.concatenate([_mm(h1T, du_g, "nn", name="mm_dup_gate", out_dtype=BF),
                                      _mm(h1T, du_u, "nn", name="mm_dup_up", out_dtype=BF)], axis=1)
    dh1 = _mm(dgpre, W["ple_w_gate"], "nt", name="mm_dh1_gate", add=dffn, add_scale=ALPHA)
    dh1 = _mm(du_g, wup_g, "nt", name="mm_dh1_upg", add=dh1)
    dh1 = _mm(du_u, wup_u, "nt", name="mm_dh1_upu", add=dh1)
    dxa, dmix, acc1 = _ln1_bwd(x, mix, sp["ln1_g"], sp["ln1_b"], dh1)
    gs["ln1_g"], gs["ln1_b"] = acc1[0:1], acc1[1:2]
    gW["w_out"] = _mm(mixin_t, dmix, "nn", name="mm_dwout", out_dtype=BF)
    dmixin = _mm(dmix, W["w_out"], "nt", name="mm_dmixin")
    do, dcat, gacc = _gdn_post_bwd(o, z, sp["gdn_norm_g"], dmixin, T)
    gs["gdn_norm_g"] = gacc[0:1]
    bwd_out = _gdn_bwd(qkvn, sm, alog, dtb, gc_s, beta_s, tinv_s, sst_s, w_s, u_s, do, dcat, Bl, S,
                       xchg=None if early_grads is None else early_grads({n: gW[n] for n in GRADS_EARLY}))
    dqkvn, dcat, cacc = bwd_out[:3]
    early_recv = bwd_out[3:]
    gs["gdn_a_log"], gs["gdn_dt_bias"] = cacc[0:1, :GDN_HEADS], cacc[1:2, :GDN_HEADS]
    dcat, gW["gdn_conv_w"] = _gdn_pre_bwd(qkv, W["gdn_conv_w"], dqkvn, dcat, Bl, S)
    dqc, dkv, dkrr = _attn_bwd(qcat, kcat, kv, lse, dmixin, Bl, S)
    dqraw, dcat = _rope_bwd(dqc, dcat, dkrr, cos, sin, S)
    gwq = _mm(cqn_t, dqraw, "nn", name="mm_dwq", out_dtype=BF)
    gW["mla_w_q_up"] = gwq.reshape(-1, MLA_HEADS, 256)[:, :, :192].reshape(-1, MLA_HEADS * 192)
    gW["mla_w_kv_up"] = _mm(ckvn_t, dkv, "nn", name="mm_dwkv", out_dtype=BF)
    dcqn = _mm(dqraw, wq, "nt", name="mm_dcqn")
    dckvn = _mm(dkv, wkv, "nt", name="mm_dckvn")
    dcat, qacc = _rms_bwd(sm, sp["mla_q_norm_g"], dcqn, dcat, 0, "rms_q_bwd")
    dcat, kacc = _rms_bwd(sm, sp["mla_kv_norm_g"], dckvn, dcat, 2, "rms_kv_bwd")
    gs["mla_q_norm_g"], gs["mla_kv_norm_g"] = qacc[0:1], kacc[0:1]
    gcat = _mm(xT, dcat, "nn", name="mm_dwin", out_dtype=BF)
    gsm, gqkv, gz = gcat[:, :DCAT_QKV], gcat[:, DCAT_QKV:DCAT_Z], gcat[:, DCAT_Z:]
    gW["w_in"] = jnp.concatenate([gqkv, gz, gsm[:, 768:772], gsm[:, 896:900], gsm[:, 0:384], gsm[:, 512:768],
                                  gsm[:, 384:448]], axis=1)
    wcat = jnp.concatenate([wsm, wqkv, wz], axis=1)
    dx = _mm(dcat, wcat, "nt", name="mm_dx", add=dxa, xchg=None if late_grads is None else late_grads(gW, gs))
    late_recv = ()
    if late_grads is not None:
        dx, late_recv = dx[0], dx[1:]
    return loss, dx, gW, gs, early_recv, late_recv


COL_SHARDED = ("w_in", "mla_w_q_up", "mla_w_kv_up", "ffn_w_up", "ple_w_proj", "gdn_conv_w", "ffn_conv_w")
SHARDED = EARLY + LATE
WEIGHTS = ("w_in", "gdn_conv_w", "gdn_a_log", "gdn_dt_bias", "gdn_norm_g", "mla_q_norm_g", "mla_w_q_up",
           "mla_kv_norm_g", "mla_w_kv_up", "w_out", "ln1_g", "ln1_b", "ffn_w_up", "ffn_conv_w", "ffn_conv_b",
           "ffn_w_down", "ple_w_gate", "ple_b_gate", "ple_w_proj", "ln2_g", "ln2_b")
CONV = ("gdn_conv_w", "ffn_conv_w")


def _gathered_to_full(name, g):
    if name in COL_SHARDED:
        return jnp.transpose(g, (1, 0, 2)).reshape(g.shape[1], -1)
    return g.reshape(-1, g.shape[-1])


def _full_to_blocks(name, gfull, shard_shape):
    r, c = shard_shape
    if name in COL_SHARDED:
        return jnp.transpose(gfull.reshape(r, N_DEV, c), (1, 0, 2))
    return gfull.reshape(N_DEV, r, c)


def kernel(x, p, w_in, gdn_conv_w, gdn_a_log, gdn_dt_bias, gdn_norm_g, mla_q_norm_g, mla_w_q_up, mla_kv_norm_g, mla_w_kv_up, w_out, ln1_g, ln1_b, ffn_w_up, ffn_conv_w, ffn_conv_b, ffn_w_down, ple_w_gate, ple_b_gate, ple_w_proj, ln2_g, ln2_b, loss_target, m_w_in, m_gdn_conv_w, m_gdn_a_log, m_gdn_dt_bias, m_gdn_norm_g, m_mla_q_norm_g, m_mla_w_q_up, m_mla_kv_norm_g, m_mla_w_kv_up, m_w_out, m_ln1_g, m_ln1_b, m_ffn_w_up, m_ffn_conv_w, m_ffn_conv_b, m_ffn_w_down, m_ple_w_gate, m_ple_b_gate, m_ple_w_proj, m_ln2_g, m_ln2_b, v_w_in, v_gdn_conv_w, v_gdn_a_log, v_gdn_dt_bias, v_gdn_norm_g, v_mla_q_norm_g, v_mla_w_q_up, v_mla_kv_norm_g, v_mla_w_kv_up, v_w_out, v_ln1_g, v_ln1_b, v_ffn_w_up, v_ffn_conv_w, v_ffn_conv_b, v_ffn_w_down, v_ple_w_gate, v_ple_b_gate, v_ple_w_proj, v_ln2_g, v_ln2_b):
    loc = dict(locals())
    wts = {n: loc[n] for n in WEIGHTS}
    ms = {n: loc["m_" + n] for n in WEIGHTS}
    vs = {n: loc["v_" + n] for n in WEIGHTS}
    Bl, S, D = x.shape
    T = Bl * S

    wire = lambda n: wts[n][0] if n in CONV else wts[n][0].astype(BF)
    gather = lambda names, mode: ([wire(n) for n in names], [mode] * len(names),
                                  lambda res: {n: _gathered_to_full(n, g) for n, g in zip(names, res)})
    blocks = lambda g, names: [_full_to_blocks(n, g[n], wts[n].shape[1:]) for n in names]
    pack_early = lambda g: (blocks(g, GRADS_EARLY), ["a2a"] * len(GRADS_EARLY))
    pack_late = lambda g, gs: (blocks(g, GRADS_LATE) + [_pack_small(gs)], ["a2a"] * len(GRADS_LATE) + ["ag"])
    sp = {n: wts[n].reshape(1, -1) for n in SMALL_ORDER}

    loss, dx, gW, gs, early_recv, late_recv = _local_step(
        x.reshape(T, D), p.reshape(T, -1), loss_target.reshape(T, D), {}, sp, Bl, S,
        early_weights=gather(EARLY, "ag2"), late_weights=gather(LATE, "ag"), early_grads=pack_early,
        late_grads=pack_late)

    res = {}
    for n, parts in list(zip(GRADS_EARLY, early_recv)) + list(zip(GRADS_LATE, late_recv[:-1])):
        res[n] = tuple(t[None] for t in _adamw(parts, wts[n][0], ms[n][0], vs[n][0], "adamw_" + n))
    res.update(_adamw_small(late_recv[-1], wts, ms, vs))

    loss = lax.psum(loss, ("x", "y", "c"))
    outs = [loss, dx.reshape(Bl, S, D)]
    for k in range(4):
        outs += [res[n][k] for n in WEIGHTS]
    return tuple(outs)
```

```python
import math

import jax
import jax.numpy as jnp
from jax import lax
from jax.experimental import pallas as pl
from jax.experimental.pallas import tpu as pltpu

F32 = jnp.float32
BF = jnp.bfloat16
_MXU = jnp.bfloat16
_INTERPRET = None
_VMEM_LIMIT = 56 * 1024 * 1024
HI = lax.Precision.HIGHEST

N_DEV = 8
CHUNK = 64
GDN_HEADS = 4
GDN_DK = 128
MLA_HEADS = 4
MLA_NOPE = 128
MLA_ROPE = 64
ROPE_THETA = 10000.0
ALPHA = 2.0 ** 0.25
NORM_EPS = 1e-6
ATTN_SCALE = (MLA_NOPE + MLA_ROPE) ** -0.5
DCAT_QKV, DCAT_Z, DCAT_W = 1024, 2560, 3072
ADAM_LR, ADAM_B1, ADAM_B2, ADAM_EPS, ADAM_WD, ADAM_STEP = 0.001, 0.9, 0.999, 1e-08, 0.01, 10
MESH = pl.DeviceIdType.MESH
ANY = pl.BlockSpec(memory_space=pl.ANY)
_NN = (((1,), (0,)), ((), ()))
_NT = (((1,), (1,)), ((), ()))
_TN = (((0,), (0,)), ((), ()))


def _sds(shape, dtype=F32):
    return jax.ShapeDtypeStruct(tuple(shape), dtype)


def _pick(n, cands):
    for c in cands:
        if n % c == 0:
            return c
    return n


def _xchg_out_shapes(arrs, modes):
    return [_sds(a.shape if m == "a2a" else (N_DEV,) + a.shape, a.dtype) for a, m in zip(arrs, modes)]


def _xchg_scratch(n):
    return [pltpu.SemaphoreType.DMA((n, N_DEV - 1)), pltpu.SemaphoreType.DMA((n, N_DEV - 1)),
            pltpu.SemaphoreType.DMA((n,))]


def _xchg_plan(ins, outs, send, recv, loc, modes):
    x, y, c = lax.axis_index("x"), lax.axis_index("y"), lax.axis_index("c")
    me = 4 * x + 2 * y + c
    starts, relays, waits = [], [], []
    for ai, mode in enumerate(modes):
        src_all, out = ins[ai], outs[ai]

        def remote(k, src, dst, dev):
            return pltpu.make_async_remote_copy(src_ref=src, dst_ref=dst, send_sem=send.at[ai, k], recv_sem=recv.at[ai, k],
                                                device_id=dev, device_id_type=MESH)

        own = pltpu.make_async_copy(src_all.at[me] if mode == "a2a" else src_all, out.at[me], loc.at[ai])
        starts.append(own)
        waits.append(own.wait)
        if mode == "ag2":
            sib = (x, y, 1 - c)
            chips = [(1 - x, y), (x, 1 - y), (1 - x, 1 - y)]
            first = [remote(0, src_all, out.at[me], sib)]
            first += [remote(1 + k, src_all, out.at[me], (px, py, c)) for k, (px, py) in enumerate(chips)]
            starts += first
            waits += [cp.wait_send for cp in first]
            waits.append(remote(0, src_all, out.at[4 * x + 2 * y + 1 - c], sib).wait_recv)
            for k, (px, py) in enumerate(chips):
                same, other = out.at[4 * px + 2 * py + c], out.at[4 * px + 2 * py + 1 - c]
                relay = remote(4 + k, same, same, sib)
                relays.append((remote(1 + k, src_all, same, (px, py, c)), relay))
                waits += [relay.wait_send, remote(4 + k, src_all, other, sib).wait_recv]
            continue
        for r in range(1, N_DEV):
            px = 1 - x if (r >> 2) & 1 else x
            py = 1 - y if (r >> 1) & 1 else y
            pc = 1 - c if r & 1 else c
            cp = remote(r - 1, src_all.at[4 * px + 2 * py + pc] if mode == "a2a" else src_all, out.at[me], (px, py, pc))
            starts.append(cp)
            waits.append(cp.wait)
    return starts, relays, waits


def _pc(body, *, name, out_shape, grid=None, in_specs=None, out_specs=None, scratch=(), sem=None, aliases=None,
        xchg=None):
    kw = {}
    if _INTERPRET is not None:
        kw["interpret"] = _INTERPRET
    single = not isinstance(out_shape, (tuple, list))
    out_shape = [out_shape] if single else list(out_shape)
    if out_specs is not None:
        out_specs = [out_specs] if single else list(out_specs)
    scratch = list(scratch)
    if xchg is not None:
        xarrs, xmodes = xchg
        n_in, n_out, n_scr, nx = len(in_specs), len(out_shape), len(scratch), len(xarrs)
        single = False
        inner = body

        def body(*refs):
            ins, xins = refs[:n_in], refs[n_in:n_in + nx]
            outs = refs[n_in + nx:n_in + nx + n_out]
            xouts = refs[n_in + nx + n_out:n_in + 2 * nx + n_out]
            scr = refs[n_in + 2 * nx + n_out:n_in + 2 * nx + n_out + n_scr]
            send, recv, loc = refs[n_in + 2 * nx + n_out + n_scr:]
            first = last = None
            for d, g in enumerate(grid):
                f, l = pl.program_id(d) == 0, pl.program_id(d) == g - 1
                first, last = (f, l) if first is None else (first & f, last & l)

            @pl.when(first)
            def _():
                for cp in _xchg_plan(xins, xouts, send, recv, loc, xmodes)[0]:
                    cp.start()

            inner(*ins, *outs, *scr)

            @pl.when(last)
            def _():
                _, relays, waits = _xchg_plan(xins, xouts, send, recv, loc, xmodes)
                for arrival, relay in relays:
                    arrival.wait_recv()
                    relay.start()
                for wait in waits:
                    wait()

        in_specs = list(in_specs) + [ANY] * nx
        out_specs = out_specs + [ANY] * nx
        out_shape = out_shape + _xchg_out_shapes(xarrs, xmodes)
        scratch = scratch + _xchg_scratch(nx)
        sem = ("arbitrary",) * len(grid)
    if grid is not None:
        kw["grid"] = grid
    if in_specs is not None:
        kw["in_specs"] = in_specs
    if out_specs is not None:
        kw["out_specs"] = out_specs[0] if single else tuple(out_specs)
    cp = dict(vmem_limit_bytes=_VMEM_LIMIT)
    if sem is not None:
        cp["dimension_semantics"] = sem
    call = pl.pallas_call(body, name=name, out_shape=out_shape[0] if single else tuple(out_shape),
                          scratch_shapes=scratch, input_output_aliases=aliases or {},
                          compiler_params=pltpu.CompilerParams(**cp), **kw)
    if xchg is None:
        return call
    return lambda *ins: call(*ins, *xchg[0])


_FULL_K_MAX = 3072


def _mm(a, b, mode, *, name, add=None, add_scale=1.0, out_dtype=F32, xchg=None):
    (M, K) = a.shape
    (K2, N) = b.shape if mode == "nn" else b.shape[::-1]
    assert K == K2, (a.shape, b.shape, mode)
    tm = _pick(M, (1024, 1408, 512, 384, 256, 128))
    tn = _pick(N, (1024, 2816, 1408, 768, 512, 384, 256, 128))
    tk = K if K <= _FULL_K_MAX else _pick(K, (2048, 1408, 1024, 512) if tn <= 1408 else (1024, 512))
    nk = K // tk
    dims = _NN if mode == "nn" else _NT
    has_add = add is not None

    def finish(r, add_ref, o_ref):
        if has_add:
            r = r + add_scale * add_ref[...].astype(F32)
        o_ref[...] = r.astype(out_dtype)

    if nk == 1:
        def body(a_ref, b_ref, *rest):
            r = lax.dot_general(a_ref[...].astype(_MXU), b_ref[...].astype(_MXU), dims, preferred_element_type=F32)
            finish(r, rest[0] if has_add else None, rest[-1])

        a_spec = pl.BlockSpec((tm, K), lambda i, j: (i, 0))
        b_spec = (pl.BlockSpec((K, tn), lambda i, j: (0, j)) if mode == "nn"
                  else pl.BlockSpec((tn, K), lambda i, j: (j, 0)))
        o_spec = pl.BlockSpec((tm, tn), lambda i, j: (i, j))
        grid, sem, scratch = (M // tm, N // tn), ("parallel", "parallel"), []
    else:
        def body(a_ref, b_ref, *rest):
            o_ref, acc = rest[-2], rest[-1]
            k = pl.program_id(2)

            @pl.when(k == 0)
            def _():
                acc[...] = jnp.zeros_like(acc)

            acc[...] += lax.dot_general(a_ref[...].astype(_MXU), b_ref[...].astype(_MXU), dims,
                                        preferred_element_type=F32)

            @pl.when(k == nk - 1)
            def _():
                finish(acc[...], rest[0] if has_add else None, o_ref)

        a_spec = pl.BlockSpec((tm, tk), lambda i, j, k: (i, k))
        b_spec = (pl.BlockSpec((tk, tn), lambda i, j, k: (k, j)) if mode == "nn"
                  else pl.BlockSpec((tn, tk), lambda i, j, k: (j, k)))
        o_spec = pl.BlockSpec((tm, tn), lambda i, j, k: (i, j))
        grid, sem, scratch = (M // tm, N // tn, nk), ("parallel", "parallel", "arbitrary"), [pltpu.VMEM((tm, tn), F32)]

    ins = [a, b] + ([add] if has_add else [])
    specs = [a_spec, b_spec] + ([o_spec] if has_add else [])
    return _pc(body, name=name, out_shape=_sds((M, N), out_dtype), grid=grid, in_specs=specs, out_specs=o_spec,
               scratch=scratch, sem=sem, xchg=xchg)(*ins)


def _dot(a, b, dims):
    return lax.dot_general(a.astype(_MXU), b.astype(_MXU), dims, preferred_element_type=F32)


def _split(a):
    hi = a.astype(BF)
    lo = (a - hi.astype(F32)).astype(BF)
    return hi, lo


def _dot3(a, b, dims):
    ah, al = _split(a)
    bh, bl = _split(b)
    d = lambda u, v: lax.dot_general(u, v, dims, preferred_element_type=F32)
    return d(ah, bh) + (d(ah, bl) + d(al, bh))


def _softplus(x):
    return jnp.maximum(x, 0.0) + jnp.log1p(jnp.exp(-jnp.abs(x)))


def _silu(x):
    return x * jax.nn.sigmoid(x)


def _rmsnorm(x, g):
    return x * lax.rsqrt(jnp.mean(x * x, axis=-1, keepdims=True) + NORM_EPS) * g


def _layernorm(x, g, b):
    mu = jnp.mean(x, axis=-1, keepdims=True)
    xc = x - mu
    var = jnp.mean(xc * xc, axis=-1, keepdims=True)
    return xc * lax.rsqrt(var + NORM_EPS) * g + b


def _shift_dn(x, s):
    if s == 0:
        return x
    t = lax.broadcasted_iota(jnp.int32, x.shape, 0)
    return jnp.where(t >= s, pltpu.roll(x, s, 0), 0.0)


def _shift_up(x, s):
    if s == 0:
        return x
    n = x.shape[0]
    t = lax.broadcasted_iota(jnp.int32, x.shape, 0)
    return jnp.where(t < n - s, pltpu.roll(x, n - s, 0), 0.0)


def _taps(x, kk):
    return [_shift_dn(x, kk - 1 - j) for j in range(kk)]


def _conv(x, w, taps=None):
    kk = w.shape[0]
    taps = _taps(x, kk) if taps is None else taps
    y = w[kk - 1:kk, :] * taps[kk - 1]
    for j in range(kk - 1):
        y = y + w[j:j + 1, :] * taps[j]
    return y


def _conv_bwd_x(dy, w):
    kk = w.shape[0]
    dx = w[kk - 1:kk, :] * dy
    for j in range(kk - 1):
        dx = dx + w[j:j + 1, :] * _shift_up(dy, kk - 1 - j)
    return dx


def _conv_bwd_w(taps, dy, dw_ref, first):
    kk = dw_ref.shape[0]
    for j in range(kk):
        r = jnp.sum(dy * taps[j], axis=0, keepdims=True)
        prev = jnp.where(first, 0.0, dw_ref[j:j + 1, :])
        dw_ref[j:j + 1, :] = prev + r


def _gdn_post_conv(c, j):
    h = _silu(c)
    hn = h * lax.rsqrt(jnp.sum(h * h, axis=-1, keepdims=True) + NORM_EPS)
    return jnp.where(j < GDN_HEADS, hn * (GDN_DK ** -0.5), jnp.where(j < 2 * GDN_HEADS, hn, h))


def _gdn_pre_fwd(qkv, conv_w, Bl, S):
    T, C = qkv.shape
    nj = C // 128

    def body(x_ref, w_ref, o_ref):
        o_ref[...] = _gdn_post_conv(_conv(x_ref[...], w_ref[...]), pl.program_id(1))

    blk = pl.BlockSpec((S, 128), lambda b, j: (b, j))
    return _pc(body, name="gdn_pre_fwd", out_shape=_sds((T, C)), grid=(Bl, nj),
               in_specs=[blk, pl.BlockSpec((conv_w.shape[0], 128), lambda b, j: (0, j))], out_specs=blk,
               sem=("parallel", "parallel"))(qkv, conv_w)


def _gdn_pre_bwd(qkv, conv_w, dout, dcat, Bl, S):
    T, C = qkv.shape
    nj = C // 128
    kk = conv_w.shape[0]

    def body(x_ref, w_ref, d_ref, alias_ref, dx_ref, dw_ref):
        j, b = pl.program_id(0), pl.program_id(1)
        w = w_ref[...]
        taps = _taps(x_ref[...], kk)
        c = _conv(None, w, taps)
        _, vjp = jax.vjp(lambda u: _gdn_post_conv(u, j), c)
        (dc,) = vjp(d_ref[...])
        dx_ref[...] = _conv_bwd_x(dc, w).astype(dx_ref.dtype)
        _conv_bwd_w(taps, dc, dw_ref, b == 0)

    blk = pl.BlockSpec((S, 128), lambda j, b: (b, j))
    wblk = pl.BlockSpec((kk, 128), lambda j, b: (0, j))
    return _pc(body, name="gdn_pre_bwd", out_shape=(_sds(dcat.shape, dcat.dtype), _sds((kk, C))), grid=(nj, Bl),
               in_specs=[blk, wblk, blk, ANY],
               out_specs=(pl.BlockSpec((S, 128), lambda j, b: (b, DCAT_QKV // 128 + j)), wblk),
               aliases={3: 0}, sem=("parallel", "arbitrary"))(qkv, conv_w, dout, dcat)


_H = GDN_HEADS
_HC = _H * CHUNK


def _st(x):
    return jnp.concatenate([x[:, h * 128:(h + 1) * 128] for h in range(_H)], axis=0)


def _unst(x):
    return jnp.concatenate([x[h * CHUNK:(h + 1) * CHUNK] for h in range(_H)], axis=1)


def _stc(t):
    return jnp.concatenate([t[:, h:h + 1] for h in range(_H)], axis=0)


def _untile(col):
    lane = lax.broadcasted_iota(jnp.int32, (CHUNK, 128), 1)
    out = jnp.zeros((CHUNK, 128), F32)
    for h in range(_H):
        out = out + jnp.where(lane == h, col[h * CHUNK:(h + 1) * CHUNK], 0.0)
    return out


def _rowform(col):
    return jnp.broadcast_to(col, (_HC, 128)).T[0:1, :]


def _tri(n, upper=False):
    i = lax.broadcasted_iota(jnp.int32, (n, n), 0)
    j = lax.broadcasted_iota(jnp.int32, (n, n), 1)
    return jnp.where((j >= i) if upper else (j <= i), 1.0, 0.0).astype(F32)


def _gdn_chunk_common(q, k, v, gc, beta):
    Q, K, V = _st(q), _st(k), _st(v)
    B, GC = _stc(beta), _stc(gc)
    GL = jnp.concatenate([jnp.broadcast_to(gc[CHUNK - 1:CHUNK, h:h + 1], (CHUNK, 1)) for h in range(_H)], axis=0)
    ii = lax.broadcasted_iota(jnp.int32, (_HC, _HC), 0)
    jj = lax.broadcasted_iota(jnp.int32, (_HC, _HC), 1)
    same = (ii >> 6) == (jj >> 6)
    incl = same & (ii >= jj)
    strict = same & (ii > jj)
    diff = GC - _rowform(GC)
    D = jnp.where(incl, jnp.exp(jnp.where(incl, diff, 0.0)), 0.0)
    KB = K * B
    A = jnp.where(strict, _dot(KB, K, _NT) * D, 0.0)
    P = jnp.where(incl, _dot(Q, K, _NT) * D, 0.0)
    EG = jnp.exp(GC)
    ED = jnp.exp(GL - GC)
    return dict(Q=Q, K=K, V=V, B=B, GC=GC, GL=GL, incl=incl, strict=strict, D=D, KB=KB, A=A, P=P, EG=EG, ED=ED,
                QG=Q * EG, KD=K * ED)


def _neumann_inv(A):
    n = A.shape[0]
    i = lax.broadcasted_iota(jnp.int32, (n, n), 0)
    j = lax.broadcasted_iota(jnp.int32, (n, n), 1)
    N = -A
    acc = jnp.where(i == j, 1.0, 0.0) + N
    Pw = N
    for _ in range(5):
        Pw = _dot3(Pw, Pw, _NN)
        acc = acc + _dot3(acc, Pw, _NN)
    return acc


def _gates(a, alog, dtb):
    return -jnp.exp(alog) * _softplus(a + dtb)


def _hs(x, h, n=CHUNK):
    return x[h * n:(h + 1) * n]


def _per_step(Bl):
    return 4 if Bl % 4 == 0 else 2 if Bl % 2 == 0 else 1


def _gdn_fwd(qkvn, sm, alog, dtb, Bl, S, xchg=None):
    T = Bl * S
    nC = S // CHUNK
    E = _per_step(Bl)

    def body(*refs):
        S_ref = refs[-1]

        @pl.when(pl.program_id(1) == 0)
        def _():
            S_ref[...] = jnp.zeros_like(S_ref)

        for e in range(E):
            one(*[r if i in (5, 6) else r.at[e] for i, r in enumerate(refs)])

    def one(q_ref, k_ref, v_ref, a_ref, b_ref, alog_ref, dtb_ref,
            o_ref, tinv_ref, sst_ref, w_ref, u_ref, gc_ref, beta_ref, S_ref):
        g = _gates(a_ref[...], alog_ref[...], dtb_ref[...])
        beta = jax.nn.sigmoid(b_ref[...])
        gc = jnp.dot(_tri(CHUNK), g, precision=HI, preferred_element_type=F32)
        cm = _gdn_chunk_common(q_ref[...], k_ref[...], v_ref[...], gc, beta)
        tinv = _neumann_inv(cm["A"])
        WU = _dot3(tinv, jnp.concatenate([cm["KB"] * cm["EG"], cm["V"] * cm["B"]], axis=1), _NN)
        W, U = WU[:, :128], WU[:, 128:]
        s_old = [S_ref[h * 128:(h + 1) * 128, :] for h in range(_H)]
        ws = [_dot(jnp.concatenate([_hs(W, h), _hs(cm["QG"], h)], axis=0), s_old[h], _NN) for h in range(_H)]
        vn = [_hs(U, h) - ws[h][:CHUNK] for h in range(_H)]
        o_intra = _dot(cm["P"], jnp.concatenate(vn, axis=0), _NN)
        outs = []
        for h in range(_H):
            outs.append(ws[h][CHUNK:] + _hs(o_intra, h))
            gl = jnp.exp(gc[CHUNK - 1:CHUNK, h:h + 1])
            S_ref[h * 128:(h + 1) * 128, :] = gl * s_old[h] + _dot(_hs(cm["KD"], h), vn[h], _TN)
            sst_ref[h * 128:(h + 1) * 128, :] = s_old[h]
        o_ref[...] = jnp.concatenate(outs, axis=1)
        tinv_ref[...] = tinv
        w_ref[...] = _unst(W)
        u_ref[...] = _unst(U)
        gc_ref[...] = gc
        beta_ref[...] = beta

    blk = lambda r, w, c: pl.BlockSpec((E, r, w), lambda b, n: (b, n, c))
    par = pl.BlockSpec((1, 128), lambda b, n: (0, 0))
    out_shape = (_sds((Bl, S, 512)), _sds((Bl, nC * _HC, _HC)), _sds((Bl, nC * 512, 128)),
                 _sds((Bl, S, 512)), _sds((Bl, S, 512)), _sds((Bl, S, 128)), _sds((Bl, S, 128)))
    out_specs = (blk(CHUNK, 512, 0), blk(_HC, _HC, 0), blk(512, 128, 0), blk(CHUNK, 512, 0), blk(CHUNK, 512, 0),
                 blk(CHUNK, 128, 0), blk(CHUNK, 128, 0))
    q3, s3 = qkvn.reshape(Bl, S, -1), sm.reshape(Bl, S, -1)
    res = _pc(body, name="gdn_chunk_fwd", out_shape=out_shape, grid=(Bl // E, nC),
              in_specs=[blk(CHUNK, 512, 0), blk(CHUNK, 512, 1), blk(CHUNK, 512, 2), blk(CHUNK, 128, 6),
                        blk(CHUNK, 128, 7), par, par],
              out_specs=out_specs, scratch=[pltpu.VMEM((E, 512, 128), F32)], sem=("parallel", "arbitrary"),
              xchg=xchg)(q3, q3, q3, s3, s3, alog, dtb)
    return (res[0].reshape(T, 512),) + tuple(res[1:])


def _gdn_bwd(qkvn, sm, alog, dtb, gc_s, beta_s, tinv_s, sst_s, w_s, u_s, do, dcat, Bl, S, xchg=None):
    T = Bl * S
    nC = S // CHUNK
    E = _per_step(Bl)

    def body(*refs):
        acc_ref, dS_ref = refs[-2], refs[-1]
        first_chunk = pl.program_id(1) == 0

        @pl.when(first_chunk)
        def _():
            dS_ref[...] = jnp.zeros_like(dS_ref)

        @pl.when(first_chunk & (pl.program_id(0) == 0))
        def _():
            acc_ref[...] = jnp.zeros_like(acc_ref)

        for e in range(E):
            one(*[r if i in (4, 5, 13, 16) else r.at[e] for i, r in enumerate(refs)])

    def one(q_ref, k_ref, v_ref, a_ref, alog_ref, dtb_ref, gc_ref, beta_ref, tinv_ref, sst_ref, w_ref, u_ref,
            do_ref, alias_ref, dqkv_ref, dsm_ref, acc_ref, dS_ref):
        gc, beta = gc_ref[...], beta_ref[...]
        cm = _gdn_chunk_common(q_ref[...], k_ref[...], v_ref[...], gc, beta)
        Q, K, V, B, D, KB, A, P = (cm[n] for n in ("Q", "K", "V", "B", "D", "KB", "A", "P"))
        EG, ED, QG, KD = cm["EG"], cm["ED"], cm["QG"], cm["KD"]
        tinv = tinv_ref[...]
        W, U, DO = _st(w_ref[...]), _st(u_ref[...]), _st(do_ref[...])
        s_old = [sst_ref[h * 128:(h + 1) * 128, :] for h in range(_H)]
        ds_new = [dS_ref[h * 128:(h + 1) * 128, :] for h in range(_H)]
        VN = jnp.concatenate([_hs(U, h) - _dot(_hs(W, h), s_old[h], _NN) for h in range(_H)], axis=0)
        dP = jnp.where(cm["incl"], _dot(DO, VN, _NT), 0.0)
        dVN0 = _dot(P, DO, _TN)
        dVN, dQG, dKD, dW, TL = [], [], [], [], []
        for h in range(_H):
            gl = jnp.exp(gc[CHUNK - 1:CHUNK, h:h + 1])
            dvn = _hs(dVN0, h) + _dot(_hs(KD, h), ds_new[h], _NN)
            dkd = _dot(_hs(VN, h), ds_new[h], _NT)
            dvn_do = jnp.concatenate([dvn, _hs(DO, h)], axis=0)
            both = _dot(dvn_do, s_old[h], _NT)
            dVN.append(dvn)
            dQG.append(both[CHUNK:])
            dKD.append(dkd)
            dW.append(-both[:CHUNK])
            dgl = jnp.sum(jnp.sum(ds_new[h] * s_old[h], axis=1, keepdims=True), axis=0, keepdims=True)
            tl = jnp.sum(jnp.sum(dkd * _hs(KD, h), axis=1, keepdims=True), axis=0, keepdims=True) + dgl * gl
            TL.append(jnp.broadcast_to(tl, (CHUNK, 1)))
            dS_ref[h * 128:(h + 1) * 128, :] = gl * ds_new[h] + _dot(
                jnp.concatenate([-_hs(W, h), _hs(QG, h)], axis=0), dvn_do, _TN)
        dVN, dQG, dKD, dW, TL = (jnp.concatenate(z, axis=0) for z in (dVN, dQG, dKD, dW, TL))
        dB = _dot3(tinv, jnp.concatenate([dVN, dW], axis=1), _TN)
        dVB, dKBE = dB[:, :128], dB[:, 128:]
        dA = jnp.where(cm["strict"], -_dot3(dB, jnp.concatenate([U, W], axis=1), _NT), 0.0)
        dG = dA * D
        dQK = dP * D
        dKB = _dot(dG, K, _NN) + dKBE * EG
        dK = (_dot(jnp.concatenate([dG, dQK], axis=0), jnp.concatenate([KB, Q], axis=0), _TN)
              + dKB * B + dKD * ED)
        dQ = _dot(dQK, K, _NN) + dQG * EG
        Mx = dA * A + dP * P
        rs = lambda z: jnp.sum(z, axis=1, keepdims=True)
        ri = lax.broadcasted_iota(jnp.int32, (_HC, 1), 0)
        dGC = (rs(Mx) - rs(Mx.T) + rs(dKBE * KB * EG) + rs(dQG * QG) - rs(dKD * KD)
               + jnp.where((ri & (CHUNK - 1)) == CHUNK - 1, TL, 0.0))
        dBeta = rs(dVB * V) + rs(dKB * K)
        dqkv_ref[:, 0:512] = _unst(dQ)
        dqkv_ref[:, 512:1024] = _unst(dK)
        dqkv_ref[:, 1024:1536] = _unst(dVB * B)
        dg = jnp.dot(_tri(CHUNK, upper=True), _untile(dGC), precision=HI, preferred_element_type=F32)
        a, alog_v, dtb_v = a_ref[...], alog_ref[...], dtb_ref[...]
        g = _gates(a, alog_v, dtb_v)
        lane = lax.broadcasted_iota(jnp.int32, (CHUNK, 128), 1)
        valid = lane < _H
        da = jnp.where(valid, dg * (-jnp.exp(alog_v)) * jax.nn.sigmoid(a + dtb_v), 0.0)
        dsm_ref[:, 0:128] = da.astype(dsm_ref.dtype)
        dsm_ref[:, 128:256] = jnp.where(valid, _untile(dBeta) * beta * (1.0 - beta), 0.0).astype(dsm_ref.dtype)
        acc_ref[0:1, :] += jnp.sum(jnp.where(valid, dg * g, 0.0), axis=0, keepdims=True)
        acc_ref[1:2, :] += jnp.sum(da, axis=0, keepdims=True)

    blk = lambda r, w, c: pl.BlockSpec((E, r, w), lambda b, n: (b, nC - 1 - n, c))
    par = pl.BlockSpec((1, 128), lambda b, n: (0, 0))
    in_specs = [blk(CHUNK, 512, 0), blk(CHUNK, 512, 1), blk(CHUNK, 512, 2), blk(CHUNK, 128, 6), par, par,
                blk(CHUNK, 128, 0), blk(CHUNK, 128, 0), blk(_HC, _HC, 0), blk(512, 128, 0),
                blk(CHUNK, 512, 0), blk(CHUNK, 512, 0), blk(CHUNK, 512, 0), ANY]
    out_shape = (_sds((Bl, S, 1536)), _sds((Bl, S, DCAT_W), BF), _sds((8, 128)))
    out_specs = (blk(CHUNK, 1536, 0), blk(CHUNK, 256, 3), pl.BlockSpec((8, 128), lambda b, n: (0, 0)))
    q3, s3 = qkvn.reshape(Bl, S, -1), sm.reshape(Bl, S, -1)
    res = _pc(body, name="gdn_chunk_bwd", out_shape=out_shape, grid=(Bl // E, nC), in_specs=in_specs,
              out_specs=out_specs, scratch=[pltpu.VMEM((E, 512, 128), F32)], sem=("arbitrary", "arbitrary"),
              aliases={13: 1}, xchg=xchg)(q3, q3, q3, s3, alog, dtb, gc_s, beta_s, tinv_s, sst_s, w_s, u_s,
                                          do.reshape(Bl, S, -1), dcat.reshape(Bl, S, DCAT_W))
    return (res[0].reshape(T, 1536), res[1].reshape(T, DCAT_W)) + tuple(res[2:])


def _gdn_post(o, z, g, T):
    tt = _pick(T, (2048, 1024, 512, 256))

    def body(o_ref, z_ref, g_ref, out_ref, outt_ref):
        y = (_rmsnorm(o_ref[...], g_ref[...]) * _silu(z_ref[...])).astype(out_ref.dtype)
        out_ref[...] = y
        outt_ref[...] = y.T

    blk = pl.BlockSpec((tt, 128), lambda i, h: (i, h))
    return _pc(body, name="gdn_post_fwd", out_shape=(_sds((T, 1024), BF), _sds((1024, T), BF)), grid=(T // tt, _H),
               in_specs=[blk, blk, pl.BlockSpec((1, 128), lambda i, h: (0, 0))],
               out_specs=(blk, pl.BlockSpec((128, tt), lambda i, h: (h, i))),
               sem=("parallel", "parallel"))(o, z, g)


def _gdn_post_bwd(o, z, g, dmixin, T):
    tt = _pick(T, (2048, 1024, 512, 256))

    def body(o_ref, z_ref, g_ref, d_ref, do_ref, dz_ref, dg_ref):
        @pl.when((pl.program_id(0) == 0) & (pl.program_id(1) == 0))
        def _():
            dg_ref[...] = jnp.zeros_like(dg_ref)

        _, vjp = jax.vjp(lambda a, b, c: _rmsnorm(a, c) * _silu(b), o_ref[...], z_ref[...], g_ref[...])
        do, dz, dg = vjp(d_ref[...])
        do_ref[...] = do
        dz_ref[...] = dz.astype(dz_ref.dtype)
        dg_ref[0:1, :] += dg

    blk = pl.BlockSpec((tt, 128), lambda i, h: (i, h))
    return _pc(body, name="gdn_post_bwd", out_shape=(_sds((T, 512)), _sds((T, DCAT_W), BF), _sds((8, 128))),
               grid=(T // tt, _H), in_specs=[blk, blk, pl.BlockSpec((1, 128), lambda i, h: (0, 0)), blk],
               out_specs=(blk, pl.BlockSpec((tt, 128), lambda i, h: (i, DCAT_Z // 128 + h)),
                          pl.BlockSpec((8, 128), lambda i, h: (0, 0))),
               sem=("arbitrary", "arbitrary"))(o, z, g, dmixin)


def _rms_fwd(sm, g, col_blk, name):
    T = sm.shape[0]
    d = g.shape[1]
    tt = _pick(T, (1024, 512, 256))

    def body(x_ref, g_ref, o_ref, ot_ref):
        y = _rmsnorm(x_ref[...], g_ref[...]).astype(o_ref.dtype)
        o_ref[...] = y
        ot_ref[...] = y.T

    return _pc(body, name=name, out_shape=(_sds((T, d), BF), _sds((d, T), BF)), grid=(T // tt,),
               in_specs=[pl.BlockSpec((tt, d), lambda i: (i, col_blk)), pl.BlockSpec((1, d), lambda i: (0, 0))],
               out_specs=(pl.BlockSpec((tt, d), lambda i: (i, 0)), pl.BlockSpec((d, tt), lambda i: (0, i))),
               sem=("parallel",))(sm, g)


def _rms_bwd(sm, g, dy, dsm, col_blk, name):
    T = sm.shape[0]
    d = g.shape[1]
    tt = _pick(T, (1024, 512, 256))

    def body(x_ref, g_ref, d_ref, alias_ref, dx_ref, dg_ref):
        @pl.when(pl.program_id(0) == 0)
        def _():
            dg_ref[...] = jnp.zeros_like(dg_ref)

        _, vjp = jax.vjp(_rmsnorm, x_ref[...], g_ref[...])
        dx, dg = vjp(d_ref[...])
        dx_ref[...] = dx.astype(dx_ref.dtype)
        dg_ref[0:1, :] += dg

    grp = pl.BlockSpec((tt, d), lambda i: (i, col_blk))
    return _pc(body, name=name, out_shape=(_sds(dsm.shape, dsm.dtype), _sds((8, d))), grid=(T // tt,),
               in_specs=[grp, pl.BlockSpec((1, d), lambda i: (0, 0)), pl.BlockSpec((tt, d), lambda i: (i, 0)), ANY],
               out_specs=(grp, pl.BlockSpec((8, d), lambda i: (0, 0))), aliases={3: 0},
               sem=("arbitrary",))(sm, g, dy, dsm)


def _rope_tables(S):
    inv = ROPE_THETA ** (-jnp.arange(0, MLA_ROPE, 2, dtype=F32) / MLA_ROPE)
    ang = jnp.arange(S, dtype=F32)[:, None] * inv[None, :]
    cos, sin = jnp.cos(ang), jnp.sin(ang)
    z = jnp.zeros((S, 64), F32)
    return jnp.concatenate([cos, cos, z], axis=1), jnp.concatenate([-sin, sin, z], axis=1)


def _swap_halves(x):
    lane = lax.broadcasted_iota(jnp.int32, x.shape, 1)
    return jnp.where(lane < 32, pltpu.roll(x, 96, 1), jnp.where(lane < 64, pltpu.roll(x, 32, 1), 0.0))


def _mm_rows(a, b, *, name, tm, extras, out_shape, out_specs, epi):
    (M, K), N = a.shape, b.shape[1]
    nx = len(extras)

    def body(a_ref, b_ref, *rest):
        r = lax.dot_general(a_ref[...].astype(_MXU), b_ref[...].astype(_MXU), _NN, preferred_element_type=F32)
        epi(r, rest[:nx], rest[nx:])

    in_specs = [pl.BlockSpec((tm, K), lambda i: (i, 0)), pl.BlockSpec((K, N), lambda i: (0, 0))]
    return _pc(body, name=name, out_shape=out_shape, grid=(M // tm,), in_specs=in_specs + [s for _, s in extras],
               out_specs=out_specs, sem=("parallel",))(a, b, *[x for x, _ in extras])


def _qup_rope(cqn, wq, cos, sin, S):
    T = cqn.shape[0]
    tm = _pick(S, (1024, 512, 256))
    nps = S // tm

    def epi(r, xs, outs):
        c, s = xs[0][...], xs[1][...]
        for h in range(MLA_HEADS):
            lo = 256 * h
            rp = r[:, lo + 128:lo + 256]
            outs[0][:, lo:lo + 128] = (r[:, lo:lo + 128] * ATTN_SCALE).astype(BF)
            outs[0][:, lo + 128:lo + 256] = ((rp * c + _swap_halves(rp) * s) * ATTN_SCALE).astype(BF)

    tab = pl.BlockSpec((tm, 128), lambda i: (i % nps, 0))
    return _mm_rows(cqn, wq, name="mm_qup_rope", tm=tm, extras=[(cos, tab), (sin, tab)],
                    out_shape=_sds((T, 1024), BF), out_specs=pl.BlockSpec((tm, 1024), lambda i: (i, 0)), epi=epi)


def _kvup_rope(ckvn, wkv, sm, cos, sin, S):
    T = ckvn.shape[0]
    tm = _pick(S, (1024, 512, 256))
    nps = S // tm

    def epi(r, xs, outs):
        k, c, s = xs[0][...], xs[1][...], xs[2][...]
        kr = (k * c + _swap_halves(k) * s).astype(BF)
        kvb = r.astype(BF)
        outs[0][...] = kvb
        for h in range(MLA_HEADS):
            lo = 256 * h
            outs[1][:, lo:lo + 128] = kvb[:, lo:lo + 128]
            outs[1][:, lo + 128:lo + 256] = kr

    tab = pl.BlockSpec((tm, 128), lambda i: (i % nps, 0))
    row = pl.BlockSpec((tm, 1024), lambda i: (i, 0))
    return _mm_rows(ckvn, wkv, name="mm_kvup_rope", tm=tm,
                    extras=[(sm, pl.BlockSpec((tm, 128), lambda i: (i, 3))), (cos, tab), (sin, tab)],
                    out_shape=(_sds((T, 1024), BF), _sds((T, 1024), BF)), out_specs=(row, row), epi=epi)


def _rope_bwd(dqc, dcat, dkrr, cos, sin, S):
    T = dqc.shape[0]
    tt = _pick(S, (2048, 1024, 512, 256))
    nps = S // tt

    def body(alias_s, dq_ref, dk_ref, c_ref, s_ref, qo_ref, ko_ref):
        c, s = c_ref[...], s_ref[...]
        d = dq_ref[:, 128:256]
        qo_ref[:, 0:128] = dq_ref[:, 0:128].astype(qo_ref.dtype)
        qo_ref[:, 128:256] = (d * c + _swap_halves(d * s)).astype(qo_ref.dtype)

        @pl.when(pl.program_id(1) == 0)
        def _():
            k = dk_ref[...]
            ko_ref[...] = (k * c + _swap_halves(k * s)).astype(ko_ref.dtype)

    tab = pl.BlockSpec((tt, 128), lambda i, h: (i % nps, 0))
    head = pl.BlockSpec((tt, 256), lambda i, h: (i, h))
    return _pc(body, name="rope_bwd", out_shape=(_sds((T, 1024), BF), _sds(dcat.shape, dcat.dtype)),
               grid=(T // tt, MLA_HEADS),
               in_specs=[ANY, head, pl.BlockSpec((tt, 128), lambda i, h: (i, 0)), tab, tab],
               out_specs=(head, pl.BlockSpec((tt, 128), lambda i, h: (i, 3))),
               aliases={0: 1}, sem=("parallel", "arbitrary"))(dcat, dqc, dkrr, cos, sin)


def _attn_scores(q, k_ref, L, tq):
    def sc(lo, hi):
        return _dot(q, k_ref[lo:hi, :], _NT)

    sd = sc(L - tq, L)
    qc = lax.broadcasted_iota(jnp.int32, sd.shape, 0) >> 6
    kc = lax.broadcasted_iota(jnp.int32, sd.shape, 1) >> 6
    sd = jnp.where(kc <= qc, sd, -jnp.inf)
    return sd if L == tq else jnp.concatenate([sc(0, L - tq), sd], axis=1)


def _attn_fwd(mixin, mixin_t, qcat, kcat, kv, Bl, S):
    T = Bl * S
    tq = _pick(S, (256, 128))
    nq = S // tq

    def body(alias_ref, alias_t, q_ref, k_ref, v_ref, o_ref, lse_ref, ot_ref):
        i = pl.program_id(2)
        q = q_ref[...]
        for ii in range(nq):
            @pl.when(i == ii)
            def _(L=(ii + 1) * tq):
                s = _attn_scores(q, k_ref, L, tq)
                m = jnp.max(s, axis=-1, keepdims=True)
                e = jnp.exp(s - m)
                l = jnp.sum(e, axis=-1, keepdims=True)
                o = (_dot(e, v_ref[0:L, :], _NN) / l).astype(o_ref.dtype)
                o_ref[...] = o
                ot_ref[...] = o.T
                lse_ref[...] = jnp.broadcast_to(m + jnp.log(l), lse_ref.shape)

    qrow = lambda b, h, i: b * nq + i
    in_specs = [ANY, ANY,
                pl.BlockSpec((tq, 256), lambda b, h, i: (qrow(b, h, i), h)),
                pl.BlockSpec((S, 256), lambda b, h, i: (b, h)),
                pl.BlockSpec((S, 128), lambda b, h, i: (b, 2 * h + 1))]
    return _pc(body, name="mla_attn_fwd",
               out_shape=(_sds(mixin.shape, mixin.dtype), _sds((T, 512)), _sds(mixin_t.shape, mixin_t.dtype)),
               grid=(Bl, MLA_HEADS, nq), in_specs=in_specs,
               out_specs=(pl.BlockSpec((tq, 128), lambda b, h, i: (qrow(b, h, i), 4 + h)),
                          pl.BlockSpec((tq, 128), lambda b, h, i: (qrow(b, h, i), h)),
                          pl.BlockSpec((128, tq), lambda b, h, i: (4 + h, qrow(b, h, i)))),
               aliases={0: 0, 1: 2}, sem=("parallel", "parallel", "parallel"))(mixin, mixin_t, qcat, kcat, kv)


def _attn_bwd(qcat, kcat, kv, lse, dmixin, Bl, S):
    T = Bl * S
    tq = _pick(S, (256, 128))
    nq = S // tq

    def body(q_ref, k_ref, v_ref, do_ref, lse_ref, dq_ref, dkv_ref, dkr_ref, acck_ref, accv_ref):
        h, i = pl.program_id(1), pl.program_id(2)
        q = q_ref[...]
        do32 = do_ref[...]
        do = do32.astype(BF)
        lse_col = lse_ref[:, 0:1]

        @pl.when(i == 0)
        def _():
            acck_ref[...] = jnp.zeros_like(acck_ref)
            accv_ref[...] = jnp.zeros_like(accv_ref)

        for ii in range(nq):
            @pl.when(i == ii)
            def _(L=(ii + 1) * tq):
                p = jnp.exp(_attn_scores(q, k_ref, L, tq) - lse_col)
                pb = p.astype(BF)
                v = v_ref[0:L, :]
                delta = jnp.sum(do32 * _dot(pb, v, _NN), axis=-1, keepdims=True)
                ds = (p * (_dot(do, v, _NT) - delta)).astype(BF)
                dq_ref[...] = _dot(ds, k_ref[0:L, :], _NN) * ATTN_SCALE
                acck_ref[0:L, :] += _dot(ds, q, _TN)
                accv_ref[0:L, :] += _dot(pb, do, _TN)

        @pl.when(i == nq - 1)
        def _():
            dkv_ref[:, 0:128] = acck_ref[:, 0:128].astype(dkv_ref.dtype)
            dkv_ref[:, 128:256] = accv_ref[...].astype(dkv_ref.dtype)
            dkr_ref[...] = jnp.where(h == 0, 0.0, dkr_ref[...]) + acck_ref[:, 128:256]

    qrow = lambda b, h, i: b * nq + i
    in_specs = [pl.BlockSpec((tq, 256), lambda b, h, i: (qrow(b, h, i), h)),
                pl.BlockSpec((S, 256), lambda b, h, i: (b, h)),
                pl.BlockSpec((S, 128), lambda b, h, i: (b, 2 * h + 1)),
                pl.BlockSpec((tq, 128), lambda b, h, i: (qrow(b, h, i), 4 + h)),
                pl.BlockSpec((tq, 128), lambda b, h, i: (qrow(b, h, i), h))]
    out_shape = (_sds((T, 1024)), _sds((T, 1024), BF), _sds((T, 128)))
    out_specs = (pl.BlockSpec((tq, 256), lambda b, h, i: (qrow(b, h, i), h)),
                 pl.BlockSpec((S, 256), lambda b, h, i: (b, h)),
                 pl.BlockSpec((S, 128), lambda b, h, i: (b, 0)))
    return _pc(body, name="mla_attn_bwd", out_shape=out_shape, grid=(Bl, MLA_HEADS, nq), in_specs=in_specs,
               out_specs=out_specs, scratch=[pltpu.VMEM((S, 256), F32), pltpu.VMEM((S, 128), F32)],
               sem=("arbitrary", "arbitrary", "arbitrary"))(qcat, kcat, kv, dmixin, lse)


def _out_ln1(mixin, w_out, x, g, b):
    T, D = x.shape
    tm = _pick(T, (512, 256))

    def epi(r, xs, outs):
        h = _layernorm(ALPHA * xs[0][...] + r, xs[1][...], xs[2][...])
        outs[0][...] = r
        outs[1][...] = h
        hb = h.astype(BF)
        outs[2][...] = hb
        outs[3][...] = hb.T

    row = pl.BlockSpec((tm, D), lambda i: (i, 0))
    par = pl.BlockSpec((1, D), lambda i: (0, 0))
    return _mm_rows(mixin, w_out, name="mm_out_ln1", tm=tm, extras=[(x, row), (g, par), (b, par)],
                    out_shape=(_sds((T, D)), _sds((T, D)), _sds((T, D), BF), _sds((D, T), BF)),
                    out_specs=(row, row, row, pl.BlockSpec((D, tm), lambda i: (0, i))), epi=epi)


def _ln1_bwd(x, mix, g, b, dh):
    T, D = x.shape
    tt = _pick(T, (256,))

    def body(x_ref, m_ref, g_ref, b_ref, d_ref, dx_ref, dm_ref, acc_ref):
        @pl.when(pl.program_id(0) == 0)
        def _():
            acc_ref[...] = jnp.zeros_like(acc_ref)

        f = lambda xx, mm, gg, bb: _layernorm(ALPHA * xx + mm, gg, bb)
        _, vjp = jax.vjp(f, x_ref[...], m_ref[...], g_ref[...], b_ref[...])
        dx, dm, dg, db = vjp(d_ref[...])
        dx_ref[...] = dx
        dm_ref[...] = dm.astype(dm_ref.dtype)
        acc_ref[0:1, :] += dg
        acc_ref[1:2, :] += db

    blk = pl.BlockSpec((tt, D), lambda i: (i, 0))
    par = pl.BlockSpec((1, D), lambda i: (0, 0))
    return _pc(body, name="ln1_bwd", out_shape=(_sds((T, D)), _sds((T, D), BF), _sds((8, D))), grid=(T // tt,),
               in_specs=[blk, blk, par, par, blk], out_specs=(blk, blk, pl.BlockSpec((8, D), lambda i: (0, 0))),
               sem=("arbitrary",))(x, mix, g, b, dh)


def _ffn_act_fwd(ug, uu, wg, wu, bg, bu, Bl, S):
    T, C = ug.shape
    cb = _pick(C, (256, 128))
    kk = wg.shape[0]

    def body(g_ref, u_ref, wg_ref, wu_ref, bg_ref, bu_ref, o_ref, ot_ref):
        cg = _conv(g_ref[...], wg_ref[...]) + bg_ref[...]
        cu = _conv(u_ref[...], wu_ref[...]) + bu_ref[...]
        a = (_silu(cg) * cu).astype(o_ref.dtype)
        o_ref[...] = a
        ot_ref[...] = a.T

    blk = pl.BlockSpec((S, cb), lambda b, j: (b, j))
    wblk = pl.BlockSpec((kk, cb), lambda b, j: (0, j))
    bblk = pl.BlockSpec((1, cb), lambda b, j: (0, j))
    return _pc(body, name="ffn_act_fwd", out_shape=(_sds((T, C), BF), _sds((C, T), BF)), grid=(Bl, C // cb),
               in_specs=[blk, blk, wblk, wblk, bblk, bblk],
               out_specs=(blk, pl.BlockSpec((cb, S), lambda b, j: (j, b))),
               sem=("parallel", "parallel"))(ug, uu, wg, wu, bg, bu)


def _ffn_act_bwd(ug, uu, wg, wu, bg, bu, dact, Bl, S):
    T, C = ug.shape
    cb = _pick(C, (256, 128))
    kk = wg.shape[0]

    def body(g_ref, u_ref, wg_ref, wu_ref, bg_ref, bu_ref, d_ref,
             dg_ref, du_ref, dwg_ref, dwu_ref, dbg_ref, dbu_ref):
        first = pl.program_id(1) == 0
        wgv, wuv = wg_ref[...], wu_ref[...]
        tg, tu = _taps(g_ref[...], kk), _taps(u_ref[...], kk)
        cg = _conv(None, wgv, tg) + bg_ref[...]
        cu = _conv(None, wuv, tu) + bu_ref[...]
        _, vjp = jax.vjp(lambda a, b: _silu(a) * b, cg, cu)
        dcg, dcu = vjp(d_ref[...].astype(F32))
        dg_ref[...] = _conv_bwd_x(dcg, wgv).astype(dg_ref.dtype)
        du_ref[...] = _conv_bwd_x(dcu, wuv).astype(du_ref.dtype)
        _conv_bwd_w(tg, dcg, dwg_ref, first)
        _conv_bwd_w(tu, dcu, dwu_ref, first)
        dbg_ref[...] = jnp.where(first, 0.0, dbg_ref[...]) + jnp.sum(dcg, axis=0, keepdims=True)
        dbu_ref[...] = jnp.where(first, 0.0, dbu_ref[...]) + jnp.sum(dcu, axis=0, keepdims=True)

    blk = pl.BlockSpec((S, cb), lambda j, b: (b, j))
    wblk = pl.BlockSpec((kk, cb), lambda j, b: (0, j))
    bblk = pl.BlockSpec((1, cb), lambda j, b: (0, j))
    out_shape = (_sds((T, C), BF), _sds((T, C), BF), _sds((kk, C)), _sds((kk, C)), _sds((1, C)), _sds((1, C)))
    return _pc(body, name="ffn_act_bwd", out_shape=out_shape, grid=(C // cb, Bl),
               in_specs=[blk, blk, wblk, wblk, bblk, bblk, blk],
               out_specs=(blk, blk, wblk, wblk, bblk, bblk), sem=("parallel", "arbitrary"))(
        ug, uu, wg, wu, bg, bu, dact)


def _head(h1, ffn, gpre, pp, tgt, bg, g2, b2):
    T, D = h1.shape
    tt = _pick(T, (256,))

    def body(h_ref, f_ref, gp_ref, pp_ref, t_ref, bg_ref, g2_ref, b2_ref, df_ref, dfb_ref, dgp_ref, dpp_ref, acc_ref):
        @pl.when(pl.program_id(0) == 0)
        def _():
            acc_ref[...] = jnp.zeros_like(acc_ref)

        h, tg = h_ref[...], t_ref[...]

        def loss_fn(f, gp, p_, bgv, g2v, b2v):
            pre = ALPHA * h + f + jax.nn.sigmoid(gp + bgv) * p_
            err = _layernorm(pre, g2v, b2v) - tg
            return 0.5 * jnp.sum(jnp.mean(err * err, axis=-1, keepdims=True))

        loss, grads = jax.value_and_grad(loss_fn, argnums=(0, 1, 2, 3, 4, 5))(
            f_ref[...], gp_ref[...], pp_ref[...], bg_ref[...], g2_ref[...], b2_ref[...])
        df_ref[...] = grads[0]
        dfb_ref[...] = grads[0].astype(dfb_ref.dtype)
        dgp_ref[...] = grads[1].astype(dgp_ref.dtype)
        dpp_ref[...] = grads[2].astype(dpp_ref.dtype)
        acc_ref[0:1, :] += grads[3]
        acc_ref[1:2, :] += grads[4]
        acc_ref[2:3, :] += grads[5]
        acc_ref[3:4, :] += jnp.broadcast_to(loss, (1, D))

    blk = pl.BlockSpec((tt, D), lambda i: (i, 0))
    par = pl.BlockSpec((1, D), lambda i: (0, 0))
    o_bf = _sds((T, D), BF)
    return _pc(body, name="loss_head", out_shape=(_sds((T, D)), o_bf, o_bf, o_bf, _sds((8, D))),
               grid=(T // tt,), in_specs=[blk, blk, blk, blk, blk, par, par, par],
               out_specs=(blk, blk, blk, blk, pl.BlockSpec((8, D), lambda i: (0, 0))), sem=("arbitrary",))(
        h1, ffn, gpre, pp, tgt, bg, g2, b2)


def _adamw(parts, w, m, v, name):
    R, C = w.shape
    tr = R if R <= 512 else _pick(R, (256,))

    def body(p_ref, w_ref, m_ref, v_ref, g_ref, d_ref, nm_ref, nv_ref):
        g = p_ref[0].astype(F32)
        for j in range(1, N_DEV):
            g = g + p_ref[j].astype(F32)
        g_ref[...] = g
        d_ref[...], nm_ref[...], nv_ref[...] = _adam_math(g, w_ref[...], m_ref[...], v_ref[...])

    blk = pl.BlockSpec((tr, C), lambda i: (i, 0))
    o = _sds((R, C))
    return _pc(body, name=name, out_shape=(o, o, o, o), grid=(R // tr,),
               in_specs=[pl.BlockSpec((N_DEV, tr, C), lambda i: (0, i, 0)), blk, blk, blk],
               out_specs=(blk, blk, blk, blk), sem=("parallel",))(parts, w, m, v)


def _adam_math(g, w, m, v):
    mm = ADAM_B1 * m + (1.0 - ADAM_B1) * g
    vv = ADAM_B2 * v + (1.0 - ADAM_B2) * jnp.square(g)
    m_hat = mm / (1.0 - ADAM_B1 ** ADAM_STEP)
    v_hat = vv / (1.0 - ADAM_B2 ** ADAM_STEP)
    return -ADAM_LR * (m_hat / (jnp.sqrt(v_hat) + ADAM_EPS) + ADAM_WD * w), mm, vv


def _small_layout(sizes):
    offs, off = {}, 0
    for n in SMALL_ORDER:
        offs[n] = off
        off += -(-sizes[n] // 128) * 128
    return offs, off


def _pack_small(d):
    return jnp.concatenate([_pad_lanes(d[n].reshape(1, -1), -(-d[n].size // 128) * 128) for n in SMALL_ORDER], axis=1)


def _adamw_small(parts, ws, ms, vs):
    k = len(SMALL_ORDER)
    sizes = {n: ws[n].shape[1] for n in SMALL_ORDER}
    offs, _ = _small_layout(sizes)

    def body(p_ref, *refs):
        ins, outs = refs[:3 * k], refs[3 * k:]
        for i, n in enumerate(SMALL_ORDER):
            lo, hi = offs[n], offs[n] + sizes[n]
            g = p_ref[0, :, lo:hi]
            for j in range(1, N_DEV):
                g = g + p_ref[j, :, lo:hi]
            d, mm, vv = _adam_math(g, ins[i][...], ins[k + i][...], ins[2 * k + i][...])
            outs[4 * i][...], outs[4 * i + 1][...], outs[4 * i + 2][...], outs[4 * i + 3][...] = g, d, mm, vv

    out_shape = tuple(_sds((1, sizes[n])) for n in SMALL_ORDER for _ in range(4))
    args = [ws[n] for n in SMALL_ORDER] + [ms[n] for n in SMALL_ORDER] + [vs[n] for n in SMALL_ORDER]
    res = _pc(body, name="adamw_small", out_shape=out_shape)(parts, *args)
    return {n: tuple(res[4 * i:4 * i + 4]) for i, n in enumerate(SMALL_ORDER)}


def _prep(x, p, xchg=None):
    T, D = x.shape
    Dp = p.shape[1]
    tt = _pick(T, (512, 256))

    def body(x_ref, p_ref, xb_ref, xt_ref, pb_ref, pt_ref):
        xb = x_ref[...].astype(BF)
        xb_ref[...] = xb
        xt_ref[...] = xb.T
        pb = p_ref[...].astype(BF)
        pb_ref[...] = pb
        pt_ref[...] = pb.T

    row = lambda w: pl.BlockSpec((tt, w), lambda i: (i, 0))
    col = lambda w: pl.BlockSpec((w, tt), lambda i: (0, i))
    return _pc(body, name="prep_inputs",
               out_shape=(_sds((T, D), BF), _sds((D, T), BF), _sds((T, Dp), BF), _sds((Dp, T), BF)),
               grid=(T // tt,), in_specs=[row(D), row(Dp)], out_specs=(row(D), col(D), row(Dp), col(Dp)),
               sem=("parallel",), xchg=xchg)(x, p)


SMALL_ORDER = ("gdn_a_log", "gdn_dt_bias", "gdn_norm_g", "mla_q_norm_g", "mla_kv_norm_g", "ln1_g", "ln1_b",
               "ffn_conv_b", "ple_b_gate", "ln2_g", "ln2_b")
EARLY = ("w_in", "gdn_conv_w")
LATE = ("mla_w_q_up", "mla_w_kv_up", "w_out", "ffn_w_up", "ffn_conv_w", "ffn_w_down", "ple_w_gate", "ple_w_proj")
GRADS_EARLY = ("w_out", "ffn_w_up", "ffn_conv_w", "ffn_w_down", "ple_w_gate", "ple_w_proj")
GRADS_LATE = ("w_in", "gdn_conv_w", "mla_w_q_up", "mla_w_kv_up")


def _pad_lanes(v, n=128):
    return jnp.pad(v, ((0, 0), (0, n - v.shape[1])))


def _local_step(x, p, tgt, W, sp, Bl, S, early_weights=None, late_weights=None, early_grads=None, late_grads=None):
    T = Bl * S
    W = dict(W)
    prep = _prep(x, p, xchg=None if early_weights is None else early_weights[:2])
    xb, xT, pb, pT = prep[:4]
    if early_weights is not None:
        W.update(early_weights[2](prep[4:]))
    w_in = W["w_in"]
    wqkv, wz = w_in[:, :1536], w_in[:, 1536:2048]
    z64 = jnp.zeros((w_in.shape[0], 64), w_in.dtype)
    z124 = jnp.zeros((w_in.shape[0], 124), w_in.dtype)
    wsm = jnp.concatenate([w_in[:, 2056:2440], w_in[:, 2696:2760], z64, w_in[:, 2440:2696],
                           w_in[:, 2048:2052], z124, w_in[:, 2052:2056], z124], axis=1)
    alog, dtb = _pad_lanes(sp["gdn_a_log"]), _pad_lanes(sp["gdn_dt_bias"])
    cos, sin = _rope_tables(S)

    qkv = _mm(xb, wqkv, "nn", name="mm_qkv")
    z = _mm(xb, wz, "nn", name="mm_z")
    sm = _mm(xb, wsm, "nn", name="mm_sm")
    qkvn = _gdn_pre_fwd(qkv, W["gdn_conv_w"], Bl, S)
    gdn_out = _gdn_fwd(qkvn, sm, alog, dtb, Bl, S, xchg=None if late_weights is None else late_weights[:2])
    o, tinv_s, sst_s, w_s, u_s, gc_s, beta_s = gdn_out[:7]
    if late_weights is not None:
        W.update(late_weights[2](gdn_out[7:]))
    wq = jnp.pad(W["mla_w_q_up"].reshape(-1, MLA_HEADS, 192), ((0, 0), (0, 0), (0, 64))).reshape(-1, 1024)
    wkv = W["mla_w_kv_up"]
    C = W["ffn_w_down"].shape[0]
    wup_g, wup_u = W["ffn_w_up"][:, :C], W["ffn_w_up"][:, C:]
    cw_g, cw_u = W["ffn_conv_w"][:, :C], W["ffn_conv_w"][:, C:]
    cb_g, cb_u = sp["ffn_conv_b"][:, :C], sp["ffn_conv_b"][:, C:]
    mixin, mixin_t = _gdn_post(o, z, sp["gdn_norm_g"], T)
    cqn, cqn_t = _rms_fwd(sm, sp["mla_q_norm_g"], 0, "rms_q_fwd")
    ckvn, ckvn_t = _rms_fwd(sm, sp["mla_kv_norm_g"], 2, "rms_kv_fwd")
    qcat = _qup_rope(cqn, wq, cos, sin, S)
    kv, kcat = _kvup_rope(ckvn, wkv, sm, cos, sin, S)
    mixin, lse, mixin_t = _attn_fwd(mixin, mixin_t, qcat, kcat, kv, Bl, S)
    mix, h1, h1b, h1T = _out_ln1(mixin, W["w_out"], x, sp["ln1_g"], sp["ln1_b"])
    ug = _mm(h1b, wup_g, "nn", name="mm_up_gate")
    uu = _mm(h1b, wup_u, "nn", name="mm_up_up")
    act, act_t = _ffn_act_fwd(ug, uu, cw_g, cw_u, cb_g, cb_u, Bl, S)
    ffn = _mm(act, W["ffn_w_down"], "nn", name="mm_down")
    gpre = _mm(h1b, W["ple_w_gate"], "nn", name="mm_gate")
    pp = _mm(pb, W["ple_w_proj"], "nn", name="mm_proj")

    dffn, dffnb, dgpre, dpp, hacc = _head(h1, ffn, gpre, pp, tgt, sp["ple_b_gate"], sp["ln2_g"], sp["ln2_b"])
    loss = hacc[3, 0]
    gW, gs = {}, {}
    gs["ple_b_gate"], gs["ln2_g"], gs["ln2_b"] = hacc[0:1], hacc[1:2], hacc[2:3]
    gW["ple_w_proj"] = _mm(pT, dpp, "nn", name="mm_dproj", out_dtype=BF)
    gW["ple_w_gate"] = _mm(h1T, dgpre, "nn", name="mm_dgate", out_dtype=BF)
    gW["ffn_w_down"] = _mm(act_t, dffnb, "nn", name="mm_ddown", out_dtype=BF)
    dact = _mm(dffnb, W["ffn_w_down"], "nt", name="mm_dact")
    du_g, du_u, dcw_g, dcw_u, dcb_g, dcb_u = _ffn_act_bwd(ug, uu, cw_g, cw_u, cb_g, cb_u, dact, Bl, S)
    gW["ffn_conv_w"] = jnp.concatenate([dcw_g, dcw_u], axis=1)
    gs["ffn_conv_b"] = jnp.concatenate([dcb_g, dcb_u], axis=1)
    gW["ffn_w_up"] = jnp.concatenate([_mm(h1T, du_g, "nn", name="mm_dup_gate", out_dtype=BF),
                                      _mm(h1T, du_u, "nn", name="mm_dup_up", out_dtype=BF)], axis=1)
    dh1 = _mm(dgpre, W["ple_w_gate"], "nt", name="mm_dh1_gate", add=dffn, add_scale=ALPHA)
    dh1 = _mm(du_g, wup_g, "nt", name="mm_dh1_upg", add=dh1)
    dh1 = _mm(du_u, wup_u, "nt", name="mm_dh1_upu", add=dh1)
    dxa, dmix, acc1 = _ln1_bwd(x, mix, sp["ln1_g"], sp["ln1_b"], dh1)
    gs["ln1_g"], gs["ln1_b"] = acc1[0:1], acc1[1:2]
    gW["w_out"] = _mm(mixin_t, dmix, "nn", name="mm_dwout", out_dtype=BF)
    dmixin = _mm(dmix, W["w_out"], "nt", name="mm_dmixin")
    do, dcat, gacc = _gdn_post_bwd(o, z, sp["gdn_norm_g"], dmixin, T)
    gs["gdn_norm_g"] = gacc[0:1]
    bwd_out = _gdn_bwd(qkvn, sm, alog, dtb, gc_s, beta_s, tinv_s, sst_s, w_s, u_s, do, dcat, Bl, S,
                       xchg=None if early_grads is None else early_grads({n: gW[n] for n in GRADS_EARLY}))
    dqkvn, dcat, cacc = bwd_out[:3]
    early_recv = bwd_out[3:]
    gs["gdn_a_log"], gs["gdn_dt_bias"] = cacc[0:1, :GDN_HEADS], cacc[1:2, :GDN_HEADS]
    dcat, gW["gdn_conv_w"] = _gdn_pre_bwd(qkv, W["gdn_conv_w"], dqkvn, dcat, Bl, S)
    dqc, dkv, dkrr = _attn_bwd(qcat, kcat, kv, lse, dmixin, Bl, S)
    dqraw, dcat = _rope_bwd(dqc, dcat, dkrr, cos, sin, S)
    gwq = _mm(cqn_t, dqraw, "nn", name="mm_dwq", out_dtype=BF)
    gW["mla_w_q_up"] = gwq.reshape(-1, MLA_HEADS, 256)[:, :, :192].reshape(-1, MLA_HEADS * 192)
    gW["mla_w_kv_up"] = _mm(ckvn_t, dkv, "nn", name="mm_dwkv", out_dtype=BF)
    dcqn = _mm(dqraw, wq, "nt", name="mm_dcqn")
    dckvn = _mm(dkv, wkv, "nt", name="mm_dckvn")
    dcat, qacc = _rms_bwd(sm, sp["mla_q_norm_g"], dcqn, dcat, 0, "rms_q_bwd")
    dcat, kacc = _rms_bwd(sm, sp["mla_kv_norm_g"], dckvn, dcat, 2, "rms_kv_bwd")
    gs["mla_q_norm_g"], gs["mla_kv_norm_g"] = qacc[0:1], kacc[0:1]
    gcat = _mm(xT, dcat, "nn", name="mm_dwin", out_dtype=BF)
    gsm, gqkv, gz = gcat[:, :DCAT_QKV], gcat[:, DCAT_QKV:DCAT_Z], gcat[:, DCAT_Z:]
    gW["w_in"] = jnp.concatenate([gqkv, gz, gsm[:, 768:772], gsm[:, 896:900], gsm[:, 0:384], gsm[:, 512:768],
                                  gsm[:, 384:448]], axis=1)
    wcat = jnp.concatenate([wsm, wqkv, wz], axis=1)
    dx = _mm(dcat, wcat, "nt", name="mm_dx", add=dxa, xchg=None if late_grads is None else late_grads(gW, gs))
    late_recv = ()
    if late_grads is not None:
        dx, late_recv = dx[0], dx[1:]
    return loss, dx, gW, gs, early_recv, late_recv


COL_SHARDED = ("w_in", "mla_w_q_up", "mla_w_kv_up", "ffn_w_up", "ple_w_proj", "gdn_conv_w", "ffn_conv_w")
SHARDED = EARLY + LATE
WEIGHTS = ("w_in", "gdn_conv_w", "gdn_a_log", "gdn_dt_bias", "gdn_norm_g", "mla_q_norm_g", "mla_w_q_up",
           "mla_kv_norm_g", "mla_w_kv_up", "w_out", "ln1_g", "ln1_b", "ffn_w_up", "ffn_conv_w", "ffn_conv_b",
           "ffn_w_down", "ple_w_gate", "ple_b_gate", "ple_w_proj", "ln2_g", "ln2_b")
CONV = ("gdn_conv_w", "ffn_conv_w")


def _gathered_to_full(name, g):
    if name in COL_SHARDED:
        return jnp.transpose(g, (1, 0, 2)).reshape(g.shape[1], -1)
    return g.reshape(-1, g.shape[-1])


def _full_to_blocks(name, gfull, shard_shape):
    r, c = shard_shape
    if name in COL_SHARDED:
        return jnp.transpose(gfull.reshape(r, N_DEV, c), (1, 0, 2))
    return gfull.reshape(N_DEV, r, c)


def kernel(x, p, w_in, gdn_conv_w, gdn_a_log, gdn_dt_bias, gdn_norm_g, mla_q_norm_g, mla_w_q_up, mla_kv_norm_g, mla_w_kv_up, w_out, ln1_g, ln1_b, ffn_w_up, ffn_conv_w, ffn_conv_b, ffn_w_down, ple_w_gate, ple_b_gate, ple_w_proj, ln2_g, ln2_b, loss_target, m_w_in, m_gdn_conv_w, m_gdn_a_log, m_gdn_dt_bias, m_gdn_norm_g, m_mla_q_norm_g, m_mla_w_q_up, m_mla_kv_norm_g, m_mla_w_kv_up, m_w_out, m_ln1_g, m_ln1_b, m_ffn_w_up, m_ffn_conv_w, m_ffn_conv_b, m_ffn_w_down, m_ple_w_gate, m_ple_b_gate, m_ple_w_proj, m_ln2_g, m_ln2_b, v_w_in, v_gdn_conv_w, v_gdn_a_log, v_gdn_dt_bias, v_gdn_norm_g, v_mla_q_norm_g, v_mla_w_q_up, v_mla_kv_norm_g, v_mla_w_kv_up, v_w_out, v_ln1_g, v_ln1_b, v_ffn_w_up, v_ffn_conv_w, v_ffn_conv_b, v_ffn_w_down, v_ple_w_gate, v_ple_b_gate, v_ple_w_proj, v_ln2_g, v_ln2_b):
    loc = dict(locals())
    wts = {n: loc[n] for n in WEIGHTS}
    ms = {n: loc["m_" + n] for n in WEIGHTS}
    vs = {n: loc["v_" + n] for n in WEIGHTS}
    Bl, S, D = x.shape
    T = Bl * S

    wire = lambda n: wts[n][0] if n in CONV else wts[n][0].astype(BF)
    gather = lambda names, mode: ([wire(n) for n in names], [mode] * len(names),
                                  lambda res: {n: _gathered_to_full(n, g) for n, g in zip(names, res)})
    blocks = lambda g, names: [_full_to_blocks(n, g[n], wts[n].shape[1:]) for n in names]
    pack_early = lambda g: (blocks(g, GRADS_EARLY), ["a2a"] * len(GRADS_EARLY))
    pack_late = lambda g, gs: (blocks(g, GRADS_LATE) + [_pack_small(gs)], ["a2a"] * len(GRADS_LATE) + ["ag"])
    sp = {n: wts[n].reshape(1, -1) for n in SMALL_ORDER}

    loss, dx, gW, gs, early_recv, late_recv = _local_step(
        x.reshape(T, D), p.reshape(T, -1), loss_target.reshape(T, D), {}, sp, Bl, S,
        early_weights=gather(EARLY, "ag2"), late_weights=gather(LATE, "ag"), early_grads=pack_early,
        late_grads=pack_late)

    res = {}
    for n, parts in list(zip(GRADS_EARLY, early_recv)) + list(zip(GRADS_LATE, late_recv[:-1])):
        res[n] = tuple(t[None] for t in _adamw(parts, wts[n][0], ms[n][0], vs[n][0], "adamw_" + n))
    res.update(_adamw_small(late_recv[-1], wts, ms, vs))

    loss = lax.psum(loss, ("x", "y", "c"))
    outs = [loss, dx.reshape(Bl, S, D)]
    for k in range(4):
        outs += [res[n][k] for n in WEIGHTS]
    return tuple(outs)
```

```python
import math

import jax
import jax.numpy as jnp
from jax import lax
from jax.experimental import pallas as pl
from jax.experimental.pallas import tpu as pltpu

F32 = jnp.float32
BF = jnp.bfloat16
_MXU = jnp.bfloat16
_INTERPRET = None
_VMEM_LIMIT = 56 * 1024 * 1024
HI = lax.Precision.HIGHEST

N_DEV = 8
CHUNK = 64
GDN_HEADS = 4
GDN_DK = 128
MLA_HEADS = 4
MLA_NOPE = 128
MLA_ROPE = 64
ROPE_THETA = 10000.0
ALPHA = 2.0 ** 0.25
NORM_EPS = 1e-6
ATTN_SCALE = (MLA_NOPE + MLA_ROPE) ** -0.5
DCAT_QKV, DCAT_Z, DCAT_W = 1024, 2560, 3072
ADAM_LR, ADAM_B1, ADAM_B2, ADAM_EPS, ADAM_WD, ADAM_STEP = 0.001, 0.9, 0.999, 1e-08, 0.01, 10
MESH = pl.DeviceIdType.MESH
ANY = pl.BlockSpec(memory_space=pl.ANY)
_NN = (((1,), (0,)), ((), ()))
_NT = (((1,), (1,)), ((), ()))
_TN = (((0,), (0,)), ((), ()))


def _sds(shape, dtype=F32):
    return jax.ShapeDtypeStruct(tuple(shape), dtype)


def _pick(n, cands):
    for c in cands:
        if n % c == 0:
            return c
    return n


def _xchg_out_shapes(arrs, modes):
    return [_sds(a.shape if m == "a2a" else (N_DEV,) + a.shape, a.dtype) for a, m in zip(arrs, modes)]


def _xchg_scratch(n):
    return [pltpu.SemaphoreType.DMA((n, N_DEV - 1)), pltpu.SemaphoreType.DMA((n, N_DEV - 1)),
            pltpu.SemaphoreType.DMA((n,))]


def _xchg_plan(ins, outs, send, recv, loc, modes):
    x, y, c = lax.axis_index("x"), lax.axis_index("y"), lax.axis_index("c")
    me = 4 * x + 2 * y + c
    starts, relays, waits = [], [], []
    for ai, mode in enumerate(modes):
        src_all, out = ins[ai], outs[ai]

        def remote(k, src, dst, dev):
            return pltpu.make_async_remote_copy(src_ref=src, dst_ref=dst, send_sem=send.at[ai, k], recv_sem=recv.at[ai, k],
                                                device_id=dev, device_id_type=MESH)

        own = pltpu.make_async_copy(src_all.at[me] if mode == "a2a" else src_all, out.at[me], loc.at[ai])
        starts.append(own)
        waits.append(own.wait)
        if mode == "ag2":
            sib = (x, y, 1 - c)
            chips = [(1 - x, y), (x, 1 - y), (1 - x, 1 - y)]
            first = [remote(0, src_all, out.at[me], sib)]
            first += [remote(1 + k, src_all, out.at[me], (px, py, c)) for k, (px, py) in enumerate(chips)]
            starts += first
            waits += [cp.wait_send for cp in first]
            waits.append(remote(0, src_all, out.at[4 * x + 2 * y + 1 - c], sib).wait_recv)
            for k, (px, py) in enumerate(chips):
                same, other = out.at[4 * px + 2 * py + c], out.at[4 * px + 2 * py + 1 - c]
                relay = remote(4 + k, same, same, sib)
                relays.append((remote(1 + k, src_all, same, (px, py, c)), relay))
                waits += [relay.wait_send, remote(4 + k, src_all, other, sib).wait_recv]
            continue
        for r in range(1, N_DEV):
            px = 1 - x if (r >> 2) & 1 else x
            py = 1 - y if (r >> 1) & 1 else y
            pc = 1 - c if r & 1 else c
            cp = remote(r - 1, src_all.at[4 * px + 2 * py + pc] if mode == "a2a" else src_all, out.at[me], (px, py, pc))
            starts.append(cp)
            waits.append(cp.wait)
    return starts, relays, waits


def _pc(body, *, name, out_shape, grid=None, in_specs=None, out_specs=None, scratch=(), sem=None, aliases=None,
        xchg=None):
    kw = {}
    if _INTERPRET is not None:
        kw["interpret"] = _INTERPRET
    single = not isinstance(out_shape, (tuple, list))
    out_shape = [out_shape] if single else list(out_shape)
    if out_specs is not None:
        out_specs = [out_specs] if single else list(out_specs)
    scratch = list(scratch)
    if xchg is not None:
        xarrs, xmodes = xchg
        n_in, n_out, n_scr, nx = len(in_specs), len(out_shape), len(scratch), len(xarrs)
        single = False
        inner = body

        def body(*refs):
            ins, xins = refs[:n_in], refs[n_in:n_in + nx]
            outs = refs[n_in + nx:n_in + nx + n_out]
            xouts = refs[n_in + nx + n_out:n_in + 2 * nx + n_out]
            scr = refs[n_in + 2 * nx + n_out:n_in + 2 * nx + n_out + n_scr]
            send, recv, loc = refs[n_in + 2 * nx + n_out + n_scr:]
            first = last = None
            for d, g in enumerate(grid):
                f, l = pl.program_id(d) == 0, pl.program_id(d) == g - 1
                first, last = (f, l) if first is None else (first & f, last & l)

            @pl.when(first)
            def _():
                for cp in _xchg_plan(xins, xouts, send, recv, loc, xmodes)[0]:
                    cp.start()

            inner(*ins, *outs, *scr)

            @pl.when(last)
            def _():
                _, relays, waits = _xchg_plan(xins, xouts, send, recv, loc, xmodes)
                for arrival, relay in relays:
                    arrival.wait_recv()
                    relay.start()
                for wait in waits:
                    wait()

        in_specs = list(in_specs) + [ANY] * nx
        out_specs = out_specs + [ANY] * nx
        out_shape = out_shape + _xchg_out_shapes(xarrs, xmodes)
        scratch = scratch + _xchg_scratch(nx)
        sem = ("arbitrary",) * len(grid)
    if grid is not None:
        kw["grid"] = grid
    if in_specs is not None:
        kw["in_specs"] = in_specs
    if out_specs is not None:
        kw["out_specs"] = out_specs[0] if single else tuple(out_specs)
    cp = dict(vmem_limit_bytes=_VMEM_LIMIT)
    if sem is not None:
        cp["dimension_semantics"] = sem
    call = pl.pallas_call(body, name=name, out_shape=out_shape[0] if single else tuple(out_shape),
                          scratch_shapes=scratch, input_output_aliases=aliases or {},
                          compiler_params=pltpu.CompilerParams(**cp), **kw)
    if xchg is None:
        return call
    return lambda *ins: call(*ins, *xchg[0])


_FULL_K_MAX = 3072


def _mm(a, b, mode, *, name, add=None, add_scale=1.0, out_dtype=F32, xchg=None):
    (M, K) = a.shape
    (K2, N) = b.shape if mode == "nn" else b.shape[::-1]
    assert K == K2, (a.shape, b.shape, mode)
    tm = _pick(M, (1024, 1408, 512, 384, 256, 128))
    tn = _pick(N, (1024, 2816, 1408, 768, 512, 384, 256, 128))
    tk = K if K <= _FULL_K_MAX else _pick(K, (2048, 1408, 1024, 512) if tn <= 1408 else (1024, 512))
    nk = K // tk
    dims = _NN if mode == "nn" else _NT
    has_add = add is not None

    def finish(r, add_ref, o_ref):
        if has_add:
            r = r + add_scale * add_ref[...].astype(F32)
        o_ref[...] = r.astype(out_dtype)

    if nk == 1:
        def body(a_ref, b_ref, *rest):
            r = lax.dot_general(a_ref[...].astype(_MXU), b_ref[...].astype(_MXU), dims, preferred_element_type=F32)
            finish(r, rest[0] if has_add else None, rest[-1])

        a_spec = pl.BlockSpec((tm, K), lambda i, j: (i, 0))
        b_spec = (pl.BlockSpec((K, tn), lambda i, j: (0, j)) if mode == "nn"
                  else pl.BlockSpec((tn, K), lambda i, j: (j, 0)))
        o_spec = pl.BlockSpec((tm, tn), lambda i, j: (i, j))
        grid, sem, scratch = (M // tm, N // tn), ("parallel", "parallel"), []
    else:
        def body(a_ref, b_ref, *rest):
            o_ref, acc = rest[-2], rest[-1]
            k = pl.program_id(2)

            @pl.when(k == 0)
            def _():
                acc[...] = jnp.zeros_like(acc)

            acc[...] += lax.dot_general(a_ref[...].astype(_MXU), b_ref[...].astype(_MXU), dims,
                                        preferred_element_type=F32)

            @pl.when(k == nk - 1)
            def _():
                finish(acc[...], rest[0] if has_add else None, o_ref)

        a_spec = pl.BlockSpec((tm, tk), lambda i, j, k: (i, k))
        b_spec = (pl.BlockSpec((tk, tn), lambda i, j, k: (k, j)) if mode == "nn"
                  else pl.BlockSpec((tn, tk), lambda i, j, k: (j, k)))
        o_spec = pl.BlockSpec((tm, tn), lambda i, j, k: (i, j))
        grid, sem, scratch = (M // tm, N // tn, nk), ("parallel", "parallel", "arbitrary"), [pltpu.VMEM((tm, tn), F32)]

    ins = [a, b] + ([add] if has_add else [])
    specs = [a_spec, b_spec] + ([o_spec] if has_add else [])
    return _pc(body, name=name, out_shape=_sds((M, N), out_dtype), grid=grid, in_specs=specs, out_specs=o_spec,
               scratch=scratch, sem=sem, xchg=xchg)(*ins)


def _dot(a, b, dims):
    return lax.dot_general(a.astype(_MXU), b.astype(_MXU), dims, preferred_element_type=F32)


def _split(a):
    hi = a.astype(BF)
    lo = (a - hi.astype(F32)).astype(BF)
    return hi, lo


def _dot3(a, b, dims):
    ah, al = _split(a)
    bh, bl = _split(b)
    d = lambda u, v: lax.dot_general(u, v, dims, preferred_element_type=F32)
    return d(ah, bh) + (d(ah, bl) + d(al, bh))


def _softplus(x):
    return jnp.maximum(x, 0.0) + jnp.log1p(jnp.exp(-jnp.abs(x)))


def _silu(x):
    return x * jax.nn.sigmoid(x)


def _rmsnorm(x, g):
    return x * lax.rsqrt(jnp.mean(x * x, axis=-1, keepdims=True) + NORM_EPS) * g


def _layernorm(x, g, b):
    mu = jnp.mean(x, axis=-1, keepdims=True)
    xc = x - mu
    var = jnp.mean(xc * xc, axis=-1, keepdims=True)
    return xc * lax.rsqrt(var + NORM_EPS) * g + b


def _shift_dn(x, s):
    if s == 0:
        return x
    t = lax.broadcasted_iota(jnp.int32, x.shape, 0)
    return jnp.where(t >= s, pltpu.roll(x, s, 0), 0.0)


def _shift_up(x, s):
    if s == 0:
        return x
    n = x.shape[0]
    t = lax.broadcasted_iota(jnp.int32, x.shape, 0)
    return jnp.where(t < n - s, pltpu.roll(x, n - s, 0), 0.0)


def _taps(x, kk):
    return [_shift_dn(x, kk - 1 - j) for j in range(kk)]


def _conv(x, w, taps=None):
    kk = w.shape[0]
    taps = _taps(x, kk) if taps is None else taps
    y = w[kk - 1:kk, :] * taps[kk - 1]
    for j in range(kk - 1):
        y = y + w[j:j + 1, :] * taps[j]
    return y


def _conv_bwd_x(dy, w):
    kk = w.shape[0]
    dx = w[kk - 1:kk, :] * dy
    for j in range(kk - 1):
        dx = dx + w[j:j + 1, :] * _shift_up(dy, kk - 1 - j)
    return dx


def _conv_bwd_w(taps, dy, dw_ref, first):
    kk = dw_ref.shape[0]
    for j in range(kk):
        r = jnp.sum(dy * taps[j], axis=0, keepdims=True)
        prev = jnp.where(first, 0.0, dw_ref[j:j + 1, :])
        dw_ref[j:j + 1, :] = prev + r


def _gdn_post_conv(c, j):
    h = _silu(c)
    hn = h * lax.rsqrt(jnp.sum(h * h, axis=-1, keepdims=True) + NORM_EPS)
    return jnp.where(j < GDN_HEADS, hn * (GDN_DK ** -0.5), jnp.where(j < 2 * GDN_HEADS, hn, h))


def _gdn_pre_fwd(qkv, conv_w, Bl, S):
    T, C = qkv.shape
    nj = C // 128

    def body(x_ref, w_ref, o_ref):
        o_ref[...] = _gdn_post_conv(_conv(x_ref[...], w_ref[...]), pl.program_id(1))

    blk = pl.BlockSpec((S, 128), lambda b, j: (b, j))
    return _pc(body, name="gdn_pre_fwd", out_shape=_sds((T, C)), grid=(Bl, nj),
               in_specs=[blk, pl.BlockSpec((conv_w.shape[0], 128), lambda b, j: (0, j))], out_specs=blk,
               sem=("parallel", "parallel"))(qkv, conv_w)


def _gdn_pre_bwd(qkv, conv_w, dout, dcat, Bl, S):
    T, C = qkv.shape
    nj = C // 128
    kk = conv_w.shape[0]

    def body(x_ref, w_ref, d_ref, alias_ref, dx_ref, dw_ref):
        j, b = pl.program_id(0), pl.program_id(1)
        w = w_ref[...]
        taps = _taps(x_ref[...], kk)
        c = _conv(None, w, taps)
        _, vjp = jax.vjp(lambda u: _gdn_post_conv(u, j), c)
        (dc,) = vjp(d_ref[...])
        dx_ref[...] = _conv_bwd_x(dc, w).astype(dx_ref.dtype)
        _conv_bwd_w(taps, dc, dw_ref, b == 0)

    blk = pl.BlockSpec((S, 128), lambda j, b: (b, j))
    wblk = pl.BlockSpec((kk, 128), lambda j, b: (0, j))
    return _pc(body, name="gdn_pre_bwd", out_shape=(_sds(dcat.shape, dcat.dtype), _sds((kk, C))), grid=(nj, Bl),
               in_specs=[blk, wblk, blk, ANY],
               out_specs=(pl.BlockSpec((S, 128), lambda j, b: (b, DCAT_QKV // 128 + j)), wblk),
               aliases={3: 0}, sem=("parallel", "arbitrary"))(qkv, conv_w, dout, dcat)


_H = GDN_HEADS
_HC = _H * CHUNK


def _st(x):
    return jnp.concatenate([x[:, h * 128:(h + 1) * 128] for h in range(_H)], axis=0)


def _unst(x):
    return jnp.concatenate([x[h * CHUNK:(h + 1) * CHUNK] for h in range(_H)], axis=1)


def _stc(t):
    return jnp.concatenate([t[:, h:h + 1] for h in range(_H)], axis=0)


def _untile(col):
    lane = lax.broadcasted_iota(jnp.int32, (CHUNK, 128), 1)
    out = jnp.zeros((CHUNK, 128), F32)
    for h in range(_H):
        out = out + jnp.where(lane == h, col[h * CHUNK:(h + 1) * CHUNK], 0.0)
    return out


def _rowform(col):
    return jnp.broadcast_to(col, (_HC, 128)).T[0:1, :]


def _tri(n, upper=False):
    i = lax.broadcasted_iota(jnp.int32, (n, n), 0)
    j = lax.broadcasted_iota(jnp.int32, (n, n), 1)
    return jnp.where((j >= i) if upper else (j <= i), 1.0, 0.0).astype(F32)


def _gdn_chunk_common(q, k, v, gc, beta):
    Q, K, V = _st(q), _st(k), _st(v)
    B, GC = _stc(beta), _stc(gc)
    GL = jnp.concatenate([jnp.broadcast_to(gc[CHUNK - 1:CHUNK, h:h + 1], (CHUNK, 1)) for h in range(_H)], axis=0)
    ii = lax.broadcasted_iota(jnp.int32, (_HC, _HC), 0)
    jj = lax.broadcasted_iota(jnp.int32, (_HC, _HC), 1)
    same = (ii >> 6) == (jj >> 6)
    incl = same & (ii >= jj)
    strict = same & (ii > jj)
    diff = GC - _rowform(GC)
    D = jnp.where(incl, jnp.exp(jnp.where(incl, diff, 0.0)), 0.0)
    KB = K * B
    A = jnp.where(strict, _dot(KB, K, _NT) * D, 0.0)
    P = jnp.where(incl, _dot(Q, K, _NT) * D, 0.0)
    EG = jnp.exp(GC)
    ED = jnp.exp(GL - GC)
    return dict(Q=Q, K=K, V=V, B=B, GC=GC, GL=GL, incl=incl, strict=strict, D=D, KB=KB, A=A, P=P, EG=EG, ED=ED,
                QG=Q * EG, KD=K * ED)


def _neumann_inv(A):
    n = A.shape[0]
    i = lax.broadcasted_iota(jnp.int32, (n, n), 0)
    j = lax.broadcasted_iota(jnp.int32, (n, n), 1)
    N = -A
    acc = jnp.where(i == j, 1.0, 0.0) + N
    Pw = N
    for _ in range(5):
        Pw = _dot3(Pw, Pw, _NN)
        acc = acc + _dot3(acc, Pw, _NN)
    return acc


def _gates(a, alog, dtb):
    return -jnp.exp(alog) * _softplus(a + dtb)


def _hs(x, h, n=CHUNK):
    return x[h * n:(h + 1) * n]


def _per_step(Bl):
    return 4 if Bl % 4 == 0 else 2 if Bl % 2 == 0 else 1


def _gdn_fwd(qkvn, sm, alog, dtb, Bl, S, xchg=None):
    T = Bl * S
    nC = S // CHUNK
    E = _per_step(Bl)

    def body(*refs):
        S_ref = refs[-1]

        @pl.when(pl.program_id(1) == 0)
        def _():
            S_ref[...] = jnp.zeros_like(S_ref)

        for e in range(E):
            one(*[r if i in (5, 6) else r.at[e] for i, r in enumerate(refs)])

    def one(q_ref, k_ref, v_ref, a_ref, b_ref, alog_ref, dtb_ref,
            o_ref, tinv_ref, sst_ref, w_ref, u_ref, gc_ref, beta_ref, S_ref):
        g = _gates(a_ref[...], alog_ref[...], dtb_ref[...])
        beta = jax.nn.sigmoid(b_ref[...])
        gc = jnp.dot(_tri(CHUNK), g, precision=HI, preferred_element_type=F32)
        cm = _gdn_chunk_common(q_ref[...], k_ref[...], v_ref[...], gc, beta)
        tinv = _neumann_inv(cm["A"])
        WU = _dot3(tinv, jnp.concatenate([cm["KB"] * cm["EG"], cm["V"] * cm["B"]], axis=1), _NN)
        W, U = WU[:, :128], WU[:, 128:]
        s_old = [S_ref[h * 128:(h + 1) * 128, :] for h in range(_H)]
        vn = [_hs(U, h) - _dot(_hs(W, h), s_old[h], _NN) for h in range(_H)]
        o_intra = _dot(cm["P"], jnp.concatenate(vn, axis=0), _NN)
        outs = []
        for h in range(_H):
            outs.append(_dot(_hs(cm["QG"], h), s_old[h], _NN) + _hs(o_intra, h))
            gl = jnp.exp(gc[CHUNK - 1:CHUNK, h:h + 1])
            S_ref[h * 128:(h + 1) * 128, :] = gl * s_old[h] + _dot(_hs(cm["KD"], h), vn[h], _TN)
            sst_ref[h * 128:(h + 1) * 128, :] = s_old[h]
        o_ref[...] = jnp.concatenate(outs, axis=1)
        tinv_ref[...] = tinv
        w_ref[...] = _unst(W)
        u_ref[...] = _unst(U)
        gc_ref[...] = gc
        beta_ref[...] = beta

    blk = lambda r, w, c: pl.BlockSpec((E, r, w), lambda b, n: (b, n, c))
    par = pl.BlockSpec((1, 128), lambda b, n: (0, 0))
    out_shape = (_sds((Bl, S, 512)), _sds((Bl, nC * _HC, _HC)), _sds((Bl, nC * 512, 128)),
                 _sds((Bl, S, 512)), _sds((Bl, S, 512)), _sds((Bl, S, 128)), _sds((Bl, S, 128)))
    out_specs = (blk(CHUNK, 512, 0), blk(_HC, _HC, 0), blk(512, 128, 0), blk(CHUNK, 512, 0), blk(CHUNK, 512, 0),
                 blk(CHUNK, 128, 0), blk(CHUNK, 128, 0))
    q3, s3 = qkvn.reshape(Bl, S, -1), sm.reshape(Bl, S, -1)
    res = _pc(body, name="gdn_chunk_fwd", out_shape=out_shape, grid=(Bl // E, nC),
              in_specs=[blk(CHUNK, 512, 0), blk(CHUNK, 512, 1), blk(CHUNK, 512, 2), blk(CHUNK, 128, 6),
                        blk(CHUNK, 128, 7), par, par],
              out_specs=out_specs, scratch=[pltpu.VMEM((E, 512, 128), F32)], sem=("parallel", "arbitrary"),
              xchg=xchg)(q3, q3, q3, s3, s3, alog, dtb)
    return (res[0].reshape(T, 512),) + tuple(res[1:])


def _gdn_bwd(qkvn, sm, alog, dtb, gc_s, beta_s, tinv_s, sst_s, w_s, u_s, do, dcat, Bl, S, xchg=None):
    T = Bl * S
    nC = S // CHUNK
    E = _per_step(Bl)

    def body(*refs):
        acc_ref, dS_ref = refs[-2], refs[-1]
        first_chunk = pl.program_id(1) == 0

        @pl.when(first_chunk)
        def _():
            dS_ref[...] = jnp.zeros_like(dS_ref)

        @pl.when(first_chunk & (pl.program_id(0) == 0))
        def _():
            acc_ref[...] = jnp.zeros_like(acc_ref)

        for e in range(E):
            one(*[r if i in (4, 5, 13, 16) else r.at[e] for i, r in enumerate(refs)])

    def one(q_ref, k_ref, v_ref, a_ref, alog_ref, dtb_ref, gc_ref, beta_ref, tinv_ref, sst_ref, w_ref, u_ref,
            do_ref, alias_ref, dqkv_ref, dsm_ref, acc_ref, dS_ref):
        gc, beta = gc_ref[...], beta_ref[...]
        cm = _gdn_chunk_common(q_ref[...], k_ref[...], v_ref[...], gc, beta)
        Q, K, V, B, D, KB, A, P = (cm[n] for n in ("Q", "K", "V", "B", "D", "KB", "A", "P"))
        EG, ED, QG, KD = cm["EG"], cm["ED"], cm["QG"], cm["KD"]
        tinv = tinv_ref[...]
        W, U, DO = _st(w_ref[...]), _st(u_ref[...]), _st(do_ref[...])
        s_old = [sst_ref[h * 128:(h + 1) * 128, :] for h in range(_H)]
        ds_new = [dS_ref[h * 128:(h + 1) * 128, :] for h in range(_H)]
        VN = jnp.concatenate([_hs(U, h) - _dot(_hs(W, h), s_old[h], _NN) for h in range(_H)], axis=0)
        dP = jnp.where(cm["incl"], _dot(DO, VN, _NT), 0.0)
        dVN0 = _dot(P, DO, _TN)
        dVN, dQG, dKD, dW, TL = [], [], [], [], []
        for h in range(_H):
            gl = jnp.exp(gc[CHUNK - 1:CHUNK, h:h + 1])
            dvn = _hs(dVN0, h) + _dot(_hs(KD, h), ds_new[h], _NN)
            dkd = _dot(_hs(VN, h), ds_new[h], _NT)
            dVN.append(dvn)
            dQG.append(_dot(_hs(DO, h), s_old[h], _NT))
            dKD.append(dkd)
            dW.append(-_dot(dvn, s_old[h], _NT))
            dgl = jnp.sum(jnp.sum(ds_new[h] * s_old[h], axis=1, keepdims=True), axis=0, keepdims=True)
            tl = jnp.sum(jnp.sum(dkd * _hs(KD, h), axis=1, keepdims=True), axis=0, keepdims=True) + dgl * gl
            TL.append(jnp.broadcast_to(tl, (CHUNK, 1)))
            dS_ref[h * 128:(h + 1) * 128, :] = (_dot(_hs(QG, h), _hs(DO, h), _TN) + gl * ds_new[h]
                                                - _dot(_hs(W, h), dvn, _TN))
        dVN, dQG, dKD, dW, TL = (jnp.concatenate(z, axis=0) for z in (dVN, dQG, dKD, dW, TL))
        dB = _dot3(tinv, jnp.concatenate([dVN, dW], axis=1), _TN)
        dVB, dKBE = dB[:, :128], dB[:, 128:]
        dA = jnp.where(cm["strict"], -_dot3(dB, jnp.concatenate([U, W], axis=1), _NT), 0.0)
        dG = dA * D
        dQK = dP * D
        dKB = _dot(dG, K, _NN) + dKBE * EG
        dK = (_dot(jnp.concatenate([dG, dQK], axis=0), jnp.concatenate([KB, Q], axis=0), _TN)
              + dKB * B + dKD * ED)
        dQ = _dot(dQK, K, _NN) + dQG * EG
        Mx = dA * A + dP * P
        rs = lambda z: jnp.sum(z, axis=1, keepdims=True)
        ri = lax.broadcasted_iota(jnp.int32, (_HC, 1), 0)
        dGC = (rs(Mx) - rs(Mx.T) + rs(dKBE * KB * EG) + rs(dQG * QG) - rs(dKD * KD)
               + jnp.where((ri & (CHUNK - 1)) == CHUNK - 1, TL, 0.0))
        dBeta = rs(dVB * V) + rs(dKB * K)
        dqkv_ref[:, 0:512] = _unst(dQ)
        dqkv_ref[:, 512:1024] = _unst(dK)
        dqkv_ref[:, 1024:1536] = _unst(dVB * B)
        dg = jnp.dot(_tri(CHUNK, upper=True), _untile(dGC), precision=HI, preferred_element_type=F32)
        a, alog_v, dtb_v = a_ref[...], alog_ref[...], dtb_ref[...]
        g = _gates(a, alog_v, dtb_v)
        lane = lax.broadcasted_iota(jnp.int32, (CHUNK, 128), 1)
        valid = lane < _H
        da = jnp.where(valid, dg * (-jnp.exp(alog_v)) * jax.nn.sigmoid(a + dtb_v), 0.0)
        dsm_ref[:, 0:128] = da.astype(dsm_ref.dtype)
        dsm_ref[:, 128:256] = jnp.where(valid, _untile(dBeta) * beta * (1.0 - beta), 0.0).astype(dsm_ref.dtype)
        acc_ref[0:1, :] += jnp.sum(jnp.where(valid, dg * g, 0.0), axis=0, keepdims=True)
        acc_ref[1:2, :] += jnp.sum(da, axis=0, keepdims=True)

    blk = lambda r, w, c: pl.BlockSpec((E, r, w), lambda b, n: (b, nC - 1 - n, c))
    par = pl.BlockSpec((1, 128), lambda b, n: (0, 0))
    in_specs = [blk(CHUNK, 512, 0), blk(CHUNK, 512, 1), blk(CHUNK, 512, 2), blk(CHUNK, 128, 6), par, par,
                blk(CHUNK, 128, 0), blk(CHUNK, 128, 0), blk(_HC, _HC, 0), blk(512, 128, 0),
                blk(CHUNK, 512, 0), blk(CHUNK, 512, 0), blk(CHUNK, 512, 0), ANY]
    out_shape = (_sds((Bl, S, 1536)), _sds((Bl, S, DCAT_W), BF), _sds((8, 128)))
    out_specs = (blk(CHUNK, 1536, 0), blk(CHUNK, 256, 3), pl.BlockSpec((8, 128), lambda b, n: (0, 0)))
    q3, s3 = qkvn.reshape(Bl, S, -1), sm.reshape(Bl, S, -1)
    res = _pc(body, name="gdn_chunk_bwd", out_shape=out_shape, grid=(Bl // E, nC), in_specs=in_specs,
              out_specs=out_specs, scratch=[pltpu.VMEM((E, 512, 128), F32)], sem=("arbitrary", "arbitrary"),
              aliases={13: 1}, xchg=xchg)(q3, q3, q3, s3, alog, dtb, gc_s, beta_s, tinv_s, sst_s, w_s, u_s,
                                          do.reshape(Bl, S, -1), dcat.reshape(Bl, S, DCAT_W))
    return (res[0].reshape(T, 1536), res[1].reshape(T, DCAT_W)) + tuple(res[2:])


def _gdn_post(o, z, g, T):
    tt = _pick(T, (2048, 1024, 512, 256))

    def body(o_ref, z_ref, g_ref, out_ref, outt_ref):
        y = (_rmsnorm(o_ref[...], g_ref[...]) * _silu(z_ref[...])).astype(out_ref.dtype)
        out_ref[...] = y
        outt_ref[...] = y.T

    blk = pl.BlockSpec((tt, 128), lambda i, h: (i, h))
    return _pc(body, name="gdn_post_fwd", out_shape=(_sds((T, 1024), BF), _sds((1024, T), BF)), grid=(T // tt, _H),
               in_specs=[blk, blk, pl.BlockSpec((1, 128), lambda i, h: (0, 0))],
               out_specs=(blk, pl.BlockSpec((128, tt), lambda i, h: (h, i))),
               sem=("parallel", "parallel"))(o, z, g)


def _gdn_post_bwd(o, z, g, dmixin, T):
    tt = _pick(T, (2048, 1024, 512, 256))

    def body(o_ref, z_ref, g_ref, d_ref, do_ref, dz_ref, dg_ref):
        @pl.when((pl.program_id(0) == 0) & (pl.program_id(1) == 0))
        def _():
            dg_ref[...] = jnp.zeros_like(dg_ref)

        _, vjp = jax.vjp(lambda a, b, c: _rmsnorm(a, c) * _silu(b), o_ref[...], z_ref[...], g_ref[...])
        do, dz, dg = vjp(d_ref[...])
        do_ref[...] = do
        dz_ref[...] = dz.astype(dz_ref.dtype)
        dg_ref[0:1, :] += dg

    blk = pl.BlockSpec((tt, 128), lambda i, h: (i, h))
    return _pc(body, name="gdn_post_bwd", out_shape=(_sds((T, 512)), _sds((T, DCAT_W), BF), _sds((8, 128))),
               grid=(T // tt, _H), in_specs=[blk, blk, pl.BlockSpec((1, 128), lambda i, h: (0, 0)), blk],
               out_specs=(blk, pl.BlockSpec((tt, 128), lambda i, h: (i, DCAT_Z // 128 + h)),
                          pl.BlockSpec((8, 128), lambda i, h: (0, 0))),
               sem=("arbitrary", "arbitrary"))(o, z, g, dmixin)


def _rms_fwd(sm, g, col_blk, name):
    T = sm.shape[0]
    d = g.shape[1]
    tt = _pick(T, (1024, 512, 256))

    def body(x_ref, g_ref, o_ref, ot_ref):
        y = _rmsnorm(x_ref[...], g_ref[...]).astype(o_ref.dtype)
        o_ref[...] = y
        ot_ref[...] = y.T

    return _pc(body, name=name, out_shape=(_sds((T, d), BF), _sds((d, T), BF)), grid=(T // tt,),
               in_specs=[pl.BlockSpec((tt, d), lambda i: (i, col_blk)), pl.BlockSpec((1, d), lambda i: (0, 0))],
               out_specs=(pl.BlockSpec((tt, d), lambda i: (i, 0)), pl.BlockSpec((d, tt), lambda i: (0, i))),
               sem=("parallel",))(sm, g)


def _rms_bwd(sm, g, dy, dsm, col_blk, name):
    T = sm.shape[0]
    d = g.shape[1]
    tt = _pick(T, (1024, 512, 256))

    def body(x_ref, g_ref, d_ref, alias_ref, dx_ref, dg_ref):
        @pl.when(pl.program_id(0) == 0)
        def _():
            dg_ref[...] = jnp.zeros_like(dg_ref)

        _, vjp = jax.vjp(_rmsnorm, x_ref[...], g_ref[...])
        dx, dg = vjp(d_ref[...])
        dx_ref[...] = dx.astype(dx_ref.dtype)
        dg_ref[0:1, :] += dg

    grp = pl.BlockSpec((tt, d), lambda i: (i, col_blk))
    return _pc(body, name=name, out_shape=(_sds(dsm.shape, dsm.dtype), _sds((8, d))), grid=(T // tt,),
               in_specs=[grp, pl.BlockSpec((1, d), lambda i: (0, 0)), pl.BlockSpec((tt, d), lambda i: (i, 0)), ANY],
               out_specs=(grp, pl.BlockSpec((8, d), lambda i: (0, 0))), aliases={3: 0},
               sem=("arbitrary",))(sm, g, dy, dsm)


def _rope_tables(S):
    inv = ROPE_THETA ** (-jnp.arange(0, MLA_ROPE, 2, dtype=F32) / MLA_ROPE)
    ang = jnp.arange(S, dtype=F32)[:, None] * inv[None, :]
    cos, sin = jnp.cos(ang), jnp.sin(ang)
    z = jnp.zeros((S, 64), F32)
    return jnp.concatenate([cos, cos, z], axis=1), jnp.concatenate([-sin, sin, z], axis=1)


def _swap_halves(x):
    lane = lax.broadcasted_iota(jnp.int32, x.shape, 1)
    return jnp.where(lane < 32, pltpu.roll(x, 96, 1), jnp.where(lane < 64, pltpu.roll(x, 32, 1), 0.0))


def _mm_rows(a, b, *, name, tm, extras, out_shape, out_specs, epi, mode="nn", sem="parallel"):
    M, K = a.shape
    nx = len(extras)

    def body(a_ref, b_ref, *rest):
        r = lax.dot_general(a_ref[...].astype(_MXU), b_ref[...].astype(_MXU), _NN if mode == "nn" else _NT,
                            preferred_element_type=F32)
        epi(r, rest[:nx], rest[nx:])

    in_specs = [pl.BlockSpec((tm, K), lambda i: (i, 0)), pl.BlockSpec(b.shape, lambda i: (0, 0))]
    return _pc(body, name=name, out_shape=out_shape, grid=(M // tm,), in_specs=in_specs + [s for _, s in extras],
               out_specs=out_specs, sem=(sem,))(a, b, *[x for x, _ in extras])


def _qup_rope(cqn, wq, cos, sin, S):
    T = cqn.shape[0]
    tm = _pick(S, (1024, 512, 256))
    nps = S // tm

    def epi(r, xs, outs):
        c, s = xs[0][...], xs[1][...]
        for h in range(MLA_HEADS):
            lo = 256 * h
            rp = r[:, lo + 128:lo + 256]
            outs[0][:, lo:lo + 128] = (r[:, lo:lo + 128] * ATTN_SCALE).astype(BF)
            outs[0][:, lo + 128:lo + 256] = ((rp * c + _swap_halves(rp) * s) * ATTN_SCALE).astype(BF)

    tab = pl.BlockSpec((tm, 128), lambda i: (i % nps, 0))
    return _mm_rows(cqn, wq, name="mm_qup_rope", tm=tm, extras=[(cos, tab), (sin, tab)],
                    out_shape=_sds((T, 1024), BF), out_specs=pl.BlockSpec((tm, 1024), lambda i: (i, 0)), epi=epi)


def _kvup_rope(ckvn, wkv, sm, cos, sin, S):
    T = ckvn.shape[0]
    tm = _pick(S, (1024, 512, 256))
    nps = S // tm

    def epi(r, xs, outs):
        k, c, s = xs[0][...], xs[1][...], xs[2][...]
        kr = (k * c + _swap_halves(k) * s).astype(BF)
        kvb = r.astype(BF)
        outs[0][...] = kvb
        for h in range(MLA_HEADS):
            lo = 256 * h
            outs[1][:, lo:lo + 128] = kvb[:, lo:lo + 128]
            outs[1][:, lo + 128:lo + 256] = kr

    tab = pl.BlockSpec((tm, 128), lambda i: (i % nps, 0))
    row = pl.BlockSpec((tm, 1024), lambda i: (i, 0))
    return _mm_rows(ckvn, wkv, name="mm_kvup_rope", tm=tm,
                    extras=[(sm, pl.BlockSpec((tm, 128), lambda i: (i, 3))), (cos, tab), (sin, tab)],
                    out_shape=(_sds((T, 1024), BF), _sds((T, 1024), BF)), out_specs=(row, row), epi=epi)


def _rope_bwd(dqc, dcat, dkrr, cos, sin, S):
    T = dqc.shape[0]
    tt = _pick(S, (2048, 1024, 512, 256))
    nps = S // tt

    def body(alias_s, dq_ref, dk_ref, c_ref, s_ref, qo_ref, ko_ref):
        c, s = c_ref[...], s_ref[...]
        d = dq_ref[:, 128:256]
        qo_ref[:, 0:128] = dq_ref[:, 0:128].astype(qo_ref.dtype)
        qo_ref[:, 128:256] = (d * c + _swap_halves(d * s)).astype(qo_ref.dtype)

        @pl.when(pl.program_id(1) == 0)
        def _():
            k = dk_ref[...]
            ko_ref[...] = (k * c + _swap_halves(k * s)).astype(ko_ref.dtype)

    tab = pl.BlockSpec((tt, 128), lambda i, h: (i % nps, 0))
    head = pl.BlockSpec((tt, 256), lambda i, h: (i, h))
    return _pc(body, name="rope_bwd", out_shape=(_sds((T, 1024), BF), _sds(dcat.shape, dcat.dtype)),
               grid=(T // tt, MLA_HEADS),
               in_specs=[ANY, head, pl.BlockSpec((tt, 128), lambda i, h: (i, 0)), tab, tab],
               out_specs=(head, pl.BlockSpec((tt, 128), lambda i, h: (i, 3))),
               aliases={0: 1}, sem=("parallel", "arbitrary"))(dcat, dqc, dkrr, cos, sin)


def _attn_scores(q, k_ref, L, tq):
    def sc(lo, hi):
        return _dot(q, k_ref[lo:hi, :], _NT)

    sd = sc(L - tq, L)
    qc = lax.broadcasted_iota(jnp.int32, sd.shape, 0) >> 6
    kc = lax.broadcasted_iota(jnp.int32, sd.shape, 1) >> 6
    sd = jnp.where(kc <= qc, sd, -jnp.inf)
    return sd if L == tq else jnp.concatenate([sc(0, L - tq), sd], axis=1)


def _attn_fwd(mixin, mixin_t, qcat, kcat, kv, Bl, S):
    T = Bl * S
    tq = _pick(S, (256, 128))
    nq = S // tq

    def body(alias_ref, alias_t, q_ref, k_ref, v_ref, o_ref, lse_ref, ot_ref):
        i = pl.program_id(2)
        q = q_ref[...]
        for ii in range(nq):
            @pl.when(i == ii)
            def _(L=(ii + 1) * tq):
                s = _attn_scores(q, k_ref, L, tq)
                m = jnp.max(s, axis=-1, keepdims=True)
                e = jnp.exp(s - m)
                l = jnp.sum(e, axis=-1, keepdims=True)
                o = (_dot(e, v_ref[0:L, :], _NN) / l).astype(o_ref.dtype)
                o_ref[...] = o
                ot_ref[...] = o.T
                lse_ref[...] = jnp.broadcast_to(m + jnp.log(l), lse_ref.shape)

    qrow = lambda b, h, i: b * nq + i
    in_specs = [ANY, ANY,
                pl.BlockSpec((tq, 256), lambda b, h, i: (qrow(b, h, i), h)),
                pl.BlockSpec((S, 256), lambda b, h, i: (b, h)),
                pl.BlockSpec((S, 128), lambda b, h, i: (b, 2 * h + 1))]
    return _pc(body, name="mla_attn_fwd",
               out_shape=(_sds(mixin.shape, mixin.dtype), _sds((T, 512)), _sds(mixin_t.shape, mixin_t.dtype)),
               grid=(Bl, MLA_HEADS, nq), in_specs=in_specs,
               out_specs=(pl.BlockSpec((tq, 128), lambda b, h, i: (qrow(b, h, i), 4 + h)),
                          pl.BlockSpec((tq, 128), lambda b, h, i: (qrow(b, h, i), h)),
                          pl.BlockSpec((128, tq), lambda b, h, i: (4 + h, qrow(b, h, i)))),
               aliases={0: 0, 1: 2}, sem=("parallel", "parallel", "parallel"))(mixin, mixin_t, qcat, kcat, kv)


def _attn_bwd(qcat, kcat, kv, lse, dmixin, Bl, S):
    T = Bl * S
    tq = _pick(S, (256, 128))
    nq = S // tq

    def body(q_ref, k_ref, v_ref, do_ref, lse_ref, dq_ref, dkv_ref, dkr_ref, acck_ref, accv_ref):
        h, i = pl.program_id(1), pl.program_id(2)
        q = q_ref[...]
        do32 = do_ref[...]
        do = do32.astype(BF)
        lse_col = lse_ref[:, 0:1]

        @pl.when(i == 0)
        def _():
            acck_ref[...] = jnp.zeros_like(acck_ref)
            accv_ref[...] = jnp.zeros_like(accv_ref)

        for ii in range(nq):
            @pl.when(i == ii)
            def _(L=(ii + 1) * tq):
                p = jnp.exp(_attn_scores(q, k_ref, L, tq) - lse_col)
                pb = p.astype(BF)
                v = v_ref[0:L, :]
                delta = jnp.sum(do32 * _dot(pb, v, _NN), axis=-1, keepdims=True)
                ds = (p * (_dot(do, v, _NT) - delta)).astype(BF)
                dq_ref[...] = _dot(ds, k_ref[0:L, :], _NN) * ATTN_SCALE
                acck_ref[0:L, :] += _dot(ds, q, _TN)
                accv_ref[0:L, :] += _dot(pb, do, _TN)

        @pl.when(i == nq - 1)
        def _():
            dkv_ref[:, 0:128] = acck_ref[:, 0:128].astype(dkv_ref.dtype)
            dkv_ref[:, 128:256] = accv_ref[...].astype(dkv_ref.dtype)
            dkr_ref[...] = jnp.where(h == 0, 0.0, dkr_ref[...]) + acck_ref[:, 128:256]

    qrow = lambda b, h, i: b * nq + i
    in_specs = [pl.BlockSpec((tq, 256), lambda b, h, i: (qrow(b, h, i), h)),
                pl.BlockSpec((S, 256), lambda b, h, i: (b, h)),
                pl.BlockSpec((S, 128), lambda b, h, i: (b, 2 * h + 1)),
                pl.BlockSpec((tq, 128), lambda b, h, i: (qrow(b, h, i), 4 + h)),
                pl.BlockSpec((tq, 128), lambda b, h, i: (qrow(b, h, i), h))]
    out_shape = (_sds((T, 1024)), _sds((T, 1024), BF), _sds((T, 128)))
    out_specs = (pl.BlockSpec((tq, 256), lambda b, h, i: (qrow(b, h, i), h)),
                 pl.BlockSpec((S, 256), lambda b, h, i: (b, h)),
                 pl.BlockSpec((S, 128), lambda b, h, i: (b, 0)))
    return _pc(body, name="mla_attn_bwd", out_shape=out_shape, grid=(Bl, MLA_HEADS, nq), in_specs=in_specs,
               out_specs=out_specs, scratch=[pltpu.VMEM((S, 256), F32), pltpu.VMEM((S, 128), F32)],
               sem=("arbitrary", "arbitrary", "arbitrary"))(qcat, kcat, kv, dmixin, lse)


def _out_ln1(mixin, w_out, x, g, b):
    T, D = x.shape
    tm = _pick(T, (512, 256))

    def epi(r, xs, outs):
        h = _layernorm(ALPHA * xs[0][...] + r, xs[1][...], xs[2][...])
        outs[0][...] = r
        outs[1][...] = h
        hb = h.astype(BF)
        outs[2][...] = hb
        outs[3][...] = hb.T

    row = pl.BlockSpec((tm, D), lambda i: (i, 0))
    par = pl.BlockSpec((1, D), lambda i: (0, 0))
    return _mm_rows(mixin, w_out, name="mm_out_ln1", tm=tm, extras=[(x, row), (g, par), (b, par)],
                    out_shape=(_sds((T, D)), _sds((T, D)), _sds((T, D), BF), _sds((D, T), BF)),
                    out_specs=(row, row, row, pl.BlockSpec((D, tm), lambda i: (0, i))), epi=epi)


def _dh1_ln1_bwd(du, w, dh_part, x, mix, g, b):
    T, D = x.shape
    tm = _pick(T, (256,))

    def epi(r, xs, outs):
        acc_ref = outs[2]

        @pl.when(pl.program_id(0) == 0)
        def _():
            acc_ref[...] = jnp.zeros_like(acc_ref)

        f = lambda xx, mm, gg, bb: _layernorm(ALPHA * xx + mm, gg, bb)
        _, vjp = jax.vjp(f, xs[1][...], xs[2][...], xs[3][...], xs[4][...])
        dx, dm, dg, db = vjp(r + xs[0][...])
        outs[0][...] = dx
        outs[1][...] = dm.astype(BF)
        acc_ref[0:1, :] += dg
        acc_ref[1:2, :] += db

    row = pl.BlockSpec((tm, D), lambda i: (i, 0))
    par = pl.BlockSpec((1, D), lambda i: (0, 0))
    return _mm_rows(du, w, name="mm_dh1_ln1_bwd", tm=tm, mode="nt", sem="arbitrary",
                    extras=[(dh_part, row), (x, row), (mix, row), (g, par), (b, par)],
                    out_shape=(_sds((T, D)), _sds((T, D), BF), _sds((8, D))),
                    out_specs=(row, row, pl.BlockSpec((8, D), lambda i: (0, 0))), epi=epi)


def _ffn_act_fwd(ug, uu, wg, wu, bg, bu, Bl, S):
    T, C = ug.shape
    cb = _pick(C, (256, 128))
    kk = wg.shape[0]

    def body(g_ref, u_ref, wg_ref, wu_ref, bg_ref, bu_ref, o_ref, ot_ref):
        cg = _conv(g_ref[...], wg_ref[...]) + bg_ref[...]
        cu = _conv(u_ref[...], wu_ref[...]) + bu_ref[...]
        a = (_silu(cg) * cu).astype(o_ref.dtype)
        o_ref[...] = a
        ot_ref[...] = a.T

    blk = pl.BlockSpec((S, cb), lambda b, j: (b, j))
    wblk = pl.BlockSpec((kk, cb), lambda b, j: (0, j))
    bblk = pl.BlockSpec((1, cb), lambda b, j: (0, j))
    return _pc(body, name="ffn_act_fwd", out_shape=(_sds((T, C), BF), _sds((C, T), BF)), grid=(Bl, C // cb),
               in_specs=[blk, blk, wblk, wblk, bblk, bblk],
               out_specs=(blk, pl.BlockSpec((cb, S), lambda b, j: (j, b))),
               sem=("parallel", "parallel"))(ug, uu, wg, wu, bg, bu)


def _ffn_act_bwd(ug, uu, wg, wu, bg, bu, dact, Bl, S):
    T, C = ug.shape
    cb = _pick(C, (256, 128))
    kk = wg.shape[0]

    def body(g_ref, u_ref, wg_ref, wu_ref, bg_ref, bu_ref, d_ref,
             dg_ref, du_ref, dwg_ref, dwu_ref, dbg_ref, dbu_ref):
        first = pl.program_id(1) == 0
        wgv, wuv = wg_ref[...], wu_ref[...]
        tg, tu = _taps(g_ref[...], kk), _taps(u_ref[...], kk)
        cg = _conv(None, wgv, tg) + bg_ref[...]
        cu = _conv(None, wuv, tu) + bu_ref[...]
        _, vjp = jax.vjp(lambda a, b: _silu(a) * b, cg, cu)
        dcg, dcu = vjp(d_ref[...].astype(F32))
        dg_ref[...] = _conv_bwd_x(dcg, wgv).astype(dg_ref.dtype)
        du_ref[...] = _conv_bwd_x(dcu, wuv).astype(du_ref.dtype)
        _conv_bwd_w(tg, dcg, dwg_ref, first)
        _conv_bwd_w(tu, dcu, dwu_ref, first)
        dbg_ref[...] = jnp.where(first, 0.0, dbg_ref[...]) + jnp.sum(dcg, axis=0, keepdims=True)
        dbu_ref[...] = jnp.where(first, 0.0, dbu_ref[...]) + jnp.sum(dcu, axis=0, keepdims=True)

    blk = pl.BlockSpec((S, cb), lambda j, b: (b, j))
    wblk = pl.BlockSpec((kk, cb), lambda j, b: (0, j))
    bblk = pl.BlockSpec((1, cb), lambda j, b: (0, j))
    out_shape = (_sds((T, C), BF), _sds((T, C), BF), _sds((kk, C)), _sds((kk, C)), _sds((1, C)), _sds((1, C)))
    return _pc(body, name="ffn_act_bwd", out_shape=out_shape, grid=(C // cb, Bl),
               in_specs=[blk, blk, wblk, wblk, bblk, bblk, blk],
               out_specs=(blk, blk, wblk, wblk, bblk, bblk), sem=("parallel", "arbitrary"))(
        ug, uu, wg, wu, bg, bu, dact)


def _head(act, w_down, h1, h1b, w_gate, pb, w_proj, tgt, bg, g2, b2):
    T, D = h1.shape
    tm = _pick(T, (256,))

    def epi(ffn, xs, outs):
        h_ref, hb_ref, wg_ref, pb_ref, wp_ref, t_ref, bg_ref, g2_ref, b2_ref = xs
        acc_ref = outs[4]

        @pl.when(pl.program_id(0) == 0)
        def _():
            acc_ref[...] = jnp.zeros_like(acc_ref)

        h, tg = h_ref[...], t_ref[...]
        gpre = _dot(hb_ref[...], wg_ref[...], _NN)
        pp = _dot(pb_ref[...], wp_ref[...], _NN)

        def loss_fn(f, gp, p_, bgv, g2v, b2v):
            pre = ALPHA * h + f + jax.nn.sigmoid(gp + bgv) * p_
            err = _layernorm(pre, g2v, b2v) - tg
            return 0.5 * jnp.sum(jnp.mean(err * err, axis=-1, keepdims=True))

        loss, grads = jax.value_and_grad(loss_fn, argnums=(0, 1, 2, 3, 4, 5))(
            ffn, gpre, pp, bg_ref[...], g2_ref[...], b2_ref[...])
        outs[0][...] = grads[0]
        outs[1][...] = grads[0].astype(BF)
        outs[2][...] = grads[1].astype(BF)
        outs[3][...] = grads[2].astype(BF)
        acc_ref[0:1, :] += grads[3]
        acc_ref[1:2, :] += grads[4]
        acc_ref[2:3, :] += grads[5]
        acc_ref[3:4, :] += jnp.broadcast_to(loss, (1, D))

    row = pl.BlockSpec((tm, D), lambda i: (i, 0))
    par = pl.BlockSpec((1, D), lambda i: (0, 0))
    whole = lambda a: pl.BlockSpec(a.shape, lambda i: (0, 0))
    o_bf = _sds((T, D), BF)
    extras = [(h1, row), (h1b, row), (w_gate, whole(w_gate)), (pb, pl.BlockSpec((tm, pb.shape[1]), lambda i: (i, 0))),
              (w_proj, whole(w_proj)), (tgt, row), (bg, par), (g2, par), (b2, par)]
    return _mm_rows(act, w_down, name="mm_down_loss_head", tm=tm, sem="arbitrary", extras=extras,
                    out_shape=(_sds((T, D)), o_bf, o_bf, o_bf, _sds((8, D))),
                    out_specs=(row, row, row, row, pl.BlockSpec((8, D), lambda i: (0, 0))), epi=epi)


def _adamw(parts, w, m, v, name):
    R, C = w.shape
    tr = R if R <= 512 else _pick(R, (256,))

    def body(p_ref, w_ref, m_ref, v_ref, g_ref, d_ref, nm_ref, nv_ref):
        g = p_ref[0].astype(F32)
        for j in range(1, N_DEV):
            g = g + p_ref[j].astype(F32)
        g_ref[...] = g
        d_ref[...], nm_ref[...], nv_ref[...] = _adam_math(g, w_ref[...], m_ref[...], v_ref[...])

    blk = pl.BlockSpec((tr, C), lambda i: (i, 0))
    o = _sds((R, C))
    return _pc(body, name=name, out_shape=(o, o, o, o), grid=(R // tr,),
               in_specs=[pl.BlockSpec((N_DEV, tr, C), lambda i: (0, i, 0)), blk, blk, blk],
               out_specs=(blk, blk, blk, blk), sem=("parallel",))(parts, w, m, v)


def _adam_math(g, w, m, v):
    mm = ADAM_B1 * m + (1.0 - ADAM_B1) * g
    vv = ADAM_B2 * v + (1.0 - ADAM_B2) * jnp.square(g)
    m_hat = mm / (1.0 - ADAM_B1 ** ADAM_STEP)
    v_hat = vv / (1.0 - ADAM_B2 ** ADAM_STEP)
    return -ADAM_LR * (m_hat / (jnp.sqrt(v_hat) + ADAM_EPS) + ADAM_WD * w), mm, vv


def _small_layout(sizes):
    offs, off = {}, 0
    for n in SMALL_ORDER:
        offs[n] = off
        off += -(-sizes[n] // 128) * 128
    return offs, off


def _pack_small(d):
    return jnp.concatenate([_pad_lanes(d[n].reshape(1, -1), -(-d[n].size // 128) * 128) for n in SMALL_ORDER], axis=1)


def _adamw_small(parts, ws, ms, vs):
    k = len(SMALL_ORDER)
    sizes = {n: ws[n].shape[1] for n in SMALL_ORDER}
    offs, _ = _small_layout(sizes)

    def body(p_ref, *refs):
        ins, outs = refs[:3 * k], refs[3 * k:]
        for i, n in enumerate(SMALL_ORDER):
            lo, hi = offs[n], offs[n] + sizes[n]
            g = p_ref[0, :, lo:hi]
            for j in range(1, N_DEV):
                g = g + p_ref[j, :, lo:hi]
            d, mm, vv = _adam_math(g, ins[i][...], ins[k + i][...], ins[2 * k + i][...])
            outs[4 * i][...], outs[4 * i + 1][...], outs[4 * i + 2][...], outs[4 * i + 3][...] = g, d, mm, vv

    out_shape = tuple(_sds((1, sizes[n])) for n in SMALL_ORDER for _ in range(4))
    args = [ws[n] for n in SMALL_ORDER] + [ms[n] for n in SMALL_ORDER] + [vs[n] for n in SMALL_ORDER]
    res = _pc(body, name="adamw_small", out_shape=out_shape)(parts, *args)
    return {n: tuple(res[4 * i:4 * i + 4]) for i, n in enumerate(SMALL_ORDER)}


def _prep(x, p, xchg=None):
    T, D = x.shape
    Dp = p.shape[1]
    tt = _pick(T, (512, 256))

    def body(x_ref, p_ref, xb_ref, xt_ref, pb_ref, pt_ref):
        xb = x_ref[...].astype(BF)
        xb_ref[...] = xb
        xt_ref[...] = xb.T
        pb = p_ref[...].astype(BF)
        pb_ref[...] = pb
        pt_ref[...] = pb.T

    row = lambda w: pl.BlockSpec((tt, w), lambda i: (i, 0))
    col = lambda w: pl.BlockSpec((w, tt), lambda i: (0, i))
    return _pc(body, name="prep_inputs",
               out_shape=(_sds((T, D), BF), _sds((D, T), BF), _sds((T, Dp), BF), _sds((Dp, T), BF)),
               grid=(T // tt,), in_specs=[row(D), row(Dp)], out_specs=(row(D), col(D), row(Dp), col(Dp)),
               sem=("parallel",), xchg=xchg)(x, p)


SMALL_ORDER = ("gdn_a_log", "gdn_dt_bias", "gdn_norm_g", "mla_q_norm_g", "mla_kv_norm_g", "ln1_g", "ln1_b",
               "ffn_conv_b", "ple_b_gate", "ln2_g", "ln2_b")
EARLY = ("w_in", "gdn_conv_w")
LATE = ("mla_w_q_up", "mla_w_kv_up", "w_out", "ffn_w_up", "ffn_conv_w", "ffn_w_down", "ple_w_gate", "ple_w_proj")
GRADS_EARLY = ("w_out", "ffn_w_up", "ffn_conv_w", "ffn_w_down", "ple_w_gate", "ple_w_proj")
GRADS_LATE = ("w_in", "gdn_conv_w", "mla_w_q_up", "mla_w_kv_up")


def _pad_lanes(v, n=128):
    return jnp.pad(v, ((0, 0), (0, n - v.shape[1])))


def _local_step(x, p, tgt, W, sp, Bl, S, early_weights=None, late_weights=None, early_grads=None, late_grads=None):
    T = Bl * S
    W = dict(W)
    prep = _prep(x, p, xchg=None if early_weights is None else early_weights[:2])
    xb, xT, pb, pT = prep[:4]
    if early_weights is not None:
        W.update(early_weights[2](prep[4:]))
    w_in = W["w_in"]
    wqkv, wz = w_in[:, :1536], w_in[:, 1536:2048]
    z64 = jnp.zeros((w_in.shape[0], 64), w_in.dtype)
    z124 = jnp.zeros((w_in.shape[0], 124), w_in.dtype)
    wsm = jnp.concatenate([w_in[:, 2056:2440], w_in[:, 2696:2760], z64, w_in[:, 2440:2696],
                           w_in[:, 2048:2052], z124, w_in[:, 2052:2056], z124], axis=1)
    alog, dtb = _pad_lanes(sp["gdn_a_log"]), _pad_lanes(sp["gdn_dt_bias"])
    cos, sin = _rope_tables(S)

    qkv = _mm(xb, wqkv, "nn", name="mm_qkv")
    z = _mm(xb, wz, "nn", name="mm_z")
    sm = _mm(xb, wsm, "nn", name="mm_sm")
    qkvn = _gdn_pre_fwd(qkv, W["gdn_conv_w"], Bl, S)
    gdn_out = _gdn_fwd(qkvn, sm, alog, dtb, Bl, S, xchg=None if late_weights is None else late_weights[:2])
    o, tinv_s, sst_s, w_s, u_s, gc_s, beta_s = gdn_out[:7]
    if late_weights is not None:
        W.update(late_weights[2](gdn_out[7:]))
    wq = jnp.pad(W["mla_w_q_up"].reshape(-1, MLA_HEADS, 192), ((0, 0), (0, 0), (0, 64))).reshape(-1, 1024)
    wkv = W["mla_w_kv_up"]
    C = W["ffn_w_down"].shape[0]
    wup_g, wup_u = W["ffn_w_up"][:, :C], W["ffn_w_up"][:, C:]
    cw_g, cw_u = W["ffn_conv_w"][:, :C], W["ffn_conv_w"][:, C:]
    cb_g, cb_u = sp["ffn_conv_b"][:, :C], sp["ffn_conv_b"][:, C:]
    mixin, mixin_t = _gdn_post(o, z, sp["gdn_norm_g"], T)
    cqn, cqn_t = _rms_fwd(sm, sp["mla_q_norm_g"], 0, "rms_q_fwd")
    ckvn, ckvn_t = _rms_fwd(sm, sp["mla_kv_norm_g"], 2, "rms_kv_fwd")
    qcat = _qup_rope(cqn, wq, cos, sin, S)
    kv, kcat = _kvup_rope(ckvn, wkv, sm, cos, sin, S)
    mixin, lse, mixin_t = _attn_fwd(mixin, mixin_t, qcat, kcat, kv, Bl, S)
    mix, h1, h1b, h1T = _out_ln1(mixin, W["w_out"], x, sp["ln1_g"], sp["ln1_b"])
    ug = _mm(h1b, wup_g, "nn", name="mm_up_gate")
    uu = _mm(h1b, wup_u, "nn", name="mm_up_up")
    act, act_t = _ffn_act_fwd(ug, uu, cw_g, cw_u, cb_g, cb_u, Bl, S)

    dffn, dffnb, dgpre, dpp, hacc = _head(act, W["ffn_w_down"], h1, h1b, W["ple_w_gate"], pb, W["ple_w_proj"], tgt,
                                          sp["ple_b_gate"], sp["ln2_g"], sp["ln2_b"])
    loss = hacc[3, 0]
    gW, gs = {}, {}
    gs["ple_b_gate"], gs["ln2_g"], gs["ln2_b"] = hacc[0:1], hacc[1:2], hacc[2:3]
    gW["ple_w_proj"] = _mm(pT, dpp, "nn", name="mm_dproj", out_dtype=BF)
    gW["ple_w_gate"] = _mm(h1T, dgpre, "nn", name="mm_dgate", out_dtype=BF)
    gW["ffn_w_down"] = _mm(act_t, dffnb, "nn", name="mm_ddown", out_dtype=BF)
    dact = _mm(dffnb, W["ffn_w_down"], "nt", name="mm_dact")
    du_g, du_u, dcw_g, dcw_u, dcb_g, dcb_u = _ffn_act_bwd(ug, uu, cw_g, cw_u, cb_g, cb_u, dact, Bl, S)
    gW["ffn_conv_w"] = jnp.concatenate([dcw_g, dcw_u], axis=1)
    gs["ffn_conv_b"] = jnp.concatenate([dcb_g, dcb_u], axis=1)
    gW["ffn_w_up"] = jnp.concatenate([_mm(h1T, du_g, "nn", name="mm_dup_gate", out_dtype=BF),
                                      _mm(h1T, du_u, "nn", name="mm_dup_up", out_dtype=BF)], axis=1)
    dh1 = _mm(dgpre, W["ple_w_gate"], "nt", name="mm_dh1_gate", add=dffn, add_scale=ALPHA)
    dh1 = _mm(du_g, wup_g, "nt", name="mm_dh1_upg", add=dh1)
    dxa, dmix, acc1 = _dh1_ln1_bwd(du_u, wup_u, dh1, x, mix, sp["ln1_g"], sp["ln1_b"])
    gs["ln1_g"], gs["ln1_b"] = acc1[0:1], acc1[1:2]
    gW["w_out"] = _mm(mixin_t, dmix, "nn", name="mm_dwout", out_dtype=BF)
    dmixin = _mm(dmix, W["w_out"], "nt", name="mm_dmixin")
    do, dcat, gacc = _gdn_post_bwd(o, z, sp["gdn_norm_g"], dmixin, T)
    gs["gdn_norm_g"] = gacc[0:1]
    bwd_out = _gdn_bwd(qkvn, sm, alog, dtb, gc_s, beta_s, tinv_s, sst_s, w_s, u_s, do, dcat, Bl, S,
                       xchg=None if early_grads is None else early_grads({n: gW[n] for n in GRADS_EARLY}))
    dqkvn, dcat, cacc = bwd_out[:3]
    early_recv = bwd_out[3:]
    gs["gdn_a_log"], gs["gdn_dt_bias"] = cacc[0:1, :GDN_HEADS], cacc[1:2, :GDN_HEADS]
    dcat, gW["gdn_conv_w"] = _gdn_pre_bwd(qkv, W["gdn_conv_w"], dqkvn, dcat, Bl, S)
    dqc, dkv, dkrr = _attn_bwd(qcat, kcat, kv, lse, dmixin, Bl, S)
    dqraw, dcat = _rope_bwd(dqc, dcat, dkrr, cos, sin, S)
    gwq = _mm(cqn_t, dqraw, "nn", name="mm_dwq", out_dtype=BF)
    gW["mla_w_q_up"] = gwq.reshape(-1, MLA_HEADS, 256)[:, :, :192].reshape(-1, MLA_HEADS * 192)
    gW["mla_w_kv_up"] = _mm(ckvn_t, dkv, "nn", name="mm_dwkv", out_dtype=BF)
    dcqn = _mm(dqraw, wq, "nt", name="mm_dcqn")
    dckvn = _mm(dkv, wkv, "nt", name="mm_dckvn")
    dcat, qacc = _rms_bwd(sm, sp["mla_q_norm_g"], dcqn, dcat, 0, "rms_q_bwd")
    dcat, kacc = _rms_bwd(sm, sp["mla_kv_norm_g"], dckvn, dcat, 2, "rms_kv_bwd")
    gs["mla_q_norm_g"], gs["mla_kv_norm_g"] = qacc[0:1], kacc[0:1]
    gcat = _mm(xT, dcat, "nn", name="mm_dwin", out_dtype=BF)
    gsm, gqkv, gz = gcat[:, :DCAT_QKV], gcat[:, DCAT_QKV:DCAT_Z], gcat[:, DCAT_Z:]
    gW["w_in"] = jnp.concatenate([gqkv, gz, gsm[:, 768:772], gsm[:, 896:900], gsm[:, 0:384], gsm[:, 512:768],
                                  gsm[:, 384:448]], axis=1)
    wcat = jnp.concatenate([wsm, wqkv, wz], axis=1)
    dx = _mm(dcat, wcat, "nt", name="mm_dx", add=dxa, xchg=None if late_grads is None else late_grads(gW, gs))
    late_recv = ()
    if late_grads is not None:
        dx, late_recv = dx[0], dx[1:]
    return loss, dx, gW, gs, early_recv, late_recv


COL_SHARDED = ("w_in", "mla_w_q_up", "mla_w_kv_up", "ffn_w_up", "ple_w_proj", "gdn_conv_w", "ffn_conv_w")
SHARDED = EARLY + LATE
WEIGHTS = ("w_in", "gdn_conv_w", "gdn_a_log", "gdn_dt_bias", "gdn_norm_g", "mla_q_norm_g", "mla_w_q_up",
           "mla_kv_norm_g", "mla_w_kv_up", "w_out", "ln1_g", "ln1_b", "ffn_w_up", "ffn_conv_w", "ffn_conv_b",
           "ffn_w_down", "ple_w_gate", "ple_b_gate", "ple_w_proj", "ln2_g", "ln2_b")
CONV = ("gdn_conv_w", "ffn_conv_w")


def _gathered_to_full(name, g):
    if name in COL_SHARDED:
        return jnp.transpose(g, (1, 0, 2)).reshape(g.shape[1], -1)
    return g.reshape(-1, g.shape[-1])


def _full_to_blocks(name, gfull, shard_shape):
    r, c = shard_shape
    if name in COL_SHARDED:
        return jnp.transpose(gfull.reshape(r, N_DEV, c), (1, 0, 2))
    return gfull.reshape(N_DEV, r, c)


def kernel(x, p, w_in, gdn_conv_w, gdn_a_log, gdn_dt_bias, gdn_norm_g, mla_q_norm_g, mla_w_q_up, mla_kv_norm_g, mla_w_kv_up, w_out, ln1_g, ln1_b, ffn_w_up, ffn_conv_w, ffn_conv_b, ffn_w_down, ple_w_gate, ple_b_gate, ple_w_proj, ln2_g, ln2_b, loss_target, m_w_in, m_gdn_conv_w, m_gdn_a_log, m_gdn_dt_bias, m_gdn_norm_g, m_mla_q_norm_g, m_mla_w_q_up, m_mla_kv_norm_g, m_mla_w_kv_up, m_w_out, m_ln1_g, m_ln1_b, m_ffn_w_up, m_ffn_conv_w, m_ffn_conv_b, m_ffn_w_down, m_ple_w_gate, m_ple_b_gate, m_ple_w_proj, m_ln2_g, m_ln2_b, v_w_in, v_gdn_conv_w, v_gdn_a_log, v_gdn_dt_bias, v_gdn_norm_g, v_mla_q_norm_g, v_mla_w_q_up, v_mla_kv_norm_g, v_mla_w_kv_up, v_w_out, v_ln1_g, v_ln1_b, v_ffn_w_up, v_ffn_conv_w, v_ffn_conv_b, v_ffn_w_down, v_ple_w_gate, v_ple_b_gate, v_ple_w_proj, v_ln2_g, v_ln2_b):
    loc = dict(locals())
    wts = {n: loc[n] for n in WEIGHTS}
    ms = {n: loc["m_" + n] for n in WEIGHTS}
    vs = {n: loc["v_" + n] for n in WEIGHTS}
    Bl, S, D = x.shape
    T = Bl * S

    wire = lambda n: wts[n][0] if n in CONV else wts[n][0].astype(BF)
    gather = lambda names, mode: ([wire(n) for n in names], [mode] * len(names),
                                  lambda res: {n: _gathered_to_full(n, g) for n, g in zip(names, res)})
    blocks = lambda g, names: [_full_to_blocks(n, g[n], wts[n].shape[1:]) for n in names]
    pack_early = lambda g: (blocks(g, GRADS_EARLY), ["a2a"] * len(GRADS_EARLY))
    pack_late = lambda g, gs: (blocks(g, GRADS_LATE) + [_pack_small(gs)], ["a2a"] * len(GRADS_LATE) + ["ag"])
    sp = {n: wts[n].reshape(1, -1) for n in SMALL_ORDER}

    loss, dx, gW, gs, early_recv, late_recv = _local_step(
        x.reshape(T, D), p.reshape(T, -1), loss_target.reshape(T, D), {}, sp, Bl, S,
        early_weights=gather(EARLY, "ag2"), late_weights=gather(LATE, "ag"), early_grads=pack_early,
        late_grads=pack_late)

    res = {}
    for n, parts in list(zip(GRADS_EARLY, early_recv)) + list(zip(GRADS_LATE, late_recv[:-1])):
        res[n] = tuple(t[None] for t in _adamw(parts, wts[n][0], ms[n][0], vs[n][0], "adamw_" + n))
    res.update(_adamw_small(late_recv[-1], wts, ms, vs))

    loss = lax.psum(loss, ("x", "y", "c"))
    outs = [loss, dx.reshape(Bl, S, D)]
    for k in range(4):
        outs += [res[n][k] for n in WEIGHTS]
    return tuple(outs)
```

```python
import math

import jax
import jax.numpy as jnp
from jax import lax
from jax.experimental import pallas as pl
from jax.experimental.pallas import tpu as pltpu

F32 = jnp.float32
BF = jnp.bfloat16
_MXU = jnp.bfloat16
_INTERPRET = None
_VMEM_LIMIT = 56 * 1024 * 1024
HI = lax.Precision.HIGHEST

N_DEV = 8
CHUNK = 64
GDN_HEADS = 4
GDN_DK = 128
MLA_HEADS = 4
MLA_NOPE = 128
MLA_ROPE = 64
ROPE_THETA = 10000.0
ALPHA = 2.0 ** 0.25
NORM_EPS = 1e-6
ATTN_SCALE = (MLA_NOPE + MLA_ROPE) ** -0.5
DCAT_QKV, DCAT_Z, DCAT_W = 1024, 2560, 3072
ADAM_LR, ADAM_B1, ADAM_B2, ADAM_EPS, ADAM_WD, ADAM_STEP = 0.001, 0.9, 0.999, 1e-08, 0.01, 10
MESH = pl.DeviceIdType.MESH
ANY = pl.BlockSpec(memory_space=pl.ANY)
_NN = (((1,), (0,)), ((), ()))
_NT = (((1,), (1,)), ((), ()))
_TN = (((0,), (0,)), ((), ()))


def _sds(shape, dtype=F32):
    return jax.ShapeDtypeStruct(tuple(shape), dtype)


def _pick(n, cands):
    for c in cands:
        if n % c == 0:
            return c
    return n


def _xchg_out_shapes(arrs, modes):
    return [_sds(a.shape if m == "a2a" else (N_DEV,) + a.shape, a.dtype) for a, m in zip(arrs, modes)]


def _xchg_scratch(n):
    return [pltpu.SemaphoreType.DMA((n, N_DEV - 1)), pltpu.SemaphoreType.DMA((n, N_DEV - 1)),
            pltpu.SemaphoreType.DMA((n,))]


def _xchg_plan(ins, outs, send, recv, loc, modes):
    x, y, c = lax.axis_index("x"), lax.axis_index("y"), lax.axis_index("c")
    me = 4 * x + 2 * y + c
    starts, relays, waits = [], [], []
    for ai, mode in enumerate(modes):
        src_all, out = ins[ai], outs[ai]

        def remote(k, src, dst, dev):
            return pltpu.make_async_remote_copy(src_ref=src, dst_ref=dst, send_sem=send.at[ai, k], recv_sem=recv.at[ai, k],
                                                device_id=dev, device_id_type=MESH)

        own = pltpu.make_async_copy(src_all.at[me] if mode == "a2a" else src_all, out.at[me], loc.at[ai])
        starts.append(own)
        waits.append(own.wait)
        if mode == "ag2":
            sib = (x, y, 1 - c)
            chips = [(1 - x, y), (x, 1 - y), (1 - x, 1 - y)]
            first = [remote(0, src_all, out.at[me], sib)]
            first += [remote(1 + k, src_all, out.at[me], (px, py, c)) for k, (px, py) in enumerate(chips)]
            starts += first
            waits += [cp.wait_send for cp in first]
            waits.append(remote(0, src_all, out.at[4 * x + 2 * y + 1 - c], sib).wait_recv)
            for k, (px, py) in enumerate(chips):
                same, other = out.at[4 * px + 2 * py + c], out.at[4 * px + 2 * py + 1 - c]
                relay = remote(4 + k, same, same, sib)
                relays.append((remote(1 + k, src_all, same, (px, py, c)), relay))
                waits += [relay.wait_send, remote(4 + k, src_all, other, sib).wait_recv]
            continue
        for r in range(1, N_DEV):
            px = 1 - x if (r >> 2) & 1 else x
            py = 1 - y if (r >> 1) & 1 else y
            pc = 1 - c if r & 1 else c
            cp = remote(r - 1, src_all.at[4 * px + 2 * py + pc] if mode == "a2a" else src_all, out.at[me], (px, py, pc))
            starts.append(cp)
            waits.append(cp.wait)
    return starts, relays, waits


def _pc(body, *, name, out_shape, grid=None, in_specs=None, out_specs=None, scratch=(), sem=None, aliases=None,
        xchg=None):
    kw = {}
    if _INTERPRET is not None:
        kw["interpret"] = _INTERPRET
    single = not isinstance(out_shape, (tuple, list))
    out_shape = [out_shape] if single else list(out_shape)
    if out_specs is not None:
        out_specs = [out_specs] if single else list(out_specs)
    scratch = list(scratch)
    if xchg is not None:
        xarrs, xmodes = xchg
        n_in, n_out, n_scr, nx = len(in_specs), len(out_shape), len(scratch), len(xarrs)
        single = False
        inner = body

        def body(*refs):
            ins, xins = refs[:n_in], refs[n_in:n_in + nx]
            outs = refs[n_in + nx:n_in + nx + n_out]
            xouts = refs[n_in + nx + n_out:n_in + 2 * nx + n_out]
            scr = refs[n_in + 2 * nx + n_out:n_in + 2 * nx + n_out + n_scr]
            send, recv, loc = refs[n_in + 2 * nx + n_out + n_scr:]
            first = last = None
            for d, g in enumerate(grid):
                f, l = pl.program_id(d) == 0, pl.program_id(d) == g - 1
                first, last = (f, l) if first is None else (first & f, last & l)

            @pl.when(first)
            def _():
                for cp in _xchg_plan(xins, xouts, send, recv, loc, xmodes)[0]:
                    cp.start()

            inner(*ins, *outs, *scr)

            @pl.when(last)
            def _():
                _, relays, waits = _xchg_plan(xins, xouts, send, recv, loc, xmodes)
                for arrival, relay in relays:
                    arrival.wait_recv()
                    relay.start()
                for wait in waits:
                    wait()

        in_specs = list(in_specs) + [ANY] * nx
        out_specs = out_specs + [ANY] * nx
        out_shape = out_shape + _xchg_out_shapes(xarrs, xmodes)
        scratch = scratch + _xchg_scratch(nx)
        sem = ("arbitrary",) * len(grid)
    if grid is not None:
        kw["grid"] = grid
    if in_specs is not None:
        kw["in_specs"] = in_specs
    if out_specs is not None:
        kw["out_specs"] = out_specs[0] if single else tuple(out_specs)
    cp = dict(vmem_limit_bytes=_VMEM_LIMIT)
    if sem is not None:
        cp["dimension_semantics"] = sem
    call = pl.pallas_call(body, name=name, out_shape=out_shape[0] if single else tuple(out_shape),
                          scratch_shapes=scratch, input_output_aliases=aliases or {},
                          compiler_params=pltpu.CompilerParams(**cp), **kw)
    if xchg is None:
        return call
    return lambda *ins: call(*ins, *xchg[0])


_FULL_K_MAX = 3072


def _mm(a, b, mode, *, name, add=None, add_scale=1.0, out_dtype=F32, xchg=None):
    (M, K) = a.shape
    (K2, N) = b.shape if mode == "nn" else b.shape[::-1]
    assert K == K2, (a.shape, b.shape, mode)
    tm = _pick(M, (1024, 1408, 512, 384, 256, 128))
    tn = _pick(N, (1024, 2816, 1408, 768, 512, 384, 256, 128))
    tk = K if K <= _FULL_K_MAX else _pick(K, (2048, 1408, 1024, 512) if tn <= 1408 else (1024, 512))
    nk = K // tk
    dims = _NN if mode == "nn" else _NT
    has_add = add is not None

    def finish(r, add_ref, o_ref):
        if has_add:
            r = r + add_scale * add_ref[...].astype(F32)
        o_ref[...] = r.astype(out_dtype)

    if nk == 1:
        def body(a_ref, b_ref, *rest):
            r = lax.dot_general(a_ref[...].astype(_MXU), b_ref[...].astype(_MXU), dims, preferred_element_type=F32)
            finish(r, rest[0] if has_add else None, rest[-1])

        a_spec = pl.BlockSpec((tm, K), lambda i, j: (i, 0))
        b_spec = (pl.BlockSpec((K, tn), lambda i, j: (0, j)) if mode == "nn"
                  else pl.BlockSpec((tn, K), lambda i, j: (j, 0)))
        o_spec = pl.BlockSpec((tm, tn), lambda i, j: (i, j))
        grid, sem, scratch = (M // tm, N // tn), ("parallel", "parallel"), []
    else:
        def body(a_ref, b_ref, *rest):
            o_ref, acc = rest[-2], rest[-1]
            k = pl.program_id(2)

            @pl.when(k == 0)
            def _():
                acc[...] = jnp.zeros_like(acc)

            acc[...] += lax.dot_general(a_ref[...].astype(_MXU), b_ref[...].astype(_MXU), dims,
                                        preferred_element_type=F32)

            @pl.when(k == nk - 1)
            def _():
                finish(acc[...], rest[0] if has_add else None, o_ref)

        a_spec = pl.BlockSpec((tm, tk), lambda i, j, k: (i, k))
        b_spec = (pl.BlockSpec((tk, tn), lambda i, j, k: (k, j)) if mode == "nn"
                  else pl.BlockSpec((tn, tk), lambda i, j, k: (j, k)))
        o_spec = pl.BlockSpec((tm, tn), lambda i, j, k: (i, j))
        grid, sem, scratch = (M // tm, N // tn, nk), ("parallel", "parallel", "arbitrary"), [pltpu.VMEM((tm, tn), F32)]

    ins = [a, b] + ([add] if has_add else [])
    specs = [a_spec, b_spec] + ([o_spec] if has_add else [])
    return _pc(body, name=name, out_shape=_sds((M, N), out_dtype), grid=grid, in_specs=specs, out_specs=o_spec,
               scratch=scratch, sem=sem, xchg=xchg)(*ins)


def _dot(a, b, dims):
    return lax.dot_general(a.astype(_MXU), b.astype(_MXU), dims, preferred_element_type=F32)


def _split(a):
    hi = a.astype(BF)
    lo = (a - hi.astype(F32)).astype(BF)
    return hi, lo


def _dot3(a, b, dims):
    ah, al = _split(a)
    bh, bl = _split(b)
    d = lambda u, v: lax.dot_general(u, v, dims, preferred_element_type=F32)
    return d(ah, bh) + (d(ah, bl) + d(al, bh))


def _softplus(x):
    return jnp.maximum(x, 0.0) + jnp.log1p(jnp.exp(-jnp.abs(x)))


def _silu(x):
    return x * jax.nn.sigmoid(x)


def _rmsnorm(x, g):
    return x * lax.rsqrt(jnp.mean(x * x, axis=-1, keepdims=True) + NORM_EPS) * g


def _layernorm(x, g, b):
    mu = jnp.mean(x, axis=-1, keepdims=True)
    xc = x - mu
    var = jnp.mean(xc * xc, axis=-1, keepdims=True)
    return xc * lax.rsqrt(var + NORM_EPS) * g + b


def _shift_dn(x, s):
    if s == 0:
        return x
    t = lax.broadcasted_iota(jnp.int32, x.shape, 0)
    return jnp.where(t >= s, pltpu.roll(x, s, 0), 0.0)


def _shift_up(x, s):
    if s == 0:
        return x
    n = x.shape[0]
    t = lax.broadcasted_iota(jnp.int32, x.shape, 0)
    return jnp.where(t < n - s, pltpu.roll(x, n - s, 0), 0.0)


def _taps(x, kk):
    return [_shift_dn(x, kk - 1 - j) for j in range(kk)]


def _conv(x, w, taps=None):
    kk = w.shape[0]
    taps = _taps(x, kk) if taps is None else taps
    y = w[kk - 1:kk, :] * taps[kk - 1]
    for j in range(kk - 1):
        y = y + w[j:j + 1, :] * taps[j]
    return y


def _conv_bwd_x(dy, w):
    kk = w.shape[0]
    dx = w[kk - 1:kk, :] * dy
    for j in range(kk - 1):
        dx = dx + w[j:j + 1, :] * _shift_up(dy, kk - 1 - j)
    return dx


def _conv_bwd_w(taps, dy, dw_ref, first):
    kk = dw_ref.shape[0]
    for j in range(kk):
        r = jnp.sum(dy * taps[j], axis=0, keepdims=True)
        prev = jnp.where(first, 0.0, dw_ref[j:j + 1, :])
        dw_ref[j:j + 1, :] = prev + r


def _gdn_post_conv(c, j):
    h = _silu(c)
    hn = h * lax.rsqrt(jnp.sum(h * h, axis=-1, keepdims=True) + NORM_EPS)
    return jnp.where(j < GDN_HEADS, hn * (GDN_DK ** -0.5), jnp.where(j < 2 * GDN_HEADS, hn, h))


def _gdn_proj_pre_fwd(xb, wqkv, conv_w, Bl, S):
    T, D = xb.shape
    C = wqkv.shape[1]
    nj = C // 256

    def body(x_ref, p_ref, w_ref, qkv_ref, o_ref):
        j = pl.program_id(1)
        qkv = _dot(x_ref[...], p_ref[...], _NN)
        qkv_ref[...] = qkv
        c = _conv(qkv, w_ref[...])
        for half in range(2):
            o_ref[:, 128 * half:128 * (half + 1)] = _gdn_post_conv(c[:, 128 * half:128 * (half + 1)], 2 * j)

    blk = pl.BlockSpec((S, 256), lambda b, j: (b, j))
    return _pc(body, name="gdn_proj_pre_fwd", out_shape=(_sds((T, C)), _sds((T, C))), grid=(Bl, nj),
               in_specs=[pl.BlockSpec((S, D), lambda b, j: (b, 0)), pl.BlockSpec((D, 256), lambda b, j: (0, j)),
                         pl.BlockSpec((conv_w.shape[0], 256), lambda b, j: (0, j))],
               out_specs=(blk, blk), sem=("parallel", "parallel"))(xb, wqkv, conv_w)


def _gdn_pre_bwd(qkv, conv_w, dout, dcat, Bl, S):
    T, C = qkv.shape
    nj = C // 128
    kk = conv_w.shape[0]

    def body(x_ref, w_ref, d_ref, alias_ref, dx_ref, dw_ref):
        j, b = pl.program_id(0), pl.program_id(1)
        w = w_ref[...]
        taps = _taps(x_ref[...], kk)
        c = _conv(None, w, taps)
        _, vjp = jax.vjp(lambda u: _gdn_post_conv(u, j), c)
        (dc,) = vjp(d_ref[...])
        dx_ref[...] = _conv_bwd_x(dc, w).astype(dx_ref.dtype)
        _conv_bwd_w(taps, dc, dw_ref, b == 0)

    blk = pl.BlockSpec((S, 128), lambda j, b: (b, j))
    wblk = pl.BlockSpec((kk, 128), lambda j, b: (0, j))
    return _pc(body, name="gdn_pre_bwd", out_shape=(_sds(dcat.shape, dcat.dtype), _sds((kk, C))), grid=(nj, Bl),
               in_specs=[blk, wblk, blk, ANY],
               out_specs=(pl.BlockSpec((S, 128), lambda j, b: (b, DCAT_QKV // 128 + j)), wblk),
               aliases={3: 0}, sem=("parallel", "arbitrary"))(qkv, conv_w, dout, dcat)


_H = GDN_HEADS
_HC = _H * CHUNK


def _st(x):
    return jnp.concatenate([x[:, h * 128:(h + 1) * 128] for h in range(_H)], axis=0)


def _unst(x):
    return jnp.concatenate([x[h * CHUNK:(h + 1) * CHUNK] for h in range(_H)], axis=1)


def _stc(t):
    return jnp.concatenate([t[:, h:h + 1] for h in range(_H)], axis=0)


def _untile(col):
    lane = lax.broadcasted_iota(jnp.int32, (CHUNK, 128), 1)
    out = jnp.zeros((CHUNK, 128), F32)
    for h in range(_H):
        out = out + jnp.where(lane == h, col[h * CHUNK:(h + 1) * CHUNK], 0.0)
    return out


def _rowform(col):
    return jnp.broadcast_to(col, (_HC, 128)).T[0:1, :]


def _tri(n, upper=False):
    i = lax.broadcasted_iota(jnp.int32, (n, n), 0)
    j = lax.broadcasted_iota(jnp.int32, (n, n), 1)
    return jnp.where((j >= i) if upper else (j <= i), 1.0, 0.0).astype(F32)


def _gdn_chunk_common(q, k, v, gc, beta):
    Q, K, V = _st(q), _st(k), _st(v)
    B, GC = _stc(beta), _stc(gc)
    GL = jnp.concatenate([jnp.broadcast_to(gc[CHUNK - 1:CHUNK, h:h + 1], (CHUNK, 1)) for h in range(_H)], axis=0)
    ii = lax.broadcasted_iota(jnp.int32, (_HC, _HC), 0)
    jj = lax.broadcasted_iota(jnp.int32, (_HC, _HC), 1)
    same = (ii >> 6) == (jj >> 6)
    incl = same & (ii >= jj)
    strict = same & (ii > jj)
    diff = GC - _rowform(GC)
    D = jnp.where(incl, jnp.exp(jnp.where(incl, diff, 0.0)), 0.0)
    KB = K * B
    A = jnp.where(strict, _dot(KB, K, _NT) * D, 0.0)
    P = jnp.where(incl, _dot(Q, K, _NT) * D, 0.0)
    EG = jnp.exp(GC)
    ED = jnp.exp(GL - GC)
    return dict(Q=Q, K=K, V=V, B=B, GC=GC, GL=GL, incl=incl, strict=strict, D=D, KB=KB, A=A, P=P, EG=EG, ED=ED,
                QG=Q * EG, KD=K * ED)


def _neumann_inv(A):
    n = A.shape[0]
    i = lax.broadcasted_iota(jnp.int32, (n, n), 0)
    j = lax.broadcasted_iota(jnp.int32, (n, n), 1)
    N = -A
    acc = jnp.where(i == j, 1.0, 0.0) + N
    Pw = N
    for _ in range(5):
        Pw = _dot3(Pw, Pw, _NN)
        acc = acc + _dot3(acc, Pw, _NN)
    return acc


def _gates(a, alog, dtb):
    return -jnp.exp(alog) * _softplus(a + dtb)


def _hs(x, h, n=CHUNK):
    return x[h * n:(h + 1) * n]


def _per_step(Bl):
    return 4 if Bl % 4 == 0 else 2 if Bl % 2 == 0 else 1


def _gdn_fwd(qkvn, sm, alog, dtb, Bl, S, xchg=None):
    T = Bl * S
    nC = S // CHUNK
    E = _per_step(Bl)

    def body(*refs):
        S_ref = refs[-1]

        @pl.when(pl.program_id(1) == 0)
        def _():
            S_ref[...] = jnp.zeros_like(S_ref)

        for e in range(E):
            one(*[r if i in (5, 6) else r.at[e] for i, r in enumerate(refs)])

    def one(q_ref, k_ref, v_ref, a_ref, b_ref, alog_ref, dtb_ref,
            o_ref, tinv_ref, sst_ref, w_ref, u_ref, gc_ref, beta_ref, S_ref):
        g = _gates(a_ref[...], alog_ref[...], dtb_ref[...])
        beta = jax.nn.sigmoid(b_ref[...])
        gc = jnp.dot(_tri(CHUNK), g, precision=HI, preferred_element_type=F32)
        cm = _gdn_chunk_common(q_ref[...], k_ref[...], v_ref[...], gc, beta)
        tinv = _neumann_inv(cm["A"])
        WU = _dot3(tinv, jnp.concatenate([cm["KB"] * cm["EG"], cm["V"] * cm["B"]], axis=1), _NN)
        W, U = WU[:, :128], WU[:, 128:]
        s_old = [S_ref[h * 128:(h + 1) * 128, :] for h in range(_H)]
        vn = [_hs(U, h) - _dot(_hs(W, h), s_old[h], _NN) for h in range(_H)]
        o_intra = _dot(cm["P"], jnp.concatenate(vn, axis=0), _NN)
        outs = []
        for h in range(_H):
            outs.append(_dot(_hs(cm["QG"], h), s_old[h], _NN) + _hs(o_intra, h))
            gl = jnp.exp(gc[CHUNK - 1:CHUNK, h:h + 1])
            S_ref[h * 128:(h + 1) * 128, :] = gl * s_old[h] + _dot(_hs(cm["KD"], h), vn[h], _TN)
            sst_ref[h * 128:(h + 1) * 128, :] = s_old[h]
        o_ref[...] = jnp.concatenate(outs, axis=1)
        tinv_ref[...] = tinv
        w_ref[...] = _unst(W)
        u_ref[...] = _unst(U)
        gc_ref[...] = gc
        beta_ref[...] = beta

    blk = lambda r, w, c: pl.BlockSpec((E, r, w), lambda b, n: (b, n, c))
    par = pl.BlockSpec((1, 128), lambda b, n: (0, 0))
    out_shape = (_sds((Bl, S, 512)), _sds((Bl, nC * _HC, _HC)), _sds((Bl, nC * 512, 128)),
                 _sds((Bl, S, 512)), _sds((Bl, S, 512)), _sds((Bl, S, 128)), _sds((Bl, S, 128)))
    out_specs = (blk(CHUNK, 512, 0), blk(_HC, _HC, 0), blk(512, 128, 0), blk(CHUNK, 512, 0), blk(CHUNK, 512, 0),
                 blk(CHUNK, 128, 0), blk(CHUNK, 128, 0))
    q3, s3 = qkvn.reshape(Bl, S, -1), sm.reshape(Bl, S, -1)
    res = _pc(body, name="gdn_chunk_fwd", out_shape=out_shape, grid=(Bl // E, nC),
              in_specs=[blk(CHUNK, 512, 0), blk(CHUNK, 512, 1), blk(CHUNK, 512, 2), blk(CHUNK, 128, 6),
                        blk(CHUNK, 128, 7), par, par],
              out_specs=out_specs, scratch=[pltpu.VMEM((E, 512, 128), F32)], sem=("parallel", "arbitrary"),
              xchg=xchg)(q3, q3, q3, s3, s3, alog, dtb)
    return (res[0].reshape(T, 512),) + tuple(res[1:])


def _gdn_bwd(qkvn, sm, alog, dtb, gc_s, beta_s, tinv_s, sst_s, w_s, u_s, do, dcat, Bl, S, xchg=None):
    T = Bl * S
    nC = S // CHUNK
    E = _per_step(Bl)

    def body(*refs):
        acc_ref, dS_ref = refs[-2], refs[-1]
        first_chunk = pl.program_id(1) == 0

        @pl.when(first_chunk)
        def _():
            dS_ref[...] = jnp.zeros_like(dS_ref)

        @pl.when(first_chunk & (pl.program_id(0) == 0))
        def _():
            acc_ref[...] = jnp.zeros_like(acc_ref)

        for e in range(E):
            one(*[r if i in (4, 5, 13, 16) else r.at[e] for i, r in enumerate(refs)])

    def one(q_ref, k_ref, v_ref, a_ref, alog_ref, dtb_ref, gc_ref, beta_ref, tinv_ref, sst_ref, w_ref, u_ref,
            do_ref, alias_ref, dqkv_ref, dsm_ref, acc_ref, dS_ref):
        gc, beta = gc_ref[...], beta_ref[...]
        cm = _gdn_chunk_common(q_ref[...], k_ref[...], v_ref[...], gc, beta)
        Q, K, V, B, D, KB, A, P = (cm[n] for n in ("Q", "K", "V", "B", "D", "KB", "A", "P"))
        EG, ED, QG, KD = cm["EG"], cm["ED"], cm["QG"], cm["KD"]
        tinv = tinv_ref[...]
        W, U, DO = _st(w_ref[...]), _st(u_ref[...]), _st(do_ref[...])
        s_old = [sst_ref[h * 128:(h + 1) * 128, :] for h in range(_H)]
        ds_new = [dS_ref[h * 128:(h + 1) * 128, :] for h in range(_H)]
        VN = jnp.concatenate([_hs(U, h) - _dot(_hs(W, h), s_old[h], _NN) for h in range(_H)], axis=0)
        dP = jnp.where(cm["incl"], _dot(DO, VN, _NT), 0.0)
        dVN0 = _dot(P, DO, _TN)
        dVN, dQG, dKD, dW, TL = [], [], [], [], []
        for h in range(_H):
            gl = jnp.exp(gc[CHUNK - 1:CHUNK, h:h + 1])
            dvn = _hs(dVN0, h) + _dot(_hs(KD, h), ds_new[h], _NN)
            dkd = _dot(_hs(VN, h), ds_new[h], _NT)
            dVN.append(dvn)
            dQG.append(_dot(_hs(DO, h), s_old[h], _NT))
            dKD.append(dkd)
            dW.append(-_dot(dvn, s_old[h], _NT))
            dgl = jnp.sum(jnp.sum(ds_new[h] * s_old[h], axis=1, keepdims=True), axis=0, keepdims=True)
            tl = jnp.sum(jnp.sum(dkd * _hs(KD, h), axis=1, keepdims=True), axis=0, keepdims=True) + dgl * gl
            TL.append(jnp.broadcast_to(tl, (CHUNK, 1)))
            dS_ref[h * 128:(h + 1) * 128, :] = (_dot(_hs(QG, h), _hs(DO, h), _TN) + gl * ds_new[h]
                                                - _dot(_hs(W, h), dvn, _TN))
        dVN, dQG, dKD, dW, TL = (jnp.concatenate(z, axis=0) for z in (dVN, dQG, dKD, dW, TL))
        dB = _dot3(tinv, jnp.concatenate([dVN, dW], axis=1), _TN)
        dVB, dKBE = dB[:, :128], dB[:, 128:]
        dA = jnp.where(cm["strict"], -_dot3(dB, jnp.concatenate([U, W], axis=1), _NT), 0.0)
        dG = dA * D
        dQK = dP * D
        dKB = _dot(dG, K, _NN) + dKBE * EG
        dK = (_dot(jnp.concatenate([dG, dQK], axis=0), jnp.concatenate([KB, Q], axis=0), _TN)
              + dKB * B + dKD * ED)
        dQ = _dot(dQK, K, _NN) + dQG * EG
        Mx = dA * A + dP * P
        rs = lambda z: jnp.sum(z, axis=1, keepdims=True)
        ri = lax.broadcasted_iota(jnp.int32, (_HC, 1), 0)
        dGC = (rs(Mx) - rs(Mx.T) + rs(dKBE * KB * EG) + rs(dQG * QG) - rs(dKD * KD)
               + jnp.where((ri & (CHUNK - 1)) == CHUNK - 1, TL, 0.0))
        dBeta = rs(dVB * V) + rs(dKB * K)
        dqkv_ref[:, 0:512] = _unst(dQ)
        dqkv_ref[:, 512:1024] = _unst(dK)
        dqkv_ref[:, 1024:1536] = _unst(dVB * B)
        dg = jnp.dot(_tri(CHUNK, upper=True), _untile(dGC), precision=HI, preferred_element_type=F32)
        a, alog_v, dtb_v = a_ref[...], alog_ref[...], dtb_ref[...]
        g = _gates(a, alog_v, dtb_v)
        lane = lax.broadcasted_iota(jnp.int32, (CHUNK, 128), 1)
        valid = lane < _H
        da = jnp.where(valid, dg * (-jnp.exp(alog_v)) * jax.nn.sigmoid(a + dtb_v), 0.0)
        dsm_ref[:, 0:128] = da.astype(dsm_ref.dtype)
        dsm_ref[:, 128:256] = jnp.where(valid, _untile(dBeta) * beta * (1.0 - beta), 0.0).astype(dsm_ref.dtype)
        acc_ref[0:1, :] += jnp.sum(jnp.where(valid, dg * g, 0.0), axis=0, keepdims=True)
        acc_ref[1:2, :] += jnp.sum(da, axis=0, keepdims=True)

    blk = lambda r, w, c: pl.BlockSpec((E, r, w), lambda b, n: (b, nC - 1 - n, c))
    par = pl.BlockSpec((1, 128), lambda b, n: (0, 0))
    in_specs = [blk(CHUNK, 512, 0), blk(CHUNK, 512, 1), blk(CHUNK, 512, 2), blk(CHUNK, 128, 6), par, par,
                blk(CHUNK, 128, 0), blk(CHUNK, 128, 0), blk(_HC, _HC, 0), blk(512, 128, 0),
                blk(CHUNK, 512, 0), blk(CHUNK, 512, 0), blk(CHUNK, 512, 0), ANY]
    out_shape = (_sds((Bl, S, 1536)), _sds((Bl, S, DCAT_W), BF), _sds((8, 128)))
    out_specs = (blk(CHUNK, 1536, 0), blk(CHUNK, 256, 3), pl.BlockSpec((8, 128), lambda b, n: (0, 0)))
    q3, s3 = qkvn.reshape(Bl, S, -1), sm.reshape(Bl, S, -1)
    res = _pc(body, name="gdn_chunk_bwd", out_shape=out_shape, grid=(Bl // E, nC), in_specs=in_specs,
              out_specs=out_specs, scratch=[pltpu.VMEM((E, 512, 128), F32)], sem=("arbitrary", "arbitrary"),
              aliases={13: 1}, xchg=xchg)(q3, q3, q3, s3, alog, dtb, gc_s, beta_s, tinv_s, sst_s, w_s, u_s,
                                          do.reshape(Bl, S, -1), dcat.reshape(Bl, S, DCAT_W))
    return (res[0].reshape(T, 1536), res[1].reshape(T, DCAT_W)) + tuple(res[2:])


def _gdn_post(o, z, g, T):
    tt = _pick(T, (2048, 1024, 512, 256))

    def body(o_ref, z_ref, g_ref, out_ref, outt_ref):
        y = (_rmsnorm(o_ref[...], g_ref[...]) * _silu(z_ref[...])).astype(out_ref.dtype)
        out_ref[...] = y
        outt_ref[...] = y.T

    blk = pl.BlockSpec((tt, 128), lambda i, h: (i, h))
    return _pc(body, name="gdn_post_fwd", out_shape=(_sds((T, 1024), BF), _sds((1024, T), BF)), grid=(T // tt, _H),
               in_specs=[blk, blk, pl.BlockSpec((1, 128), lambda i, h: (0, 0))],
               out_specs=(blk, pl.BlockSpec((128, tt), lambda i, h: (h, i))),
               sem=("parallel", "parallel"))(o, z, g)


def _gdn_post_bwd(o, z, g, dmixin, T):
    tt = _pick(T, (2048, 1024, 512, 256))

    def body(o_ref, z_ref, g_ref, d_ref, do_ref, dz_ref, dg_ref):
        @pl.when((pl.program_id(0) == 0) & (pl.program_id(1) == 0))
        def _():
            dg_ref[...] = jnp.zeros_like(dg_ref)

        _, vjp = jax.vjp(lambda a, b, c: _rmsnorm(a, c) * _silu(b), o_ref[...], z_ref[...], g_ref[...])
        do, dz, dg = vjp(d_ref[...])
        do_ref[...] = do
        dz_ref[...] = dz.astype(dz_ref.dtype)
        dg_ref[0:1, :] += dg

    blk = pl.BlockSpec((tt, 128), lambda i, h: (i, h))
    return _pc(body, name="gdn_post_bwd", out_shape=(_sds((T, 512)), _sds((T, DCAT_W), BF), _sds((8, 128))),
               grid=(T // tt, _H), in_specs=[blk, blk, pl.BlockSpec((1, 128), lambda i, h: (0, 0)), blk],
               out_specs=(blk, pl.BlockSpec((tt, 128), lambda i, h: (i, DCAT_Z // 128 + h)),
                          pl.BlockSpec((8, 128), lambda i, h: (0, 0))),
               sem=("arbitrary", "arbitrary"))(o, z, g, dmixin)


def _rms_fwd(sm, g, col_blk, name):
    T = sm.shape[0]
    d = g.shape[1]
    tt = _pick(T, (1024, 512, 256))

    def body(x_ref, g_ref, o_ref, ot_ref):
        y = _rmsnorm(x_ref[...], g_ref[...]).astype(o_ref.dtype)
        o_ref[...] = y
        ot_ref[...] = y.T

    return _pc(body, name=name, out_shape=(_sds((T, d), BF), _sds((d, T), BF)), grid=(T // tt,),
               in_specs=[pl.BlockSpec((tt, d), lambda i: (i, col_blk)), pl.BlockSpec((1, d), lambda i: (0, 0))],
               out_specs=(pl.BlockSpec((tt, d), lambda i: (i, 0)), pl.BlockSpec((d, tt), lambda i: (0, i))),
               sem=("parallel",))(sm, g)


def _rms_bwd(sm, g, dy, dsm, col_blk, name):
    T = sm.shape[0]
    d = g.shape[1]
    tt = _pick(T, (1024, 512, 256))

    def body(x_ref, g_ref, d_ref, alias_ref, dx_ref, dg_ref):
        @pl.when(pl.program_id(0) == 0)
        def _():
            dg_ref[...] = jnp.zeros_like(dg_ref)

        _, vjp = jax.vjp(_rmsnorm, x_ref[...], g_ref[...])
        dx, dg = vjp(d_ref[...])
        dx_ref[...] = dx.astype(dx_ref.dtype)
        dg_ref[0:1, :] += dg

    grp = pl.BlockSpec((tt, d), lambda i: (i, col_blk))
    return _pc(body, name=name, out_shape=(_sds(dsm.shape, dsm.dtype), _sds((8, d))), grid=(T // tt,),
               in_specs=[grp, pl.BlockSpec((1, d), lambda i: (0, 0)), pl.BlockSpec((tt, d), lambda i: (i, 0)), ANY],
               out_specs=(grp, pl.BlockSpec((8, d), lambda i: (0, 0))), aliases={3: 0},
               sem=("arbitrary",))(sm, g, dy, dsm)


def _rope_tables(S):
    inv = ROPE_THETA ** (-jnp.arange(0, MLA_ROPE, 2, dtype=F32) / MLA_ROPE)
    ang = jnp.arange(S, dtype=F32)[:, None] * inv[None, :]
    cos, sin = jnp.cos(ang), jnp.sin(ang)
    z = jnp.zeros((S, 64), F32)
    return jnp.concatenate([cos, cos, z], axis=1), jnp.concatenate([-sin, sin, z], axis=1)


def _swap_halves(x):
    lane = lax.broadcasted_iota(jnp.int32, x.shape, 1)
    return jnp.where(lane < 32, pltpu.roll(x, 96, 1), jnp.where(lane < 64, pltpu.roll(x, 32, 1), 0.0))


def _mm_rows(a, b, *, name, tm, extras, out_shape, out_specs, epi, mode="nn", sem="parallel"):
    M, K = a.shape
    nx = len(extras)

    def body(a_ref, b_ref, *rest):
        r = lax.dot_general(a_ref[...].astype(_MXU), b_ref[...].astype(_MXU), _NN if mode == "nn" else _NT,
                            preferred_element_type=F32)
        epi(r, rest[:nx], rest[nx:])

    in_specs = [pl.BlockSpec((tm, K), lambda i: (i, 0)), pl.BlockSpec(b.shape, lambda i: (0, 0))]
    return _pc(body, name=name, out_shape=out_shape, grid=(M // tm,), in_specs=in_specs + [s for _, s in extras],
               out_specs=out_specs, sem=(sem,))(a, b, *[x for x, _ in extras])


def _qup_rope(cqn, wq, cos, sin, S):
    T = cqn.shape[0]
    tm = _pick(S, (1024, 512, 256))
    nps = S // tm

    def epi(r, xs, outs):
        c, s = xs[0][...], xs[1][...]
        for h in range(MLA_HEADS):
            lo = 256 * h
            rp = r[:, lo + 128:lo + 256]
            outs[0][:, lo:lo + 128] = (r[:, lo:lo + 128] * ATTN_SCALE).astype(BF)
            outs[0][:, lo + 128:lo + 256] = ((rp * c + _swap_halves(rp) * s) * ATTN_SCALE).astype(BF)

    tab = pl.BlockSpec((tm, 128), lambda i: (i % nps, 0))
    return _mm_rows(cqn, wq, name="mm_qup_rope", tm=tm, extras=[(cos, tab), (sin, tab)],
                    out_shape=_sds((T, 1024), BF), out_specs=pl.BlockSpec((tm, 1024), lambda i: (i, 0)), epi=epi)


def _kvup_rope(ckvn, wkv, sm, cos, sin, S):
    T = ckvn.shape[0]
    tm = _pick(S, (1024, 512, 256))
    nps = S // tm

    def epi(r, xs, outs):
        k, c, s = xs[0][...], xs[1][...], xs[2][...]
        kr = (k * c + _swap_halves(k) * s).astype(BF)
        kvb = r.astype(BF)
        outs[0][...] = kvb
        for h in range(MLA_HEADS):
            lo = 256 * h
            outs[1][:, lo:lo + 128] = kvb[:, lo:lo + 128]
            outs[1][:, lo + 128:lo + 256] = kr

    tab = pl.BlockSpec((tm, 128), lambda i: (i % nps, 0))
    row = pl.BlockSpec((tm, 1024), lambda i: (i, 0))
    return _mm_rows(ckvn, wkv, name="mm_kvup_rope", tm=tm,
                    extras=[(sm, pl.BlockSpec((tm, 128), lambda i: (i, 3))), (cos, tab), (sin, tab)],
                    out_shape=(_sds((T, 1024), BF), _sds((T, 1024), BF)), out_specs=(row, row), epi=epi)


def _rope_bwd(dqc, dcat, dkrr, cos, sin, S):
    T = dqc.shape[0]
    tt = _pick(S, (2048, 1024, 512, 256))
    nps = S // tt

    def body(alias_s, dq_ref, dk_ref, c_ref, s_ref, qo_ref, ko_ref):
        c, s = c_ref[...], s_ref[...]
        d = dq_ref[:, 128:256]
        qo_ref[:, 0:128] = dq_ref[:, 0:128].astype(qo_ref.dtype)
        qo_ref[:, 128:256] = (d * c + _swap_halves(d * s)).astype(qo_ref.dtype)

        @pl.when(pl.program_id(1) == 0)
        def _():
            k = dk_ref[...]
            ko_ref[...] = (k * c + _swap_halves(k * s)).astype(ko_ref.dtype)

    tab = pl.BlockSpec((tt, 128), lambda i, h: (i % nps, 0))
    head = pl.BlockSpec((tt, 256), lambda i, h: (i, h))
    return _pc(body, name="rope_bwd", out_shape=(_sds((T, 1024), BF), _sds(dcat.shape, dcat.dtype)),
               grid=(T // tt, MLA_HEADS),
               in_specs=[ANY, head, pl.BlockSpec((tt, 128), lambda i, h: (i, 0)), tab, tab],
               out_specs=(head, pl.BlockSpec((tt, 128), lambda i, h: (i, 3))),
               aliases={0: 1}, sem=("parallel", "arbitrary"))(dcat, dqc, dkrr, cos, sin)


def _attn_scores(q, k_ref, L, tq):
    def sc(lo, hi):
        return _dot(q, k_ref[lo:hi, :], _NT)

    sd = sc(L - tq, L)
    qc = lax.broadcasted_iota(jnp.int32, sd.shape, 0) >> 6
    kc = lax.broadcasted_iota(jnp.int32, sd.shape, 1) >> 6
    sd = jnp.where(kc <= qc, sd, -jnp.inf)
    return sd if L == tq else jnp.concatenate([sc(0, L - tq), sd], axis=1)


def _attn_fwd(mixin, mixin_t, qcat, kcat, kv, Bl, S):
    T = Bl * S
    tq = _pick(S, (256, 128))
    nq = S // tq

    def body(alias_ref, alias_t, q_ref, k_ref, v_ref, o_ref, lse_ref, ot_ref):
        i = pl.program_id(2)
        q = q_ref[...]
        for ii in range(nq):
            @pl.when(i == ii)
            def _(L=(ii + 1) * tq):
                s = _attn_scores(q, k_ref, L, tq)
                m = jnp.max(s, axis=-1, keepdims=True)
                e = jnp.exp(s - m)
                l = jnp.sum(e, axis=-1, keepdims=True)
                o = (_dot(e, v_ref[0:L, :], _NN) / l).astype(o_ref.dtype)
                o_ref[...] = o
                ot_ref[...] = o.T
                lse_ref[...] = jnp.broadcast_to(m + jnp.log(l), lse_ref.shape)

    qrow = lambda b, h, i: b * nq + i
    in_specs = [ANY, ANY,
                pl.BlockSpec((tq, 256), lambda b, h, i: (qrow(b, h, i), h)),
                pl.BlockSpec((S, 256), lambda b, h, i: (b, h)),
                pl.BlockSpec((S, 128), lambda b, h, i: (b, 2 * h + 1))]
    return _pc(body, name="mla_attn_fwd",
               out_shape=(_sds(mixin.shape, mixin.dtype), _sds((T, 512)), _sds(mixin_t.shape, mixin_t.dtype)),
               grid=(Bl, MLA_HEADS, nq), in_specs=in_specs,
               out_specs=(pl.BlockSpec((tq, 128), lambda b, h, i: (qrow(b, h, i), 4 + h)),
                          pl.BlockSpec((tq, 128), lambda b, h, i: (qrow(b, h, i), h)),
                          pl.BlockSpec((128, tq), lambda b, h, i: (4 + h, qrow(b, h, i)))),
               aliases={0: 0, 1: 2}, sem=("parallel", "parallel", "parallel"))(mixin, mixin_t, qcat, kcat, kv)


def _attn_bwd(qcat, kcat, kv, lse, dmixin, Bl, S):
    T = Bl * S
    tq = _pick(S, (256, 128))
    nq = S // tq

    def body(q_ref, k_ref, v_ref, do_ref, lse_ref, dq_ref, dkv_ref, dkr_ref, acck_ref, accv_ref):
        h, i = pl.program_id(1), pl.program_id(2)
        q = q_ref[...]
        do32 = do_ref[...]
        do = do32.astype(BF)
        lse_col = lse_ref[:, 0:1]

        @pl.when(i == 0)
        def _():
            acck_ref[...] = jnp.zeros_like(acck_ref)
            accv_ref[...] = jnp.zeros_like(accv_ref)

        for ii in range(nq):
            @pl.when(i == ii)
            def _(L=(ii + 1) * tq):
                p = jnp.exp(_attn_scores(q, k_ref, L, tq) - lse_col)
                pb = p.astype(BF)
                v = v_ref[0:L, :]
                delta = jnp.sum(do32 * _dot(pb, v, _NN), axis=-1, keepdims=True)
                ds = (p * (_dot(do, v, _NT) - delta)).astype(BF)
                dq_ref[...] = _dot(ds, k_ref[0:L, :], _NN) * ATTN_SCALE
                acck_ref[0:L, :] += _dot(ds, q, _TN)
                accv_ref[0:L, :] += _dot(pb, do, _TN)

        @pl.when(i == nq - 1)
        def _():
            dkv_ref[:, 0:128] = acck_ref[:, 0:128].astype(dkv_ref.dtype)
            dkv_ref[:, 128:256] = accv_ref[...].astype(dkv_ref.dtype)
            dkr_ref[...] = jnp.where(h == 0, 0.0, dkr_ref[...]) + acck_ref[:, 128:256]

    qrow = lambda b, h, i: b * nq + i
    in_specs = [pl.BlockSpec((tq, 256), lambda b, h, i: (qrow(b, h, i), h)),
                pl.BlockSpec((S, 256), lambda b, h, i: (b, h)),
                pl.BlockSpec((S, 128), lambda b, h, i: (b, 2 * h + 1)),
                pl.BlockSpec((tq, 128), lambda b, h, i: (qrow(b, h, i), 4 + h)),
                pl.BlockSpec((tq, 128), lambda b, h, i: (qrow(b, h, i), h))]
    out_shape = (_sds((T, 1024)), _sds((T, 1024), BF), _sds((T, 128)))
    out_specs = (pl.BlockSpec((tq, 256), lambda b, h, i: (qrow(b, h, i), h)),
                 pl.BlockSpec((S, 256), lambda b, h, i: (b, h)),
                 pl.BlockSpec((S, 128), lambda b, h, i: (b, 0)))
    return _pc(body, name="mla_attn_bwd", out_shape=out_shape, grid=(Bl, MLA_HEADS, nq), in_specs=in_specs,
               out_specs=out_specs, scratch=[pltpu.VMEM((S, 256), F32), pltpu.VMEM((S, 128), F32)],
               sem=("arbitrary", "arbitrary", "arbitrary"))(qcat, kcat, kv, dmixin, lse)


def _out_ln1(mixin, w_out, x, g, b):
    T, D = x.shape
    tm = _pick(T, (512, 256))

    def epi(r, xs, outs):
        h = _layernorm(ALPHA * xs[0][...] + r, xs[1][...], xs[2][...])
        outs[0][...] = r
        outs[1][...] = h
        hb = h.astype(BF)
        outs[2][...] = hb
        outs[3][...] = hb.T

    row = pl.BlockSpec((tm, D), lambda i: (i, 0))
    par = pl.BlockSpec((1, D), lambda i: (0, 0))
    return _mm_rows(mixin, w_out, name="mm_out_ln1", tm=tm, extras=[(x, row), (g, par), (b, par)],
                    out_shape=(_sds((T, D)), _sds((T, D)), _sds((T, D), BF), _sds((D, T), BF)),
                    out_specs=(row, row, row, pl.BlockSpec((D, tm), lambda i: (0, i))), epi=epi)


def _dh1_ln1_bwd(du, w, dh_part, x, mix, g, b):
    T, D = x.shape
    tm = _pick(T, (256,))

    def epi(r, xs, outs):
        acc_ref = outs[2]

        @pl.when(pl.program_id(0) == 0)
        def _():
            acc_ref[...] = jnp.zeros_like(acc_ref)

        f = lambda xx, mm, gg, bb: _layernorm(ALPHA * xx + mm, gg, bb)
        _, vjp = jax.vjp(f, xs[1][...], xs[2][...], xs[3][...], xs[4][...])
        dx, dm, dg, db = vjp(r + xs[0][...])
        outs[0][...] = dx
        outs[1][...] = dm.astype(BF)
        acc_ref[0:1, :] += dg
        acc_ref[1:2, :] += db

    row = pl.BlockSpec((tm, D), lambda i: (i, 0))
    par = pl.BlockSpec((1, D), lambda i: (0, 0))
    return _mm_rows(du, w, name="mm_dh1_ln1_bwd", tm=tm, mode="nt", sem="arbitrary",
                    extras=[(dh_part, row), (x, row), (mix, row), (g, par), (b, par)],
                    out_shape=(_sds((T, D)), _sds((T, D), BF), _sds((8, D))),
                    out_specs=(row, row, pl.BlockSpec((8, D), lambda i: (0, 0))), epi=epi)


def _ffn_up_act_fwd(h1b, wup_g, wup_u, wg, wu, bg, bu, Bl, S):
    T, D = h1b.shape
    C = wup_g.shape[1]
    cb = _pick(C, (256, 128))
    kk = wg.shape[0]

    def body(h_ref, pg_ref, pu_ref, wg_ref, wu_ref, bg_ref, bu_ref, ug_ref, uu_ref, o_ref, ot_ref):
        h = h_ref[...]
        ug = _dot(h, pg_ref[...], _NN)
        uu = _dot(h, pu_ref[...], _NN)
        ug_ref[...] = ug
        uu_ref[...] = uu
        cg = _conv(ug, wg_ref[...]) + bg_ref[...]
        cu = _conv(uu, wu_ref[...]) + bu_ref[...]
        a = (_silu(cg) * cu).astype(o_ref.dtype)
        o_ref[...] = a
        ot_ref[...] = a.T

    blk = pl.BlockSpec((S, cb), lambda b, j: (b, j))
    pblk = pl.BlockSpec((D, cb), lambda b, j: (0, j))
    wblk = pl.BlockSpec((kk, cb), lambda b, j: (0, j))
    bblk = pl.BlockSpec((1, cb), lambda b, j: (0, j))
    return _pc(body, name="ffn_up_act_fwd", out_shape=(_sds((T, C)), _sds((T, C)), _sds((T, C), BF), _sds((C, T), BF)),
               grid=(Bl, C // cb), in_specs=[pl.BlockSpec((S, D), lambda b, j: (b, 0)), pblk, pblk, wblk, wblk, bblk, bblk],
               out_specs=(blk, blk, blk, pl.BlockSpec((cb, S), lambda b, j: (j, b))),
               sem=("parallel", "parallel"))(h1b, wup_g, wup_u, wg, wu, bg, bu)


def _ffn_act_bwd(ug, uu, wg, wu, bg, bu, dffnb, w_down, Bl, S):
    T, C = ug.shape
    D = dffnb.shape[1]
    cb = _pick(C, (256, 128))
    kk = wg.shape[0]

    def body(g_ref, u_ref, wg_ref, wu_ref, bg_ref, bu_ref, d_ref, wd_ref,
             dg_ref, du_ref, dwg_ref, dwu_ref, dbg_ref, dbu_ref):
        first = pl.program_id(1) == 0
        wgv, wuv = wg_ref[...], wu_ref[...]
        tg, tu = _taps(g_ref[...], kk), _taps(u_ref[...], kk)
        cg = _conv(None, wgv, tg) + bg_ref[...]
        cu = _conv(None, wuv, tu) + bu_ref[...]
        _, vjp = jax.vjp(lambda a, b: _silu(a) * b, cg, cu)
        dcg, dcu = vjp(_dot(d_ref[...], wd_ref[...], _NT))
        dg_ref[...] = _conv_bwd_x(dcg, wgv).astype(dg_ref.dtype)
        du_ref[...] = _conv_bwd_x(dcu, wuv).astype(du_ref.dtype)
        _conv_bwd_w(tg, dcg, dwg_ref, first)
        _conv_bwd_w(tu, dcu, dwu_ref, first)
        dbg_ref[...] = jnp.where(first, 0.0, dbg_ref[...]) + jnp.sum(dcg, axis=0, keepdims=True)
        dbu_ref[...] = jnp.where(first, 0.0, dbu_ref[...]) + jnp.sum(dcu, axis=0, keepdims=True)

    blk = pl.BlockSpec((S, cb), lambda j, b: (b, j))
    wblk = pl.BlockSpec((kk, cb), lambda j, b: (0, j))
    bblk = pl.BlockSpec((1, cb), lambda j, b: (0, j))
    out_shape = (_sds((T, C), BF), _sds((T, C), BF), _sds((kk, C)), _sds((kk, C)), _sds((1, C)), _sds((1, C)))
    return _pc(body, name="ffn_act_bwd", out_shape=out_shape, grid=(C // cb, Bl),
               in_specs=[blk, blk, wblk, wblk, bblk, bblk, pl.BlockSpec((S, D), lambda j, b: (b, 0)),
                         pl.BlockSpec((cb, D), lambda j, b: (j, 0))],
               out_specs=(blk, blk, wblk, wblk, bblk, bblk), sem=("parallel", "arbitrary"))(
        ug, uu, wg, wu, bg, bu, dffnb, w_down)


def _head(act, w_down, h1, h1b, w_gate, pb, w_proj, tgt, bg, g2, b2):
    T, D = h1.shape
    tm = _pick(T, (256,))

    def epi(ffn, xs, outs):
        h_ref, hb_ref, wg_ref, pb_ref, wp_ref, t_ref, bg_ref, g2_ref, b2_ref = xs
        acc_ref = outs[4]

        @pl.when(pl.program_id(0) == 0)
        def _():
            acc_ref[...] = jnp.zeros_like(acc_ref)

        h, tg = h_ref[...], t_ref[...]
        gpre = _dot(hb_ref[...], wg_ref[...], _NN)
        pp = _dot(pb_ref[...], wp_ref[...], _NN)

        def loss_fn(f, gp, p_, bgv, g2v, b2v):
            pre = ALPHA * h + f + jax.nn.sigmoid(gp + bgv) * p_
            err = _layernorm(pre, g2v, b2v) - tg
            return 0.5 * jnp.sum(jnp.mean(err * err, axis=-1, keepdims=True))

        loss, grads = jax.value_and_grad(loss_fn, argnums=(0, 1, 2, 3, 4, 5))(
            ffn, gpre, pp, bg_ref[...], g2_ref[...], b2_ref[...])
        outs[0][...] = grads[0]
        outs[1][...] = grads[0].astype(BF)
        outs[2][...] = grads[1].astype(BF)
        outs[3][...] = grads[2].astype(BF)
        acc_ref[0:1, :] += grads[3]
        acc_ref[1:2, :] += grads[4]
        acc_ref[2:3, :] += grads[5]
        acc_ref[3:4, :] += jnp.broadcast_to(loss, (1, D))

    row = pl.BlockSpec((tm, D), lambda i: (i, 0))
    par = pl.BlockSpec((1, D), lambda i: (0, 0))
    whole = lambda a: pl.BlockSpec(a.shape, lambda i: (0, 0))
    o_bf = _sds((T, D), BF)
    extras = [(h1, row), (h1b, row), (w_gate, whole(w_gate)), (pb, pl.BlockSpec((tm, pb.shape[1]), lambda i: (i, 0))),
              (w_proj, whole(w_proj)), (tgt, row), (bg, par), (g2, par), (b2, par)]
    return _mm_rows(act, w_down, name="mm_down_loss_head", tm=tm, sem="arbitrary", extras=extras,
                    out_shape=(_sds((T, D)), o_bf, o_bf, o_bf, _sds((8, D))),
                    out_specs=(row, row, row, row, pl.BlockSpec((8, D), lambda i: (0, 0))), epi=epi)


def _adamw(parts, w, m, v, name):
    R, C = w.shape
    tr = R if R <= 512 else _pick(R, (256,))

    def body(p_ref, w_ref, m_ref, v_ref, g_ref, d_ref, nm_ref, nv_ref):
        g = p_ref[0].astype(F32)
        for j in range(1, N_DEV):
            g = g + p_ref[j].astype(F32)
        g_ref[...] = g
        d_ref[...], nm_ref[...], nv_ref[...] = _adam_math(g, w_ref[...], m_ref[...], v_ref[...])

    blk = pl.BlockSpec((tr, C), lambda i: (i, 0))
    o = _sds((R, C))
    return _pc(body, name=name, out_shape=(o, o, o, o), grid=(R // tr,),
               in_specs=[pl.BlockSpec((N_DEV, tr, C), lambda i: (0, i, 0)), blk, blk, blk],
               out_specs=(blk, blk, blk, blk), sem=("parallel",))(parts, w, m, v)


def _adam_math(g, w, m, v):
    mm = ADAM_B1 * m + (1.0 - ADAM_B1) * g
    vv = ADAM_B2 * v + (1.0 - ADAM_B2) * jnp.square(g)
    m_hat = mm / (1.0 - ADAM_B1 ** ADAM_STEP)
    v_hat = vv / (1.0 - ADAM_B2 ** ADAM_STEP)
    return -ADAM_LR * (m_hat / (jnp.sqrt(v_hat) + ADAM_EPS) + ADAM_WD * w), mm, vv


def _small_layout(sizes):
    offs, off = {}, 0
    for n in SMALL_ORDER:
        offs[n] = off
        off += -(-sizes[n] // 128) * 128
    return offs, off


def _pack_small(d):
    return jnp.concatenate([_pad_lanes(d[n].reshape(1, -1), -(-d[n].size // 128) * 128) for n in SMALL_ORDER], axis=1)


def _adamw_small(parts, ws, ms, vs):
    k = len(SMALL_ORDER)
    sizes = {n: ws[n].shape[1] for n in SMALL_ORDER}
    offs, _ = _small_layout(sizes)

    def body(p_ref, *refs):
        ins, outs = refs[:3 * k], refs[3 * k:]
        for i, n in enumerate(SMALL_ORDER):
            lo, hi = offs[n], offs[n] + sizes[n]
            g = p_ref[0, :, lo:hi]
            for j in range(1, N_DEV):
                g = g + p_ref[j, :, lo:hi]
            d, mm, vv = _adam_math(g, ins[i][...], ins[k + i][...], ins[2 * k + i][...])
            outs[4 * i][...], outs[4 * i + 1][...], outs[4 * i + 2][...], outs[4 * i + 3][...] = g, d, mm, vv

    out_shape = tuple(_sds((1, sizes[n])) for n in SMALL_ORDER for _ in range(4))
    args = [ws[n] for n in SMALL_ORDER] + [ms[n] for n in SMALL_ORDER] + [vs[n] for n in SMALL_ORDER]
    res = _pc(body, name="adamw_small", out_shape=out_shape)(parts, *args)
    return {n: tuple(res[4 * i:4 * i + 4]) for i, n in enumerate(SMALL_ORDER)}


def _prep(x, p, xchg=None):
    T, D = x.shape
    Dp = p.shape[1]
    tt = _pick(T, (512, 256))

    def body(x_ref, p_ref, xb_ref, xt_ref, pb_ref, pt_ref):
        xb = x_ref[...].astype(BF)
        xb_ref[...] = xb
        xt_ref[...] = xb.T
        pb = p_ref[...].astype(BF)
        pb_ref[...] = pb
        pt_ref[...] = pb.T

    row = lambda w: pl.BlockSpec((tt, w), lambda i: (i, 0))
    col = lambda w: pl.BlockSpec((w, tt), lambda i: (0, i))
    return _pc(body, name="prep_inputs",
               out_shape=(_sds((T, D), BF), _sds((D, T), BF), _sds((T, Dp), BF), _sds((Dp, T), BF)),
               grid=(T // tt,), in_specs=[row(D), row(Dp)], out_specs=(row(D), col(D), row(Dp), col(Dp)),
               sem=("parallel",), xchg=xchg)(x, p)


SMALL_ORDER = ("gdn_a_log", "gdn_dt_bias", "gdn_norm_g", "mla_q_norm_g", "mla_kv_norm_g", "ln1_g", "ln1_b",
               "ffn_conv_b", "ple_b_gate", "ln2_g", "ln2_b")
EARLY = ("w_in", "gdn_conv_w")
LATE = ("mla_w_q_up", "mla_w_kv_up", "w_out", "ffn_w_up", "ffn_conv_w", "ffn_w_down", "ple_w_gate", "ple_w_proj")
GRADS_EARLY = ("w_out", "ffn_w_up", "ffn_conv_w", "ffn_w_down", "ple_w_gate", "ple_w_proj")
GRADS_LATE = ("w_in", "gdn_conv_w", "mla_w_q_up", "mla_w_kv_up")


def _pad_lanes(v, n=128):
    return jnp.pad(v, ((0, 0), (0, n - v.shape[1])))


def _local_step(x, p, tgt, W, sp, Bl, S, early_weights=None, late_weights=None, early_grads=None, late_grads=None):
    T = Bl * S
    W = dict(W)
    prep = _prep(x, p, xchg=None if early_weights is None else early_weights[:2])
    xb, xT, pb, pT = prep[:4]
    if early_weights is not None:
        W.update(early_weights[2](prep[4:]))
    w_in = W["w_in"]
    wqkv, wz = w_in[:, :1536], w_in[:, 1536:2048]
    z64 = jnp.zeros((w_in.shape[0], 64), w_in.dtype)
    z124 = jnp.zeros((w_in.shape[0], 124), w_in.dtype)
    wsm = jnp.concatenate([w_in[:, 2056:2440], w_in[:, 2696:2760], z64, w_in[:, 2440:2696],
                           w_in[:, 2048:2052], z124, w_in[:, 2052:2056], z124], axis=1)
    alog, dtb = _pad_lanes(sp["gdn_a_log"]), _pad_lanes(sp["gdn_dt_bias"])
    cos, sin = _rope_tables(S)

    qkv, qkvn = _gdn_proj_pre_fwd(xb, wqkv, W["gdn_conv_w"], Bl, S)
    z = _mm(xb, wz, "nn", name="mm_z")
    sm = _mm(xb, wsm, "nn", name="mm_sm")
    gdn_out = _gdn_fwd(qkvn, sm, alog, dtb, Bl, S, xchg=None if late_weights is None else late_weights[:2])
    o, tinv_s, sst_s, w_s, u_s, gc_s, beta_s = gdn_out[:7]
    if late_weights is not None:
        W.update(late_weights[2](gdn_out[7:]))
    wq = jnp.pad(W["mla_w_q_up"].reshape(-1, MLA_HEADS, 192), ((0, 0), (0, 0), (0, 64))).reshape(-1, 1024)
    wkv = W["mla_w_kv_up"]
    C = W["ffn_w_down"].shape[0]
    wup_g, wup_u = W["ffn_w_up"][:, :C], W["ffn_w_up"][:, C:]
    cw_g, cw_u = W["ffn_conv_w"][:, :C], W["ffn_conv_w"][:, C:]
    cb_g, cb_u = sp["ffn_conv_b"][:, :C], sp["ffn_conv_b"][:, C:]
    mixin, mixin_t = _gdn_post(o, z, sp["gdn_norm_g"], T)
    cqn, cqn_t = _rms_fwd(sm, sp["mla_q_norm_g"], 0, "rms_q_fwd")
    ckvn, ckvn_t = _rms_fwd(sm, sp["mla_kv_norm_g"], 2, "rms_kv_fwd")
    qcat = _qup_rope(cqn, wq, cos, sin, S)
    kv, kcat = _kvup_rope(ckvn, wkv, sm, cos, sin, S)
    mixin, lse, mixin_t = _attn_fwd(mixin, mixin_t, qcat, kcat, kv, Bl, S)
    mix, h1, h1b, h1T = _out_ln1(mixin, W["w_out"], x, sp["ln1_g"], sp["ln1_b"])
    ug, uu, act, act_t = _ffn_up_act_fwd(h1b, wup_g, wup_u, cw_g, cw_u, cb_g, cb_u, Bl, S)

    dffn, dffnb, dgpre, dpp, hacc = _head(act, W["ffn_w_down"], h1, h1b, W["ple_w_gate"], pb, W["ple_w_proj"], tgt,
                                          sp["ple_b_gate"], sp["ln2_g"], sp["ln2_b"])
    loss = hacc[3, 0]
    gW, gs = {}, {}
    gs["ple_b_gate"], gs["ln2_g"], gs["ln2_b"] = hacc[0:1], hacc[1:2], hacc[2:3]
    gW["ple_w_proj"] = _mm(pT, dpp, "nn", name="mm_dproj", out_dtype=BF)
    gW["ple_w_gate"] = _mm(h1T, dgpre, "nn", name="mm_dgate", out_dtype=BF)
    gW["ffn_w_down"] = _mm(act_t, dffnb, "nn", name="mm_ddown", out_dtype=BF)
    du_g, du_u, dcw_g, dcw_u, dcb_g, dcb_u = _ffn_act_bwd(ug, uu, cw_g, cw_u, cb_g, cb_u, dffnb, W["ffn_w_down"],
                                                          Bl, S)
    gW["ffn_conv_w"] = jnp.concatenate([dcw_g, dcw_u], axis=1)
    gs["ffn_conv_b"] = jnp.concatenate([dcb_g, dcb_u], axis=1)
    gW["ffn_w_up"] = jnp.concatenate([_mm(h1T, du_g, "nn", name="mm_dup_gate", out_dtype=BF),
                                      _mm(h1T, du_u, "nn", name="mm_dup_up", out_dtype=BF)], axis=1)
    dh1 = _mm(dgpre, W["ple_w_gate"], "nt", name="mm_dh1_gate", add=dffn, add_scale=ALPHA)
    dh1 = _mm(du_g, wup_g, "nt", name="mm_dh1_upg", add=dh1)
    dxa, dmix, acc1 = _dh1_ln1_bwd(du_u, wup_u, dh1, x, mix, sp["ln1_g"], sp["ln1_b"])
    gs["ln1_g"], gs["ln1_b"] = acc1[0:1], acc1[1:2]
    gW["w_out"] = _mm(mixin_t, dmix, "nn", name="mm_dwout", out_dtype=BF)
    dmixin = _mm(dmix, W["w_out"], "nt", name="mm_dmixin")
    do, dcat, gacc = _gdn_post_bwd(o, z, sp["gdn_norm_g"], dmixin, T)
    gs["gdn_norm_g"] = gacc[0:1]
    bwd_out = _gdn_bwd(qkvn, sm, alog, dtb, gc_s, beta_s, tinv_s, sst_s, w_s, u_s, do, dcat, Bl, S,
                       xchg=None if early_grads is None else early_grads({n: gW[n] for n in GRADS_EARLY}))
    dqkvn, dcat, cacc = bwd_out[:3]
    early_recv = bwd_out[3:]
    gs["gdn_a_log"], gs["gdn_dt_bias"] = cacc[0:1, :GDN_HEADS], cacc[1:2, :GDN_HEADS]
    dcat, gW["gdn_conv_w"] = _gdn_pre_bwd(qkv, W["gdn_conv_w"], dqkvn, dcat, Bl, S)
    dqc, dkv, dkrr = _attn_bwd(qcat, kcat, kv, lse, dmixin, Bl, S)
    dqraw, dcat = _rope_bwd(dqc, dcat, dkrr, cos, sin, S)
    gwq = _mm(cqn_t, dqraw, "nn", name="mm_dwq", out_dtype=BF)
    gW["mla_w_q_up"] = gwq.reshape(-1, MLA_HEADS, 256)[:, :, :192].reshape(-1, MLA_HEADS * 192)
    gW["mla_w_kv_up"] = _mm(ckvn_t, dkv, "nn", name="mm_dwkv", out_dtype=BF)
    dcqn = _mm(dqraw, wq, "nt", name="mm_dcqn")
    dckvn = _mm(dkv, wkv, "nt", name="mm_dckvn")
    dcat, qacc = _rms_bwd(sm, sp["mla_q_norm_g"], dcqn, dcat, 0, "rms_q_bwd")
    dcat, kacc = _rms_bwd(sm, sp["mla_kv_norm_g"], dckvn, dcat, 2, "rms_kv_bwd")
    gs["mla_q_norm_g"], gs["mla_kv_norm_g"] = qacc[0:1], kacc[0:1]
    gcat = _mm(xT, dcat, "nn", name="mm_dwin", out_dtype=BF)
    gsm, gqkv, gz = gcat[:, :DCAT_QKV], gcat[:, DCAT_QKV:DCAT_Z], gcat[:, DCAT_Z:]
    gW["w_in"] = jnp.concatenate([gqkv, gz, gsm[:, 768:772], gsm[:, 896:900], gsm[:, 0:384], gsm[:, 512:768],
                                  gsm[:, 384:448]], axis=1)
    wcat = jnp.concatenate([wsm, wqkv, wz], axis=1)
    dx = _mm(dcat, wcat, "nt", name="mm_dx", add=dxa, xchg=None if late_grads is None else late_grads(gW, gs))
    late_recv = ()
    if late_grads is not None:
        dx, late_recv = dx[0], dx[1:]
    return loss, dx, gW, gs, early_recv, late_recv


COL_SHARDED = ("w_in", "mla_w_q_up", "mla_w_kv_up", "ffn_w_up", "ple_w_proj", "gdn_conv_w", "ffn_conv_w")
SHARDED = EARLY + LATE
WEIGHTS = ("w_in", "gdn_conv_w", "gdn_a_log", "gdn_dt_bias", "gdn_norm_g", "mla_q_norm_g", "mla_w_q_up",
           "mla_kv_norm_g", "mla_w_kv_up", "w_out", "ln1_g", "ln1_b", "ffn_w_up", "ffn_conv_w", "ffn_conv_b",
           "ffn_w_down", "ple_w_gate", "ple_b_gate", "ple_w_proj", "ln2_g", "ln2_b")
CONV = ("gdn_conv_w", "ffn_conv_w")


def _gathered_to_full(name, g):
    if name in COL_SHARDED:
        return jnp.transpose(g, (1, 0, 2)).reshape(g.shape[1], -1)
    return g.reshape(-1, g.shape[-1])


def _full_to_blocks(name, gfull, shard_shape):
    r, c = shard_shape
    if name in COL_SHARDED:
        return jnp.transpose(gfull.reshape(r, N_DEV, c), (1, 0, 2))
    return gfull.reshape(N_DEV, r, c)


def kernel(x, p, w_in, gdn_conv_w, gdn_a_log, gdn_dt_bias, gdn_norm_g, mla_q_norm_g, mla_w_q_up, mla_kv_norm_g, mla_w_kv_up, w_out, ln1_g, ln1_b, ffn_w_up, ffn_conv_w, ffn_conv_b, ffn_w_down, ple_w_gate, ple_b_gate, ple_w_proj, ln2_g, ln2_b, loss_target, m_w_in, m_gdn_conv_w, m_gdn_a_log, m_gdn_dt_bias, m_gdn_norm_g, m_mla_q_norm_g, m_mla_w_q_up, m_mla_kv_norm_g, m_mla_w_kv_up, m_w_out, m_ln1_g, m_ln1_b, m_ffn_w_up, m_ffn_conv_w, m_ffn_conv_b, m_ffn_w_down, m_ple_w_gate, m_ple_b_gate, m_ple_w_proj, m_ln2_g, m_ln2_b, v_w_in, v_gdn_conv_w, v_gdn_a_log, v_gdn_dt_bias, v_gdn_norm_g, v_mla_q_norm_g, v_mla_w_q_up, v_mla_kv_norm_g, v_mla_w_kv_up, v_w_out, v_ln1_g, v_ln1_b, v_ffn_w_up, v_ffn_conv_w, v_ffn_conv_b, v_ffn_w_down, v_ple_w_gate, v_ple_b_gate, v_ple_w_proj, v_ln2_g, v_ln2_b):
    loc = dict(locals())
    wts = {n: loc[n] for n in WEIGHTS}
    ms = {n: loc["m_" + n] for n in WEIGHTS}
    vs = {n: loc["v_" + n] for n in WEIGHTS}
    Bl, S, D = x.shape
    T = Bl * S

    wire = lambda n: wts[n][0] if n in CONV else wts[n][0].astype(BF)
    gather = lambda names, mode: ([wire(n) for n in names], [mode] * len(names),
                                  lambda res: {n: _gathered_to_full(n, g) for n, g in zip(names, res)})
    blocks = lambda g, names: [_full_to_blocks(n, g[n], wts[n].shape[1:]) for n in names]
    pack_early = lambda g: (blocks(g, GRADS_EARLY), ["a2a"] * len(GRADS_EARLY))
    pack_late = lambda g, gs: (blocks(g, GRADS_LATE) + [_pack_small(gs)], ["a2a"] * len(GRADS_LATE) + ["ag"])
    sp = {n: wts[n].reshape(1, -1) for n in SMALL_ORDER}

    loss, dx, gW, gs, early_recv, late_recv = _local_step(
        x.reshape(T, D), p.reshape(T, -1), loss_target.reshape(T, D), {}, sp, Bl, S,
        early_weights=gather(EARLY, "ag2"), late_weights=gather(LATE, "ag"), early_grads=pack_early,
        late_grads=pack_late)

    res = {}
    for n, parts in list(zip(GRADS_EARLY, early_recv)) + list(zip(GRADS_LATE, late_recv[:-1])):
        res[n] = tuple(t[None] for t in _adamw(parts, wts[n][0], ms[n][0], vs[n][0], "adamw_" + n))
    res.update(_adamw_small(late_recv[-1], wts, ms, vs))

    loss = lax.psum(loss, ("x", "y", "c"))
    outs = [loss, dx.reshape(Bl, S, D)]
    for k in range(4):
        outs += [res[n][k] for n in WEIGHTS]
    return tuple(outs)
```

```python
import math

import jax
import jax.numpy as jnp
from jax import lax
from jax.experimental import pallas as pl
from jax.experimental.pallas import tpu as pltpu

F32 = jnp.float32
BF = jnp.bfloat16
_MXU = jnp.bfloat16
_INTERPRET = None
_VMEM_LIMIT = 56 * 1024 * 1024
HI = lax.Precision.HIGHEST

N_DEV = 8
CHUNK = 64
GDN_HEADS = 4
GDN_DK = 128
MLA_HEADS = 4
MLA_NOPE = 128
MLA_ROPE = 64
ROPE_THETA = 10000.0
ALPHA = 2.0 ** 0.25
NORM_EPS = 1e-6
ATTN_SCALE = (MLA_NOPE + MLA_ROPE) ** -0.5
DCAT_QKV, DCAT_Z, DCAT_W = 1024, 2560, 3072
ADAM_LR, ADAM_B1, ADAM_B2, ADAM_EPS, ADAM_WD, ADAM_STEP = 0.001, 0.9, 0.999, 1e-08, 0.01, 10
MESH = pl.DeviceIdType.MESH
ANY = pl.BlockSpec(memory_space=pl.ANY)
_NN = (((1,), (0,)), ((), ()))
_NT = (((1,), (1,)), ((), ()))
_TN = (((0,), (0,)), ((), ()))


def _sds(shape, dtype=F32):
    return jax.ShapeDtypeStruct(tuple(shape), dtype)


def _pick(n, cands):
    for c in cands:
        if n % c == 0:
            return c
    return n


def _xchg_out_shapes(arrs, modes):
    return [_sds(a.shape if m == "a2a" else (N_DEV,) + a.shape, a.dtype) for a, m in zip(arrs, modes)]


def _xchg_scratch(n):
    return [pltpu.SemaphoreType.DMA((n, N_DEV - 1)), pltpu.SemaphoreType.DMA((n, N_DEV - 1)),
            pltpu.SemaphoreType.DMA((n,))]


def _xchg_plan(ins, outs, send, recv, loc, modes):
    x, y, c = lax.axis_index("x"), lax.axis_index("y"), lax.axis_index("c")
    me = 4 * x + 2 * y + c
    starts, relays, waits = [], [], []
    for ai, mode in enumerate(modes):
        src_all, out = ins[ai], outs[ai]

        def remote(k, src, dst, dev):
            return pltpu.make_async_remote_copy(src_ref=src, dst_ref=dst, send_sem=send.at[ai, k], recv_sem=recv.at[ai, k],
                                                device_id=dev, device_id_type=MESH)

        own = pltpu.make_async_copy(src_all.at[me] if mode == "a2a" else src_all, out.at[me], loc.at[ai])
        starts.append(own)
        waits.append(own.wait)
        if mode == "ag2":
            sib = (x, y, 1 - c)
            chips = [(1 - x, y), (x, 1 - y), (1 - x, 1 - y)]
            first = [remote(0, src_all, out.at[me], sib)]
            first += [remote(1 + k, src_all, out.at[me], (px, py, c)) for k, (px, py) in enumerate(chips)]
            starts += first
            waits += [cp.wait_send for cp in first]
            waits.append(remote(0, src_all, out.at[4 * x + 2 * y + 1 - c], sib).wait_recv)
            for k, (px, py) in enumerate(chips):
                same, other = out.at[4 * px + 2 * py + c], out.at[4 * px + 2 * py + 1 - c]
                relay = remote(4 + k, same, same, sib)
                relays.append((remote(1 + k, src_all, same, (px, py, c)), relay))
                waits += [relay.wait_send, remote(4 + k, src_all, other, sib).wait_recv]
            continue
        for r in range(1, N_DEV):
            px = 1 - x if (r >> 2) & 1 else x
            py = 1 - y if (r >> 1) & 1 else y
            pc = 1 - c if r & 1 else c
            cp = remote(r - 1, src_all.at[4 * px + 2 * py + pc] if mode == "a2a" else src_all, out.at[me], (px, py, pc))
            starts.append(cp)
            waits.append(cp.wait)
    return starts, relays, waits


def _pc(body, *, name, out_shape, grid=None, in_specs=None, out_specs=None, scratch=(), sem=None, aliases=None,
        xchg=None):
    kw = {}
    if _INTERPRET is not None:
        kw["interpret"] = _INTERPRET
    single = not isinstance(out_shape, (tuple, list))
    out_shape = [out_shape] if single else list(out_shape)
    if out_specs is not None:
        out_specs = [out_specs] if single else list(out_specs)
    scratch = list(scratch)
    if xchg is not None:
        xarrs, xmodes = xchg
        n_in, n_out, n_scr, nx = len(in_specs), len(out_shape), len(scratch), len(xarrs)
        single = False
        inner = body

        def body(*refs):
            ins, xins = refs[:n_in], refs[n_in:n_in + nx]
            outs = refs[n_in + nx:n_in + nx + n_out]
            xouts = refs[n_in + nx + n_out:n_in + 2 * nx + n_out]
            scr = refs[n_in + 2 * nx + n_out:n_in + 2 * nx + n_out + n_scr]
            send, recv, loc = refs[n_in + 2 * nx + n_out + n_scr:]
            first = last = None
            for d, g in enumerate(grid):
                f, l = pl.program_id(d) == 0, pl.program_id(d) == g - 1
                first, last = (f, l) if first is None else (first & f, last & l)

            @pl.when(first)
            def _():
                for cp in _xchg_plan(xins, xouts, send, recv, loc, xmodes)[0]:
                    cp.start()

            inner(*ins, *outs, *scr)

            @pl.when(last)
            def _():
                _, relays, waits = _xchg_plan(xins, xouts, send, recv, loc, xmodes)
                for arrival, relay in relays:
                    arrival.wait_recv()
                    relay.start()
                for wait in waits:
                    wait()

        in_specs = list(in_specs) + [ANY] * nx
        out_specs = out_specs + [ANY] * nx
        out_shape = out_shape + _xchg_out_shapes(xarrs, xmodes)
        scratch = scratch + _xchg_scratch(nx)
        sem = ("arbitrary",) * len(grid)
    if grid is not None:
        kw["grid"] = grid
    if in_specs is not None:
        kw["in_specs"] = in_specs
    if out_specs is not None:
        kw["out_specs"] = out_specs[0] if single else tuple(out_specs)
    cp = dict(vmem_limit_bytes=_VMEM_LIMIT)
    if sem is not None:
        cp["dimension_semantics"] = sem
    call = pl.pallas_call(body, name=name, out_shape=out_shape[0] if single else tuple(out_shape),
                          scratch_shapes=scratch, input_output_aliases=aliases or {},
                          compiler_params=pltpu.CompilerParams(**cp), **kw)
    if xchg is None:
        return call
    return lambda *ins: call(*ins, *xchg[0])


_FULL_K_MAX = 3072


def _mm(a, b, mode, *, name, add=None, add_scale=1.0, out_dtype=F32, xchg=None):
    (M, K) = a.shape
    (K2, N) = b.shape if mode == "nn" else b.shape[::-1]
    assert K == K2, (a.shape, b.shape, mode)
    tm = _pick(M, (1024, 1408, 512, 384, 256, 128))
    tn = _pick(N, (1024, 2816, 1408, 768, 512, 384, 256, 128))
    tk = K if K <= _FULL_K_MAX else _pick(K, (2048, 1408, 1024, 512) if tn <= 1408 else (1024, 512))
    nk = K // tk
    dims = _NN if mode == "nn" else _NT
    has_add = add is not None

    def finish(r, add_ref, o_ref):
        if has_add:
            r = r + add_scale * add_ref[...].astype(F32)
        o_ref[...] = r.astype(out_dtype)

    if nk == 1:
        def body(a_ref, b_ref, *rest):
            r = lax.dot_general(a_ref[...].astype(_MXU), b_ref[...].astype(_MXU), dims, preferred_element_type=F32)
            finish(r, rest[0] if has_add else None, rest[-1])

        a_spec = pl.BlockSpec((tm, K), lambda i, j: (i, 0))
        b_spec = (pl.BlockSpec((K, tn), lambda i, j: (0, j)) if mode == "nn"
                  else pl.BlockSpec((tn, K), lambda i, j: (j, 0)))
        o_spec = pl.BlockSpec((tm, tn), lambda i, j: (i, j))
        grid, sem, scratch = (M // tm, N // tn), ("parallel", "parallel"), []
    else:
        def body(a_ref, b_ref, *rest):
            o_ref, acc = rest[-2], rest[-1]
            k = pl.program_id(2)

            @pl.when(k == 0)
            def _():
                acc[...] = jnp.zeros_like(acc)

            acc[...] += lax.dot_general(a_ref[...].astype(_MXU), b_ref[...].astype(_MXU), dims,
                                        preferred_element_type=F32)

            @pl.when(k == nk - 1)
            def _():
                finish(acc[...], rest[0] if has_add else None, o_ref)

        a_spec = pl.BlockSpec((tm, tk), lambda i, j, k: (i, k))
        b_spec = (pl.BlockSpec((tk, tn), lambda i, j, k: (k, j)) if mode == "nn"
                  else pl.BlockSpec((tn, tk), lambda i, j, k: (j, k)))
        o_spec = pl.BlockSpec((tm, tn), lambda i, j, k: (i, j))
        grid, sem, scratch = (M // tm, N // tn, nk), ("parallel", "parallel", "arbitrary"), [pltpu.VMEM((tm, tn), F32)]

    ins = [a, b] + ([add] if has_add else [])
    specs = [a_spec, b_spec] + ([o_spec] if has_add else [])
    return _pc(body, name=name, out_shape=_sds((M, N), out_dtype), grid=grid, in_specs=specs, out_specs=o_spec,
               scratch=scratch, sem=sem, xchg=xchg)(*ins)


def _dot(a, b, dims):
    return lax.dot_general(a.astype(_MXU), b.astype(_MXU), dims, preferred_element_type=F32)


def _split(a):
    hi = a.astype(BF)
    lo = (a - hi.astype(F32)).astype(BF)
    return hi, lo


def _dot3(a, b, dims):
    ah, al = _split(a)
    bh, bl = _split(b)
    d = lambda u, v: lax.dot_general(u, v, dims, preferred_element_type=F32)
    return d(ah, bh) + (d(ah, bl) + d(al, bh))


def _softplus(x):
    return jnp.maximum(x, 0.0) + jnp.log1p(jnp.exp(-jnp.abs(x)))


def _silu(x):
    return x * jax.nn.sigmoid(x)


def _rmsnorm(x, g):
    return x * lax.rsqrt(jnp.mean(x * x, axis=-1, keepdims=True) + NORM_EPS) * g


def _layernorm(x, g, b):
    mu = jnp.mean(x, axis=-1, keepdims=True)
    xc = x - mu
    var = jnp.mean(xc * xc, axis=-1, keepdims=True)
    return xc * lax.rsqrt(var + NORM_EPS) * g + b


def _shift_dn(x, s):
    if s == 0:
        return x
    t = lax.broadcasted_iota(jnp.int32, x.shape, 0)
    return jnp.where(t >= s, pltpu.roll(x, s, 0), 0.0)


def _shift_up(x, s):
    if s == 0:
        return x
    n = x.shape[0]
    t = lax.broadcasted_iota(jnp.int32, x.shape, 0)
    return jnp.where(t < n - s, pltpu.roll(x, n - s, 0), 0.0)


def _taps(x, kk):
    return [_shift_dn(x, kk - 1 - j) for j in range(kk)]


def _conv(x, w, taps=None):
    kk = w.shape[0]
    taps = _taps(x, kk) if taps is None else taps
    y = w[kk - 1:kk, :] * taps[kk - 1]
    for j in range(kk - 1):
        y = y + w[j:j + 1, :] * taps[j]
    return y


def _conv_bwd_x(dy, w):
    kk = w.shape[0]
    dx = w[kk - 1:kk, :] * dy
    for j in range(kk - 1):
        dx = dx + w[j:j + 1, :] * _shift_up(dy, kk - 1 - j)
    return dx


def _conv_bwd_w(taps, dy, dw_ref, first):
    kk = dw_ref.shape[0]
    for j in range(kk):
        r = jnp.sum(dy * taps[j], axis=0, keepdims=True)
        prev = jnp.where(first, 0.0, dw_ref[j:j + 1, :])
        dw_ref[j:j + 1, :] = prev + r


def _gdn_post_conv(c, j):
    h = _silu(c)
    hn = h * lax.rsqrt(jnp.sum(h * h, axis=-1, keepdims=True) + NORM_EPS)
    return jnp.where(j < GDN_HEADS, hn * (GDN_DK ** -0.5), jnp.where(j < 2 * GDN_HEADS, hn, h))


def _gdn_proj_pre_fwd(xb, wqkv, conv_w, Bl, S):
    T, D = xb.shape
    C = wqkv.shape[1]
    nj = C // 256

    def body(x_ref, p_ref, w_ref, qkv_ref, o_ref):
        j = pl.program_id(1)
        qkv = _dot(x_ref[...], p_ref[...], _NN)
        qkv_ref[...] = qkv
        c = _conv(qkv, w_ref[...])
        for half in range(2):
            o_ref[:, 128 * half:128 * (half + 1)] = _gdn_post_conv(c[:, 128 * half:128 * (half + 1)], 2 * j)

    blk = pl.BlockSpec((S, 256), lambda b, j: (b, j))
    return _pc(body, name="gdn_proj_pre_fwd", out_shape=(_sds((T, C)), _sds((T, C))), grid=(Bl, nj),
               in_specs=[pl.BlockSpec((S, D), lambda b, j: (b, 0)), pl.BlockSpec((D, 256), lambda b, j: (0, j)),
                         pl.BlockSpec((conv_w.shape[0], 256), lambda b, j: (0, j))],
               out_specs=(blk, blk), sem=("parallel", "parallel"))(xb, wqkv, conv_w)


def _gdn_pre_bwd(qkv, conv_w, dout, dcat, Bl, S):
    T, C = qkv.shape
    nj = C // 128
    kk = conv_w.shape[0]

    def body(x_ref, w_ref, d_ref, alias_ref, dx_ref, dw_ref):
        j, b = pl.program_id(0), pl.program_id(1)
        w = w_ref[...]
        taps = _taps(x_ref[...], kk)
        c = _conv(None, w, taps)
        _, vjp = jax.vjp(lambda u: _gdn_post_conv(u, j), c)
        (dc,) = vjp(d_ref[...])
        dx_ref[...] = _conv_bwd_x(dc, w).astype(dx_ref.dtype)
        _conv_bwd_w(taps, dc, dw_ref, b == 0)

    blk = pl.BlockSpec((S, 128), lambda j, b: (b, j))
    wblk = pl.BlockSpec((kk, 128), lambda j, b: (0, j))
    return _pc(body, name="gdn_pre_bwd", out_shape=(_sds(dcat.shape, dcat.dtype), _sds((kk, C))), grid=(nj, Bl),
               in_specs=[blk, wblk, blk, ANY],
               out_specs=(pl.BlockSpec((S, 128), lambda j, b: (b, DCAT_QKV // 128 + j)), wblk),
               aliases={3: 0}, sem=("parallel", "arbitrary"))(qkv, conv_w, dout, dcat)


_H = GDN_HEADS
_HC = _H * CHUNK


def _st(x):
    return jnp.concatenate([x[:, h * 128:(h + 1) * 128] for h in range(_H)], axis=0)


def _unst(x):
    return jnp.concatenate([x[h * CHUNK:(h + 1) * CHUNK] for h in range(_H)], axis=1)


def _stc(t):
    return jnp.concatenate([t[:, h:h + 1] for h in range(_H)], axis=0)


def _untile(col):
    lane = lax.broadcasted_iota(jnp.int32, (CHUNK, 128), 1)
    out = jnp.zeros((CHUNK, 128), F32)
    for h in range(_H):
        out = out + jnp.where(lane == h, col[h * CHUNK:(h + 1) * CHUNK], 0.0)
    return out


def _rowform(col):
    return jnp.broadcast_to(col, (_HC, 128)).T[0:1, :]


def _tri(n, upper=False):
    i = lax.broadcasted_iota(jnp.int32, (n, n), 0)
    j = lax.broadcasted_iota(jnp.int32, (n, n), 1)
    return jnp.where((j >= i) if upper else (j <= i), 1.0, 0.0).astype(F32)


def _gdn_chunk_common(q, k, v, gc, beta):
    Q, K, V = _st(q), _st(k), _st(v)
    B, GC = _stc(beta), _stc(gc)
    GL = jnp.concatenate([jnp.broadcast_to(gc[CHUNK - 1:CHUNK, h:h + 1], (CHUNK, 1)) for h in range(_H)], axis=0)
    ii = lax.broadcasted_iota(jnp.int32, (_HC, _HC), 0)
    jj = lax.broadcasted_iota(jnp.int32, (_HC, _HC), 1)
    same = (ii >> 6) == (jj >> 6)
    incl = same & (ii >= jj)
    strict = same & (ii > jj)
    diff = GC - _rowform(GC)
    D = jnp.where(incl, jnp.exp(jnp.where(incl, diff, 0.0)), 0.0)
    KB = K * B
    A = jnp.where(strict, _dot(KB, K, _NT) * D, 0.0)
    P = jnp.where(incl, _dot(Q, K, _NT) * D, 0.0)
    EG = jnp.exp(GC)
    ED = jnp.exp(GL - GC)
    return dict(Q=Q, K=K, V=V, B=B, GC=GC, GL=GL, incl=incl, strict=strict, D=D, KB=KB, A=A, P=P, EG=EG, ED=ED,
                QG=Q * EG, KD=K * ED)


def _neumann_inv(A):
    n = A.shape[0]
    i = lax.broadcasted_iota(jnp.int32, (n, n), 0)
    j = lax.broadcasted_iota(jnp.int32, (n, n), 1)
    N = -A
    acc = jnp.where(i == j, 1.0, 0.0) + N
    Pw = N
    for _ in range(5):
        Pw = _dot3(Pw, Pw, _NN)
        acc = acc + _dot3(acc, Pw, _NN)
    return acc


def _gates(a, alog, dtb):
    return -jnp.exp(alog) * _softplus(a + dtb)


def _hs(x, h, n=CHUNK):
    return x[h * n:(h + 1) * n]


def _per_step(Bl):
    return 4 if Bl % 4 == 0 else 2 if Bl % 2 == 0 else 1


def _gdn_fwd(qkvn, sm, alog, dtb, Bl, S, xchg=None):
    T = Bl * S
    nC = S // CHUNK
    E = _per_step(Bl)

    def body(*refs):
        S_ref = refs[-1]

        @pl.when(pl.program_id(1) == 0)
        def _():
            S_ref[...] = jnp.zeros_like(S_ref)

        for e in range(E):
            one(*[r if i in (5, 6) else r.at[e] for i, r in enumerate(refs)])

    def one(q_ref, k_ref, v_ref, a_ref, b_ref, alog_ref, dtb_ref,
            o_ref, tinv_ref, sst_ref, w_ref, u_ref, gc_ref, beta_ref, S_ref):
        g = _gates(a_ref[...], alog_ref[...], dtb_ref[...])
        beta = jax.nn.sigmoid(b_ref[...])
        gc = jnp.dot(_tri(CHUNK), g, precision=HI, preferred_element_type=F32)
        cm = _gdn_chunk_common(q_ref[...], k_ref[...], v_ref[...], gc, beta)
        tinv = _neumann_inv(cm["A"])
        WU = _dot3(tinv, jnp.concatenate([cm["KB"] * cm["EG"], cm["V"] * cm["B"]], axis=1), _NN)
        W, U = WU[:, :128], WU[:, 128:]
        s_old = [S_ref[h * 128:(h + 1) * 128, :] for h in range(_H)]
        vn = [_hs(U, h) - _dot(_hs(W, h), s_old[h], _NN) for h in range(_H)]
        o_intra = _dot(cm["P"], jnp.concatenate(vn, axis=0), _NN)
        outs = []
        for h in range(_H):
            outs.append(_dot(_hs(cm["QG"], h), s_old[h], _NN) + _hs(o_intra, h))
            gl = jnp.exp(gc[CHUNK - 1:CHUNK, h:h + 1])
            S_ref[h * 128:(h + 1) * 128, :] = gl * s_old[h] + _dot(_hs(cm["KD"], h), vn[h], _TN)
            sst_ref[h * 128:(h + 1) * 128, :] = s_old[h]
        o_ref[...] = jnp.concatenate(outs, axis=1)
        tinv_ref[...] = tinv
        w_ref[...] = _unst(W)
        u_ref[...] = _unst(U)
        gc_ref[...] = gc
        beta_ref[...] = beta

    blk = lambda r, w, c: pl.BlockSpec((E, r, w), lambda b, n: (b, n, c))
    par = pl.BlockSpec((1, 128), lambda b, n: (0, 0))
    out_shape = (_sds((Bl, S, 512)), _sds((Bl, nC * _HC, _HC)), _sds((Bl, nC * 512, 128)),
                 _sds((Bl, S, 512)), _sds((Bl, S, 512)), _sds((Bl, S, 128)), _sds((Bl, S, 128)))
    out_specs = (blk(CHUNK, 512, 0), blk(_HC, _HC, 0), blk(512, 128, 0), blk(CHUNK, 512, 0), blk(CHUNK, 512, 0),
                 blk(CHUNK, 128, 0), blk(CHUNK, 128, 0))
    q3, s3 = qkvn.reshape(Bl, S, -1), sm.reshape(Bl, S, -1)
    res = _pc(body, name="gdn_chunk_fwd", out_shape=out_shape, grid=(Bl // E, nC),
              in_specs=[blk(CHUNK, 512, 0), blk(CHUNK, 512, 1), blk(CHUNK, 512, 2), blk(CHUNK, 128, 6),
                        blk(CHUNK, 128, 7), par, par],
              out_specs=out_specs, scratch=[pltpu.VMEM((E, 512, 128), F32)], sem=("parallel", "arbitrary"),
              xchg=xchg)(q3, q3, q3, s3, s3, alog, dtb)
    return (res[0].reshape(T, 512),) + tuple(res[1:])


def _gdn_bwd(qkvn, sm, alog, dtb, gc_s, beta_s, tinv_s, sst_s, w_s, u_s, do, dcat, Bl, S, xchg=None):
    T = Bl * S
    nC = S // CHUNK
    E = _per_step(Bl)

    def body(*refs):
        acc_ref, dS_ref = refs[-2], refs[-1]
        first_chunk = pl.program_id(1) == 0

        @pl.when(first_chunk)
        def _():
            dS_ref[...] = jnp.zeros_like(dS_ref)

        @pl.when(first_chunk & (pl.program_id(0) == 0))
        def _():
            acc_ref[...] = jnp.zeros_like(acc_ref)

        for e in range(E):
            one(*[r if i in (4, 5, 13, 16) else r.at[e] for i, r in enumerate(refs)])

    def one(q_ref, k_ref, v_ref, a_ref, alog_ref, dtb_ref, gc_ref, beta_ref, tinv_ref, sst_ref, w_ref, u_ref,
            do_ref, alias_ref, dqkv_ref, dsm_ref, acc_ref, dS_ref):
        gc, beta = gc_ref[...], beta_ref[...]
        cm = _gdn_chunk_common(q_ref[...], k_ref[...], v_ref[...], gc, beta)
        Q, K, V, B, D, KB, A, P = (cm[n] for n in ("Q", "K", "V", "B", "D", "KB", "A", "P"))
        EG, ED, QG, KD = cm["EG"], cm["ED"], cm["QG"], cm["KD"]
        tinv = tinv_ref[...]
        W, U, DO = _st(w_ref[...]), _st(u_ref[...]), _st(do_ref[...])
        s_old = [sst_ref[h * 128:(h + 1) * 128, :] for h in range(_H)]
        ds_new = [dS_ref[h * 128:(h + 1) * 128, :] for h in range(_H)]
        VN = jnp.concatenate([_hs(U, h) - _dot(_hs(W, h), s_old[h], _NN) for h in range(_H)], axis=0)
        dP = jnp.where(cm["incl"], _dot(DO, VN, _NT), 0.0)
        dVN0 = _dot(P, DO, _TN)
        dVN, dQG, dKD, dW, TL = [], [], [], [], []
        for h in range(_H):
            gl = jnp.exp(gc[CHUNK - 1:CHUNK, h:h + 1])
            dvn = _hs(dVN0, h) + _dot(_hs(KD, h), ds_new[h], _NN)
            dkd = _dot(_hs(VN, h), ds_new[h], _NT)
            dVN.append(dvn)
            dQG.append(_dot(_hs(DO, h), s_old[h], _NT))
            dKD.append(dkd)
            dW.append(-_dot(dvn, s_old[h], _NT))
            dgl = jnp.sum(jnp.sum(ds_new[h] * s_old[h], axis=1, keepdims=True), axis=0, keepdims=True)
            tl = jnp.sum(jnp.sum(dkd * _hs(KD, h), axis=1, keepdims=True), axis=0, keepdims=True) + dgl * gl
            TL.append(jnp.broadcast_to(tl, (CHUNK, 1)))
            dS_ref[h * 128:(h + 1) * 128, :] = (_dot(_hs(QG, h), _hs(DO, h), _TN) + gl * ds_new[h]
                                                - _dot(_hs(W, h), dvn, _TN))
        dVN, dQG, dKD, dW, TL = (jnp.concatenate(z, axis=0) for z in (dVN, dQG, dKD, dW, TL))
        dB = _dot3(tinv, jnp.concatenate([dVN, dW], axis=1), _TN)
        dVB, dKBE = dB[:, :128], dB[:, 128:]
        dA = jnp.where(cm["strict"], -_dot3(dB, jnp.concatenate([U, W], axis=1), _NT), 0.0)
        dG = dA * D
        dQK = dP * D
        dKB = _dot(dG, K, _NN) + dKBE * EG
        dK = (_dot(jnp.concatenate([dG, dQK], axis=0), jnp.concatenate([KB, Q], axis=0), _TN)
              + dKB * B + dKD * ED)
        dQ = _dot(dQK, K, _NN) + dQG * EG
        Mx = dA * A + dP * P
        rs = lambda z: jnp.sum(z, axis=1, keepdims=True)
        ri = lax.broadcasted_iota(jnp.int32, (_HC, 1), 0)
        dGC = (rs(Mx) - rs(Mx.T) + rs(dKBE * KB * EG) + rs(dQG * QG) - rs(dKD * KD)
               + jnp.where((ri & (CHUNK - 1)) == CHUNK - 1, TL, 0.0))
        dBeta = rs(dVB * V) + rs(dKB * K)
        dqkv_ref[:, 0:512] = _unst(dQ)
        dqkv_ref[:, 512:1024] = _unst(dK)
        dqkv_ref[:, 1024:1536] = _unst(dVB * B)
        dg = jnp.dot(_tri(CHUNK, upper=True), _untile(dGC), precision=HI, preferred_element_type=F32)
        a, alog_v, dtb_v = a_ref[...], alog_ref[...], dtb_ref[...]
        g = _gates(a, alog_v, dtb_v)
        lane = lax.broadcasted_iota(jnp.int32, (CHUNK, 128), 1)
        valid = lane < _H
        da = jnp.where(valid, dg * (-jnp.exp(alog_v)) * jax.nn.sigmoid(a + dtb_v), 0.0)
        dsm_ref[:, 0:128] = da.astype(dsm_ref.dtype)
        dsm_ref[:, 128:256] = jnp.where(valid, _untile(dBeta) * beta * (1.0 - beta), 0.0).astype(dsm_ref.dtype)
        acc_ref[0:1, :] += jnp.sum(jnp.where(valid, dg * g, 0.0), axis=0, keepdims=True)
        acc_ref[1:2, :] += jnp.sum(da, axis=0, keepdims=True)

    blk = lambda r, w, c: pl.BlockSpec((E, r, w), lambda b, n: (b, nC - 1 - n, c))
    par = pl.BlockSpec((1, 128), lambda b, n: (0, 0))
    in_specs = [blk(CHUNK, 512, 0), blk(CHUNK, 512, 1), blk(CHUNK, 512, 2), blk(CHUNK, 128, 6), par, par,
                blk(CHUNK, 128, 0), blk(CHUNK, 128, 0), blk(_HC, _HC, 0), blk(512, 128, 0),
                blk(CHUNK, 512, 0), blk(CHUNK, 512, 0), blk(CHUNK, 512, 0), ANY]
    out_shape = (_sds((Bl, S, 1536)), _sds((Bl, S, DCAT_W), BF), _sds((8, 128)))
    out_specs = (blk(CHUNK, 1536, 0), blk(CHUNK, 256, 3), pl.BlockSpec((8, 128), lambda b, n: (0, 0)))
    q3, s3 = qkvn.reshape(Bl, S, -1), sm.reshape(Bl, S, -1)
    res = _pc(body, name="gdn_chunk_bwd", out_shape=out_shape, grid=(Bl // E, nC), in_specs=in_specs,
              out_specs=out_specs, scratch=[pltpu.VMEM((E, 512, 128), F32)], sem=("arbitrary", "arbitrary"),
              aliases={13: 1}, xchg=xchg)(q3, q3, q3, s3, alog, dtb, gc_s, beta_s, tinv_s, sst_s, w_s, u_s,
                                          do.reshape(Bl, S, -1), dcat.reshape(Bl, S, DCAT_W))
    return (res[0].reshape(T, 1536), res[1].reshape(T, DCAT_W)) + tuple(res[2:])


def _gdn_post(o, z, g, T):
    tt = _pick(T, (2048, 1024, 512, 256))

    def body(o_ref, z_ref, g_ref, out_ref, outt_ref):
        y = (_rmsnorm(o_ref[...], g_ref[...]) * _silu(z_ref[...])).astype(out_ref.dtype)
        out_ref[...] = y
        outt_ref[...] = y.T

    blk = pl.BlockSpec((tt, 128), lambda i, h: (i, h))
    return _pc(body, name="gdn_post_fwd", out_shape=(_sds((T, 1024), BF), _sds((1024, T), BF)), grid=(T // tt, _H),
               in_specs=[blk, blk, pl.BlockSpec((1, 128), lambda i, h: (0, 0))],
               out_specs=(blk, pl.BlockSpec((128, tt), lambda i, h: (h, i))),
               sem=("parallel", "parallel"))(o, z, g)


def _gdn_post_bwd(o, z, g, dmixin, T):
    tt = _pick(T, (2048, 1024, 512, 256))

    def body(o_ref, z_ref, g_ref, d_ref, do_ref, dz_ref, dg_ref):
        @pl.when((pl.program_id(0) == 0) & (pl.program_id(1) == 0))
        def _():
            dg_ref[...] = jnp.zeros_like(dg_ref)

        _, vjp = jax.vjp(lambda a, b, c: _rmsnorm(a, c) * _silu(b), o_ref[...], z_ref[...], g_ref[...])
        do, dz, dg = vjp(d_ref[...])
        do_ref[...] = do
        dz_ref[...] = dz.astype(dz_ref.dtype)
        dg_ref[0:1, :] += dg

    blk = pl.BlockSpec((tt, 128), lambda i, h: (i, h))
    return _pc(body, name="gdn_post_bwd", out_shape=(_sds((T, 512)), _sds((T, DCAT_W), BF), _sds((8, 128))),
               grid=(T // tt, _H), in_specs=[blk, blk, pl.BlockSpec((1, 128), lambda i, h: (0, 0)), blk],
               out_specs=(blk, pl.BlockSpec((tt, 128), lambda i, h: (i, DCAT_Z // 128 + h)),
                          pl.BlockSpec((8, 128), lambda i, h: (0, 0))),
               sem=("arbitrary", "arbitrary"))(o, z, g, dmixin)


def _rms_fwd(sm, g, col_blk, name):
    T = sm.shape[0]
    d = g.shape[1]
    tt = _pick(T, (1024, 512, 256))

    def body(x_ref, g_ref, o_ref, ot_ref):
        y = _rmsnorm(x_ref[...], g_ref[...]).astype(o_ref.dtype)
        o_ref[...] = y
        ot_ref[...] = y.T

    return _pc(body, name=name, out_shape=(_sds((T, d), BF), _sds((d, T), BF)), grid=(T // tt,),
               in_specs=[pl.BlockSpec((tt, d), lambda i: (i, col_blk)), pl.BlockSpec((1, d), lambda i: (0, 0))],
               out_specs=(pl.BlockSpec((tt, d), lambda i: (i, 0)), pl.BlockSpec((d, tt), lambda i: (0, i))),
               sem=("parallel",))(sm, g)


def _rms_bwd(sm, g, dy, dsm, col_blk, name):
    T = sm.shape[0]
    d = g.shape[1]
    tt = _pick(T, (1024, 512, 256))

    def body(x_ref, g_ref, d_ref, alias_ref, dx_ref, dg_ref):
        @pl.when(pl.program_id(0) == 0)
        def _():
            dg_ref[...] = jnp.zeros_like(dg_ref)

        _, vjp = jax.vjp(_rmsnorm, x_ref[...], g_ref[...])
        dx, dg = vjp(d_ref[...])
        dx_ref[...] = dx.astype(dx_ref.dtype)
        dg_ref[0:1, :] += dg

    grp = pl.BlockSpec((tt, d), lambda i: (i, col_blk))
    return _pc(body, name=name, out_shape=(_sds(dsm.shape, dsm.dtype), _sds((8, d))), grid=(T // tt,),
               in_specs=[grp, pl.BlockSpec((1, d), lambda i: (0, 0)), pl.BlockSpec((tt, d), lambda i: (i, 0)), ANY],
               out_specs=(grp, pl.BlockSpec((8, d), lambda i: (0, 0))), aliases={3: 0},
               sem=("arbitrary",))(sm, g, dy, dsm)


def _rope_tables(S):
    inv = ROPE_THETA ** (-jnp.arange(0, MLA_ROPE, 2, dtype=F32) / MLA_ROPE)
    ang = jnp.arange(S, dtype=F32)[:, None] * inv[None, :]
    cos, sin = jnp.cos(ang), jnp.sin(ang)
    z = jnp.zeros((S, 64), F32)
    return jnp.concatenate([cos, cos, z], axis=1), jnp.concatenate([-sin, sin, z], axis=1)


def _swap_halves(x):
    lane = lax.broadcasted_iota(jnp.int32, x.shape, 1)
    return jnp.where(lane < 32, pltpu.roll(x, 96, 1), jnp.where(lane < 64, pltpu.roll(x, 32, 1), 0.0))


def _mm_rows(a, b, *, name, tm, extras, out_shape, out_specs, epi, mode="nn", sem="parallel"):
    M, K = a.shape
    nx = len(extras)

    def body(a_ref, b_ref, *rest):
        r = lax.dot_general(a_ref[...].astype(_MXU), b_ref[...].astype(_MXU), _NN if mode == "nn" else _NT,
                            preferred_element_type=F32)
        epi(r, rest[:nx], rest[nx:])

    in_specs = [pl.BlockSpec((tm, K), lambda i: (i, 0)), pl.BlockSpec(b.shape, lambda i: (0, 0))]
    return _pc(body, name=name, out_shape=out_shape, grid=(M // tm,), in_specs=in_specs + [s for _, s in extras],
               out_specs=out_specs, sem=(sem,))(a, b, *[x for x, _ in extras])


def _qup_rope(cqn, wq, cos, sin, S):
    T = cqn.shape[0]
    tm = _pick(S, (1024, 512, 256))
    nps = S // tm

    def epi(r, xs, outs):
        c, s = xs[0][...], xs[1][...]
        for h in range(MLA_HEADS):
            lo = 256 * h
            rp = r[:, lo + 128:lo + 256]
            outs[0][:, lo:lo + 128] = (r[:, lo:lo + 128] * ATTN_SCALE).astype(BF)
            outs[0][:, lo + 128:lo + 256] = ((rp * c + _swap_halves(rp) * s) * ATTN_SCALE).astype(BF)

    tab = pl.BlockSpec((tm, 128), lambda i: (i % nps, 0))
    return _mm_rows(cqn, wq, name="mm_qup_rope", tm=tm, extras=[(cos, tab), (sin, tab)],
                    out_shape=_sds((T, 1024), BF), out_specs=pl.BlockSpec((tm, 1024), lambda i: (i, 0)), epi=epi)


def _kvup_rope(ckvn, wkv, sm, cos, sin, S):
    T = ckvn.shape[0]
    tm = _pick(S, (1024, 512, 256))
    nps = S // tm

    def epi(r, xs, outs):
        k, c, s = xs[0][...], xs[1][...], xs[2][...]
        kr = (k * c + _swap_halves(k) * s).astype(BF)
        kvb = r.astype(BF)
        outs[0][...] = kvb
        for h in range(MLA_HEADS):
            lo = 256 * h
            outs[1][:, lo:lo + 128] = kvb[:, lo:lo + 128]
            outs[1][:, lo + 128:lo + 256] = kr

    tab = pl.BlockSpec((tm, 128), lambda i: (i % nps, 0))
    row = pl.BlockSpec((tm, 1024), lambda i: (i, 0))
    return _mm_rows(ckvn, wkv, name="mm_kvup_rope", tm=tm,
                    extras=[(sm, pl.BlockSpec((tm, 128), lambda i: (i, 3))), (cos, tab), (sin, tab)],
                    out_shape=(_sds((T, 1024), BF), _sds((T, 1024), BF)), out_specs=(row, row), epi=epi)


def _rope_bwd(dqc, dcat, dkrr, cos, sin, S):
    T = dqc.shape[0]
    tt = _pick(S, (2048, 1024, 512, 256))
    nps = S // tt

    def body(alias_s, dq_ref, dk_ref, c_ref, s_ref, qo_ref, ko_ref):
        c, s = c_ref[...], s_ref[...]
        d = dq_ref[:, 128:256]
        qo_ref[:, 0:128] = dq_ref[:, 0:128].astype(qo_ref.dtype)
        qo_ref[:, 128:256] = (d * c + _swap_halves(d * s)).astype(qo_ref.dtype)

        @pl.when(pl.program_id(1) == 0)
        def _():
            k = dk_ref[...]
            ko_ref[...] = (k * c + _swap_halves(k * s)).astype(ko_ref.dtype)

    tab = pl.BlockSpec((tt, 128), lambda i, h: (i % nps, 0))
    head = pl.BlockSpec((tt, 256), lambda i, h: (i, h))
    return _pc(body, name="rope_bwd", out_shape=(_sds((T, 1024), BF), _sds(dcat.shape, dcat.dtype)),
               grid=(T // tt, MLA_HEADS),
               in_specs=[ANY, head, pl.BlockSpec((tt, 128), lambda i, h: (i, 0)), tab, tab],
               out_specs=(head, pl.BlockSpec((tt, 128), lambda i, h: (i, 3))),
               aliases={0: 1}, sem=("parallel", "arbitrary"))(dcat, dqc, dkrr, cos, sin)


def _attn_scores(q, k_ref, L, tq):
    def sc(lo, hi):
        return _dot(q, k_ref[lo:hi, :], _NT)

    sd = sc(L - tq, L)
    qc = lax.broadcasted_iota(jnp.int32, sd.shape, 0) >> 6
    kc = lax.broadcasted_iota(jnp.int32, sd.shape, 1) >> 6
    sd = jnp.where(kc <= qc, sd, -jnp.inf)
    return sd if L == tq else jnp.concatenate([sc(0, L - tq), sd], axis=1)


def _attn_fwd(mixin, mixin_t, qcat, kcat, kv, Bl, S):
    T = Bl * S
    tq = _pick(S, (256, 128))
    nq = S // tq

    def body(alias_ref, alias_t, q_ref, k_ref, v_ref, o_ref, lse_ref, ot_ref):
        i = pl.program_id(2)
        q = q_ref[...]
        for ii in range(nq):
            @pl.when(i == ii)
            def _(L=(ii + 1) * tq):
                s = _attn_scores(q, k_ref, L, tq)
                m = jnp.max(s, axis=-1, keepdims=True)
                e = jnp.exp(s - m)
                l = jnp.sum(e, axis=-1, keepdims=True)
                o = (_dot(e, v_ref[0:L, :], _NN) / l).astype(o_ref.dtype)
                o_ref[...] = o
                ot_ref[...] = o.T
                lse_ref[...] = jnp.broadcast_to(m + jnp.log(l), lse_ref.shape)

    qrow = lambda b, h, i: b * nq + i
    in_specs = [ANY, ANY,
                pl.BlockSpec((tq, 256), lambda b, h, i: (qrow(b, h, i), h)),
                pl.BlockSpec((S, 256), lambda b, h, i: (b, h)),
                pl.BlockSpec((S, 128), lambda b, h, i: (b, 2 * h + 1))]
    return _pc(body, name="mla_attn_fwd",
               out_shape=(_sds(mixin.shape, mixin.dtype), _sds((T, 512)), _sds(mixin_t.shape, mixin_t.dtype)),
               grid=(Bl, MLA_HEADS, nq), in_specs=in_specs,
               out_specs=(pl.BlockSpec((tq, 128), lambda b, h, i: (qrow(b, h, i), 4 + h)),
                          pl.BlockSpec((tq, 128), lambda b, h, i: (qrow(b, h, i), h)),
                          pl.BlockSpec((128, tq), lambda b, h, i: (4 + h, qrow(b, h, i)))),
               aliases={0: 0, 1: 2}, sem=("parallel", "parallel", "parallel"))(mixin, mixin_t, qcat, kcat, kv)


def _attn_bwd(qcat, kcat, kv, lse, dmixin, Bl, S):
    T = Bl * S
    tq = _pick(S, (256, 128))
    nq = S // tq

    def body(q_ref, k_ref, v_ref, do_ref, lse_ref, dq_ref, dkv_ref, dkr_ref, acck_ref, accv_ref):
        h, i = pl.program_id(1), pl.program_id(2)
        q = q_ref[...]
        do32 = do_ref[...]
        do = do32.astype(BF)
        lse_col = lse_ref[:, 0:1]

        @pl.when(i == 0)
        def _():
            acck_ref[...] = jnp.zeros_like(acck_ref)
            accv_ref[...] = jnp.zeros_like(accv_ref)

        for ii in range(nq):
            @pl.when(i == ii)
            def _(L=(ii + 1) * tq):
                p = jnp.exp(_attn_scores(q, k_ref, L, tq) - lse_col)
                pb = p.astype(BF)
                v = v_ref[0:L, :]
                delta = jnp.sum(do32 * _dot(pb, v, _NN), axis=-1, keepdims=True)
                ds = (p * (_dot(do, v, _NT) - delta)).astype(BF)
                dq_ref[...] = _dot(ds, k_ref[0:L, :], _NN) * ATTN_SCALE
                acck_ref[0:L, :] += _dot(ds, q, _TN)
                accv_ref[0:L, :] += _dot(pb, do, _TN)

        @pl.when(i == nq - 1)
        def _():
            dkv_ref[:, 0:128] = acck_ref[:, 0:128].astype(dkv_ref.dtype)
            dkv_ref[:, 128:256] = accv_ref[...].astype(dkv_ref.dtype)
            dkr_ref[...] = jnp.where(h == 0, 0.0, dkr_ref[...]) + acck_ref[:, 128:256]

    qrow = lambda b, h, i: b * nq + i
    in_specs = [pl.BlockSpec((tq, 256), lambda b, h, i: (qrow(b, h, i), h)),
                pl.BlockSpec((S, 256), lambda b, h, i: (b, h)),
                pl.BlockSpec((S, 128), lambda b, h, i: (b, 2 * h + 1)),
                pl.BlockSpec((tq, 128), lambda b, h, i: (qrow(b, h, i), 4 + h)),
                pl.BlockSpec((tq, 128), lambda b, h, i: (qrow(b, h, i), h))]
    out_shape = (_sds((T, 1024)), _sds((T, 1024), BF), _sds((T, 128)))
    out_specs = (pl.BlockSpec((tq, 256), lambda b, h, i: (qrow(b, h, i), h)),
                 pl.BlockSpec((S, 256), lambda b, h, i: (b, h)),
                 pl.BlockSpec((S, 128), lambda b, h, i: (b, 0)))
    return _pc(body, name="mla_attn_bwd", out_shape=out_shape, grid=(Bl, MLA_HEADS, nq), in_specs=in_specs,
               out_specs=out_specs, scratch=[pltpu.VMEM((S, 256), F32), pltpu.VMEM((S, 128), F32)],
               sem=("arbitrary", "arbitrary", "arbitrary"))(qcat, kcat, kv, dmixin, lse)


def _out_ln1(mixin, w_out, x, g, b):
    T, D = x.shape
    tm = _pick(T, (512, 256))

    def epi(r, xs, outs):
        h = _layernorm(ALPHA * xs[0][...] + r, xs[1][...], xs[2][...])
        outs[0][...] = r
        outs[1][...] = h
        hb = h.astype(BF)
        outs[2][...] = hb
        outs[3][...] = hb.T

    row = pl.BlockSpec((tm, D), lambda i: (i, 0))
    par = pl.BlockSpec((1, D), lambda i: (0, 0))
    return _mm_rows(mixin, w_out, name="mm_out_ln1", tm=tm, extras=[(x, row), (g, par), (b, par)],
                    out_shape=(_sds((T, D)), _sds((T, D)), _sds((T, D), BF), _sds((D, T), BF)),
                    out_specs=(row, row, row, pl.BlockSpec((D, tm), lambda i: (0, i))), epi=epi)


def _dh1_ln1_bwd(du_u, wup_u, du_g, wup_g, dgpre, w_gate, dffn, x, mix, g, b):
    T, D = x.shape
    tm = _pick(T, (256,))

    def epi(r, xs, outs):
        dug_ref, wg_ref, dgp_ref, wgate_ref, dffn_ref, x_ref, mix_ref, g_ref, b_ref = xs
        acc_ref = outs[2]

        @pl.when(pl.program_id(0) == 0)
        def _():
            acc_ref[...] = jnp.zeros_like(acc_ref)

        dh = (r + _dot(dug_ref[...], wg_ref[...], _NT) + _dot(dgp_ref[...], wgate_ref[...], _NT)
              + ALPHA * dffn_ref[...])
        f = lambda xx, mm, gg, bb: _layernorm(ALPHA * xx + mm, gg, bb)
        _, vjp = jax.vjp(f, x_ref[...], mix_ref[...], g_ref[...], b_ref[...])
        dx, dm, dg, db = vjp(dh)
        outs[0][...] = dx
        outs[1][...] = dm.astype(BF)
        acc_ref[0:1, :] += dg
        acc_ref[1:2, :] += db

    row = pl.BlockSpec((tm, D), lambda i: (i, 0))
    par = pl.BlockSpec((1, D), lambda i: (0, 0))
    whole = lambda a: pl.BlockSpec(a.shape, lambda i: (0, 0))
    extras = [(du_g, pl.BlockSpec((tm, du_g.shape[1]), lambda i: (i, 0))), (wup_g, whole(wup_g)), (dgpre, row),
              (w_gate, whole(w_gate)), (dffn, row), (x, row), (mix, row), (g, par), (b, par)]
    return _mm_rows(du_u, wup_u, name="mm_dh1_ln1_bwd", tm=tm, mode="nt", sem="arbitrary", extras=extras,
                    out_shape=(_sds((T, D)), _sds((T, D), BF), _sds((8, D))),
                    out_specs=(row, row, pl.BlockSpec((8, D), lambda i: (0, 0))), epi=epi)


def _ffn_up_act_fwd(h1b, wup_g, wup_u, wg, wu, bg, bu, Bl, S):
    T, D = h1b.shape
    C = wup_g.shape[1]
    cb = _pick(C, (256, 128))
    kk = wg.shape[0]

    def body(h_ref, pg_ref, pu_ref, wg_ref, wu_ref, bg_ref, bu_ref, ug_ref, uu_ref, o_ref, ot_ref):
        h = h_ref[...]
        ug = _dot(h, pg_ref[...], _NN)
        uu = _dot(h, pu_ref[...], _NN)
        ug_ref[...] = ug
        uu_ref[...] = uu
        cg = _conv(ug, wg_ref[...]) + bg_ref[...]
        cu = _conv(uu, wu_ref[...]) + bu_ref[...]
        a = (_silu(cg) * cu).astype(o_ref.dtype)
        o_ref[...] = a
        ot_ref[...] = a.T

    blk = pl.BlockSpec((S, cb), lambda b, j: (b, j))
    pblk = pl.BlockSpec((D, cb), lambda b, j: (0, j))
    wblk = pl.BlockSpec((kk, cb), lambda b, j: (0, j))
    bblk = pl.BlockSpec((1, cb), lambda b, j: (0, j))
    return _pc(body, name="ffn_up_act_fwd", out_shape=(_sds((T, C)), _sds((T, C)), _sds((T, C), BF), _sds((C, T), BF)),
               grid=(Bl, C // cb), in_specs=[pl.BlockSpec((S, D), lambda b, j: (b, 0)), pblk, pblk, wblk, wblk, bblk, bblk],
               out_specs=(blk, blk, blk, pl.BlockSpec((cb, S), lambda b, j: (j, b))),
               sem=("parallel", "parallel"))(h1b, wup_g, wup_u, wg, wu, bg, bu)


def _ffn_act_bwd(ug, uu, wg, wu, bg, bu, dffnb, w_down, h1_t, Bl, S):
    T, C = ug.shape
    D = dffnb.shape[1]
    cb = _pick(C, (256, 128))
    kk = wg.shape[0]

    def body(g_ref, u_ref, wg_ref, wu_ref, bg_ref, bu_ref, d_ref, wd_ref, ht_ref,
             dg_ref, du_ref, dpg_ref, dpu_ref, dwg_ref, dwu_ref, dbg_ref, dbu_ref, accg_ref, accu_ref):
        b = pl.program_id(1)
        first = b == 0
        wgv, wuv = wg_ref[...], wu_ref[...]
        tg, tu = _taps(g_ref[...], kk), _taps(u_ref[...], kk)
        cg = _conv(None, wgv, tg) + bg_ref[...]
        cu = _conv(None, wuv, tu) + bu_ref[...]
        _, vjp = jax.vjp(lambda a, b: _silu(a) * b, cg, cu)
        dcg, dcu = vjp(_dot(d_ref[...], wd_ref[...], _NT))
        dxg = _conv_bwd_x(dcg, wgv).astype(dg_ref.dtype)
        dxu = _conv_bwd_x(dcu, wuv).astype(du_ref.dtype)
        dg_ref[...] = dxg
        du_ref[...] = dxu
        ht = ht_ref[...]
        accg_ref[...] = jnp.where(first, 0.0, accg_ref[...]) + _dot(ht, dxg, _NN)
        accu_ref[...] = jnp.where(first, 0.0, accu_ref[...]) + _dot(ht, dxu, _NN)

        @pl.when(b == Bl - 1)
        def _():
            dpg_ref[...] = accg_ref[...].astype(dpg_ref.dtype)
            dpu_ref[...] = accu_ref[...].astype(dpu_ref.dtype)

        _conv_bwd_w(tg, dcg, dwg_ref, first)
        _conv_bwd_w(tu, dcu, dwu_ref, first)
        dbg_ref[...] = jnp.where(first, 0.0, dbg_ref[...]) + jnp.sum(dcg, axis=0, keepdims=True)
        dbu_ref[...] = jnp.where(first, 0.0, dbu_ref[...]) + jnp.sum(dcu, axis=0, keepdims=True)

    blk = pl.BlockSpec((S, cb), lambda j, b: (b, j))
    pblk = pl.BlockSpec((D, cb), lambda j, b: (0, j))
    wblk = pl.BlockSpec((kk, cb), lambda j, b: (0, j))
    bblk = pl.BlockSpec((1, cb), lambda j, b: (0, j))
    out_shape = (_sds((T, C), BF), _sds((T, C), BF), _sds((D, C), BF), _sds((D, C), BF),
                 _sds((kk, C)), _sds((kk, C)), _sds((1, C)), _sds((1, C)))
    return _pc(body, name="ffn_act_bwd", out_shape=out_shape, grid=(C // cb, Bl),
               in_specs=[blk, blk, wblk, wblk, bblk, bblk, pl.BlockSpec((S, D), lambda j, b: (b, 0)),
                         pl.BlockSpec((cb, D), lambda j, b: (j, 0)), pl.BlockSpec((D, S), lambda j, b: (0, b))],
               out_specs=(blk, blk, pblk, pblk, wblk, wblk, bblk, bblk),
               scratch=[pltpu.VMEM((D, cb), F32), pltpu.VMEM((D, cb), F32)], sem=("parallel", "arbitrary"))(
        ug, uu, wg, wu, bg, bu, dffnb, w_down, h1_t)


def _head(act, w_down, h1, h1b, w_gate, pb, w_proj, tgt, bg, g2, b2):
    T, D = h1.shape
    tm = _pick(T, (256,))

    def epi(ffn, xs, outs):
        h_ref, hb_ref, wg_ref, pb_ref, wp_ref, t_ref, bg_ref, g2_ref, b2_ref = xs
        acc_ref = outs[4]

        @pl.when(pl.program_id(0) == 0)
        def _():
            acc_ref[...] = jnp.zeros_like(acc_ref)

        h, tg = h_ref[...], t_ref[...]
        gpre = _dot(hb_ref[...], wg_ref[...], _NN)
        pp = _dot(pb_ref[...], wp_ref[...], _NN)

        def loss_fn(f, gp, p_, bgv, g2v, b2v):
            pre = ALPHA * h + f + jax.nn.sigmoid(gp + bgv) * p_
            err = _layernorm(pre, g2v, b2v) - tg
            return 0.5 * jnp.sum(jnp.mean(err * err, axis=-1, keepdims=True))

        loss, grads = jax.value_and_grad(loss_fn, argnums=(0, 1, 2, 3, 4, 5))(
            ffn, gpre, pp, bg_ref[...], g2_ref[...], b2_ref[...])
        outs[0][...] = grads[0]
        outs[1][...] = grads[0].astype(BF)
        outs[2][...] = grads[1].astype(BF)
        outs[3][...] = grads[2].astype(BF)
        acc_ref[0:1, :] += grads[3]
        acc_ref[1:2, :] += grads[4]
        acc_ref[2:3, :] += grads[5]
        acc_ref[3:4, :] += jnp.broadcast_to(loss, (1, D))

    row = pl.BlockSpec((tm, D), lambda i: (i, 0))
    par = pl.BlockSpec((1, D), lambda i: (0, 0))
    whole = lambda a: pl.BlockSpec(a.shape, lambda i: (0, 0))
    o_bf = _sds((T, D), BF)
    extras = [(h1, row), (h1b, row), (w_gate, whole(w_gate)), (pb, pl.BlockSpec((tm, pb.shape[1]), lambda i: (i, 0))),
              (w_proj, whole(w_proj)), (tgt, row), (bg, par), (g2, par), (b2, par)]
    return _mm_rows(act, w_down, name="mm_down_loss_head", tm=tm, sem="arbitrary", extras=extras,
                    out_shape=(_sds((T, D)), o_bf, o_bf, o_bf, _sds((8, D))),
                    out_specs=(row, row, row, row, pl.BlockSpec((8, D), lambda i: (0, 0))), epi=epi)


def _adamw(parts, w, m, v, name):
    R, C = w.shape
    tr = R if R <= 512 else _pick(R, (256,))

    def body(p_ref, w_ref, m_ref, v_ref, g_ref, d_ref, nm_ref, nv_ref):
        g = p_ref[0].astype(F32)
        for j in range(1, N_DEV):
            g = g + p_ref[j].astype(F32)
        g_ref[...] = g
        d_ref[...], nm_ref[...], nv_ref[...] = _adam_math(g, w_ref[...], m_ref[...], v_ref[...])

    blk = pl.BlockSpec((tr, C), lambda i: (i, 0))
    o = _sds((R, C))
    return _pc(body, name=name, out_shape=(o, o, o, o), grid=(R // tr,),
               in_specs=[pl.BlockSpec((N_DEV, tr, C), lambda i: (0, i, 0)), blk, blk, blk],
               out_specs=(blk, blk, blk, blk), sem=("parallel",))(parts, w, m, v)


def _adam_math(g, w, m, v):
    mm = ADAM_B1 * m + (1.0 - ADAM_B1) * g
    vv = ADAM_B2 * v + (1.0 - ADAM_B2) * jnp.square(g)
    m_hat = mm / (1.0 - ADAM_B1 ** ADAM_STEP)
    v_hat = vv / (1.0 - ADAM_B2 ** ADAM_STEP)
    return -ADAM_LR * (m_hat / (jnp.sqrt(v_hat) + ADAM_EPS) + ADAM_WD * w), mm, vv


def _small_layout(sizes):
    offs, off = {}, 0
    for n in SMALL_ORDER:
        offs[n] = off
        off += -(-sizes[n] // 128) * 128
    return offs, off


def _pack_small(d):
    return jnp.concatenate([_pad_lanes(d[n].reshape(1, -1), -(-d[n].size // 128) * 128) for n in SMALL_ORDER], axis=1)


def _adamw_small(parts, ws, ms, vs):
    k = len(SMALL_ORDER)
    sizes = {n: ws[n].shape[1] for n in SMALL_ORDER}
    offs, _ = _small_layout(sizes)

    def body(p_ref, *refs):
        ins, outs = refs[:3 * k], refs[3 * k:]
        for i, n in enumerate(SMALL_ORDER):
            lo, hi = offs[n], offs[n] + sizes[n]
            g = p_ref[0, :, lo:hi]
            for j in range(1, N_DEV):
                g = g + p_ref[j, :, lo:hi]
            d, mm, vv = _adam_math(g, ins[i][...], ins[k + i][...], ins[2 * k + i][...])
            outs[4 * i][...], outs[4 * i + 1][...], outs[4 * i + 2][...], outs[4 * i + 3][...] = g, d, mm, vv

    out_shape = tuple(_sds((1, sizes[n])) for n in SMALL_ORDER for _ in range(4))
    args = [ws[n] for n in SMALL_ORDER] + [ms[n] for n in SMALL_ORDER] + [vs[n] for n in SMALL_ORDER]
    res = _pc(body, name="adamw_small", out_shape=out_shape)(parts, *args)
    return {n: tuple(res[4 * i:4 * i + 4]) for i, n in enumerate(SMALL_ORDER)}


def _prep(x, p, xchg=None):
    T, D = x.shape
    Dp = p.shape[1]
    tt = _pick(T, (512, 256))

    def body(x_ref, p_ref, xb_ref, xt_ref, pb_ref, pt_ref):
        xb = x_ref[...].astype(BF)
        xb_ref[...] = xb
        xt_ref[...] = xb.T
        pb = p_ref[...].astype(BF)
        pb_ref[...] = pb
        pt_ref[...] = pb.T

    row = lambda w: pl.BlockSpec((tt, w), lambda i: (i, 0))
    col = lambda w: pl.BlockSpec((w, tt), lambda i: (0, i))
    return _pc(body, name="prep_inputs",
               out_shape=(_sds((T, D), BF), _sds((D, T), BF), _sds((T, Dp), BF), _sds((Dp, T), BF)),
               grid=(T // tt,), in_specs=[row(D), row(Dp)], out_specs=(row(D), col(D), row(Dp), col(Dp)),
               sem=("parallel",), xchg=xchg)(x, p)


SMALL_ORDER = ("gdn_a_log", "gdn_dt_bias", "gdn_norm_g", "mla_q_norm_g", "mla_kv_norm_g", "ln1_g", "ln1_b",
               "ffn_conv_b", "ple_b_gate", "ln2_g", "ln2_b")
EARLY = ("w_in", "gdn_conv_w")
LATE = ("mla_w_q_up", "mla_w_kv_up", "w_out", "ffn_w_up", "ffn_conv_w", "ffn_w_down", "ple_w_gate", "ple_w_proj")
GRADS_EARLY = ("w_out", "ffn_w_up", "ffn_conv_w", "ffn_w_down", "ple_w_gate", "ple_w_proj")
GRADS_LATE = ("w_in", "gdn_conv_w", "mla_w_q_up", "mla_w_kv_up")


def _pad_lanes(v, n=128):
    return jnp.pad(v, ((0, 0), (0, n - v.shape[1])))


def _local_step(x, p, tgt, W, sp, Bl, S, early_weights=None, late_weights=None, early_grads=None, late_grads=None):
    T = Bl * S
    W = dict(W)
    prep = _prep(x, p, xchg=None if early_weights is None else early_weights[:2])
    xb, xT, pb, pT = prep[:4]
    if early_weights is not None:
        W.update(early_weights[2](prep[4:]))
    w_in = W["w_in"]
    wqkv, wz = w_in[:, :1536], w_in[:, 1536:2048]
    z64 = jnp.zeros((w_in.shape[0], 64), w_in.dtype)
    z124 = jnp.zeros((w_in.shape[0], 124), w_in.dtype)
    wsm = jnp.concatenate([w_in[:, 2056:2440], w_in[:, 2696:2760], z64, w_in[:, 2440:2696],
                           w_in[:, 2048:2052], z124, w_in[:, 2052:2056], z124], axis=1)
    alog, dtb = _pad_lanes(sp["gdn_a_log"]), _pad_lanes(sp["gdn_dt_bias"])
    cos, sin = _rope_tables(S)

    qkv, qkvn = _gdn_proj_pre_fwd(xb, wqkv, W["gdn_conv_w"], Bl, S)
    z = _mm(xb, wz, "nn", name="mm_z")
    sm = _mm(xb, wsm, "nn", name="mm_sm")
    gdn_out = _gdn_fwd(qkvn, sm, alog, dtb, Bl, S, xchg=None if late_weights is None else late_weights[:2])
    o, tinv_s, sst_s, w_s, u_s, gc_s, beta_s = gdn_out[:7]
    if late_weights is not None:
        W.update(late_weights[2](gdn_out[7:]))
    wq = jnp.pad(W["mla_w_q_up"].reshape(-1, MLA_HEADS, 192), ((0, 0), (0, 0), (0, 64))).reshape(-1, 1024)
    wkv = W["mla_w_kv_up"]
    C = W["ffn_w_down"].shape[0]
    wup_g, wup_u = W["ffn_w_up"][:, :C], W["ffn_w_up"][:, C:]
    cw_g, cw_u = W["ffn_conv_w"][:, :C], W["ffn_conv_w"][:, C:]
    cb_g, cb_u = sp["ffn_conv_b"][:, :C], sp["ffn_conv_b"][:, C:]
    mixin, mixin_t = _gdn_post(o, z, sp["gdn_norm_g"], T)
    cqn, cqn_t = _rms_fwd(sm, sp["mla_q_norm_g"], 0, "rms_q_fwd")
    ckvn, ckvn_t = _rms_fwd(sm, sp["mla_kv_norm_g"], 2, "rms_kv_fwd")
    qcat = _qup_rope(cqn, wq, cos, sin, S)
    kv, kcat = _kvup_rope(ckvn, wkv, sm, cos, sin, S)
    mixin, lse, mixin_t = _attn_fwd(mixin, mixin_t, qcat, kcat, kv, Bl, S)
    mix, h1, h1b, h1T = _out_ln1(mixin, W["w_out"], x, sp["ln1_g"], sp["ln1_b"])
    ug, uu, act, act_t = _ffn_up_act_fwd(h1b, wup_g, wup_u, cw_g, cw_u, cb_g, cb_u, Bl, S)

    dffn, dffnb, dgpre, dpp, hacc = _head(act, W["ffn_w_down"], h1, h1b, W["ple_w_gate"], pb, W["ple_w_proj"], tgt,
                                          sp["ple_b_gate"], sp["ln2_g"], sp["ln2_b"])
    loss = hacc[3, 0]
    gW, gs = {}, {}
    gs["ple_b_gate"], gs["ln2_g"], gs["ln2_b"] = hacc[0:1], hacc[1:2], hacc[2:3]
    gW["ple_w_proj"] = _mm(pT, dpp, "nn", name="mm_dproj", out_dtype=BF)
    gW["ple_w_gate"] = _mm(h1T, dgpre, "nn", name="mm_dgate", out_dtype=BF)
    gW["ffn_w_down"] = _mm(act_t, dffnb, "nn", name="mm_ddown", out_dtype=BF)
    du_g, du_u, dwp_g, dwp_u, dcw_g, dcw_u, dcb_g, dcb_u = _ffn_act_bwd(
        ug, uu, cw_g, cw_u, cb_g, cb_u, dffnb, W["ffn_w_down"], h1T, Bl, S)
    gW["ffn_conv_w"] = jnp.concatenate([dcw_g, dcw_u], axis=1)
    gs["ffn_conv_b"] = jnp.concatenate([dcb_g, dcb_u], axis=1)
    gW["ffn_w_up"] = jnp.concatenate([dwp_g, dwp_u], axis=1)
    dxa, dmix, acc1 = _dh1_ln1_bwd(du_u, wup_u, du_g, wup_g, dgpre, W["ple_w_gate"], dffn, x, mix,
                                   sp["ln1_g"], sp["ln1_b"])
    gs["ln1_g"], gs["ln1_b"] = acc1[0:1], acc1[1:2]
    gW["w_out"] = _mm(mixin_t, dmix, "nn", name="mm_dwout", out_dtype=BF)
    dmixin = _mm(dmix, W["w_out"], "nt", name="mm_dmixin")
    do, dcat, gacc = _gdn_post_bwd(o, z, sp["gdn_norm_g"], dmixin, T)
    gs["gdn_norm_g"] = gacc[0:1]
    bwd_out = _gdn_bwd(qkvn, sm, alog, dtb, gc_s, beta_s, tinv_s, sst_s, w_s, u_s, do, dcat, Bl, S,
                       xchg=None if early_grads is None else early_grads({n: gW[n] for n in GRADS_EARLY}))
    dqkvn, dcat, cacc = bwd_out[:3]
    early_recv = bwd_out[3:]
    gs["gdn_a_log"], gs["gdn_dt_bias"] = cacc[0:1, :GDN_HEADS], cacc[1:2, :GDN_HEADS]
    dcat, gW["gdn_conv_w"] = _gdn_pre_bwd(qkv, W["gdn_conv_w"], dqkvn, dcat, Bl, S)
    dqc, dkv, dkrr = _attn_bwd(qcat, kcat, kv, lse, dmixin, Bl, S)
    dqraw, dcat = _rope_bwd(dqc, dcat, dkrr, cos, sin, S)
    gwq = _mm(cqn_t, dqraw, "nn", name="mm_dwq", out_dtype=BF)
    gW["mla_w_q_up"] = gwq.reshape(-1, MLA_HEADS, 256)[:, :, :192].reshape(-1, MLA_HEADS * 192)
    gW["mla_w_kv_up"] = _mm(ckvn_t, dkv, "nn", name="mm_dwkv", out_dtype=BF)
    dcqn = _mm(dqraw, wq, "nt", name="mm_dcqn")
    dckvn = _mm(dkv, wkv, "nt", name="mm_dckvn")
    dcat, qacc = _rms_bwd(sm, sp["mla_q_norm_g"], dcqn, dcat, 0, "rms_q_bwd")
    dcat, kacc = _rms_bwd(sm, sp["mla_kv_norm_g"], dckvn, dcat, 2, "rms_kv_bwd")
    gs["mla_q_norm_g"], gs["mla_kv_norm_g"] = qacc[0:1], kacc[0:1]
    gcat = _mm(xT, dcat, "nn", name="mm_dwin", out_dtype=BF)
    gsm, gqkv, gz = gcat[:, :DCAT_QKV], gcat[:, DCAT_QKV:DCAT_Z], gcat[:, DCAT_Z:]
    gW["w_in"] = jnp.concatenate([gqkv, gz, gsm[:, 768:772], gsm[:, 896:900], gsm[:, 0:384], gsm[:, 512:768],
                                  gsm[:, 384:448]], axis=1)
    wcat = jnp.concatenate([wsm, wqkv, wz], axis=1)
    dx = _mm(dcat, wcat, "nt", name="mm_dx", add=dxa, xchg=None if late_grads is None else late_grads(gW, gs))
    late_recv = ()
    if late_grads is not None:
        dx, late_recv = dx[0], dx[1:]
    return loss, dx, gW, gs, early_recv, late_recv


COL_SHARDED = ("w_in", "mla_w_q_up", "mla_w_kv_up", "ffn_w_up", "ple_w_proj", "gdn_conv_w", "ffn_conv_w")
SHARDED = EARLY + LATE
WEIGHTS = ("w_in", "gdn_conv_w", "gdn_a_log", "gdn_dt_bias", "gdn_norm_g", "mla_q_norm_g", "mla_w_q_up",
           "mla_kv_norm_g", "mla_w_kv_up", "w_out", "ln1_g", "ln1_b", "ffn_w_up", "ffn_conv_w", "ffn_conv_b",
           "ffn_w_down", "ple_w_gate", "ple_b_gate", "ple_w_proj", "ln2_g", "ln2_b")
CONV = ("gdn_conv_w", "ffn_conv_w")


def _gathered_to_full(name, g):
    if name in COL_SHARDED:
        return jnp.transpose(g, (1, 0, 2)).reshape(g.shape[1], -1)
    return g.reshape(-1, g.shape[-1])


def _full_to_blocks(name, gfull, shard_shape):
    r, c = shard_shape
    if name in COL_SHARDED:
        return jnp.transpose(gfull.reshape(r, N_DEV, c), (1, 0, 2))
    return gfull.reshape(N_DEV, r, c)


def kernel(x, p, w_in, gdn_conv_w, gdn_a_log, gdn_dt_bias, gdn_norm_g, mla_q_norm_g, mla_w_q_up, mla_kv_norm_g, mla_w_kv_up, w_out, ln1_g, ln1_b, ffn_w_up, ffn_conv_w, ffn_conv_b, ffn_w_down, ple_w_gate, ple_b_gate, ple_w_proj, ln2_g, ln2_b, loss_target, m_w_in, m_gdn_conv_w, m_gdn_a_log, m_gdn_dt_bias, m_gdn_norm_g, m_mla_q_norm_g, m_mla_w_q_up, m_mla_kv_norm_g, m_mla_w_kv_up, m_w_out, m_ln1_g, m_ln1_b, m_ffn_w_up, m_ffn_conv_w, m_ffn_conv_b, m_ffn_w_down, m_ple_w_gate, m_ple_b_gate, m_ple_w_proj, m_ln2_g, m_ln2_b, v_w_in, v_gdn_conv_w, v_gdn_a_log, v_gdn_dt_bias, v_gdn_norm_g, v_mla_q_norm_g, v_mla_w_q_up, v_mla_kv_norm_g, v_mla_w_kv_up, v_w_out, v_ln1_g, v_ln1_b, v_ffn_w_up, v_ffn_conv_w, v_ffn_conv_b, v_ffn_w_down, v_ple_w_gate, v_ple_b_gate, v_ple_w_proj, v_ln2_g, v_ln2_b):
    loc = dict(locals())
    wts = {n: loc[n] for n in WEIGHTS}
    ms = {n: loc["m_" + n] for n in WEIGHTS}
    vs = {n: loc["v_" + n] for n in WEIGHTS}
    Bl, S, D = x.shape
    T = Bl * S

    wire = lambda n: wts[n][0] if n in CONV else wts[n][0].astype(BF)
    gather = lambda names, mode: ([wire(n) for n in names], [mode] * len(names),
                                  lambda res: {n: _gathered_to_full(n, g) for n, g in zip(names, res)})
    blocks = lambda g, names: [_full_to_blocks(n, g[n], wts[n].shape[1:]) for n in names]
    pack_early = lambda g: (blocks(g, GRADS_EARLY), ["a2a"] * len(GRADS_EARLY))
    pack_late = lambda g, gs: (blocks(g, GRADS_LATE) + [_pack_small(gs)], ["a2a"] * len(GRADS_LATE) + ["ag"])
    sp = {n: wts[n].reshape(1, -1) for n in SMALL_ORDER}

    loss, dx, gW, gs, early_recv, late_recv = _local_step(
        x.reshape(T, D), p.reshape(T, -1), loss_target.reshape(T, D), {}, sp, Bl, S,
        early_weights=gather(EARLY, "ag2"), late_weights=gather(LATE, "ag"), early_grads=pack_early,
        late_grads=pack_late)

    res = {}
    for n, parts in list(zip(GRADS_EARLY, early_recv)) + list(zip(GRADS_LATE, late_recv[:-1])):
        res[n] = tuple(t[None] for t in _adamw(parts, wts[n][0], ms[n][0], vs[n][0], "adamw_" + n))
    res.update(_adamw_small(late_recv[-1], wts, ms, vs))

    loss = lax.psum(loss, ("x", "y", "c"))
    outs = [loss, dx.reshape(Bl, S, D)]
    for k in range(4):
        outs += [res[n][k] for n in WEIGHTS]
    return tuple(outs)
```

```python
import math

import jax
import jax.numpy as jnp
from jax import lax
from jax.experimental import pallas as pl
from jax.experimental.pallas import tpu as pltpu

F32 = jnp.float32
BF = jnp.bfloat16
_MXU = jnp.bfloat16
_INTERPRET = None
_VMEM_LIMIT = 56 * 1024 * 1024
HI = lax.Precision.HIGHEST

N_DEV = 8
CHUNK = 64
GDN_HEADS = 4
GDN_DK = 128
MLA_HEADS = 4
MLA_NOPE = 128
MLA_ROPE = 64
ROPE_THETA = 10000.0
ALPHA = 2.0 ** 0.25
NORM_EPS = 1e-6
ATTN_SCALE = (MLA_NOPE + MLA_ROPE) ** -0.5
DCAT_QKV, DCAT_Z, DCAT_W = 1024, 2560, 3072
ADAM_LR, ADAM_B1, ADAM_B2, ADAM_EPS, ADAM_WD, ADAM_STEP = 0.001, 0.9, 0.999, 1e-08, 0.01, 10
MESH = pl.DeviceIdType.MESH
ANY = pl.BlockSpec(memory_space=pl.ANY)
_NN = (((1,), (0,)), ((), ()))
_NT = (((1,), (1,)), ((), ()))
_TN = (((0,), (0,)), ((), ()))


def _sds(shape, dtype=F32):
    return jax.ShapeDtypeStruct(tuple(shape), dtype)


def _pick(n, cands):
    for c in cands:
        if n % c == 0:
            return c
    return n


def _xchg_out_shapes(arrs, modes):
    return [_sds(a.shape if m == "a2a" else (N_DEV,) + a.shape, a.dtype) for a, m in zip(arrs, modes)]


def _xchg_scratch(n):
    return [pltpu.SemaphoreType.DMA((n, N_DEV - 1)), pltpu.SemaphoreType.DMA((n, N_DEV - 1)),
            pltpu.SemaphoreType.DMA((n,))]


def _xchg_plan(ins, outs, send, recv, loc, modes):
    x, y, c = lax.axis_index("x"), lax.axis_index("y"), lax.axis_index("c")
    me = 4 * x + 2 * y + c
    starts, relays, waits = [], [], []
    for ai, mode in enumerate(modes):
        src_all, out = ins[ai], outs[ai]

        def remote(k, src, dst, dev):
            return pltpu.make_async_remote_copy(src_ref=src, dst_ref=dst, send_sem=send.at[ai, k], recv_sem=recv.at[ai, k],
                                                device_id=dev, device_id_type=MESH)

        own = pltpu.make_async_copy(src_all.at[me] if mode == "a2a" else src_all, out.at[me], loc.at[ai])
        starts.append(own)
        waits.append(own.wait)
        if mode == "ag2":
            sib = (x, y, 1 - c)
            chips = [(1 - x, y), (x, 1 - y), (1 - x, 1 - y)]
            first = [remote(0, src_all, out.at[me], sib)]
            first += [remote(1 + k, src_all, out.at[me], (px, py, c)) for k, (px, py) in enumerate(chips)]
            starts += first
            waits += [cp.wait_send for cp in first]
            waits.append(remote(0, src_all, out.at[4 * x + 2 * y + 1 - c], sib).wait_recv)
            for k, (px, py) in enumerate(chips):
                same, other = out.at[4 * px + 2 * py + c], out.at[4 * px + 2 * py + 1 - c]
                relay = remote(4 + k, same, same, sib)
                relays.append((remote(1 + k, src_all, same, (px, py, c)), relay))
                waits += [relay.wait_send, remote(4 + k, src_all, other, sib).wait_recv]
            continue
        for r in range(1, N_DEV):
            px = 1 - x if (r >> 2) & 1 else x
            py = 1 - y if (r >> 1) & 1 else y
            pc = 1 - c if r & 1 else c
            cp = remote(r - 1, src_all.at[4 * px + 2 * py + pc] if mode == "a2a" else src_all, out.at[me], (px, py, pc))
            starts.append(cp)
            waits.append(cp.wait)
    return starts, relays, waits


def _pc(body, *, name, out_shape, grid=None, in_specs=None, out_specs=None, scratch=(), sem=None, aliases=None,
        xchg=None):
    kw = {}
    if _INTERPRET is not None:
        kw["interpret"] = _INTERPRET
    single = not isinstance(out_shape, (tuple, list))
    out_shape = [out_shape] if single else list(out_shape)
    if out_specs is not None:
        out_specs = [out_specs] if single else list(out_specs)
    scratch = list(scratch)
    if xchg is not None:
        xarrs, xmodes = xchg
        n_in, n_out, n_scr, nx = len(in_specs), len(out_shape), len(scratch), len(xarrs)
        single = False
        inner = body

        def body(*refs):
            ins, xins = refs[:n_in], refs[n_in:n_in + nx]
            outs = refs[n_in + nx:n_in + nx + n_out]
            xouts = refs[n_in + nx + n_out:n_in + 2 * nx + n_out]
            scr = refs[n_in + 2 * nx + n_out:n_in + 2 * nx + n_out + n_scr]
            send, recv, loc = refs[n_in + 2 * nx + n_out + n_scr:]
            first = last = None
            for d, g in enumerate(grid):
                f, l = pl.program_id(d) == 0, pl.program_id(d) == g - 1
                first, last = (f, l) if first is None else (first & f, last & l)

            @pl.when(first)
            def _():
                for cp in _xchg_plan(xins, xouts, send, recv, loc, xmodes)[0]:
                    cp.start()

            inner(*ins, *outs, *scr)

            @pl.when(last)
            def _():
                _, relays, waits = _xchg_plan(xins, xouts, send, recv, loc, xmodes)
                for arrival, relay in relays:
                    arrival.wait_recv()
                    relay.start()
                for wait in waits:
                    wait()

        in_specs = list(in_specs) + [ANY] * nx
        out_specs = out_specs + [ANY] * nx
        out_shape = out_shape + _xchg_out_shapes(xarrs, xmodes)
        scratch = scratch + _xchg_scratch(nx)
        sem = ("arbitrary",) * len(grid)
    if grid is not None:
        kw["grid"] = grid
    if in_specs is not None:
        kw["in_specs"] = in_specs
    if out_specs is not None:
        kw["out_specs"] = out_specs[0] if single else tuple(out_specs)
    cp = dict(vmem_limit_bytes=_VMEM_LIMIT)
    if sem is not None:
        cp["dimension_semantics"] = sem
    call = pl.pallas_call(body, name=name, out_shape=out_shape[0] if single else tuple(out_shape),
                          scratch_shapes=scratch, input_output_aliases=aliases or {},
                          compiler_params=pltpu.CompilerParams(**cp), **kw)
    if xchg is None:
        return call
    return lambda *ins: call(*ins, *xchg[0])


_FULL_K_MAX = 3072


def _mm(a, b, mode, *, name, add=None, add_scale=1.0, out_dtype=F32, xchg=None):
    (M, K) = a.shape
    (K2, N) = b.shape if mode == "nn" else b.shape[::-1]
    assert K == K2, (a.shape, b.shape, mode)
    tm = _pick(M, (1024, 1408, 512, 384, 256, 128))
    tn = _pick(N, (1024, 2816, 1408, 768, 512, 384, 256, 128))
    tk = K if K <= _FULL_K_MAX else _pick(K, (2048, 1408, 1024, 512) if tn <= 1408 else (1024, 512))
    nk = K // tk
    dims = _NN if mode == "nn" else _NT
    has_add = add is not None

    def finish(r, add_ref, o_ref):
        if has_add:
            r = r + add_scale * add_ref[...].astype(F32)
        o_ref[...] = r.astype(out_dtype)

    if nk == 1:
        def body(a_ref, b_ref, *rest):
            r = lax.dot_general(a_ref[...].astype(_MXU), b_ref[...].astype(_MXU), dims, preferred_element_type=F32)
            finish(r, rest[0] if has_add else None, rest[-1])

        a_spec = pl.BlockSpec((tm, K), lambda i, j: (i, 0))
        b_spec = (pl.BlockSpec((K, tn), lambda i, j: (0, j)) if mode == "nn"
                  else pl.BlockSpec((tn, K), lambda i, j: (j, 0)))
        o_spec = pl.BlockSpec((tm, tn), lambda i, j: (i, j))
        grid, sem, scratch = (M // tm, N // tn), ("parallel", "parallel"), []
    else:
        def body(a_ref, b_ref, *rest):
            o_ref, acc = rest[-2], rest[-1]
            k = pl.program_id(2)

            @pl.when(k == 0)
            def _():
                acc[...] = jnp.zeros_like(acc)

            acc[...] += lax.dot_general(a_ref[...].astype(_MXU), b_ref[...].astype(_MXU), dims,
                                        preferred_element_type=F32)

            @pl.when(k == nk - 1)
            def _():
                finish(acc[...], rest[0] if has_add else None, o_ref)

        a_spec = pl.BlockSpec((tm, tk), lambda i, j, k: (i, k))
        b_spec = (pl.BlockSpec((tk, tn), lambda i, j, k: (k, j)) if mode == "nn"
                  else pl.BlockSpec((tn, tk), lambda i, j, k: (j, k)))
        o_spec = pl.BlockSpec((tm, tn), lambda i, j, k: (i, j))
        grid, sem, scratch = (M // tm, N // tn, nk), ("parallel", "parallel", "arbitrary"), [pltpu.VMEM((tm, tn), F32)]

    ins = [a, b] + ([add] if has_add else [])
    specs = [a_spec, b_spec] + ([o_spec] if has_add else [])
    return _pc(body, name=name, out_shape=_sds((M, N), out_dtype), grid=grid, in_specs=specs, out_specs=o_spec,
               scratch=scratch, sem=sem, xchg=xchg)(*ins)


def _dot(a, b, dims):
    return lax.dot_general(a.astype(_MXU), b.astype(_MXU), dims, preferred_element_type=F32)


def _split(a):
    hi = a.astype(BF)
    lo = (a - hi.astype(F32)).astype(BF)
    return hi, lo


def _dot3(a, b, dims):
    ah, al = _split(a)
    bh, bl = _split(b)
    d = lambda u, v: lax.dot_general(u, v, dims, preferred_element_type=F32)
    return d(ah, bh) + (d(ah, bl) + d(al, bh))


def _softplus(x):
    return jnp.maximum(x, 0.0) + jnp.log1p(jnp.exp(-jnp.abs(x)))


def _silu(x):
    return x * jax.nn.sigmoid(x)


def _rmsnorm(x, g):
    return x * lax.rsqrt(jnp.mean(x * x, axis=-1, keepdims=True) + NORM_EPS) * g


def _layernorm(x, g, b):
    mu = jnp.mean(x, axis=-1, keepdims=True)
    xc = x - mu
    var = jnp.mean(xc * xc, axis=-1, keepdims=True)
    return xc * lax.rsqrt(var + NORM_EPS) * g + b


def _shift_dn(x, s):
    if s == 0:
        return x
    t = lax.broadcasted_iota(jnp.int32, x.shape, 0)
    return jnp.where(t >= s, pltpu.roll(x, s, 0), 0.0)


def _shift_up(x, s):
    if s == 0:
        return x
    n = x.shape[0]
    t = lax.broadcasted_iota(jnp.int32, x.shape, 0)
    return jnp.where(t < n - s, pltpu.roll(x, n - s, 0), 0.0)


def _taps(x, kk):
    return [_shift_dn(x, kk - 1 - j) for j in range(kk)]


def _conv(x, w, taps=None):
    kk = w.shape[0]
    taps = _taps(x, kk) if taps is None else taps
    y = w[kk - 1:kk, :] * taps[kk - 1]
    for j in range(kk - 1):
        y = y + w[j:j + 1, :] * taps[j]
    return y


def _conv_bwd_x(dy, w):
    kk = w.shape[0]
    dx = w[kk - 1:kk, :] * dy
    for j in range(kk - 1):
        dx = dx + w[j:j + 1, :] * _shift_up(dy, kk - 1 - j)
    return dx


def _conv_bwd_w(taps, dy, dw_ref, first):
    kk = dw_ref.shape[0]
    for j in range(kk):
        r = jnp.sum(dy * taps[j], axis=0, keepdims=True)
        prev = jnp.where(first, 0.0, dw_ref[j:j + 1, :])
        dw_ref[j:j + 1, :] = prev + r


def _gdn_post_conv(c, j):
    h = _silu(c)
    hn = h * lax.rsqrt(jnp.sum(h * h, axis=-1, keepdims=True) + NORM_EPS)
    return jnp.where(j < GDN_HEADS, hn * (GDN_DK ** -0.5), jnp.where(j < 2 * GDN_HEADS, hn, h))


def _gdn_proj_pre_fwd(xb, wqkv, conv_w, Bl, S):
    T, D = xb.shape
    C = wqkv.shape[1]
    nj = C // 256

    def body(x_ref, p_ref, w_ref, qkv_ref, o_ref):
        j = pl.program_id(1)
        qkv = _dot(x_ref[...], p_ref[...], _NN)
        qkv_ref[...] = qkv
        c = _conv(qkv, w_ref[...])
        for half in range(2):
            o_ref[:, 128 * half:128 * (half + 1)] = _gdn_post_conv(c[:, 128 * half:128 * (half + 1)], 2 * j)

    blk = pl.BlockSpec((S, 256), lambda b, j: (b, j))
    return _pc(body, name="gdn_proj_pre_fwd", out_shape=(_sds((T, C)), _sds((T, C))), grid=(Bl, nj),
               in_specs=[pl.BlockSpec((S, D), lambda b, j: (b, 0)), pl.BlockSpec((D, 256), lambda b, j: (0, j)),
                         pl.BlockSpec((conv_w.shape[0], 256), lambda b, j: (0, j))],
               out_specs=(blk, blk), sem=("parallel", "parallel"))(xb, wqkv, conv_w)


def _gdn_pre_bwd(qkv, conv_w, dout, dcat, Bl, S):
    T, C = qkv.shape
    nj = C // 128
    kk = conv_w.shape[0]

    def body(x_ref, w_ref, d_ref, alias_ref, dx_ref, dw_ref):
        j, b = pl.program_id(0), pl.program_id(1)
        w = w_ref[...]
        taps = _taps(x_ref[...], kk)
        c = _conv(None, w, taps)
        _, vjp = jax.vjp(lambda u: _gdn_post_conv(u, j), c)
        (dc,) = vjp(d_ref[...])
        dx_ref[...] = _conv_bwd_x(dc, w).astype(dx_ref.dtype)
        _conv_bwd_w(taps, dc, dw_ref, b == 0)

    blk = pl.BlockSpec((S, 128), lambda j, b: (b, j))
    wblk = pl.BlockSpec((kk, 128), lambda j, b: (0, j))
    return _pc(body, name="gdn_pre_bwd", out_shape=(_sds(dcat.shape, dcat.dtype), _sds((kk, C))), grid=(nj, Bl),
               in_specs=[blk, wblk, blk, ANY],
               out_specs=(pl.BlockSpec((S, 128), lambda j, b: (b, DCAT_QKV // 128 + j)), wblk),
               aliases={3: 0}, sem=("parallel", "arbitrary"))(qkv, conv_w, dout, dcat)


_H = GDN_HEADS
_HC = _H * CHUNK


def _st(x):
    return jnp.concatenate([x[:, h * 128:(h + 1) * 128] for h in range(_H)], axis=0)


def _unst(x):
    return jnp.concatenate([x[h * CHUNK:(h + 1) * CHUNK] for h in range(_H)], axis=1)


def _stc(t):
    return jnp.concatenate([t[:, h:h + 1] for h in range(_H)], axis=0)


def _untile(col):
    lane = lax.broadcasted_iota(jnp.int32, (CHUNK, 128), 1)
    out = jnp.zeros((CHUNK, 128), F32)
    for h in range(_H):
        out = out + jnp.where(lane == h, col[h * CHUNK:(h + 1) * CHUNK], 0.0)
    return out


def _rowform(col):
    return jnp.broadcast_to(col, (_HC, 128)).T[0:1, :]


def _tri(n, upper=False):
    i = lax.broadcasted_iota(jnp.int32, (n, n), 0)
    j = lax.broadcasted_iota(jnp.int32, (n, n), 1)
    return jnp.where((j >= i) if upper else (j <= i), 1.0, 0.0).astype(F32)


def _gdn_chunk_common(q, k, v, gc, beta):
    Q, K, V = _st(q), _st(k), _st(v)
    B, GC = _stc(beta), _stc(gc)
    GL = jnp.concatenate([jnp.broadcast_to(gc[CHUNK - 1:CHUNK, h:h + 1], (CHUNK, 1)) for h in range(_H)], axis=0)
    ii = lax.broadcasted_iota(jnp.int32, (_HC, _HC), 0)
    jj = lax.broadcasted_iota(jnp.int32, (_HC, _HC), 1)
    same = (ii >> 6) == (jj >> 6)
    incl = same & (ii >= jj)
    strict = same & (ii > jj)
    diff = GC - _rowform(GC)
    D = jnp.where(incl, jnp.exp(jnp.where(incl, diff, 0.0)), 0.0)
    KB = K * B
    A = jnp.where(strict, _dot(KB, K, _NT) * D, 0.0)
    P = jnp.where(incl, _dot(Q, K, _NT) * D, 0.0)
    EG = jnp.exp(GC)
    ED = jnp.exp(GL - GC)
    return dict(Q=Q, K=K, V=V, B=B, GC=GC, GL=GL, incl=incl, strict=strict, D=D, KB=KB, A=A, P=P, EG=EG, ED=ED,
                QG=Q * EG, KD=K * ED)


def _neumann_inv(A):
    n = A.shape[0]
    i = lax.broadcasted_iota(jnp.int32, (n, n), 0)
    j = lax.broadcasted_iota(jnp.int32, (n, n), 1)
    N = -A
    acc = jnp.where(i == j, 1.0, 0.0) + N
    Pw = N
    for _ in range(5):
        Pw = _dot3(Pw, Pw, _NN)
        acc = acc + _dot3(acc, Pw, _NN)
    return acc


def _gates(a, alog, dtb):
    return -jnp.exp(alog) * _softplus(a + dtb)


def _hs(x, h, n=CHUNK):
    return x[h * n:(h + 1) * n]


def _per_step(Bl):
    return 4 if Bl % 4 == 0 else 2 if Bl % 2 == 0 else 1


def _gdn_fwd(qkvn, sm, alog, dtb, Bl, S, xchg=None):
    T = Bl * S
    nC = S // CHUNK
    E = _per_step(Bl)

    def body(*refs):
        S_ref = refs[-1]

        @pl.when(pl.program_id(1) == 0)
        def _():
            S_ref[...] = jnp.zeros_like(S_ref)

        for e in range(E):
            one(*[r if i in (5, 6) else r.at[e] for i, r in enumerate(refs)])

    def one(q_ref, k_ref, v_ref, a_ref, b_ref, alog_ref, dtb_ref,
            o_ref, tinv_ref, sst_ref, w_ref, u_ref, gc_ref, beta_ref, S_ref):
        g = _gates(a_ref[...], alog_ref[...], dtb_ref[...])
        beta = jax.nn.sigmoid(b_ref[...])
        gc = jnp.dot(_tri(CHUNK), g, precision=HI, preferred_element_type=F32)
        cm = _gdn_chunk_common(q_ref[...], k_ref[...], v_ref[...], gc, beta)
        tinv = _neumann_inv(cm["A"])
        WU = _dot3(tinv, jnp.concatenate([cm["KB"] * cm["EG"], cm["V"] * cm["B"]], axis=1), _NN)
        W, U = WU[:, :128], WU[:, 128:]
        s_old = [S_ref[h * 128:(h + 1) * 128, :] for h in range(_H)]
        vn = [_hs(U, h) - _dot(_hs(W, h), s_old[h], _NN) for h in range(_H)]
        o_intra = _dot(cm["P"], jnp.concatenate(vn, axis=0), _NN)
        outs = []
        for h in range(_H):
            outs.append(_dot(_hs(cm["QG"], h), s_old[h], _NN) + _hs(o_intra, h))
            gl = jnp.exp(gc[CHUNK - 1:CHUNK, h:h + 1])
            S_ref[h * 128:(h + 1) * 128, :] = gl * s_old[h] + _dot(_hs(cm["KD"], h), vn[h], _TN)
            sst_ref[h * 128:(h + 1) * 128, :] = s_old[h]
        o_ref[...] = jnp.concatenate(outs, axis=1)
        tinv_ref[...] = tinv
        w_ref[...] = _unst(W)
        u_ref[...] = _unst(U)
        gc_ref[...] = gc
        beta_ref[...] = beta

    blk = lambda r, w, c: pl.BlockSpec((E, r, w), lambda b, n: (b, n, c))
    par = pl.BlockSpec((1, 128), lambda b, n: (0, 0))
    out_shape = (_sds((Bl, S, 512)), _sds((Bl, nC * _HC, _HC)), _sds((Bl, nC * 512, 128)),
                 _sds((Bl, S, 512)), _sds((Bl, S, 512)), _sds((Bl, S, 128)), _sds((Bl, S, 128)))
    out_specs = (blk(CHUNK, 512, 0), blk(_HC, _HC, 0), blk(512, 128, 0), blk(CHUNK, 512, 0), blk(CHUNK, 512, 0),
                 blk(CHUNK, 128, 0), blk(CHUNK, 128, 0))
    q3, s3 = qkvn.reshape(Bl, S, -1), sm.reshape(Bl, S, -1)
    res = _pc(body, name="gdn_chunk_fwd", out_shape=out_shape, grid=(Bl // E, nC),
              in_specs=[blk(CHUNK, 512, 0), blk(CHUNK, 512, 1), blk(CHUNK, 512, 2), blk(CHUNK, 128, 6),
                        blk(CHUNK, 128, 7), par, par],
              out_specs=out_specs, scratch=[pltpu.VMEM((E, 512, 128), F32)], sem=("parallel", "arbitrary"),
              xchg=xchg)(q3, q3, q3, s3, s3, alog, dtb)
    return (res[0].reshape(T, 512),) + tuple(res[1:])


def _gdn_bwd(qkvn, sm, alog, dtb, gc_s, beta_s, tinv_s, sst_s, w_s, u_s, do, dcat, Bl, S, xchg=None):
    T = Bl * S
    nC = S // CHUNK
    E = _per_step(Bl)

    def body(*refs):
        acc_ref, dS_ref = refs[-2], refs[-1]
        first_chunk = pl.program_id(1) == 0

        @pl.when(first_chunk)
        def _():
            dS_ref[...] = jnp.zeros_like(dS_ref)

        @pl.when(first_chunk & (pl.program_id(0) == 0))
        def _():
            acc_ref[...] = jnp.zeros_like(acc_ref)

        for e in range(E):
            one(*[r if i in (4, 5, 13, 16) else r.at[e] for i, r in enumerate(refs)])

    def one(q_ref, k_ref, v_ref, a_ref, alog_ref, dtb_ref, gc_ref, beta_ref, tinv_ref, sst_ref, w_ref, u_ref,
            do_ref, alias_ref, dqkv_ref, dsm_ref, acc_ref, dS_ref):
        gc, beta = gc_ref[...], beta_ref[...]
        cm = _gdn_chunk_common(q_ref[...], k_ref[...], v_ref[...], gc, beta)
        Q, K, V, B, D, KB, A, P = (cm[n] for n in ("Q", "K", "V", "B", "D", "KB", "A", "P"))
        EG, ED, QG, KD = cm["EG"], cm["ED"], cm["QG"], cm["KD"]
        tinv = tinv_ref[...]
        W, U, DO = _st(w_ref[...]), _st(u_ref[...]), _st(do_ref[...])
        s_old = [sst_ref[h * 128:(h + 1) * 128, :] for h in range(_H)]
        ds_new = [dS_ref[h * 128:(h + 1) * 128, :] for h in range(_H)]
        VN = jnp.concatenate([_hs(U, h) - _dot(_hs(W, h), s_old[h], _NN) for h in range(_H)], axis=0)
        dP = jnp.where(cm["incl"], _dot(DO, VN, _NT), 0.0)
        dVN0 = _dot(P, DO, _TN)
        dVN, dQG, dKD, dW, TL = [], [], [], [], []
        for h in range(_H):
            gl = jnp.exp(gc[CHUNK - 1:CHUNK, h:h + 1])
            dvn = _hs(dVN0, h) + _dot(_hs(KD, h), ds_new[h], _NN)
            dkd = _dot(_hs(VN, h), ds_new[h], _NT)
            dVN.append(dvn)
            dQG.append(_dot(_hs(DO, h), s_old[h], _NT))
            dKD.append(dkd)
            dW.append(-_dot(dvn, s_old[h], _NT))
            dgl = jnp.sum(jnp.sum(ds_new[h] * s_old[h], axis=1, keepdims=True), axis=0, keepdims=True)
            tl = jnp.sum(jnp.sum(dkd * _hs(KD, h), axis=1, keepdims=True), axis=0, keepdims=True) + dgl * gl
            TL.append(jnp.broadcast_to(tl, (CHUNK, 1)))
            dS_ref[h * 128:(h + 1) * 128, :] = (_dot(_hs(QG, h), _hs(DO, h), _TN) + gl * ds_new[h]
                                                - _dot(_hs(W, h), dvn, _TN))
        dVN, dQG, dKD, dW, TL = (jnp.concatenate(z, axis=0) for z in (dVN, dQG, dKD, dW, TL))
        dB = _dot3(tinv, jnp.concatenate([dVN, dW], axis=1), _TN)
        dVB, dKBE = dB[:, :128], dB[:, 128:]
        dA = jnp.where(cm["strict"], -_dot3(dB, jnp.concatenate([U, W], axis=1), _NT), 0.0)
        dG = dA * D
        dQK = dP * D
        dKB = _dot(dG, K, _NN) + dKBE * EG
        dK = (_dot(jnp.concatenate([dG, dQK], axis=0), jnp.concatenate([KB, Q], axis=0), _TN)
              + dKB * B + dKD * ED)
        dQ = _dot(dQK, K, _NN) + dQG * EG
        Mx = dA * A + dP * P
        rs = lambda z: jnp.sum(z, axis=1, keepdims=True)
        ri = lax.broadcasted_iota(jnp.int32, (_HC, 1), 0)
        dGC = (rs(Mx) - rs(Mx.T) + rs(dKBE * KB * EG) + rs(dQG * QG) - rs(dKD * KD)
               + jnp.where((ri & (CHUNK - 1)) == CHUNK - 1, TL, 0.0))
        dBeta = rs(dVB * V) + rs(dKB * K)
        dqkv_ref[:, 0:512] = _unst(dQ)
        dqkv_ref[:, 512:1024] = _unst(dK)
        dqkv_ref[:, 1024:1536] = _unst(dVB * B)
        dg = jnp.dot(_tri(CHUNK, upper=True), _untile(dGC), precision=HI, preferred_element_type=F32)
        a, alog_v, dtb_v = a_ref[...], alog_ref[...], dtb_ref[...]
        g = _gates(a, alog_v, dtb_v)
        lane = lax.broadcasted_iota(jnp.int32, (CHUNK, 128), 1)
        valid = lane < _H
        da = jnp.where(valid, dg * (-jnp.exp(alog_v)) * jax.nn.sigmoid(a + dtb_v), 0.0)
        dsm_ref[:, 0:128] = da.astype(dsm_ref.dtype)
        dsm_ref[:, 128:256] = jnp.where(valid, _untile(dBeta) * beta * (1.0 - beta), 0.0).astype(dsm_ref.dtype)
        acc_ref[0:1, :] += jnp.sum(jnp.where(valid, dg * g, 0.0), axis=0, keepdims=True)
        acc_ref[1:2, :] += jnp.sum(da, axis=0, keepdims=True)

    blk = lambda r, w, c: pl.BlockSpec((E, r, w), lambda b, n: (b, nC - 1 - n, c))
    par = pl.BlockSpec((1, 128), lambda b, n: (0, 0))
    in_specs = [blk(CHUNK, 512, 0), blk(CHUNK, 512, 1), blk(CHUNK, 512, 2), blk(CHUNK, 128, 6), par, par,
                blk(CHUNK, 128, 0), blk(CHUNK, 128, 0), blk(_HC, _HC, 0), blk(512, 128, 0),
                blk(CHUNK, 512, 0), blk(CHUNK, 512, 0), blk(CHUNK, 512, 0), ANY]
    out_shape = (_sds((Bl, S, 1536)), _sds((Bl, S, DCAT_W), BF), _sds((8, 128)))
    out_specs = (blk(CHUNK, 1536, 0), blk(CHUNK, 256, 3), pl.BlockSpec((8, 128), lambda b, n: (0, 0)))
    q3, s3 = qkvn.reshape(Bl, S, -1), sm.reshape(Bl, S, -1)
    res = _pc(body, name="gdn_chunk_bwd", out_shape=out_shape, grid=(Bl // E, nC), in_specs=in_specs,
              out_specs=out_specs, scratch=[pltpu.VMEM((E, 512, 128), F32)], sem=("arbitrary", "arbitrary"),
              aliases={13: 1}, xchg=xchg)(q3, q3, q3, s3, alog, dtb, gc_s, beta_s, tinv_s, sst_s, w_s, u_s,
                                          do.reshape(Bl, S, -1), dcat.reshape(Bl, S, DCAT_W))
    return (res[0].reshape(T, 1536), res[1].reshape(T, DCAT_W)) + tuple(res[2:])


def _gdn_post(o, z, g, T):
    tt = _pick(T, (2048, 1024, 512, 256))

    def body(o_ref, z_ref, g_ref, out_ref, outt_ref):
        y = (_rmsnorm(o_ref[...], g_ref[...]) * _silu(z_ref[...])).astype(out_ref.dtype)
        out_ref[...] = y
        outt_ref[...] = y.T

    blk = pl.BlockSpec((tt, 128), lambda i, h: (i, h))
    return _pc(body, name="gdn_post_fwd", out_shape=(_sds((T, 1024), BF), _sds((1024, T), BF)), grid=(T // tt, _H),
               in_specs=[blk, blk, pl.BlockSpec((1, 128), lambda i, h: (0, 0))],
               out_specs=(blk, pl.BlockSpec((128, tt), lambda i, h: (h, i))),
               sem=("parallel", "parallel"))(o, z, g)


def _gdn_post_bwd(o, z, g, dmixin, T):
    tt = _pick(T, (2048, 1024, 512, 256))

    def body(o_ref, z_ref, g_ref, d_ref, do_ref, dz_ref, dg_ref):
        @pl.when((pl.program_id(0) == 0) & (pl.program_id(1) == 0))
        def _():
            dg_ref[...] = jnp.zeros_like(dg_ref)

        _, vjp = jax.vjp(lambda a, b, c: _rmsnorm(a, c) * _silu(b), o_ref[...], z_ref[...], g_ref[...])
        do, dz, dg = vjp(d_ref[...])
        do_ref[...] = do
        dz_ref[...] = dz.astype(dz_ref.dtype)
        dg_ref[0:1, :] += dg

    blk = pl.BlockSpec((tt, 128), lambda i, h: (i, h))
    return _pc(body, name="gdn_post_bwd", out_shape=(_sds((T, 512)), _sds((T, DCAT_W), BF), _sds((8, 128))),
               grid=(T // tt, _H), in_specs=[blk, blk, pl.BlockSpec((1, 128), lambda i, h: (0, 0)), blk],
               out_specs=(blk, pl.BlockSpec((tt, 128), lambda i, h: (i, DCAT_Z // 128 + h)),
                          pl.BlockSpec((8, 128), lambda i, h: (0, 0))),
               sem=("arbitrary", "arbitrary"))(o, z, g, dmixin)


def _rms_fwd(sm, g, col_blk, name):
    T = sm.shape[0]
    d = g.shape[1]
    tt = _pick(T, (1024, 512, 256))

    def body(x_ref, g_ref, o_ref, ot_ref):
        y = _rmsnorm(x_ref[...], g_ref[...]).astype(o_ref.dtype)
        o_ref[...] = y
        ot_ref[...] = y.T

    return _pc(body, name=name, out_shape=(_sds((T, d), BF), _sds((d, T), BF)), grid=(T // tt,),
               in_specs=[pl.BlockSpec((tt, d), lambda i: (i, col_blk)), pl.BlockSpec((1, d), lambda i: (0, 0))],
               out_specs=(pl.BlockSpec((tt, d), lambda i: (i, 0)), pl.BlockSpec((d, tt), lambda i: (0, i))),
               sem=("parallel",))(sm, g)


def _rms_bwd(sm, g, dy, dsm, col_blk, name):
    T = sm.shape[0]
    d = g.shape[1]
    tt = _pick(T, (1024, 512, 256))

    def body(x_ref, g_ref, d_ref, alias_ref, dx_ref, dg_ref):
        @pl.when(pl.program_id(0) == 0)
        def _():
            dg_ref[...] = jnp.zeros_like(dg_ref)

        _, vjp = jax.vjp(_rmsnorm, x_ref[...], g_ref[...])
        dx, dg = vjp(d_ref[...])
        dx_ref[...] = dx.astype(dx_ref.dtype)
        dg_ref[0:1, :] += dg

    grp = pl.BlockSpec((tt, d), lambda i: (i, col_blk))
    return _pc(body, name=name, out_shape=(_sds(dsm.shape, dsm.dtype), _sds((8, d))), grid=(T // tt,),
               in_specs=[grp, pl.BlockSpec((1, d), lambda i: (0, 0)), pl.BlockSpec((tt, d), lambda i: (i, 0)), ANY],
               out_specs=(grp, pl.BlockSpec((8, d), lambda i: (0, 0))), aliases={3: 0},
               sem=("arbitrary",))(sm, g, dy, dsm)


def _rope_tables(S):
    inv = ROPE_THETA ** (-jnp.arange(0, MLA_ROPE, 2, dtype=F32) / MLA_ROPE)
    ang = jnp.arange(S, dtype=F32)[:, None] * inv[None, :]
    cos, sin = jnp.cos(ang), jnp.sin(ang)
    z = jnp.zeros((S, 64), F32)
    return jnp.concatenate([cos, cos, z], axis=1), jnp.concatenate([-sin, sin, z], axis=1)


def _swap_halves(x):
    lane = lax.broadcasted_iota(jnp.int32, x.shape, 1)
    return jnp.where(lane < 32, pltpu.roll(x, 96, 1), jnp.where(lane < 64, pltpu.roll(x, 32, 1), 0.0))


def _mm_rows(a, b, *, name, tm, extras, out_shape, out_specs, epi, mode="nn", sem="parallel"):
    M, K = a.shape
    nx = len(extras)

    def body(a_ref, b_ref, *rest):
        r = lax.dot_general(a_ref[...].astype(_MXU), b_ref[...].astype(_MXU), _NN if mode == "nn" else _NT,
                            preferred_element_type=F32)
        epi(r, rest[:nx], rest[nx:])

    in_specs = [pl.BlockSpec((tm, K), lambda i: (i, 0)), pl.BlockSpec(b.shape, lambda i: (0, 0))]
    return _pc(body, name=name, out_shape=out_shape, grid=(M // tm,), in_specs=in_specs + [s for _, s in extras],
               out_specs=out_specs, sem=(sem,))(a, b, *[x for x, _ in extras])


def _qup_rope(cqn, wq, cos, sin, S):
    T = cqn.shape[0]
    tm = _pick(S, (1024, 512, 256))
    nps = S // tm

    def epi(r, xs, outs):
        c, s = xs[0][...], xs[1][...]
        for h in range(MLA_HEADS):
            lo = 256 * h
            rp = r[:, lo + 128:lo + 256]
            outs[0][:, lo:lo + 128] = (r[:, lo:lo + 128] * ATTN_SCALE).astype(BF)
            outs[0][:, lo + 128:lo + 256] = ((rp * c + _swap_halves(rp) * s) * ATTN_SCALE).astype(BF)

    tab = pl.BlockSpec((tm, 128), lambda i: (i % nps, 0))
    return _mm_rows(cqn, wq, name="mm_qup_rope", tm=tm, extras=[(cos, tab), (sin, tab)],
                    out_shape=_sds((T, 1024), BF), out_specs=pl.BlockSpec((tm, 1024), lambda i: (i, 0)), epi=epi)


def _kvup_rope(ckvn, wkv, sm, cos, sin, S):
    T = ckvn.shape[0]
    tm = _pick(S, (1024, 512, 256))
    nps = S // tm

    def epi(r, xs, outs):
        k, c, s = xs[0][...], xs[1][...], xs[2][...]
        kr = (k * c + _swap_halves(k) * s).astype(BF)
        kvb = r.astype(BF)
        outs[0][...] = kvb
        for h in range(MLA_HEADS):
            lo = 256 * h
            outs[1][:, lo:lo + 128] = kvb[:, lo:lo + 128]
            outs[1][:, lo + 128:lo + 256] = kr

    tab = pl.BlockSpec((tm, 128), lambda i: (i % nps, 0))
    row = pl.BlockSpec((tm, 1024), lambda i: (i, 0))
    return _mm_rows(ckvn, wkv, name="mm_kvup_rope", tm=tm,
                    extras=[(sm, pl.BlockSpec((tm, 128), lambda i: (i, 3))), (cos, tab), (sin, tab)],
                    out_shape=(_sds((T, 1024), BF), _sds((T, 1024), BF)), out_specs=(row, row), epi=epi)


def _rope_bwd(dqc, dcat, dkrr, cos, sin, S):
    T = dqc.shape[0]
    tt = _pick(S, (2048, 1024, 512, 256))
    nps = S // tt

    def body(alias_s, dq_ref, dk_ref, c_ref, s_ref, qo_ref, ko_ref):
        c, s = c_ref[...], s_ref[...]
        d = dq_ref[:, 128:256]
        qo_ref[:, 0:128] = dq_ref[:, 0:128].astype(qo_ref.dtype)
        qo_ref[:, 128:256] = (d * c + _swap_halves(d * s)).astype(qo_ref.dtype)

        @pl.when(pl.program_id(1) == 0)
        def _():
            k = dk_ref[...]
            ko_ref[...] = (k * c + _swap_halves(k * s)).astype(ko_ref.dtype)

    tab = pl.BlockSpec((tt, 128), lambda i, h: (i % nps, 0))
    head = pl.BlockSpec((tt, 256), lambda i, h: (i, h))
    return _pc(body, name="rope_bwd", out_shape=(_sds((T, 1024), BF), _sds(dcat.shape, dcat.dtype)),
               grid=(T // tt, MLA_HEADS),
               in_specs=[ANY, head, pl.BlockSpec((tt, 128), lambda i, h: (i, 0)), tab, tab],
               out_specs=(head, pl.BlockSpec((tt, 128), lambda i, h: (i, 3))),
               aliases={0: 1}, sem=("parallel", "arbitrary"))(dcat, dqc, dkrr, cos, sin)


def _attn_scores(q, k_ref, L, tq):
    def sc(lo, hi):
        return _dot(q, k_ref[lo:hi, :], _NT)

    sd = sc(L - tq, L)
    qc = lax.broadcasted_iota(jnp.int32, sd.shape, 0) >> 6
    kc = lax.broadcasted_iota(jnp.int32, sd.shape, 1) >> 6
    sd = jnp.where(kc <= qc, sd, -jnp.inf)
    return sd if L == tq else jnp.concatenate([sc(0, L - tq), sd], axis=1)


def _attn_fwd(mixin, mixin_t, qcat, kcat, kv, Bl, S):
    T = Bl * S
    tq = _pick(S, (256, 128))
    nq = S // tq

    def body(alias_ref, alias_t, q_ref, k_ref, v_ref, o_ref, lse_ref, ot_ref):
        i = pl.program_id(2)
        q = q_ref[...]
        for ii in range(nq):
            @pl.when(i == ii)
            def _(L=(ii + 1) * tq):
                s = _attn_scores(q, k_ref, L, tq)
                m = jnp.max(s, axis=-1, keepdims=True)
                e = jnp.exp(s - m)
                l = jnp.sum(e, axis=-1, keepdims=True)
                o = (_dot(e, v_ref[0:L, :], _NN) / l).astype(o_ref.dtype)
                o_ref[...] = o
                ot_ref[...] = o.T
                lse_ref[...] = jnp.broadcast_to(m + jnp.log(l), lse_ref.shape)

    qrow = lambda b, h, i: b * nq + i
    in_specs = [ANY, ANY,
                pl.BlockSpec((tq, 256), lambda b, h, i: (qrow(b, h, i), h)),
                pl.BlockSpec((S, 256), lambda b, h, i: (b, h)),
                pl.BlockSpec((S, 128), lambda b, h, i: (b, 2 * h + 1))]
    return _pc(body, name="mla_attn_fwd",
               out_shape=(_sds(mixin.shape, mixin.dtype), _sds((T, 512)), _sds(mixin_t.shape, mixin_t.dtype)),
               grid=(Bl, MLA_HEADS, nq), in_specs=in_specs,
               out_specs=(pl.BlockSpec((tq, 128), lambda b, h, i: (qrow(b, h, i), 4 + h)),
                          pl.BlockSpec((tq, 128), lambda b, h, i: (qrow(b, h, i), h)),
                          pl.BlockSpec((128, tq), lambda b, h, i: (4 + h, qrow(b, h, i)))),
               aliases={0: 0, 1: 2}, sem=("parallel", "parallel", "parallel"))(mixin, mixin_t, qcat, kcat, kv)


def _attn_bwd(qcat, kcat, kv, lse, dmixin, Bl, S):
    T = Bl * S
    tq = _pick(S, (256, 128))
    nq = S // tq

    def body(q_ref, k_ref, v_ref, do_ref, lse_ref, dq_ref, dkv_ref, dkr_ref, acck_ref, accv_ref):
        h, i = pl.program_id(1), pl.program_id(2)
        q = q_ref[...]
        do = do_ref[...].astype(BF)
        lse_col = lse_ref[:, 0:1]

        @pl.when(i == 0)
        def _():
            acck_ref[...] = jnp.zeros_like(acck_ref)
            accv_ref[...] = jnp.zeros_like(accv_ref)

        for ii in range(nq):
            @pl.when(i == ii)
            def _(L=(ii + 1) * tq):
                p = jnp.exp(_attn_scores(q, k_ref, L, tq) - lse_col)
                pb = p.astype(BF)
                v = v_ref[0:L, :]
                dp = _dot(do, v, _NT)
                ds = (p * (dp - jnp.sum(dp * p, axis=-1, keepdims=True))).astype(BF)
                dq_ref[...] = _dot(ds, k_ref[0:L, :], _NN) * ATTN_SCALE
                acck_ref[0:L, :] += _dot(ds, q, _TN)
                accv_ref[0:L, :] += _dot(pb, do, _TN)

        @pl.when(i == nq - 1)
        def _():
            dkv_ref[:, 0:128] = acck_ref[:, 0:128].astype(dkv_ref.dtype)
            dkv_ref[:, 128:256] = accv_ref[...].astype(dkv_ref.dtype)
            dkr_ref[...] = jnp.where(h == 0, 0.0, dkr_ref[...]) + acck_ref[:, 128:256]

    qrow = lambda b, h, i: b * nq + i
    in_specs = [pl.BlockSpec((tq, 256), lambda b, h, i: (qrow(b, h, i), h)),
                pl.BlockSpec((S, 256), lambda b, h, i: (b, h)),
                pl.BlockSpec((S, 128), lambda b, h, i: (b, 2 * h + 1)),
                pl.BlockSpec((tq, 128), lambda b, h, i: (qrow(b, h, i), 4 + h)),
                pl.BlockSpec((tq, 128), lambda b, h, i: (qrow(b, h, i), h))]
    out_shape = (_sds((T, 1024)), _sds((T, 1024), BF), _sds((T, 128)))
    out_specs = (pl.BlockSpec((tq, 256), lambda b, h, i: (qrow(b, h, i), h)),
                 pl.BlockSpec((S, 256), lambda b, h, i: (b, h)),
                 pl.BlockSpec((S, 128), lambda b, h, i: (b, 0)))
    return _pc(body, name="mla_attn_bwd", out_shape=out_shape, grid=(Bl, MLA_HEADS, nq), in_specs=in_specs,
               out_specs=out_specs, scratch=[pltpu.VMEM((S, 256), F32), pltpu.VMEM((S, 128), F32)],
               sem=("arbitrary", "arbitrary", "arbitrary"))(qcat, kcat, kv, dmixin, lse)


def _out_ln1(mixin, w_out, x, g, b):
    T, D = x.shape
    tm = _pick(T, (512, 256))

    def epi(r, xs, outs):
        h = _layernorm(ALPHA * xs[0][...] + r, xs[1][...], xs[2][...])
        outs[0][...] = r
        outs[1][...] = h
        hb = h.astype(BF)
        outs[2][...] = hb
        outs[3][...] = hb.T

    row = pl.BlockSpec((tm, D), lambda i: (i, 0))
    par = pl.BlockSpec((1, D), lambda i: (0, 0))
    return _mm_rows(mixin, w_out, name="mm_out_ln1", tm=tm, extras=[(x, row), (g, par), (b, par)],
                    out_shape=(_sds((T, D)), _sds((T, D)), _sds((T, D), BF), _sds((D, T), BF)),
                    out_specs=(row, row, row, pl.BlockSpec((D, tm), lambda i: (0, i))), epi=epi)


def _dh1_ln1_bwd(du_u, wup_u, du_g, wup_g, dgpre, w_gate, dffn, x, mix, g, b):
    T, D = x.shape
    tm = _pick(T, (256,))

    def epi(r, xs, outs):
        dug_ref, wg_ref, dgp_ref, wgate_ref, dffn_ref, x_ref, mix_ref, g_ref, b_ref = xs
        acc_ref = outs[2]

        @pl.when(pl.program_id(0) == 0)
        def _():
            acc_ref[...] = jnp.zeros_like(acc_ref)

        dh = (r + _dot(dug_ref[...], wg_ref[...], _NT) + _dot(dgp_ref[...], wgate_ref[...], _NT)
              + ALPHA * dffn_ref[...])
        f = lambda xx, mm, gg, bb: _layernorm(ALPHA * xx + mm, gg, bb)
        _, vjp = jax.vjp(f, x_ref[...], mix_ref[...], g_ref[...], b_ref[...])
        dx, dm, dg, db = vjp(dh)
        outs[0][...] = dx
        outs[1][...] = dm.astype(BF)
        acc_ref[0:1, :] += dg
        acc_ref[1:2, :] += db

    row = pl.BlockSpec((tm, D), lambda i: (i, 0))
    par = pl.BlockSpec((1, D), lambda i: (0, 0))
    whole = lambda a: pl.BlockSpec(a.shape, lambda i: (0, 0))
    extras = [(du_g, pl.BlockSpec((tm, du_g.shape[1]), lambda i: (i, 0))), (wup_g, whole(wup_g)), (dgpre, row),
              (w_gate, whole(w_gate)), (dffn, row), (x, row), (mix, row), (g, par), (b, par)]
    return _mm_rows(du_u, wup_u, name="mm_dh1_ln1_bwd", tm=tm, mode="nt", sem="arbitrary", extras=extras,
                    out_shape=(_sds((T, D)), _sds((T, D), BF), _sds((8, D))),
                    out_specs=(row, row, pl.BlockSpec((8, D), lambda i: (0, 0))), epi=epi)


def _ffn_up_act_fwd(h1b, wup_g, wup_u, wg, wu, bg, bu, Bl, S):
    T, D = h1b.shape
    C = wup_g.shape[1]
    cb = _pick(C, (256, 128))
    kk = wg.shape[0]

    def body(h_ref, pg_ref, pu_ref, wg_ref, wu_ref, bg_ref, bu_ref, ug_ref, uu_ref, o_ref, ot_ref):
        h = h_ref[...]
        ug = _dot(h, pg_ref[...], _NN)
        uu = _dot(h, pu_ref[...], _NN)
        ug_ref[...] = ug
        uu_ref[...] = uu
        cg = _conv(ug, wg_ref[...]) + bg_ref[...]
        cu = _conv(uu, wu_ref[...]) + bu_ref[...]
        a = (_silu(cg) * cu).astype(o_ref.dtype)
        o_ref[...] = a
        ot_ref[...] = a.T

    blk = pl.BlockSpec((S, cb), lambda b, j: (b, j))
    pblk = pl.BlockSpec((D, cb), lambda b, j: (0, j))
    wblk = pl.BlockSpec((kk, cb), lambda b, j: (0, j))
    bblk = pl.BlockSpec((1, cb), lambda b, j: (0, j))
    return _pc(body, name="ffn_up_act_fwd", out_shape=(_sds((T, C)), _sds((T, C)), _sds((T, C), BF), _sds((C, T), BF)),
               grid=(Bl, C // cb), in_specs=[pl.BlockSpec((S, D), lambda b, j: (b, 0)), pblk, pblk, wblk, wblk, bblk, bblk],
               out_specs=(blk, blk, blk, pl.BlockSpec((cb, S), lambda b, j: (j, b))),
               sem=("parallel", "parallel"))(h1b, wup_g, wup_u, wg, wu, bg, bu)


def _ffn_act_bwd(ug, uu, wg, wu, bg, bu, dffnb, w_down, Bl, S):
    T, C = ug.shape
    D = dffnb.shape[1]
    cb = _pick(C, (256, 128))
    kk = wg.shape[0]

    def body(g_ref, u_ref, wg_ref, wu_ref, bg_ref, bu_ref, d_ref, wd_ref,
             dg_ref, du_ref, dwg_ref, dwu_ref, dbg_ref, dbu_ref):
        first = pl.program_id(1) == 0
        wgv, wuv = wg_ref[...], wu_ref[...]
        tg, tu = _taps(g_ref[...], kk), _taps(u_ref[...], kk)
        cg = _conv(None, wgv, tg) + bg_ref[...]
        cu = _conv(None, wuv, tu) + bu_ref[...]
        _, vjp = jax.vjp(lambda a, b: _silu(a) * b, cg, cu)
        dcg, dcu = vjp(_dot(d_ref[...], wd_ref[...], _NT))
        dg_ref[...] = _conv_bwd_x(dcg, wgv).astype(dg_ref.dtype)
        du_ref[...] = _conv_bwd_x(dcu, wuv).astype(du_ref.dtype)
        _conv_bwd_w(tg, dcg, dwg_ref, first)
        _conv_bwd_w(tu, dcu, dwu_ref, first)
        dbg_ref[...] = jnp.where(first, 0.0, dbg_ref[...]) + jnp.sum(dcg, axis=0, keepdims=True)
        dbu_ref[...] = jnp.where(first, 0.0, dbu_ref[...]) + jnp.sum(dcu, axis=0, keepdims=True)

    blk = pl.BlockSpec((S, cb), lambda j, b: (b, j))
    wblk = pl.BlockSpec((kk, cb), lambda j, b: (0, j))
    bblk = pl.BlockSpec((1, cb), lambda j, b: (0, j))
    out_shape = (_sds((T, C), BF), _sds((T, C), BF), _sds((kk, C)), _sds((kk, C)), _sds((1, C)), _sds((1, C)))
    return _pc(body, name="ffn_act_bwd", out_shape=out_shape, grid=(C // cb, Bl),
               in_specs=[blk, blk, wblk, wblk, bblk, bblk, pl.BlockSpec((S, D), lambda j, b: (b, 0)),
                         pl.BlockSpec((cb, D), lambda j, b: (j, 0))],
               out_specs=(blk, blk, wblk, wblk, bblk, bblk), sem=("parallel", "arbitrary"))(
        ug, uu, wg, wu, bg, bu, dffnb, w_down)


def _head(act, w_down, h1, h1b, w_gate, pb, w_proj, tgt, bg, g2, b2):
    T, D = h1.shape
    tm = _pick(T, (256,))

    def epi(ffn, xs, outs):
        h_ref, hb_ref, wg_ref, pb_ref, wp_ref, t_ref, bg_ref, g2_ref, b2_ref = xs
        acc_ref = outs[4]

        @pl.when(pl.program_id(0) == 0)
        def _():
            acc_ref[...] = jnp.zeros_like(acc_ref)

        h, tg = h_ref[...], t_ref[...]
        gpre = _dot(hb_ref[...], wg_ref[...], _NN)
        pp = _dot(pb_ref[...], wp_ref[...], _NN)

        def loss_fn(f, gp, p_, bgv, g2v, b2v):
            pre = ALPHA * h + f + jax.nn.sigmoid(gp + bgv) * p_
            err = _layernorm(pre, g2v, b2v) - tg
            return 0.5 * jnp.sum(jnp.mean(err * err, axis=-1, keepdims=True))

        loss, grads = jax.value_and_grad(loss_fn, argnums=(0, 1, 2, 3, 4, 5))(
            ffn, gpre, pp, bg_ref[...], g2_ref[...], b2_ref[...])
        outs[0][...] = grads[0]
        outs[1][...] = grads[0].astype(BF)
        outs[2][...] = grads[1].astype(BF)
        outs[3][...] = grads[2].astype(BF)
        acc_ref[0:1, :] += grads[3]
        acc_ref[1:2, :] += grads[4]
        acc_ref[2:3, :] += grads[5]
        acc_ref[3:4, :] += jnp.broadcast_to(loss, (1, D))

    row = pl.BlockSpec((tm, D), lambda i: (i, 0))
    par = pl.BlockSpec((1, D), lambda i: (0, 0))
    whole = lambda a: pl.BlockSpec(a.shape, lambda i: (0, 0))
    o_bf = _sds((T, D), BF)
    extras = [(h1, row), (h1b, row), (w_gate, whole(w_gate)), (pb, pl.BlockSpec((tm, pb.shape[1]), lambda i: (i, 0))),
              (w_proj, whole(w_proj)), (tgt, row), (bg, par), (g2, par), (b2, par)]
    return _mm_rows(act, w_down, name="mm_down_loss_head", tm=tm, sem="arbitrary", extras=extras,
                    out_shape=(_sds((T, D)), o_bf, o_bf, o_bf, _sds((8, D))),
                    out_specs=(row, row, row, row, pl.BlockSpec((8, D), lambda i: (0, 0))), epi=epi)


def _adamw(parts, w, m, v, name):
    R, C = w.shape
    tr = R if R <= 512 else _pick(R, (256,))

    def body(p_ref, w_ref, m_ref, v_ref, g_ref, d_ref, nm_ref, nv_ref):
        g = p_ref[0].astype(F32)
        for j in range(1, N_DEV):
            g = g + p_ref[j].astype(F32)
        g_ref[...] = g
        d_ref[...], nm_ref[...], nv_ref[...] = _adam_math(g, w_ref[...], m_ref[...], v_ref[...])

    blk = pl.BlockSpec((tr, C), lambda i: (i, 0))
    o = _sds((R, C))
    return _pc(body, name=name, out_shape=(o, o, o, o), grid=(R // tr,),
               in_specs=[pl.BlockSpec((N_DEV, tr, C), lambda i: (0, i, 0)), blk, blk, blk],
               out_specs=(blk, blk, blk, blk), sem=("parallel",))(parts, w, m, v)


def _adam_math(g, w, m, v):
    mm = ADAM_B1 * m + (1.0 - ADAM_B1) * g
    vv = ADAM_B2 * v + (1.0 - ADAM_B2) * jnp.square(g)
    m_hat = mm / (1.0 - ADAM_B1 ** ADAM_STEP)
    v_hat = vv / (1.0 - ADAM_B2 ** ADAM_STEP)
    return -ADAM_LR * (m_hat / (jnp.sqrt(v_hat) + ADAM_EPS) + ADAM_WD * w), mm, vv


def _small_layout(sizes):
    offs, off = {}, 0
    for n in SMALL_ORDER:
        offs[n] = off
        off += -(-sizes[n] // 128) * 128
    return offs, off


def _pack_small(d):
    return jnp.concatenate([_pad_lanes(d[n].reshape(1, -1), -(-d[n].size // 128) * 128) for n in SMALL_ORDER], axis=1)


def _adamw_small(parts, ws, ms, vs):
    k = len(SMALL_ORDER)
    sizes = {n: ws[n].shape[1] for n in SMALL_ORDER}
    offs, _ = _small_layout(sizes)

    def body(p_ref, *refs):
        ins, outs = refs[:3 * k], refs[3 * k:]
        for i, n in enumerate(SMALL_ORDER):
            lo, hi = offs[n], offs[n] + sizes[n]
            g = p_ref[0, :, lo:hi]
            for j in range(1, N_DEV):
                g = g + p_ref[j, :, lo:hi]
            d, mm, vv = _adam_math(g, ins[i][...], ins[k + i][...], ins[2 * k + i][...])
            outs[4 * i][...], outs[4 * i + 1][...], outs[4 * i + 2][...], outs[4 * i + 3][...] = g, d, mm, vv

    out_shape = tuple(_sds((1, sizes[n])) for n in SMALL_ORDER for _ in range(4))
    args = [ws[n] for n in SMALL_ORDER] + [ms[n] for n in SMALL_ORDER] + [vs[n] for n in SMALL_ORDER]
    res = _pc(body, name="adamw_small", out_shape=out_shape)(parts, *args)
    return {n: tuple(res[4 * i:4 * i + 4]) for i, n in enumerate(SMALL_ORDER)}


def _prep(x, p, xchg=None):
    T, D = x.shape
    Dp = p.shape[1]
    tt = _pick(T, (512, 256))

    def body(x_ref, p_ref, xb_ref, xt_ref, pb_ref, pt_ref):
        xb = x_ref[...].astype(BF)
        xb_ref[...] = xb
        xt_ref[...] = xb.T
        pb = p_ref[...].astype(BF)
        pb_ref[...] = pb
        pt_ref[...] = pb.T

    row = lambda w: pl.BlockSpec((tt, w), lambda i: (i, 0))
    col = lambda w: pl.BlockSpec((w, tt), lambda i: (0, i))
    return _pc(body, name="prep_inputs",
               out_shape=(_sds((T, D), BF), _sds((D, T), BF), _sds((T, Dp), BF), _sds((Dp, T), BF)),
               grid=(T // tt,), in_specs=[row(D), row(Dp)], out_specs=(row(D), col(D), row(Dp), col(Dp)),
               sem=("parallel",), xchg=xchg)(x, p)


SMALL_ORDER = ("gdn_a_log", "gdn_dt_bias", "gdn_norm_g", "mla_q_norm_g", "mla_kv_norm_g", "ln1_g", "ln1_b",
               "ffn_conv_b", "ple_b_gate", "ln2_g", "ln2_b")
EARLY = ("w_in", "gdn_conv_w")
LATE = ("mla_w_q_up", "mla_w_kv_up", "w_out", "ffn_w_up", "ffn_conv_w", "ffn_w_down", "ple_w_gate", "ple_w_proj")
GRADS_EARLY = ("w_out", "ffn_w_up", "ffn_conv_w", "ffn_w_down", "ple_w_gate", "ple_w_proj")
GRADS_LATE = ("w_in", "gdn_conv_w", "mla_w_q_up", "mla_w_kv_up")


def _pad_lanes(v, n=128):
    return jnp.pad(v, ((0, 0), (0, n - v.shape[1])))


def _local_step(x, p, tgt, W, sp, Bl, S, early_weights=None, late_weights=None, early_grads=None, late_grads=None):
    T = Bl * S
    W = dict(W)
    prep = _prep(x, p, xchg=None if early_weights is None else early_weights[:2])
    xb, xT, pb, pT = prep[:4]
    if early_weights is not None:
        W.update(early_weights[2](prep[4:]))
    w_in = W["w_in"]
    wqkv, wz = w_in[:, :1536], w_in[:, 1536:2048]
    z64 = jnp.zeros((w_in.shape[0], 64), w_in.dtype)
    z124 = jnp.zeros((w_in.shape[0], 124), w_in.dtype)
    wsm = jnp.concatenate([w_in[:, 2056:2440], w_in[:, 2696:2760], z64, w_in[:, 2440:2696],
                           w_in[:, 2048:2052], z124, w_in[:, 2052:2056], z124], axis=1)
    alog, dtb = _pad_lanes(sp["gdn_a_log"]), _pad_lanes(sp["gdn_dt_bias"])
    cos, sin = _rope_tables(S)

    qkv, qkvn = _gdn_proj_pre_fwd(xb, wqkv, W["gdn_conv_w"], Bl, S)
    z = _mm(xb, wz, "nn", name="mm_z")
    sm = _mm(xb, wsm, "nn", name="mm_sm")
    gdn_out = _gdn_fwd(qkvn, sm, alog, dtb, Bl, S, xchg=None if late_weights is None else late_weights[:2])
    o, tinv_s, sst_s, w_s, u_s, gc_s, beta_s = gdn_out[:7]
    if late_weights is not None:
        W.update(late_weights[2](gdn_out[7:]))
    wq = jnp.pad(W["mla_w_q_up"].reshape(-1, MLA_HEADS, 192), ((0, 0), (0, 0), (0, 64))).reshape(-1, 1024)
    wkv = W["mla_w_kv_up"]
    C = W["ffn_w_down"].shape[0]
    wup_g, wup_u = W["ffn_w_up"][:, :C], W["ffn_w_up"][:, C:]
    cw_g, cw_u = W["ffn_conv_w"][:, :C], W["ffn_conv_w"][:, C:]
    cb_g, cb_u = sp["ffn_conv_b"][:, :C], sp["ffn_conv_b"][:, C:]
    mixin, mixin_t = _gdn_post(o, z, sp["gdn_norm_g"], T)
    cqn, cqn_t = _rms_fwd(sm, sp["mla_q_norm_g"], 0, "rms_q_fwd")
    ckvn, ckvn_t = _rms_fwd(sm, sp["mla_kv_norm_g"], 2, "rms_kv_fwd")
    qcat = _qup_rope(cqn, wq, cos, sin, S)
    kv, kcat = _kvup_rope(ckvn, wkv, sm, cos, sin, S)
    mixin, lse, mixin_t = _attn_fwd(mixin, mixin_t, qcat, kcat, kv, Bl, S)
    mix, h1, h1b, h1T = _out_ln1(mixin, W["w_out"], x, sp["ln1_g"], sp["ln1_b"])
    ug, uu, act, act_t = _ffn_up_act_fwd(h1b, wup_g, wup_u, cw_g, cw_u, cb_g, cb_u, Bl, S)

    dffn, dffnb, dgpre, dpp, hacc = _head(act, W["ffn_w_down"], h1, h1b, W["ple_w_gate"], pb, W["ple_w_proj"], tgt,
                                          sp["ple_b_gate"], sp["ln2_g"], sp["ln2_b"])
    loss = hacc[3, 0]
    gW, gs = {}, {}
    gs["ple_b_gate"], gs["ln2_g"], gs["ln2_b"] = hacc[0:1], hacc[1:2], hacc[2:3]
    gW["ple_w_proj"] = _mm(pT, dpp, "nn", name="mm_dproj", out_dtype=BF)
    gW["ple_w_gate"] = _mm(h1T, dgpre, "nn", name="mm_dgate", out_dtype=BF)
    gW["ffn_w_down"] = _mm(act_t, dffnb, "nn", name="mm_ddown", out_dtype=BF)
    du_g, du_u, dcw_g, dcw_u, dcb_g, dcb_u = _ffn_act_bwd(ug, uu, cw_g, cw_u, cb_g, cb_u, dffnb, W["ffn_w_down"],
                                                          Bl, S)
    gW["ffn_conv_w"] = jnp.concatenate([dcw_g, dcw_u], axis=1)
    gs["ffn_conv_b"] = jnp.concatenate([dcb_g, dcb_u], axis=1)
    gW["ffn_w_up"] = jnp.concatenate([_mm(h1T, du_g, "nn", name="mm_dup_gate", out_dtype=BF),
                                      _mm(h1T, du_u, "nn", name="mm_dup_up", out_dtype=BF)], axis=1)
    dxa, dmix, acc1 = _dh1_ln1_bwd(du_u, wup_u, du_g, wup_g, dgpre, W["ple_w_gate"], dffn, x, mix,
                                   sp["ln1_g"], sp["ln1_b"])
    gs["ln1_g"], gs["ln1_b"] = acc1[0:1], acc1[1:2]
    gW["w_out"] = _mm(mixin_t, dmix, "nn", name="mm_dwout", out_dtype=BF)
    dmixin = _mm(dmix, W["w_out"], "nt", name="mm_dmixin")
    do, dcat, gacc = _gdn_post_bwd(o, z, sp["gdn_norm_g"], dmixin, T)
    gs["gdn_norm_g"] = gacc[0:1]
    bwd_out = _gdn_bwd(qkvn, sm, alog, dtb, gc_s, beta_s, tinv_s, sst_s, w_s, u_s, do, dcat, Bl, S,
                       xchg=None if early_grads is None else early_grads({n: gW[n] for n in GRADS_EARLY}))
    dqkvn, dcat, cacc = bwd_out[:3]
    early_recv = bwd_out[3:]
    gs["gdn_a_log"], gs["gdn_dt_bias"] = cacc[0:1, :GDN_HEADS], cacc[1:2, :GDN_HEADS]
    dcat, gW["gdn_conv_w"] = _gdn_pre_bwd(qkv, W["gdn_conv_w"], dqkvn, dcat, Bl, S)
    dqc, dkv, dkrr = _attn_bwd(qcat, kcat, kv, lse, dmixin, Bl, S)
    dqraw, dcat = _rope_bwd(dqc, dcat, dkrr, cos, sin, S)
    gwq = _mm(cqn_t, dqraw, "nn", name="mm_dwq", out_dtype=BF)
    gW["mla_w_q_up"] = gwq.reshape(-1, MLA_HEADS, 256)[:, :, :192].reshape(-1, MLA_HEADS * 192)
    gW["mla_w_kv_up"] = _mm(ckvn_t, dkv, "nn", name="mm_dwkv", out_dtype=BF)
    dcqn = _mm(dqraw, wq, "nt", name="mm_dcqn")
    dckvn = _mm(dkv, wkv, "nt", name="mm_dckvn")
    dcat, qacc = _rms_bwd(sm, sp["mla_q_norm_g"], dcqn, dcat, 0, "rms_q_bwd")
    dcat, kacc = _rms_bwd(sm, sp["mla_kv_norm_g"], dckvn, dcat, 2, "rms_kv_bwd")
    gs["mla_q_norm_g"], gs["mla_kv_norm_g"] = qacc[0:1], kacc[0:1]
    gcat = _mm(xT, dcat, "nn", name="mm_dwin", out_dtype=BF)
    gsm, gqkv, gz = gcat[:, :DCAT_QKV], gcat[:, DCAT_QKV:DCAT_Z], gcat[:, DCAT_Z:]
    gW["w_in"] = jnp.concatenate([gqkv, gz, gsm[:, 768:772], gsm[:, 896:900], gsm[:, 0:384], gsm[:, 512:768],
                                  gsm[:, 384:448]], axis=1)
    wcat = jnp.concatenate([wsm, wqkv, wz], axis=1)
    dx = _mm(dcat, wcat, "nt", name="mm_dx", add=dxa, xchg=None if late_grads is None else late_grads(gW, gs))
    late_recv = ()
    if late_grads is not None:
        dx, late_recv = dx[0], dx[1:]
    return loss, dx, gW, gs, early_recv, late_recv


COL_SHARDED = ("w_in", "mla_w_q_up", "mla_w_kv_up", "ffn_w_up", "ple_w_proj", "gdn_conv_w", "ffn_conv_w")
SHARDED = EARLY + LATE
WEIGHTS = ("w_in", "gdn_conv_w", "gdn_a_log", "gdn_dt_bias", "gdn_norm_g", "mla_q_norm_g", "mla_w_q_up",
           "mla_kv_norm_g", "mla_w_kv_up", "w_out", "ln1_g", "ln1_b", "ffn_w_up", "ffn_conv_w", "ffn_conv_b",
           "ffn_w_down", "ple_w_gate", "ple_b_gate", "ple_w_proj", "ln2_g", "ln2_b")
CONV = ("gdn_conv_w", "ffn_conv_w")


def _gathered_to_full(name, g):
    if name in COL_SHARDED:
        return jnp.transpose(g, (1, 0, 2)).reshape(g.shape[1], -1)
    return g.reshape(-1, g.shape[-1])


def _full_to_blocks(name, gfull, shard_shape):
    r, c = shard_shape
    if name in COL_SHARDED:
        return jnp.transpose(gfull.reshape(r, N_DEV, c), (1, 0, 2))
    return gfull.reshape(N_DEV, r, c)


def kernel(x, p, w_in, gdn_conv_w, gdn_a_log, gdn_dt_bias, gdn_norm_g, mla_q_norm_g, mla_w_q_up, mla_kv_norm_g, mla_w_kv_up, w_out, ln1_g, ln1_b, ffn_w_up, ffn_conv_w, ffn_conv_b, ffn_w_down, ple_w_gate, ple_b_gate, ple_w_proj, ln2_g, ln2_b, loss_target, m_w_in, m_gdn_conv_w, m_gdn_a_log, m_gdn_dt_bias, m_gdn_norm_g, m_mla_q_norm_g, m_mla_w_q_up, m_mla_kv_norm_g, m_mla_w_kv_up, m_w_out, m_ln1_g, m_ln1_b, m_ffn_w_up, m_ffn_conv_w, m_ffn_conv_b, m_ffn_w_down, m_ple_w_gate, m_ple_b_gate, m_ple_w_proj, m_ln2_g, m_ln2_b, v_w_in, v_gdn_conv_w, v_gdn_a_log, v_gdn_dt_bias, v_gdn_norm_g, v_mla_q_norm_g, v_mla_w_q_up, v_mla_kv_norm_g, v_mla_w_kv_up, v_w_out, v_ln1_g, v_ln1_b, v_ffn_w_up, v_ffn_conv_w, v_ffn_conv_b, v_ffn_w_down, v_ple_w_gate, v_ple_b_gate, v_ple_w_proj, v_ln2_g, v_ln2_b):
    loc = dict(locals())
    wts = {n: loc[n] for n in WEIGHTS}
    ms = {n: loc["m_" + n] for n in WEIGHTS}
    vs = {n: loc["v_" + n] for n in WEIGHTS}
    Bl, S, D = x.shape
    T = Bl * S

    wire = lambda n: wts[n][0] if n in CONV else wts[n][0].astype(BF)
    gather = lambda names, mode: ([wire(n) for n in names], [mode] * len(names),
                                  lambda res: {n: _gathered_to_full(n, g) for n, g in zip(names, res)})
    blocks = lambda g, names: [_full_to_blocks(n, g[n], wts[n].shape[1:]) for n in names]
    pack_early = lambda g: (blocks(g, GRADS_EARLY), ["a2a"] * len(GRADS_EARLY))
    pack_late = lambda g, gs: (blocks(g, GRADS_LATE) + [_pack_small(gs)], ["a2a"] * len(GRADS_LATE) + ["ag"])
    sp = {n: wts[n].reshape(1, -1) for n in SMALL_ORDER}

    loss, dx, gW, gs, early_recv, late_recv = _local_step(
        x.reshape(T, D), p.reshape(T, -1), loss_target.reshape(T, D), {}, sp, Bl, S,
        early_weights=gather(EARLY, "ag2"), late_weights=gather(LATE, "ag"), early_grads=pack_early,
        late_grads=pack_late)

    res = {}
    for n, parts in list(zip(GRADS_EARLY, early_recv)) + list(zip(GRADS_LATE, late_recv[:-1])):
        res[n] = tuple(t[None] for t in _adamw(parts, wts[n][0], ms[n][0], vs[n][0], "adamw_" + n))
    res.update(_adamw_small(late_recv[-1], wts, ms, vs))

    loss = lax.psum(loss, ("x", "y", "c"))
    outs = [loss, dx.reshape(Bl, S, D)]
    for k in range(4):
        outs += [res[n][k] for n in WEIGHTS]
    return tuple(outs)
```

```python
import math

import jax
import jax.numpy as jnp
from jax import lax
from jax.experimental import pallas as pl
from jax.experimental.pallas import tpu as pltpu

F32 = jnp.float32
BF = jnp.bfloat16
_MXU = jnp.bfloat16
_INTERPRET = None
_VMEM_LIMIT = 56 * 1024 * 1024
HI = lax.Precision.HIGHEST

N_DEV = 8
CHUNK = 64
GDN_HEADS = 4
GDN_DK = 128
MLA_HEADS = 4
MLA_NOPE = 128
MLA_ROPE = 64
ROPE_THETA = 10000.0
ALPHA = 2.0 ** 0.25
NORM_EPS = 1e-6
ATTN_SCALE = (MLA_NOPE + MLA_ROPE) ** -0.5
DCAT_QKV, DCAT_Z, DCAT_W = 1024, 2560, 3072
ADAM_LR, ADAM_B1, ADAM_B2, ADAM_EPS, ADAM_WD, ADAM_STEP = 0.001, 0.9, 0.999, 1e-08, 0.01, 10
MESH = pl.DeviceIdType.MESH
ANY = pl.BlockSpec(memory_space=pl.ANY)
_NN = (((1,), (0,)), ((), ()))
_NT = (((1,), (1,)), ((), ()))
_TN = (((0,), (0,)), ((), ()))


def _sds(shape, dtype=F32):
    return jax.ShapeDtypeStruct(tuple(shape), dtype)


def _pick(n, cands):
    for c in cands:
        if n % c == 0:
            return c
    return n


def _xchg_out_shapes(arrs, modes):
    return [_sds(a.shape if m == "a2a" else (N_DEV,) + a.shape, a.dtype) for a, m in zip(arrs, modes)]


def _xchg_scratch(n):
    return [pltpu.SemaphoreType.DMA((n, N_DEV - 1)), pltpu.SemaphoreType.DMA((n, N_DEV - 1)),
            pltpu.SemaphoreType.DMA((n,))]


def _xchg_plan(ins, outs, send, recv, loc, modes):
    x, y, c = lax.axis_index("x"), lax.axis_index("y"), lax.axis_index("c")
    me = 4 * x + 2 * y + c
    starts, relays, waits = [], [], []
    for ai, mode in enumerate(modes):
        src_all, out = ins[ai], outs[ai]

        def remote(k, src, dst, dev):
            return pltpu.make_async_remote_copy(src_ref=src, dst_ref=dst, send_sem=send.at[ai, k], recv_sem=recv.at[ai, k],
                                                device_id=dev, device_id_type=MESH)

        own = pltpu.make_async_copy(src_all.at[me] if mode == "a2a" else src_all, out.at[me], loc.at[ai])
        starts.append(own)
        waits.append(own.wait)
        if mode == "ag2":
            sib = (x, y, 1 - c)
            chips = [(1 - x, y), (x, 1 - y), (1 - x, 1 - y)]
            first = [remote(0, src_all, out.at[me], sib)]
            first += [remote(1 + k, src_all, out.at[me], (px, py, c)) for k, (px, py) in enumerate(chips)]
            starts += first
            waits += [cp.wait_send for cp in first]
            waits.append(remote(0, src_all, out.at[4 * x + 2 * y + 1 - c], sib).wait_recv)
            for k, (px, py) in enumerate(chips):
                same, other = out.at[4 * px + 2 * py + c], out.at[4 * px + 2 * py + 1 - c]
                relay = remote(4 + k, same, same, sib)
                relays.append((remote(1 + k, src_all, same, (px, py, c)), relay))
                waits += [relay.wait_send, remote(4 + k, src_all, other, sib).wait_recv]
            continue
        for r in range(1, N_DEV):
            px = 1 - x if (r >> 2) & 1 else x
            py = 1 - y if (r >> 1) & 1 else y
            pc = 1 - c if r & 1 else c
            cp = remote(r - 1, src_all.at[4 * px + 2 * py + pc] if mode == "a2a" else src_all, out.at[me], (px, py, pc))
            starts.append(cp)
            waits.append(cp.wait)
    return starts, relays, waits


def _pc(body, *, name, out_shape, grid=None, in_specs=None, out_specs=None, scratch=(), sem=None, aliases=None,
        xchg=None):
    kw = {}
    if _INTERPRET is not None:
        kw["interpret"] = _INTERPRET
    single = not isinstance(out_shape, (tuple, list))
    out_shape = [out_shape] if single else list(out_shape)
    if out_specs is not None:
        out_specs = [out_specs] if single else list(out_specs)
    scratch = list(scratch)
    if xchg is not None:
        xarrs, xmodes = xchg
        n_in, n_out, n_scr, nx = len(in_specs), len(out_shape), len(scratch), len(xarrs)
        single = False
        inner = body

        def body(*refs):
            ins, xins = refs[:n_in], refs[n_in:n_in + nx]
            outs = refs[n_in + nx:n_in + nx + n_out]
            xouts = refs[n_in + nx + n_out:n_in + 2 * nx + n_out]
            scr = refs[n_in + 2 * nx + n_out:n_in + 2 * nx + n_out + n_scr]
            send, recv, loc = refs[n_in + 2 * nx + n_out + n_scr:]
            first = last = None
            for d, g in enumerate(grid):
                f, l = pl.program_id(d) == 0, pl.program_id(d) == g - 1
                first, last = (f, l) if first is None else (first & f, last & l)

            @pl.when(first)
            def _():
                for cp in _xchg_plan(xins, xouts, send, recv, loc, xmodes)[0]:
                    cp.start()

            inner(*ins, *outs, *scr)

            @pl.when(last)
            def _():
                _, relays, waits = _xchg_plan(xins, xouts, send, recv, loc, xmodes)
                for arrival, relay in relays:
                    arrival.wait_recv()
                    relay.start()
                for wait in waits:
                    wait()

        in_specs = list(in_specs) + [ANY] * nx
        out_specs = out_specs + [ANY] * nx
        out_shape = out_shape + _xchg_out_shapes(xarrs, xmodes)
        scratch = scratch + _xchg_scratch(nx)
        sem = ("arbitrary",) * len(grid)
    if grid is not None:
        kw["grid"] = grid
    if in_specs is not None:
        kw["in_specs"] = in_specs
    if out_specs is not None:
        kw["out_specs"] = out_specs[0] if single else tuple(out_specs)
    cp = dict(vmem_limit_bytes=_VMEM_LIMIT)
    if sem is not None:
        cp["dimension_semantics"] = sem
    call = pl.pallas_call(body, name=name, out_shape=out_shape[0] if single else tuple(out_shape),
                          scratch_shapes=scratch, input_output_aliases=aliases or {},
                          compiler_params=pltpu.CompilerParams(**cp), **kw)
    if xchg is None:
        return call
    return lambda *ins: call(*ins, *xchg[0])


_FULL_K_MAX = 3072


def _mm(a, b, mode, *, name, add=None, add_scale=1.0, out_dtype=F32, xchg=None):
    (M, K) = a.shape
    (K2, N) = b.shape if mode == "nn" else b.shape[::-1]
    assert K == K2, (a.shape, b.shape, mode)
    tm = _pick(M, (1024, 1408, 512, 384, 256, 128))
    tn = _pick(N, (1024, 2816, 1408, 768, 512, 384, 256, 128))
    tk = K if K <= _FULL_K_MAX else _pick(K, (2048, 1408, 1024, 512) if tn <= 1408 else (1024, 512))
    nk = K // tk
    dims = _NN if mode == "nn" else _NT
    has_add = add is not None

    def finish(r, add_ref, o_ref):
        if has_add:
            r = r + add_scale * add_ref[...].astype(F32)
        o_ref[...] = r.astype(out_dtype)

    if nk == 1:
        def body(a_ref, b_ref, *rest):
            r = lax.dot_general(a_ref[...].astype(_MXU), b_ref[...].astype(_MXU), dims, preferred_element_type=F32)
            finish(r, rest[0] if has_add else None, rest[-1])

        a_spec = pl.BlockSpec((tm, K), lambda i, j: (i, 0))
        b_spec = (pl.BlockSpec((K, tn), lambda i, j: (0, j)) if mode == "nn"
                  else pl.BlockSpec((tn, K), lambda i, j: (j, 0)))
        o_spec = pl.BlockSpec((tm, tn), lambda i, j: (i, j))
        grid, sem, scratch = (M // tm, N // tn), ("parallel", "parallel"), []
    else:
        def body(a_ref, b_ref, *rest):
            o_ref, acc = rest[-2], rest[-1]
            k = pl.program_id(2)

            @pl.when(k == 0)
            def _():
                acc[...] = jnp.zeros_like(acc)

            acc[...] += lax.dot_general(a_ref[...].astype(_MXU), b_ref[...].astype(_MXU), dims,
                                        preferred_element_type=F32)

            @pl.when(k == nk - 1)
            def _():
                finish(acc[...], rest[0] if has_add else None, o_ref)

        a_spec = pl.BlockSpec((tm, tk), lambda i, j, k: (i, k))
        b_spec = (pl.BlockSpec((tk, tn), lambda i, j, k: (k, j)) if mode == "nn"
                  else pl.BlockSpec((tn, tk), lambda i, j, k: (j, k)))
        o_spec = pl.BlockSpec((tm, tn), lambda i, j, k: (i, j))
        grid, sem, scratch = (M // tm, N // tn, nk), ("parallel", "parallel", "arbitrary"), [pltpu.VMEM((tm, tn), F32)]

    ins = [a, b] + ([add] if has_add else [])
    specs = [a_spec, b_spec] + ([o_spec] if has_add else [])
    return _pc(body, name=name, out_shape=_sds((M, N), out_dtype), grid=grid, in_specs=specs, out_specs=o_spec,
               scratch=scratch, sem=sem, xchg=xchg)(*ins)


def _dot(a, b, dims):
    return lax.dot_general(a.astype(_MXU), b.astype(_MXU), dims, preferred_element_type=F32)


def _split(a):
    hi = a.astype(BF)
    lo = (a - hi.astype(F32)).astype(BF)
    return hi, lo


def _dot3(a, b, dims):
    ah, al = _split(a)
    bh, bl = _split(b)
    d = lambda u, v: lax.dot_general(u, v, dims, preferred_element_type=F32)
    return d(ah, bh) + (d(ah, bl) + d(al, bh))


def _softplus(x):
    return jnp.maximum(x, 0.0) + jnp.log1p(jnp.exp(-jnp.abs(x)))


def _silu(x):
    return x * jax.nn.sigmoid(x)


def _rmsnorm(x, g):
    return x * lax.rsqrt(jnp.mean(x * x, axis=-1, keepdims=True) + NORM_EPS) * g


def _layernorm(x, g, b):
    mu = jnp.mean(x, axis=-1, keepdims=True)
    xc = x - mu
    var = jnp.mean(xc * xc, axis=-1, keepdims=True)
    return xc * lax.rsqrt(var + NORM_EPS) * g + b


def _shift_dn(x, s):
    if s == 0:
        return x
    t = lax.broadcasted_iota(jnp.int32, x.shape, 0)
    return jnp.where(t >= s, pltpu.roll(x, s, 0), 0.0)


def _shift_up(x, s):
    if s == 0:
        return x
    n = x.shape[0]
    t = lax.broadcasted_iota(jnp.int32, x.shape, 0)
    return jnp.where(t < n - s, pltpu.roll(x, n - s, 0), 0.0)


def _taps(x, kk):
    return [_shift_dn(x, kk - 1 - j) for j in range(kk)]


def _conv(x, w, taps=None):
    kk = w.shape[0]
    taps = _taps(x, kk) if taps is None else taps
    y = w[kk - 1:kk, :] * taps[kk - 1]
    for j in range(kk - 1):
        y = y + w[j:j + 1, :] * taps[j]
    return y


def _conv_bwd_x(dy, w):
    kk = w.shape[0]
    dx = w[kk - 1:kk, :] * dy
    for j in range(kk - 1):
        dx = dx + w[j:j + 1, :] * _shift_up(dy, kk - 1 - j)
    return dx


def _conv_bwd_w(taps, dy, dw_ref, first):
    kk = dw_ref.shape[0]
    for j in range(kk):
        r = jnp.sum(dy * taps[j], axis=0, keepdims=True)
        prev = jnp.where(first, 0.0, dw_ref[j:j + 1, :])
        dw_ref[j:j + 1, :] = prev + r


def _gdn_post_conv(c, j):
    h = _silu(c)
    hn = h * lax.rsqrt(jnp.sum(h * h, axis=-1, keepdims=True) + NORM_EPS)
    return jnp.where(j < GDN_HEADS, hn * (GDN_DK ** -0.5), jnp.where(j < 2 * GDN_HEADS, hn, h))


def _gdn_proj_pre_fwd(xb, wqkv, conv_w, Bl, S):
    T, D = xb.shape
    C = wqkv.shape[1]
    nj = C // 256

    def body(x_ref, p_ref, w_ref, qkv_ref, o_ref):
        j = pl.program_id(1)
        qkv = _dot(x_ref[...], p_ref[...], _NN)
        qkv_ref[...] = qkv
        c = _conv(qkv, w_ref[...])
        for half in range(2):
            o_ref[:, 128 * half:128 * (half + 1)] = _gdn_post_conv(c[:, 128 * half:128 * (half + 1)], 2 * j)

    blk = pl.BlockSpec((S, 256), lambda b, j: (b, j))
    return _pc(body, name="gdn_proj_pre_fwd", out_shape=(_sds((T, C)), _sds((T, C))), grid=(Bl, nj),
               in_specs=[pl.BlockSpec((S, D), lambda b, j: (b, 0)), pl.BlockSpec((D, 256), lambda b, j: (0, j)),
                         pl.BlockSpec((conv_w.shape[0], 256), lambda b, j: (0, j))],
               out_specs=(blk, blk), sem=("parallel", "parallel"))(xb, wqkv, conv_w)


def _gdn_pre_bwd(qkv, conv_w, dout, dcat, Bl, S):
    T, C = qkv.shape
    nj = C // 128
    kk = conv_w.shape[0]

    def body(x_ref, w_ref, d_ref, alias_ref, dx_ref, dw_ref):
        j, b = pl.program_id(0), pl.program_id(1)
        w = w_ref[...]
        taps = _taps(x_ref[...], kk)
        c = _conv(None, w, taps)
        _, vjp = jax.vjp(lambda u: _gdn_post_conv(u, j), c)
        (dc,) = vjp(d_ref[...])
        dx_ref[...] = _conv_bwd_x(dc, w).astype(dx_ref.dtype)
        _conv_bwd_w(taps, dc, dw_ref, b == 0)

    blk = pl.BlockSpec((S, 128), lambda j, b: (b, j))
    wblk = pl.BlockSpec((kk, 128), lambda j, b: (0, j))
    return _pc(body, name="gdn_pre_bwd", out_shape=(_sds(dcat.shape, dcat.dtype), _sds((kk, C))), grid=(nj, Bl),
               in_specs=[blk, wblk, blk, ANY],
               out_specs=(pl.BlockSpec((S, 128), lambda j, b: (b, DCAT_QKV // 128 + j)), wblk),
               aliases={3: 0}, sem=("parallel", "arbitrary"))(qkv, conv_w, dout, dcat)


_H = GDN_HEADS
_HC = _H * CHUNK


def _st(x):
    return jnp.concatenate([x[:, h * 128:(h + 1) * 128] for h in range(_H)], axis=0)


def _unst(x):
    return jnp.concatenate([x[h * CHUNK:(h + 1) * CHUNK] for h in range(_H)], axis=1)


def _stc(t):
    return jnp.concatenate([t[:, h:h + 1] for h in range(_H)], axis=0)


def _untile(col):
    lane = lax.broadcasted_iota(jnp.int32, (CHUNK, 128), 1)
    out = jnp.zeros((CHUNK, 128), F32)
    for h in range(_H):
        out = out + jnp.where(lane == h, col[h * CHUNK:(h + 1) * CHUNK], 0.0)
    return out


def _rowform(col):
    return jnp.broadcast_to(col, (_HC, 128)).T[0:1, :]


def _tri(n, upper=False):
    i = lax.broadcasted_iota(jnp.int32, (n, n), 0)
    j = lax.broadcasted_iota(jnp.int32, (n, n), 1)
    return jnp.where((j >= i) if upper else (j <= i), 1.0, 0.0).astype(F32)


def _gdn_chunk_common(q, k, v, gc, beta):
    Q, K, V = _st(q), _st(k), _st(v)
    B, GC = _stc(beta), _stc(gc)
    GL = jnp.concatenate([jnp.broadcast_to(gc[CHUNK - 1:CHUNK, h:h + 1], (CHUNK, 1)) for h in range(_H)], axis=0)
    ii = lax.broadcasted_iota(jnp.int32, (_HC, _HC), 0)
    jj = lax.broadcasted_iota(jnp.int32, (_HC, _HC), 1)
    same = (ii >> 6) == (jj >> 6)
    incl = same & (ii >= jj)
    strict = same & (ii > jj)
    diff = GC - _rowform(GC)
    D = jnp.where(incl, jnp.exp(jnp.where(incl, diff, 0.0)), 0.0)
    KB = K * B
    A = jnp.where(strict, _dot(KB, K, _NT) * D, 0.0)
    P = jnp.where(incl, _dot(Q, K, _NT) * D, 0.0)
    EG = jnp.exp(GC)
    ED = jnp.exp(GL - GC)
    return dict(Q=Q, K=K, V=V, B=B, GC=GC, GL=GL, incl=incl, strict=strict, D=D, KB=KB, A=A, P=P, EG=EG, ED=ED,
                QG=Q * EG, KD=K * ED)


def _neumann_inv(A):
    n = A.shape[0]
    i = lax.broadcasted_iota(jnp.int32, (n, n), 0)
    j = lax.broadcasted_iota(jnp.int32, (n, n), 1)
    N = -A
    acc = jnp.where(i == j, 1.0, 0.0) + N
    Pw = N
    for _ in range(5):
        Pw = _dot3(Pw, Pw, _NN)
        acc = acc + _dot3(acc, Pw, _NN)
    return acc


def _gates(a, alog, dtb):
    return -jnp.exp(alog) * _softplus(a + dtb)


def _hs(x, h, n=CHUNK):
    return x[h * n:(h + 1) * n]


def _per_step(Bl):
    return 4 if Bl % 4 == 0 else 2 if Bl % 2 == 0 else 1


def _gdn_fwd(qkvn, sm, alog, dtb, Bl, S, xchg=None):
    T = Bl * S
    nC = S // CHUNK
    E = _per_step(Bl)

    def body(*refs):
        S_ref = refs[-1]

        @pl.when(pl.program_id(1) == 0)
        def _():
            S_ref[...] = jnp.zeros_like(S_ref)

        for e in range(E):
            one(*[r if i in (5, 6) else r.at[e] for i, r in enumerate(refs)])

    def one(q_ref, k_ref, v_ref, a_ref, b_ref, alog_ref, dtb_ref,
            o_ref, tinv_ref, sst_ref, w_ref, u_ref, gc_ref, beta_ref, S_ref):
        g = _gates(a_ref[...], alog_ref[...], dtb_ref[...])
        beta = jax.nn.sigmoid(b_ref[...])
        gc = jnp.dot(_tri(CHUNK), g, precision=HI, preferred_element_type=F32)
        cm = _gdn_chunk_common(q_ref[...], k_ref[...], v_ref[...], gc, beta)
        tinv = _neumann_inv(cm["A"])
        WU = _dot3(tinv, jnp.concatenate([cm["KB"] * cm["EG"], cm["V"] * cm["B"]], axis=1), _NN)
        W, U = WU[:, :128], WU[:, 128:]
        s_old = [S_ref[h * 128:(h + 1) * 128, :] for h in range(_H)]
        vn = [_hs(U, h) - _dot(_hs(W, h), s_old[h], _NN) for h in range(_H)]
        o_intra = _dot(cm["P"], jnp.concatenate(vn, axis=0), _NN)
        outs = []
        for h in range(_H):
            outs.append(_dot(_hs(cm["QG"], h), s_old[h], _NN) + _hs(o_intra, h))
            gl = jnp.exp(gc[CHUNK - 1:CHUNK, h:h + 1])
            S_ref[h * 128:(h + 1) * 128, :] = gl * s_old[h] + _dot(_hs(cm["KD"], h), vn[h], _TN)
            sst_ref[h * 128:(h + 1) * 128, :] = s_old[h]
        o_ref[...] = jnp.concatenate(outs, axis=1)
        tinv_ref[...] = tinv
        w_ref[...] = _unst(W)
        u_ref[...] = _unst(U)
        gc_ref[...] = gc
        beta_ref[...] = beta

    blk = lambda r, w, c: pl.BlockSpec((E, r, w), lambda b, n: (b, n, c))
    par = pl.BlockSpec((1, 128), lambda b, n: (0, 0))
    out_shape = (_sds((Bl, S, 512)), _sds((Bl, nC * _HC, _HC)), _sds((Bl, nC * 512, 128)),
                 _sds((Bl, S, 512)), _sds((Bl, S, 512)), _sds((Bl, S, 128)), _sds((Bl, S, 128)))
    out_specs = (blk(CHUNK, 512, 0), blk(_HC, _HC, 0), blk(512, 128, 0), blk(CHUNK, 512, 0), blk(CHUNK, 512, 0),
                 blk(CHUNK, 128, 0), blk(CHUNK, 128, 0))
    q3, s3 = qkvn.reshape(Bl, S, -1), sm.reshape(Bl, S, -1)
    res = _pc(body, name="gdn_chunk_fwd", out_shape=out_shape, grid=(Bl // E, nC),
              in_specs=[blk(CHUNK, 512, 0), blk(CHUNK, 512, 1), blk(CHUNK, 512, 2), blk(CHUNK, 128, 6),
                        blk(CHUNK, 128, 7), par, par],
              out_specs=out_specs, scratch=[pltpu.VMEM((E, 512, 128), F32)], sem=("parallel", "arbitrary"),
              xchg=xchg)(q3, q3, q3, s3, s3, alog, dtb)
    return (res[0].reshape(T, 512),) + tuple(res[1:])


def _gdn_bwd(qkvn, sm, alog, dtb, gc_s, beta_s, tinv_s, sst_s, w_s, u_s, do, dcat, Bl, S, xchg=None):
    T = Bl * S
    nC = S // CHUNK
    E = _per_step(Bl)

    def body(*refs):
        acc_ref, dS_ref = refs[-2], refs[-1]
        first_chunk = pl.program_id(1) == 0

        @pl.when(first_chunk)
        def _():
            dS_ref[...] = jnp.zeros_like(dS_ref)

        @pl.when(first_chunk & (pl.program_id(0) == 0))
        def _():
            acc_ref[...] = jnp.zeros_like(acc_ref)

        for e in range(E):
            one(*[r if i in (4, 5, 13, 16) else r.at[e] for i, r in enumerate(refs)])

    def one(q_ref, k_ref, v_ref, a_ref, alog_ref, dtb_ref, gc_ref, beta_ref, tinv_ref, sst_ref, w_ref, u_ref,
            do_ref, alias_ref, dqkv_ref, dsm_ref, acc_ref, dS_ref):
        gc, beta = gc_ref[...], beta_ref[...]
        cm = _gdn_chunk_common(q_ref[...], k_ref[...], v_ref[...], gc, beta)
        Q, K, V, B, D, KB, A, P = (cm[n] for n in ("Q", "K", "V", "B", "D", "KB", "A", "P"))
        EG, ED, QG, KD = cm["EG"], cm["ED"], cm["QG"], cm["KD"]
        tinv = tinv_ref[...]
        W, U, DO = _st(w_ref[...]), _st(u_ref[...]), _st(do_ref[...])
        s_old = [sst_ref[h * 128:(h + 1) * 128, :] for h in range(_H)]
        ds_new = [dS_ref[h * 128:(h + 1) * 128, :] for h in range(_H)]
        VN = jnp.concatenate([_hs(U, h) - _dot(_hs(W, h), s_old[h], _NN) for h in range(_H)], axis=0)
        dP = jnp.where(cm["incl"], _dot(DO, VN, _NT), 0.0)
        dVN0 = _dot(P, DO, _TN)
        dVN, dQG, dKD, dW, TL = [], [], [], [], []
        for h in range(_H):
            gl = jnp.exp(gc[CHUNK - 1:CHUNK, h:h + 1])
            dvn = _hs(dVN0, h) + _dot(_hs(KD, h), ds_new[h], _NN)
            dkd = _dot(_hs(VN, h), ds_new[h], _NT)
            dVN.append(dvn)
            dQG.append(_dot(_hs(DO, h), s_old[h], _NT))
            dKD.append(dkd)
            dW.append(-_dot(dvn, s_old[h], _NT))
            dgl = jnp.sum(jnp.sum(ds_new[h] * s_old[h], axis=1, keepdims=True), axis=0, keepdims=True)
            tl = jnp.sum(jnp.sum(dkd * _hs(KD, h), axis=1, keepdims=True), axis=0, keepdims=True) + dgl * gl
            TL.append(jnp.broadcast_to(tl, (CHUNK, 1)))
            dS_ref[h * 128:(h + 1) * 128, :] = (_dot(_hs(QG, h), _hs(DO, h), _TN) + gl * ds_new[h]
                                                - _dot(_hs(W, h), dvn, _TN))
        dVN, dQG, dKD, dW, TL = (jnp.concatenate(z, axis=0) for z in (dVN, dQG, dKD, dW, TL))
        dB = _dot3(tinv, jnp.concatenate([dVN, dW], axis=1), _TN)
        dVB, dKBE = dB[:, :128], dB[:, 128:]
        dA = jnp.where(cm["strict"], -_dot3(dB, jnp.concatenate([U, W], axis=1), _NT), 0.0)
        dG = dA * D
        dQK = dP * D
        dKB = _dot(dG, K, _NN) + dKBE * EG
        dK = (_dot(jnp.concatenate([dG, dQK], axis=0), jnp.concatenate([KB, Q], axis=0), _TN)
              + dKB * B + dKD * ED)
        dQ = _dot(dQK, K, _NN) + dQG * EG
        Mx = dA * A + dP * P
        rs = lambda z: jnp.sum(z, axis=1, keepdims=True)
        ri = lax.broadcasted_iota(jnp.int32, (_HC, 1), 0)
        dGC = (rs(Mx) - rs(Mx.T) + rs(dKBE * KB * EG) + rs(dQG * QG) - rs(dKD * KD)
               + jnp.where((ri & (CHUNK - 1)) == CHUNK - 1, TL, 0.0))
        dBeta = rs(dVB * V) + rs(dKB * K)
        dqkv_ref[:, 0:512] = _unst(dQ)
        dqkv_ref[:, 512:1024] = _unst(dK)
        dqkv_ref[:, 1024:1536] = _unst(dVB * B)
        dg = jnp.dot(_tri(CHUNK, upper=True), _untile(dGC), precision=HI, preferred_element_type=F32)
        a, alog_v, dtb_v = a_ref[...], alog_ref[...], dtb_ref[...]
        g = _gates(a, alog_v, dtb_v)
        lane = lax.broadcasted_iota(jnp.int32, (CHUNK, 128), 1)
        valid = lane < _H
        da = jnp.where(valid, dg * (-jnp.exp(alog_v)) * jax.nn.sigmoid(a + dtb_v), 0.0)
        dsm_ref[:, 0:128] = da.astype(dsm_ref.dtype)
        dsm_ref[:, 128:256] = jnp.where(valid, _untile(dBeta) * beta * (1.0 - beta), 0.0).astype(dsm_ref.dtype)
        acc_ref[0:1, :] += jnp.sum(jnp.where(valid, dg * g, 0.0), axis=0, keepdims=True)
        acc_ref[1:2, :] += jnp.sum(da, axis=0, keepdims=True)

    blk = lambda r, w, c: pl.BlockSpec((E, r, w), lambda b, n: (b, nC - 1 - n, c))
    par = pl.BlockSpec((1, 128), lambda b, n: (0, 0))
    in_specs = [blk(CHUNK, 512, 0), blk(CHUNK, 512, 1), blk(CHUNK, 512, 2), blk(CHUNK, 128, 6), par, par,
                blk(CHUNK, 128, 0), blk(CHUNK, 128, 0), blk(_HC, _HC, 0), blk(512, 128, 0),
                blk(CHUNK, 512, 0), blk(CHUNK, 512, 0), blk(CHUNK, 512, 0), ANY]
    out_shape = (_sds((Bl, S, 1536)), _sds((Bl, S, DCAT_W), BF), _sds((8, 128)))
    out_specs = (blk(CHUNK, 1536, 0), blk(CHUNK, 256, 3), pl.BlockSpec((8, 128), lambda b, n: (0, 0)))
    q3, s3 = qkvn.reshape(Bl, S, -1), sm.reshape(Bl, S, -1)
    res = _pc(body, name="gdn_chunk_bwd", out_shape=out_shape, grid=(Bl // E, nC), in_specs=in_specs,
              out_specs=out_specs, scratch=[pltpu.VMEM((E, 512, 128), F32)], sem=("arbitrary", "arbitrary"),
              aliases={13: 1}, xchg=xchg)(q3, q3, q3, s3, alog, dtb, gc_s, beta_s, tinv_s, sst_s, w_s, u_s,
                                          do.reshape(Bl, S, -1), dcat.reshape(Bl, S, DCAT_W))
    return (res[0].reshape(T, 1536), res[1].reshape(T, DCAT_W)) + tuple(res[2:])


def _gdn_post(o, z, g, T):
    tt = _pick(T, (2048, 1024, 512, 256))

    def body(o_ref, z_ref, g_ref, out_ref, outt_ref):
        y = (_rmsnorm(o_ref[...], g_ref[...]) * _silu(z_ref[...])).astype(out_ref.dtype)
        out_ref[...] = y
        outt_ref[...] = y.T

    blk = pl.BlockSpec((tt, 128), lambda i, h: (i, h))
    return _pc(body, name="gdn_post_fwd", out_shape=(_sds((T, 1024), BF), _sds((1024, T), BF)), grid=(T // tt, _H),
               in_specs=[blk, blk, pl.BlockSpec((1, 128), lambda i, h: (0, 0))],
               out_specs=(blk, pl.BlockSpec((128, tt), lambda i, h: (h, i))),
               sem=("parallel", "parallel"))(o, z, g)


def _gdn_post_bwd(o, z, g, dmixin, T):
    tt = _pick(T, (2048, 1024, 512, 256))

    def body(o_ref, z_ref, g_ref, d_ref, do_ref, dz_ref, dg_ref):
        @pl.when((pl.program_id(0) == 0) & (pl.program_id(1) == 0))
        def _():
            dg_ref[...] = jnp.zeros_like(dg_ref)

        _, vjp = jax.vjp(lambda a, b, c: _rmsnorm(a, c) * _silu(b), o_ref[...], z_ref[...], g_ref[...])
        do, dz, dg = vjp(d_ref[...])
        do_ref[...] = do
        dz_ref[...] = dz.astype(dz_ref.dtype)
        dg_ref[0:1, :] += dg

    blk = pl.BlockSpec((tt, 128), lambda i, h: (i, h))
    return _pc(body, name="gdn_post_bwd", out_shape=(_sds((T, 512)), _sds((T, DCAT_W), BF), _sds((8, 128))),
               grid=(T // tt, _H), in_specs=[blk, blk, pl.BlockSpec((1, 128), lambda i, h: (0, 0)), blk],
               out_specs=(blk, pl.BlockSpec((tt, 128), lambda i, h: (i, DCAT_Z // 128 + h)),
                          pl.BlockSpec((8, 128), lambda i, h: (0, 0))),
               sem=("arbitrary", "arbitrary"))(o, z, g, dmixin)


def _rms_fwd(sm, g, col_blk, name):
    T = sm.shape[0]
    d = g.shape[1]
    tt = _pick(T, (1024, 512, 256))

    def body(x_ref, g_ref, o_ref, ot_ref):
        y = _rmsnorm(x_ref[...], g_ref[...]).astype(o_ref.dtype)
        o_ref[...] = y
        ot_ref[...] = y.T

    return _pc(body, name=name, out_shape=(_sds((T, d), BF), _sds((d, T), BF)), grid=(T // tt,),
               in_specs=[pl.BlockSpec((tt, d), lambda i: (i, col_blk)), pl.BlockSpec((1, d), lambda i: (0, 0))],
               out_specs=(pl.BlockSpec((tt, d), lambda i: (i, 0)), pl.BlockSpec((d, tt), lambda i: (0, i))),
               sem=("parallel",))(sm, g)


def _rms_bwd(sm, g, dy, dsm, col_blk, name):
    T = sm.shape[0]
    d = g.shape[1]
    tt = _pick(T, (1024, 512, 256))

    def body(x_ref, g_ref, d_ref, alias_ref, dx_ref, dg_ref):
        @pl.when(pl.program_id(0) == 0)
        def _():
            dg_ref[...] = jnp.zeros_like(dg_ref)

        _, vjp = jax.vjp(_rmsnorm, x_ref[...], g_ref[...])
        dx, dg = vjp(d_ref[...])
        dx_ref[...] = dx.astype(dx_ref.dtype)
        dg_ref[0:1, :] += dg

    grp = pl.BlockSpec((tt, d), lambda i: (i, col_blk))
    return _pc(body, name=name, out_shape=(_sds(dsm.shape, dsm.dtype), _sds((8, d))), grid=(T // tt,),
               in_specs=[grp, pl.BlockSpec((1, d), lambda i: (0, 0)), pl.BlockSpec((tt, d), lambda i: (i, 0)), ANY],
               out_specs=(grp, pl.BlockSpec((8, d), lambda i: (0, 0))), aliases={3: 0},
               sem=("arbitrary",))(sm, g, dy, dsm)


def _rope_tables(S):
    inv = ROPE_THETA ** (-jnp.arange(0, MLA_ROPE, 2, dtype=F32) / MLA_ROPE)
    ang = jnp.arange(S, dtype=F32)[:, None] * inv[None, :]
    cos, sin = jnp.cos(ang), jnp.sin(ang)
    z = jnp.zeros((S, 64), F32)
    return jnp.concatenate([cos, cos, z], axis=1), jnp.concatenate([-sin, sin, z], axis=1)


def _swap_halves(x):
    lane = lax.broadcasted_iota(jnp.int32, x.shape, 1)
    return jnp.where(lane < 32, pltpu.roll(x, 96, 1), jnp.where(lane < 64, pltpu.roll(x, 32, 1), 0.0))


def _mm_rows(a, b, *, name, tm, extras, out_shape, out_specs, epi, mode="nn", sem="parallel"):
    M, K = a.shape
    nx = len(extras)

    def body(a_ref, b_ref, *rest):
        r = lax.dot_general(a_ref[...].astype(_MXU), b_ref[...].astype(_MXU), _NN if mode == "nn" else _NT,
                            preferred_element_type=F32)
        epi(r, rest[:nx], rest[nx:])

    in_specs = [pl.BlockSpec((tm, K), lambda i: (i, 0)), pl.BlockSpec(b.shape, lambda i: (0, 0))]
    return _pc(body, name=name, out_shape=out_shape, grid=(M // tm,), in_specs=in_specs + [s for _, s in extras],
               out_specs=out_specs, sem=(sem,))(a, b, *[x for x, _ in extras])


def _qup_rope(cqn, wq, cos, sin, S):
    T = cqn.shape[0]
    tm = _pick(S, (1024, 512, 256))
    nps = S // tm

    def epi(r, xs, outs):
        c, s = xs[0][...], xs[1][...]
        for h in range(MLA_HEADS):
            lo = 256 * h
            rp = r[:, lo + 128:lo + 256]
            outs[0][:, lo:lo + 128] = (r[:, lo:lo + 128] * ATTN_SCALE).astype(BF)
            outs[0][:, lo + 128:lo + 256] = ((rp * c + _swap_halves(rp) * s) * ATTN_SCALE).astype(BF)

    tab = pl.BlockSpec((tm, 128), lambda i: (i % nps, 0))
    return _mm_rows(cqn, wq, name="mm_qup_rope", tm=tm, extras=[(cos, tab), (sin, tab)],
                    out_shape=_sds((T, 1024), BF), out_specs=pl.BlockSpec((tm, 1024), lambda i: (i, 0)), epi=epi)


def _kvup_rope(ckvn, wkv, sm, cos, sin, S):
    T = ckvn.shape[0]
    tm = _pick(S, (1024, 512, 256))
    nps = S // tm

    def epi(r, xs, outs):
        k, c, s = xs[0][...], xs[1][...], xs[2][...]
        kr = (k * c + _swap_halves(k) * s).astype(BF)
        kvb = r.astype(BF)
        outs[0][...] = kvb
        for h in range(MLA_HEADS):
            lo = 256 * h
            outs[1][:, lo:lo + 128] = kvb[:, lo:lo + 128]
            outs[1][:, lo + 128:lo + 256] = kr

    tab = pl.BlockSpec((tm, 128), lambda i: (i % nps, 0))
    row = pl.BlockSpec((tm, 1024), lambda i: (i, 0))
    return _mm_rows(ckvn, wkv, name="mm_kvup_rope", tm=tm,
                    extras=[(sm, pl.BlockSpec((tm, 128), lambda i: (i, 3))), (cos, tab), (sin, tab)],
                    out_shape=(_sds((T, 1024), BF), _sds((T, 1024), BF)), out_specs=(row, row), epi=epi)


def _rope_bwd(dqc, dcat, dkrr, cos, sin, S):
    T = dqc.shape[0]
    tt = _pick(S, (2048, 1024, 512, 256))
    nps = S // tt

    def body(alias_s, dq_ref, dk_ref, c_ref, s_ref, qo_ref, ko_ref):
        c, s = c_ref[...], s_ref[...]
        d = dq_ref[:, 128:256]
        qo_ref[:, 0:128] = dq_ref[:, 0:128].astype(qo_ref.dtype)
        qo_ref[:, 128:256] = (d * c + _swap_halves(d * s)).astype(qo_ref.dtype)

        @pl.when(pl.program_id(1) == 0)
        def _():
            k = dk_ref[...]
            ko_ref[...] = (k * c + _swap_halves(k * s)).astype(ko_ref.dtype)

    tab = pl.BlockSpec((tt, 128), lambda i, h: (i % nps, 0))
    head = pl.BlockSpec((tt, 256), lambda i, h: (i, h))
    return _pc(body, name="rope_bwd", out_shape=(_sds((T, 1024), BF), _sds(dcat.shape, dcat.dtype)),
               grid=(T // tt, MLA_HEADS),
               in_specs=[ANY, head, pl.BlockSpec((tt, 128), lambda i, h: (i, 0)), tab, tab],
               out_specs=(head, pl.BlockSpec((tt, 128), lambda i, h: (i, 3))),
               aliases={0: 1}, sem=("parallel", "arbitrary"))(dcat, dqc, dkrr, cos, sin)


def _attn_scores(q, k_ref, L, tq):
    def sc(lo, hi):
        return _dot(q, k_ref[lo:hi, :], _NT)

    sd = sc(L - tq, L)
    qc = lax.broadcasted_iota(jnp.int32, sd.shape, 0) >> 6
    kc = lax.broadcasted_iota(jnp.int32, sd.shape, 1) >> 6
    sd = jnp.where(kc <= qc, sd, -jnp.inf)
    return sd if L == tq else jnp.concatenate([sc(0, L - tq), sd], axis=1)


def _attn_fwd(mixin, mixin_t, qcat, kcat, kv, Bl, S):
    T = Bl * S
    tq = _pick(S, (256, 128))
    nq = S // tq

    def body(alias_ref, alias_t, q_ref, k_ref, v_ref, o_ref, lse_ref, ot_ref):
        i = pl.program_id(2)
        q = q_ref[...]
        for ii in range(nq):
            @pl.when(i == ii)
            def _(L=(ii + 1) * tq):
                s = _attn_scores(q, k_ref, L, tq)
                m = jnp.max(s, axis=-1, keepdims=True)
                e = jnp.exp(s - m)
                l = jnp.sum(e, axis=-1, keepdims=True)
                o = (_dot(e, v_ref[0:L, :], _NN) / l).astype(o_ref.dtype)
                o_ref[...] = o
                ot_ref[...] = o.T
                lse_ref[...] = jnp.broadcast_to(m + jnp.log(l), lse_ref.shape)

    qrow = lambda b, h, i: b * nq + i
    in_specs = [ANY, ANY,
                pl.BlockSpec((tq, 256), lambda b, h, i: (qrow(b, h, i), h)),
                pl.BlockSpec((S, 256), lambda b, h, i: (b, h)),
                pl.BlockSpec((S, 128), lambda b, h, i: (b, 2 * h + 1))]
    return _pc(body, name="mla_attn_fwd",
               out_shape=(_sds(mixin.shape, mixin.dtype), _sds((T, 512)), _sds(mixin_t.shape, mixin_t.dtype)),
               grid=(Bl, MLA_HEADS, nq), in_specs=in_specs,
               out_specs=(pl.BlockSpec((tq, 128), lambda b, h, i: (qrow(b, h, i), 4 + h)),
                          pl.BlockSpec((tq, 128), lambda b, h, i: (qrow(b, h, i), h)),
                          pl.BlockSpec((128, tq), lambda b, h, i: (4 + h, qrow(b, h, i)))),
               aliases={0: 0, 1: 2}, sem=("parallel", "parallel", "parallel"))(mixin, mixin_t, qcat, kcat, kv)


def _attn_bwd(qcat, kcat, kv, lse, dmixin, Bl, S):
    T = Bl * S
    tq = _pick(S, (256, 128))
    nq = S // tq

    def body(q_ref, k_ref, v_ref, do_ref, lse_ref, dq_ref, dkv_ref, dkr_ref, acck_ref, accv_ref):
        h, i = pl.program_id(1), pl.program_id(2)
        q = q_ref[...]
        do = do_ref[...].astype(BF)
        lse_col = lse_ref[:, 0:1]

        @pl.when(i == 0)
        def _():
            acck_ref[...] = jnp.zeros_like(acck_ref)
            accv_ref[...] = jnp.zeros_like(accv_ref)

        for ii in range(nq):
            @pl.when(i == ii)
            def _(L=(ii + 1) * tq):
                p = jnp.exp(_attn_scores(q, k_ref, L, tq) - lse_col)
                pb = p.astype(BF)
                v = v_ref[0:L, :]
                dp = _dot(do, v, _NT)
                ds = (p * (dp - jnp.sum(dp * p, axis=-1, keepdims=True))).astype(BF)
                dq_ref[...] = _dot(ds, k_ref[0:L, :], _NN) * ATTN_SCALE
                acck_ref[0:L, :] += _dot(ds, q, _TN)
                accv_ref[0:L, :] += _dot(pb, do, _TN)

        @pl.when(i == nq - 1)
        def _():
            dkv_ref[:, 0:128] = acck_ref[:, 0:128].astype(dkv_ref.dtype)
            dkv_ref[:, 128:256] = accv_ref[...].astype(dkv_ref.dtype)
            dkr_ref[...] = jnp.where(h == 0, 0.0, dkr_ref[...]) + acck_ref[:, 128:256]

    qrow = lambda b, h, i: b * nq + i
    in_specs = [pl.BlockSpec((tq, 256), lambda b, h, i: (qrow(b, h, i), h)),
                pl.BlockSpec((S, 256), lambda b, h, i: (b, h)),
                pl.BlockSpec((S, 128), lambda b, h, i: (b, 2 * h + 1)),
                pl.BlockSpec((tq, 128), lambda b, h, i: (qrow(b, h, i), 4 + h)),
                pl.BlockSpec((tq, 128), lambda b, h, i: (qrow(b, h, i), h))]
    out_shape = (_sds((T, 1024)), _sds((T, 1024), BF), _sds((T, 128)))
    out_specs = (pl.BlockSpec((tq, 256), lambda b, h, i: (qrow(b, h, i), h)),
                 pl.BlockSpec((S, 256), lambda b, h, i: (b, h)),
                 pl.BlockSpec((S, 128), lambda b, h, i: (b, 0)))
    return _pc(body, name="mla_attn_bwd", out_shape=out_shape, grid=(Bl, MLA_HEADS, nq), in_specs=in_specs,
               out_specs=out_specs, scratch=[pltpu.VMEM((S, 256), F32), pltpu.VMEM((S, 128), F32)],
               sem=("arbitrary", "arbitrary", "arbitrary"))(qcat, kcat, kv, dmixin, lse)


def _out_ln1(mixin, w_out, x, g, b):
    T, D = x.shape
    tm = _pick(T, (512, 256))

    def epi(r, xs, outs):
        h = _layernorm(ALPHA * xs[0][...] + r, xs[1][...], xs[2][...])
        outs[0][...] = r
        outs[1][...] = h
        hb = h.astype(BF)
        outs[2][...] = hb
        outs[3][...] = hb.T

    row = pl.BlockSpec((tm, D), lambda i: (i, 0))
    par = pl.BlockSpec((1, D), lambda i: (0, 0))
    return _mm_rows(mixin, w_out, name="mm_out_ln1", tm=tm, extras=[(x, row), (g, par), (b, par)],
                    out_shape=(_sds((T, D)), _sds((T, D)), _sds((T, D), BF), _sds((D, T), BF)),
                    out_specs=(row, row, row, pl.BlockSpec((D, tm), lambda i: (0, i))), epi=epi)


def _dh1_ln1_bwd(du_u, wup_u, du_g, wup_g, dgpre, w_gate, dffn, x, mix, g, b):
    T, D = x.shape
    tm = _pick(T, (256,))

    def epi(r, xs, outs):
        dug_ref, wg_ref, dgp_ref, wgate_ref, dffn_ref, x_ref, mix_ref, g_ref, b_ref = xs
        acc_ref = outs[2]

        @pl.when(pl.program_id(0) == 0)
        def _():
            acc_ref[...] = jnp.zeros_like(acc_ref)

        dh = (r + _dot(dug_ref[...], wg_ref[...], _NT) + _dot(dgp_ref[...], wgate_ref[...], _NT)
              + ALPHA * dffn_ref[...])
        f = lambda xx, mm, gg, bb: _layernorm(ALPHA * xx + mm, gg, bb)
        _, vjp = jax.vjp(f, x_ref[...], mix_ref[...], g_ref[...], b_ref[...])
        dx, dm, dg, db = vjp(dh)
        outs[0][...] = dx
        outs[1][...] = dm.astype(BF)
        acc_ref[0:1, :] += dg
        acc_ref[1:2, :] += db

    row = pl.BlockSpec((tm, D), lambda i: (i, 0))
    par = pl.BlockSpec((1, D), lambda i: (0, 0))
    whole = lambda a: pl.BlockSpec(a.shape, lambda i: (0, 0))
    extras = [(du_g, pl.BlockSpec((tm, du_g.shape[1]), lambda i: (i, 0))), (wup_g, whole(wup_g)), (dgpre, row),
              (w_gate, whole(w_gate)), (dffn, row), (x, row), (mix, row), (g, par), (b, par)]
    return _mm_rows(du_u, wup_u, name="mm_dh1_ln1_bwd", tm=tm, mode="nt", sem="arbitrary", extras=extras,
                    out_shape=(_sds((T, D)), _sds((T, D), BF), _sds((8, D))),
                    out_specs=(row, row, pl.BlockSpec((8, D), lambda i: (0, 0))), epi=epi)


def _ffn_up_act_fwd(h1b, wup_g, wup_u, wg, wu, bg, bu, Bl, S):
    T, D = h1b.shape
    C = wup_g.shape[1]
    cb = _pick(C, (256, 128))
    kk = wg.shape[0]

    nj = C // cb

    def body(h_ref, pg_ref, pu_ref, ng_ref, nu_ref, wg_ref, wu_ref, bg_ref, bu_ref, ug_ref, uu_ref, o_ref, ot_ref,
             sg_ref, su_ref):
        j = pl.program_id(1)
        h = h_ref[...]

        @pl.when(j == 0)
        def _():
            sg_ref[0] = _dot(h, pg_ref[...], _NN)
            su_ref[0] = _dot(h, pu_ref[...], _NN)

        slot = j % 2
        ug, uu = sg_ref[slot], su_ref[slot]
        sg_ref[1 - slot] = _dot(h, ng_ref[...], _NN)
        su_ref[1 - slot] = _dot(h, nu_ref[...], _NN)
        ug_ref[...] = ug
        uu_ref[...] = uu
        cg = _conv(ug, wg_ref[...]) + bg_ref[...]
        cu = _conv(uu, wu_ref[...]) + bu_ref[...]
        a = (_silu(cg) * cu).astype(o_ref.dtype)
        o_ref[...] = a
        ot_ref[...] = a.T

    blk = pl.BlockSpec((S, cb), lambda b, j: (b, j))
    pblk = pl.BlockSpec((D, cb), lambda b, j: (0, j))
    nblk = pl.BlockSpec((D, cb), lambda b, j: (0, jnp.minimum(j + 1, nj - 1)))
    wblk = pl.BlockSpec((kk, cb), lambda b, j: (0, j))
    bblk = pl.BlockSpec((1, cb), lambda b, j: (0, j))
    return _pc(body, name="ffn_up_act_fwd", out_shape=(_sds((T, C)), _sds((T, C)), _sds((T, C), BF), _sds((C, T), BF)),
               grid=(Bl, nj),
               in_specs=[pl.BlockSpec((S, D), lambda b, j: (b, 0)), pblk, pblk, nblk, nblk, wblk, wblk, bblk, bblk],
               out_specs=(blk, blk, blk, pl.BlockSpec((cb, S), lambda b, j: (j, b))),
               scratch=[pltpu.VMEM((2, S, cb), F32), pltpu.VMEM((2, S, cb), F32)],
               sem=("parallel", "arbitrary"))(h1b, wup_g, wup_u, wup_g, wup_u, wg, wu, bg, bu)


def _ffn_act_bwd(ug, uu, wg, wu, bg, bu, dffnb, w_down, Bl, S):
    T, C = ug.shape
    D = dffnb.shape[1]
    cb = _pick(C, (256, 128))
    kk = wg.shape[0]

    def body(g_ref, u_ref, wg_ref, wu_ref, bg_ref, bu_ref, d_ref, wd_ref,
             dg_ref, du_ref, dwg_ref, dwu_ref, dbg_ref, dbu_ref):
        first = pl.program_id(1) == 0
        wgv, wuv = wg_ref[...], wu_ref[...]
        tg, tu = _taps(g_ref[...], kk), _taps(u_ref[...], kk)
        cg = _conv(None, wgv, tg) + bg_ref[...]
        cu = _conv(None, wuv, tu) + bu_ref[...]
        _, vjp = jax.vjp(lambda a, b: _silu(a) * b, cg, cu)
        dcg, dcu = vjp(_dot(d_ref[...], wd_ref[...], _NT))
        dg_ref[...] = _conv_bwd_x(dcg, wgv).astype(dg_ref.dtype)
        du_ref[...] = _conv_bwd_x(dcu, wuv).astype(du_ref.dtype)
        _conv_bwd_w(tg, dcg, dwg_ref, first)
        _conv_bwd_w(tu, dcu, dwu_ref, first)
        dbg_ref[...] = jnp.where(first, 0.0, dbg_ref[...]) + jnp.sum(dcg, axis=0, keepdims=True)
        dbu_ref[...] = jnp.where(first, 0.0, dbu_ref[...]) + jnp.sum(dcu, axis=0, keepdims=True)

    blk = pl.BlockSpec((S, cb), lambda j, b: (b, j))
    wblk = pl.BlockSpec((kk, cb), lambda j, b: (0, j))
    bblk = pl.BlockSpec((1, cb), lambda j, b: (0, j))
    out_shape = (_sds((T, C), BF), _sds((T, C), BF), _sds((kk, C)), _sds((kk, C)), _sds((1, C)), _sds((1, C)))
    return _pc(body, name="ffn_act_bwd", out_shape=out_shape, grid=(C // cb, Bl),
               in_specs=[blk, blk, wblk, wblk, bblk, bblk, pl.BlockSpec((S, D), lambda j, b: (b, 0)),
                         pl.BlockSpec((cb, D), lambda j, b: (j, 0))],
               out_specs=(blk, blk, wblk, wblk, bblk, bblk), sem=("parallel", "arbitrary"))(
        ug, uu, wg, wu, bg, bu, dffnb, w_down)


def _head(act, w_down, h1, h1b, w_gate, pb, w_proj, tgt, bg, g2, b2):
    T, D = h1.shape
    tm = _pick(T, (256,))

    def epi(ffn, xs, outs):
        h_ref, hb_ref, wg_ref, pb_ref, wp_ref, t_ref, bg_ref, g2_ref, b2_ref = xs
        acc_ref = outs[4]

        @pl.when(pl.program_id(0) == 0)
        def _():
            acc_ref[...] = jnp.zeros_like(acc_ref)

        h, tg = h_ref[...], t_ref[...]
        gpre = _dot(hb_ref[...], wg_ref[...], _NN)
        pp = _dot(pb_ref[...], wp_ref[...], _NN)

        def loss_fn(f, gp, p_, bgv, g2v, b2v):
            pre = ALPHA * h + f + jax.nn.sigmoid(gp + bgv) * p_
            err = _layernorm(pre, g2v, b2v) - tg
            return 0.5 * jnp.sum(jnp.mean(err * err, axis=-1, keepdims=True))

        loss, grads = jax.value_and_grad(loss_fn, argnums=(0, 1, 2, 3, 4, 5))(
            ffn, gpre, pp, bg_ref[...], g2_ref[...], b2_ref[...])
        outs[0][...] = grads[0]
        outs[1][...] = grads[0].astype(BF)
        outs[2][...] = grads[1].astype(BF)
        outs[3][...] = grads[2].astype(BF)
        acc_ref[0:1, :] += grads[3]
        acc_ref[1:2, :] += grads[4]
        acc_ref[2:3, :] += grads[5]
        acc_ref[3:4, :] += jnp.broadcast_to(loss, (1, D))

    row = pl.BlockSpec((tm, D), lambda i: (i, 0))
    par = pl.BlockSpec((1, D), lambda i: (0, 0))
    whole = lambda a: pl.BlockSpec(a.shape, lambda i: (0, 0))
    o_bf = _sds((T, D), BF)
    extras = [(h1, row), (h1b, row), (w_gate, whole(w_gate)), (pb, pl.BlockSpec((tm, pb.shape[1]), lambda i: (i, 0))),
              (w_proj, whole(w_proj)), (tgt, row), (bg, par), (g2, par), (b2, par)]
    return _mm_rows(act, w_down, name="mm_down_loss_head", tm=tm, sem="arbitrary", extras=extras,
                    out_shape=(_sds((T, D)), o_bf, o_bf, o_bf, _sds((8, D))),
                    out_specs=(row, row, row, row, pl.BlockSpec((8, D), lambda i: (0, 0))), epi=epi)


def _adamw(parts, w, m, v, name):
    R, C = w.shape
    tr = R if R <= 512 else _pick(R, (256,))

    def body(p_ref, w_ref, m_ref, v_ref, g_ref, d_ref, nm_ref, nv_ref):
        g = p_ref[0].astype(F32)
        for j in range(1, N_DEV):
            g = g + p_ref[j].astype(F32)
        g_ref[...] = g
        d_ref[...], nm_ref[...], nv_ref[...] = _adam_math(g, w_ref[...], m_ref[...], v_ref[...])

    blk = pl.BlockSpec((tr, C), lambda i: (i, 0))
    o = _sds((R, C))
    return _pc(body, name=name, out_shape=(o, o, o, o), grid=(R // tr,),
               in_specs=[pl.BlockSpec((N_DEV, tr, C), lambda i: (0, i, 0)), blk, blk, blk],
               out_specs=(blk, blk, blk, blk), sem=("parallel",))(parts, w, m, v)


def _adam_math(g, w, m, v):
    mm = ADAM_B1 * m + (1.0 - ADAM_B1) * g
    vv = ADAM_B2 * v + (1.0 - ADAM_B2) * jnp.square(g)
    m_hat = mm / (1.0 - ADAM_B1 ** ADAM_STEP)
    v_hat = vv / (1.0 - ADAM_B2 ** ADAM_STEP)
    return -ADAM_LR * (m_hat / (jnp.sqrt(v_hat) + ADAM_EPS) + ADAM_WD * w), mm, vv


def _small_layout(sizes):
    offs, off = {}, 0
    for n in SMALL_ORDER:
        offs[n] = off
        off += -(-sizes[n] // 128) * 128
    return offs, off


def _pack_small(d):
    return jnp.concatenate([_pad_lanes(d[n].reshape(1, -1), -(-d[n].size // 128) * 128) for n in SMALL_ORDER], axis=1)


def _adamw_small(parts, ws, ms, vs):
    k = len(SMALL_ORDER)
    sizes = {n: ws[n].shape[1] for n in SMALL_ORDER}
    offs, _ = _small_layout(sizes)

    def body(p_ref, *refs):
        ins, outs = refs[:3 * k], refs[3 * k:]
        for i, n in enumerate(SMALL_ORDER):
            lo, hi = offs[n], offs[n] + sizes[n]
            g = p_ref[0, :, lo:hi]
            for j in range(1, N_DEV):
                g = g + p_ref[j, :, lo:hi]
            d, mm, vv = _adam_math(g, ins[i][...], ins[k + i][...], ins[2 * k + i][...])
            outs[4 * i][...], outs[4 * i + 1][...], outs[4 * i + 2][...], outs[4 * i + 3][...] = g, d, mm, vv

    out_shape = tuple(_sds((1, sizes[n])) for n in SMALL_ORDER for _ in range(4))
    args = [ws[n] for n in SMALL_ORDER] + [ms[n] for n in SMALL_ORDER] + [vs[n] for n in SMALL_ORDER]
    res = _pc(body, name="adamw_small", out_shape=out_shape)(parts, *args)
    return {n: tuple(res[4 * i:4 * i + 4]) for i, n in enumerate(SMALL_ORDER)}


def _prep(x, p, xchg=None):
    T, D = x.shape
    Dp = p.shape[1]
    tt = _pick(T, (512, 256))

    def body(x_ref, p_ref, xb_ref, xt_ref, pb_ref, pt_ref):
        xb = x_ref[...].astype(BF)
        xb_ref[...] = xb
        xt_ref[...] = xb.T
        pb = p_ref[...].astype(BF)
        pb_ref[...] = pb
        pt_ref[...] = pb.T

    row = lambda w: pl.BlockSpec((tt, w), lambda i: (i, 0))
    col = lambda w: pl.BlockSpec((w, tt), lambda i: (0, i))
    return _pc(body, name="prep_inputs",
               out_shape=(_sds((T, D), BF), _sds((D, T), BF), _sds((T, Dp), BF), _sds((Dp, T), BF)),
               grid=(T // tt,), in_specs=[row(D), row(Dp)], out_specs=(row(D), col(D), row(Dp), col(Dp)),
               sem=("parallel",), xchg=xchg)(x, p)


SMALL_ORDER = ("gdn_a_log", "gdn_dt_bias", "gdn_norm_g", "mla_q_norm_g", "mla_kv_norm_g", "ln1_g", "ln1_b",
               "ffn_conv_b", "ple_b_gate", "ln2_g", "ln2_b")
EARLY = ("w_in", "gdn_conv_w")
LATE = ("mla_w_q_up", "mla_w_kv_up", "w_out", "ffn_w_up", "ffn_conv_w", "ffn_w_down", "ple_w_gate", "ple_w_proj")
GRADS_EARLY = ("w_out", "ffn_w_up", "ffn_conv_w", "ffn_w_down", "ple_w_gate", "ple_w_proj")
GRADS_LATE = ("w_in", "gdn_conv_w", "mla_w_q_up", "mla_w_kv_up")


def _pad_lanes(v, n=128):
    return jnp.pad(v, ((0, 0), (0, n - v.shape[1])))


def _local_step(x, p, tgt, W, sp, Bl, S, early_weights=None, late_weights=None, early_grads=None, late_grads=None):
    T = Bl * S
    W = dict(W)
    prep = _prep(x, p, xchg=None if early_weights is None else early_weights[:2])
    xb, xT, pb, pT = prep[:4]
    if early_weights is not None:
        W.update(early_weights[2](prep[4:]))
    w_in = W["w_in"]
    wqkv, wz = w_in[:, :1536], w_in[:, 1536:2048]
    z64 = jnp.zeros((w_in.shape[0], 64), w_in.dtype)
    z124 = jnp.zeros((w_in.shape[0], 124), w_in.dtype)
    wsm = jnp.concatenate([w_in[:, 2056:2440], w_in[:, 2696:2760], z64, w_in[:, 2440:2696],
                           w_in[:, 2048:2052], z124, w_in[:, 2052:2056], z124], axis=1)
    alog, dtb = _pad_lanes(sp["gdn_a_log"]), _pad_lanes(sp["gdn_dt_bias"])
    cos, sin = _rope_tables(S)

    qkv, qkvn = _gdn_proj_pre_fwd(xb, wqkv, W["gdn_conv_w"], Bl, S)
    z = _mm(xb, wz, "nn", name="mm_z")
    sm = _mm(xb, wsm, "nn", name="mm_sm")
    gdn_out = _gdn_fwd(qkvn, sm, alog, dtb, Bl, S, xchg=None if late_weights is None else late_weights[:2])
    o, tinv_s, sst_s, w_s, u_s, gc_s, beta_s = gdn_out[:7]
    if late_weights is not None:
        W.update(late_weights[2](gdn_out[7:]))
    wq = jnp.pad(W["mla_w_q_up"].reshape(-1, MLA_HEADS, 192), ((0, 0), (0, 0), (0, 64))).reshape(-1, 1024)
    wkv = W["mla_w_kv_up"]
    C = W["ffn_w_down"].shape[0]
    wup_g, wup_u = W["ffn_w_up"][:, :C], W["ffn_w_up"][:, C:]
    cw_g, cw_u = W["ffn_conv_w"][:, :C], W["ffn_conv_w"][:, C:]
    cb_g, cb_u = sp["ffn_conv_b"][:, :C], sp["ffn_conv_b"][:, C:]
    mixin, mixin_t = _gdn_post(o, z, sp["gdn_norm_g"], T)
    cqn, cqn_t = _rms_fwd(sm, sp["mla_q_norm_g"], 0, "rms_q_fwd")
    ckvn, ckvn_t = _rms_fwd(sm, sp["mla_kv_norm_g"], 2, "rms_kv_fwd")
    qcat = _qup_rope(cqn, wq, cos, sin, S)
    kv, kcat = _kvup_rope(ckvn, wkv, sm, cos, sin, S)
    mixin, lse, mixin_t = _attn_fwd(mixin, mixin_t, qcat, kcat, kv, Bl, S)
    mix, h1, h1b, h1T = _out_ln1(mixin, W["w_out"], x, sp["ln1_g"], sp["ln1_b"])
    ug, uu, act, act_t = _ffn_up_act_fwd(h1b, wup_g, wup_u, cw_g, cw_u, cb_g, cb_u, Bl, S)

    dffn, dffnb, dgpre, dpp, hacc = _head(act, W["ffn_w_down"], h1, h1b, W["ple_w_gate"], pb, W["ple_w_proj"], tgt,
                                          sp["ple_b_gate"], sp["ln2_g"], sp["ln2_b"])
    loss = hacc[3, 0]
    gW, gs = {}, {}
    gs["ple_b_gate"], gs["ln2_g"], gs["ln2_b"] = hacc[0:1], hacc[1:2], hacc[2:3]
    gW["ple_w_proj"] = _mm(pT, dpp, "nn", name="mm_dproj", out_dtype=BF)
    gW["ple_w_gate"] = _mm(h1T, dgpre, "nn", name="mm_dgate", out_dtype=BF)
    gW["ffn_w_down"] = _mm(act_t, dffnb, "nn", name="mm_ddown", out_dtype=BF)
    du_g, du_u, dcw_g, dcw_u, dcb_g, dcb_u = _ffn_act_bwd(ug, uu, cw_g, cw_u, cb_g, cb_u, dffnb, W["ffn_w_down"],
                                                          Bl, S)
    gW["ffn_conv_w"] = jnp.concatenate([dcw_g, dcw_u], axis=1)
    gs["ffn_conv_b"] = jnp.concatenate([dcb_g, dcb_u], axis=1)
    gW["ffn_w_up"] = jnp.concatenate([_mm(h1T, du_g, "nn", name="mm_dup_gate", out_dtype=BF),
                                      _mm(h1T, du_u, "nn", name="mm_dup_up", out_dtype=BF)], axis=1)
    dxa, dmix, acc1 = _dh1_ln1_bwd(du_u, wup_u, du_g, wup_g, dgpre, W["ple_w_gate"], dffn, x, mix,
                                   sp["ln1_g"], sp["ln1_b"])
    gs["ln1_g"], gs["ln1_b"] = acc1[0:1], acc1[1:2]
    gW["w_out"] = _mm(mixin_t, dmix, "nn", name="mm_dwout", out_dtype=BF)
    dmixin = _mm(dmix, W["w_out"], "nt", name="mm_dmixin")
    do, dcat, gacc = _gdn_post_bwd(o, z, sp["gdn_norm_g"], dmixin, T)
    gs["gdn_norm_g"] = gacc[0:1]
    bwd_out = _gdn_bwd(qkvn, sm, alog, dtb, gc_s, beta_s, tinv_s, sst_s, w_s, u_s, do, dcat, Bl, S,
                       xchg=None if early_grads is None else early_grads({n: gW[n] for n in GRADS_EARLY}))
    dqkvn, dcat, cacc = bwd_out[:3]
    early_recv = bwd_out[3:]
    gs["gdn_a_log"], gs["gdn_dt_bias"] = cacc[0:1, :GDN_HEADS], cacc[1:2, :GDN_HEADS]
    dcat, gW["gdn_conv_w"] = _gdn_pre_bwd(qkv, W["gdn_conv_w"], dqkvn, dcat, Bl, S)
    dqc, dkv, dkrr = _attn_bwd(qcat, kcat, kv, lse, dmixin, Bl, S)
    dqraw, dcat = _rope_bwd(dqc, dcat, dkrr, cos, sin, S)
    gwq = _mm(cqn_t, dqraw, "nn", name="mm_dwq", out_dtype=BF)
    gW["mla_w_q_up"] = gwq.reshape(-1, MLA_HEADS, 256)[:, :, :192].reshape(-1, MLA_HEADS * 192)
    gW["mla_w_kv_up"] = _mm(ckvn_t, dkv, "nn", name="mm_dwkv", out_dtype=BF)
    dcqn = _mm(dqraw, wq, "nt", name="mm_dcqn")
    dckvn = _mm(dkv, wkv, "nt", name="mm_dckvn")
    dcat, qacc = _rms_bwd(sm, sp["mla_q_norm_g"], dcqn, dcat, 0, "rms_q_bwd")
    dcat, kacc = _rms_bwd(sm, sp["mla_kv_norm_g"], dckvn, dcat, 2, "rms_kv_bwd")
    gs["mla_q_norm_g"], gs["mla_kv_norm_g"] = qacc[0:1], kacc[0:1]
    gcat = _mm(xT, dcat, "nn", name="mm_dwin", out_dtype=BF)
    gsm, gqkv, gz = gcat[:, :DCAT_QKV], gcat[:, DCAT_QKV:DCAT_Z], gcat[:, DCAT_Z:]
    gW["w_in"] = jnp.concatenate([gqkv, gz, gsm[:, 768:772], gsm[:, 896:900], gsm[:, 0:384], gsm[:, 512:768],
                                  gsm[:, 384:448]], axis=1)
    wcat = jnp.concatenate([wsm, wqkv, wz], axis=1)
    dx = _mm(dcat, wcat, "nt", name="mm_dx", add=dxa, xchg=None if late_grads is None else late_grads(gW, gs))
    late_recv = ()
    if late_grads is not None:
        dx, late_recv = dx[0], dx[1:]
    return loss, dx, gW, gs, early_recv, late_recv


COL_SHARDED = ("w_in", "mla_w_q_up", "mla_w_kv_up", "ffn_w_up", "ple_w_proj", "gdn_conv_w", "ffn_conv_w")
SHARDED = EARLY + LATE
WEIGHTS = ("w_in", "gdn_conv_w", "gdn_a_log", "gdn_dt_bias", "gdn_norm_g", "mla_q_norm_g", "mla_w_q_up",
           "mla_kv_norm_g", "mla_w_kv_up", "w_out", "ln1_g", "ln1_b", "ffn_w_up", "ffn_conv_w", "ffn_conv_b",
           "ffn_w_down", "ple_w_gate", "ple_b_gate", "ple_w_proj", "ln2_g", "ln2_b")
CONV = ("gdn_conv_w", "ffn_conv_w")


def _gathered_to_full(name, g):
    if name in COL_SHARDED:
        return jnp.transpose(g, (1, 0, 2)).reshape(g.shape[1], -1)
    return g.reshape(-1, g.shape[-1])


def _full_to_blocks(name, gfull, shard_shape):
    r, c = shard_shape
    if name in COL_SHARDED:
        return jnp.transpose(gfull.reshape(r, N_DEV, c), (1, 0, 2))
    return gfull.reshape(N_DEV, r, c)


def kernel(x, p, w_in, gdn_conv_w, gdn_a_log, gdn_dt_bias, gdn_norm_g, mla_q_norm_g, mla_w_q_up, mla_kv_norm_g, mla_w_kv_up, w_out, ln1_g, ln1_b, ffn_w_up, ffn_conv_w, ffn_conv_b, ffn_w_down, ple_w_gate, ple_b_gate, ple_w_proj, ln2_g, ln2_b, loss_target, m_w_in, m_gdn_conv_w, m_gdn_a_log, m_gdn_dt_bias, m_gdn_norm_g, m_mla_q_norm_g, m_mla_w_q_up, m_mla_kv_norm_g, m_mla_w_kv_up, m_w_out, m_ln1_g, m_ln1_b, m_ffn_w_up, m_ffn_conv_w, m_ffn_conv_b, m_ffn_w_down, m_ple_w_gate, m_ple_b_gate, m_ple_w_proj, m_ln2_g, m_ln2_b, v_w_in, v_gdn_conv_w, v_gdn_a_log, v_gdn_dt_bias, v_gdn_norm_g, v_mla_q_norm_g, v_mla_w_q_up, v_mla_kv_norm_g, v_mla_w_kv_up, v_w_out, v_ln1_g, v_ln1_b, v_ffn_w_up, v_ffn_conv_w, v_ffn_conv_b, v_ffn_w_down, v_ple_w_gate, v_ple_b_gate, v_ple_w_proj, v_ln2_g, v_ln2_b):
    loc = dict(locals())
    wts = {n: loc[n] for n in WEIGHTS}
    ms = {n: loc["m_" + n] for n in WEIGHTS}
    vs = {n: loc["v_" + n] for n in WEIGHTS}
    Bl, S, D = x.shape
    T = Bl * S

    wire = lambda n: wts[n][0] if n in CONV else wts[n][0].astype(BF)
    gather = lambda names, mode: ([wire(n) for n in names], [mode] * len(names),
                                  lambda res: {n: _gathered_to_full(n, g) for n, g in zip(names, res)})
    blocks = lambda g, names: [_full_to_blocks(n, g[n], wts[n].shape[1:]) for n in names]
    pack_early = lambda g: (blocks(g, GRADS_EARLY), ["a2a"] * len(GRADS_EARLY))
    pack_late = lambda g, gs: (blocks(g, GRADS_LATE) + [_pack_small(gs)], ["a2a"] * len(GRADS_LATE) + ["ag"])
    sp = {n: wts[n].reshape(1, -1) for n in SMALL_ORDER}

    loss, dx, gW, gs, early_recv, late_recv = _local_step(
        x.reshape(T, D), p.reshape(T, -1), loss_target.reshape(T, D), {}, sp, Bl, S,
        early_weights=gather(EARLY, "ag2"), late_weights=gather(LATE, "ag"), early_grads=pack_early,
        late_grads=pack_late)

    res = {}
    for n, parts in list(zip(GRADS_EARLY, early_recv)) + list(zip(GRADS_LATE, late_recv[:-1])):
        res[n] = tuple(t[None] for t in _adamw(parts, wts[n][0], ms[n][0], vs[n][0], "adamw_" + n))
    res.update(_adamw_small(late_recv[-1], wts, ms, vs))

    loss = lax.psum(loss, ("x", "y", "c"))
    outs = [loss, dx.reshape(Bl, S, D)]
    for k in range(4):
        outs += [res[n][k] for n in WEIGHTS]
    return tuple(outs)
```

```python
import math

import jax
import jax.numpy as jnp
from jax import lax
from jax.experimental import pallas as pl
from jax.experimental.pallas import tpu as pltpu

F32 = jnp.float32
BF = jnp.bfloat16
_MXU = jnp.bfloat16
_INTERPRET = None
_VMEM_LIMIT = 56 * 1024 * 1024
HI = lax.Precision.HIGHEST

N_DEV = 8
CHUNK = 64
GDN_HEADS = 4
GDN_DK = 128
MLA_HEADS = 4
MLA_NOPE = 128
MLA_ROPE = 64
ROPE_THETA = 10000.0
ALPHA = 2.0 ** 0.25
NORM_EPS = 1e-6
ATTN_SCALE = (MLA_NOPE + MLA_ROPE) ** -0.5
DCAT_QKV, DCAT_Z, DCAT_W = 1024, 2560, 3072
ADAM_LR, ADAM_B1, ADAM_B2, ADAM_EPS, ADAM_WD, ADAM_STEP = 0.001, 0.9, 0.999, 1e-08, 0.01, 10
MESH = pl.DeviceIdType.MESH
ANY = pl.BlockSpec(memory_space=pl.ANY)
_NN = (((1,), (0,)), ((), ()))
_NT = (((1,), (1,)), ((), ()))
_TN = (((0,), (0,)), ((), ()))


def _sds(shape, dtype=F32):
    return jax.ShapeDtypeStruct(tuple(shape), dtype)


def _pick(n, cands):
    for c in cands:
        if n % c == 0:
            return c
    return n


def _xchg_out_shapes(arrs, modes):
    return [_sds(a.shape if m == "a2a" else (N_DEV,) + a.shape, a.dtype) for a, m in zip(arrs, modes)]


def _xchg_scratch(n):
    return [pltpu.SemaphoreType.DMA((n, N_DEV - 1)), pltpu.SemaphoreType.DMA((n, N_DEV - 1)),
            pltpu.SemaphoreType.DMA((n,))]


def _xchg_plan(ins, outs, send, recv, loc, modes):
    x, y, c = lax.axis_index("x"), lax.axis_index("y"), lax.axis_index("c")
    me = 4 * x + 2 * y + c
    starts, relays, waits = [], [], []
    for ai, mode in enumerate(modes):
        src_all, out = ins[ai], outs[ai]

        def remote(k, src, dst, dev):
            return pltpu.make_async_remote_copy(src_ref=src, dst_ref=dst, send_sem=send.at[ai, k], recv_sem=recv.at[ai, k],
                                                device_id=dev, device_id_type=MESH)

        own = pltpu.make_async_copy(src_all.at[me] if mode == "a2a" else src_all, out.at[me], loc.at[ai])
        starts.append(own)
        waits.append(own.wait)
        if mode == "ag2":
            sib = (x, y, 1 - c)
            chips = [(1 - x, y), (x, 1 - y), (1 - x, 1 - y)]
            first = [remote(0, src_all, out.at[me], sib)]
            first += [remote(1 + k, src_all, out.at[me], (px, py, c)) for k, (px, py) in enumerate(chips)]
            starts += first
            waits += [cp.wait_send for cp in first]
            waits.append(remote(0, src_all, out.at[4 * x + 2 * y + 1 - c], sib).wait_recv)
            for k, (px, py) in enumerate(chips):
                same, other = out.at[4 * px + 2 * py + c], out.at[4 * px + 2 * py + 1 - c]
                relay = remote(4 + k, same, same, sib)
                relays.append((remote(1 + k, src_all, same, (px, py, c)), relay))
                waits += [relay.wait_send, remote(4 + k, src_all, other, sib).wait_recv]
            continue
        for r in range(1, N_DEV):
            px = 1 - x if (r >> 2) & 1 else x
            py = 1 - y if (r >> 1) & 1 else y
            pc = 1 - c if r & 1 else c
            cp = remote(r - 1, src_all.at[4 * px + 2 * py + pc] if mode == "a2a" else src_all, out.at[me], (px, py, pc))
            starts.append(cp)
            waits.append(cp.wait)
    return starts, relays, waits


def _pc(body, *, name, out_shape, grid=None, in_specs=None, out_specs=None, scratch=(), sem=None, aliases=None,
        xchg=None):
    kw = {}
    if _INTERPRET is not None:
        kw["interpret"] = _INTERPRET
    single = not isinstance(out_shape, (tuple, list))
    out_shape = [out_shape] if single else list(out_shape)
    if out_specs is not None:
        out_specs = [out_specs] if single else list(out_specs)
    scratch = list(scratch)
    if xchg is not None:
        xarrs, xmodes = xchg
        n_in, n_out, n_scr, nx = len(in_specs), len(out_shape), len(scratch), len(xarrs)
        single = False
        inner = body

        def body(*refs):
            ins, xins = refs[:n_in], refs[n_in:n_in + nx]
            outs = refs[n_in + nx:n_in + nx + n_out]
            xouts = refs[n_in + nx + n_out:n_in + 2 * nx + n_out]
            scr = refs[n_in + 2 * nx + n_out:n_in + 2 * nx + n_out + n_scr]
            send, recv, loc = refs[n_in + 2 * nx + n_out + n_scr:]
            first = last = None
            for d, g in enumerate(grid):
                f, l = pl.program_id(d) == 0, pl.program_id(d) == g - 1
                first, last = (f, l) if first is None else (first & f, last & l)

            @pl.when(first)
            def _():
                for cp in _xchg_plan(xins, xouts, send, recv, loc, xmodes)[0]:
                    cp.start()

            inner(*ins, *outs, *scr)

            @pl.when(last)
            def _():
                _, relays, waits = _xchg_plan(xins, xouts, send, recv, loc, xmodes)
                for arrival, relay in relays:
                    arrival.wait_recv()
                    relay.start()
                for wait in waits:
                    wait()

        in_specs = list(in_specs) + [ANY] * nx
        out_specs = out_specs + [ANY] * nx
        out_shape = out_shape + _xchg_out_shapes(xarrs, xmodes)
        scratch = scratch + _xchg_scratch(nx)
        sem = ("arbitrary",) * len(grid)
    if grid is not None:
        kw["grid"] = grid
    if in_specs is not None:
        kw["in_specs"] = in_specs
    if out_specs is not None:
        kw["out_specs"] = out_specs[0] if single else tuple(out_specs)
    cp = dict(vmem_limit_bytes=_VMEM_LIMIT)
    if sem is not None:
        cp["dimension_semantics"] = sem
    call = pl.pallas_call(body, name=name, out_shape=out_shape[0] if single else tuple(out_shape),
                          scratch_shapes=scratch, input_output_aliases=aliases or {},
                          compiler_params=pltpu.CompilerParams(**cp), **kw)
    if xchg is None:
        return call
    return lambda *ins: call(*ins, *xchg[0])


_FULL_K_MAX = 3072


def _mm(a, b, mode, *, name, add=None, add_scale=1.0, out_dtype=F32, xchg=None):
    (M, K) = a.shape
    (K2, N) = b.shape if mode == "nn" else b.shape[::-1]
    assert K == K2, (a.shape, b.shape, mode)
    tm = _pick(M, (1024, 1408, 512, 384, 256, 128))
    tn = _pick(N, (1024, 2816, 1408, 768, 512, 384, 256, 128))
    tk = K if K <= _FULL_K_MAX else _pick(K, (2048, 1408, 1024, 512) if tn <= 1408 else (1024, 512))
    nk = K // tk
    dims = _NN if mode == "nn" else _NT
    has_add = add is not None

    def finish(r, add_ref, o_ref):
        if has_add:
            r = r + add_scale * add_ref[...].astype(F32)
        o_ref[...] = r.astype(out_dtype)

    if nk == 1:
        def body(a_ref, b_ref, *rest):
            r = lax.dot_general(a_ref[...].astype(_MXU), b_ref[...].astype(_MXU), dims, preferred_element_type=F32)
            finish(r, rest[0] if has_add else None, rest[-1])

        a_spec = pl.BlockSpec((tm, K), lambda i, j: (i, 0))
        b_spec = (pl.BlockSpec((K, tn), lambda i, j: (0, j)) if mode == "nn"
                  else pl.BlockSpec((tn, K), lambda i, j: (j, 0)))
        o_spec = pl.BlockSpec((tm, tn), lambda i, j: (i, j))
        grid, sem, scratch = (M // tm, N // tn), ("parallel", "parallel"), []
    else:
        def body(a_ref, b_ref, *rest):
            o_ref, acc = rest[-2], rest[-1]
            k = pl.program_id(2)

            @pl.when(k == 0)
            def _():
                acc[...] = jnp.zeros_like(acc)

            acc[...] += lax.dot_general(a_ref[...].astype(_MXU), b_ref[...].astype(_MXU), dims,
                                        preferred_element_type=F32)

            @pl.when(k == nk - 1)
            def _():
                finish(acc[...], rest[0] if has_add else None, o_ref)

        a_spec = pl.BlockSpec((tm, tk), lambda i, j, k: (i, k))
        b_spec = (pl.BlockSpec((tk, tn), lambda i, j, k: (k, j)) if mode == "nn"
                  else pl.BlockSpec((tn, tk), lambda i, j, k: (j, k)))
        o_spec = pl.BlockSpec((tm, tn), lambda i, j, k: (i, j))
        grid, sem, scratch = (M // tm, N // tn, nk), ("parallel", "parallel", "arbitrary"), [pltpu.VMEM((tm, tn), F32)]

    ins = [a, b] + ([add] if has_add else [])
    specs = [a_spec, b_spec] + ([o_spec] if has_add else [])
    return _pc(body, name=name, out_shape=_sds((M, N), out_dtype), grid=grid, in_specs=specs, out_specs=o_spec,
               scratch=scratch, sem=sem, xchg=xchg)(*ins)


def _dot(a, b, dims):
    return lax.dot_general(a.astype(_MXU), b.astype(_MXU), dims, preferred_element_type=F32)


def _split(a):
    hi = a.astype(BF)
    lo = (a - hi.astype(F32)).astype(BF)
    return hi, lo


def _dot3(a, b, dims):
    ah, al = _split(a)
    bh, bl = _split(b)
    d = lambda u, v: lax.dot_general(u, v, dims, preferred_element_type=F32)
    return d(ah, bh) + (d(ah, bl) + d(al, bh))


def _softplus(x):
    return jnp.maximum(x, 0.0) + jnp.log1p(jnp.exp(-jnp.abs(x)))


def _silu(x):
    return x * jax.nn.sigmoid(x)


def _rmsnorm(x, g):
    return x * lax.rsqrt(jnp.mean(x * x, axis=-1, keepdims=True) + NORM_EPS) * g


def _layernorm(x, g, b):
    mu = jnp.mean(x, axis=-1, keepdims=True)
    xc = x - mu
    var = jnp.mean(xc * xc, axis=-1, keepdims=True)
    return xc * lax.rsqrt(var + NORM_EPS) * g + b


def _shift_dn(x, s):
    if s == 0:
        return x
    t = lax.broadcasted_iota(jnp.int32, x.shape, 0)
    return jnp.where(t >= s, pltpu.roll(x, s, 0), 0.0)


def _shift_up(x, s):
    if s == 0:
        return x
    n = x.shape[0]
    t = lax.broadcasted_iota(jnp.int32, x.shape, 0)
    return jnp.where(t < n - s, pltpu.roll(x, n - s, 0), 0.0)


def _taps(x, kk):
    return [_shift_dn(x, kk - 1 - j) for j in range(kk)]


def _conv(x, w, taps=None):
    kk = w.shape[0]
    taps = _taps(x, kk) if taps is None else taps
    y = w[kk - 1:kk, :] * taps[kk - 1]
    for j in range(kk - 1):
        y = y + w[j:j + 1, :] * taps[j]
    return y


def _conv_bwd_x(dy, w):
    kk = w.shape[0]
    dx = w[kk - 1:kk, :] * dy
    for j in range(kk - 1):
        dx = dx + w[j:j + 1, :] * _shift_up(dy, kk - 1 - j)
    return dx


def _conv_bwd_w(taps, dy, dw_ref, first):
    kk = dw_ref.shape[0]
    for j in range(kk):
        r = jnp.sum(dy * taps[j], axis=0, keepdims=True)
        prev = jnp.where(first, 0.0, dw_ref[j:j + 1, :])
        dw_ref[j:j + 1, :] = prev + r


def _gdn_post_conv(c, j):
    h = _silu(c)
    hn = h * lax.rsqrt(jnp.sum(h * h, axis=-1, keepdims=True) + NORM_EPS)
    return jnp.where(j < GDN_HEADS, hn * (GDN_DK ** -0.5), jnp.where(j < 2 * GDN_HEADS, hn, h))


def _gdn_proj_pre_fwd(xb, wqkv, conv_w, Bl, S):
    T, D = xb.shape
    C = wqkv.shape[1]
    nj = C // 256

    def body(x_ref, p_ref, w_ref, qkv_ref, o_ref):
        j = pl.program_id(1)
        qkv = _dot(x_ref[...], p_ref[...], _NN)
        qkv_ref[...] = qkv
        c = _conv(qkv, w_ref[...])
        for half in range(2):
            o_ref[:, 128 * half:128 * (half + 1)] = _gdn_post_conv(c[:, 128 * half:128 * (half + 1)], 2 * j)

    blk = pl.BlockSpec((S, 256), lambda b, j: (b, j))
    return _pc(body, name="gdn_proj_pre_fwd", out_shape=(_sds((T, C)), _sds((T, C))), grid=(Bl, nj),
               in_specs=[pl.BlockSpec((S, D), lambda b, j: (b, 0)), pl.BlockSpec((D, 256), lambda b, j: (0, j)),
                         pl.BlockSpec((conv_w.shape[0], 256), lambda b, j: (0, j))],
               out_specs=(blk, blk), sem=("parallel", "parallel"))(xb, wqkv, conv_w)


def _gdn_pre_bwd(qkv, conv_w, dout, dcat, Bl, S):
    T, C = qkv.shape
    nj = C // 128
    kk = conv_w.shape[0]

    def body(x_ref, w_ref, d_ref, alias_ref, dx_ref, dw_ref):
        j, b = pl.program_id(0), pl.program_id(1)
        w = w_ref[...]
        taps = _taps(x_ref[...], kk)
        c = _conv(None, w, taps)
        _, vjp = jax.vjp(lambda u: _gdn_post_conv(u, j), c)
        (dc,) = vjp(d_ref[...])
        dx_ref[...] = _conv_bwd_x(dc, w).astype(dx_ref.dtype)
        _conv_bwd_w(taps, dc, dw_ref, b == 0)

    blk = pl.BlockSpec((S, 128), lambda j, b: (b, j))
    wblk = pl.BlockSpec((kk, 128), lambda j, b: (0, j))
    return _pc(body, name="gdn_pre_bwd", out_shape=(_sds(dcat.shape, dcat.dtype), _sds((kk, C))), grid=(nj, Bl),
               in_specs=[blk, wblk, blk, ANY],
               out_specs=(pl.BlockSpec((S, 128), lambda j, b: (b, DCAT_QKV // 128 + j)), wblk),
               aliases={3: 0}, sem=("parallel", "arbitrary"))(qkv, conv_w, dout, dcat)


_H = GDN_HEADS
_HC = _H * CHUNK


def _st(x):
    return jnp.concatenate([x[:, h * 128:(h + 1) * 128] for h in range(_H)], axis=0)


def _unst(x):
    return jnp.concatenate([x[h * CHUNK:(h + 1) * CHUNK] for h in range(_H)], axis=1)


def _stc(t):
    return jnp.concatenate([t[:, h:h + 1] for h in range(_H)], axis=0)


def _untile(col):
    lane = lax.broadcasted_iota(jnp.int32, (CHUNK, 128), 1)
    out = jnp.zeros((CHUNK, 128), F32)
    for h in range(_H):
        out = out + jnp.where(lane == h, col[h * CHUNK:(h + 1) * CHUNK], 0.0)
    return out


def _rowform(col):
    return jnp.broadcast_to(col, (_HC, 128)).T[0:1, :]


def _tri(n, upper=False):
    i = lax.broadcasted_iota(jnp.int32, (n, n), 0)
    j = lax.broadcasted_iota(jnp.int32, (n, n), 1)
    return jnp.where((j >= i) if upper else (j <= i), 1.0, 0.0).astype(F32)


def _gdn_chunk_common(q, k, v, gc, beta):
    Q, K, V = _st(q), _st(k), _st(v)
    B, GC = _stc(beta), _stc(gc)
    GL = jnp.concatenate([jnp.broadcast_to(gc[CHUNK - 1:CHUNK, h:h + 1], (CHUNK, 1)) for h in range(_H)], axis=0)
    ii = lax.broadcasted_iota(jnp.int32, (_HC, _HC), 0)
    jj = lax.broadcasted_iota(jnp.int32, (_HC, _HC), 1)
    same = (ii >> 6) == (jj >> 6)
    incl = same & (ii >= jj)
    strict = same & (ii > jj)
    diff = GC - _rowform(GC)
    D = jnp.where(incl, jnp.exp(jnp.where(incl, diff, 0.0)), 0.0)
    KB = K * B
    A = jnp.where(strict, _dot(KB, K, _NT) * D, 0.0)
    P = jnp.where(incl, _dot(Q, K, _NT) * D, 0.0)
    EG = jnp.exp(GC)
    ED = jnp.exp(GL - GC)
    return dict(Q=Q, K=K, V=V, B=B, GC=GC, GL=GL, incl=incl, strict=strict, D=D, KB=KB, A=A, P=P, EG=EG, ED=ED,
                QG=Q * EG, KD=K * ED)


def _neumann_inv(A):
    n = A.shape[0]
    i = lax.broadcasted_iota(jnp.int32, (n, n), 0)
    j = lax.broadcasted_iota(jnp.int32, (n, n), 1)
    N = -A
    acc = jnp.where(i == j, 1.0, 0.0) + N
    Pw = N
    for _ in range(5):
        Pw = _dot3(Pw, Pw, _NN)
        acc = acc + _dot3(acc, Pw, _NN)
    return acc


def _gates(a, alog, dtb):
    return -jnp.exp(alog) * _softplus(a + dtb)


def _hs(x, h, n=CHUNK):
    return x[h * n:(h + 1) * n]


def _per_step(Bl):
    return 4 if Bl % 4 == 0 else 2 if Bl % 2 == 0 else 1


def _gdn_fwd(qkvn, sm, alog, dtb, Bl, S, xchg=None):
    T = Bl * S
    nC = S // CHUNK
    E = _per_step(Bl)

    def body(*refs):
        S_ref = refs[-1]

        @pl.when(pl.program_id(1) == 0)
        def _():
            S_ref[...] = jnp.zeros_like(S_ref)

        for e in range(E):
            one(*[r if i in (5, 6) else r.at[e] for i, r in enumerate(refs)])

    def one(q_ref, k_ref, v_ref, a_ref, b_ref, alog_ref, dtb_ref,
            o_ref, tinv_ref, sst_ref, w_ref, u_ref, gc_ref, beta_ref, S_ref):
        g = _gates(a_ref[...], alog_ref[...], dtb_ref[...])
        beta = jax.nn.sigmoid(b_ref[...])
        gc = jnp.dot(_tri(CHUNK), g, precision=HI, preferred_element_type=F32)
        cm = _gdn_chunk_common(q_ref[...], k_ref[...], v_ref[...], gc, beta)
        tinv = _neumann_inv(cm["A"])
        WU = _dot3(tinv, jnp.concatenate([cm["KB"] * cm["EG"], cm["V"] * cm["B"]], axis=1), _NN)
        W, U = WU[:, :128], WU[:, 128:]
        s_old = [S_ref[h * 128:(h + 1) * 128, :] for h in range(_H)]
        vn = [_hs(U, h) - _dot(_hs(W, h), s_old[h], _NN) for h in range(_H)]
        o_intra = _dot(cm["P"], jnp.concatenate(vn, axis=0), _NN)
        outs = []
        for h in range(_H):
            outs.append(_dot(_hs(cm["QG"], h), s_old[h], _NN) + _hs(o_intra, h))
            gl = jnp.exp(gc[CHUNK - 1:CHUNK, h:h + 1])
            S_ref[h * 128:(h + 1) * 128, :] = gl * s_old[h] + _dot(_hs(cm["KD"], h), vn[h], _TN)
            sst_ref[h * 128:(h + 1) * 128, :] = s_old[h]
        o_ref[...] = jnp.concatenate(outs, axis=1)
        tinv_ref[...] = tinv
        w_ref[...] = _unst(W)
        u_ref[...] = _unst(U)
        gc_ref[...] = gc
        beta_ref[...] = beta

    blk = lambda r, w, c: pl.BlockSpec((E, r, w), lambda b, n: (b, n, c))
    par = pl.BlockSpec((1, 128), lambda b, n: (0, 0))
    out_shape = (_sds((Bl, S, 512)), _sds((Bl, nC * _HC, _HC)), _sds((Bl, nC * 512, 128)),
                 _sds((Bl, S, 512)), _sds((Bl, S, 512)), _sds((Bl, S, 128)), _sds((Bl, S, 128)))
    out_specs = (blk(CHUNK, 512, 0), blk(_HC, _HC, 0), blk(512, 128, 0), blk(CHUNK, 512, 0), blk(CHUNK, 512, 0),
                 blk(CHUNK, 128, 0), blk(CHUNK, 128, 0))
    q3, s3 = qkvn.reshape(Bl, S, -1), sm.reshape(Bl, S, -1)
    res = _pc(body, name="gdn_chunk_fwd", out_shape=out_shape, grid=(Bl // E, nC),
              in_specs=[blk(CHUNK, 512, 0), blk(CHUNK, 512, 1), blk(CHUNK, 512, 2), blk(CHUNK, 128, 6),
                        blk(CHUNK, 128, 7), par, par],
              out_specs=out_specs, scratch=[pltpu.VMEM((E, 512, 128), F32)], sem=("parallel", "arbitrary"),
              xchg=xchg)(q3, q3, q3, s3, s3, alog, dtb)
    return (res[0].reshape(T, 512),) + tuple(res[1:])


def _gdn_bwd(qkvn, sm, alog, dtb, gc_s, beta_s, tinv_s, sst_s, w_s, u_s, do, dcat, Bl, S, xchg=None):
    T = Bl * S
    nC = S // CHUNK
    E = _per_step(Bl)

    def body(*refs):
        acc_ref, dS_ref = refs[-2], refs[-1]
        first_chunk = pl.program_id(1) == 0

        @pl.when(first_chunk)
        def _():
            dS_ref[...] = jnp.zeros_like(dS_ref)

        @pl.when(first_chunk & (pl.program_id(0) == 0))
        def _():
            acc_ref[...] = jnp.zeros_like(acc_ref)

        for e in range(E):
            one(*[r if i in (4, 5, 13, 16) else r.at[e] for i, r in enumerate(refs)])

    def one(q_ref, k_ref, v_ref, a_ref, alog_ref, dtb_ref, gc_ref, beta_ref, tinv_ref, sst_ref, w_ref, u_ref,
            do_ref, alias_ref, dqkv_ref, dsm_ref, acc_ref, dS_ref):
        gc, beta = gc_ref[...], beta_ref[...]
        cm = _gdn_chunk_common(q_ref[...], k_ref[...], v_ref[...], gc, beta)
        Q, K, V, B, D, KB, A, P = (cm[n] for n in ("Q", "K", "V", "B", "D", "KB", "A", "P"))
        EG, ED, QG, KD = cm["EG"], cm["ED"], cm["QG"], cm["KD"]
        tinv = tinv_ref[...]
        W, U, DO = _st(w_ref[...]), _st(u_ref[...]), _st(do_ref[...])
        s_old = [sst_ref[h * 128:(h + 1) * 128, :] for h in range(_H)]
        ds_new = [dS_ref[h * 128:(h + 1) * 128, :] for h in range(_H)]
        VN = jnp.concatenate([_hs(U, h) - _dot(_hs(W, h), s_old[h], _NN) for h in range(_H)], axis=0)
        dP = jnp.where(cm["incl"], _dot(DO, VN, _NT), 0.0)
        dVN0 = _dot(P, DO, _TN)
        dVN, dQG, dKD, dW, TL = [], [], [], [], []
        for h in range(_H):
            gl = jnp.exp(gc[CHUNK - 1:CHUNK, h:h + 1])
            dvn = _hs(dVN0, h) + _dot(_hs(KD, h), ds_new[h], _NN)
            dkd = _dot(_hs(VN, h), ds_new[h], _NT)
            dVN.append(dvn)
            dQG.append(_dot(_hs(DO, h), s_old[h], _NT))
            dKD.append(dkd)
            dW.append(-_dot(dvn, s_old[h], _NT))
            dgl = jnp.sum(jnp.sum(ds_new[h] * s_old[h], axis=1, keepdims=True), axis=0, keepdims=True)
            tl = jnp.sum(jnp.sum(dkd * _hs(KD, h), axis=1, keepdims=True), axis=0, keepdims=True) + dgl * gl
            TL.append(jnp.broadcast_to(tl, (CHUNK, 1)))
            dS_ref[h * 128:(h + 1) * 128, :] = (_dot(_hs(QG, h), _hs(DO, h), _TN) + gl * ds_new[h]
                                                - _dot(_hs(W, h), dvn, _TN))
        dVN, dQG, dKD, dW, TL = (jnp.concatenate(z, axis=0) for z in (dVN, dQG, dKD, dW, TL))
        dB = _dot3(tinv, jnp.concatenate([dVN, dW], axis=1), _TN)
        dVB, dKBE = dB[:, :128], dB[:, 128:]
        dA = jnp.where(cm["strict"], -_dot3(dB, jnp.concatenate([U, W], axis=1), _NT), 0.0)
        dG = dA * D
        dQK = dP * D
        dKB = _dot(dG, K, _NN) + dKBE * EG
        dK = (_dot(jnp.concatenate([dG, dQK], axis=0), jnp.concatenate([KB, Q], axis=0), _TN)
              + dKB * B + dKD * ED)
        dQ = _dot(dQK, K, _NN) + dQG * EG
        Mx = dA * A + dP * P
        rs = lambda z: jnp.sum(z, axis=1, keepdims=True)
        ri = lax.broadcasted_iota(jnp.int32, (_HC, 1), 0)
        dGC = (rs(Mx) - rs(Mx.T) + rs(dKBE * KB * EG) + rs(dQG * QG) - rs(dKD * KD)
               + jnp.where((ri & (CHUNK - 1)) == CHUNK - 1, TL, 0.0))
        dBeta = rs(dVB * V) + rs(dKB * K)
        dqkv_ref[:, 0:512] = _unst(dQ)
        dqkv_ref[:, 512:1024] = _unst(dK)
        dqkv_ref[:, 1024:1536] = _unst(dVB * B)
        dg = jnp.dot(_tri(CHUNK, upper=True), _untile(dGC), precision=HI, preferred_element_type=F32)
        a, alog_v, dtb_v = a_ref[...], alog_ref[...], dtb_ref[...]
        g = _gates(a, alog_v, dtb_v)
        lane = lax.broadcasted_iota(jnp.int32, (CHUNK, 128), 1)
        valid = lane < _H
        da = jnp.where(valid, dg * (-jnp.exp(alog_v)) * jax.nn.sigmoid(a + dtb_v), 0.0)
        dsm_ref[:, 0:128] = da.astype(dsm_ref.dtype)
        dsm_ref[:, 128:256] = jnp.where(valid, _untile(dBeta) * beta * (1.0 - beta), 0.0).astype(dsm_ref.dtype)
        acc_ref[0:1, :] += jnp.sum(jnp.where(valid, dg * g, 0.0), axis=0, keepdims=True)
        acc_ref[1:2, :] += jnp.sum(da, axis=0, keepdims=True)

    blk = lambda r, w, c: pl.BlockSpec((E, r, w), lambda b, n: (b, nC - 1 - n, c))
    par = pl.BlockSpec((1, 128), lambda b, n: (0, 0))
    in_specs = [blk(CHUNK, 512, 0), blk(CHUNK, 512, 1), blk(CHUNK, 512, 2), blk(CHUNK, 128, 6), par, par,
                blk(CHUNK, 128, 0), blk(CHUNK, 128, 0), blk(_HC, _HC, 0), blk(512, 128, 0),
                blk(CHUNK, 512, 0), blk(CHUNK, 512, 0), blk(CHUNK, 512, 0), ANY]
    out_shape = (_sds((Bl, S, 1536)), _sds((Bl, S, DCAT_W), BF), _sds((8, 128)))
    out_specs = (blk(CHUNK, 1536, 0), blk(CHUNK, 256, 3), pl.BlockSpec((8, 128), lambda b, n: (0, 0)))
    q3, s3 = qkvn.reshape(Bl, S, -1), sm.reshape(Bl, S, -1)
    res = _pc(body, name="gdn_chunk_bwd", out_shape=out_shape, grid=(Bl // E, nC), in_specs=in_specs,
              out_specs=out_specs, scratch=[pltpu.VMEM((E, 512, 128), F32)], sem=("arbitrary", "arbitrary"),
              aliases={13: 1}, xchg=xchg)(q3, q3, q3, s3, alog, dtb, gc_s, beta_s, tinv_s, sst_s, w_s, u_s,
                                          do.reshape(Bl, S, -1), dcat.reshape(Bl, S, DCAT_W))
    return (res[0].reshape(T, 1536), res[1].reshape(T, DCAT_W)) + tuple(res[2:])


def _gdn_post(o, z, g, T):
    tt = _pick(T, (2048, 1024, 512, 256))

    def body(o_ref, z_ref, g_ref, out_ref, outt_ref):
        y = (_rmsnorm(o_ref[...], g_ref[...]) * _silu(z_ref[...])).astype(out_ref.dtype)
        out_ref[...] = y
        outt_ref[...] = y.T

    blk = pl.BlockSpec((tt, 128), lambda i, h: (i, h))
    return _pc(body, name="gdn_post_fwd", out_shape=(_sds((T, 1024), BF), _sds((1024, T), BF)), grid=(T // tt, _H),
               in_specs=[blk, blk, pl.BlockSpec((1, 128), lambda i, h: (0, 0))],
               out_specs=(blk, pl.BlockSpec((128, tt), lambda i, h: (h, i))),
               sem=("parallel", "parallel"))(o, z, g)


def _gdn_post_bwd(o, z, g, dmixin, T):
    tt = _pick(T, (2048, 1024, 512, 256))

    def body(o_ref, z_ref, g_ref, d_ref, do_ref, dz_ref, dg_ref):
        @pl.when((pl.program_id(0) == 0) & (pl.program_id(1) == 0))
        def _():
            dg_ref[...] = jnp.zeros_like(dg_ref)

        _, vjp = jax.vjp(lambda a, b, c: _rmsnorm(a, c) * _silu(b), o_ref[...], z_ref[...], g_ref[...])
        do, dz, dg = vjp(d_ref[...])
        do_ref[...] = do
        dz_ref[...] = dz.astype(dz_ref.dtype)
        dg_ref[0:1, :] += dg

    blk = pl.BlockSpec((tt, 128), lambda i, h: (i, h))
    return _pc(body, name="gdn_post_bwd", out_shape=(_sds((T, 512)), _sds((T, DCAT_W), BF), _sds((8, 128))),
               grid=(T // tt, _H), in_specs=[blk, blk, pl.BlockSpec((1, 128), lambda i, h: (0, 0)), blk],
               out_specs=(blk, pl.BlockSpec((tt, 128), lambda i, h: (i, DCAT_Z // 128 + h)),
                          pl.BlockSpec((8, 128), lambda i, h: (0, 0))),
               sem=("arbitrary", "arbitrary"))(o, z, g, dmixin)


def _rms_fwd(sm, g, col_blk, name):
    T = sm.shape[0]
    d = g.shape[1]
    tt = _pick(T, (1024, 512, 256))

    def body(x_ref, g_ref, o_ref, ot_ref):
        y = _rmsnorm(x_ref[...], g_ref[...]).astype(o_ref.dtype)
        o_ref[...] = y
        ot_ref[...] = y.T

    return _pc(body, name=name, out_shape=(_sds((T, d), BF), _sds((d, T), BF)), grid=(T // tt,),
               in_specs=[pl.BlockSpec((tt, d), lambda i: (i, col_blk)), pl.BlockSpec((1, d), lambda i: (0, 0))],
               out_specs=(pl.BlockSpec((tt, d), lambda i: (i, 0)), pl.BlockSpec((d, tt), lambda i: (0, i))),
               sem=("parallel",))(sm, g)


def _rms_bwd(sm, g, dy, dsm, col_blk, name):
    T = sm.shape[0]
    d = g.shape[1]
    tt = _pick(T, (1024, 512, 256))

    def body(x_ref, g_ref, d_ref, alias_ref, dx_ref, dg_ref):
        @pl.when(pl.program_id(0) == 0)
        def _():
            dg_ref[...] = jnp.zeros_like(dg_ref)

        _, vjp = jax.vjp(_rmsnorm, x_ref[...], g_ref[...])
        dx, dg = vjp(d_ref[...])
        dx_ref[...] = dx.astype(dx_ref.dtype)
        dg_ref[0:1, :] += dg

    grp = pl.BlockSpec((tt, d), lambda i: (i, col_blk))
    return _pc(body, name=name, out_shape=(_sds(dsm.shape, dsm.dtype), _sds((8, d))), grid=(T // tt,),
               in_specs=[grp, pl.BlockSpec((1, d), lambda i: (0, 0)), pl.BlockSpec((tt, d), lambda i: (i, 0)), ANY],
               out_specs=(grp, pl.BlockSpec((8, d), lambda i: (0, 0))), aliases={3: 0},
               sem=("arbitrary",))(sm, g, dy, dsm)


def _rope_tables(S):
    inv = ROPE_THETA ** (-jnp.arange(0, MLA_ROPE, 2, dtype=F32) / MLA_ROPE)
    ang = jnp.arange(S, dtype=F32)[:, None] * inv[None, :]
    cos, sin = jnp.cos(ang), jnp.sin(ang)
    z = jnp.zeros((S, 64), F32)
    return jnp.concatenate([cos, cos, z], axis=1), jnp.concatenate([-sin, sin, z], axis=1)


def _swap_halves(x):
    lane = lax.broadcasted_iota(jnp.int32, x.shape, 1)
    return jnp.where(lane < 32, pltpu.roll(x, 96, 1), jnp.where(lane < 64, pltpu.roll(x, 32, 1), 0.0))


def _mm_rows(a, b, *, name, tm, extras, out_shape, out_specs, epi, mode="nn", sem="parallel"):
    M, K = a.shape
    nx = len(extras)

    def body(a_ref, b_ref, *rest):
        r = lax.dot_general(a_ref[...].astype(_MXU), b_ref[...].astype(_MXU), _NN if mode == "nn" else _NT,
                            preferred_element_type=F32)
        epi(r, rest[:nx], rest[nx:])

    in_specs = [pl.BlockSpec((tm, K), lambda i: (i, 0)), pl.BlockSpec(b.shape, lambda i: (0, 0))]
    return _pc(body, name=name, out_shape=out_shape, grid=(M // tm,), in_specs=in_specs + [s for _, s in extras],
               out_specs=out_specs, sem=(sem,))(a, b, *[x for x, _ in extras])


def _qup_rope(cqn, wq, cos, sin, S):
    T = cqn.shape[0]
    tm = _pick(S, (1024, 512, 256))
    nps = S // tm

    def epi(r, xs, outs):
        c, s = xs[0][...], xs[1][...]
        for h in range(MLA_HEADS):
            lo = 256 * h
            rp = r[:, lo + 128:lo + 256]
            outs[0][:, lo:lo + 128] = (r[:, lo:lo + 128] * ATTN_SCALE).astype(BF)
            outs[0][:, lo + 128:lo + 256] = ((rp * c + _swap_halves(rp) * s) * ATTN_SCALE).astype(BF)

    tab = pl.BlockSpec((tm, 128), lambda i: (i % nps, 0))
    return _mm_rows(cqn, wq, name="mm_qup_rope", tm=tm, extras=[(cos, tab), (sin, tab)],
                    out_shape=_sds((T, 1024), BF), out_specs=pl.BlockSpec((tm, 1024), lambda i: (i, 0)), epi=epi)


def _kvup_rope(ckvn, wkv, sm, cos, sin, S):
    T = ckvn.shape[0]
    tm = _pick(S, (1024, 512, 256))
    nps = S // tm

    def epi(r, xs, outs):
        k, c, s = xs[0][...], xs[1][...], xs[2][...]
        kr = (k * c + _swap_halves(k) * s).astype(BF)
        kvb = r.astype(BF)
        outs[0][...] = kvb
        for h in range(MLA_HEADS):
            lo = 256 * h
            outs[1][:, lo:lo + 128] = kvb[:, lo:lo + 128]
            outs[1][:, lo + 128:lo + 256] = kr

    tab = pl.BlockSpec((tm, 128), lambda i: (i % nps, 0))
    row = pl.BlockSpec((tm, 1024), lambda i: (i, 0))
    return _mm_rows(ckvn, wkv, name="mm_kvup_rope", tm=tm,
                    extras=[(sm, pl.BlockSpec((tm, 128), lambda i: (i, 3))), (cos, tab), (sin, tab)],
                    out_shape=(_sds((T, 1024), BF), _sds((T, 1024), BF)), out_specs=(row, row), epi=epi)


def _rope_bwd(dqc, dcat, dkrr, cos, sin, S):
    T = dqc.shape[0]
    tt = _pick(S, (2048, 1024, 512, 256))
    nps = S // tt

    def body(alias_s, dq_ref, dk_ref, c_ref, s_ref, qo_ref, ko_ref):
        c, s = c_ref[...], s_ref[...]
        d = dq_ref[:, 128:256]
        qo_ref[:, 0:128] = dq_ref[:, 0:128].astype(qo_ref.dtype)
        qo_ref[:, 128:256] = (d * c + _swap_halves(d * s)).astype(qo_ref.dtype)

        @pl.when(pl.program_id(1) == 0)
        def _():
            k = dk_ref[...]
            ko_ref[...] = (k * c + _swap_halves(k * s)).astype(ko_ref.dtype)

    tab = pl.BlockSpec((tt, 128), lambda i, h: (i % nps, 0))
    head = pl.BlockSpec((tt, 256), lambda i, h: (i, h))
    return _pc(body, name="rope_bwd", out_shape=(_sds((T, 1024), BF), _sds(dcat.shape, dcat.dtype)),
               grid=(T // tt, MLA_HEADS),
               in_specs=[ANY, head, pl.BlockSpec((tt, 128), lambda i, h: (i, 0)), tab, tab],
               out_specs=(head, pl.BlockSpec((tt, 128), lambda i, h: (i, 3))),
               aliases={0: 1}, sem=("parallel", "arbitrary"))(dcat, dqc, dkrr, cos, sin)


def _attn_scores(q, k_ref, L, tq):
    def sc(lo, hi):
        return _dot(q, k_ref[lo:hi, :], _NT)

    sd = sc(L - tq, L)
    qc = lax.broadcasted_iota(jnp.int32, sd.shape, 0) >> 6
    kc = lax.broadcasted_iota(jnp.int32, sd.shape, 1) >> 6
    sd = jnp.where(kc <= qc, sd, -jnp.inf)
    return sd if L == tq else jnp.concatenate([sc(0, L - tq), sd], axis=1)


def _attn_fwd(mixin, mixin_t, qcat, kcat, kv, Bl, S):
    T = Bl * S
    tq = _pick(S, (512, 256, 128))
    nq = S // tq

    def body(alias_ref, alias_t, q_ref, k_ref, v_ref, o_ref, lse_ref, ot_ref):
        i = pl.program_id(2)
        q = q_ref[...]
        for ii in range(nq):
            @pl.when(i == ii)
            def _(L=(ii + 1) * tq):
                s = _attn_scores(q, k_ref, L, tq)
                m = jnp.max(s, axis=-1, keepdims=True)
                e = jnp.exp(s - m)
                l = jnp.sum(e, axis=-1, keepdims=True)
                o = (_dot(e, v_ref[0:L, :], _NN) / l).astype(o_ref.dtype)
                o_ref[...] = o
                ot_ref[...] = o.T
                lse_ref[...] = jnp.broadcast_to(m + jnp.log(l), lse_ref.shape)

    qrow = lambda b, h, i: b * nq + i
    in_specs = [ANY, ANY,
                pl.BlockSpec((tq, 256), lambda b, h, i: (qrow(b, h, i), h)),
                pl.BlockSpec((S, 256), lambda b, h, i: (b, h)),
                pl.BlockSpec((S, 128), lambda b, h, i: (b, 2 * h + 1))]
    return _pc(body, name="mla_attn_fwd",
               out_shape=(_sds(mixin.shape, mixin.dtype), _sds((T, 512)), _sds(mixin_t.shape, mixin_t.dtype)),
               grid=(Bl, MLA_HEADS, nq), in_specs=in_specs,
               out_specs=(pl.BlockSpec((tq, 128), lambda b, h, i: (qrow(b, h, i), 4 + h)),
                          pl.BlockSpec((tq, 128), lambda b, h, i: (qrow(b, h, i), h)),
                          pl.BlockSpec((128, tq), lambda b, h, i: (4 + h, qrow(b, h, i)))),
               aliases={0: 0, 1: 2}, sem=("parallel", "parallel", "parallel"))(mixin, mixin_t, qcat, kcat, kv)


def _attn_bwd(qcat, kcat, kv, lse, dmixin, Bl, S):
    T = Bl * S
    tq = _pick(S, (512, 256, 128))
    nq = S // tq

    def body(q_ref, k_ref, v_ref, do_ref, lse_ref, dq_ref, dkv_ref, dkr_ref, acck_ref, accv_ref):
        h, i = pl.program_id(1), pl.program_id(2)
        q = q_ref[...]
        do = do_ref[...].astype(BF)
        lse_col = lse_ref[:, 0:1]

        @pl.when(i == 0)
        def _():
            acck_ref[...] = jnp.zeros_like(acck_ref)
            accv_ref[...] = jnp.zeros_like(accv_ref)

        for ii in range(nq):
            @pl.when(i == ii)
            def _(L=(ii + 1) * tq):
                p = jnp.exp(_attn_scores(q, k_ref, L, tq) - lse_col)
                pb = p.astype(BF)
                v = v_ref[0:L, :]
                dp = _dot(do, v, _NT)
                ds = (p * (dp - jnp.sum(dp * p, axis=-1, keepdims=True))).astype(BF)
                dq_ref[...] = _dot(ds, k_ref[0:L, :], _NN) * ATTN_SCALE
                acck_ref[0:L, :] += _dot(ds, q, _TN)
                accv_ref[0:L, :] += _dot(pb, do, _TN)

        @pl.when(i == nq - 1)
        def _():
            dkv_ref[:, 0:128] = acck_ref[:, 0:128].astype(dkv_ref.dtype)
            dkv_ref[:, 128:256] = accv_ref[...].astype(dkv_ref.dtype)
            dkr_ref[...] = jnp.where(h == 0, 0.0, dkr_ref[...]) + acck_ref[:, 128:256]

    qrow = lambda b, h, i: b * nq + i
    in_specs = [pl.BlockSpec((tq, 256), lambda b, h, i: (qrow(b, h, i), h)),
                pl.BlockSpec((S, 256), lambda b, h, i: (b, h)),
                pl.BlockSpec((S, 128), lambda b, h, i: (b, 2 * h + 1)),
                pl.BlockSpec((tq, 128), lambda b, h, i: (qrow(b, h, i), 4 + h)),
                pl.BlockSpec((tq, 128), lambda b, h, i: (qrow(b, h, i), h))]
    out_shape = (_sds((T, 1024)), _sds((T, 1024), BF), _sds((T, 128)))
    out_specs = (pl.BlockSpec((tq, 256), lambda b, h, i: (qrow(b, h, i), h)),
                 pl.BlockSpec((S, 256), lambda b, h, i: (b, h)),
                 pl.BlockSpec((S, 128), lambda b, h, i: (b, 0)))
    return _pc(body, name="mla_attn_bwd", out_shape=out_shape, grid=(Bl, MLA_HEADS, nq), in_specs=in_specs,
               out_specs=out_specs, scratch=[pltpu.VMEM((S, 256), F32), pltpu.VMEM((S, 128), F32)],
               sem=("arbitrary", "arbitrary", "arbitrary"))(qcat, kcat, kv, dmixin, lse)


def _out_ln1(mixin, w_out, x, g, b):
    T, D = x.shape
    tm = _pick(T, (512, 256))

    def epi(r, xs, outs):
        h = _layernorm(ALPHA * xs[0][...] + r, xs[1][...], xs[2][...])
        outs[0][...] = r
        outs[1][...] = h
        hb = h.astype(BF)
        outs[2][...] = hb
        outs[3][...] = hb.T

    row = pl.BlockSpec((tm, D), lambda i: (i, 0))
    par = pl.BlockSpec((1, D), lambda i: (0, 0))
    return _mm_rows(mixin, w_out, name="mm_out_ln1", tm=tm, extras=[(x, row), (g, par), (b, par)],
                    out_shape=(_sds((T, D)), _sds((T, D)), _sds((T, D), BF), _sds((D, T), BF)),
                    out_specs=(row, row, row, pl.BlockSpec((D, tm), lambda i: (0, i))), epi=epi)


def _dh1_ln1_bwd(du_u, wup_u, du_g, wup_g, dgpre, w_gate, dffn, x, mix, g, b):
    T, D = x.shape
    tm = _pick(T, (256,))

    def epi(r, xs, outs):
        dug_ref, wg_ref, dgp_ref, wgate_ref, dffn_ref, x_ref, mix_ref, g_ref, b_ref = xs
        acc_ref = outs[2]

        @pl.when(pl.program_id(0) == 0)
        def _():
            acc_ref[...] = jnp.zeros_like(acc_ref)

        dh = (r + _dot(dug_ref[...], wg_ref[...], _NT) + _dot(dgp_ref[...], wgate_ref[...], _NT)
              + ALPHA * dffn_ref[...])
        f = lambda xx, mm, gg, bb: _layernorm(ALPHA * xx + mm, gg, bb)
        _, vjp = jax.vjp(f, x_ref[...], mix_ref[...], g_ref[...], b_ref[...])
        dx, dm, dg, db = vjp(dh)
        outs[0][...] = dx
        outs[1][...] = dm.astype(BF)
        acc_ref[0:1, :] += dg
        acc_ref[1:2, :] += db

    row = pl.BlockSpec((tm, D), lambda i: (i, 0))
    par = pl.BlockSpec((1, D), lambda i: (0, 0))
    whole = lambda a: pl.BlockSpec(a.shape, lambda i: (0, 0))
    extras = [(du_g, pl.BlockSpec((tm, du_g.shape[1]), lambda i: (i, 0))), (wup_g, whole(wup_g)), (dgpre, row),
              (w_gate, whole(w_gate)), (dffn, row), (x, row), (mix, row), (g, par), (b, par)]
    return _mm_rows(du_u, wup_u, name="mm_dh1_ln1_bwd", tm=tm, mode="nt", sem="arbitrary", extras=extras,
                    out_shape=(_sds((T, D)), _sds((T, D), BF), _sds((8, D))),
                    out_specs=(row, row, pl.BlockSpec((8, D), lambda i: (0, 0))), epi=epi)


def _ffn_up_act_fwd(h1b, wup_g, wup_u, wg, wu, bg, bu, Bl, S):
    T, D = h1b.shape
    C = wup_g.shape[1]
    cb = _pick(C, (256, 128))
    kk = wg.shape[0]

    def body(h_ref, pg_ref, pu_ref, wg_ref, wu_ref, bg_ref, bu_ref, ug_ref, uu_ref, o_ref, ot_ref):
        h = h_ref[...]
        ug = _dot(h, pg_ref[...], _NN)
        uu = _dot(h, pu_ref[...], _NN)
        ug_ref[...] = ug
        uu_ref[...] = uu
        cg = _conv(ug, wg_ref[...]) + bg_ref[...]
        cu = _conv(uu, wu_ref[...]) + bu_ref[...]
        a = (_silu(cg) * cu).astype(o_ref.dtype)
        o_ref[...] = a
        ot_ref[...] = a.T

    blk = pl.BlockSpec((S, cb), lambda b, j: (b, j))
    pblk = pl.BlockSpec((D, cb), lambda b, j: (0, j))
    wblk = pl.BlockSpec((kk, cb), lambda b, j: (0, j))
    bblk = pl.BlockSpec((1, cb), lambda b, j: (0, j))
    return _pc(body, name="ffn_up_act_fwd", out_shape=(_sds((T, C)), _sds((T, C)), _sds((T, C), BF), _sds((C, T), BF)),
               grid=(Bl, C // cb), in_specs=[pl.BlockSpec((S, D), lambda b, j: (b, 0)), pblk, pblk, wblk, wblk, bblk, bblk],
               out_specs=(blk, blk, blk, pl.BlockSpec((cb, S), lambda b, j: (j, b))),
               sem=("parallel", "parallel"))(h1b, wup_g, wup_u, wg, wu, bg, bu)


def _ffn_act_bwd(ug, uu, wg, wu, bg, bu, dffnb, w_down, Bl, S):
    T, C = ug.shape
    D = dffnb.shape[1]
    cb = _pick(C, (256, 128))
    kk = wg.shape[0]

    def body(g_ref, u_ref, wg_ref, wu_ref, bg_ref, bu_ref, d_ref, wd_ref,
             dg_ref, du_ref, dwg_ref, dwu_ref, dbg_ref, dbu_ref):
        first = pl.program_id(1) == 0
        wgv, wuv = wg_ref[...], wu_ref[...]
        tg, tu = _taps(g_ref[...], kk), _taps(u_ref[...], kk)
        cg = _conv(None, wgv, tg) + bg_ref[...]
        cu = _conv(None, wuv, tu) + bu_ref[...]
        _, vjp = jax.vjp(lambda a, b: _silu(a) * b, cg, cu)
        dcg, dcu = vjp(_dot(d_ref[...], wd_ref[...], _NT))
        dg_ref[...] = _conv_bwd_x(dcg, wgv).astype(dg_ref.dtype)
        du_ref[...] = _conv_bwd_x(dcu, wuv).astype(du_ref.dtype)
        _conv_bwd_w(tg, dcg, dwg_ref, first)
        _conv_bwd_w(tu, dcu, dwu_ref, first)
        dbg_ref[...] = jnp.where(first, 0.0, dbg_ref[...]) + jnp.sum(dcg, axis=0, keepdims=True)
        dbu_ref[...] = jnp.where(first, 0.0, dbu_ref[...]) + jnp.sum(dcu, axis=0, keepdims=True)

    blk = pl.BlockSpec((S, cb), lambda j, b: (b, j))
    wblk = pl.BlockSpec((kk, cb), lambda j, b: (0, j))
    bblk = pl.BlockSpec((1, cb), lambda j, b: (0, j))
    out_shape = (_sds((T, C), BF), _sds((T, C), BF), _sds((kk, C)), _sds((kk, C)), _sds((1, C)), _sds((1, C)))
    return _pc(body, name="ffn_act_bwd", out_shape=out_shape, grid=(C // cb, Bl),
               in_specs=[blk, blk, wblk, wblk, bblk, bblk, pl.BlockSpec((S, D), lambda j, b: (b, 0)),
                         pl.BlockSpec((cb, D), lambda j, b: (j, 0))],
               out_specs=(blk, blk, wblk, wblk, bblk, bblk), sem=("parallel", "arbitrary"))(
        ug, uu, wg, wu, bg, bu, dffnb, w_down)


def _head(act, w_down, h1, h1b, w_gate, pb, w_proj, tgt, bg, g2, b2):
    T, D = h1.shape
    tm = _pick(T, (256,))

    def epi(ffn, xs, outs):
        h_ref, hb_ref, wg_ref, pb_ref, wp_ref, t_ref, bg_ref, g2_ref, b2_ref = xs
        acc_ref = outs[4]

        @pl.when(pl.program_id(0) == 0)
        def _():
            acc_ref[...] = jnp.zeros_like(acc_ref)

        h, tg = h_ref[...], t_ref[...]
        gpre = _dot(hb_ref[...], wg_ref[...], _NN)
        pp = _dot(pb_ref[...], wp_ref[...], _NN)

        def loss_fn(f, gp, p_, bgv, g2v, b2v):
            pre = ALPHA * h + f + jax.nn.sigmoid(gp + bgv) * p_
            err = _layernorm(pre, g2v, b2v) - tg
            return 0.5 * jnp.sum(jnp.mean(err * err, axis=-1, keepdims=True))

        loss, grads = jax.value_and_grad(loss_fn, argnums=(0, 1, 2, 3, 4, 5))(
            ffn, gpre, pp, bg_ref[...], g2_ref[...], b2_ref[...])
        outs[0][...] = grads[0]
        outs[1][...] = grads[0].astype(BF)
        outs[2][...] = grads[1].astype(BF)
        outs[3][...] = grads[2].astype(BF)
        acc_ref[0:1, :] += grads[3]
        acc_ref[1:2, :] += grads[4]
        acc_ref[2:3, :] += grads[5]
        acc_ref[3:4, :] += jnp.broadcast_to(loss, (1, D))

    row = pl.BlockSpec((tm, D), lambda i: (i, 0))
    par = pl.BlockSpec((1, D), lambda i: (0, 0))
    whole = lambda a: pl.BlockSpec(a.shape, lambda i: (0, 0))
    o_bf = _sds((T, D), BF)
    extras = [(h1, row), (h1b, row), (w_gate, whole(w_gate)), (pb, pl.BlockSpec((tm, pb.shape[1]), lambda i: (i, 0))),
              (w_proj, whole(w_proj)), (tgt, row), (bg, par), (g2, par), (b2, par)]
    return _mm_rows(act, w_down, name="mm_down_loss_head", tm=tm, sem="arbitrary", extras=extras,
                    out_shape=(_sds((T, D)), o_bf, o_bf, o_bf, _sds((8, D))),
                    out_specs=(row, row, row, row, pl.BlockSpec((8, D), lambda i: (0, 0))), epi=epi)


def _adamw(parts, w, m, v, name):
    R, C = w.shape
    tr = R if R <= 512 else _pick(R, (256,))

    def body(p_ref, w_ref, m_ref, v_ref, g_ref, d_ref, nm_ref, nv_ref):
        g = p_ref[0].astype(F32)
        for j in range(1, N_DEV):
            g = g + p_ref[j].astype(F32)
        g_ref[...] = g
        d_ref[...], nm_ref[...], nv_ref[...] = _adam_math(g, w_ref[...], m_ref[...], v_ref[...])

    blk = pl.BlockSpec((tr, C), lambda i: (i, 0))
    o = _sds((R, C))
    return _pc(body, name=name, out_shape=(o, o, o, o), grid=(R // tr,),
               in_specs=[pl.BlockSpec((N_DEV, tr, C), lambda i: (0, i, 0)), blk, blk, blk],
               out_specs=(blk, blk, blk, blk), sem=("parallel",))(parts, w, m, v)


def _adam_math(g, w, m, v):
    mm = ADAM_B1 * m + (1.0 - ADAM_B1) * g
    vv = ADAM_B2 * v + (1.0 - ADAM_B2) * jnp.square(g)
    m_hat = mm / (1.0 - ADAM_B1 ** ADAM_STEP)
    v_hat = vv / (1.0 - ADAM_B2 ** ADAM_STEP)
    return -ADAM_LR * (m_hat / (jnp.sqrt(v_hat) + ADAM_EPS) + ADAM_WD * w), mm, vv


def _small_layout(sizes):
    offs, off = {}, 0
    for n in SMALL_ORDER:
        offs[n] = off
        off += -(-sizes[n] // 128) * 128
    return offs, off


def _pack_small(d):
    return jnp.concatenate([_pad_lanes(d[n].reshape(1, -1), -(-d[n].size // 128) * 128) for n in SMALL_ORDER], axis=1)


def _adamw_small(parts, ws, ms, vs):
    k = len(SMALL_ORDER)
    sizes = {n: ws[n].shape[1] for n in SMALL_ORDER}
    offs, _ = _small_layout(sizes)

    def body(p_ref, *refs):
        ins, outs = refs[:3 * k], refs[3 * k:]
        for i, n in enumerate(SMALL_ORDER):
            lo, hi = offs[n], offs[n] + sizes[n]
            g = p_ref[0, :, lo:hi]
            for j in range(1, N_DEV):
                g = g + p_ref[j, :, lo:hi]
            d, mm, vv = _adam_math(g, ins[i][...], ins[k + i][...], ins[2 * k + i][...])
            outs[4 * i][...], outs[4 * i + 1][...], outs[4 * i + 2][...], outs[4 * i + 3][...] = g, d, mm, vv

    out_shape = tuple(_sds((1, sizes[n])) for n in SMALL_ORDER for _ in range(4))
    args = [ws[n] for n in SMALL_ORDER] + [ms[n] for n in SMALL_ORDER] + [vs[n] for n in SMALL_ORDER]
    res = _pc(body, name="adamw_small", out_shape=out_shape)(parts, *args)
    return {n: tuple(res[4 * i:4 * i + 4]) for i, n in enumerate(SMALL_ORDER)}


def _prep(x, p, xchg=None):
    T, D = x.shape
    Dp = p.shape[1]
    tt = _pick(T, (512, 256))

    def body(x_ref, p_ref, xb_ref, xt_ref, pb_ref, pt_ref):
        xb = x_ref[...].astype(BF)
        xb_ref[...] = xb
        xt_ref[...] = xb.T
        pb = p_ref[...].astype(BF)
        pb_ref[...] = pb
        pt_ref[...] = pb.T

    row = lambda w: pl.BlockSpec((tt, w), lambda i: (i, 0))
    col = lambda w: pl.BlockSpec((w, tt), lambda i: (0, i))
    return _pc(body, name="prep_inputs",
               out_shape=(_sds((T, D), BF), _sds((D, T), BF), _sds((T, Dp), BF), _sds((Dp, T), BF)),
               grid=(T // tt,), in_specs=[row(D), row(Dp)], out_specs=(row(D), col(D), row(Dp), col(Dp)),
               sem=("parallel",), xchg=xchg)(x, p)


SMALL_ORDER = ("gdn_a_log", "gdn_dt_bias", "gdn_norm_g", "mla_q_norm_g", "mla_kv_norm_g", "ln1_g", "ln1_b",
               "ffn_conv_b", "ple_b_gate", "ln2_g", "ln2_b")
EARLY = ("w_in", "gdn_conv_w")
LATE = ("mla_w_q_up", "mla_w_kv_up", "w_out", "ffn_w_up", "ffn_conv_w", "ffn_w_down", "ple_w_gate", "ple_w_proj")
GRADS_EARLY = ("w_out", "ffn_w_up", "ffn_conv_w", "ffn_w_down", "ple_w_gate", "ple_w_proj")
GRADS_LATE = ("w_in", "gdn_conv_w", "mla_w_q_up", "mla_w_kv_up")


def _pad_lanes(v, n=128):
    return jnp.pad(v, ((0, 0), (0, n - v.shape[1])))


def _local_step(x, p, tgt, W, sp, Bl, S, early_weights=None, late_weights=None, early_grads=None, late_grads=None):
    T = Bl * S
    W = dict(W)
    prep = _prep(x, p, xchg=None if early_weights is None else early_weights[:2])
    xb, xT, pb, pT = prep[:4]
    if early_weights is not None:
        W.update(early_weights[2](prep[4:]))
    w_in = W["w_in"]
    wqkv, wz = w_in[:, :1536], w_in[:, 1536:2048]
    z64 = jnp.zeros((w_in.shape[0], 64), w_in.dtype)
    z124 = jnp.zeros((w_in.shape[0], 124), w_in.dtype)
    wsm = jnp.concatenate([w_in[:, 2056:2440], w_in[:, 2696:2760], z64, w_in[:, 2440:2696],
                           w_in[:, 2048:2052], z124, w_in[:, 2052:2056], z124], axis=1)
    alog, dtb = _pad_lanes(sp["gdn_a_log"]), _pad_lanes(sp["gdn_dt_bias"])
    cos, sin = _rope_tables(S)

    qkv, qkvn = _gdn_proj_pre_fwd(xb, wqkv, W["gdn_conv_w"], Bl, S)
    z = _mm(xb, wz, "nn", name="mm_z")
    sm = _mm(xb, wsm, "nn", name="mm_sm")
    gdn_out = _gdn_fwd(qkvn, sm, alog, dtb, Bl, S, xchg=None if late_weights is None else late_weights[:2])
    o, tinv_s, sst_s, w_s, u_s, gc_s, beta_s = gdn_out[:7]
    if late_weights is not None:
        W.update(late_weights[2](gdn_out[7:]))
    wq = jnp.pad(W["mla_w_q_up"].reshape(-1, MLA_HEADS, 192), ((0, 0), (0, 0), (0, 64))).reshape(-1, 1024)
    wkv = W["mla_w_kv_up"]
    C = W["ffn_w_down"].shape[0]
    wup_g, wup_u = W["ffn_w_up"][:, :C], W["ffn_w_up"][:, C:]
    cw_g, cw_u = W["ffn_conv_w"][:, :C], W["ffn_conv_w"][:, C:]
    cb_g, cb_u = sp["ffn_conv_b"][:, :C], sp["ffn_conv_b"][:, C:]
    mixin, mixin_t = _gdn_post(o, z, sp["gdn_norm_g"], T)
    cqn, cqn_t = _rms_fwd(sm, sp["mla_q_norm_g"], 0, "rms_q_fwd")
    ckvn, ckvn_t = _rms_fwd(sm, sp["mla_kv_norm_g"], 2, "rms_kv_fwd")
    qcat = _qup_rope(cqn, wq, cos, sin, S)
    kv, kcat = _kvup_rope(ckvn, wkv, sm, cos, sin, S)
    mixin, lse, mixin_t = _attn_fwd(mixin, mixin_t, qcat, kcat, kv, Bl, S)
    mix, h1, h1b, h1T = _out_ln1(mixin, W["w_out"], x, sp["ln1_g"], sp["ln1_b"])
    ug, uu, act, act_t = _ffn_up_act_fwd(h1b, wup_g, wup_u, cw_g, cw_u, cb_g, cb_u, Bl, S)

    dffn, dffnb, dgpre, dpp, hacc = _head(act, W["ffn_w_down"], h1, h1b, W["ple_w_gate"], pb, W["ple_w_proj"], tgt,
                                          sp["ple_b_gate"], sp["ln2_g"], sp["ln2_b"])
    loss = hacc[3, 0]
    gW, gs = {}, {}
    gs["ple_b_gate"], gs["ln2_g"], gs["ln2_b"] = hacc[0:1], hacc[1:2], hacc[2:3]
    gW["ple_w_proj"] = _mm(pT, dpp, "nn", name="mm_dproj", out_dtype=BF)
    gW["ple_w_gate"] = _mm(h1T, dgpre, "nn", name="mm_dgate", out_dtype=BF)
    gW["ffn_w_down"] = _mm(act_t, dffnb, "nn", name="mm_ddown", out_dtype=BF)
    du_g, du_u, dcw_g, dcw_u, dcb_g, dcb_u = _ffn_act_bwd(ug, uu, cw_g, cw_u, cb_g, cb_u, dffnb, W["ffn_w_down"],
                                                          Bl, S)
    gW["ffn_conv_w"] = jnp.concatenate([dcw_g, dcw_u], axis=1)
    gs["ffn_conv_b"] = jnp.concatenate([dcb_g, dcb_u], axis=1)
    gW["ffn_w_up"] = jnp.concatenate([_mm(h1T, du_g, "nn", name="mm_dup_gate", out_dtype=BF),
                                      _mm(h1T, du_u, "nn", name="mm_dup_up", out_dtype=BF)], axis=1)
    dxa, dmix, acc1 = _dh1_ln1_bwd(du_u, wup_u, du_g, wup_g, dgpre, W["ple_w_gate"], dffn, x, mix,
                                   sp["ln1_g"], sp["ln1_b"])
    gs["ln1_g"], gs["ln1_b"] = acc1[0:1], acc1[1:2]
    gW["w_out"] = _mm(mixin_t, dmix, "nn", name="mm_dwout", out_dtype=BF)
    dmixin = _mm(dmix, W["w_out"], "nt", name="mm_dmixin")
    do, dcat, gacc = _gdn_post_bwd(o, z, sp["gdn_norm_g"], dmixin, T)
    gs["gdn_norm_g"] = gacc[0:1]
    bwd_out = _gdn_bwd(qkvn, sm, alog, dtb, gc_s, beta_s, tinv_s, sst_s, w_s, u_s, do, dcat, Bl, S,
                       xchg=None if early_grads is None else early_grads({n: gW[n] for n in GRADS_EARLY}))
    dqkvn, dcat, cacc = bwd_out[:3]
    early_recv = bwd_out[3:]
    gs["gdn_a_log"], gs["gdn_dt_bias"] = cacc[0:1, :GDN_HEADS], cacc[1:2, :GDN_HEADS]
    dcat, gW["gdn_conv_w"] = _gdn_pre_bwd(qkv, W["gdn_conv_w"], dqkvn, dcat, Bl, S)
    dqc, dkv, dkrr = _attn_bwd(qcat, kcat, kv, lse, dmixin, Bl, S)
    dqraw, dcat = _rope_bwd(dqc, dcat, dkrr, cos, sin, S)
    gwq = _mm(cqn_t, dqraw, "nn", name="mm_dwq", out_dtype=BF)
    gW["mla_w_q_up"] = gwq.reshape(-1, MLA_HEADS, 256)[:, :, :192].reshape(-1, MLA_HEADS * 192)
    gW["mla_w_kv_up"] = _mm(ckvn_t, dkv, "nn", name="mm_dwkv", out_dtype=BF)
    dcqn = _mm(dqraw, wq, "nt", name="mm_dcqn")
    dckvn = _mm(dkv, wkv, "nt", name="mm_dckvn")
    dcat, qacc = _rms_bwd(sm, sp["mla_q_norm_g"], dcqn, dcat, 0, "rms_q_bwd")
    dcat, kacc = _rms_bwd(sm, sp["mla_kv_norm_g"], dckvn, dcat, 2, "rms_kv_bwd")
    gs["mla_q_norm_g"], gs["mla_kv_norm_g"] = qacc[0:1], kacc[0:1]
    gcat = _mm(xT, dcat, "nn", name="mm_dwin", out_dtype=BF)
    gsm, gqkv, gz = gcat[:, :DCAT_QKV], gcat[:, DCAT_QKV:DCAT_Z], gcat[:, DCAT_Z:]
    gW["w_in"] = jnp.concatenate([gqkv, gz, gsm[:, 768:772], gsm[:, 896:900], gsm[:, 0:384], gsm[:, 512:768],
                                  gsm[:, 384:448]], axis=1)
    wcat = jnp.concatenate([wsm, wqkv, wz], axis=1)
    dx = _mm(dcat, wcat, "nt", name="mm_dx", add=dxa, xchg=None if late_grads is None else late_grads(gW, gs))
    late_recv = ()
    if late_grads is not None:
        dx, late_recv = dx[0], dx[1:]
    return loss, dx, gW, gs, early_recv, late_recv


COL_SHARDED = ("w_in", "mla_w_q_up", "mla_w_kv_up", "ffn_w_up", "ple_w_proj", "gdn_conv_w", "ffn_conv_w")
SHARDED = EARLY + LATE
WEIGHTS = ("w_in", "gdn_conv_w", "gdn_a_log", "gdn_dt_bias", "gdn_norm_g", "mla_q_norm_g", "mla_w_q_up",
           "mla_kv_norm_g", "mla_w_kv_up", "w_out", "ln1_g", "ln1_b", "ffn_w_up", "ffn_conv_w", "ffn_conv_b",
           "ffn_w_down", "ple_w_gate", "ple_b_gate", "ple_w_proj", "ln2_g", "ln2_b")
CONV = ("gdn_conv_w", "ffn_conv_w")


def _gathered_to_full(name, g):
    if name in COL_SHARDED:
        return jnp.transpose(g, (1, 0, 2)).reshape(g.shape[1], -1)
    return g.reshape(-1, g.shape[-1])


def _full_to_blocks(name, gfull, shard_shape):
    r, c = shard_shape
    if name in COL_SHARDED:
        return jnp.transpose(gfull.reshape(r, N_DEV, c), (1, 0, 2))
    return gfull.reshape(N_DEV, r, c)


def kernel(x, p, w_in, gdn_conv_w, gdn_a_log, gdn_dt_bias, gdn_norm_g, mla_q_norm_g, mla_w_q_up, mla_kv_norm_g, mla_w_kv_up, w_out, ln1_g, ln1_b, ffn_w_up, ffn_conv_w, ffn_conv_b, ffn_w_down, ple_w_gate, ple_b_gate, ple_w_proj, ln2_g, ln2_b, loss_target, m_w_in, m_gdn_conv_w, m_gdn_a_log, m_gdn_dt_bias, m_gdn_norm_g, m_mla_q_norm_g, m_mla_w_q_up, m_mla_kv_norm_g, m_mla_w_kv_up, m_w_out, m_ln1_g, m_ln1_b, m_ffn_w_up, m_ffn_conv_w, m_ffn_conv_b, m_ffn_w_down, m_ple_w_gate, m_ple_b_gate, m_ple_w_proj, m_ln2_g, m_ln2_b, v_w_in, v_gdn_conv_w, v_gdn_a_log, v_gdn_dt_bias, v_gdn_norm_g, v_mla_q_norm_g, v_mla_w_q_up, v_mla_kv_norm_g, v_mla_w_kv_up, v_w_out, v_ln1_g, v_ln1_b, v_ffn_w_up, v_ffn_conv_w, v_ffn_conv_b, v_ffn_w_down, v_ple_w_gate, v_ple_b_gate, v_ple_w_proj, v_ln2_g, v_ln2_b):
    loc = dict(locals())
    wts = {n: loc[n] for n in WEIGHTS}
    ms = {n: loc["m_" + n] for n in WEIGHTS}
    vs = {n: loc["v_" + n] for n in WEIGHTS}
    Bl, S, D = x.shape
    T = Bl * S

    wire = lambda n: wts[n][0] if n in CONV else wts[n][0].astype(BF)
    gather = lambda names, mode: ([wire(n) for n in names], [mode] * len(names),
                                  lambda res: {n: _gathered_to_full(n, g) for n, g in zip(names, res)})
    blocks = lambda g, names: [_full_to_blocks(n, g[n], wts[n].shape[1:]) for n in names]
    pack_early = lambda g: (blocks(g, GRADS_EARLY), ["a2a"] * len(GRADS_EARLY))
    pack_late = lambda g, gs: (blocks(g, GRADS_LATE) + [_pack_small(gs)], ["a2a"] * len(GRADS_LATE) + ["ag"])
    sp = {n: wts[n].reshape(1, -1) for n in SMALL_ORDER}

    loss, dx, gW, gs, early_recv, late_recv = _local_step(
        x.reshape(T, D), p.reshape(T, -1), loss_target.reshape(T, D), {}, sp, Bl, S,
        early_weights=gather(EARLY, "ag2"), late_weights=gather(LATE, "ag"), early_grads=pack_early,
        late_grads=pack_late)

    res = {}
    for n, parts in list(zip(GRADS_EARLY, early_recv)) + list(zip(GRADS_LATE, late_recv[:-1])):
        res[n] = tuple(t[None] for t in _adamw(parts, wts[n][0], ms[n][0], vs[n][0], "adamw_" + n))
    res.update(_adamw_small(late_recv[-1], wts, ms, vs))

    loss = lax.psum(loss, ("x", "y", "c"))
    outs = [loss, dx.reshape(Bl, S, D)]
    for k in range(4):
        outs += [res[n][k] for n in WEIGHTS]
    return tuple(outs)
```

```python
import math

import jax
import jax.numpy as jnp
from jax import lax
from jax.experimental import pallas as pl
from jax.experimental.pallas import tpu as pltpu

F32 = jnp.float32
BF = jnp.bfloat16
_MXU = jnp.bfloat16
_INTERPRET = None
_VMEM_LIMIT = 56 * 1024 * 1024
HI = lax.Precision.HIGHEST

N_DEV = 8
CHUNK = 64
GDN_HEADS = 4
GDN_DK = 128
MLA_HEADS = 4
MLA_NOPE = 128
MLA_ROPE = 64
ROPE_THETA = 10000.0
ALPHA = 2.0 ** 0.25
NORM_EPS = 1e-6
ATTN_SCALE = (MLA_NOPE + MLA_ROPE) ** -0.5
DCAT_QKV, DCAT_Z, DCAT_W = 1024, 2560, 3072
ADAM_LR, ADAM_B1, ADAM_B2, ADAM_EPS, ADAM_WD, ADAM_STEP = 0.001, 0.9, 0.999, 1e-08, 0.01, 10
MESH = pl.DeviceIdType.MESH
ANY = pl.BlockSpec(memory_space=pl.ANY)
_NN = (((1,), (0,)), ((), ()))
_NT = (((1,), (1,)), ((), ()))
_TN = (((0,), (0,)), ((), ()))


def _sds(shape, dtype=F32):
    return jax.ShapeDtypeStruct(tuple(shape), dtype)


def _pick(n, cands):
    for c in cands:
        if n % c == 0:
            return c
    return n


def _xchg_out_shapes(arrs, modes):
    return [_sds(a.shape if m == "a2a" else (N_DEV,) + a.shape, a.dtype) for a, m in zip(arrs, modes)]


def _xchg_scratch(n):
    return [pltpu.SemaphoreType.DMA((n, N_DEV - 1)), pltpu.SemaphoreType.DMA((n, N_DEV - 1)),
            pltpu.SemaphoreType.DMA((n,))]


def _xchg_plan(ins, outs, send, recv, loc, modes):
    x, y, c = lax.axis_index("x"), lax.axis_index("y"), lax.axis_index("c")
    me = 4 * x + 2 * y + c
    starts, relays, waits = [], [], []
    for ai, mode in enumerate(modes):
        src_all, out = ins[ai], outs[ai]

        def remote(k, src, dst, dev):
            return pltpu.make_async_remote_copy(src_ref=src, dst_ref=dst, send_sem=send.at[ai, k], recv_sem=recv.at[ai, k],
                                                device_id=dev, device_id_type=MESH)

        own = pltpu.make_async_copy(src_all.at[me] if mode == "a2a" else src_all, out.at[me], loc.at[ai])
        starts.append(own)
        waits.append(own.wait)
        if mode == "ag2":
            sib = (x, y, 1 - c)
            chips = [(1 - x, y), (x, 1 - y), (1 - x, 1 - y)]
            first = [remote(0, src_all, out.at[me], sib)]
            first += [remote(1 + k, src_all, out.at[me], (px, py, c)) for k, (px, py) in enumerate(chips)]
            starts += first
            waits += [cp.wait_send for cp in first]
            waits.append(remote(0, src_all, out.at[4 * x + 2 * y + 1 - c], sib).wait_recv)
            for k, (px, py) in enumerate(chips):
                same, other = out.at[4 * px + 2 * py + c], out.at[4 * px + 2 * py + 1 - c]
                relay = remote(4 + k, same, same, sib)
                relays.append((remote(1 + k, src_all, same, (px, py, c)), relay))
                waits += [relay.wait_send, remote(4 + k, src_all, other, sib).wait_recv]
            continue
        for r in range(1, N_DEV):
            px = 1 - x if (r >> 2) & 1 else x
            py = 1 - y if (r >> 1) & 1 else y
            pc = 1 - c if r & 1 else c
            cp = remote(r - 1, src_all.at[4 * px + 2 * py + pc] if mode == "a2a" else src_all, out.at[me], (px, py, pc))
            starts.append(cp)
            waits.append(cp.wait)
    return starts, relays, waits


def _pc(body, *, name, out_shape, grid=None, in_specs=None, out_specs=None, scratch=(), sem=None, aliases=None,
        xchg=None):
    kw = {}
    if _INTERPRET is not None:
        kw["interpret"] = _INTERPRET
    single = not isinstance(out_shape, (tuple, list))
    out_shape = [out_shape] if single else list(out_shape)
    if out_specs is not None:
        out_specs = [out_specs] if single else list(out_specs)
    scratch = list(scratch)
    if xchg is not None:
        xarrs, xmodes = xchg
        n_in, n_out, n_scr, nx = len(in_specs), len(out_shape), len(scratch), len(xarrs)
        single = False
        inner = body

        def body(*refs):
            ins, xins = refs[:n_in], refs[n_in:n_in + nx]
            outs = refs[n_in + nx:n_in + nx + n_out]
            xouts = refs[n_in + nx + n_out:n_in + 2 * nx + n_out]
            scr = refs[n_in + 2 * nx + n_out:n_in + 2 * nx + n_out + n_scr]
            send, recv, loc = refs[n_in + 2 * nx + n_out + n_scr:]
            first = last = None
            for d, g in enumerate(grid):
                f, l = pl.program_id(d) == 0, pl.program_id(d) == g - 1
                first, last = (f, l) if first is None else (first & f, last & l)

            @pl.when(first)
            def _():
                for cp in _xchg_plan(xins, xouts, send, recv, loc, xmodes)[0]:
                    cp.start()

            inner(*ins, *outs, *scr)

            @pl.when(last)
            def _():
                _, relays, waits = _xchg_plan(xins, xouts, send, recv, loc, xmodes)
                for arrival, relay in relays:
                    arrival.wait_recv()
                    relay.start()
                for wait in waits:
                    wait()

        in_specs = list(in_specs) + [ANY] * nx
        out_specs = out_specs + [ANY] * nx
        out_shape = out_shape + _xchg_out_shapes(xarrs, xmodes)
        scratch = scratch + _xchg_scratch(nx)
        sem = ("arbitrary",) * len(grid)
    if grid is not None:
        kw["grid"] = grid
    if in_specs is not None:
        kw["in_specs"] = in_specs
    if out_specs is not None:
        kw["out_specs"] = out_specs[0] if single else tuple(out_specs)
    cp = dict(vmem_limit_bytes=_VMEM_LIMIT)
    if sem is not None:
        cp["dimension_semantics"] = sem
    call = pl.pallas_call(body, name=name, out_shape=out_shape[0] if single else tuple(out_shape),
                          scratch_shapes=scratch, input_output_aliases=aliases or {},
                          compiler_params=pltpu.CompilerParams(**cp), **kw)
    if xchg is None:
        return call
    return lambda *ins: call(*ins, *xchg[0])


_FULL_K_MAX = 3072


def _mm(a, b, mode, *, name, add=None, add_scale=1.0, out_dtype=F32, xchg=None):
    (M, K) = a.shape
    (K2, N) = b.shape if mode == "nn" else b.shape[::-1]
    assert K == K2, (a.shape, b.shape, mode)
    tm = _pick(M, (1024, 1408, 512, 384, 256, 128))
    tn = _pick(N, (1024, 2816, 1408, 768, 512, 384, 256, 128))
    tk = K if K <= _FULL_K_MAX else _pick(K, (2048, 1408, 1024, 512) if tn <= 1408 else (1024, 512))
    nk = K // tk
    dims = _NN if mode == "nn" else _NT
    has_add = add is not None

    def finish(r, add_ref, o_ref):
        if has_add:
            r = r + add_scale * add_ref[...].astype(F32)
        o_ref[...] = r.astype(out_dtype)

    if nk == 1:
        def body(a_ref, b_ref, *rest):
            r = lax.dot_general(a_ref[...].astype(_MXU), b_ref[...].astype(_MXU), dims, preferred_element_type=F32)
            finish(r, rest[0] if has_add else None, rest[-1])

        a_spec = pl.BlockSpec((tm, K), lambda i, j: (i, 0))
        b_spec = (pl.BlockSpec((K, tn), lambda i, j: (0, j)) if mode == "nn"
                  else pl.BlockSpec((tn, K), lambda i, j: (j, 0)))
        o_spec = pl.BlockSpec((tm, tn), lambda i, j: (i, j))
        grid, sem, scratch = (M // tm, N // tn), ("parallel", "parallel"), []
    else:
        def body(a_ref, b_ref, *rest):
            o_ref, acc = rest[-2], rest[-1]
            k = pl.program_id(2)

            @pl.when(k == 0)
            def _():
                acc[...] = jnp.zeros_like(acc)

            acc[...] += lax.dot_general(a_ref[...].astype(_MXU), b_ref[...].astype(_MXU), dims,
                                        preferred_element_type=F32)

            @pl.when(k == nk - 1)
            def _():
                finish(acc[...], rest[0] if has_add else None, o_ref)

        a_spec = pl.BlockSpec((tm, tk), lambda i, j, k: (i, k))
        b_spec = (pl.BlockSpec((tk, tn), lambda i, j, k: (k, j)) if mode == "nn"
                  else pl.BlockSpec((tn, tk), lambda i, j, k: (j, k)))
        o_spec = pl.BlockSpec((tm, tn), lambda i, j, k: (i, j))
        grid, sem, scratch = (M // tm, N // tn, nk), ("parallel", "parallel", "arbitrary"), [pltpu.VMEM((tm, tn), F32)]

    ins = [a, b] + ([add] if has_add else [])
    specs = [a_spec, b_spec] + ([o_spec] if has_add else [])
    return _pc(body, name=name, out_shape=_sds((M, N), out_dtype), grid=grid, in_specs=specs, out_specs=o_spec,
               scratch=scratch, sem=sem, xchg=xchg)(*ins)


def _dot(a, b, dims):
    return lax.dot_general(a.astype(_MXU), b.astype(_MXU), dims, preferred_element_type=F32)


def _split(a):
    hi = a.astype(BF)
    lo = (a - hi.astype(F32)).astype(BF)
    return hi, lo


def _dot3(a, b, dims):
    ah, al = _split(a)
    bh, bl = _split(b)
    d = lambda u, v: lax.dot_general(u, v, dims, preferred_element_type=F32)
    return d(ah, bh) + (d(ah, bl) + d(al, bh))


def _softplus(x):
    return jnp.maximum(x, 0.0) + jnp.log1p(jnp.exp(-jnp.abs(x)))


def _silu(x):
    return x * jax.nn.sigmoid(x)


def _rmsnorm(x, g):
    return x * lax.rsqrt(jnp.mean(x * x, axis=-1, keepdims=True) + NORM_EPS) * g


def _layernorm(x, g, b):
    mu = jnp.mean(x, axis=-1, keepdims=True)
    xc = x - mu
    var = jnp.mean(xc * xc, axis=-1, keepdims=True)
    return xc * lax.rsqrt(var + NORM_EPS) * g + b


def _shift_dn(x, s):
    if s == 0:
        return x
    t = lax.broadcasted_iota(jnp.int32, x.shape, 0)
    return jnp.where(t >= s, pltpu.roll(x, s, 0), 0.0)


def _shift_up(x, s):
    if s == 0:
        return x
    n = x.shape[0]
    t = lax.broadcasted_iota(jnp.int32, x.shape, 0)
    return jnp.where(t < n - s, pltpu.roll(x, n - s, 0), 0.0)


def _taps(x, kk):
    return [_shift_dn(x, kk - 1 - j) for j in range(kk)]


def _conv(x, w, taps=None):
    kk = w.shape[0]
    taps = _taps(x, kk) if taps is None else taps
    y = w[kk - 1:kk, :] * taps[kk - 1]
    for j in range(kk - 1):
        y = y + w[j:j + 1, :] * taps[j]
    return y


def _conv_bwd_x(dy, w):
    kk = w.shape[0]
    dx = w[kk - 1:kk, :] * dy
    for j in range(kk - 1):
        dx = dx + w[j:j + 1, :] * _shift_up(dy, kk - 1 - j)
    return dx


def _conv_bwd_w(taps, dy, dw_ref, first):
    kk = dw_ref.shape[0]
    for j in range(kk):
        r = jnp.sum(dy * taps[j], axis=0, keepdims=True)
        prev = jnp.where(first, 0.0, dw_ref[j:j + 1, :])
        dw_ref[j:j + 1, :] = prev + r


def _gdn_post_conv(c, j):
    h = _silu(c)
    hn = h * lax.rsqrt(jnp.sum(h * h, axis=-1, keepdims=True) + NORM_EPS)
    return jnp.where(j < GDN_HEADS, hn * (GDN_DK ** -0.5), jnp.where(j < 2 * GDN_HEADS, hn, h))


def _gdn_proj_pre_fwd(xb, wqkv, conv_w, Bl, S):
    T, D = xb.shape
    C = wqkv.shape[1]
    nj = C // 256

    def body(x_ref, p_ref, w_ref, qkv_ref, o_ref):
        j = pl.program_id(1)
        qkv = _dot(x_ref[...], p_ref[...], _NN)
        qkv_ref[...] = qkv
        c = _conv(qkv, w_ref[...])
        for half in range(2):
            o_ref[:, 128 * half:128 * (half + 1)] = _gdn_post_conv(c[:, 128 * half:128 * (half + 1)], 2 * j)

    blk = pl.BlockSpec((S, 256), lambda b, j: (b, j))
    return _pc(body, name="gdn_proj_pre_fwd", out_shape=(_sds((T, C)), _sds((T, C))), grid=(Bl, nj),
               in_specs=[pl.BlockSpec((S, D), lambda b, j: (b, 0)), pl.BlockSpec((D, 256), lambda b, j: (0, j)),
                         pl.BlockSpec((conv_w.shape[0], 256), lambda b, j: (0, j))],
               out_specs=(blk, blk), sem=("parallel", "parallel"))(xb, wqkv, conv_w)


def _gdn_pre_bwd(qkv, conv_w, dout, dcat, Bl, S):
    T, C = qkv.shape
    nj = C // 128
    kk = conv_w.shape[0]

    def body(x_ref, w_ref, d_ref, alias_ref, dx_ref, dw_ref):
        j, b = pl.program_id(0), pl.program_id(1)
        w = w_ref[...]
        taps = _taps(x_ref[...], kk)
        c = _conv(None, w, taps)
        _, vjp = jax.vjp(lambda u: _gdn_post_conv(u, j), c)
        (dc,) = vjp(d_ref[...])
        dx_ref[...] = _conv_bwd_x(dc, w).astype(dx_ref.dtype)
        _conv_bwd_w(taps, dc, dw_ref, b == 0)

    blk = pl.BlockSpec((S, 128), lambda j, b: (b, j))
    wblk = pl.BlockSpec((kk, 128), lambda j, b: (0, j))
    return _pc(body, name="gdn_pre_bwd", out_shape=(_sds(dcat.shape, dcat.dtype), _sds((kk, C))), grid=(nj, Bl),
               in_specs=[blk, wblk, blk, ANY],
               out_specs=(pl.BlockSpec((S, 128), lambda j, b: (b, DCAT_QKV // 128 + j)), wblk),
               aliases={3: 0}, sem=("parallel", "arbitrary"))(qkv, conv_w, dout, dcat)


_H = GDN_HEADS
_HC = _H * CHUNK


def _st(x):
    return jnp.concatenate([x[:, h * 128:(h + 1) * 128] for h in range(_H)], axis=0)


def _unst(x):
    return jnp.concatenate([x[h * CHUNK:(h + 1) * CHUNK] for h in range(_H)], axis=1)


def _stc(t):
    return jnp.concatenate([t[:, h:h + 1] for h in range(_H)], axis=0)


def _untile(col):
    lane = lax.broadcasted_iota(jnp.int32, (CHUNK, 128), 1)
    out = jnp.zeros((CHUNK, 128), F32)
    for h in range(_H):
        out = out + jnp.where(lane == h, col[h * CHUNK:(h + 1) * CHUNK], 0.0)
    return out


def _rowform(col):
    return jnp.broadcast_to(col, (_HC, 128)).T[0:1, :]


def _tri(n, upper=False):
    i = lax.broadcasted_iota(jnp.int32, (n, n), 0)
    j = lax.broadcasted_iota(jnp.int32, (n, n), 1)
    return jnp.where((j >= i) if upper else (j <= i), 1.0, 0.0).astype(F32)


def _gdn_chunk_common(q, k, v, gc, beta):
    Q, K, V = _st(q), _st(k), _st(v)
    B, GC = _stc(beta), _stc(gc)
    GL = jnp.concatenate([jnp.broadcast_to(gc[CHUNK - 1:CHUNK, h:h + 1], (CHUNK, 1)) for h in range(_H)], axis=0)
    ii = lax.broadcasted_iota(jnp.int32, (_HC, _HC), 0)
    jj = lax.broadcasted_iota(jnp.int32, (_HC, _HC), 1)
    same = (ii >> 6) == (jj >> 6)
    incl = same & (ii >= jj)
    strict = same & (ii > jj)
    diff = GC - _rowform(GC)
    D = jnp.where(incl, jnp.exp(jnp.where(incl, diff, 0.0)), 0.0)
    KB = K * B
    A = jnp.where(strict, _dot(KB, K, _NT) * D, 0.0)
    P = jnp.where(incl, _dot(Q, K, _NT) * D, 0.0)
    EG = jnp.exp(GC)
    ED = jnp.exp(GL - GC)
    return dict(Q=Q, K=K, V=V, B=B, GC=GC, GL=GL, incl=incl, strict=strict, D=D, KB=KB, A=A, P=P, EG=EG, ED=ED,
                QG=Q * EG, KD=K * ED)


def _neumann_inv(A):
    n = A.shape[0]
    i = lax.broadcasted_iota(jnp.int32, (n, n), 0)
    j = lax.broadcasted_iota(jnp.int32, (n, n), 1)
    N = -A
    acc = jnp.where(i == j, 1.0, 0.0) + N
    Pw = N
    for _ in range(5):
        Pw = _dot3(Pw, Pw, _NN)
        acc = acc + _dot3(acc, Pw, _NN)
    return acc


def _gates(a, alog, dtb):
    return -jnp.exp(alog) * _softplus(a + dtb)


def _hs(x, h, n=CHUNK):
    return x[h * n:(h + 1) * n]


def _per_step(Bl):
    return 4 if Bl % 4 == 0 else 2 if Bl % 2 == 0 else 1


def _gdn_fwd(qkvn, sm, alog, dtb, Bl, S, xchg=None):
    T = Bl * S
    nC = S // CHUNK
    E = _per_step(Bl)

    def body(*refs):
        S_ref = refs[-1]

        @pl.when(pl.program_id(1) == 0)
        def _():
            S_ref[...] = jnp.zeros_like(S_ref)

        for e in range(E):
            one(*[r if i in (5, 6) else r.at[e] for i, r in enumerate(refs)])

    def one(q_ref, k_ref, v_ref, a_ref, b_ref, alog_ref, dtb_ref,
            o_ref, tinv_ref, sst_ref, w_ref, u_ref, gc_ref, beta_ref, S_ref):
        g = _gates(a_ref[...], alog_ref[...], dtb_ref[...])
        beta = jax.nn.sigmoid(b_ref[...])
        gc = jnp.dot(_tri(CHUNK), g, precision=HI, preferred_element_type=F32)
        cm = _gdn_chunk_common(q_ref[...], k_ref[...], v_ref[...], gc, beta)
        tinv = _neumann_inv(cm["A"])
        WU = _dot3(tinv, jnp.concatenate([cm["KB"] * cm["EG"], cm["V"] * cm["B"]], axis=1), _NN)
        W, U = WU[:, :128], WU[:, 128:]
        s_old = [S_ref[h * 128:(h + 1) * 128, :] for h in range(_H)]
        vn = [_hs(U, h) - _dot(_hs(W, h), s_old[h], _NN) for h in range(_H)]
        o_intra = _dot(cm["P"], jnp.concatenate(vn, axis=0), _NN)
        outs = []
        for h in range(_H):
            outs.append(_dot(_hs(cm["QG"], h), s_old[h], _NN) + _hs(o_intra, h))
            gl = jnp.exp(gc[CHUNK - 1:CHUNK, h:h + 1])
            S_ref[h * 128:(h + 1) * 128, :] = gl * s_old[h] + _dot(_hs(cm["KD"], h), vn[h], _TN)
            sst_ref[h * 128:(h + 1) * 128, :] = s_old[h]
        o_ref[...] = jnp.concatenate(outs, axis=1)
        tinv_ref[...] = tinv
        w_ref[...] = _unst(W)
        u_ref[...] = _unst(U)
        gc_ref[...] = gc
        beta_ref[...] = beta

    blk = lambda r, w, c: pl.BlockSpec((E, r, w), lambda b, n: (b, n, c))
    par = pl.BlockSpec((1, 128), lambda b, n: (0, 0))
    out_shape = (_sds((Bl, S, 512)), _sds((Bl, nC * _HC, _HC)), _sds((Bl, nC * 512, 128)),
                 _sds((Bl, S, 512)), _sds((Bl, S, 512)), _sds((Bl, S, 128)), _sds((Bl, S, 128)))
    out_specs = (blk(CHUNK, 512, 0), blk(_HC, _HC, 0), blk(512, 128, 0), blk(CHUNK, 512, 0), blk(CHUNK, 512, 0),
                 blk(CHUNK, 128, 0), blk(CHUNK, 128, 0))
    q3, s3 = qkvn.reshape(Bl, S, -1), sm.reshape(Bl, S, -1)
    res = _pc(body, name="gdn_chunk_fwd", out_shape=out_shape, grid=(Bl // E, nC),
              in_specs=[blk(CHUNK, 512, 0), blk(CHUNK, 512, 1), blk(CHUNK, 512, 2), blk(CHUNK, 128, 6),
                        blk(CHUNK, 128, 7), par, par],
              out_specs=out_specs, scratch=[pltpu.VMEM((E, 512, 128), F32)], sem=("parallel", "arbitrary"),
              xchg=xchg)(q3, q3, q3, s3, s3, alog, dtb)
    return (res[0].reshape(T, 512),) + tuple(res[1:])


def _gdn_bwd(qkvn, sm, alog, dtb, gc_s, beta_s, tinv_s, sst_s, w_s, u_s, do, dcat, Bl, S, xchg=None):
    T = Bl * S
    nC = S // CHUNK
    E = _per_step(Bl)

    def body(*refs):
        acc_ref, dS_ref = refs[-2], refs[-1]
        first_chunk = pl.program_id(1) == 0

        @pl.when(first_chunk)
        def _():
            dS_ref[...] = jnp.zeros_like(dS_ref)

        @pl.when(first_chunk & (pl.program_id(0) == 0))
        def _():
            acc_ref[...] = jnp.zeros_like(acc_ref)

        for e in range(E):
            one(*[r if i in (4, 5, 13, 16) else r.at[e] for i, r in enumerate(refs)])

    def one(q_ref, k_ref, v_ref, a_ref, alog_ref, dtb_ref, gc_ref, beta_ref, tinv_ref, sst_ref, w_ref, u_ref,
            do_ref, alias_ref, dqkv_ref, dsm_ref, acc_ref, dS_ref):
        gc, beta = gc_ref[...], beta_ref[...]
        cm = _gdn_chunk_common(q_ref[...], k_ref[...], v_ref[...], gc, beta)
        Q, K, V, B, D, KB, A, P = (cm[n] for n in ("Q", "K", "V", "B", "D", "KB", "A", "P"))
        EG, ED, QG, KD = cm["EG"], cm["ED"], cm["QG"], cm["KD"]
        tinv = tinv_ref[...]
        W, U, DO = _st(w_ref[...]), _st(u_ref[...]), _st(do_ref[...])
        s_old = [sst_ref[h * 128:(h + 1) * 128, :] for h in range(_H)]
        ds_new = [dS_ref[h * 128:(h + 1) * 128, :] for h in range(_H)]
        VN = jnp.concatenate([_hs(U, h) - _dot(_hs(W, h), s_old[h], _NN) for h in range(_H)], axis=0)
        dP = jnp.where(cm["incl"], _dot(DO, VN, _NT), 0.0)
        dVN0 = _dot(P, DO, _TN)
        dVN, dQG, dKD, dW, TL = [], [], [], [], []
        for h in range(_H):
            gl = jnp.exp(gc[CHUNK - 1:CHUNK, h:h + 1])
            dvn = _hs(dVN0, h) + _dot(_hs(KD, h), ds_new[h], _NN)
            dkd = _dot(_hs(VN, h), ds_new[h], _NT)
            dVN.append(dvn)
            dQG.append(_dot(_hs(DO, h), s_old[h], _NT))
            dKD.append(dkd)
            dW.append(-_dot(dvn, s_old[h], _NT))
            dgl = jnp.sum(jnp.sum(ds_new[h] * s_old[h], axis=1, keepdims=True), axis=0, keepdims=True)
            tl = jnp.sum(jnp.sum(dkd * _hs(KD, h), axis=1, keepdims=True), axis=0, keepdims=True) + dgl * gl
            TL.append(jnp.broadcast_to(tl, (CHUNK, 1)))
            dS_ref[h * 128:(h + 1) * 128, :] = (_dot(_hs(QG, h), _hs(DO, h), _TN) + gl * ds_new[h]
                                                - _dot(_hs(W, h), dvn, _TN))
        dVN, dQG, dKD, dW, TL = (jnp.concatenate(z, axis=0) for z in (dVN, dQG, dKD, dW, TL))
        dB = _dot3(tinv, jnp.concatenate([dVN, dW], axis=1), _TN)
        dVB, dKBE = dB[:, :128], dB[:, 128:]
        dA = jnp.where(cm["strict"], -_dot3(dB, jnp.concatenate([U, W], axis=1), _NT), 0.0)
        dG = dA * D
        dQK = dP * D
        dKB = _dot(dG, K, _NN) + dKBE * EG
        dK = (_dot(jnp.concatenate([dG, dQK], axis=0), jnp.concatenate([KB, Q], axis=0), _TN)
              + dKB * B + dKD * ED)
        dQ = _dot(dQK, K, _NN) + dQG * EG
        Mx = dA * A + dP * P
        rs = lambda z: jnp.sum(z, axis=1, keepdims=True)
        ri = lax.broadcasted_iota(jnp.int32, (_HC, 1), 0)
        dGC = (rs(Mx) - rs(Mx.T) + rs(dKBE * KB * EG) + rs(dQG * QG) - rs(dKD * KD)
               + jnp.where((ri & (CHUNK - 1)) == CHUNK - 1, TL, 0.0))
        dBeta = rs(dVB * V) + rs(dKB * K)
        dqkv_ref[:, 0:512] = _unst(dQ)
        dqkv_ref[:, 512:1024] = _unst(dK)
        dqkv_ref[:, 1024:1536] = _unst(dVB * B)
        dg = jnp.dot(_tri(CHUNK, upper=True), _untile(dGC), precision=HI, preferred_element_type=F32)
        a, alog_v, dtb_v = a_ref[...], alog_ref[...], dtb_ref[...]
        g = _gates(a, alog_v, dtb_v)
        lane = lax.broadcasted_iota(jnp.int32, (CHUNK, 128), 1)
        valid = lane < _H
        da = jnp.where(valid, dg * (-jnp.exp(alog_v)) * jax.nn.sigmoid(a + dtb_v), 0.0)
        dsm_ref[:, 0:128] = da.astype(dsm_ref.dtype)
        dsm_ref[:, 128:256] = jnp.where(valid, _untile(dBeta) * beta * (1.0 - beta), 0.0).astype(dsm_ref.dtype)
        acc_ref[0:1, :] += jnp.sum(jnp.where(valid, dg * g, 0.0), axis=0, keepdims=True)
        acc_ref[1:2, :] += jnp.sum(da, axis=0, keepdims=True)

    blk = lambda r, w, c: pl.BlockSpec((E, r, w), lambda b, n: (b, nC - 1 - n, c))
    par = pl.BlockSpec((1, 128), lambda b, n: (0, 0))
    in_specs = [blk(CHUNK, 512, 0), blk(CHUNK, 512, 1), blk(CHUNK, 512, 2), blk(CHUNK, 128, 6), par, par,
                blk(CHUNK, 128, 0), blk(CHUNK, 128, 0), blk(_HC, _HC, 0), blk(512, 128, 0),
                blk(CHUNK, 512, 0), blk(CHUNK, 512, 0), blk(CHUNK, 512, 0), ANY]
    out_shape = (_sds((Bl, S, 1536)), _sds((Bl, S, DCAT_W), BF), _sds((8, 128)))
    out_specs = (blk(CHUNK, 1536, 0), blk(CHUNK, 256, 3), pl.BlockSpec((8, 128), lambda b, n: (0, 0)))
    q3, s3 = qkvn.reshape(Bl, S, -1), sm.reshape(Bl, S, -1)
    res = _pc(body, name="gdn_chunk_bwd", out_shape=out_shape, grid=(Bl // E, nC), in_specs=in_specs,
              out_specs=out_specs, scratch=[pltpu.VMEM((E, 512, 128), F32)], sem=("arbitrary", "arbitrary"),
              aliases={13: 1}, xchg=xchg)(q3, q3, q3, s3, alog, dtb, gc_s, beta_s, tinv_s, sst_s, w_s, u_s,
                                          do.reshape(Bl, S, -1), dcat.reshape(Bl, S, DCAT_W))
    return (res[0].reshape(T, 1536), res[1].reshape(T, DCAT_W)) + tuple(res[2:])


def _gdn_post(o, z, g, T):
    tt = _pick(T, (2048, 1024, 512, 256))

    def body(o_ref, z_ref, g_ref, out_ref, outt_ref):
        y = (_rmsnorm(o_ref[...], g_ref[...]) * _silu(z_ref[...])).astype(out_ref.dtype)
        out_ref[...] = y
        outt_ref[...] = y.T

    blk = pl.BlockSpec((tt, 128), lambda i, h: (i, h))
    return _pc(body, name="gdn_post_fwd", out_shape=(_sds((T, 1024), BF), _sds((1024, T), BF)), grid=(T // tt, _H),
               in_specs=[blk, blk, pl.BlockSpec((1, 128), lambda i, h: (0, 0))],
               out_specs=(blk, pl.BlockSpec((128, tt), lambda i, h: (h, i))),
               sem=("parallel", "parallel"))(o, z, g)


def _gdn_post_bwd(o, z, g, dmixin, T):
    tt = _pick(T, (2048, 1024, 512, 256))

    def body(o_ref, z_ref, g_ref, d_ref, do_ref, dz_ref, dg_ref):
        @pl.when((pl.program_id(0) == 0) & (pl.program_id(1) == 0))
        def _():
            dg_ref[...] = jnp.zeros_like(dg_ref)

        _, vjp = jax.vjp(lambda a, b, c: _rmsnorm(a, c) * _silu(b), o_ref[...], z_ref[...], g_ref[...])
        do, dz, dg = vjp(d_ref[...])
        do_ref[...] = do
        dz_ref[...] = dz.astype(dz_ref.dtype)
        dg_ref[0:1, :] += dg

    blk = pl.BlockSpec((tt, 128), lambda i, h: (i, h))
    return _pc(body, name="gdn_post_bwd", out_shape=(_sds((T, 512)), _sds((T, DCAT_W), BF), _sds((8, 128))),
               grid=(T // tt, _H), in_specs=[blk, blk, pl.BlockSpec((1, 128), lambda i, h: (0, 0)), blk],
               out_specs=(blk, pl.BlockSpec((tt, 128), lambda i, h: (i, DCAT_Z // 128 + h)),
                          pl.BlockSpec((8, 128), lambda i, h: (0, 0))),
               sem=("arbitrary", "arbitrary"))(o, z, g, dmixin)


def _rms_fwd(sm, g, col_blk, name):
    T = sm.shape[0]
    d = g.shape[1]
    tt = _pick(T, (1024, 512, 256))

    def body(x_ref, g_ref, o_ref, ot_ref):
        y = _rmsnorm(x_ref[...], g_ref[...]).astype(o_ref.dtype)
        o_ref[...] = y
        ot_ref[...] = y.T

    return _pc(body, name=name, out_shape=(_sds((T, d), BF), _sds((d, T), BF)), grid=(T // tt,),
               in_specs=[pl.BlockSpec((tt, d), lambda i: (i, col_blk)), pl.BlockSpec((1, d), lambda i: (0, 0))],
               out_specs=(pl.BlockSpec((tt, d), lambda i: (i, 0)), pl.BlockSpec((d, tt), lambda i: (0, i))),
               sem=("parallel",))(sm, g)


def _rms_bwd(sm, g, dy, dsm, col_blk, name):
    T = sm.shape[0]
    d = g.shape[1]
    tt = _pick(T, (1024, 512, 256))

    def body(x_ref, g_ref, d_ref, alias_ref, dx_ref, dg_ref):
        @pl.when(pl.program_id(0) == 0)
        def _():
            dg_ref[...] = jnp.zeros_like(dg_ref)

        _, vjp = jax.vjp(_rmsnorm, x_ref[...], g_ref[...])
        dx, dg = vjp(d_ref[...])
        dx_ref[...] = dx.astype(dx_ref.dtype)
        dg_ref[0:1, :] += dg

    grp = pl.BlockSpec((tt, d), lambda i: (i, col_blk))
    return _pc(body, name=name, out_shape=(_sds(dsm.shape, dsm.dtype), _sds((8, d))), grid=(T // tt,),
               in_specs=[grp, pl.BlockSpec((1, d), lambda i: (0, 0)), pl.BlockSpec((tt, d), lambda i: (i, 0)), ANY],
               out_specs=(grp, pl.BlockSpec((8, d), lambda i: (0, 0))), aliases={3: 0},
               sem=("arbitrary",))(sm, g, dy, dsm)


def _rope_tables(S):
    inv = ROPE_THETA ** (-jnp.arange(0, MLA_ROPE, 2, dtype=F32) / MLA_ROPE)
    ang = jnp.arange(S, dtype=F32)[:, None] * inv[None, :]
    cos, sin = jnp.cos(ang), jnp.sin(ang)
    z = jnp.zeros((S, 64), F32)
    return jnp.concatenate([cos, cos, z], axis=1), jnp.concatenate([-sin, sin, z], axis=1)


def _swap_halves(x):
    lane = lax.broadcasted_iota(jnp.int32, x.shape, 1)
    return jnp.where(lane < 32, pltpu.roll(x, 96, 1), jnp.where(lane < 64, pltpu.roll(x, 32, 1), 0.0))


def _mm_rows(a, b, *, name, tm, extras, out_shape, out_specs, epi, mode="nn", sem="parallel"):
    M, K = a.shape
    nx = len(extras)

    def body(a_ref, b_ref, *rest):
        r = lax.dot_general(a_ref[...].astype(_MXU), b_ref[...].astype(_MXU), _NN if mode == "nn" else _NT,
                            preferred_element_type=F32)
        epi(r, rest[:nx], rest[nx:])

    in_specs = [pl.BlockSpec((tm, K), lambda i: (i, 0)), pl.BlockSpec(b.shape, lambda i: (0, 0))]
    return _pc(body, name=name, out_shape=out_shape, grid=(M // tm,), in_specs=in_specs + [s for _, s in extras],
               out_specs=out_specs, sem=(sem,))(a, b, *[x for x, _ in extras])


def _qup_rope(cqn, wq, cos, sin, S):
    T = cqn.shape[0]
    tm = _pick(S, (1024, 512, 256))
    nps = S // tm

    def epi(r, xs, outs):
        c, s = xs[0][...], xs[1][...]
        for h in range(MLA_HEADS):
            lo = 256 * h
            rp = r[:, lo + 128:lo + 256]
            outs[0][:, lo:lo + 128] = (r[:, lo:lo + 128] * ATTN_SCALE).astype(BF)
            outs[0][:, lo + 128:lo + 256] = ((rp * c + _swap_halves(rp) * s) * ATTN_SCALE).astype(BF)

    tab = pl.BlockSpec((tm, 128), lambda i: (i % nps, 0))
    return _mm_rows(cqn, wq, name="mm_qup_rope", tm=tm, extras=[(cos, tab), (sin, tab)],
                    out_shape=_sds((T, 1024), BF), out_specs=pl.BlockSpec((tm, 1024), lambda i: (i, 0)), epi=epi)


def _kvup_rope(ckvn, wkv, sm, cos, sin, S):
    T = ckvn.shape[0]
    tm = _pick(S, (1024, 512, 256))
    nps = S // tm

    def epi(r, xs, outs):
        k, c, s = xs[0][...], xs[1][...], xs[2][...]
        kr = (k * c + _swap_halves(k) * s).astype(BF)
        kvb = r.astype(BF)
        outs[0][...] = kvb
        for h in range(MLA_HEADS):
            lo = 256 * h
            outs[1][:, lo:lo + 128] = kvb[:, lo:lo + 128]
            outs[1][:, lo + 128:lo + 256] = kr

    tab = pl.BlockSpec((tm, 128), lambda i: (i % nps, 0))
    row = pl.BlockSpec((tm, 1024), lambda i: (i, 0))
    return _mm_rows(ckvn, wkv, name="mm_kvup_rope", tm=tm,
                    extras=[(sm, pl.BlockSpec((tm, 128), lambda i: (i, 3))), (cos, tab), (sin, tab)],
                    out_shape=(_sds((T, 1024), BF), _sds((T, 1024), BF)), out_specs=(row, row), epi=epi)


def _rope_bwd(dqc, dcat, dkrr, cos, sin, S):
    T = dqc.shape[0]
    tt = _pick(S, (2048, 1024, 512, 256))
    nps = S // tt

    def body(alias_s, dq_ref, dk_ref, c_ref, s_ref, qo_ref, ko_ref):
        c, s = c_ref[...], s_ref[...]
        d = dq_ref[:, 128:256]
        qo_ref[:, 0:128] = dq_ref[:, 0:128].astype(qo_ref.dtype)
        qo_ref[:, 128:256] = (d * c + _swap_halves(d * s)).astype(qo_ref.dtype)

        @pl.when(pl.program_id(1) == 0)
        def _():
            k = dk_ref[...]
            ko_ref[...] = (k * c + _swap_halves(k * s)).astype(ko_ref.dtype)

    tab = pl.BlockSpec((tt, 128), lambda i, h: (i % nps, 0))
    head = pl.BlockSpec((tt, 256), lambda i, h: (i, h))
    return _pc(body, name="rope_bwd", out_shape=(_sds((T, 1024), BF), _sds(dcat.shape, dcat.dtype)),
               grid=(T // tt, MLA_HEADS),
               in_specs=[ANY, head, pl.BlockSpec((tt, 128), lambda i, h: (i, 0)), tab, tab],
               out_specs=(head, pl.BlockSpec((tt, 128), lambda i, h: (i, 3))),
               aliases={0: 1}, sem=("parallel", "arbitrary"))(dcat, dqc, dkrr, cos, sin)


def _attn_scores(q, k_ref, L, tq):
    def sc(lo, hi):
        return _dot(q, k_ref[lo:hi, :], _NT)

    sd = sc(L - tq, L)
    qc = lax.broadcasted_iota(jnp.int32, sd.shape, 0) >> 6
    kc = lax.broadcasted_iota(jnp.int32, sd.shape, 1) >> 6
    sd = jnp.where(kc <= qc, sd, -jnp.inf)
    return sd if L == tq else jnp.concatenate([sc(0, L - tq), sd], axis=1)


def _attn_fwd(mixin, mixin_t, qcat, kcat, kv, Bl, S):
    T = Bl * S
    tq = _pick(S, (512, 256, 128))
    nq = S // tq

    def body(alias_ref, alias_t, q_ref, k_ref, v_ref, o_ref, lse_ref, ot_ref):
        i = pl.program_id(2)
        q = q_ref[...]
        for ii in range(nq):
            @pl.when(i == ii)
            def _(L=(ii + 1) * tq):
                s = _attn_scores(q, k_ref, L, tq)
                m = jnp.max(s, axis=-1, keepdims=True)
                e = jnp.exp(s - m)
                l = jnp.sum(e, axis=-1, keepdims=True)
                o = (_dot(e, v_ref[0:L, :], _NN) / l).astype(o_ref.dtype)
                o_ref[...] = o
                ot_ref[...] = o.T
                lse_ref[...] = jnp.broadcast_to(m + jnp.log(l), lse_ref.shape)

    qrow = lambda b, h, i: b * nq + i
    in_specs = [ANY, ANY,
                pl.BlockSpec((tq, 256), lambda b, h, i: (qrow(b, h, i), h)),
                pl.BlockSpec((S, 256), lambda b, h, i: (b, h)),
                pl.BlockSpec((S, 128), lambda b, h, i: (b, 2 * h + 1))]
    return _pc(body, name="mla_attn_fwd",
               out_shape=(_sds(mixin.shape, mixin.dtype), _sds((T, 512)), _sds(mixin_t.shape, mixin_t.dtype)),
               grid=(Bl, MLA_HEADS, nq), in_specs=in_specs,
               out_specs=(pl.BlockSpec((tq, 128), lambda b, h, i: (qrow(b, h, i), 4 + h)),
                          pl.BlockSpec((tq, 128), lambda b, h, i: (qrow(b, h, i), h)),
                          pl.BlockSpec((128, tq), lambda b, h, i: (4 + h, qrow(b, h, i)))),
               aliases={0: 0, 1: 2}, sem=("parallel", "parallel", "parallel"))(mixin, mixin_t, qcat, kcat, kv)


def _attn_bwd(qcat, kcat, kv, lse, dmixin, Bl, S):
    T = Bl * S
    tq = _pick(S, (512, 256, 128))
    nq = S // tq

    def body(q_ref, k_ref, v_ref, do_ref, lse_ref, dq_ref, dkv_ref, dkr_ref, acck_ref, accv_ref):
        h, i = pl.program_id(1), pl.program_id(2)
        q = q_ref[...]
        do = do_ref[...].astype(BF)
        lse_col = lse_ref[:, 0:1]

        @pl.when(i == 0)
        def _():
            acck_ref[...] = jnp.zeros_like(acck_ref)
            accv_ref[...] = jnp.zeros_like(accv_ref)

        for ii in range(nq):
            @pl.when(i == ii)
            def _(L=(ii + 1) * tq):
                p = jnp.exp(_attn_scores(q, k_ref, L, tq) - lse_col)
                pb = p.astype(BF)
                v = v_ref[0:L, :]
                dp = _dot(do, v, _NT)
                ds = (p * (dp - jnp.sum(dp * p, axis=-1, keepdims=True))).astype(BF)
                dq_ref[...] = _dot(ds, k_ref[0:L, :], _NN) * ATTN_SCALE
                acck_ref[0:L, :] += _dot(ds, q, _TN)
                accv_ref[0:L, :] += _dot(pb, do, _TN)

        @pl.when(i == nq - 1)
        def _():
            dkv_ref[:, 0:128] = acck_ref[:, 0:128].astype(dkv_ref.dtype)
            dkv_ref[:, 128:256] = accv_ref[...].astype(dkv_ref.dtype)
            dkr_ref[...] = jnp.where(h == 0, 0.0, dkr_ref[...]) + acck_ref[:, 128:256]

    qrow = lambda b, h, i: b * nq + i
    in_specs = [pl.BlockSpec((tq, 256), lambda b, h, i: (qrow(b, h, i), h)),
                pl.BlockSpec((S, 256), lambda b, h, i: (b, h)),
                pl.BlockSpec((S, 128), lambda b, h, i: (b, 2 * h + 1)),
                pl.BlockSpec((tq, 128), lambda b, h, i: (qrow(b, h, i), 4 + h)),
                pl.BlockSpec((tq, 128), lambda b, h, i: (qrow(b, h, i), h))]
    out_shape = (_sds((T, 1024)), _sds((T, 1024), BF), _sds((T, 128)))
    out_specs = (pl.BlockSpec((tq, 256), lambda b, h, i: (qrow(b, h, i), h)),
                 pl.BlockSpec((S, 256), lambda b, h, i: (b, h)),
                 pl.BlockSpec((S, 128), lambda b, h, i: (b, 0)))
    return _pc(body, name="mla_attn_bwd", out_shape=out_shape, grid=(Bl, MLA_HEADS, nq), in_specs=in_specs,
               out_specs=out_specs, scratch=[pltpu.VMEM((S, 256), F32), pltpu.VMEM((S, 128), F32)],
               sem=("arbitrary", "arbitrary", "arbitrary"))(qcat, kcat, kv, dmixin, lse)


def _out_ln1(mixin, w_out, x, g, b):
    T, D = x.shape
    tm = _pick(T, (512, 256))

    def epi(r, xs, outs):
        h = _layernorm(ALPHA * xs[0][...] + r, xs[1][...], xs[2][...])
        outs[0][...] = r
        outs[1][...] = h
        hb = h.astype(BF)
        outs[2][...] = hb
        outs[3][...] = hb.T

    row = pl.BlockSpec((tm, D), lambda i: (i, 0))
    par = pl.BlockSpec((1, D), lambda i: (0, 0))
    return _mm_rows(mixin, w_out, name="mm_out_ln1", tm=tm, extras=[(x, row), (g, par), (b, par)],
                    out_shape=(_sds((T, D)), _sds((T, D)), _sds((T, D), BF), _sds((D, T), BF)),
                    out_specs=(row, row, row, pl.BlockSpec((D, tm), lambda i: (0, i))), epi=epi)


def _dh1_ln1_bwd(du_u, wup_u, du_g, wup_g, dgpre, w_gate, dffn, x, mix, g, b):
    T, D = x.shape
    tm = _pick(T, (512, 256))

    def epi(r, xs, outs):
        dug_ref, wg_ref, dgp_ref, wgate_ref, dffn_ref, x_ref, mix_ref, g_ref, b_ref = xs
        acc_ref = outs[2]

        @pl.when(pl.program_id(0) == 0)
        def _():
            acc_ref[...] = jnp.zeros_like(acc_ref)

        dh = (r + _dot(dug_ref[...], wg_ref[...], _NT) + _dot(dgp_ref[...], wgate_ref[...], _NT)
              + ALPHA * dffn_ref[...])
        f = lambda xx, mm, gg, bb: _layernorm(ALPHA * xx + mm, gg, bb)
        _, vjp = jax.vjp(f, x_ref[...], mix_ref[...], g_ref[...], b_ref[...])
        dx, dm, dg, db = vjp(dh)
        outs[0][...] = dx
        outs[1][...] = dm.astype(BF)
        acc_ref[0:1, :] += dg
        acc_ref[1:2, :] += db

    row = pl.BlockSpec((tm, D), lambda i: (i, 0))
    par = pl.BlockSpec((1, D), lambda i: (0, 0))
    whole = lambda a: pl.BlockSpec(a.shape, lambda i: (0, 0))
    extras = [(du_g, pl.BlockSpec((tm, du_g.shape[1]), lambda i: (i, 0))), (wup_g, whole(wup_g)), (dgpre, row),
              (w_gate, whole(w_gate)), (dffn, row), (x, row), (mix, row), (g, par), (b, par)]
    return _mm_rows(du_u, wup_u, name="mm_dh1_ln1_bwd", tm=tm, mode="nt", sem="arbitrary", extras=extras,
                    out_shape=(_sds((T, D)), _sds((T, D), BF), _sds((8, D))),
                    out_specs=(row, row, pl.BlockSpec((8, D), lambda i: (0, 0))), epi=epi)


def _ffn_up_act_fwd(h1b, wup_g, wup_u, wg, wu, bg, bu, Bl, S):
    T, D = h1b.shape
    C = wup_g.shape[1]
    cb = _pick(C, (256, 128))
    kk = wg.shape[0]

    def body(h_ref, pg_ref, pu_ref, wg_ref, wu_ref, bg_ref, bu_ref, ug_ref, uu_ref, o_ref, ot_ref):
        h = h_ref[...]
        ug = _dot(h, pg_ref[...], _NN)
        uu = _dot(h, pu_ref[...], _NN)
        ug_ref[...] = ug
        uu_ref[...] = uu
        cg = _conv(ug, wg_ref[...]) + bg_ref[...]
        cu = _conv(uu, wu_ref[...]) + bu_ref[...]
        a = (_silu(cg) * cu).astype(o_ref.dtype)
        o_ref[...] = a
        ot_ref[...] = a.T

    blk = pl.BlockSpec((S, cb), lambda b, j: (b, j))
    pblk = pl.BlockSpec((D, cb), lambda b, j: (0, j))
    wblk = pl.BlockSpec((kk, cb), lambda b, j: (0, j))
    bblk = pl.BlockSpec((1, cb), lambda b, j: (0, j))
    return _pc(body, name="ffn_up_act_fwd", out_shape=(_sds((T, C)), _sds((T, C)), _sds((T, C), BF), _sds((C, T), BF)),
               grid=(Bl, C // cb), in_specs=[pl.BlockSpec((S, D), lambda b, j: (b, 0)), pblk, pblk, wblk, wblk, bblk, bblk],
               out_specs=(blk, blk, blk, pl.BlockSpec((cb, S), lambda b, j: (j, b))),
               sem=("parallel", "parallel"))(h1b, wup_g, wup_u, wg, wu, bg, bu)


def _ffn_act_bwd(ug, uu, wg, wu, bg, bu, dffnb, w_down, Bl, S):
    T, C = ug.shape
    D = dffnb.shape[1]
    cb = _pick(C, (256, 128))
    kk = wg.shape[0]

    def body(g_ref, u_ref, wg_ref, wu_ref, bg_ref, bu_ref, d_ref, wd_ref,
             dg_ref, du_ref, dwg_ref, dwu_ref, dbg_ref, dbu_ref):
        first = pl.program_id(1) == 0
        wgv, wuv = wg_ref[...], wu_ref[...]
        tg, tu = _taps(g_ref[...], kk), _taps(u_ref[...], kk)
        cg = _conv(None, wgv, tg) + bg_ref[...]
        cu = _conv(None, wuv, tu) + bu_ref[...]
        _, vjp = jax.vjp(lambda a, b: _silu(a) * b, cg, cu)
        dcg, dcu = vjp(_dot(d_ref[...], wd_ref[...], _NT))
        dg_ref[...] = _conv_bwd_x(dcg, wgv).astype(dg_ref.dtype)
        du_ref[...] = _conv_bwd_x(dcu, wuv).astype(du_ref.dtype)
        _conv_bwd_w(tg, dcg, dwg_ref, first)
        _conv_bwd_w(tu, dcu, dwu_ref, first)
        dbg_ref[...] = jnp.where(first, 0.0, dbg_ref[...]) + jnp.sum(dcg, axis=0, keepdims=True)
        dbu_ref[...] = jnp.where(first, 0.0, dbu_ref[...]) + jnp.sum(dcu, axis=0, keepdims=True)

    blk = pl.BlockSpec((S, cb), lambda j, b: (b, j))
    wblk = pl.BlockSpec((kk, cb), lambda j, b: (0, j))
    bblk = pl.BlockSpec((1, cb), lambda j, b: (0, j))
    out_shape = (_sds((T, C), BF), _sds((T, C), BF), _sds((kk, C)), _sds((kk, C)), _sds((1, C)), _sds((1, C)))
    return _pc(body, name="ffn_act_bwd", out_shape=out_shape, grid=(C // cb, Bl),
               in_specs=[blk, blk, wblk, wblk, bblk, bblk, pl.BlockSpec((S, D), lambda j, b: (b, 0)),
                         pl.BlockSpec((cb, D), lambda j, b: (j, 0))],
               out_specs=(blk, blk, wblk, wblk, bblk, bblk), sem=("parallel", "arbitrary"))(
        ug, uu, wg, wu, bg, bu, dffnb, w_down)


def _head(act, w_down, h1, h1b, w_gate, pb, w_proj, tgt, bg, g2, b2):
    T, D = h1.shape
    tm = _pick(T, (512, 256))

    def epi(ffn, xs, outs):
        h_ref, hb_ref, wg_ref, pb_ref, wp_ref, t_ref, bg_ref, g2_ref, b2_ref = xs
        acc_ref = outs[4]

        @pl.when(pl.program_id(0) == 0)
        def _():
            acc_ref[...] = jnp.zeros_like(acc_ref)

        h, tg = h_ref[...], t_ref[...]
        gpre = _dot(hb_ref[...], wg_ref[...], _NN)
        pp = _dot(pb_ref[...], wp_ref[...], _NN)

        def loss_fn(f, gp, p_, bgv, g2v, b2v):
            pre = ALPHA * h + f + jax.nn.sigmoid(gp + bgv) * p_
            err = _layernorm(pre, g2v, b2v) - tg
            return 0.5 * jnp.sum(jnp.mean(err * err, axis=-1, keepdims=True))

        loss, grads = jax.value_and_grad(loss_fn, argnums=(0, 1, 2, 3, 4, 5))(
            ffn, gpre, pp, bg_ref[...], g2_ref[...], b2_ref[...])
        outs[0][...] = grads[0]
        outs[1][...] = grads[0].astype(BF)
        outs[2][...] = grads[1].astype(BF)
        outs[3][...] = grads[2].astype(BF)
        acc_ref[0:1, :] += grads[3]
        acc_ref[1:2, :] += grads[4]
        acc_ref[2:3, :] += grads[5]
        acc_ref[3:4, :] += jnp.broadcast_to(loss, (1, D))

    row = pl.BlockSpec((tm, D), lambda i: (i, 0))
    par = pl.BlockSpec((1, D), lambda i: (0, 0))
    whole = lambda a: pl.BlockSpec(a.shape, lambda i: (0, 0))
    o_bf = _sds((T, D), BF)
    extras = [(h1, row), (h1b, row), (w_gate, whole(w_gate)), (pb, pl.BlockSpec((tm, pb.shape[1]), lambda i: (i, 0))),
              (w_proj, whole(w_proj)), (tgt, row), (bg, par), (g2, par), (b2, par)]
    return _mm_rows(act, w_down, name="mm_down_loss_head", tm=tm, sem="arbitrary", extras=extras,
                    out_shape=(_sds((T, D)), o_bf, o_bf, o_bf, _sds((8, D))),
                    out_specs=(row, row, row, row, pl.BlockSpec((8, D), lambda i: (0, 0))), epi=epi)


def _adamw(parts, w, m, v, name):
    R, C = w.shape
    tr = R if R <= 512 else _pick(R, (256,))

    def body(p_ref, w_ref, m_ref, v_ref, g_ref, d_ref, nm_ref, nv_ref):
        g = p_ref[0].astype(F32)
        for j in range(1, N_DEV):
            g = g + p_ref[j].astype(F32)
        g_ref[...] = g
        d_ref[...], nm_ref[...], nv_ref[...] = _adam_math(g, w_ref[...], m_ref[...], v_ref[...])

    blk = pl.BlockSpec((tr, C), lambda i: (i, 0))
    o = _sds((R, C))
    return _pc(body, name=name, out_shape=(o, o, o, o), grid=(R // tr,),
               in_specs=[pl.BlockSpec((N_DEV, tr, C), lambda i: (0, i, 0)), blk, blk, blk],
               out_specs=(blk, blk, blk, blk), sem=("parallel",))(parts, w, m, v)


def _adam_math(g, w, m, v):
    mm = ADAM_B1 * m + (1.0 - ADAM_B1) * g
    vv = ADAM_B2 * v + (1.0 - ADAM_B2) * jnp.square(g)
    m_hat = mm / (1.0 - ADAM_B1 ** ADAM_STEP)
    v_hat = vv / (1.0 - ADAM_B2 ** ADAM_STEP)
    return -ADAM_LR * (m_hat / (jnp.sqrt(v_hat) + ADAM_EPS) + ADAM_WD * w), mm, vv


def _small_layout(sizes):
    offs, off = {}, 0
    for n in SMALL_ORDER:
        offs[n] = off
        off += -(-sizes[n] // 128) * 128
    return offs, off


def _pack_small(d):
    return jnp.concatenate([_pad_lanes(d[n].reshape(1, -1), -(-d[n].size // 128) * 128) for n in SMALL_ORDER], axis=1)


def _adamw_small(parts, ws, ms, vs):
    k = len(SMALL_ORDER)
    sizes = {n: ws[n].shape[1] for n in SMALL_ORDER}
    offs, _ = _small_layout(sizes)

    def body(p_ref, *refs):
        ins, outs = refs[:3 * k], refs[3 * k:]
        for i, n in enumerate(SMALL_ORDER):
            lo, hi = offs[n], offs[n] + sizes[n]
            g = p_ref[0, :, lo:hi]
            for j in range(1, N_DEV):
                g = g + p_ref[j, :, lo:hi]
            d, mm, vv = _adam_math(g, ins[i][...], ins[k + i][...], ins[2 * k + i][...])
            outs[4 * i][...], outs[4 * i + 1][...], outs[4 * i + 2][...], outs[4 * i + 3][...] = g, d, mm, vv

    out_shape = tuple(_sds((1, sizes[n])) for n in SMALL_ORDER for _ in range(4))
    args = [ws[n] for n in SMALL_ORDER] + [ms[n] for n in SMALL_ORDER] + [vs[n] for n in SMALL_ORDER]
    res = _pc(body, name="adamw_small", out_shape=out_shape)(parts, *args)
    return {n: tuple(res[4 * i:4 * i + 4]) for i, n in enumerate(SMALL_ORDER)}


def _prep(x, p, xchg=None):
    T, D = x.shape
    Dp = p.shape[1]
    tt = _pick(T, (512, 256))

    def body(x_ref, p_ref, xb_ref, xt_ref, pb_ref, pt_ref):
        xb = x_ref[...].astype(BF)
        xb_ref[...] = xb
        xt_ref[...] = xb.T
        pb = p_ref[...].astype(BF)
        pb_ref[...] = pb
        pt_ref[...] = pb.T

    row = lambda w: pl.BlockSpec((tt, w), lambda i: (i, 0))
    col = lambda w: pl.BlockSpec((w, tt), lambda i: (0, i))
    return _pc(body, name="prep_inputs",
               out_shape=(_sds((T, D), BF), _sds((D, T), BF), _sds((T, Dp), BF), _sds((Dp, T), BF)),
               grid=(T // tt,), in_specs=[row(D), row(Dp)], out_specs=(row(D), col(D), row(Dp), col(Dp)),
               sem=("parallel",), xchg=xchg)(x, p)


SMALL_ORDER = ("gdn_a_log", "gdn_dt_bias", "gdn_norm_g", "mla_q_norm_g", "mla_kv_norm_g", "ln1_g", "ln1_b",
               "ffn_conv_b", "ple_b_gate", "ln2_g", "ln2_b")
EARLY = ("w_in", "gdn_conv_w")
LATE = ("mla_w_q_up", "mla_w_kv_up", "w_out", "ffn_w_up", "ffn_conv_w", "ffn_w_down", "ple_w_gate", "ple_w_proj")
GRADS_EARLY = ("w_out", "ffn_w_up", "ffn_conv_w", "ffn_w_down", "ple_w_gate", "ple_w_proj")
GRADS_LATE = ("w_in", "gdn_conv_w", "mla_w_q_up", "mla_w_kv_up")


def _pad_lanes(v, n=128):
    return jnp.pad(v, ((0, 0), (0, n - v.shape[1])))


def _local_step(x, p, tgt, W, sp, Bl, S, early_weights=None, late_weights=None, early_grads=None, late_grads=None):
    T = Bl * S
    W = dict(W)
    prep = _prep(x, p, xchg=None if early_weights is None else early_weights[:2])
    xb, xT, pb, pT = prep[:4]
    if early_weights is not None:
        W.update(early_weights[2](prep[4:]))
    w_in = W["w_in"]
    wqkv, wz = w_in[:, :1536], w_in[:, 1536:2048]
    z64 = jnp.zeros((w_in.shape[0], 64), w_in.dtype)
    z124 = jnp.zeros((w_in.shape[0], 124), w_in.dtype)
    wsm = jnp.concatenate([w_in[:, 2056:2440], w_in[:, 2696:2760], z64, w_in[:, 2440:2696],
                           w_in[:, 2048:2052], z124, w_in[:, 2052:2056], z124], axis=1)
    alog, dtb = _pad_lanes(sp["gdn_a_log"]), _pad_lanes(sp["gdn_dt_bias"])
    cos, sin = _rope_tables(S)

    qkv, qkvn = _gdn_proj_pre_fwd(xb, wqkv, W["gdn_conv_w"], Bl, S)
    z = _mm(xb, wz, "nn", name="mm_z")
    sm = _mm(xb, wsm, "nn", name="mm_sm")
    gdn_out = _gdn_fwd(qkvn, sm, alog, dtb, Bl, S, xchg=None if late_weights is None else late_weights[:2])
    o, tinv_s, sst_s, w_s, u_s, gc_s, beta_s = gdn_out[:7]
    if late_weights is not None:
        W.update(late_weights[2](gdn_out[7:]))
    wq = jnp.pad(W["mla_w_q_up"].reshape(-1, MLA_HEADS, 192), ((0, 0), (0, 0), (0, 64))).reshape(-1, 1024)
    wkv = W["mla_w_kv_up"]
    C = W["ffn_w_down"].shape[0]
    wup_g, wup_u = W["ffn_w_up"][:, :C], W["ffn_w_up"][:, C:]
    cw_g, cw_u = W["ffn_conv_w"][:, :C], W["ffn_conv_w"][:, C:]
    cb_g, cb_u = sp["ffn_conv_b"][:, :C], sp["ffn_conv_b"][:, C:]
    mixin, mixin_t = _gdn_post(o, z, sp["gdn_norm_g"], T)
    cqn, cqn_t = _rms_fwd(sm, sp["mla_q_norm_g"], 0, "rms_q_fwd")
    ckvn, ckvn_t = _rms_fwd(sm, sp["mla_kv_norm_g"], 2, "rms_kv_fwd")
    qcat = _qup_rope(cqn, wq, cos, sin, S)
    kv, kcat = _kvup_rope(ckvn, wkv, sm, cos, sin, S)
    mixin, lse, mixin_t = _attn_fwd(mixin, mixin_t, qcat, kcat, kv, Bl, S)
    mix, h1, h1b, h1T = _out_ln1(mixin, W["w_out"], x, sp["ln1_g"], sp["ln1_b"])
    ug, uu, act, act_t = _ffn_up_act_fwd(h1b, wup_g, wup_u, cw_g, cw_u, cb_g, cb_u, Bl, S)

    dffn, dffnb, dgpre, dpp, hacc = _head(act, W["ffn_w_down"], h1, h1b, W["ple_w_gate"], pb, W["ple_w_proj"], tgt,
                                          sp["ple_b_gate"], sp["ln2_g"], sp["ln2_b"])
    loss = hacc[3, 0]
    gW, gs = {}, {}
    gs["ple_b_gate"], gs["ln2_g"], gs["ln2_b"] = hacc[0:1], hacc[1:2], hacc[2:3]
    gW["ple_w_proj"] = _mm(pT, dpp, "nn", name="mm_dproj", out_dtype=BF)
    gW["ple_w_gate"] = _mm(h1T, dgpre, "nn", name="mm_dgate", out_dtype=BF)
    gW["ffn_w_down"] = _mm(act_t, dffnb, "nn", name="mm_ddown", out_dtype=BF)
    du_g, du_u, dcw_g, dcw_u, dcb_g, dcb_u = _ffn_act_bwd(ug, uu, cw_g, cw_u, cb_g, cb_u, dffnb, W["ffn_w_down"],
                                                          Bl, S)
    gW["ffn_conv_w"] = jnp.concatenate([dcw_g, dcw_u], axis=1)
    gs["ffn_conv_b"] = jnp.concatenate([dcb_g, dcb_u], axis=1)
    gW["ffn_w_up"] = jnp.concatenate([_mm(h1T, du_g, "nn", name="mm_dup_gate", out_dtype=BF),
                                      _mm(h1T, du_u, "nn", name="mm_dup_up", out_dtype=BF)], axis=1)
    dxa, dmix, acc1 = _dh1_ln1_bwd(du_u, wup_u, du_g, wup_g, dgpre, W["ple_w_gate"], dffn, x, mix,
                                   sp["ln1_g"], sp["ln1_b"])
    gs["ln1_g"], gs["ln1_b"] = acc1[0:1], acc1[1:2]
    gW["w_out"] = _mm(mixin_t, dmix, "nn", name="mm_dwout", out_dtype=BF)
    dmixin = _mm(dmix, W["w_out"], "nt", name="mm_dmixin")
    do, dcat, gacc = _gdn_post_bwd(o, z, sp["gdn_norm_g"], dmixin, T)
    gs["gdn_norm_g"] = gacc[0:1]
    bwd_out = _gdn_bwd(qkvn, sm, alog, dtb, gc_s, beta_s, tinv_s, sst_s, w_s, u_s, do, dcat, Bl, S,
                       xchg=None if early_grads is None else early_grads({n: gW[n] for n in GRADS_EARLY}))
    dqkvn, dcat, cacc = bwd_out[:3]
    early_recv = bwd_out[3:]
    gs["gdn_a_log"], gs["gdn_dt_bias"] = cacc[0:1, :GDN_HEADS], cacc[1:2, :GDN_HEADS]
    dcat, gW["gdn_conv_w"] = _gdn_pre_bwd(qkv, W["gdn_conv_w"], dqkvn, dcat, Bl, S)
    dqc, dkv, dkrr = _attn_bwd(qcat, kcat, kv, lse, dmixin, Bl, S)
    dqraw, dcat = _rope_bwd(dqc, dcat, dkrr, cos, sin, S)
    gwq = _mm(cqn_t, dqraw, "nn", name="mm_dwq", out_dtype=BF)
    gW["mla_w_q_up"] = gwq.reshape(-1, MLA_HEADS, 256)[:, :, :192].reshape(-1, MLA_HEADS * 192)
    gW["mla_w_kv_up"] = _mm(ckvn_t, dkv, "nn", name="mm_dwkv", out_dtype=BF)
    dcqn = _mm(dqraw, wq, "nt", name="mm_dcqn")
    dckvn = _mm(dkv, wkv, "nt", name="mm_dckvn")
    dcat, qacc = _rms_bwd(sm, sp["mla_q_norm_g"], dcqn, dcat, 0, "rms_q_bwd")
    dcat, kacc = _rms_bwd(sm, sp["mla_kv_norm_g"], dckvn, dcat, 2, "rms_kv_bwd")
    gs["mla_q_norm_g"], gs["mla_kv_norm_g"] = qacc[0:1], kacc[0:1]
    gcat = _mm(xT, dcat, "nn", name="mm_dwin", out_dtype=BF)
    gsm, gqkv, gz = gcat[:, :DCAT_QKV], gcat[:, DCAT_QKV:DCAT_Z], gcat[:, DCAT_Z:]
    gW["w_in"] = jnp.concatenate([gqkv, gz, gsm[:, 768:772], gsm[:, 896:900], gsm[:, 0:384], gsm[:, 512:768],
                                  gsm[:, 384:448]], axis=1)
    wcat = jnp.concatenate([wsm, wqkv, wz], axis=1)
    dx = _mm(dcat, wcat, "nt", name="mm_dx", add=dxa, xchg=None if late_grads is None else late_grads(gW, gs))
    late_recv = ()
    if late_grads is not None:
        dx, late_recv = dx[0], dx[1:]
    return loss, dx, gW, gs, early_recv, late_recv


COL_SHARDED = ("w_in", "mla_w_q_up", "mla_w_kv_up", "ffn_w_up", "ple_w_proj", "gdn_conv_w", "ffn_conv_w")
SHARDED = EARLY + LATE
WEIGHTS = ("w_in", "gdn_conv_w", "gdn_a_log", "gdn_dt_bias", "gdn_norm_g", "mla_q_norm_g", "mla_w_q_up",
           "mla_kv_norm_g", "mla_w_kv_up", "w_out", "ln1_g", "ln1_b", "ffn_w_up", "ffn_conv_w", "ffn_conv_b",
           "ffn_w_down", "ple_w_gate", "ple_b_gate", "ple_w_proj", "ln2_g", "ln2_b")
CONV = ("gdn_conv_w", "ffn_conv_w")


def _gathered_to_full(name, g):
    if name in COL_SHARDED:
        return jnp.transpose(g, (1, 0, 2)).reshape(g.shape[1], -1)
    return g.reshape(-1, g.shape[-1])


def _full_to_blocks(name, gfull, shard_shape):
    r, c = shard_shape
    if name in COL_SHARDED:
        return jnp.transpose(gfull.reshape(r, N_DEV, c), (1, 0, 2))
    return gfull.reshape(N_DEV, r, c)


def kernel(x, p, w_in, gdn_conv_w, gdn_a_log, gdn_dt_bias, gdn_norm_g, mla_q_norm_g, mla_w_q_up, mla_kv_norm_g, mla_w_kv_up, w_out, ln1_g, ln1_b, ffn_w_up, ffn_conv_w, ffn_conv_b, ffn_w_down, ple_w_gate, ple_b_gate, ple_w_proj, ln2_g, ln2_b, loss_target, m_w_in, m_gdn_conv_w, m_gdn_a_log, m_gdn_dt_bias, m_gdn_norm_g, m_mla_q_norm_g, m_mla_w_q_up, m_mla_kv_norm_g, m_mla_w_kv_up, m_w_out, m_ln1_g, m_ln1_b, m_ffn_w_up, m_ffn_conv_w, m_ffn_conv_b, m_ffn_w_down, m_ple_w_gate, m_ple_b_gate, m_ple_w_proj, m_ln2_g, m_ln2_b, v_w_in, v_gdn_conv_w, v_gdn_a_log, v_gdn_dt_bias, v_gdn_norm_g, v_mla_q_norm_g, v_mla_w_q_up, v_mla_kv_norm_g, v_mla_w_kv_up, v_w_out, v_ln1_g, v_ln1_b, v_ffn_w_up, v_ffn_conv_w, v_ffn_conv_b, v_ffn_w_down, v_ple_w_gate, v_ple_b_gate, v_ple_w_proj, v_ln2_g, v_ln2_b):
    loc = dict(locals())
    wts = {n: loc[n] for n in WEIGHTS}
    ms = {n: loc["m_" + n] for n in WEIGHTS}
    vs = {n: loc["v_" + n] for n in WEIGHTS}
    Bl, S, D = x.shape
    T = Bl * S

    wire = lambda n: wts[n][0] if n in CONV else wts[n][0].astype(BF)
    gather = lambda names, mode: ([wire(n) for n in names], [mode] * len(names),
                                  lambda res: {n: _gathered_to_full(n, g) for n, g in zip(names, res)})
    blocks = lambda g, names: [_full_to_blocks(n, g[n], wts[n].shape[1:]) for n in names]
    pack_early = lambda g: (blocks(g, GRADS_EARLY), ["a2a"] * len(GRADS_EARLY))
    pack_late = lambda g, gs: (blocks(g, GRADS_LATE) + [_pack_small(gs)], ["a2a"] * len(GRADS_LATE) + ["ag"])
    sp = {n: wts[n].reshape(1, -1) for n in SMALL_ORDER}

    loss, dx, gW, gs, early_recv, late_recv = _local_step(
        x.reshape(T, D), p.reshape(T, -1), loss_target.reshape(T, D), {}, sp, Bl, S,
        early_weights=gather(EARLY, "ag2"), late_weights=gather(LATE, "ag"), early_grads=pack_early,
        late_grads=pack_late)

    res = {}
    for n, parts in list(zip(GRADS_EARLY, early_recv)) + list(zip(GRADS_LATE, late_recv[:-1])):
        res[n] = tuple(t[None] for t in _adamw(parts, wts[n][0], ms[n][0], vs[n][0], "adamw_" + n))
    res.update(_adamw_small(late_recv[-1], wts, ms, vs))

    loss = lax.psum(loss, ("x", "y", "c"))
    outs = [loss, dx.reshape(Bl, S, D)]
    for k in range(4):
        outs += [res[n][k] for n in WEIGHTS]
    return tuple(outs)
```

```python
import math

import jax
import jax.numpy as jnp
from jax import lax
from jax.experimental import pallas as pl
from jax.experimental.pallas import tpu as pltpu

F32 = jnp.float32
BF = jnp.bfloat16
_MXU = jnp.bfloat16
_INTERPRET = None
_VMEM_LIMIT = 56 * 1024 * 1024
HI = lax.Precision.HIGHEST

N_DEV = 8
CHUNK = 64
GDN_HEADS = 4
GDN_DK = 128
MLA_HEADS = 4
MLA_NOPE = 128
MLA_ROPE = 64
ROPE_THETA = 10000.0
ALPHA = 2.0 ** 0.25
NORM_EPS = 1e-6
ATTN_SCALE = (MLA_NOPE + MLA_ROPE) ** -0.5
DCAT_QKV, DCAT_Z, DCAT_W = 1024, 2560, 3072
ADAM_LR, ADAM_B1, ADAM_B2, ADAM_EPS, ADAM_WD, ADAM_STEP = 0.001, 0.9, 0.999, 1e-08, 0.01, 10
MESH = pl.DeviceIdType.MESH
ANY = pl.BlockSpec(memory_space=pl.ANY)
_NN = (((1,), (0,)), ((), ()))
_NT = (((1,), (1,)), ((), ()))
_TN = (((0,), (0,)), ((), ()))


def _sds(shape, dtype=F32):
    return jax.ShapeDtypeStruct(tuple(shape), dtype)


def _pick(n, cands):
    for c in cands:
        if n % c == 0:
            return c
    return n


def _xchg_out_shapes(arrs, modes):
    return [_sds(a.shape if m == "a2a" else (N_DEV,) + a.shape, a.dtype) for a, m in zip(arrs, modes)]


def _xchg_scratch(n):
    return [pltpu.SemaphoreType.DMA((n, N_DEV - 1)), pltpu.SemaphoreType.DMA((n, N_DEV - 1)),
            pltpu.SemaphoreType.DMA((n,))]


def _xchg_plan(ins, outs, send, recv, loc, modes):
    x, y, c = lax.axis_index("x"), lax.axis_index("y"), lax.axis_index("c")
    me = 4 * x + 2 * y + c
    starts, relays, waits = [], [], []
    for ai, mode in enumerate(modes):
        src_all, out = ins[ai], outs[ai]

        def remote(k, src, dst, dev):
            return pltpu.make_async_remote_copy(src_ref=src, dst_ref=dst, send_sem=send.at[ai, k], recv_sem=recv.at[ai, k],
                                                device_id=dev, device_id_type=MESH)

        own = pltpu.make_async_copy(src_all.at[me] if mode == "a2a" else src_all, out.at[me], loc.at[ai])
        starts.append(own)
        waits.append(own.wait)
        if mode == "ag2":
            sib = (x, y, 1 - c)
            chips = [(1 - x, y), (x, 1 - y), (1 - x, 1 - y)]
            first = [remote(0, src_all, out.at[me], sib)]
            first += [remote(1 + k, src_all, out.at[me], (px, py, c)) for k, (px, py) in enumerate(chips)]
            starts += first
            waits += [cp.wait_send for cp in first]
            waits.append(remote(0, src_all, out.at[4 * x + 2 * y + 1 - c], sib).wait_recv)
            for k, (px, py) in enumerate(chips):
                same, other = out.at[4 * px + 2 * py + c], out.at[4 * px + 2 * py + 1 - c]
                relay = remote(4 + k, same, same, sib)
                relays.append((remote(1 + k, src_all, same, (px, py, c)), relay))
                waits += [relay.wait_send, remote(4 + k, src_all, other, sib).wait_recv]
            continue
        for r in range(1, N_DEV):
            px = 1 - x if (r >> 2) & 1 else x
            py = 1 - y if (r >> 1) & 1 else y
            pc = 1 - c if r & 1 else c
            cp = remote(r - 1, src_all.at[4 * px + 2 * py + pc] if mode == "a2a" else src_all, out.at[me], (px, py, pc))
            starts.append(cp)
            waits.append(cp.wait)
    return starts, relays, waits


def _pc(body, *, name, out_shape, grid=None, in_specs=None, out_specs=None, scratch=(), sem=None, aliases=None,
        xchg=None):
    kw = {}
    if _INTERPRET is not None:
        kw["interpret"] = _INTERPRET
    single = not isinstance(out_shape, (tuple, list))
    out_shape = [out_shape] if single else list(out_shape)
    if out_specs is not None:
        out_specs = [out_specs] if single else list(out_specs)
    scratch = list(scratch)
    if xchg is not None:
        xarrs, xmodes = xchg
        n_in, n_out, n_scr, nx = len(in_specs), len(out_shape), len(scratch), len(xarrs)
        single = False
        inner = body

        def body(*refs):
            ins, xins = refs[:n_in], refs[n_in:n_in + nx]
            outs = refs[n_in + nx:n_in + nx + n_out]
            xouts = refs[n_in + nx + n_out:n_in + 2 * nx + n_out]
            scr = refs[n_in + 2 * nx + n_out:n_in + 2 * nx + n_out + n_scr]
            send, recv, loc = refs[n_in + 2 * nx + n_out + n_scr:]
            first = last = None
            for d, g in enumerate(grid):
                f, l = pl.program_id(d) == 0, pl.program_id(d) == g - 1
                first, last = (f, l) if first is None else (first & f, last & l)

            @pl.when(first)
            def _():
                for cp in _xchg_plan(xins, xouts, send, recv, loc, xmodes)[0]:
                    cp.start()

            inner(*ins, *outs, *scr)

            @pl.when(last)
            def _():
                _, relays, waits = _xchg_plan(xins, xouts, send, recv, loc, xmodes)
                for arrival, relay in relays:
                    arrival.wait_recv()
                    relay.start()
                for wait in waits:
                    wait()

        in_specs = list(in_specs) + [ANY] * nx
        out_specs = out_specs + [ANY] * nx
        out_shape = out_shape + _xchg_out_shapes(xarrs, xmodes)
        scratch = scratch + _xchg_scratch(nx)
        sem = ("arbitrary",) * len(grid)
    if grid is not None:
        kw["grid"] = grid
    if in_specs is not None:
        kw["in_specs"] = in_specs
    if out_specs is not None:
        kw["out_specs"] = out_specs[0] if single else tuple(out_specs)
    cp = dict(vmem_limit_bytes=_VMEM_LIMIT)
    if sem is not None:
        cp["dimension_semantics"] = sem
    call = pl.pallas_call(body, name=name, out_shape=out_shape[0] if single else tuple(out_shape),
                          scratch_shapes=scratch, input_output_aliases=aliases or {},
                          compiler_params=pltpu.CompilerParams(**cp), **kw)
    if xchg is None:
        return call
    return lambda *ins: call(*ins, *xchg[0])


_FULL_K_MAX = 3072


def _mm(a, b, mode, *, name, add=None, add_scale=1.0, out_dtype=F32, xchg=None):
    (M, K) = a.shape
    (K2, N) = b.shape if mode == "nn" else b.shape[::-1]
    assert K == K2, (a.shape, b.shape, mode)
    tm = _pick(M, (1024, 1408, 512, 384, 256, 128))
    tn = _pick(N, (1024, 2816, 1408, 768, 512, 384, 256, 128))
    tk = K if K <= _FULL_K_MAX else _pick(K, (2048, 1408, 1024, 512) if tn <= 1408 else (1024, 512))
    nk = K // tk
    dims = _NN if mode == "nn" else _NT
    has_add = add is not None

    def finish(r, add_ref, o_ref):
        if has_add:
            r = r + add_scale * add_ref[...].astype(F32)
        o_ref[...] = r.astype(out_dtype)

    if nk == 1:
        def body(a_ref, b_ref, *rest):
            r = lax.dot_general(a_ref[...].astype(_MXU), b_ref[...].astype(_MXU), dims, preferred_element_type=F32)
            finish(r, rest[0] if has_add else None, rest[-1])

        a_spec = pl.BlockSpec((tm, K), lambda i, j: (i, 0))
        b_spec = (pl.BlockSpec((K, tn), lambda i, j: (0, j)) if mode == "nn"
                  else pl.BlockSpec((tn, K), lambda i, j: (j, 0)))
        o_spec = pl.BlockSpec((tm, tn), lambda i, j: (i, j))
        grid, sem, scratch = (M // tm, N // tn), ("parallel", "parallel"), []
    else:
        def body(a_ref, b_ref, *rest):
            o_ref, acc = rest[-2], rest[-1]
            k = pl.program_id(2)

            @pl.when(k == 0)
            def _():
                acc[...] = jnp.zeros_like(acc)

            acc[...] += lax.dot_general(a_ref[...].astype(_MXU), b_ref[...].astype(_MXU), dims,
                                        preferred_element_type=F32)

            @pl.when(k == nk - 1)
            def _():
                finish(acc[...], rest[0] if has_add else None, o_ref)

        a_spec = pl.BlockSpec((tm, tk), lambda i, j, k: (i, k))
        b_spec = (pl.BlockSpec((tk, tn), lambda i, j, k: (k, j)) if mode == "nn"
                  else pl.BlockSpec((tn, tk), lambda i, j, k: (j, k)))
        o_spec = pl.BlockSpec((tm, tn), lambda i, j, k: (i, j))
        grid, sem, scratch = (M // tm, N // tn, nk), ("parallel", "parallel", "arbitrary"), [pltpu.VMEM((tm, tn), F32)]

    ins = [a, b] + ([add] if has_add else [])
    specs = [a_spec, b_spec] + ([o_spec] if has_add else [])
    return _pc(body, name=name, out_shape=_sds((M, N), out_dtype), grid=grid, in_specs=specs, out_specs=o_spec,
               scratch=scratch, sem=sem, xchg=xchg)(*ins)


def _dot(a, b, dims):
    return lax.dot_general(a.astype(_MXU), b.astype(_MXU), dims, preferred_element_type=F32)


def _split(a):
    hi = a.astype(BF)
    lo = (a - hi.astype(F32)).astype(BF)
    return hi, lo


def _dot3(a, b, dims):
    ah, al = _split(a)
    bh, bl = _split(b)
    d = lambda u, v: lax.dot_general(u, v, dims, preferred_element_type=F32)
    return d(ah, bh) + (d(ah, bl) + d(al, bh))


def _softplus(x):
    return jnp.maximum(x, 0.0) + jnp.log1p(jnp.exp(-jnp.abs(x)))


def _silu(x):
    return x * jax.nn.sigmoid(x)


def _rmsnorm(x, g):
    return x * lax.rsqrt(jnp.mean(x * x, axis=-1, keepdims=True) + NORM_EPS) * g


def _layernorm(x, g, b):
    mu = jnp.mean(x, axis=-1, keepdims=True)
    xc = x - mu
    var = jnp.mean(xc * xc, axis=-1, keepdims=True)
    return xc * lax.rsqrt(var + NORM_EPS) * g + b


def _shift_dn(x, s):
    if s == 0:
        return x
    t = lax.broadcasted_iota(jnp.int32, x.shape, 0)
    return jnp.where(t >= s, pltpu.roll(x, s, 0), 0.0)


def _shift_up(x, s):
    if s == 0:
        return x
    n = x.shape[0]
    t = lax.broadcasted_iota(jnp.int32, x.shape, 0)
    return jnp.where(t < n - s, pltpu.roll(x, n - s, 0), 0.0)


def _taps(x, kk):
    return [_shift_dn(x, kk - 1 - j) for j in range(kk)]


def _conv(x, w, taps=None):
    kk = w.shape[0]
    taps = _taps(x, kk) if taps is None else taps
    y = w[kk - 1:kk, :] * taps[kk - 1]
    for j in range(kk - 1):
        y = y + w[j:j + 1, :] * taps[j]
    return y


def _conv_bwd_x(dy, w):
    kk = w.shape[0]
    dx = w[kk - 1:kk, :] * dy
    for j in range(kk - 1):
        dx = dx + w[j:j + 1, :] * _shift_up(dy, kk - 1 - j)
    return dx


def _conv_bwd_w(taps, dy, dw_ref, first):
    kk = dw_ref.shape[0]
    for j in range(kk):
        r = jnp.sum(dy * taps[j], axis=0, keepdims=True)
        prev = jnp.where(first, 0.0, dw_ref[j:j + 1, :])
        dw_ref[j:j + 1, :] = prev + r


def _gdn_post_conv(c, j):
    h = _silu(c)
    hn = h * lax.rsqrt(jnp.sum(h * h, axis=-1, keepdims=True) + NORM_EPS)
    return jnp.where(j < GDN_HEADS, hn * (GDN_DK ** -0.5), jnp.where(j < 2 * GDN_HEADS, hn, h))


def _gdn_proj_pre_fwd(xb, wqkv, conv_w, Bl, S):
    T, D = xb.shape
    C = wqkv.shape[1]
    nj = C // 256

    def body(x_ref, p_ref, w_ref, qkv_ref, o_ref):
        j = pl.program_id(1)
        qkv = _dot(x_ref[...], p_ref[...], _NN)
        qkv_ref[...] = qkv
        c = _conv(qkv, w_ref[...])
        for half in range(2):
            o_ref[:, 128 * half:128 * (half + 1)] = _gdn_post_conv(c[:, 128 * half:128 * (half + 1)], 2 * j)

    blk = pl.BlockSpec((S, 256), lambda b, j: (b, j))
    return _pc(body, name="gdn_proj_pre_fwd", out_shape=(_sds((T, C)), _sds((T, C))), grid=(Bl, nj),
               in_specs=[pl.BlockSpec((S, D), lambda b, j: (b, 0)), pl.BlockSpec((D, 256), lambda b, j: (0, j)),
                         pl.BlockSpec((conv_w.shape[0], 256), lambda b, j: (0, j))],
               out_specs=(blk, blk), sem=("parallel", "parallel"))(xb, wqkv, conv_w)


def _gdn_pre_bwd(qkv, conv_w, dout, dcat, Bl, S):
    T, C = qkv.shape
    nj = C // 128
    kk = conv_w.shape[0]

    def body(x_ref, w_ref, d_ref, alias_ref, dx_ref, dw_ref):
        j, b = pl.program_id(0), pl.program_id(1)
        w = w_ref[...]
        taps = _taps(x_ref[...], kk)
        c = _conv(None, w, taps)
        _, vjp = jax.vjp(lambda u: _gdn_post_conv(u, j), c)
        (dc,) = vjp(d_ref[...])
        dx_ref[...] = _conv_bwd_x(dc, w).astype(dx_ref.dtype)
        _conv_bwd_w(taps, dc, dw_ref, b == 0)

    blk = pl.BlockSpec((S, 128), lambda j, b: (b, j))
    wblk = pl.BlockSpec((kk, 128), lambda j, b: (0, j))
    return _pc(body, name="gdn_pre_bwd", out_shape=(_sds(dcat.shape, dcat.dtype), _sds((kk, C))), grid=(nj, Bl),
               in_specs=[blk, wblk, blk, ANY],
               out_specs=(pl.BlockSpec((S, 128), lambda j, b: (b, DCAT_QKV // 128 + j)), wblk),
               aliases={3: 0}, sem=("parallel", "arbitrary"))(qkv, conv_w, dout, dcat)


_H = GDN_HEADS
_HC = _H * CHUNK


def _st(x):
    return jnp.concatenate([x[:, h * 128:(h + 1) * 128] for h in range(_H)], axis=0)


def _unst(x):
    return jnp.concatenate([x[h * CHUNK:(h + 1) * CHUNK] for h in range(_H)], axis=1)


def _stc(t):
    return jnp.concatenate([t[:, h:h + 1] for h in range(_H)], axis=0)


def _untile(col):
    lane = lax.broadcasted_iota(jnp.int32, (CHUNK, 128), 1)
    out = jnp.zeros((CHUNK, 128), F32)
    for h in range(_H):
        out = out + jnp.where(lane == h, col[h * CHUNK:(h + 1) * CHUNK], 0.0)
    return out


def _rowform(col):
    return jnp.broadcast_to(col, (_HC, 128)).T[0:1, :]


def _tri(n, upper=False):
    i = lax.broadcasted_iota(jnp.int32, (n, n), 0)
    j = lax.broadcasted_iota(jnp.int32, (n, n), 1)
    return jnp.where((j >= i) if upper else (j <= i), 1.0, 0.0).astype(F32)


def _gdn_chunk_common(q, k, v, gc, beta):
    Q, K, V = _st(q), _st(k), _st(v)
    B, GC = _stc(beta), _stc(gc)
    GL = jnp.concatenate([jnp.broadcast_to(gc[CHUNK - 1:CHUNK, h:h + 1], (CHUNK, 1)) for h in range(_H)], axis=0)
    ii = lax.broadcasted_iota(jnp.int32, (_HC, _HC), 0)
    jj = lax.broadcasted_iota(jnp.int32, (_HC, _HC), 1)
    same = (ii >> 6) == (jj >> 6)
    incl = same & (ii >= jj)
    strict = same & (ii > jj)
    diff = GC - _rowform(GC)
    D = jnp.where(incl, jnp.exp(jnp.where(incl, diff, 0.0)), 0.0)
    KB = K * B
    A = jnp.where(strict, _dot(KB, K, _NT) * D, 0.0)
    P = jnp.where(incl, _dot(Q, K, _NT) * D, 0.0)
    EG = jnp.exp(GC)
    ED = jnp.exp(GL - GC)
    return dict(Q=Q, K=K, V=V, B=B, GC=GC, GL=GL, incl=incl, strict=strict, D=D, KB=KB, A=A, P=P, EG=EG, ED=ED,
                QG=Q * EG, KD=K * ED)


def _neumann_inv(A):
    n = A.shape[0]
    i = lax.broadcasted_iota(jnp.int32, (n, n), 0)
    j = lax.broadcasted_iota(jnp.int32, (n, n), 1)
    N = -A
    acc = jnp.where(i == j, 1.0, 0.0) + N
    Pw = N
    for _ in range(5):
        Pw = _dot3(Pw, Pw, _NN)
        acc = acc + _dot3(acc, Pw, _NN)
    return acc


def _gates(a, alog, dtb):
    return -jnp.exp(alog) * _softplus(a + dtb)


def _hs(x, h, n=CHUNK):
    return x[h * n:(h + 1) * n]


def _per_step(Bl):
    return 4 if Bl % 4 == 0 else 2 if Bl % 2 == 0 else 1


def _gdn_fwd(qkvn, sm, alog, dtb, Bl, S, xchg=None):
    T = Bl * S
    nC = S // CHUNK
    E = _per_step(Bl)

    def body(*refs):
        S_ref = refs[-1]

        @pl.when(pl.program_id(1) == 0)
        def _():
            S_ref[...] = jnp.zeros_like(S_ref)

        for e in range(E):
            one(*[r if i in (5, 6) else r.at[e] for i, r in enumerate(refs)])

    def one(q_ref, k_ref, v_ref, a_ref, b_ref, alog_ref, dtb_ref,
            o_ref, tinv_ref, sst_ref, w_ref, u_ref, gc_ref, beta_ref, S_ref):
        g = _gates(a_ref[...], alog_ref[...], dtb_ref[...])
        beta = jax.nn.sigmoid(b_ref[...])
        gc = jnp.dot(_tri(CHUNK), g, precision=HI, preferred_element_type=F32)
        cm = _gdn_chunk_common(q_ref[...], k_ref[...], v_ref[...], gc, beta)
        tinv = _neumann_inv(cm["A"])
        WU = _dot3(tinv, jnp.concatenate([cm["KB"] * cm["EG"], cm["V"] * cm["B"]], axis=1), _NN)
        W, U = WU[:, :128], WU[:, 128:]
        s_old = [S_ref[h * 128:(h + 1) * 128, :] for h in range(_H)]
        vn = [_hs(U, h) - _dot(_hs(W, h), s_old[h], _NN) for h in range(_H)]
        o_intra = _dot(cm["P"], jnp.concatenate(vn, axis=0), _NN)
        outs = []
        for h in range(_H):
            outs.append(_dot(_hs(cm["QG"], h), s_old[h], _NN) + _hs(o_intra, h))
            gl = jnp.exp(gc[CHUNK - 1:CHUNK, h:h + 1])
            S_ref[h * 128:(h + 1) * 128, :] = gl * s_old[h] + _dot(_hs(cm["KD"], h), vn[h], _TN)
            sst_ref[h * 128:(h + 1) * 128, :] = s_old[h]
        o_ref[...] = jnp.concatenate(outs, axis=1)
        tinv_ref[...] = tinv
        w_ref[...] = _unst(W)
        u_ref[...] = _unst(U)
        gc_ref[...] = gc
        beta_ref[...] = beta

    blk = lambda r, w, c: pl.BlockSpec((E, r, w), lambda b, n: (b, n, c))
    par = pl.BlockSpec((1, 128), lambda b, n: (0, 0))
    out_shape = (_sds((Bl, S, 512)), _sds((Bl, nC * _HC, _HC)), _sds((Bl, nC * 512, 128)),
                 _sds((Bl, S, 512)), _sds((Bl, S, 512)), _sds((Bl, S, 128)), _sds((Bl, S, 128)))
    out_specs = (blk(CHUNK, 512, 0), blk(_HC, _HC, 0), blk(512, 128, 0), blk(CHUNK, 512, 0), blk(CHUNK, 512, 0),
                 blk(CHUNK, 128, 0), blk(CHUNK, 128, 0))
    q3, s3 = qkvn.reshape(Bl, S, -1), sm.reshape(Bl, S, -1)
    res = _pc(body, name="gdn_chunk_fwd", out_shape=out_shape, grid=(Bl // E, nC),
              in_specs=[blk(CHUNK, 512, 0), blk(CHUNK, 512, 1), blk(CHUNK, 512, 2), blk(CHUNK, 128, 6),
                        blk(CHUNK, 128, 7), par, par],
              out_specs=out_specs, scratch=[pltpu.VMEM((E, 512, 128), F32)], sem=("parallel", "arbitrary"),
              xchg=xchg)(q3, q3, q3, s3, s3, alog, dtb)
    return (res[0].reshape(T, 512),) + tuple(res[1:])


def _gdn_bwd(qkvn, sm, alog, dtb, gc_s, beta_s, tinv_s, sst_s, w_s, u_s, do, dcat, Bl, S, xchg=None):
    T = Bl * S
    nC = S // CHUNK
    E = _per_step(Bl)

    def body(*refs):
        acc_ref, dS_ref = refs[-2], refs[-1]
        first_chunk = pl.program_id(1) == 0

        @pl.when(first_chunk)
        def _():
            dS_ref[...] = jnp.zeros_like(dS_ref)

        @pl.when(first_chunk & (pl.program_id(0) == 0))
        def _():
            acc_ref[...] = jnp.zeros_like(acc_ref)

        for e in range(E):
            one(*[r if i in (4, 5, 13, 16) else r.at[e] for i, r in enumerate(refs)])

    def one(q_ref, k_ref, v_ref, a_ref, alog_ref, dtb_ref, gc_ref, beta_ref, tinv_ref, sst_ref, w_ref, u_ref,
            do_ref, alias_ref, dqkv_ref, dsm_ref, acc_ref, dS_ref):
        gc, beta = gc_ref[...], beta_ref[...]
        cm = _gdn_chunk_common(q_ref[...], k_ref[...], v_ref[...], gc, beta)
        Q, K, V, B, D, KB, A, P = (cm[n] for n in ("Q", "K", "V", "B", "D", "KB", "A", "P"))
        EG, ED, QG, KD = cm["EG"], cm["ED"], cm["QG"], cm["KD"]
        tinv = tinv_ref[...]
        W, U, DO = _st(w_ref[...]), _st(u_ref[...]), _st(do_ref[...])
        s_old = [sst_ref[h * 128:(h + 1) * 128, :] for h in range(_H)]
        ds_new = [dS_ref[h * 128:(h + 1) * 128, :] for h in range(_H)]
        VN = jnp.concatenate([_hs(U, h) - _dot(_hs(W, h), s_old[h], _NN) for h in range(_H)], axis=0)
        dP = jnp.where(cm["incl"], _dot(DO, VN, _NT), 0.0)
        dVN0 = _dot(P, DO, _TN)
        dVN, dQG, dKD, dW, TL = [], [], [], [], []
        for h in range(_H):
            gl = jnp.exp(gc[CHUNK - 1:CHUNK, h:h + 1])
            dvn = _hs(dVN0, h) + _dot(_hs(KD, h), ds_new[h], _NN)
            dkd = _dot(_hs(VN, h), ds_new[h], _NT)
            dVN.append(dvn)
            dQG.append(_dot(_hs(DO, h), s_old[h], _NT))
            dKD.append(dkd)
            dW.append(-_dot(dvn, s_old[h], _NT))
            dgl = jnp.sum(jnp.sum(ds_new[h] * s_old[h], axis=1, keepdims=True), axis=0, keepdims=True)
            tl = jnp.sum(jnp.sum(dkd * _hs(KD, h), axis=1, keepdims=True), axis=0, keepdims=True) + dgl * gl
            TL.append(jnp.broadcast_to(tl, (CHUNK, 1)))
            dS_ref[h * 128:(h + 1) * 128, :] = (_dot(_hs(QG, h), _hs(DO, h), _TN) + gl * ds_new[h]
                                                - _dot(_hs(W, h), dvn, _TN))
        dVN, dQG, dKD, dW, TL = (jnp.concatenate(z, axis=0) for z in (dVN, dQG, dKD, dW, TL))
        dB = _dot3(tinv, jnp.concatenate([dVN, dW], axis=1), _TN)
        dVB, dKBE = dB[:, :128], dB[:, 128:]
        dA = jnp.where(cm["strict"], -_dot3(dB, jnp.concatenate([U, W], axis=1), _NT), 0.0)
        dG = dA * D
        dQK = dP * D
        dKB = _dot(dG, K, _NN) + dKBE * EG
        dK = (_dot(jnp.concatenate([dG, dQK], axis=0), jnp.concatenate([KB, Q], axis=0), _TN)
              + dKB * B + dKD * ED)
        dQ = _dot(dQK, K, _NN) + dQG * EG
        Mx = dA * A + dP * P
        rs = lambda z: jnp.sum(z, axis=1, keepdims=True)
        ri = lax.broadcasted_iota(jnp.int32, (_HC, 1), 0)
        dGC = (rs(Mx) - rs(Mx.T) + rs(dKBE * KB * EG) + rs(dQG * QG) - rs(dKD * KD)
               + jnp.where((ri & (CHUNK - 1)) == CHUNK - 1, TL, 0.0))
        dBeta = rs(dVB * V) + rs(dKB * K)
        dqkv_ref[:, 0:512] = _unst(dQ)
        dqkv_ref[:, 512:1024] = _unst(dK)
        dqkv_ref[:, 1024:1536] = _unst(dVB * B)
        dg = jnp.dot(_tri(CHUNK, upper=True), _untile(dGC), precision=HI, preferred_element_type=F32)
        a, alog_v, dtb_v = a_ref[...], alog_ref[...], dtb_ref[...]
        g = _gates(a, alog_v, dtb_v)
        lane = lax.broadcasted_iota(jnp.int32, (CHUNK, 128), 1)
        valid = lane < _H
        da = jnp.where(valid, dg * (-jnp.exp(alog_v)) * jax.nn.sigmoid(a + dtb_v), 0.0)
        dsm_ref[:, 0:128] = da.astype(dsm_ref.dtype)
        dsm_ref[:, 128:256] = jnp.where(valid, _untile(dBeta) * beta * (1.0 - beta), 0.0).astype(dsm_ref.dtype)
        acc_ref[0:1, :] += jnp.sum(jnp.where(valid, dg * g, 0.0), axis=0, keepdims=True)
        acc_ref[1:2, :] += jnp.sum(da, axis=0, keepdims=True)

    blk = lambda r, w, c: pl.BlockSpec((E, r, w), lambda b, n: (b, nC - 1 - n, c))
    par = pl.BlockSpec((1, 128), lambda b, n: (0, 0))
    in_specs = [blk(CHUNK, 512, 0), blk(CHUNK, 512, 1), blk(CHUNK, 512, 2), blk(CHUNK, 128, 6), par, par,
                blk(CHUNK, 128, 0), blk(CHUNK, 128, 0), blk(_HC, _HC, 0), blk(512, 128, 0),
                blk(CHUNK, 512, 0), blk(CHUNK, 512, 0), blk(CHUNK, 512, 0), ANY]
    out_shape = (_sds((Bl, S, 1536)), _sds((Bl, S, DCAT_W), BF), _sds((8, 128)))
    out_specs = (blk(CHUNK, 1536, 0), blk(CHUNK, 256, 3), pl.BlockSpec((8, 128), lambda b, n: (0, 0)))
    q3, s3 = qkvn.reshape(Bl, S, -1), sm.reshape(Bl, S, -1)
    res = _pc(body, name="gdn_chunk_bwd", out_shape=out_shape, grid=(Bl // E, nC), in_specs=in_specs,
              out_specs=out_specs, scratch=[pltpu.VMEM((E, 512, 128), F32)], sem=("arbitrary", "arbitrary"),
              aliases={13: 1}, xchg=xchg)(q3, q3, q3, s3, alog, dtb, gc_s, beta_s, tinv_s, sst_s, w_s, u_s,
                                          do.reshape(Bl, S, -1), dcat.reshape(Bl, S, DCAT_W))
    return (res[0].reshape(T, 1536), res[1].reshape(T, DCAT_W)) + tuple(res[2:])


def _gdn_post(o, z, g, T):
    tt = _pick(T, (2048, 1024, 512, 256))

    def body(o_ref, z_ref, g_ref, out_ref, outt_ref):
        y = (_rmsnorm(o_ref[...], g_ref[...]) * _silu(z_ref[...])).astype(out_ref.dtype)
        out_ref[...] = y
        outt_ref[...] = y.T

    blk = pl.BlockSpec((tt, 128), lambda i, h: (i, h))
    return _pc(body, name="gdn_post_fwd", out_shape=(_sds((T, 1024), BF), _sds((1024, T), BF)), grid=(T // tt, _H),
               in_specs=[blk, blk, pl.BlockSpec((1, 128), lambda i, h: (0, 0))],
               out_specs=(blk, pl.BlockSpec((128, tt), lambda i, h: (h, i))),
               sem=("parallel", "parallel"))(o, z, g)


def _gdn_post_bwd(o, z, g, dmixin, T):
    tt = _pick(T, (2048, 1024, 512, 256))

    def body(o_ref, z_ref, g_ref, d_ref, do_ref, dz_ref, dg_ref):
        @pl.when((pl.program_id(0) == 0) & (pl.program_id(1) == 0))
        def _():
            dg_ref[...] = jnp.zeros_like(dg_ref)

        _, vjp = jax.vjp(lambda a, b, c: _rmsnorm(a, c) * _silu(b), o_ref[...], z_ref[...], g_ref[...])
        do, dz, dg = vjp(d_ref[...])
        do_ref[...] = do
        dz_ref[...] = dz.astype(dz_ref.dtype)
        dg_ref[0:1, :] += dg

    blk = pl.BlockSpec((tt, 128), lambda i, h: (i, h))
    return _pc(body, name="gdn_post_bwd", out_shape=(_sds((T, 512)), _sds((T, DCAT_W), BF), _sds((8, 128))),
               grid=(T // tt, _H), in_specs=[blk, blk, pl.BlockSpec((1, 128), lambda i, h: (0, 0)), blk],
               out_specs=(blk, pl.BlockSpec((tt, 128), lambda i, h: (i, DCAT_Z // 128 + h)),
                          pl.BlockSpec((8, 128), lambda i, h: (0, 0))),
               sem=("arbitrary", "arbitrary"))(o, z, g, dmixin)


def _rms_fwd(sm, g, col_blk, name):
    T = sm.shape[0]
    d = g.shape[1]
    tt = _pick(T, (2048, 1024, 512, 256))

    def body(x_ref, g_ref, o_ref, ot_ref):
        y = _rmsnorm(x_ref[...], g_ref[...]).astype(o_ref.dtype)
        o_ref[...] = y
        ot_ref[...] = y.T

    return _pc(body, name=name, out_shape=(_sds((T, d), BF), _sds((d, T), BF)), grid=(T // tt,),
               in_specs=[pl.BlockSpec((tt, d), lambda i: (i, col_blk)), pl.BlockSpec((1, d), lambda i: (0, 0))],
               out_specs=(pl.BlockSpec((tt, d), lambda i: (i, 0)), pl.BlockSpec((d, tt), lambda i: (0, i))),
               sem=("parallel",))(sm, g)


def _rms_bwd(sm, g, dy, dsm, col_blk, name):
    T = sm.shape[0]
    d = g.shape[1]
    tt = _pick(T, (2048, 1024, 512, 256))

    def body(x_ref, g_ref, d_ref, alias_ref, dx_ref, dg_ref):
        @pl.when(pl.program_id(0) == 0)
        def _():
            dg_ref[...] = jnp.zeros_like(dg_ref)

        _, vjp = jax.vjp(_rmsnorm, x_ref[...], g_ref[...])
        dx, dg = vjp(d_ref[...])
        dx_ref[...] = dx.astype(dx_ref.dtype)
        dg_ref[0:1, :] += dg

    grp = pl.BlockSpec((tt, d), lambda i: (i, col_blk))
    return _pc(body, name=name, out_shape=(_sds(dsm.shape, dsm.dtype), _sds((8, d))), grid=(T // tt,),
               in_specs=[grp, pl.BlockSpec((1, d), lambda i: (0, 0)), pl.BlockSpec((tt, d), lambda i: (i, 0)), ANY],
               out_specs=(grp, pl.BlockSpec((8, d), lambda i: (0, 0))), aliases={3: 0},
               sem=("arbitrary",))(sm, g, dy, dsm)


def _rope_tables(S):
    inv = ROPE_THETA ** (-jnp.arange(0, MLA_ROPE, 2, dtype=F32) / MLA_ROPE)
    ang = jnp.arange(S, dtype=F32)[:, None] * inv[None, :]
    cos, sin = jnp.cos(ang), jnp.sin(ang)
    z = jnp.zeros((S, 64), F32)
    return jnp.concatenate([cos, cos, z], axis=1), jnp.concatenate([-sin, sin, z], axis=1)


def _swap_halves(x):
    lane = lax.broadcasted_iota(jnp.int32, x.shape, 1)
    return jnp.where(lane < 32, pltpu.roll(x, 96, 1), jnp.where(lane < 64, pltpu.roll(x, 32, 1), 0.0))


def _mm_rows(a, b, *, name, tm, extras, out_shape, out_specs, epi, mode="nn", sem="parallel"):
    M, K = a.shape
    nx = len(extras)

    def body(a_ref, b_ref, *rest):
        r = lax.dot_general(a_ref[...].astype(_MXU), b_ref[...].astype(_MXU), _NN if mode == "nn" else _NT,
                            preferred_element_type=F32)
        epi(r, rest[:nx], rest[nx:])

    in_specs = [pl.BlockSpec((tm, K), lambda i: (i, 0)), pl.BlockSpec(b.shape, lambda i: (0, 0))]
    return _pc(body, name=name, out_shape=out_shape, grid=(M // tm,), in_specs=in_specs + [s for _, s in extras],
               out_specs=out_specs, sem=(sem,))(a, b, *[x for x, _ in extras])


def _qup_rope(cqn, wq, cos, sin, S):
    T = cqn.shape[0]
    tm = _pick(S, (1024, 512, 256))
    nps = S // tm

    def epi(r, xs, outs):
        c, s = xs[0][...], xs[1][...]
        for h in range(MLA_HEADS):
            lo = 256 * h
            rp = r[:, lo + 128:lo + 256]
            outs[0][:, lo:lo + 128] = (r[:, lo:lo + 128] * ATTN_SCALE).astype(BF)
            outs[0][:, lo + 128:lo + 256] = ((rp * c + _swap_halves(rp) * s) * ATTN_SCALE).astype(BF)

    tab = pl.BlockSpec((tm, 128), lambda i: (i % nps, 0))
    return _mm_rows(cqn, wq, name="mm_qup_rope", tm=tm, extras=[(cos, tab), (sin, tab)],
                    out_shape=_sds((T, 1024), BF), out_specs=pl.BlockSpec((tm, 1024), lambda i: (i, 0)), epi=epi)


def _kvup_rope(ckvn, wkv, sm, cos, sin, S):
    T = ckvn.shape[0]
    tm = _pick(S, (1024, 512, 256))
    nps = S // tm

    def epi(r, xs, outs):
        k, c, s = xs[0][...], xs[1][...], xs[2][...]
        kr = (k * c + _swap_halves(k) * s).astype(BF)
        kvb = r.astype(BF)
        outs[0][...] = kvb
        for h in range(MLA_HEADS):
            lo = 256 * h
            outs[1][:, lo:lo + 128] = kvb[:, lo:lo + 128]
            outs[1][:, lo + 128:lo + 256] = kr

    tab = pl.BlockSpec((tm, 128), lambda i: (i % nps, 0))
    row = pl.BlockSpec((tm, 1024), lambda i: (i, 0))
    return _mm_rows(ckvn, wkv, name="mm_kvup_rope", tm=tm,
                    extras=[(sm, pl.BlockSpec((tm, 128), lambda i: (i, 3))), (cos, tab), (sin, tab)],
                    out_shape=(_sds((T, 1024), BF), _sds((T, 1024), BF)), out_specs=(row, row), epi=epi)


def _rope_bwd(dqc, dcat, dkrr, cos, sin, S):
    T = dqc.shape[0]
    tt = _pick(S, (2048, 1024, 512, 256))
    nps = S // tt

    def body(alias_s, dq_ref, dk_ref, c_ref, s_ref, qo_ref, ko_ref):
        c, s = c_ref[...], s_ref[...]
        d = dq_ref[:, 128:256]
        qo_ref[:, 0:128] = dq_ref[:, 0:128].astype(qo_ref.dtype)
        qo_ref[:, 128:256] = (d * c + _swap_halves(d * s)).astype(qo_ref.dtype)

        @pl.when(pl.program_id(1) == 0)
        def _():
            k = dk_ref[...]
            ko_ref[...] = (k * c + _swap_halves(k * s)).astype(ko_ref.dtype)

    tab = pl.BlockSpec((tt, 128), lambda i, h: (i % nps, 0))
    head = pl.BlockSpec((tt, 256), lambda i, h: (i, h))
    return _pc(body, name="rope_bwd", out_shape=(_sds((T, 1024), BF), _sds(dcat.shape, dcat.dtype)),
               grid=(T // tt, MLA_HEADS),
               in_specs=[ANY, head, pl.BlockSpec((tt, 128), lambda i, h: (i, 0)), tab, tab],
               out_specs=(head, pl.BlockSpec((tt, 128), lambda i, h: (i, 3))),
               aliases={0: 1}, sem=("parallel", "arbitrary"))(dcat, dqc, dkrr, cos, sin)


def _attn_scores(q, k_ref, L, tq):
    def sc(lo, hi):
        return _dot(q, k_ref[lo:hi, :], _NT)

    sd = sc(L - tq, L)
    qc = lax.broadcasted_iota(jnp.int32, sd.shape, 0) >> 6
    kc = lax.broadcasted_iota(jnp.int32, sd.shape, 1) >> 6
    sd = jnp.where(kc <= qc, sd, -jnp.inf)
    return sd if L == tq else jnp.concatenate([sc(0, L - tq), sd], axis=1)


def _attn_fwd(mixin, mixin_t, qcat, kcat, kv, Bl, S):
    T = Bl * S
    tq = _pick(S, (512, 256, 128))
    nq = S // tq

    def body(alias_ref, alias_t, q_ref, k_ref, v_ref, o_ref, lse_ref, ot_ref):
        i = pl.program_id(2)
        q = q_ref[...]
        for ii in range(nq):
            @pl.when(i == ii)
            def _(L=(ii + 1) * tq):
                s = _attn_scores(q, k_ref, L, tq)
                m = jnp.max(s, axis=-1, keepdims=True)
                e = jnp.exp(s - m)
                l = jnp.sum(e, axis=-1, keepdims=True)
                o = (_dot(e, v_ref[0:L, :], _NN) / l).astype(o_ref.dtype)
                o_ref[...] = o
                ot_ref[...] = o.T
                lse_ref[...] = jnp.broadcast_to(m + jnp.log(l), lse_ref.shape)

    qrow = lambda b, h, i: b * nq + i
    in_specs = [ANY, ANY,
                pl.BlockSpec((tq, 256), lambda b, h, i: (qrow(b, h, i), h)),
                pl.BlockSpec((S, 256), lambda b, h, i: (b, h)),
                pl.BlockSpec((S, 128), lambda b, h, i: (b, 2 * h + 1))]
    return _pc(body, name="mla_attn_fwd",
               out_shape=(_sds(mixin.shape, mixin.dtype), _sds((T, 512)), _sds(mixin_t.shape, mixin_t.dtype)),
               grid=(Bl, MLA_HEADS, nq), in_specs=in_specs,
               out_specs=(pl.BlockSpec((tq, 128), lambda b, h, i: (qrow(b, h, i), 4 + h)),
                          pl.BlockSpec((tq, 128), lambda b, h, i: (qrow(b, h, i), h)),
                          pl.BlockSpec((128, tq), lambda b, h, i: (4 + h, qrow(b, h, i)))),
               aliases={0: 0, 1: 2}, sem=("parallel", "parallel", "parallel"))(mixin, mixin_t, qcat, kcat, kv)


def _attn_bwd(qcat, kcat, kv, lse, dmixin, Bl, S):
    T = Bl * S
    tq = _pick(S, (512, 256, 128))
    nq = S // tq

    def body(q_ref, k_ref, v_ref, do_ref, lse_ref, dq_ref, dkv_ref, dkr_ref, acck_ref, accv_ref):
        h, i = pl.program_id(1), pl.program_id(2)
        q = q_ref[...]
        do = do_ref[...].astype(BF)
        lse_col = lse_ref[:, 0:1]

        @pl.when(i == 0)
        def _():
            acck_ref[...] = jnp.zeros_like(acck_ref)
            accv_ref[...] = jnp.zeros_like(accv_ref)

        for ii in range(nq):
            @pl.when(i == ii)
            def _(L=(ii + 1) * tq):
                p = jnp.exp(_attn_scores(q, k_ref, L, tq) - lse_col)
                pb = p.astype(BF)
                v = v_ref[0:L, :]
                dp = _dot(do, v, _NT)
                ds = (p * (dp - jnp.sum(dp * p, axis=-1, keepdims=True))).astype(BF)
                dq_ref[...] = _dot(ds, k_ref[0:L, :], _NN) * ATTN_SCALE
                acck_ref[0:L, :] += _dot(ds, q, _TN)
                accv_ref[0:L, :] += _dot(pb, do, _TN)

        @pl.when(i == nq - 1)
        def _():
            dkv_ref[:, 0:128] = acck_ref[:, 0:128].astype(dkv_ref.dtype)
            dkv_ref[:, 128:256] = accv_ref[...].astype(dkv_ref.dtype)
            dkr_ref[...] = jnp.where(h == 0, 0.0, dkr_ref[...]) + acck_ref[:, 128:256]

    qrow = lambda b, h, i: b * nq + i
    in_specs = [pl.BlockSpec((tq, 256), lambda b, h, i: (qrow(b, h, i), h)),
                pl.BlockSpec((S, 256), lambda b, h, i: (b, h)),
                pl.BlockSpec((S, 128), lambda b, h, i: (b, 2 * h + 1)),
                pl.BlockSpec((tq, 128), lambda b, h, i: (qrow(b, h, i), 4 + h)),
                pl.BlockSpec((tq, 128), lambda b, h, i: (qrow(b, h, i), h))]
    out_shape = (_sds((T, 1024)), _sds((T, 1024), BF), _sds((T, 128)))
    out_specs = (pl.BlockSpec((tq, 256), lambda b, h, i: (qrow(b, h, i), h)),
                 pl.BlockSpec((S, 256), lambda b, h, i: (b, h)),
                 pl.BlockSpec((S, 128), lambda b, h, i: (b, 0)))
    return _pc(body, name="mla_attn_bwd", out_shape=out_shape, grid=(Bl, MLA_HEADS, nq), in_specs=in_specs,
               out_specs=out_specs, scratch=[pltpu.VMEM((S, 256), F32), pltpu.VMEM((S, 128), F32)],
               sem=("arbitrary", "arbitrary", "arbitrary"))(qcat, kcat, kv, dmixin, lse)


def _out_ln1(mixin, w_out, x, g, b):
    T, D = x.shape
    tm = _pick(T, (1024, 512, 256))

    def epi(r, xs, outs):
        h = _layernorm(ALPHA * xs[0][...] + r, xs[1][...], xs[2][...])
        outs[0][...] = r
        outs[1][...] = h
        hb = h.astype(BF)
        outs[2][...] = hb
        outs[3][...] = hb.T

    row = pl.BlockSpec((tm, D), lambda i: (i, 0))
    par = pl.BlockSpec((1, D), lambda i: (0, 0))
    return _mm_rows(mixin, w_out, name="mm_out_ln1", tm=tm, extras=[(x, row), (g, par), (b, par)],
                    out_shape=(_sds((T, D)), _sds((T, D)), _sds((T, D), BF), _sds((D, T), BF)),
                    out_specs=(row, row, row, pl.BlockSpec((D, tm), lambda i: (0, i))), epi=epi)


def _dh1_ln1_bwd(du_u, wup_u, du_g, wup_g, dgpre, w_gate, dffn, x, mix, g, b):
    T, D = x.shape
    tm = _pick(T, (512, 256))

    def epi(r, xs, outs):
        dug_ref, wg_ref, dgp_ref, wgate_ref, dffn_ref, x_ref, mix_ref, g_ref, b_ref = xs
        acc_ref = outs[2]

        @pl.when(pl.program_id(0) == 0)
        def _():
            acc_ref[...] = jnp.zeros_like(acc_ref)

        dh = (r + _dot(dug_ref[...], wg_ref[...], _NT) + _dot(dgp_ref[...], wgate_ref[...], _NT)
              + ALPHA * dffn_ref[...])
        f = lambda xx, mm, gg, bb: _layernorm(ALPHA * xx + mm, gg, bb)
        _, vjp = jax.vjp(f, x_ref[...], mix_ref[...], g_ref[...], b_ref[...])
        dx, dm, dg, db = vjp(dh)
        outs[0][...] = dx
        outs[1][...] = dm.astype(BF)
        acc_ref[0:1, :] += dg
        acc_ref[1:2, :] += db

    row = pl.BlockSpec((tm, D), lambda i: (i, 0))
    par = pl.BlockSpec((1, D), lambda i: (0, 0))
    whole = lambda a: pl.BlockSpec(a.shape, lambda i: (0, 0))
    extras = [(du_g, pl.BlockSpec((tm, du_g.shape[1]), lambda i: (i, 0))), (wup_g, whole(wup_g)), (dgpre, row),
              (w_gate, whole(w_gate)), (dffn, row), (x, row), (mix, row), (g, par), (b, par)]
    return _mm_rows(du_u, wup_u, name="mm_dh1_ln1_bwd", tm=tm, mode="nt", sem="arbitrary", extras=extras,
                    out_shape=(_sds((T, D)), _sds((T, D), BF), _sds((8, D))),
                    out_specs=(row, row, pl.BlockSpec((8, D), lambda i: (0, 0))), epi=epi)


def _ffn_up_act_fwd(h1b, wup_g, wup_u, wg, wu, bg, bu, Bl, S):
    T, D = h1b.shape
    C = wup_g.shape[1]
    cb = _pick(C, (256, 128))
    kk = wg.shape[0]

    def body(h_ref, pg_ref, pu_ref, wg_ref, wu_ref, bg_ref, bu_ref, ug_ref, uu_ref, o_ref, ot_ref):
        h = h_ref[...]
        ug = _dot(h, pg_ref[...], _NN)
        uu = _dot(h, pu_ref[...], _NN)
        ug_ref[...] = ug
        uu_ref[...] = uu
        cg = _conv(ug, wg_ref[...]) + bg_ref[...]
        cu = _conv(uu, wu_ref[...]) + bu_ref[...]
        a = (_silu(cg) * cu).astype(o_ref.dtype)
        o_ref[...] = a
        ot_ref[...] = a.T

    blk = pl.BlockSpec((S, cb), lambda b, j: (b, j))
    pblk = pl.BlockSpec((D, cb), lambda b, j: (0, j))
    wblk = pl.BlockSpec((kk, cb), lambda b, j: (0, j))
    bblk = pl.BlockSpec((1, cb), lambda b, j: (0, j))
    return _pc(body, name="ffn_up_act_fwd", out_shape=(_sds((T, C)), _sds((T, C)), _sds((T, C), BF), _sds((C, T), BF)),
               grid=(Bl, C // cb), in_specs=[pl.BlockSpec((S, D), lambda b, j: (b, 0)), pblk, pblk, wblk, wblk, bblk, bblk],
               out_specs=(blk, blk, blk, pl.BlockSpec((cb, S), lambda b, j: (j, b))),
               sem=("parallel", "parallel"))(h1b, wup_g, wup_u, wg, wu, bg, bu)


def _ffn_act_bwd(ug, uu, wg, wu, bg, bu, dffnb, w_down, Bl, S):
    T, C = ug.shape
    D = dffnb.shape[1]
    cb = _pick(C, (256, 128))
    kk = wg.shape[0]

    def body(g_ref, u_ref, wg_ref, wu_ref, bg_ref, bu_ref, d_ref, wd_ref,
             dg_ref, du_ref, dwg_ref, dwu_ref, dbg_ref, dbu_ref):
        first = pl.program_id(1) == 0
        wgv, wuv = wg_ref[...], wu_ref[...]
        tg, tu = _taps(g_ref[...], kk), _taps(u_ref[...], kk)
        cg = _conv(None, wgv, tg) + bg_ref[...]
        cu = _conv(None, wuv, tu) + bu_ref[...]
        _, vjp = jax.vjp(lambda a, b: _silu(a) * b, cg, cu)
        dcg, dcu = vjp(_dot(d_ref[...], wd_ref[...], _NT))
        dg_ref[...] = _conv_bwd_x(dcg, wgv).astype(dg_ref.dtype)
        du_ref[...] = _conv_bwd_x(dcu, wuv).astype(du_ref.dtype)
        _conv_bwd_w(tg, dcg, dwg_ref, first)
        _conv_bwd_w(tu, dcu, dwu_ref, first)
        dbg_ref[...] = jnp.where(first, 0.0, dbg_ref[...]) + jnp.sum(dcg, axis=0, keepdims=True)
        dbu_ref[...] = jnp.where(first, 0.0, dbu_ref[...]) + jnp.sum(dcu, axis=0, keepdims=True)

    blk = pl.BlockSpec((S, cb), lambda j, b: (b, j))
    wblk = pl.BlockSpec((kk, cb), lambda j, b: (0, j))
    bblk = pl.BlockSpec((1, cb), lambda j, b: (0, j))
    out_shape = (_sds((T, C), BF), _sds((T, C), BF), _sds((kk, C)), _sds((kk, C)), _sds((1, C)), _sds((1, C)))
    return _pc(body, name="ffn_act_bwd", out_shape=out_shape, grid=(C // cb, Bl),
               in_specs=[blk, blk, wblk, wblk, bblk, bblk, pl.BlockSpec((S, D), lambda j, b: (b, 0)),
                         pl.BlockSpec((cb, D), lambda j, b: (j, 0))],
               out_specs=(blk, blk, wblk, wblk, bblk, bblk), sem=("parallel", "arbitrary"))(
        ug, uu, wg, wu, bg, bu, dffnb, w_down)


def _head(act, w_down, h1, h1b, w_gate, pb, w_proj, tgt, bg, g2, b2):
    T, D = h1.shape
    tm = _pick(T, (512, 256))

    def epi(ffn, xs, outs):
        h_ref, hb_ref, wg_ref, pb_ref, wp_ref, t_ref, bg_ref, g2_ref, b2_ref = xs
        acc_ref = outs[4]

        @pl.when(pl.program_id(0) == 0)
        def _():
            acc_ref[...] = jnp.zeros_like(acc_ref)

        h, tg = h_ref[...], t_ref[...]
        gpre = _dot(hb_ref[...], wg_ref[...], _NN)
        pp = _dot(pb_ref[...], wp_ref[...], _NN)

        def loss_fn(f, gp, p_, bgv, g2v, b2v):
            pre = ALPHA * h + f + jax.nn.sigmoid(gp + bgv) * p_
            err = _layernorm(pre, g2v, b2v) - tg
            return 0.5 * jnp.sum(jnp.mean(err * err, axis=-1, keepdims=True))

        loss, grads = jax.value_and_grad(loss_fn, argnums=(0, 1, 2, 3, 4, 5))(
            ffn, gpre, pp, bg_ref[...], g2_ref[...], b2_ref[...])
        outs[0][...] = grads[0]
        outs[1][...] = grads[0].astype(BF)
        outs[2][...] = grads[1].astype(BF)
        outs[3][...] = grads[2].astype(BF)
        acc_ref[0:1, :] += grads[3]
        acc_ref[1:2, :] += grads[4]
        acc_ref[2:3, :] += grads[5]
        acc_ref[3:4, :] += jnp.broadcast_to(loss, (1, D))

    row = pl.BlockSpec((tm, D), lambda i: (i, 0))
    par = pl.BlockSpec((1, D), lambda i: (0, 0))
    whole = lambda a: pl.BlockSpec(a.shape, lambda i: (0, 0))
    o_bf = _sds((T, D), BF)
    extras = [(h1, row), (h1b, row), (w_gate, whole(w_gate)), (pb, pl.BlockSpec((tm, pb.shape[1]), lambda i: (i, 0))),
              (w_proj, whole(w_proj)), (tgt, row), (bg, par), (g2, par), (b2, par)]
    return _mm_rows(act, w_down, name="mm_down_loss_head", tm=tm, sem="arbitrary", extras=extras,
                    out_shape=(_sds((T, D)), o_bf, o_bf, o_bf, _sds((8, D))),
                    out_specs=(row, row, row, row, pl.BlockSpec((8, D), lambda i: (0, 0))), epi=epi)


def _adamw(parts, w, m, v, name):
    R, C = w.shape
    tr = R if R <= 512 else _pick(R, (256,))

    def body(p_ref, w_ref, m_ref, v_ref, g_ref, d_ref, nm_ref, nv_ref):
        g = p_ref[0].astype(F32)
        for j in range(1, N_DEV):
            g = g + p_ref[j].astype(F32)
        g_ref[...] = g
        d_ref[...], nm_ref[...], nv_ref[...] = _adam_math(g, w_ref[...], m_ref[...], v_ref[...])

    blk = pl.BlockSpec((tr, C), lambda i: (i, 0))
    o = _sds((R, C))
    return _pc(body, name=name, out_shape=(o, o, o, o), grid=(R // tr,),
               in_specs=[pl.BlockSpec((N_DEV, tr, C), lambda i: (0, i, 0)), blk, blk, blk],
               out_specs=(blk, blk, blk, blk), sem=("parallel",))(parts, w, m, v)


def _adam_math(g, w, m, v):
    mm = ADAM_B1 * m + (1.0 - ADAM_B1) * g
    vv = ADAM_B2 * v + (1.0 - ADAM_B2) * jnp.square(g)
    m_hat = mm / (1.0 - ADAM_B1 ** ADAM_STEP)
    v_hat = vv / (1.0 - ADAM_B2 ** ADAM_STEP)
    return -ADAM_LR * (m_hat / (jnp.sqrt(v_hat) + ADAM_EPS) + ADAM_WD * w), mm, vv


def _small_layout(sizes):
    offs, off = {}, 0
    for n in SMALL_ORDER:
        offs[n] = off
        off += -(-sizes[n] // 128) * 128
    return offs, off


def _pack_small(d):
    return jnp.concatenate([_pad_lanes(d[n].reshape(1, -1), -(-d[n].size // 128) * 128) for n in SMALL_ORDER], axis=1)


def _adamw_small(parts, ws, ms, vs):
    k = len(SMALL_ORDER)
    sizes = {n: ws[n].shape[1] for n in SMALL_ORDER}
    offs, _ = _small_layout(sizes)

    def body(p_ref, *refs):
        ins, outs = refs[:3 * k], refs[3 * k:]
        for i, n in enumerate(SMALL_ORDER):
            lo, hi = offs[n], offs[n] + sizes[n]
            g = p_ref[0, :, lo:hi]
            for j in range(1, N_DEV):
                g = g + p_ref[j, :, lo:hi]
            d, mm, vv = _adam_math(g, ins[i][...], ins[k + i][...], ins[2 * k + i][...])
            outs[4 * i][...], outs[4 * i + 1][...], outs[4 * i + 2][...], outs[4 * i + 3][...] = g, d, mm, vv

    out_shape = tuple(_sds((1, sizes[n])) for n in SMALL_ORDER for _ in range(4))
    args = [ws[n] for n in SMALL_ORDER] + [ms[n] for n in SMALL_ORDER] + [vs[n] for n in SMALL_ORDER]
    res = _pc(body, name="adamw_small", out_shape=out_shape)(parts, *args)
    return {n: tuple(res[4 * i:4 * i + 4]) for i, n in enumerate(SMALL_ORDER)}


def _prep(x, p, xchg=None):
    T, D = x.shape
    Dp = p.shape[1]
    tt = _pick(T, (1024, 512, 256))

    def body(x_ref, p_ref, xb_ref, xt_ref, pb_ref, pt_ref):
        xb = x_ref[...].astype(BF)
        xb_ref[...] = xb
        xt_ref[...] = xb.T
        pb = p_ref[...].astype(BF)
        pb_ref[...] = pb
        pt_ref[...] = pb.T

    row = lambda w: pl.BlockSpec((tt, w), lambda i: (i, 0))
    col = lambda w: pl.BlockSpec((w, tt), lambda i: (0, i))
    return _pc(body, name="prep_inputs",
               out_shape=(_sds((T, D), BF), _sds((D, T), BF), _sds((T, Dp), BF), _sds((Dp, T), BF)),
               grid=(T // tt,), in_specs=[row(D), row(Dp)], out_specs=(row(D), col(D), row(Dp), col(Dp)),
               sem=("parallel",), xchg=xchg)(x, p)


SMALL_ORDER = ("gdn_a_log", "gdn_dt_bias", "gdn_norm_g", "mla_q_norm_g", "mla_kv_norm_g", "ln1_g", "ln1_b",
               "ffn_conv_b", "ple_b_gate", "ln2_g", "ln2_b")
EARLY = ("w_in", "gdn_conv_w")
LATE = ("mla_w_q_up", "mla_w_kv_up", "w_out", "ffn_w_up", "ffn_conv_w", "ffn_w_down", "ple_w_gate", "ple_w_proj")
GRADS_EARLY = ("w_out", "ffn_w_up", "ffn_conv_w", "ffn_w_down", "ple_w_gate", "ple_w_proj")
GRADS_LATE = ("w_in", "gdn_conv_w", "mla_w_q_up", "mla_w_kv_up")


def _pad_lanes(v, n=128):
    return jnp.pad(v, ((0, 0), (0, n - v.shape[1])))


def _local_step(x, p, tgt, W, sp, Bl, S, early_weights=None, late_weights=None, early_grads=None, late_grads=None):
    T = Bl * S
    W = dict(W)
    prep = _prep(x, p, xchg=None if early_weights is None else early_weights[:2])
    xb, xT, pb, pT = prep[:4]
    if early_weights is not None:
        W.update(early_weights[2](prep[4:]))
    w_in = W["w_in"]
    wqkv, wz = w_in[:, :1536], w_in[:, 1536:2048]
    z64 = jnp.zeros((w_in.shape[0], 64), w_in.dtype)
    z124 = jnp.zeros((w_in.shape[0], 124), w_in.dtype)
    wsm = jnp.concatenate([w_in[:, 2056:2440], w_in[:, 2696:2760], z64, w_in[:, 2440:2696],
                           w_in[:, 2048:2052], z124, w_in[:, 2052:2056], z124], axis=1)
    alog, dtb = _pad_lanes(sp["gdn_a_log"]), _pad_lanes(sp["gdn_dt_bias"])
    cos, sin = _rope_tables(S)

    qkv, qkvn = _gdn_proj_pre_fwd(xb, wqkv, W["gdn_conv_w"], Bl, S)
    z = _mm(xb, wz, "nn", name="mm_z")
    sm = _mm(xb, wsm, "nn", name="mm_sm")
    gdn_out = _gdn_fwd(qkvn, sm, alog, dtb, Bl, S, xchg=None if late_weights is None else late_weights[:2])
    o, tinv_s, sst_s, w_s, u_s, gc_s, beta_s = gdn_out[:7]
    if late_weights is not None:
        W.update(late_weights[2](gdn_out[7:]))
    wq = jnp.pad(W["mla_w_q_up"].reshape(-1, MLA_HEADS, 192), ((0, 0), (0, 0), (0, 64))).reshape(-1, 1024)
    wkv = W["mla_w_kv_up"]
    C = W["ffn_w_down"].shape[0]
    wup_g, wup_u = W["ffn_w_up"][:, :C], W["ffn_w_up"][:, C:]
    cw_g, cw_u = W["ffn_conv_w"][:, :C], W["ffn_conv_w"][:, C:]
    cb_g, cb_u = sp["ffn_conv_b"][:, :C], sp["ffn_conv_b"][:, C:]
    mixin, mixin_t = _gdn_post(o, z, sp["gdn_norm_g"], T)
    cqn, cqn_t = _rms_fwd(sm, sp["mla_q_norm_g"], 0, "rms_q_fwd")
    ckvn, ckvn_t = _rms_fwd(sm, sp["mla_kv_norm_g"], 2, "rms_kv_fwd")
    qcat = _qup_rope(cqn, wq, cos, sin, S)
    kv, kcat = _kvup_rope(ckvn, wkv, sm, cos, sin, S)
    mixin, lse, mixin_t = _attn_fwd(mixin, mixin_t, qcat, kcat, kv, Bl, S)
    mix, h1, h1b, h1T = _out_ln1(mixin, W["w_out"], x, sp["ln1_g"], sp["ln1_b"])
    ug, uu, act, act_t = _ffn_up_act_fwd(h1b, wup_g, wup_u, cw_g, cw_u, cb_g, cb_u, Bl, S)

    dffn, dffnb, dgpre, dpp, hacc = _head(act, W["ffn_w_down"], h1, h1b, W["ple_w_gate"], pb, W["ple_w_proj"], tgt,
                                          sp["ple_b_gate"], sp["ln2_g"], sp["ln2_b"])
    loss = hacc[3, 0]
    gW, gs = {}, {}
    gs["ple_b_gate"], gs["ln2_g"], gs["ln2_b"] = hacc[0:1], hacc[1:2], hacc[2:3]
    gW["ple_w_proj"] = _mm(pT, dpp, "nn", name="mm_dproj", out_dtype=BF)
    gW["ple_w_gate"] = _mm(h1T, dgpre, "nn", name="mm_dgate", out_dtype=BF)
    gW["ffn_w_down"] = _mm(act_t, dffnb, "nn", name="mm_ddown", out_dtype=BF)
    du_g, du_u, dcw_g, dcw_u, dcb_g, dcb_u = _ffn_act_bwd(ug, uu, cw_g, cw_u, cb_g, cb_u, dffnb, W["ffn_w_down"],
                                                          Bl, S)
    gW["ffn_conv_w"] = jnp.concatenate([dcw_g, dcw_u], axis=1)
    gs["ffn_conv_b"] = jnp.concatenate([dcb_g, dcb_u], axis=1)
    gW["ffn_w_up"] = jnp.concatenate([_mm(h1T, du_g, "nn", name="mm_dup_gate", out_dtype=BF),
                                      _mm(h1T, du_u, "nn", name="mm_dup_up", out_dtype=BF)], axis=1)
    dxa, dmix, acc1 = _dh1_ln1_bwd(du_u, wup_u, du_g, wup_g, dgpre, W["ple_w_gate"], dffn, x, mix,
                                   sp["ln1_g"], sp["ln1_b"])
    gs["ln1_g"], gs["ln1_b"] = acc1[0:1], acc1[1:2]
    gW["w_out"] = _mm(mixin_t, dmix, "nn", name="mm_dwout", out_dtype=BF)
    dmixin = _mm(dmix, W["w_out"], "nt", name="mm_dmixin")
    do, dcat, gacc = _gdn_post_bwd(o, z, sp["gdn_norm_g"], dmixin, T)
    gs["gdn_norm_g"] = gacc[0:1]
    bwd_out = _gdn_bwd(qkvn, sm, alog, dtb, gc_s, beta_s, tinv_s, sst_s, w_s, u_s, do, dcat, Bl, S,
                       xchg=None if early_grads is None else early_grads({n: gW[n] for n in GRADS_EARLY}))
    dqkvn, dcat, cacc = bwd_out[:3]
    early_recv = bwd_out[3:]
    gs["gdn_a_log"], gs["gdn_dt_bias"] = cacc[0:1, :GDN_HEADS], cacc[1:2, :GDN_HEADS]
    dcat, gW["gdn_conv_w"] = _gdn_pre_bwd(qkv, W["gdn_conv_w"], dqkvn, dcat, Bl, S)
    dqc, dkv, dkrr = _attn_bwd(qcat, kcat, kv, lse, dmixin, Bl, S)
    dqraw, dcat = _rope_bwd(dqc, dcat, dkrr, cos, sin, S)
    gwq = _mm(cqn_t, dqraw, "nn", name="mm_dwq", out_dtype=BF)
    gW["mla_w_q_up"] = gwq.reshape(-1, MLA_HEADS, 256)[:, :, :192].reshape(-1, MLA_HEADS * 192)
    gW["mla_w_kv_up"] = _mm(ckvn_t, dkv, "nn", name="mm_dwkv", out_dtype=BF)
    dcqn = _mm(dqraw, wq, "nt", name="mm_dcqn")
    dckvn = _mm(dkv, wkv, "nt", name="mm_dckvn")
    dcat, qacc = _rms_bwd(sm, sp["mla_q_norm_g"], dcqn, dcat, 0, "rms_q_bwd")
    dcat, kacc = _rms_bwd(sm, sp["mla_kv_norm_g"], dckvn, dcat, 2, "rms_kv_bwd")
    gs["mla_q_norm_g"], gs["mla_kv_norm_g"] = qacc[0:1], kacc[0:1]
    gcat = _mm(xT, dcat, "nn", name="mm_dwin", out_dtype=BF)
    gsm, gqkv, gz = gcat[:, :DCAT_QKV], gcat[:, DCAT_QKV:DCAT_Z], gcat[:, DCAT_Z:]
    gW["w_in"] = jnp.concatenate([gqkv, gz, gsm[:, 768:772], gsm[:, 896:900], gsm[:, 0:384], gsm[:, 512:768],
                                  gsm[:, 384:448]], axis=1)
    wcat = jnp.concatenate([wsm, wqkv, wz], axis=1)
    dx = _mm(dcat, wcat, "nt", name="mm_dx", add=dxa, xchg=None if late_grads is None else late_grads(gW, gs))
    late_recv = ()
    if late_grads is not None:
        dx, late_recv = dx[0], dx[1:]
    return loss, dx, gW, gs, early_recv, late_recv


COL_SHARDED = ("w_in", "mla_w_q_up", "mla_w_kv_up", "ffn_w_up", "ple_w_proj", "gdn_conv_w", "ffn_conv_w")
SHARDED = EARLY + LATE
WEIGHTS = ("w_in", "gdn_conv_w", "gdn_a_log", "gdn_dt_bias", "gdn_norm_g", "mla_q_norm_g", "mla_w_q_up",
           "mla_kv_norm_g", "mla_w_kv_up", "w_out", "ln1_g", "ln1_b", "ffn_w_up", "ffn_conv_w", "ffn_conv_b",
           "ffn_w_down", "ple_w_gate", "ple_b_gate", "ple_w_proj", "ln2_g", "ln2_b")
CONV = ("gdn_conv_w", "ffn_conv_w")


def _gathered_to_full(name, g):
    if name in COL_SHARDED:
        return jnp.transpose(g, (1, 0, 2)).reshape(g.shape[1], -1)
    return g.reshape(-1, g.shape[-1])


def _full_to_blocks(name, gfull, shard_shape):
    r, c = shard_shape
    if name in COL_SHARDED:
        return jnp.transpose(gfull.reshape(r, N_DEV, c), (1, 0, 2))
    return gfull.reshape(N_DEV, r, c)


def kernel(x, p, w_in, gdn_conv_w, gdn_a_log, gdn_dt_bias, gdn_norm_g, mla_q_norm_g, mla_w_q_up, mla_kv_norm_g, mla_w_kv_up, w_out, ln1_g, ln1_b, ffn_w_up, ffn_conv_w, ffn_conv_b, ffn_w_down, ple_w_gate, ple_b_gate, ple_w_proj, ln2_g, ln2_b, loss_target, m_w_in, m_gdn_conv_w, m_gdn_a_log, m_gdn_dt_bias, m_gdn_norm_g, m_mla_q_norm_g, m_mla_w_q_up, m_mla_kv_norm_g, m_mla_w_kv_up, m_w_out, m_ln1_g, m_ln1_b, m_ffn_w_up, m_ffn_conv_w, m_ffn_conv_b, m_ffn_w_down, m_ple_w_gate, m_ple_b_gate, m_ple_w_proj, m_ln2_g, m_ln2_b, v_w_in, v_gdn_conv_w, v_gdn_a_log, v_gdn_dt_bias, v_gdn_norm_g, v_mla_q_norm_g, v_mla_w_q_up, v_mla_kv_norm_g, v_mla_w_kv_up, v_w_out, v_ln1_g, v_ln1_b, v_ffn_w_up, v_ffn_conv_w, v_ffn_conv_b, v_ffn_w_down, v_ple_w_gate, v_ple_b_gate, v_ple_w_proj, v_ln2_g, v_ln2_b):
    loc = dict(locals())
    wts = {n: loc[n] for n in WEIGHTS}
    ms = {n: loc["m_" + n] for n in WEIGHTS}
    vs = {n: loc["v_" + n] for n in WEIGHTS}
    Bl, S, D = x.shape
    T = Bl * S

    wire = lambda n: wts[n][0] if n in CONV else wts[n][0].astype(BF)
    gather = lambda names, mode: ([wire(n) for n in names], [mode] * len(names),
                                  lambda res: {n: _gathered_to_full(n, g) for n, g in zip(names, res)})
    blocks = lambda g, names: [_full_to_blocks(n, g[n], wts[n].shape[1:]) for n in names]
    pack_early = lambda g: (blocks(g, GRADS_EARLY), ["a2a"] * len(GRADS_EARLY))
    pack_late = lambda g, gs: (blocks(g, GRADS_LATE) + [_pack_small(gs)], ["a2a"] * len(GRADS_LATE) + ["ag"])
    sp = {n: wts[n].reshape(1, -1) for n in SMALL_ORDER}

    loss, dx, gW, gs, early_recv, late_recv = _local_step(
        x.reshape(T, D), p.reshape(T, -1), loss_target.reshape(T, D), {}, sp, Bl, S,
        early_weights=gather(EARLY, "ag2"), late_weights=gather(LATE, "ag"), early_grads=pack_early,
        late_grads=pack_late)

    res = {}
    for n, parts in list(zip(GRADS_EARLY, early_recv)) + list(zip(GRADS_LATE, late_recv[:-1])):
        res[n] = tuple(t[None] for t in _adamw(parts, wts[n][0], ms[n][0], vs[n][0], "adamw_" + n))
    res.update(_adamw_small(late_recv[-1], wts, ms, vs))

    loss = lax.psum(loss, ("x", "y", "c"))
    outs = [loss, dx.reshape(Bl, S, D)]
    for k in range(4):
        outs += [res[n][k] for n in WEIGHTS]
    return tuple(outs)
```

```python
import math

import jax
import jax.numpy as jnp
from jax import lax
from jax.experimental import pallas as pl
from jax.experimental.pallas import tpu as pltpu

F32 = jnp.float32
BF = jnp.bfloat16
_MXU = jnp.bfloat16
_INTERPRET = None
_VMEM_LIMIT = 56 * 1024 * 1024
HI = lax.Precision.HIGHEST

N_DEV = 8
CHUNK = 64
GDN_HEADS = 4
GDN_DK = 128
MLA_HEADS = 4
MLA_NOPE = 128
MLA_ROPE = 64
ROPE_THETA = 10000.0
ALPHA = 2.0 ** 0.25
NORM_EPS = 1e-6
ATTN_SCALE = (MLA_NOPE + MLA_ROPE) ** -0.5
DCAT_QKV, DCAT_Z, DCAT_W = 1024, 2560, 3072
ADAM_LR, ADAM_B1, ADAM_B2, ADAM_EPS, ADAM_WD, ADAM_STEP = 0.001, 0.9, 0.999, 1e-08, 0.01, 10
MESH = pl.DeviceIdType.MESH
ANY = pl.BlockSpec(memory_space=pl.ANY)
_NN = (((1,), (0,)), ((), ()))
_NT = (((1,), (1,)), ((), ()))
_TN = (((0,), (0,)), ((), ()))


def _sds(shape, dtype=F32):
    return jax.ShapeDtypeStruct(tuple(shape), dtype)


def _pick(n, cands):
    for c in cands:
        if n % c == 0:
            return c
    return n


def _xchg_out_shapes(arrs, modes):
    return [_sds(a.shape if m == "a2a" else (N_DEV,) + a.shape, a.dtype) for a, m in zip(arrs, modes)]


def _xchg_scratch(n):
    return [pltpu.SemaphoreType.DMA((n, N_DEV - 1)), pltpu.SemaphoreType.DMA((n, N_DEV - 1)),
            pltpu.SemaphoreType.DMA((n,))]


def _xchg_plan(ins, outs, send, recv, loc, modes):
    x, y, c = lax.axis_index("x"), lax.axis_index("y"), lax.axis_index("c")
    me = 4 * x + 2 * y + c
    starts, relays, waits = [], [], []
    for ai, mode in enumerate(modes):
        src_all, out = ins[ai], outs[ai]

        def remote(k, src, dst, dev):
            return pltpu.make_async_remote_copy(src_ref=src, dst_ref=dst, send_sem=send.at[ai, k], recv_sem=recv.at[ai, k],
                                                device_id=dev, device_id_type=MESH)

        own = pltpu.make_async_copy(src_all.at[me] if mode == "a2a" else src_all, out.at[me], loc.at[ai])
        starts.append(own)
        waits.append(own.wait)
        if mode == "ag2":
            sib = (x, y, 1 - c)
            chips = [(1 - x, y), (x, 1 - y), (1 - x, 1 - y)]
            first = [remote(0, src_all, out.at[me], sib)]
            first += [remote(1 + k, src_all, out.at[me], (px, py, c)) for k, (px, py) in enumerate(chips)]
            starts += first
            waits += [cp.wait_send for cp in first]
            waits.append(remote(0, src_all, out.at[4 * x + 2 * y + 1 - c], sib).wait_recv)
            for k, (px, py) in enumerate(chips):
                same, other = out.at[4 * px + 2 * py + c], out.at[4 * px + 2 * py + 1 - c]
                relay = remote(4 + k, same, same, sib)
                relays.append((remote(1 + k, src_all, same, (px, py, c)), relay))
                waits += [relay.wait_send, remote(4 + k, src_all, other, sib).wait_recv]
            continue
        for r in range(1, N_DEV):
            px = 1 - x if (r >> 2) & 1 else x
            py = 1 - y if (r >> 1) & 1 else y
            pc = 1 - c if r & 1 else c
            cp = remote(r - 1, src_all.at[4 * px + 2 * py + pc] if mode == "a2a" else src_all, out.at[me], (px, py, pc))
            starts.append(cp)
            waits.append(cp.wait)
    return starts, relays, waits


def _pc(body, *, name, out_shape, grid=None, in_specs=None, out_specs=None, scratch=(), sem=None, aliases=None,
        xchg=None):
    kw = {}
    if _INTERPRET is not None:
        kw["interpret"] = _INTERPRET
    single = not isinstance(out_shape, (tuple, list))
    out_shape = [out_shape] if single else list(out_shape)
    if out_specs is not None:
        out_specs = [out_specs] if single else list(out_specs)
    scratch = list(scratch)
    if xchg is not None:
        xarrs, xmodes = xchg
        n_in, n_out, n_scr, nx = len(in_specs), len(out_shape), len(scratch), len(xarrs)
        single = False
        inner = body

        def body(*refs):
            ins, xins = refs[:n_in], refs[n_in:n_in + nx]
            outs = refs[n_in + nx:n_in + nx + n_out]
            xouts = refs[n_in + nx + n_out:n_in + 2 * nx + n_out]
            scr = refs[n_in + 2 * nx + n_out:n_in + 2 * nx + n_out + n_scr]
            send, recv, loc = refs[n_in + 2 * nx + n_out + n_scr:]
            first = last = None
            for d, g in enumerate(grid):
                f, l = pl.program_id(d) == 0, pl.program_id(d) == g - 1
                first, last = (f, l) if first is None else (first & f, last & l)

            @pl.when(first)
            def _():
                for cp in _xchg_plan(xins, xouts, send, recv, loc, xmodes)[0]:
                    cp.start()

            inner(*ins, *outs, *scr)

            @pl.when(last)
            def _():
                _, relays, waits = _xchg_plan(xins, xouts, send, recv, loc, xmodes)
                for arrival, relay in relays:
                    arrival.wait_recv()
                    relay.start()
                for wait in waits:
                    wait()

        in_specs = list(in_specs) + [ANY] * nx
        out_specs = out_specs + [ANY] * nx
        out_shape = out_shape + _xchg_out_shapes(xarrs, xmodes)
        scratch = scratch + _xchg_scratch(nx)
        sem = ("arbitrary",) * len(grid)
    if grid is not None:
        kw["grid"] = grid
    if in_specs is not None:
        kw["in_specs"] = in_specs
    if out_specs is not None:
        kw["out_specs"] = out_specs[0] if single else tuple(out_specs)
    cp = dict(vmem_limit_bytes=_VMEM_LIMIT)
    if sem is not None:
        cp["dimension_semantics"] = sem
    call = pl.pallas_call(body, name=name, out_shape=out_shape[0] if single else tuple(out_shape),
                          scratch_shapes=scratch, input_output_aliases=aliases or {},
                          compiler_params=pltpu.CompilerParams(**cp), **kw)
    if xchg is None:
        return call
    return lambda *ins: call(*ins, *xchg[0])


_FULL_K_MAX = 3072


def _mm(a, b, mode, *, name, add=None, add_scale=1.0, out_dtype=F32, xchg=None):
    (M, K) = a.shape
    (K2, N) = b.shape if mode == "nn" else b.shape[::-1]
    assert K == K2, (a.shape, b.shape, mode)
    tm = _pick(M, (1024, 1408, 512, 384, 256, 128))
    tn = _pick(N, (1024, 2816, 1408, 768, 512, 384, 256, 128))
    tk = K if K <= _FULL_K_MAX else _pick(K, (2048, 1408, 1024, 512) if tn <= 1408 else (1024, 512))
    nk = K // tk
    dims = _NN if mode == "nn" else _NT
    has_add = add is not None

    def finish(r, add_ref, o_ref):
        if has_add:
            r = r + add_scale * add_ref[...].astype(F32)
        o_ref[...] = r.astype(out_dtype)

    if nk == 1:
        def body(a_ref, b_ref, *rest):
            r = lax.dot_general(a_ref[...].astype(_MXU), b_ref[...].astype(_MXU), dims, preferred_element_type=F32)
            finish(r, rest[0] if has_add else None, rest[-1])

        a_spec = pl.BlockSpec((tm, K), lambda i, j: (i, 0))
        b_spec = (pl.BlockSpec((K, tn), lambda i, j: (0, j)) if mode == "nn"
                  else pl.BlockSpec((tn, K), lambda i, j: (j, 0)))
        o_spec = pl.BlockSpec((tm, tn), lambda i, j: (i, j))
        grid, sem, scratch = (M // tm, N // tn), ("parallel", "parallel"), []
    else:
        def body(a_ref, b_ref, *rest):
            o_ref, acc = rest[-2], rest[-1]
            k = pl.program_id(2)

            @pl.when(k == 0)
            def _():
                acc[...] = jnp.zeros_like(acc)

            acc[...] += lax.dot_general(a_ref[...].astype(_MXU), b_ref[...].astype(_MXU), dims,
                                        preferred_element_type=F32)

            @pl.when(k == nk - 1)
            def _():
                finish(acc[...], rest[0] if has_add else None, o_ref)

        a_spec = pl.BlockSpec((tm, tk), lambda i, j, k: (i, k))
        b_spec = (pl.BlockSpec((tk, tn), lambda i, j, k: (k, j)) if mode == "nn"
                  else pl.BlockSpec((tn, tk), lambda i, j, k: (j, k)))
        o_spec = pl.BlockSpec((tm, tn), lambda i, j, k: (i, j))
        grid, sem, scratch = (M // tm, N // tn, nk), ("parallel", "parallel", "arbitrary"), [pltpu.VMEM((tm, tn), F32)]

    ins = [a, b] + ([add] if has_add else [])
    specs = [a_spec, b_spec] + ([o_spec] if has_add else [])
    return _pc(body, name=name, out_shape=_sds((M, N), out_dtype), grid=grid, in_specs=specs, out_specs=o_spec,
               scratch=scratch, sem=sem, xchg=xchg)(*ins)


def _dot(a, b, dims):
    return lax.dot_general(a.astype(_MXU), b.astype(_MXU), dims, preferred_element_type=F32)


def _split(a):
    hi = a.astype(BF)
    lo = (a - hi.astype(F32)).astype(BF)
    return hi, lo


def _dot3(a, b, dims):
    ah, al = _split(a)
    bh, bl = _split(b)
    d = lambda u, v: lax.dot_general(u, v, dims, preferred_element_type=F32)
    return d(ah, bh) + (d(ah, bl) + d(al, bh))


def _softplus(x):
    return jnp.maximum(x, 0.0) + jnp.log1p(jnp.exp(-jnp.abs(x)))


def _silu(x):
    return x * jax.nn.sigmoid(x)


def _rmsnorm(x, g):
    return x * lax.rsqrt(jnp.mean(x * x, axis=-1, keepdims=True) + NORM_EPS) * g


def _layernorm(x, g, b):
    mu = jnp.mean(x, axis=-1, keepdims=True)
    xc = x - mu
    var = jnp.mean(xc * xc, axis=-1, keepdims=True)
    return xc * lax.rsqrt(var + NORM_EPS) * g + b


def _shift_dn(x, s):
    if s == 0:
        return x
    t = lax.broadcasted_iota(jnp.int32, x.shape, 0)
    return jnp.where(t >= s, pltpu.roll(x, s, 0), 0.0)


def _shift_up(x, s):
    if s == 0:
        return x
    n = x.shape[0]
    t = lax.broadcasted_iota(jnp.int32, x.shape, 0)
    return jnp.where(t < n - s, pltpu.roll(x, n - s, 0), 0.0)


def _taps(x, kk):
    return [_shift_dn(x, kk - 1 - j) for j in range(kk)]


def _conv(x, w, taps=None):
    kk = w.shape[0]
    taps = _taps(x, kk) if taps is None else taps
    y = w[kk - 1:kk, :] * taps[kk - 1]
    for j in range(kk - 1):
        y = y + w[j:j + 1, :] * taps[j]
    return y


def _conv_bwd_x(dy, w):
    kk = w.shape[0]
    dx = w[kk - 1:kk, :] * dy
    for j in range(kk - 1):
        dx = dx + w[j:j + 1, :] * _shift_up(dy, kk - 1 - j)
    return dx


def _conv_bwd_w(taps, dy, dw_ref, first):
    kk = dw_ref.shape[0]
    for j in range(kk):
        r = jnp.sum(dy * taps[j], axis=0, keepdims=True)
        prev = jnp.where(first, 0.0, dw_ref[j:j + 1, :])
        dw_ref[j:j + 1, :] = prev + r


def _gdn_post_conv(c, j):
    h = _silu(c)
    hn = h * lax.rsqrt(jnp.sum(h * h, axis=-1, keepdims=True) + NORM_EPS)
    return jnp.where(j < GDN_HEADS, hn * (GDN_DK ** -0.5), jnp.where(j < 2 * GDN_HEADS, hn, h))


def _gdn_proj_pre_fwd(xb, wqkv, conv_w, Bl, S):
    T, D = xb.shape
    C = wqkv.shape[1]
    nj = C // 256

    def body(x_ref, p_ref, w_ref, qkv_ref, o_ref):
        j = pl.program_id(1)
        qkv = _dot(x_ref[...], p_ref[...], _NN)
        qkv_ref[...] = qkv
        c = _conv(qkv, w_ref[...])
        for half in range(2):
            o_ref[:, 128 * half:128 * (half + 1)] = _gdn_post_conv(c[:, 128 * half:128 * (half + 1)], 2 * j)

    blk = pl.BlockSpec((S, 256), lambda b, j: (b, j))
    return _pc(body, name="gdn_proj_pre_fwd", out_shape=(_sds((T, C)), _sds((T, C))), grid=(Bl, nj),
               in_specs=[pl.BlockSpec((S, D), lambda b, j: (b, 0)), pl.BlockSpec((D, 256), lambda b, j: (0, j)),
                         pl.BlockSpec((conv_w.shape[0], 256), lambda b, j: (0, j))],
               out_specs=(blk, blk), sem=("parallel", "parallel"))(xb, wqkv, conv_w)


def _gdn_pre_bwd(qkv, conv_w, dout, dcat, Bl, S):
    T, C = qkv.shape
    nj = C // 128
    kk = conv_w.shape[0]

    def body(x_ref, w_ref, d_ref, alias_ref, dx_ref, dw_ref):
        j, b = pl.program_id(0), pl.program_id(1)
        w = w_ref[...]
        taps = _taps(x_ref[...], kk)
        c = _conv(None, w, taps)
        _, vjp = jax.vjp(lambda u: _gdn_post_conv(u, j), c)
        (dc,) = vjp(d_ref[...])
        dx_ref[...] = _conv_bwd_x(dc, w).astype(dx_ref.dtype)
        _conv_bwd_w(taps, dc, dw_ref, b == 0)

    blk = pl.BlockSpec((S, 128), lambda j, b: (b, j))
    wblk = pl.BlockSpec((kk, 128), lambda j, b: (0, j))
    return _pc(body, name="gdn_pre_bwd", out_shape=(_sds(dcat.shape, dcat.dtype), _sds((kk, C))), grid=(nj, Bl),
               in_specs=[blk, wblk, blk, ANY],
               out_specs=(pl.BlockSpec((S, 128), lambda j, b: (b, DCAT_QKV // 128 + j)), wblk),
               aliases={3: 0}, sem=("parallel", "arbitrary"))(qkv, conv_w, dout, dcat)


_H = GDN_HEADS
_HC = _H * CHUNK


def _st(x):
    return jnp.concatenate([x[:, h * 128:(h + 1) * 128] for h in range(_H)], axis=0)


def _unst(x):
    return jnp.concatenate([x[h * CHUNK:(h + 1) * CHUNK] for h in range(_H)], axis=1)


def _stc(t):
    return jnp.concatenate([t[:, h:h + 1] for h in range(_H)], axis=0)


def _untile(col):
    lane = lax.broadcasted_iota(jnp.int32, (CHUNK, 128), 1)
    out = jnp.zeros((CHUNK, 128), F32)
    for h in range(_H):
        out = out + jnp.where(lane == h, col[h * CHUNK:(h + 1) * CHUNK], 0.0)
    return out


def _rowform(col):
    return jnp.broadcast_to(col, (_HC, 128)).T[0:1, :]


def _tri(n, upper=False):
    i = lax.broadcasted_iota(jnp.int32, (n, n), 0)
    j = lax.broadcasted_iota(jnp.int32, (n, n), 1)
    return jnp.where((j >= i) if upper else (j <= i), 1.0, 0.0).astype(F32)


def _gdn_chunk_common(q, k, v, gc, beta):
    Q, K, V = _st(q), _st(k), _st(v)
    B, GC = _stc(beta), _stc(gc)
    GL = jnp.concatenate([jnp.broadcast_to(gc[CHUNK - 1:CHUNK, h:h + 1], (CHUNK, 1)) for h in range(_H)], axis=0)
    ii = lax.broadcasted_iota(jnp.int32, (_HC, _HC), 0)
    jj = lax.broadcasted_iota(jnp.int32, (_HC, _HC), 1)
    same = (ii >> 6) == (jj >> 6)
    incl = same & (ii >= jj)
    strict = same & (ii > jj)
    diff = GC - _rowform(GC)
    D = jnp.where(incl, jnp.exp(jnp.where(incl, diff, 0.0)), 0.0)
    KB = K * B
    A = jnp.where(strict, _dot(KB, K, _NT) * D, 0.0)
    P = jnp.where(incl, _dot(Q, K, _NT) * D, 0.0)
    EG = jnp.exp(GC)
    ED = jnp.exp(GL - GC)
    return dict(Q=Q, K=K, V=V, B=B, GC=GC, GL=GL, incl=incl, strict=strict, D=D, KB=KB, A=A, P=P, EG=EG, ED=ED,
                QG=Q * EG, KD=K * ED)


def _neumann_inv(A):
    n = A.shape[0]
    i = lax.broadcasted_iota(jnp.int32, (n, n), 0)
    j = lax.broadcasted_iota(jnp.int32, (n, n), 1)
    N = -A
    acc = jnp.where(i == j, 1.0, 0.0) + N
    Pw = N
    for _ in range(5):
        Pw = _dot3(Pw, Pw, _NN)
        acc = acc + _dot3(acc, Pw, _NN)
    return acc


def _gates(a, alog, dtb):
    return -jnp.exp(alog) * _softplus(a + dtb)


def _hs(x, h, n=CHUNK):
    return x[h * n:(h + 1) * n]


def _per_step(Bl):
    return 4 if Bl % 4 == 0 else 2 if Bl % 2 == 0 else 1


def _gdn_fwd(qkvn, sm, alog, dtb, Bl, S, xchg=None):
    T = Bl * S
    nC = S // CHUNK
    E = _per_step(Bl)

    def body(*refs):
        S_ref = refs[-1]

        @pl.when(pl.program_id(1) == 0)
        def _():
            S_ref[...] = jnp.zeros_like(S_ref)

        for e in range(E):
            one(*[r if i in (5, 6) else r.at[e] for i, r in enumerate(refs)])

    def one(q_ref, k_ref, v_ref, a_ref, b_ref, alog_ref, dtb_ref,
            o_ref, tinv_ref, sst_ref, w_ref, u_ref, gc_ref, beta_ref, S_ref):
        g = _gates(a_ref[...], alog_ref[...], dtb_ref[...])
        beta = jax.nn.sigmoid(b_ref[...])
        gc = jnp.dot(_tri(CHUNK), g, precision=HI, preferred_element_type=F32)
        cm = _gdn_chunk_common(q_ref[...], k_ref[...], v_ref[...], gc, beta)
        tinv = _neumann_inv(cm["A"])
        WU = _dot3(tinv, jnp.concatenate([cm["KB"] * cm["EG"], cm["V"] * cm["B"]], axis=1), _NN)
        W, U = WU[:, :128], WU[:, 128:]
        s_old = [S_ref[h * 128:(h + 1) * 128, :] for h in range(_H)]
        vn = [_hs(U, h) - _dot(_hs(W, h), s_old[h], _NN) for h in range(_H)]
        o_intra = _dot(cm["P"], jnp.concatenate(vn, axis=0), _NN)
        outs = []
        for h in range(_H):
            outs.append(_dot(_hs(cm["QG"], h), s_old[h], _NN) + _hs(o_intra, h))
            gl = jnp.exp(gc[CHUNK - 1:CHUNK, h:h + 1])
            S_ref[h * 128:(h + 1) * 128, :] = gl * s_old[h] + _dot(_hs(cm["KD"], h), vn[h], _TN)
            sst_ref[h * 128:(h + 1) * 128, :] = s_old[h]
        o_ref[...] = jnp.concatenate(outs, axis=1)
        tinv_ref[...] = tinv
        w_ref[...] = _unst(W)
        u_ref[...] = _unst(U)
        gc_ref[...] = gc
        beta_ref[...] = beta

    blk = lambda r, w, c: pl.BlockSpec((E, r, w), lambda b, n: (b, n, c))
    par = pl.BlockSpec((1, 128), lambda b, n: (0, 0))
    out_shape = (_sds((Bl, S, 512)), _sds((Bl, nC * _HC, _HC)), _sds((Bl, nC * 512, 128)),
                 _sds((Bl, S, 512)), _sds((Bl, S, 512)), _sds((Bl, S, 128)), _sds((Bl, S, 128)))
    out_specs = (blk(CHUNK, 512, 0), blk(_HC, _HC, 0), blk(512, 128, 0), blk(CHUNK, 512, 0), blk(CHUNK, 512, 0),
                 blk(CHUNK, 128, 0), blk(CHUNK, 128, 0))
    q3, s3 = qkvn.reshape(Bl, S, -1), sm.reshape(Bl, S, -1)
    res = _pc(body, name="gdn_chunk_fwd", out_shape=out_shape, grid=(Bl // E, nC),
              in_specs=[blk(CHUNK, 512, 0), blk(CHUNK, 512, 1), blk(CHUNK, 512, 2), blk(CHUNK, 128, 6),
                        blk(CHUNK, 128, 7), par, par],
              out_specs=out_specs, scratch=[pltpu.VMEM((E, 512, 128), F32)], sem=("parallel", "arbitrary"),
              xchg=xchg)(q3, q3, q3, s3, s3, alog, dtb)
    return (res[0].reshape(T, 512),) + tuple(res[1:])


def _gdn_bwd(qkvn, sm, alog, dtb, gc_s, beta_s, tinv_s, sst_s, w_s, u_s, do, dcat, Bl, S, xchg=None):
    T = Bl * S
    nC = S // CHUNK
    E = _per_step(Bl)

    def body(*refs):
        acc_ref, dS_ref = refs[-2], refs[-1]
        first_chunk = pl.program_id(1) == 0

        @pl.when(first_chunk)
        def _():
            dS_ref[...] = jnp.zeros_like(dS_ref)

        @pl.when(first_chunk & (pl.program_id(0) == 0))
        def _():
            acc_ref[...] = jnp.zeros_like(acc_ref)

        for e in range(E):
            one(*[r if i in (4, 5, 13, 16) else r.at[e] for i, r in enumerate(refs)])

    def one(q_ref, k_ref, v_ref, a_ref, alog_ref, dtb_ref, gc_ref, beta_ref, tinv_ref, sst_ref, w_ref, u_ref,
            do_ref, alias_ref, dqkv_ref, dsm_ref, acc_ref, dS_ref):
        gc, beta = gc_ref[...], beta_ref[...]
        cm = _gdn_chunk_common(q_ref[...], k_ref[...], v_ref[...], gc, beta)
        Q, K, V, B, D, KB, A, P = (cm[n] for n in ("Q", "K", "V", "B", "D", "KB", "A", "P"))
        EG, ED, QG, KD = cm["EG"], cm["ED"], cm["QG"], cm["KD"]
        tinv = tinv_ref[...]
        W, U, DO = _st(w_ref[...]), _st(u_ref[...]), _st(do_ref[...])
        s_old = [sst_ref[h * 128:(h + 1) * 128, :] for h in range(_H)]
        ds_new = [dS_ref[h * 128:(h + 1) * 128, :] for h in range(_H)]
        VN = jnp.concatenate([_hs(U, h) - _dot(_hs(W, h), s_old[h], _NN) for h in range(_H)], axis=0)
        dP = jnp.where(cm["incl"], _dot(DO, VN, _NT), 0.0)
        dVN0 = _dot(P, DO, _TN)
        dVN, dQG, dKD, dW, TL = [], [], [], [], []
        for h in range(_H):
            gl = jnp.exp(gc[CHUNK - 1:CHUNK, h:h + 1])
            dvn = _hs(dVN0, h) + _dot(_hs(KD, h), ds_new[h], _NN)
            dkd = _dot(_hs(VN, h), ds_new[h], _NT)
            dVN.append(dvn)
            dQG.append(_dot(_hs(DO, h), s_old[h], _NT))
            dKD.append(dkd)
            dW.append(-_dot(dvn, s_old[h], _NT))
            dgl = jnp.sum(jnp.sum(ds_new[h] * s_old[h], axis=1, keepdims=True), axis=0, keepdims=True)
            tl = jnp.sum(jnp.sum(dkd * _hs(KD, h), axis=1, keepdims=True), axis=0, keepdims=True) + dgl * gl
            TL.append(jnp.broadcast_to(tl, (CHUNK, 1)))
            dS_ref[h * 128:(h + 1) * 128, :] = (_dot(_hs(QG, h), _hs(DO, h), _TN) + gl * ds_new[h]
                                                - _dot(_hs(W, h), dvn, _TN))
        dVN, dQG, dKD, dW, TL = (jnp.concatenate(z, axis=0) for z in (dVN, dQG, dKD, dW, TL))
        dB = _dot3(tinv, jnp.concatenate([dVN, dW], axis=1), _TN)
        dVB, dKBE = dB[:, :128], dB[:, 128:]
        dA = jnp.where(cm["strict"], -_dot3(dB, jnp.concatenate([U, W], axis=1), _NT), 0.0)
        dG = dA * D
        dQK = dP * D
        dKB = _dot(dG, K, _NN) + dKBE * EG
        dK = (_dot(jnp.concatenate([dG, dQK], axis=0), jnp.concatenate([KB, Q], axis=0), _TN)
              + dKB * B + dKD * ED)
        dQ = _dot(dQK, K, _NN) + dQG * EG
        Mx = dA * A + dP * P
        rs = lambda z: jnp.sum(z, axis=1, keepdims=True)
        ri = lax.broadcasted_iota(jnp.int32, (_HC, 1), 0)
        dGC = (rs(Mx) - rs(Mx.T) + rs(dKBE * KB * EG) + rs(dQG * QG) - rs(dKD * KD)
               + jnp.where((ri & (CHUNK - 1)) == CHUNK - 1, TL, 0.0))
        dBeta = rs(dVB * V) + rs(dKB * K)
        dqkv_ref[:, 0:512] = _unst(dQ)
        dqkv_ref[:, 512:1024] = _unst(dK)
        dqkv_ref[:, 1024:1536] = _unst(dVB * B)
        dg = jnp.dot(_tri(CHUNK, upper=True), _untile(dGC), precision=HI, preferred_element_type=F32)
        a, alog_v, dtb_v = a_ref[...], alog_ref[...], dtb_ref[...]
        g = _gates(a, alog_v, dtb_v)
        lane = lax.broadcasted_iota(jnp.int32, (CHUNK, 128), 1)
        valid = lane < _H
        da = jnp.where(valid, dg * (-jnp.exp(alog_v)) * jax.nn.sigmoid(a + dtb_v), 0.0)
        dsm_ref[:, 0:128] = da.astype(dsm_ref.dtype)
        dsm_ref[:, 128:256] = jnp.where(valid, _untile(dBeta) * beta * (1.0 - beta), 0.0).astype(dsm_ref.dtype)
        acc_ref[0:1, :] += jnp.sum(jnp.where(valid, dg * g, 0.0), axis=0, keepdims=True)
        acc_ref[1:2, :] += jnp.sum(da, axis=0, keepdims=True)

    blk = lambda r, w, c: pl.BlockSpec((E, r, w), lambda b, n: (b, nC - 1 - n, c))
    par = pl.BlockSpec((1, 128), lambda b, n: (0, 0))
    in_specs = [blk(CHUNK, 512, 0), blk(CHUNK, 512, 1), blk(CHUNK, 512, 2), blk(CHUNK, 128, 6), par, par,
                blk(CHUNK, 128, 0), blk(CHUNK, 128, 0), blk(_HC, _HC, 0), blk(512, 128, 0),
                blk(CHUNK, 512, 0), blk(CHUNK, 512, 0), blk(CHUNK, 512, 0), ANY]
    out_shape = (_sds((Bl, S, 1536)), _sds((Bl, S, DCAT_W), BF), _sds((8, 128)))
    out_specs = (blk(CHUNK, 1536, 0), blk(CHUNK, 256, 3), pl.BlockSpec((8, 128), lambda b, n: (0, 0)))
    q3, s3 = qkvn.reshape(Bl, S, -1), sm.reshape(Bl, S, -1)
    res = _pc(body, name="gdn_chunk_bwd", out_shape=out_shape, grid=(Bl // E, nC), in_specs=in_specs,
              out_specs=out_specs, scratch=[pltpu.VMEM((E, 512, 128), F32)], sem=("arbitrary", "arbitrary"),
              aliases={13: 1}, xchg=xchg)(q3, q3, q3, s3, alog, dtb, gc_s, beta_s, tinv_s, sst_s, w_s, u_s,
                                          do.reshape(Bl, S, -1), dcat.reshape(Bl, S, DCAT_W))
    return (res[0].reshape(T, 1536), res[1].reshape(T, DCAT_W)) + tuple(res[2:])


def _gdn_post(o, z, g, T):
    tt = _pick(T, (2048, 1024, 512, 256))

    def body(o_ref, z_ref, g_ref, out_ref, outt_ref):
        y = (_rmsnorm(o_ref[...], g_ref[...]) * _silu(z_ref[...])).astype(out_ref.dtype)
        out_ref[...] = y
        outt_ref[...] = y.T

    blk = pl.BlockSpec((tt, 128), lambda i, h: (i, h))
    return _pc(body, name="gdn_post_fwd", out_shape=(_sds((T, 1024), BF), _sds((1024, T), BF)), grid=(T // tt, _H),
               in_specs=[blk, blk, pl.BlockSpec((1, 128), lambda i, h: (0, 0))],
               out_specs=(blk, pl.BlockSpec((128, tt), lambda i, h: (h, i))),
               sem=("parallel", "parallel"))(o, z, g)


def _gdn_post_bwd(o, z, g, dmixin, T):
    tt = _pick(T, (2048, 1024, 512, 256))

    def body(o_ref, z_ref, g_ref, d_ref, do_ref, dz_ref, dg_ref):
        @pl.when((pl.program_id(0) == 0) & (pl.program_id(1) == 0))
        def _():
            dg_ref[...] = jnp.zeros_like(dg_ref)

        _, vjp = jax.vjp(lambda a, b, c: _rmsnorm(a, c) * _silu(b), o_ref[...], z_ref[...], g_ref[...])
        do, dz, dg = vjp(d_ref[...])
        do_ref[...] = do
        dz_ref[...] = dz.astype(dz_ref.dtype)
        dg_ref[0:1, :] += dg

    blk = pl.BlockSpec((tt, 128), lambda i, h: (i, h))
    return _pc(body, name="gdn_post_bwd", out_shape=(_sds((T, 512)), _sds((T, DCAT_W), BF), _sds((8, 128))),
               grid=(T // tt, _H), in_specs=[blk, blk, pl.BlockSpec((1, 128), lambda i, h: (0, 0)), blk],
               out_specs=(blk, pl.BlockSpec((tt, 128), lambda i, h: (i, DCAT_Z // 128 + h)),
                          pl.BlockSpec((8, 128), lambda i, h: (0, 0))),
               sem=("arbitrary", "arbitrary"))(o, z, g, dmixin)


def _rms_fwd(sm, g, col_blk, name):
    T = sm.shape[0]
    d = g.shape[1]
    tt = _pick(T, (1024, 512, 256))

    def body(x_ref, g_ref, o_ref, ot_ref):
        y = _rmsnorm(x_ref[...], g_ref[...]).astype(o_ref.dtype)
        o_ref[...] = y
        ot_ref[...] = y.T

    return _pc(body, name=name, out_shape=(_sds((T, d), BF), _sds((d, T), BF)), grid=(T // tt,),
               in_specs=[pl.BlockSpec((tt, d), lambda i: (i, col_blk)), pl.BlockSpec((1, d), lambda i: (0, 0))],
               out_specs=(pl.BlockSpec((tt, d), lambda i: (i, 0)), pl.BlockSpec((d, tt), lambda i: (0, i))),
               sem=("parallel",))(sm, g)


def _rms_bwd(sm, g, dy, dsm, col_blk, name):
    T = sm.shape[0]
    d = g.shape[1]
    tt = _pick(T, (1024, 512, 256))

    def body(x_ref, g_ref, d_ref, alias_ref, dx_ref, dg_ref):
        @pl.when(pl.program_id(0) == 0)
        def _():
            dg_ref[...] = jnp.zeros_like(dg_ref)

        _, vjp = jax.vjp(_rmsnorm, x_ref[...], g_ref[...])
        dx, dg = vjp(d_ref[...])
        dx_ref[...] = dx.astype(dx_ref.dtype)
        dg_ref[0:1, :] += dg

    grp = pl.BlockSpec((tt, d), lambda i: (i, col_blk))
    return _pc(body, name=name, out_shape=(_sds(dsm.shape, dsm.dtype), _sds((8, d))), grid=(T // tt,),
               in_specs=[grp, pl.BlockSpec((1, d), lambda i: (0, 0)), pl.BlockSpec((tt, d), lambda i: (i, 0)), ANY],
               out_specs=(grp, pl.BlockSpec((8, d), lambda i: (0, 0))), aliases={3: 0},
               sem=("arbitrary",))(sm, g, dy, dsm)


def _rope_tables(S):
    inv = ROPE_THETA ** (-jnp.arange(0, MLA_ROPE, 2, dtype=F32) / MLA_ROPE)
    ang = jnp.arange(S, dtype=F32)[:, None] * inv[None, :]
    cos, sin = jnp.cos(ang), jnp.sin(ang)
    z = jnp.zeros((S, 64), F32)
    return jnp.concatenate([cos, cos, z], axis=1), jnp.concatenate([-sin, sin, z], axis=1)


def _swap_halves(x):
    lane = lax.broadcasted_iota(jnp.int32, x.shape, 1)
    return jnp.where(lane < 32, pltpu.roll(x, 96, 1), jnp.where(lane < 64, pltpu.roll(x, 32, 1), 0.0))


def _mm_rows(a, b, *, name, tm, extras, out_shape, out_specs, epi, mode="nn", sem="parallel"):
    M, K = a.shape
    nx = len(extras)

    def body(a_ref, b_ref, *rest):
        r = lax.dot_general(a_ref[...].astype(_MXU), b_ref[...].astype(_MXU), _NN if mode == "nn" else _NT,
                            preferred_element_type=F32)
        epi(r, rest[:nx], rest[nx:])

    in_specs = [pl.BlockSpec((tm, K), lambda i: (i, 0)), pl.BlockSpec(b.shape, lambda i: (0, 0))]
    return _pc(body, name=name, out_shape=out_shape, grid=(M // tm,), in_specs=in_specs + [s for _, s in extras],
               out_specs=out_specs, sem=(sem,))(a, b, *[x for x, _ in extras])


def _qup_rope(cqn, wq, cos, sin, S):
    T = cqn.shape[0]
    tm = _pick(S, (1024, 512, 256))
    nps = S // tm

    def epi(r, xs, outs):
        c, s = xs[0][...], xs[1][...]
        for h in range(MLA_HEADS):
            lo = 256 * h
            rp = r[:, lo + 128:lo + 256]
            outs[0][:, lo:lo + 128] = (r[:, lo:lo + 128] * ATTN_SCALE).astype(BF)
            outs[0][:, lo + 128:lo + 256] = ((rp * c + _swap_halves(rp) * s) * ATTN_SCALE).astype(BF)

    tab = pl.BlockSpec((tm, 128), lambda i: (i % nps, 0))
    return _mm_rows(cqn, wq, name="mm_qup_rope", tm=tm, extras=[(cos, tab), (sin, tab)],
                    out_shape=_sds((T, 1024), BF), out_specs=pl.BlockSpec((tm, 1024), lambda i: (i, 0)), epi=epi)


def _kvup_rope(ckvn, wkv, sm, cos, sin, S):
    T = ckvn.shape[0]
    tm = _pick(S, (1024, 512, 256))
    nps = S // tm

    def epi(r, xs, outs):
        k, c, s = xs[0][...], xs[1][...], xs[2][...]
        kr = (k * c + _swap_halves(k) * s).astype(BF)
        kvb = r.astype(BF)
        outs[0][...] = kvb
        for h in range(MLA_HEADS):
            lo = 256 * h
            outs[1][:, lo:lo + 128] = kvb[:, lo:lo + 128]
            outs[1][:, lo + 128:lo + 256] = kr

    tab = pl.BlockSpec((tm, 128), lambda i: (i % nps, 0))
    row = pl.BlockSpec((tm, 1024), lambda i: (i, 0))
    return _mm_rows(ckvn, wkv, name="mm_kvup_rope", tm=tm,
                    extras=[(sm, pl.BlockSpec((tm, 128), lambda i: (i, 3))), (cos, tab), (sin, tab)],
                    out_shape=(_sds((T, 1024), BF), _sds((T, 1024), BF)), out_specs=(row, row), epi=epi)


def _rope_bwd(dqc, dcat, dkrr, cos, sin, S):
    T = dqc.shape[0]
    tt = _pick(S, (2048, 1024, 512, 256))
    nps = S // tt

    def body(alias_s, dq_ref, dk_ref, c_ref, s_ref, qo_ref, ko_ref):
        c, s = c_ref[...], s_ref[...]
        d = dq_ref[:, 128:256]
        qo_ref[:, 0:128] = dq_ref[:, 0:128].astype(qo_ref.dtype)
        qo_ref[:, 128:256] = (d * c + _swap_halves(d * s)).astype(qo_ref.dtype)

        @pl.when(pl.program_id(1) == 0)
        def _():
            k = dk_ref[...]
            ko_ref[...] = (k * c + _swap_halves(k * s)).astype(ko_ref.dtype)

    tab = pl.BlockSpec((tt, 128), lambda i, h: (i % nps, 0))
    head = pl.BlockSpec((tt, 256), lambda i, h: (i, h))
    return _pc(body, name="rope_bwd", out_shape=(_sds((T, 1024), BF), _sds(dcat.shape, dcat.dtype)),
               grid=(T // tt, MLA_HEADS),
               in_specs=[ANY, head, pl.BlockSpec((tt, 128), lambda i, h: (i, 0)), tab, tab],
               out_specs=(head, pl.BlockSpec((tt, 128), lambda i, h: (i, 3))),
               aliases={0: 1}, sem=("parallel", "arbitrary"))(dcat, dqc, dkrr, cos, sin)


def _attn_scores(q, k_ref, L, tq):
    def sc(lo, hi):
        return _dot(q, k_ref[lo:hi, :], _NT)

    sd = sc(L - tq, L)
    qc = lax.broadcasted_iota(jnp.int32, sd.shape, 0) >> 6
    kc = lax.broadcasted_iota(jnp.int32, sd.shape, 1) >> 6
    sd = jnp.where(kc <= qc, sd, -jnp.inf)
    return sd if L == tq else jnp.concatenate([sc(0, L - tq), sd], axis=1)


def _attn_fwd(mixin, mixin_t, qcat, kcat, kv, Bl, S):
    T = Bl * S
    tq = _pick(S, (512, 256, 128))
    nq = S // tq

    def body(alias_ref, alias_t, q_ref, k_ref, v_ref, o_ref, lse_ref, ot_ref):
        i = pl.program_id(2)
        q = q_ref[...]
        for ii in range(nq):
            @pl.when(i == ii)
            def _(L=(ii + 1) * tq):
                s = _attn_scores(q, k_ref, L, tq)
                m = jnp.max(s, axis=-1, keepdims=True)
                e = jnp.exp(s - m)
                l = jnp.sum(e, axis=-1, keepdims=True)
                o = (_dot(e, v_ref[0:L, :], _NN) / l).astype(o_ref.dtype)
                o_ref[...] = o
                ot_ref[...] = o.T
                lse_ref[...] = jnp.broadcast_to(m + jnp.log(l), lse_ref.shape)

    qrow = lambda b, h, i: b * nq + i
    in_specs = [ANY, ANY,
                pl.BlockSpec((tq, 256), lambda b, h, i: (qrow(b, h, i), h)),
                pl.BlockSpec((S, 256), lambda b, h, i: (b, h)),
                pl.BlockSpec((S, 128), lambda b, h, i: (b, 2 * h + 1))]
    return _pc(body, name="mla_attn_fwd",
               out_shape=(_sds(mixin.shape, mixin.dtype), _sds((T, 512)), _sds(mixin_t.shape, mixin_t.dtype)),
               grid=(Bl, MLA_HEADS, nq), in_specs=in_specs,
               out_specs=(pl.BlockSpec((tq, 128), lambda b, h, i: (qrow(b, h, i), 4 + h)),
                          pl.BlockSpec((tq, 128), lambda b, h, i: (qrow(b, h, i), h)),
                          pl.BlockSpec((128, tq), lambda b, h, i: (4 + h, qrow(b, h, i)))),
               aliases={0: 0, 1: 2}, sem=("parallel", "parallel", "parallel"))(mixin, mixin_t, qcat, kcat, kv)


def _attn_bwd(qcat, kcat, kv, lse, dmixin, Bl, S):
    T = Bl * S
    tq = _pick(S, (512, 256, 128))
    nq = S // tq

    def body(q_ref, k_ref, v_ref, do_ref, lse_ref, dq_ref, dkv_ref, dkr_ref, acck_ref, accv_ref):
        h, i = pl.program_id(1), pl.program_id(2)
        q = q_ref[...]
        do = do_ref[...].astype(BF)
        lse_col = lse_ref[:, 0:1]

        @pl.when(i == 0)
        def _():
            acck_ref[...] = jnp.zeros_like(acck_ref)
            accv_ref[...] = jnp.zeros_like(accv_ref)

        for ii in range(nq):
            @pl.when(i == ii)
            def _(L=(ii + 1) * tq):
                p = jnp.exp(_attn_scores(q, k_ref, L, tq) - lse_col)
                pb = p.astype(BF)
                v = v_ref[0:L, :]
                dp = _dot(do, v, _NT)
                ds = (p * (dp - jnp.sum(dp * p, axis=-1, keepdims=True))).astype(BF)
                dq_ref[...] = _dot(ds, k_ref[0:L, :], _NN) * ATTN_SCALE
                acck_ref[0:L, :] += _dot(ds, q, _TN)
                accv_ref[0:L, :] += _dot(pb, do, _TN)

        @pl.when(i == nq - 1)
        def _():
            dkv_ref[:, 0:128] = acck_ref[:, 0:128].astype(dkv_ref.dtype)
            dkv_ref[:, 128:256] = accv_ref[...].astype(dkv_ref.dtype)
            dkr_ref[...] = jnp.where(h == 0, 0.0, dkr_ref[...]) + acck_ref[:, 128:256]

    qrow = lambda b, h, i: b * nq + i
    in_specs = [pl.BlockSpec((tq, 256), lambda b, h, i: (qrow(b, h, i), h)),
                pl.BlockSpec((S, 256), lambda b, h, i: (b, h)),
                pl.BlockSpec((S, 128), lambda b, h, i: (b, 2 * h + 1)),
                pl.BlockSpec((tq, 128), lambda b, h, i: (qrow(b, h, i), 4 + h)),
                pl.BlockSpec((tq, 128), lambda b, h, i: (qrow(b, h, i), h))]
    out_shape = (_sds((T, 1024)), _sds((T, 1024), BF), _sds((T, 128)))
    out_specs = (pl.BlockSpec((tq, 256), lambda b, h, i: (qrow(b, h, i), h)),
                 pl.BlockSpec((S, 256), lambda b, h, i: (b, h)),
                 pl.BlockSpec((S, 128), lambda b, h, i: (b, 0)))
    return _pc(body, name="mla_attn_bwd", out_shape=out_shape, grid=(Bl, MLA_HEADS, nq), in_specs=in_specs,
               out_specs=out_specs, scratch=[pltpu.VMEM((S, 256), F32), pltpu.VMEM((S, 128), F32)],
               sem=("arbitrary", "arbitrary", "arbitrary"))(qcat, kcat, kv, dmixin, lse)


def _out_ln1(mixin, w_out, x, g, b):
    T, D = x.shape
    tm = _pick(T, (512, 256))

    def epi(r, xs, outs):
        h = _layernorm(ALPHA * xs[0][...] + r, xs[1][...], xs[2][...])
        outs[0][...] = r
        outs[1][...] = h
        hb = h.astype(BF)
        outs[2][...] = hb
        outs[3][...] = hb.T

    row = pl.BlockSpec((tm, D), lambda i: (i, 0))
    par = pl.BlockSpec((1, D), lambda i: (0, 0))
    return _mm_rows(mixin, w_out, name="mm_out_ln1", tm=tm, extras=[(x, row), (g, par), (b, par)],
                    out_shape=(_sds((T, D)), _sds((T, D)), _sds((T, D), BF), _sds((D, T), BF)),
                    out_specs=(row, row, row, pl.BlockSpec((D, tm), lambda i: (0, i))), epi=epi)


def _dh1_ln1_bwd(du_u, wup_u, du_g, wup_g, dgpre, w_gate, dffn, x, mix, g, b):
    T, D = x.shape
    tm = _pick(T, (512, 256))

    def epi(r, xs, outs):
        dug_ref, wg_ref, dgp_ref, wgate_ref, dffn_ref, x_ref, mix_ref, g_ref, b_ref = xs
        acc_ref = outs[2]

        @pl.when(pl.program_id(0) == 0)
        def _():
            acc_ref[...] = jnp.zeros_like(acc_ref)

        dh = (r + _dot(dug_ref[...], wg_ref[...], _NT) + _dot(dgp_ref[...], wgate_ref[...], _NT)
              + ALPHA * dffn_ref[...])
        f = lambda xx, mm, gg, bb: _layernorm(ALPHA * xx + mm, gg, bb)
        _, vjp = jax.vjp(f, x_ref[...], mix_ref[...], g_ref[...], b_ref[...])
        dx, dm, dg, db = vjp(dh)
        outs[0][...] = dx
        outs[1][...] = dm.astype(BF)
        acc_ref[0:1, :] += dg
        acc_ref[1:2, :] += db

    row = pl.BlockSpec((tm, D), lambda i: (i, 0))
    par = pl.BlockSpec((1, D), lambda i: (0, 0))
    whole = lambda a: pl.BlockSpec(a.shape, lambda i: (0, 0))
    extras = [(du_g, pl.BlockSpec((tm, du_g.shape[1]), lambda i: (i, 0))), (wup_g, whole(wup_g)), (dgpre, row),
              (w_gate, whole(w_gate)), (dffn, row), (x, row), (mix, row), (g, par), (b, par)]
    return _mm_rows(du_u, wup_u, name="mm_dh1_ln1_bwd", tm=tm, mode="nt", sem="arbitrary", extras=extras,
                    out_shape=(_sds((T, D)), _sds((T, D), BF), _sds((8, D))),
                    out_specs=(row, row, pl.BlockSpec((8, D), lambda i: (0, 0))), epi=epi)


def _ffn_up_act_fwd(h1b, wup_g, wup_u, wg, wu, bg, bu, Bl, S):
    T, D = h1b.shape
    C = wup_g.shape[1]
    cb = _pick(C, (256, 128))
    kk = wg.shape[0]

    def body(h_ref, pg_ref, pu_ref, wg_ref, wu_ref, bg_ref, bu_ref, ug_ref, uu_ref, o_ref, ot_ref):
        h = h_ref[...]
        ug = _dot(h, pg_ref[...], _NN)
        uu = _dot(h, pu_ref[...], _NN)
        ug_ref[...] = ug
        uu_ref[...] = uu
        cg = _conv(ug, wg_ref[...]) + bg_ref[...]
        cu = _conv(uu, wu_ref[...]) + bu_ref[...]
        a = (_silu(cg) * cu).astype(o_ref.dtype)
        o_ref[...] = a
        ot_ref[...] = a.T

    blk = pl.BlockSpec((S, cb), lambda b, j: (b, j))
    pblk = pl.BlockSpec((D, cb), lambda b, j: (0, j))
    wblk = pl.BlockSpec((kk, cb), lambda b, j: (0, j))
    bblk = pl.BlockSpec((1, cb), lambda b, j: (0, j))
    return _pc(body, name="ffn_up_act_fwd", out_shape=(_sds((T, C)), _sds((T, C)), _sds((T, C), BF), _sds((C, T), BF)),
               grid=(Bl, C // cb), in_specs=[pl.BlockSpec((S, D), lambda b, j: (b, 0)), pblk, pblk, wblk, wblk, bblk, bblk],
               out_specs=(blk, blk, blk, pl.BlockSpec((cb, S), lambda b, j: (j, b))),
               sem=("parallel", "parallel"))(h1b, wup_g, wup_u, wg, wu, bg, bu)


def _ffn_act_bwd(ug, uu, wg, wu, bg, bu, dffnb, w_down, Bl, S):
    T, C = ug.shape
    D = dffnb.shape[1]
    cb = _pick(C, (256, 128))
    kk = wg.shape[0]

    def body(g_ref, u_ref, wg_ref, wu_ref, bg_ref, bu_ref, d_ref, wd_ref,
             dg_ref, du_ref, dwg_ref, dwu_ref, dbg_ref, dbu_ref):
        first = pl.program_id(1) == 0
        wgv, wuv = wg_ref[...], wu_ref[...]
        tg, tu = _taps(g_ref[...], kk), _taps(u_ref[...], kk)
        cg = _conv(None, wgv, tg) + bg_ref[...]
        cu = _conv(None, wuv, tu) + bu_ref[...]
        _, vjp = jax.vjp(lambda a, b: _silu(a) * b, cg, cu)
        dcg, dcu = vjp(_dot(d_ref[...], wd_ref[...], _NT))
        dg_ref[...] = _conv_bwd_x(dcg, wgv).astype(dg_ref.dtype)
        du_ref[...] = _conv_bwd_x(dcu, wuv).astype(du_ref.dtype)
        _conv_bwd_w(tg, dcg, dwg_ref, first)
        _conv_bwd_w(tu, dcu, dwu_ref, first)
        dbg_ref[...] = jnp.where(first, 0.0, dbg_ref[...]) + jnp.sum(dcg, axis=0, keepdims=True)
        dbu_ref[...] = jnp.where(first, 0.0, dbu_ref[...]) + jnp.sum(dcu, axis=0, keepdims=True)

    blk = pl.BlockSpec((S, cb), lambda j, b: (b, j))
    wblk = pl.BlockSpec((kk, cb), lambda j, b: (0, j))
    bblk = pl.BlockSpec((1, cb), lambda j, b: (0, j))
    out_shape = (_sds((T, C), BF), _sds((T, C), BF), _sds((kk, C)), _sds((kk, C)), _sds((1, C)), _sds((1, C)))
    return _pc(body, name="ffn_act_bwd", out_shape=out_shape, grid=(C // cb, Bl),
               in_specs=[blk, blk, wblk, wblk, bblk, bblk, pl.BlockSpec((S, D), lambda j, b: (b, 0)),
                         pl.BlockSpec((cb, D), lambda j, b: (j, 0))],
               out_specs=(blk, blk, wblk, wblk, bblk, bblk), sem=("parallel", "arbitrary"))(
        ug, uu, wg, wu, bg, bu, dffnb, w_down)


def _head(act, w_down, h1, h1b, w_gate, pb, w_proj, tgt, bg, g2, b2):
    T, D = h1.shape
    tm = _pick(T, (512, 256))

    def epi(ffn, xs, outs):
        h_ref, hb_ref, wg_ref, pb_ref, wp_ref, t_ref, bg_ref, g2_ref, b2_ref = xs
        acc_ref = outs[4]

        @pl.when(pl.program_id(0) == 0)
        def _():
            acc_ref[...] = jnp.zeros_like(acc_ref)

        h, tg = h_ref[...], t_ref[...]
        gpre = _dot(hb_ref[...], wg_ref[...], _NN)
        pp = _dot(pb_ref[...], wp_ref[...], _NN)

        def loss_fn(f, gp, p_, bgv, g2v, b2v):
            pre = ALPHA * h + f + jax.nn.sigmoid(gp + bgv) * p_
            err = _layernorm(pre, g2v, b2v) - tg
            return 0.5 * jnp.sum(jnp.mean(err * err, axis=-1, keepdims=True))

        loss, grads = jax.value_and_grad(loss_fn, argnums=(0, 1, 2, 3, 4, 5))(
            ffn, gpre, pp, bg_ref[...], g2_ref[...], b2_ref[...])
        outs[0][...] = grads[0]
        outs[1][...] = grads[0].astype(BF)
        outs[2][...] = grads[1].astype(BF)
        outs[3][...] = grads[2].astype(BF)
        acc_ref[0:1, :] += grads[3]
        acc_ref[1:2, :] += grads[4]
        acc_ref[2:3, :] += grads[5]
        acc_ref[3:4, :] += jnp.broadcast_to(loss, (1, D))

    row = pl.BlockSpec((tm, D), lambda i: (i, 0))
    par = pl.BlockSpec((1, D), lambda i: (0, 0))
    whole = lambda a: pl.BlockSpec(a.shape, lambda i: (0, 0))
    o_bf = _sds((T, D), BF)
    extras = [(h1, row), (h1b, row), (w_gate, whole(w_gate)), (pb, pl.BlockSpec((tm, pb.shape[1]), lambda i: (i, 0))),
              (w_proj, whole(w_proj)), (tgt, row), (bg, par), (g2, par), (b2, par)]
    return _mm_rows(act, w_down, name="mm_down_loss_head", tm=tm, sem="arbitrary", extras=extras,
                    out_shape=(_sds((T, D)), o_bf, o_bf, o_bf, _sds((8, D))),
                    out_specs=(row, row, row, row, pl.BlockSpec((8, D), lambda i: (0, 0))), epi=epi)


def _adamw(parts, w, m, v, name):
    R, C = w.shape
    tr = R if R <= 512 else _pick(R, (256,))

    def body(p_ref, w_ref, m_ref, v_ref, g_ref, d_ref, nm_ref, nv_ref):
        g = p_ref[0].astype(F32)
        for j in range(1, N_DEV):
            g = g + p_ref[j].astype(F32)
        g_ref[...] = g
        d_ref[...], nm_ref[...], nv_ref[...] = _adam_math(g, w_ref[...], m_ref[...], v_ref[...])

    blk = pl.BlockSpec((tr, C), lambda i: (i, 0))
    o = _sds((R, C))
    return _pc(body, name=name, out_shape=(o, o, o, o), grid=(R // tr,),
               in_specs=[pl.BlockSpec((N_DEV, tr, C), lambda i: (0, i, 0)), blk, blk, blk],
               out_specs=(blk, blk, blk, blk), sem=("parallel",))(parts, w, m, v)


def _adam_math(g, w, m, v):
    mm = ADAM_B1 * m + (1.0 - ADAM_B1) * g
    vv = ADAM_B2 * v + (1.0 - ADAM_B2) * jnp.square(g)
    m_hat = mm / (1.0 - ADAM_B1 ** ADAM_STEP)
    v_hat = vv / (1.0 - ADAM_B2 ** ADAM_STEP)
    return -ADAM_LR * (m_hat / (jnp.sqrt(v_hat) + ADAM_EPS) + ADAM_WD * w), mm, vv


def _small_layout(sizes):
    offs, off = {}, 0
    for n in SMALL_ORDER:
        offs[n] = off
        off += -(-sizes[n] // 128) * 128
    return offs, off


def _pack_small(d):
    return jnp.concatenate([_pad_lanes(d[n].reshape(1, -1), -(-d[n].size // 128) * 128) for n in SMALL_ORDER], axis=1)


def _adamw_small(parts, ws, ms, vs):
    k = len(SMALL_ORDER)
    sizes = {n: ws[n].shape[1] for n in SMALL_ORDER}
    offs, _ = _small_layout(sizes)

    def body(p_ref, *refs):
        ins, outs = refs[:3 * k], refs[3 * k:]
        for i, n in enumerate(SMALL_ORDER):
            lo, hi = offs[n], offs[n] + sizes[n]
            g = p_ref[0, :, lo:hi]
            for j in range(1, N_DEV):
                g = g + p_ref[j, :, lo:hi]
            d, mm, vv = _adam_math(g, ins[i][...], ins[k + i][...], ins[2 * k + i][...])
            outs[4 * i][...], outs[4 * i + 1][...], outs[4 * i + 2][...], outs[4 * i + 3][...] = g, d, mm, vv

    out_shape = tuple(_sds((1, sizes[n])) for n in SMALL_ORDER for _ in range(4))
    args = [ws[n] for n in SMALL_ORDER] + [ms[n] for n in SMALL_ORDER] + [vs[n] for n in SMALL_ORDER]
    res = _pc(body, name="adamw_small", out_shape=out_shape)(parts, *args)
    return {n: tuple(res[4 * i:4 * i + 4]) for i, n in enumerate(SMALL_ORDER)}


def _prep(x, p, xchg=None):
    T, D = x.shape
    Dp = p.shape[1]
    tt = _pick(T, (512, 256))

    def body(x_ref, p_ref, xb_ref, xt_ref, pb_ref, pt_ref):
        xb = x_ref[...].astype(BF)
        xb_ref[...] = xb
        xt_ref[...] = xb.T
        pb = p_ref[...].astype(BF)
        pb_ref[...] = pb
        pt_ref[...] = pb.T

    row = lambda w: pl.BlockSpec((tt, w), lambda i: (i, 0))
    col = lambda w: pl.BlockSpec((w, tt), lambda i: (0, i))
    return _pc(body, name="prep_inputs",
               out_shape=(_sds((T, D), BF), _sds((D, T), BF), _sds((T, Dp), BF), _sds((Dp, T), BF)),
               grid=(T // tt,), in_specs=[row(D), row(Dp)], out_specs=(row(D), col(D), row(Dp), col(Dp)),
               sem=("parallel",), xchg=xchg)(x, p)


SMALL_ORDER = ("gdn_a_log", "gdn_dt_bias", "gdn_norm_g", "mla_q_norm_g", "mla_kv_norm_g", "ln1_g", "ln1_b",
               "ffn_conv_b", "ple_b_gate", "ln2_g", "ln2_b")
EARLY = ("w_in", "gdn_conv_w")
LATE = ("mla_w_q_up", "mla_w_kv_up", "w_out", "ffn_w_up", "ffn_conv_w", "ffn_w_down", "ple_w_gate", "ple_w_proj")
GRADS_EARLY = ("w_out", "ffn_w_up", "ffn_conv_w", "ffn_w_down", "ple_w_gate", "ple_w_proj")
GRADS_MID = ("gdn_conv_w", "mla_w_q_up", "mla_w_kv_up")
GRADS_LATE = ("w_in",)


def _pad_lanes(v, n=128):
    return jnp.pad(v, ((0, 0), (0, n - v.shape[1])))


def _local_step(x, p, tgt, W, sp, Bl, S, early_weights=None, late_weights=None, early_grads=None, mid_grads=None,
                late_grads=None):
    T = Bl * S
    W = dict(W)
    prep = _prep(x, p, xchg=None if early_weights is None else early_weights[:2])
    xb, xT, pb, pT = prep[:4]
    if early_weights is not None:
        W.update(early_weights[2](prep[4:]))
    w_in = W["w_in"]
    wqkv, wz = w_in[:, :1536], w_in[:, 1536:2048]
    z64 = jnp.zeros((w_in.shape[0], 64), w_in.dtype)
    z124 = jnp.zeros((w_in.shape[0], 124), w_in.dtype)
    wsm = jnp.concatenate([w_in[:, 2056:2440], w_in[:, 2696:2760], z64, w_in[:, 2440:2696],
                           w_in[:, 2048:2052], z124, w_in[:, 2052:2056], z124], axis=1)
    alog, dtb = _pad_lanes(sp["gdn_a_log"]), _pad_lanes(sp["gdn_dt_bias"])
    cos, sin = _rope_tables(S)

    qkv, qkvn = _gdn_proj_pre_fwd(xb, wqkv, W["gdn_conv_w"], Bl, S)
    z = _mm(xb, wz, "nn", name="mm_z")
    sm = _mm(xb, wsm, "nn", name="mm_sm")
    gdn_out = _gdn_fwd(qkvn, sm, alog, dtb, Bl, S, xchg=None if late_weights is None else late_weights[:2])
    o, tinv_s, sst_s, w_s, u_s, gc_s, beta_s = gdn_out[:7]
    if late_weights is not None:
        W.update(late_weights[2](gdn_out[7:]))
    wq = jnp.pad(W["mla_w_q_up"].reshape(-1, MLA_HEADS, 192), ((0, 0), (0, 0), (0, 64))).reshape(-1, 1024)
    wkv = W["mla_w_kv_up"]
    C = W["ffn_w_down"].shape[0]
    wup_g, wup_u = W["ffn_w_up"][:, :C], W["ffn_w_up"][:, C:]
    cw_g, cw_u = W["ffn_conv_w"][:, :C], W["ffn_conv_w"][:, C:]
    cb_g, cb_u = sp["ffn_conv_b"][:, :C], sp["ffn_conv_b"][:, C:]
    mixin, mixin_t = _gdn_post(o, z, sp["gdn_norm_g"], T)
    cqn, cqn_t = _rms_fwd(sm, sp["mla_q_norm_g"], 0, "rms_q_fwd")
    ckvn, ckvn_t = _rms_fwd(sm, sp["mla_kv_norm_g"], 2, "rms_kv_fwd")
    qcat = _qup_rope(cqn, wq, cos, sin, S)
    kv, kcat = _kvup_rope(ckvn, wkv, sm, cos, sin, S)
    mixin, lse, mixin_t = _attn_fwd(mixin, mixin_t, qcat, kcat, kv, Bl, S)
    mix, h1, h1b, h1T = _out_ln1(mixin, W["w_out"], x, sp["ln1_g"], sp["ln1_b"])
    ug, uu, act, act_t = _ffn_up_act_fwd(h1b, wup_g, wup_u, cw_g, cw_u, cb_g, cb_u, Bl, S)

    dffn, dffnb, dgpre, dpp, hacc = _head(act, W["ffn_w_down"], h1, h1b, W["ple_w_gate"], pb, W["ple_w_proj"], tgt,
                                          sp["ple_b_gate"], sp["ln2_g"], sp["ln2_b"])
    loss = hacc[3, 0]
    gW, gs = {}, {}
    gs["ple_b_gate"], gs["ln2_g"], gs["ln2_b"] = hacc[0:1], hacc[1:2], hacc[2:3]
    gW["ple_w_proj"] = _mm(pT, dpp, "nn", name="mm_dproj", out_dtype=BF)
    gW["ple_w_gate"] = _mm(h1T, dgpre, "nn", name="mm_dgate", out_dtype=BF)
    gW["ffn_w_down"] = _mm(act_t, dffnb, "nn", name="mm_ddown", out_dtype=BF)
    du_g, du_u, dcw_g, dcw_u, dcb_g, dcb_u = _ffn_act_bwd(ug, uu, cw_g, cw_u, cb_g, cb_u, dffnb, W["ffn_w_down"],
                                                          Bl, S)
    gW["ffn_conv_w"] = jnp.concatenate([dcw_g, dcw_u], axis=1)
    gs["ffn_conv_b"] = jnp.concatenate([dcb_g, dcb_u], axis=1)
    gW["ffn_w_up"] = jnp.concatenate([_mm(h1T, du_g, "nn", name="mm_dup_gate", out_dtype=BF),
                                      _mm(h1T, du_u, "nn", name="mm_dup_up", out_dtype=BF)], axis=1)
    dxa, dmix, acc1 = _dh1_ln1_bwd(du_u, wup_u, du_g, wup_g, dgpre, W["ple_w_gate"], dffn, x, mix,
                                   sp["ln1_g"], sp["ln1_b"])
    gs["ln1_g"], gs["ln1_b"] = acc1[0:1], acc1[1:2]
    gW["w_out"] = _mm(mixin_t, dmix, "nn", name="mm_dwout", out_dtype=BF)
    dmixin = _mm(dmix, W["w_out"], "nt", name="mm_dmixin")
    do, dcat, gacc = _gdn_post_bwd(o, z, sp["gdn_norm_g"], dmixin, T)
    gs["gdn_norm_g"] = gacc[0:1]
    bwd_out = _gdn_bwd(qkvn, sm, alog, dtb, gc_s, beta_s, tinv_s, sst_s, w_s, u_s, do, dcat, Bl, S,
                       xchg=None if early_grads is None else early_grads({n: gW[n] for n in GRADS_EARLY}))
    dqkvn, dcat, cacc = bwd_out[:3]
    early_recv = bwd_out[3:]
    gs["gdn_a_log"], gs["gdn_dt_bias"] = cacc[0:1, :GDN_HEADS], cacc[1:2, :GDN_HEADS]
    dcat, gW["gdn_conv_w"] = _gdn_pre_bwd(qkv, W["gdn_conv_w"], dqkvn, dcat, Bl, S)
    dqc, dkv, dkrr = _attn_bwd(qcat, kcat, kv, lse, dmixin, Bl, S)
    dqraw, dcat = _rope_bwd(dqc, dcat, dkrr, cos, sin, S)
    gwq = _mm(cqn_t, dqraw, "nn", name="mm_dwq", out_dtype=BF)
    gW["mla_w_q_up"] = gwq.reshape(-1, MLA_HEADS, 256)[:, :, :192].reshape(-1, MLA_HEADS * 192)
    gW["mla_w_kv_up"] = _mm(ckvn_t, dkv, "nn", name="mm_dwkv", out_dtype=BF)
    dcqn = _mm(dqraw, wq, "nt", name="mm_dcqn")
    dckvn = _mm(dkv, wkv, "nt", name="mm_dckvn")
    dcat, qacc = _rms_bwd(sm, sp["mla_q_norm_g"], dcqn, dcat, 0, "rms_q_bwd")
    dcat, kacc = _rms_bwd(sm, sp["mla_kv_norm_g"], dckvn, dcat, 2, "rms_kv_bwd")
    gs["mla_q_norm_g"], gs["mla_kv_norm_g"] = qacc[0:1], kacc[0:1]
    gcat = _mm(xT, dcat, "nn", name="mm_dwin", out_dtype=BF,
               xchg=None if mid_grads is None else mid_grads(gW, gs))
    mid_recv = ()
    if mid_grads is not None:
        gcat, mid_recv = gcat[0], gcat[1:]
    gsm, gqkv, gz = gcat[:, :DCAT_QKV], gcat[:, DCAT_QKV:DCAT_Z], gcat[:, DCAT_Z:]
    gW["w_in"] = jnp.concatenate([gqkv, gz, gsm[:, 768:772], gsm[:, 896:900], gsm[:, 0:384], gsm[:, 512:768],
                                  gsm[:, 384:448]], axis=1)
    wcat = jnp.concatenate([wsm, wqkv, wz], axis=1)
    dx = _mm(dcat, wcat, "nt", name="mm_dx", add=dxa, xchg=None if late_grads is None else late_grads(gW))
    late_recv = ()
    if late_grads is not None:
        dx, late_recv = dx[0], dx[1:]
    return loss, dx, gW, gs, early_recv, mid_recv, late_recv


COL_SHARDED = ("w_in", "mla_w_q_up", "mla_w_kv_up", "ffn_w_up", "ple_w_proj", "gdn_conv_w", "ffn_conv_w")
SHARDED = EARLY + LATE
WEIGHTS = ("w_in", "gdn_conv_w", "gdn_a_log", "gdn_dt_bias", "gdn_norm_g", "mla_q_norm_g", "mla_w_q_up",
           "mla_kv_norm_g", "mla_w_kv_up", "w_out", "ln1_g", "ln1_b", "ffn_w_up", "ffn_conv_w", "ffn_conv_b",
           "ffn_w_down", "ple_w_gate", "ple_b_gate", "ple_w_proj", "ln2_g", "ln2_b")
CONV = ("gdn_conv_w", "ffn_conv_w")


def _gathered_to_full(name, g):
    if name in COL_SHARDED:
        return jnp.transpose(g, (1, 0, 2)).reshape(g.shape[1], -1)
    return g.reshape(-1, g.shape[-1])


def _full_to_blocks(name, gfull, shard_shape):
    r, c = shard_shape
    if name in COL_SHARDED:
        return jnp.transpose(gfull.reshape(r, N_DEV, c), (1, 0, 2))
    return gfull.reshape(N_DEV, r, c)


def kernel(x, p, w_in, gdn_conv_w, gdn_a_log, gdn_dt_bias, gdn_norm_g, mla_q_norm_g, mla_w_q_up, mla_kv_norm_g, mla_w_kv_up, w_out, ln1_g, ln1_b, ffn_w_up, ffn_conv_w, ffn_conv_b, ffn_w_down, ple_w_gate, ple_b_gate, ple_w_proj, ln2_g, ln2_b, loss_target, m_w_in, m_gdn_conv_w, m_gdn_a_log, m_gdn_dt_bias, m_gdn_norm_g, m_mla_q_norm_g, m_mla_w_q_up, m_mla_kv_norm_g, m_mla_w_kv_up, m_w_out, m_ln1_g, m_ln1_b, m_ffn_w_up, m_ffn_conv_w, m_ffn_conv_b, m_ffn_w_down, m_ple_w_gate, m_ple_b_gate, m_ple_w_proj, m_ln2_g, m_ln2_b, v_w_in, v_gdn_conv_w, v_gdn_a_log, v_gdn_dt_bias, v_gdn_norm_g, v_mla_q_norm_g, v_mla_w_q_up, v_mla_kv_norm_g, v_mla_w_kv_up, v_w_out, v_ln1_g, v_ln1_b, v_ffn_w_up, v_ffn_conv_w, v_ffn_conv_b, v_ffn_w_down, v_ple_w_gate, v_ple_b_gate, v_ple_w_proj, v_ln2_g, v_ln2_b):
    loc = dict(locals())
    wts = {n: loc[n] for n in WEIGHTS}
    ms = {n: loc["m_" + n] for n in WEIGHTS}
    vs = {n: loc["v_" + n] for n in WEIGHTS}
    Bl, S, D = x.shape
    T = Bl * S

    wire = lambda n: wts[n][0] if n in CONV else wts[n][0].astype(BF)
    gather = lambda names, mode: ([wire(n) for n in names], [mode] * len(names),
                                  lambda res: {n: _gathered_to_full(n, g) for n, g in zip(names, res)})
    blocks = lambda g, names: [_full_to_blocks(n, g[n], wts[n].shape[1:]) for n in names]
    pack_early = lambda g: (blocks(g, GRADS_EARLY), ["a2a"] * len(GRADS_EARLY))
    pack_mid = lambda g, gs: (blocks(g, GRADS_MID) + [_pack_small(gs)], ["a2a"] * len(GRADS_MID) + ["ag"])
    pack_late = lambda g: (blocks(g, GRADS_LATE), ["a2a"] * len(GRADS_LATE))
    sp = {n: wts[n].reshape(1, -1) for n in SMALL_ORDER}

    loss, dx, gW, gs, early_recv, mid_recv, late_recv = _local_step(
        x.reshape(T, D), p.reshape(T, -1), loss_target.reshape(T, D), {}, sp, Bl, S,
        early_weights=gather(EARLY, "ag2"), late_weights=gather(LATE, "ag"), early_grads=pack_early,
        mid_grads=pack_mid, late_grads=pack_late)

    res = {}
    for n, parts in (list(zip(GRADS_EARLY, early_recv)) + list(zip(GRADS_MID, mid_recv[:-1]))
                     + list(zip(GRADS_LATE, late_recv))):
        res[n] = tuple(t[None] for t in _adamw(parts, wts[n][0], ms[n][0], vs[n][0], "adamw_" + n))
    res.update(_adamw_small(mid_recv[-1], wts, ms, vs))

    loss = lax.psum(loss, ("x", "y", "c"))
    outs = [loss, dx.reshape(Bl, S, D)]
    for k in range(4):
        outs += [res[n][k] for n in WEIGHTS]
    return tuple(outs)
```

```python
import math

import jax
import jax.numpy as jnp
from jax import lax
from jax.experimental import pallas as pl
from jax.experimental.pallas import tpu as pltpu

F32 = jnp.float32
BF = jnp.bfloat16
_MXU = jnp.bfloat16
_INTERPRET = None
_VMEM_LIMIT = 56 * 1024 * 1024
HI = lax.Precision.HIGHEST

N_DEV = 8
CHUNK = 64
GDN_HEADS = 4
GDN_DK = 128
MLA_HEADS = 4
MLA_NOPE = 128
MLA_ROPE = 64
ROPE_THETA = 10000.0
ALPHA = 2.0 ** 0.25
NORM_EPS = 1e-6
ATTN_SCALE = (MLA_NOPE + MLA_ROPE) ** -0.5
DCAT_QKV, DCAT_Z, DCAT_W = 1024, 2560, 3072
ADAM_LR, ADAM_B1, ADAM_B2, ADAM_EPS, ADAM_WD, ADAM_STEP = 0.001, 0.9, 0.999, 1e-08, 0.01, 10
MESH = pl.DeviceIdType.MESH
ANY = pl.BlockSpec(memory_space=pl.ANY)
_NN = (((1,), (0,)), ((), ()))
_NT = (((1,), (1,)), ((), ()))
_TN = (((0,), (0,)), ((), ()))


def _sds(shape, dtype=F32):
    return jax.ShapeDtypeStruct(tuple(shape), dtype)


def _pick(n, cands):
    for c in cands:
        if n % c == 0:
            return c
    return n


def _xchg_out_shapes(arrs, modes):
    return [_sds(a.shape if m == "a2a" else (N_DEV,) + a.shape, a.dtype) for a, m in zip(arrs, modes)]


def _xchg_scratch(n):
    return [pltpu.SemaphoreType.DMA((n, N_DEV - 1)), pltpu.SemaphoreType.DMA((n, N_DEV - 1)),
            pltpu.SemaphoreType.DMA((n,))]


def _xchg_plan(ins, outs, send, recv, loc, modes):
    x, y, c = lax.axis_index("x"), lax.axis_index("y"), lax.axis_index("c")
    me = 4 * x + 2 * y + c
    starts, relays, waits = [], [], []
    for ai, mode in enumerate(modes):
        src_all, out = ins[ai], outs[ai]

        def remote(k, src, dst, dev):
            return pltpu.make_async_remote_copy(src_ref=src, dst_ref=dst, send_sem=send.at[ai, k], recv_sem=recv.at[ai, k],
                                                device_id=dev, device_id_type=MESH)

        own = pltpu.make_async_copy(src_all.at[me] if mode == "a2a" else src_all, out.at[me], loc.at[ai])
        starts.append(own)
        waits.append(own.wait)
        if mode == "ag2":
            sib = (x, y, 1 - c)
            chips = [(1 - x, y), (x, 1 - y), (1 - x, 1 - y)]
            first = [remote(0, src_all, out.at[me], sib)]
            first += [remote(1 + k, src_all, out.at[me], (px, py, c)) for k, (px, py) in enumerate(chips)]
            starts += first
            waits += [cp.wait_send for cp in first]
            waits.append(remote(0, src_all, out.at[4 * x + 2 * y + 1 - c], sib).wait_recv)
            for k, (px, py) in enumerate(chips):
                same, other = out.at[4 * px + 2 * py + c], out.at[4 * px + 2 * py + 1 - c]
                relay = remote(4 + k, same, same, sib)
                relays.append((remote(1 + k, src_all, same, (px, py, c)), relay))
                waits += [relay.wait_send, remote(4 + k, src_all, other, sib).wait_recv]
            continue
        for r in range(1, N_DEV):
            px = 1 - x if (r >> 2) & 1 else x
            py = 1 - y if (r >> 1) & 1 else y
            pc = 1 - c if r & 1 else c
            cp = remote(r - 1, src_all.at[4 * px + 2 * py + pc] if mode == "a2a" else src_all, out.at[me], (px, py, pc))
            starts.append(cp)
            waits.append(cp.wait)
    return starts, relays, waits


def _pc(body, *, name, out_shape, grid=None, in_specs=None, out_specs=None, scratch=(), sem=None, aliases=None,
        xchg=None):
    kw = {}
    if _INTERPRET is not None:
        kw["interpret"] = _INTERPRET
    single = not isinstance(out_shape, (tuple, list))
    out_shape = [out_shape] if single else list(out_shape)
    if out_specs is not None:
        out_specs = [out_specs] if single else list(out_specs)
    scratch = list(scratch)
    if xchg is not None:
        xarrs, xmodes = xchg
        n_in, n_out, n_scr, nx = len(in_specs), len(out_shape), len(scratch), len(xarrs)
        single = False
        inner = body

        def body(*refs):
            ins, xins = refs[:n_in], refs[n_in:n_in + nx]
            outs = refs[n_in + nx:n_in + nx + n_out]
            xouts = refs[n_in + nx + n_out:n_in + 2 * nx + n_out]
            scr = refs[n_in + 2 * nx + n_out:n_in + 2 * nx + n_out + n_scr]
            send, recv, loc = refs[n_in + 2 * nx + n_out + n_scr:]
            first = last = None
            for d, g in enumerate(grid):
                f, l = pl.program_id(d) == 0, pl.program_id(d) == g - 1
                first, last = (f, l) if first is None else (first & f, last & l)

            @pl.when(first)
            def _():
                for cp in _xchg_plan(xins, xouts, send, recv, loc, xmodes)[0]:
                    cp.start()

            inner(*ins, *outs, *scr)

            @pl.when(last)
            def _():
                _, relays, waits = _xchg_plan(xins, xouts, send, recv, loc, xmodes)
                for arrival, relay in relays:
                    arrival.wait_recv()
                    relay.start()
                for wait in waits:
                    wait()

        in_specs = list(in_specs) + [ANY] * nx
        out_specs = out_specs + [ANY] * nx
        out_shape = out_shape + _xchg_out_shapes(xarrs, xmodes)
        scratch = scratch + _xchg_scratch(nx)
        sem = ("arbitrary",) * len(grid)
    if grid is not None:
        kw["grid"] = grid
    if in_specs is not None:
        kw["in_specs"] = in_specs
    if out_specs is not None:
        kw["out_specs"] = out_specs[0] if single else tuple(out_specs)
    cp = dict(vmem_limit_bytes=_VMEM_LIMIT)
    if sem is not None:
        cp["dimension_semantics"] = sem
    call = pl.pallas_call(body, name=name, out_shape=out_shape[0] if single else tuple(out_shape),
                          scratch_shapes=scratch, input_output_aliases=aliases or {},
                          compiler_params=pltpu.CompilerParams(**cp), **kw)
    if xchg is None:
        return call
    return lambda *ins: call(*ins, *xchg[0])


_FULL_K_MAX = 3072


def _mm(a, b, mode, *, name, add=None, add_scale=1.0, out_dtype=F32, xchg=None):
    (M, K) = a.shape
    (K2, N) = b.shape if mode == "nn" else b.shape[::-1]
    assert K == K2, (a.shape, b.shape, mode)
    tm = _pick(M, (1024, 1408, 512, 384, 256, 128))
    tn = _pick(N, (1024, 2816, 1408, 768, 512, 384, 256, 128))
    tk = K if K <= _FULL_K_MAX else _pick(K, (2048, 1408, 1024, 512) if tn <= 1408 else (1024, 512))
    nk = K // tk
    dims = _NN if mode == "nn" else _NT
    has_add = add is not None

    def finish(r, add_ref, o_ref):
        if has_add:
            r = r + add_scale * add_ref[...].astype(F32)
        o_ref[...] = r.astype(out_dtype)

    if nk == 1:
        def body(a_ref, b_ref, *rest):
            r = lax.dot_general(a_ref[...].astype(_MXU), b_ref[...].astype(_MXU), dims, preferred_element_type=F32)
            finish(r, rest[0] if has_add else None, rest[-1])

        a_spec = pl.BlockSpec((tm, K), lambda i, j: (i, 0))
        b_spec = (pl.BlockSpec((K, tn), lambda i, j: (0, j)) if mode == "nn"
                  else pl.BlockSpec((tn, K), lambda i, j: (j, 0)))
        o_spec = pl.BlockSpec((tm, tn), lambda i, j: (i, j))
        grid, sem, scratch = (M // tm, N // tn), ("parallel", "parallel"), []
    else:
        def body(a_ref, b_ref, *rest):
            o_ref, acc = rest[-2], rest[-1]
            k = pl.program_id(2)

            @pl.when(k == 0)
            def _():
                acc[...] = jnp.zeros_like(acc)

            acc[...] += lax.dot_general(a_ref[...].astype(_MXU), b_ref[...].astype(_MXU), dims,
                                        preferred_element_type=F32)

            @pl.when(k == nk - 1)
            def _():
                finish(acc[...], rest[0] if has_add else None, o_ref)

        a_spec = pl.BlockSpec((tm, tk), lambda i, j, k: (i, k))
        b_spec = (pl.BlockSpec((tk, tn), lambda i, j, k: (k, j)) if mode == "nn"
                  else pl.BlockSpec((tn, tk), lambda i, j, k: (j, k)))
        o_spec = pl.BlockSpec((tm, tn), lambda i, j, k: (i, j))
        grid, sem, scratch = (M // tm, N // tn, nk), ("parallel", "parallel", "arbitrary"), [pltpu.VMEM((tm, tn), F32)]

    ins = [a, b] + ([add] if has_add else [])
    specs = [a_spec, b_spec] + ([o_spec] if has_add else [])
    return _pc(body, name=name, out_shape=_sds((M, N), out_dtype), grid=grid, in_specs=specs, out_specs=o_spec,
               scratch=scratch, sem=sem, xchg=xchg)(*ins)


def _dot(a, b, dims):
    return lax.dot_general(a.astype(_MXU), b.astype(_MXU), dims, preferred_element_type=F32)


def _split(a):
    hi = a.astype(BF)
    lo = (a - hi.astype(F32)).astype(BF)
    return hi, lo


def _dot3(a, b, dims):
    ah, al = _split(a)
    bh, bl = _split(b)
    d = lambda u, v: lax.dot_general(u, v, dims, preferred_element_type=F32)
    return d(ah, bh) + (d(ah, bl) + d(al, bh))


def _softplus(x):
    return jnp.maximum(x, 0.0) + jnp.log1p(jnp.exp(-jnp.abs(x)))


def _silu(x):
    return x * jax.nn.sigmoid(x)


def _rmsnorm(x, g):
    return x * lax.rsqrt(jnp.mean(x * x, axis=-1, keepdims=True) + NORM_EPS) * g


def _layernorm(x, g, b):
    mu = jnp.mean(x, axis=-1, keepdims=True)
    xc = x - mu
    var = jnp.mean(xc * xc, axis=-1, keepdims=True)
    return xc * lax.rsqrt(var + NORM_EPS) * g + b


def _shift_dn(x, s):
    if s == 0:
        return x
    t = lax.broadcasted_iota(jnp.int32, x.shape, 0)
    return jnp.where(t >= s, pltpu.roll(x, s, 0), 0.0)


def _shift_up(x, s):
    if s == 0:
        return x
    n = x.shape[0]
    t = lax.broadcasted_iota(jnp.int32, x.shape, 0)
    return jnp.where(t < n - s, pltpu.roll(x, n - s, 0), 0.0)


def _taps(x, kk):
    return [_shift_dn(x, kk - 1 - j) for j in range(kk)]


def _conv(x, w, taps=None):
    kk = w.shape[0]
    taps = _taps(x, kk) if taps is None else taps
    y = w[kk - 1:kk, :] * taps[kk - 1]
    for j in range(kk - 1):
        y = y + w[j:j + 1, :] * taps[j]
    return y


def _conv_bwd_x(dy, w):
    kk = w.shape[0]
    dx = w[kk - 1:kk, :] * dy
    for j in range(kk - 1):
        dx = dx + w[j:j + 1, :] * _shift_up(dy, kk - 1 - j)
    return dx


def _conv_bwd_w(taps, dy, dw_ref, first):
    kk = dw_ref.shape[0]
    for j in range(kk):
        r = jnp.sum(dy * taps[j], axis=0, keepdims=True)
        prev = jnp.where(first, 0.0, dw_ref[j:j + 1, :])
        dw_ref[j:j + 1, :] = prev + r


def _gdn_post_conv(c, j):
    h = _silu(c)
    hn = h * lax.rsqrt(jnp.sum(h * h, axis=-1, keepdims=True) + NORM_EPS)
    return jnp.where(j < GDN_HEADS, hn * (GDN_DK ** -0.5), jnp.where(j < 2 * GDN_HEADS, hn, h))


def _gdn_proj_pre_fwd(xb, wqkv, conv_w, Bl, S):
    T, D = xb.shape
    C = wqkv.shape[1]
    nj = C // 256

    def body(x_ref, p_ref, w_ref, qkv_ref, o_ref):
        j = pl.program_id(1)
        qkv = _dot(x_ref[...], p_ref[...], _NN)
        qkv_ref[...] = qkv
        c = _conv(qkv, w_ref[...])
        for half in range(2):
            o_ref[:, 128 * half:128 * (half + 1)] = _gdn_post_conv(c[:, 128 * half:128 * (half + 1)], 2 * j)

    blk = pl.BlockSpec((S, 256), lambda b, j: (b, j))
    return _pc(body, name="gdn_proj_pre_fwd", out_shape=(_sds((T, C)), _sds((T, C))), grid=(Bl, nj),
               in_specs=[pl.BlockSpec((S, D), lambda b, j: (b, 0)), pl.BlockSpec((D, 256), lambda b, j: (0, j)),
                         pl.BlockSpec((conv_w.shape[0], 256), lambda b, j: (0, j))],
               out_specs=(blk, blk), sem=("parallel", "parallel"))(xb, wqkv, conv_w)


def _gdn_pre_bwd(qkv, conv_w, dout, dcat, Bl, S):
    T, C = qkv.shape
    nj = C // 128
    kk = conv_w.shape[0]

    def body(x_ref, w_ref, d_ref, alias_ref, dx_ref, dw_ref):
        j, b = pl.program_id(0), pl.program_id(1)
        w = w_ref[...]
        taps = _taps(x_ref[...], kk)
        c = _conv(None, w, taps)
        _, vjp = jax.vjp(lambda u: _gdn_post_conv(u, j), c)
        (dc,) = vjp(d_ref[...])
        dx_ref[...] = _conv_bwd_x(dc, w).astype(dx_ref.dtype)
        _conv_bwd_w(taps, dc, dw_ref, b == 0)

    blk = pl.BlockSpec((S, 128), lambda j, b: (b, j))
    wblk = pl.BlockSpec((kk, 128), lambda j, b: (0, j))
    return _pc(body, name="gdn_pre_bwd", out_shape=(_sds(dcat.shape, dcat.dtype), _sds((kk, C))), grid=(nj, Bl),
               in_specs=[blk, wblk, blk, ANY],
               out_specs=(pl.BlockSpec((S, 128), lambda j, b: (b, DCAT_QKV // 128 + j)), wblk),
               aliases={3: 0}, sem=("parallel", "arbitrary"))(qkv, conv_w, dout, dcat)


_H = GDN_HEADS
_HC = _H * CHUNK


def _st(x):
    return jnp.concatenate([x[:, h * 128:(h + 1) * 128] for h in range(_H)], axis=0)


def _unst(x):
    return jnp.concatenate([x[h * CHUNK:(h + 1) * CHUNK] for h in range(_H)], axis=1)


def _stc(t):
    return jnp.concatenate([t[:, h:h + 1] for h in range(_H)], axis=0)


def _untile(col):
    lane = lax.broadcasted_iota(jnp.int32, (CHUNK, 128), 1)
    out = jnp.zeros((CHUNK, 128), F32)
    for h in range(_H):
        out = out + jnp.where(lane == h, col[h * CHUNK:(h + 1) * CHUNK], 0.0)
    return out


def _rowform(col):
    return jnp.broadcast_to(col, (_HC, 128)).T[0:1, :]


def _tri(n, upper=False):
    i = lax.broadcasted_iota(jnp.int32, (n, n), 0)
    j = lax.broadcasted_iota(jnp.int32, (n, n), 1)
    return jnp.where((j >= i) if upper else (j <= i), 1.0, 0.0).astype(F32)


def _gdn_chunk_common(q, k, v, gc, beta):
    Q, K, V = _st(q), _st(k), _st(v)
    B, GC = _stc(beta), _stc(gc)
    GL = jnp.concatenate([jnp.broadcast_to(gc[CHUNK - 1:CHUNK, h:h + 1], (CHUNK, 1)) for h in range(_H)], axis=0)
    ii = lax.broadcasted_iota(jnp.int32, (_HC, _HC), 0)
    jj = lax.broadcasted_iota(jnp.int32, (_HC, _HC), 1)
    same = (ii >> 6) == (jj >> 6)
    incl = same & (ii >= jj)
    strict = same & (ii > jj)
    diff = GC - _rowform(GC)
    D = jnp.where(incl, jnp.exp(jnp.where(incl, diff, 0.0)), 0.0)
    KB = K * B
    A = jnp.where(strict, _dot(KB, K, _NT) * D, 0.0)
    P = jnp.where(incl, _dot(Q, K, _NT) * D, 0.0)
    EG = jnp.exp(GC)
    ED = jnp.exp(GL - GC)
    return dict(Q=Q, K=K, V=V, B=B, GC=GC, GL=GL, incl=incl, strict=strict, D=D, KB=KB, A=A, P=P, EG=EG, ED=ED,
                QG=Q * EG, KD=K * ED)


def _neumann_inv(A):
    n = A.shape[0]
    i = lax.broadcasted_iota(jnp.int32, (n, n), 0)
    j = lax.broadcasted_iota(jnp.int32, (n, n), 1)
    N = -A
    acc = jnp.where(i == j, 1.0, 0.0) + N
    Pw = N
    for _ in range(5):
        Pw = _dot3(Pw, Pw, _NN)
        acc = acc + _dot3(acc, Pw, _NN)
    return acc


def _gates(a, alog, dtb):
    return -jnp.exp(alog) * _softplus(a + dtb)


def _hs(x, h, n=CHUNK):
    return x[h * n:(h + 1) * n]


def _per_step(Bl):
    return 4 if Bl % 4 == 0 else 2 if Bl % 2 == 0 else 1


def _gdn_fwd(qkvn, sm, alog, dtb, Bl, S, xchg=None):
    T = Bl * S
    nC = S // CHUNK
    E = _per_step(Bl)

    def body(*refs):
        S_ref = refs[-1]

        @pl.when(pl.program_id(1) == 0)
        def _():
            S_ref[...] = jnp.zeros_like(S_ref)

        for e in range(E):
            one(*[r if i in (5, 6) else r.at[e] for i, r in enumerate(refs)])

    def one(q_ref, k_ref, v_ref, a_ref, b_ref, alog_ref, dtb_ref,
            o_ref, tinv_ref, sst_ref, w_ref, u_ref, gc_ref, beta_ref, S_ref):
        g = _gates(a_ref[...], alog_ref[...], dtb_ref[...])
        beta = jax.nn.sigmoid(b_ref[...])
        gc = jnp.dot(_tri(CHUNK), g, precision=HI, preferred_element_type=F32)
        cm = _gdn_chunk_common(q_ref[...], k_ref[...], v_ref[...], gc, beta)
        tinv = _neumann_inv(cm["A"])
        WU = _dot3(tinv, jnp.concatenate([cm["KB"] * cm["EG"], cm["V"] * cm["B"]], axis=1), _NN)
        W, U = WU[:, :128], WU[:, 128:]
        s_old = [S_ref[h * 128:(h + 1) * 128, :] for h in range(_H)]
        vn = [_hs(U, h) - _dot(_hs(W, h), s_old[h], _NN) for h in range(_H)]
        o_intra = _dot(cm["P"], jnp.concatenate(vn, axis=0), _NN)
        outs = []
        for h in range(_H):
            outs.append(_dot(_hs(cm["QG"], h), s_old[h], _NN) + _hs(o_intra, h))
            gl = jnp.exp(gc[CHUNK - 1:CHUNK, h:h + 1])
            S_ref[h * 128:(h + 1) * 128, :] = gl * s_old[h] + _dot(_hs(cm["KD"], h), vn[h], _TN)
            sst_ref[h * 128:(h + 1) * 128, :] = s_old[h]
        o_ref[...] = jnp.concatenate(outs, axis=1)
        tinv_ref[...] = tinv
        w_ref[...] = _unst(W)
        u_ref[...] = _unst(U)
        gc_ref[...] = gc
        beta_ref[...] = beta

    blk = lambda r, w, c: pl.BlockSpec((E, r, w), lambda b, n: (b, n, c))
    par = pl.BlockSpec((1, 128), lambda b, n: (0, 0))
    out_shape = (_sds((Bl, S, 512)), _sds((Bl, nC * _HC, _HC)), _sds((Bl, nC * 512, 128)),
                 _sds((Bl, S, 512)), _sds((Bl, S, 512)), _sds((Bl, S, 128)), _sds((Bl, S, 128)))
    out_specs = (blk(CHUNK, 512, 0), blk(_HC, _HC, 0), blk(512, 128, 0), blk(CHUNK, 512, 0), blk(CHUNK, 512, 0),
                 blk(CHUNK, 128, 0), blk(CHUNK, 128, 0))
    q3, s3 = qkvn.reshape(Bl, S, -1), sm.reshape(Bl, S, -1)
    res = _pc(body, name="gdn_chunk_fwd", out_shape=out_shape, grid=(Bl // E, nC),
              in_specs=[blk(CHUNK, 512, 0), blk(CHUNK, 512, 1), blk(CHUNK, 512, 2), blk(CHUNK, 128, 6),
                        blk(CHUNK, 128, 7), par, par],
              out_specs=out_specs, scratch=[pltpu.VMEM((E, 512, 128), F32)], sem=("parallel", "arbitrary"),
              xchg=xchg)(q3, q3, q3, s3, s3, alog, dtb)
    return (res[0].reshape(T, 512),) + tuple(res[1:])


def _gdn_bwd(qkvn, sm, alog, dtb, gc_s, beta_s, tinv_s, sst_s, w_s, u_s, do, dcat, Bl, S, xchg=None):
    T = Bl * S
    nC = S // CHUNK
    E = _per_step(Bl)

    def body(*refs):
        acc_ref, dS_ref = refs[-2], refs[-1]
        first_chunk = pl.program_id(1) == 0

        @pl.when(first_chunk)
        def _():
            dS_ref[...] = jnp.zeros_like(dS_ref)

        @pl.when(first_chunk & (pl.program_id(0) == 0))
        def _():
            acc_ref[...] = jnp.zeros_like(acc_ref)

        for e in range(E):
            one(*[r if i in (4, 5, 13, 16) else r.at[e] for i, r in enumerate(refs)])

    def one(q_ref, k_ref, v_ref, a_ref, alog_ref, dtb_ref, gc_ref, beta_ref, tinv_ref, sst_ref, w_ref, u_ref,
            do_ref, alias_ref, dqkv_ref, dsm_ref, acc_ref, dS_ref):
        gc, beta = gc_ref[...], beta_ref[...]
        cm = _gdn_chunk_common(q_ref[...], k_ref[...], v_ref[...], gc, beta)
        Q, K, V, B, D, KB, A, P = (cm[n] for n in ("Q", "K", "V", "B", "D", "KB", "A", "P"))
        EG, ED, QG, KD = cm["EG"], cm["ED"], cm["QG"], cm["KD"]
        tinv = tinv_ref[...]
        W, U, DO = _st(w_ref[...]), _st(u_ref[...]), _st(do_ref[...])
        s_old = [sst_ref[h * 128:(h + 1) * 128, :] for h in range(_H)]
        ds_new = [dS_ref[h * 128:(h + 1) * 128, :] for h in range(_H)]
        VN = jnp.concatenate([_hs(U, h) - _dot(_hs(W, h), s_old[h], _NN) for h in range(_H)], axis=0)
        dP = jnp.where(cm["incl"], _dot(DO, VN, _NT), 0.0)
        dVN0 = _dot(P, DO, _TN)
        dVN, dQG, dKD, dW, TL = [], [], [], [], []
        for h in range(_H):
            gl = jnp.exp(gc[CHUNK - 1:CHUNK, h:h + 1])
            dvn = _hs(dVN0, h) + _dot(_hs(KD, h), ds_new[h], _NN)
            dkd = _dot(_hs(VN, h), ds_new[h], _NT)
            dVN.append(dvn)
            dQG.append(_dot(_hs(DO, h), s_old[h], _NT))
            dKD.append(dkd)
            dW.append(-_dot(dvn, s_old[h], _NT))
            dgl = jnp.sum(jnp.sum(ds_new[h] * s_old[h], axis=1, keepdims=True), axis=0, keepdims=True)
            tl = jnp.sum(jnp.sum(dkd * _hs(KD, h), axis=1, keepdims=True), axis=0, keepdims=True) + dgl * gl
            TL.append(jnp.broadcast_to(tl, (CHUNK, 1)))
            dS_ref[h * 128:(h + 1) * 128, :] = (_dot(_hs(QG, h), _hs(DO, h), _TN) + gl * ds_new[h]
                                                - _dot(_hs(W, h), dvn, _TN))
        dVN, dQG, dKD, dW, TL = (jnp.concatenate(z, axis=0) for z in (dVN, dQG, dKD, dW, TL))
        dB = _dot3(tinv, jnp.concatenate([dVN, dW], axis=1), _TN)
        dVB, dKBE = dB[:, :128], dB[:, 128:]
        dA = jnp.where(cm["strict"], -_dot3(dB, jnp.concatenate([U, W], axis=1), _NT), 0.0)
        dG = dA * D
        dQK = dP * D
        dKB = _dot(dG, K, _NN) + dKBE * EG
        dK = (_dot(jnp.concatenate([dG, dQK], axis=0), jnp.concatenate([KB, Q], axis=0), _TN)
              + dKB * B + dKD * ED)
        dQ = _dot(dQK, K, _NN) + dQG * EG
        Mx = dA * A + dP * P
        rs = lambda z: jnp.sum(z, axis=1, keepdims=True)
        ri = lax.broadcasted_iota(jnp.int32, (_HC, 1), 0)
        dGC = (rs(Mx) - rs(Mx.T) + rs(dKBE * KB * EG) + rs(dQG * QG) - rs(dKD * KD)
               + jnp.where((ri & (CHUNK - 1)) == CHUNK - 1, TL, 0.0))
        dBeta = rs(dVB * V) + rs(dKB * K)
        dqkv_ref[:, 0:512] = _unst(dQ)
        dqkv_ref[:, 512:1024] = _unst(dK)
        dqkv_ref[:, 1024:1536] = _unst(dVB * B)
        dg = jnp.dot(_tri(CHUNK, upper=True), _untile(dGC), precision=HI, preferred_element_type=F32)
        a, alog_v, dtb_v = a_ref[...], alog_ref[...], dtb_ref[...]
        g = _gates(a, alog_v, dtb_v)
        lane = lax.broadcasted_iota(jnp.int32, (CHUNK, 128), 1)
        valid = lane < _H
        da = jnp.where(valid, dg * (-jnp.exp(alog_v)) * jax.nn.sigmoid(a + dtb_v), 0.0)
        dsm_ref[:, 0:128] = da.astype(dsm_ref.dtype)
        dsm_ref[:, 128:256] = jnp.where(valid, _untile(dBeta) * beta * (1.0 - beta), 0.0).astype(dsm_ref.dtype)
        acc_ref[0:1, :] += jnp.sum(jnp.where(valid, dg * g, 0.0), axis=0, keepdims=True)
        acc_ref[1:2, :] += jnp.sum(da, axis=0, keepdims=True)

    blk = lambda r, w, c: pl.BlockSpec((E, r, w), lambda b, n: (b, nC - 1 - n, c))
    par = pl.BlockSpec((1, 128), lambda b, n: (0, 0))
    in_specs = [blk(CHUNK, 512, 0), blk(CHUNK, 512, 1), blk(CHUNK, 512, 2), blk(CHUNK, 128, 6), par, par,
                blk(CHUNK, 128, 0), blk(CHUNK, 128, 0), blk(_HC, _HC, 0), blk(512, 128, 0),
                blk(CHUNK, 512, 0), blk(CHUNK, 512, 0), blk(CHUNK, 512, 0), ANY]
    out_shape = (_sds((Bl, S, 1536)), _sds((Bl, S, DCAT_W), BF), _sds((8, 128)))
    out_specs = (blk(CHUNK, 1536, 0), blk(CHUNK, 256, 3), pl.BlockSpec((8, 128), lambda b, n: (0, 0)))
    q3, s3 = qkvn.reshape(Bl, S, -1), sm.reshape(Bl, S, -1)
    res = _pc(body, name="gdn_chunk_bwd", out_shape=out_shape, grid=(Bl // E, nC), in_specs=in_specs,
              out_specs=out_specs, scratch=[pltpu.VMEM((E, 512, 128), F32)], sem=("arbitrary", "arbitrary"),
              aliases={13: 1}, xchg=xchg)(q3, q3, q3, s3, alog, dtb, gc_s, beta_s, tinv_s, sst_s, w_s, u_s,
                                          do.reshape(Bl, S, -1), dcat.reshape(Bl, S, DCAT_W))
    return (res[0].reshape(T, 1536), res[1].reshape(T, DCAT_W)) + tuple(res[2:])


def _gdn_post(o, z, g, T):
    tt = _pick(T, (2048, 1024, 512, 256))

    def body(o_ref, z_ref, g_ref, out_ref, outt_ref):
        y = (_rmsnorm(o_ref[...], g_ref[...]) * _silu(z_ref[...])).astype(out_ref.dtype)
        out_ref[...] = y
        outt_ref[...] = y.T

    blk = pl.BlockSpec((tt, 128), lambda i, h: (i, h))
    return _pc(body, name="gdn_post_fwd", out_shape=(_sds((T, 1024), BF), _sds((1024, T), BF)), grid=(T // tt, _H),
               in_specs=[blk, blk, pl.BlockSpec((1, 128), lambda i, h: (0, 0))],
               out_specs=(blk, pl.BlockSpec((128, tt), lambda i, h: (h, i))),
               sem=("parallel", "parallel"))(o, z, g)


def _gdn_post_bwd(o, z, g, dmixin, T):
    tt = _pick(T, (2048, 1024, 512, 256))

    def body(o_ref, z_ref, g_ref, d_ref, do_ref, dz_ref, dg_ref):
        @pl.when((pl.program_id(0) == 0) & (pl.program_id(1) == 0))
        def _():
            dg_ref[...] = jnp.zeros_like(dg_ref)

        _, vjp = jax.vjp(lambda a, b, c: _rmsnorm(a, c) * _silu(b), o_ref[...], z_ref[...], g_ref[...])
        do, dz, dg = vjp(d_ref[...])
        do_ref[...] = do
        dz_ref[...] = dz.astype(dz_ref.dtype)
        dg_ref[0:1, :] += dg

    blk = pl.BlockSpec((tt, 128), lambda i, h: (i, h))
    return _pc(body, name="gdn_post_bwd", out_shape=(_sds((T, 512)), _sds((T, DCAT_W), BF), _sds((8, 128))),
               grid=(T // tt, _H), in_specs=[blk, blk, pl.BlockSpec((1, 128), lambda i, h: (0, 0)), blk],
               out_specs=(blk, pl.BlockSpec((tt, 128), lambda i, h: (i, DCAT_Z // 128 + h)),
                          pl.BlockSpec((8, 128), lambda i, h: (0, 0))),
               sem=("arbitrary", "arbitrary"))(o, z, g, dmixin)


def _rms_fwd(sm, g, col_blk, name):
    T = sm.shape[0]
    d = g.shape[1]
    tt = _pick(T, (1024, 512, 256))

    def body(x_ref, g_ref, o_ref, ot_ref):
        y = _rmsnorm(x_ref[...], g_ref[...]).astype(o_ref.dtype)
        o_ref[...] = y
        ot_ref[...] = y.T

    return _pc(body, name=name, out_shape=(_sds((T, d), BF), _sds((d, T), BF)), grid=(T // tt,),
               in_specs=[pl.BlockSpec((tt, d), lambda i: (i, col_blk)), pl.BlockSpec((1, d), lambda i: (0, 0))],
               out_specs=(pl.BlockSpec((tt, d), lambda i: (i, 0)), pl.BlockSpec((d, tt), lambda i: (0, i))),
               sem=("parallel",))(sm, g)


def _rms_bwd(sm, g, dy, dsm, col_blk, name):
    T = sm.shape[0]
    d = g.shape[1]
    tt = _pick(T, (1024, 512, 256))

    def body(x_ref, g_ref, d_ref, alias_ref, dx_ref, dg_ref):
        @pl.when(pl.program_id(0) == 0)
        def _():
            dg_ref[...] = jnp.zeros_like(dg_ref)

        _, vjp = jax.vjp(_rmsnorm, x_ref[...], g_ref[...])
        dx, dg = vjp(d_ref[...])
        dx_ref[...] = dx.astype(dx_ref.dtype)
        dg_ref[0:1, :] += dg

    grp = pl.BlockSpec((tt, d), lambda i: (i, col_blk))
    return _pc(body, name=name, out_shape=(_sds(dsm.shape, dsm.dtype), _sds((8, d))), grid=(T // tt,),
               in_specs=[grp, pl.BlockSpec((1, d), lambda i: (0, 0)), pl.BlockSpec((tt, d), lambda i: (i, 0)), ANY],
               out_specs=(grp, pl.BlockSpec((8, d), lambda i: (0, 0))), aliases={3: 0},
               sem=("arbitrary",))(sm, g, dy, dsm)


def _rope_tables(S):
    inv = ROPE_THETA ** (-jnp.arange(0, MLA_ROPE, 2, dtype=F32) / MLA_ROPE)
    ang = jnp.arange(S, dtype=F32)[:, None] * inv[None, :]
    cos, sin = jnp.cos(ang), jnp.sin(ang)
    z = jnp.zeros((S, 64), F32)
    return jnp.concatenate([cos, cos, z], axis=1), jnp.concatenate([-sin, sin, z], axis=1)


def _swap_halves(x):
    lane = lax.broadcasted_iota(jnp.int32, x.shape, 1)
    return jnp.where(lane < 32, pltpu.roll(x, 96, 1), jnp.where(lane < 64, pltpu.roll(x, 32, 1), 0.0))


def _mm_rows(a, b, *, name, tm, extras, out_shape, out_specs, epi, mode="nn", sem="parallel"):
    M, K = a.shape
    nx = len(extras)

    def body(a_ref, b_ref, *rest):
        r = lax.dot_general(a_ref[...].astype(_MXU), b_ref[...].astype(_MXU), _NN if mode == "nn" else _NT,
                            preferred_element_type=F32)
        epi(r, rest[:nx], rest[nx:])

    in_specs = [pl.BlockSpec((tm, K), lambda i: (i, 0)), pl.BlockSpec(b.shape, lambda i: (0, 0))]
    return _pc(body, name=name, out_shape=out_shape, grid=(M // tm,), in_specs=in_specs + [s for _, s in extras],
               out_specs=out_specs, sem=(sem,))(a, b, *[x for x, _ in extras])


def _qup_rope(cqn, wq, cos, sin, S):
    T = cqn.shape[0]
    tm = _pick(S, (1024, 512, 256))
    nps = S // tm

    def epi(r, xs, outs):
        c, s = xs[0][...], xs[1][...]
        for h in range(MLA_HEADS):
            lo = 256 * h
            rp = r[:, lo + 128:lo + 256]
            outs[0][:, lo:lo + 128] = (r[:, lo:lo + 128] * ATTN_SCALE).astype(BF)
            outs[0][:, lo + 128:lo + 256] = ((rp * c + _swap_halves(rp) * s) * ATTN_SCALE).astype(BF)

    tab = pl.BlockSpec((tm, 128), lambda i: (i % nps, 0))
    return _mm_rows(cqn, wq, name="mm_qup_rope", tm=tm, extras=[(cos, tab), (sin, tab)],
                    out_shape=_sds((T, 1024), BF), out_specs=pl.BlockSpec((tm, 1024), lambda i: (i, 0)), epi=epi)


def _kvup_rope(ckvn, wkv, sm, cos, sin, S):
    T = ckvn.shape[0]
    tm = _pick(S, (1024, 512, 256))
    nps = S // tm

    def epi(r, xs, outs):
        k, c, s = xs[0][...], xs[1][...], xs[2][...]
        kr = (k * c + _swap_halves(k) * s).astype(BF)
        kvb = r.astype(BF)
        outs[0][...] = kvb
        for h in range(MLA_HEADS):
            lo = 256 * h
            outs[1][:, lo:lo + 128] = kvb[:, lo:lo + 128]
            outs[1][:, lo + 128:lo + 256] = kr

    tab = pl.BlockSpec((tm, 128), lambda i: (i % nps, 0))
    row = pl.BlockSpec((tm, 1024), lambda i: (i, 0))
    return _mm_rows(ckvn, wkv, name="mm_kvup_rope", tm=tm,
                    extras=[(sm, pl.BlockSpec((tm, 128), lambda i: (i, 3))), (cos, tab), (sin, tab)],
                    out_shape=(_sds((T, 1024), BF), _sds((T, 1024), BF)), out_specs=(row, row), epi=epi)


def _rope_bwd(dqc, dcat, dkrr, cos, sin, S):
    T = dqc.shape[0]
    tt = _pick(S, (2048, 1024, 512, 256))
    nps = S // tt

    def body(alias_s, dq_ref, dk_ref, c_ref, s_ref, qo_ref, ko_ref):
        c, s = c_ref[...], s_ref[...]
        d = dq_ref[:, 128:256]
        qo_ref[:, 0:128] = dq_ref[:, 0:128].astype(qo_ref.dtype)
        qo_ref[:, 128:256] = (d * c + _swap_halves(d * s)).astype(qo_ref.dtype)

        @pl.when(pl.program_id(1) == 0)
        def _():
            k = dk_ref[...]
            ko_ref[...] = (k * c + _swap_halves(k * s)).astype(ko_ref.dtype)

    tab = pl.BlockSpec((tt, 128), lambda i, h: (i % nps, 0))
    head = pl.BlockSpec((tt, 256), lambda i, h: (i, h))
    return _pc(body, name="rope_bwd", out_shape=(_sds((T, 1024), BF), _sds(dcat.shape, dcat.dtype)),
               grid=(T // tt, MLA_HEADS),
               in_specs=[ANY, head, pl.BlockSpec((tt, 128), lambda i, h: (i, 0)), tab, tab],
               out_specs=(head, pl.BlockSpec((tt, 128), lambda i, h: (i, 3))),
               aliases={0: 1}, sem=("parallel", "arbitrary"))(dcat, dqc, dkrr, cos, sin)


def _attn_scores(q, k_ref, L, tq):
    def sc(lo, hi):
        return _dot(q, k_ref[lo:hi, :], _NT)

    sd = sc(L - tq, L)
    qc = lax.broadcasted_iota(jnp.int32, sd.shape, 0) >> 6
    kc = lax.broadcasted_iota(jnp.int32, sd.shape, 1) >> 6
    sd = jnp.where(kc <= qc, sd, -jnp.inf)
    return sd if L == tq else jnp.concatenate([sc(0, L - tq), sd], axis=1)


def _attn_fwd(mixin, mixin_t, qcat, kcat, kv, Bl, S):
    T = Bl * S
    tq = _pick(S, (512, 256, 128))
    nq = S // tq

    def body(alias_ref, alias_t, q_ref, k_ref, v_ref, o_ref, lse_ref, ot_ref):
        i = pl.program_id(2)
        q = q_ref[...]
        for ii in range(nq):
            @pl.when(i == ii)
            def _(L=(ii + 1) * tq):
                s = _attn_scores(q, k_ref, L, tq)
                m = jnp.max(s, axis=-1, keepdims=True)
                e = jnp.exp(s - m)
                l = jnp.sum(e, axis=-1, keepdims=True)
                o = (_dot(e, v_ref[0:L, :], _NN) / l).astype(o_ref.dtype)
                o_ref[...] = o
                ot_ref[...] = o.T
                lse_ref[...] = jnp.broadcast_to(m + jnp.log(l), lse_ref.shape)

    qrow = lambda b, h, i: b * nq + i
    in_specs = [ANY, ANY,
                pl.BlockSpec((tq, 256), lambda b, h, i: (qrow(b, h, i), h)),
                pl.BlockSpec((S, 256), lambda b, h, i: (b, h)),
                pl.BlockSpec((S, 128), lambda b, h, i: (b, 2 * h + 1))]
    return _pc(body, name="mla_attn_fwd",
               out_shape=(_sds(mixin.shape, mixin.dtype), _sds((T, 512)), _sds(mixin_t.shape, mixin_t.dtype)),
               grid=(Bl, MLA_HEADS, nq), in_specs=in_specs,
               out_specs=(pl.BlockSpec((tq, 128), lambda b, h, i: (qrow(b, h, i), 4 + h)),
                          pl.BlockSpec((tq, 128), lambda b, h, i: (qrow(b, h, i), h)),
                          pl.BlockSpec((128, tq), lambda b, h, i: (4 + h, qrow(b, h, i)))),
               aliases={0: 0, 1: 2}, sem=("parallel", "parallel", "parallel"))(mixin, mixin_t, qcat, kcat, kv)


def _attn_bwd(qcat, kcat, kv, lse, dmixin, Bl, S):
    T = Bl * S
    tq = _pick(S, (512, 256, 128))
    nq = S // tq

    def body(q_ref, k_ref, v_ref, do_ref, lse_ref, dq_ref, dkv_ref, dkr_ref, acck_ref, accv_ref):
        h, i = pl.program_id(1), pl.program_id(2)
        q = q_ref[...]
        do = do_ref[...].astype(BF)
        lse_col = lse_ref[:, 0:1]

        @pl.when(i == 0)
        def _():
            acck_ref[...] = jnp.zeros_like(acck_ref)
            accv_ref[...] = jnp.zeros_like(accv_ref)

        for ii in range(nq):
            @pl.when(i == ii)
            def _(L=(ii + 1) * tq):
                p = jnp.exp(_attn_scores(q, k_ref, L, tq) - lse_col)
                pb = p.astype(BF)
                v = v_ref[0:L, :]
                dp = _dot(do, v, _NT)
                ds = (p * (dp - jnp.sum(dp * p, axis=-1, keepdims=True))).astype(BF)
                dq_ref[...] = _dot(ds, k_ref[0:L, :], _NN) * ATTN_SCALE
                acck_ref[0:L, :] += _dot(ds, q, _TN)
                accv_ref[0:L, :] += _dot(pb, do, _TN)

        @pl.when(i == nq - 1)
        def _():
            dkv_ref[:, 0:128] = acck_ref[:, 0:128].astype(dkv_ref.dtype)
            dkv_ref[:, 128:256] = accv_ref[...].astype(dkv_ref.dtype)
            dkr_ref[...] = jnp.where(h == 0, 0.0, dkr_ref[...]) + acck_ref[:, 128:256]

    qrow = lambda b, h, i: b * nq + i
    in_specs = [pl.BlockSpec((tq, 256), lambda b, h, i: (qrow(b, h, i), h)),
                pl.BlockSpec((S, 256), lambda b, h, i: (b, h)),
                pl.BlockSpec((S, 128), lambda b, h, i: (b, 2 * h + 1)),
                pl.BlockSpec((tq, 128), lambda b, h, i: (qrow(b, h, i), 4 + h)),
                pl.BlockSpec((tq, 128), lambda b, h, i: (qrow(b, h, i), h))]
    out_shape = (_sds((T, 1024)), _sds((T, 1024), BF), _sds((T, 128)))
    out_specs = (pl.BlockSpec((tq, 256), lambda b, h, i: (qrow(b, h, i), h)),
                 pl.BlockSpec((S, 256), lambda b, h, i: (b, h)),
                 pl.BlockSpec((S, 128), lambda b, h, i: (b, 0)))
    return _pc(body, name="mla_attn_bwd", out_shape=out_shape, grid=(Bl, MLA_HEADS, nq), in_specs=in_specs,
               out_specs=out_specs, scratch=[pltpu.VMEM((S, 256), F32), pltpu.VMEM((S, 128), F32)],
               sem=("arbitrary", "arbitrary", "arbitrary"))(qcat, kcat, kv, dmixin, lse)


def _out_ln1(mixin, w_out, x, g, b):
    T, D = x.shape
    tm = _pick(T, (512, 256))

    def epi(r, xs, outs):
        h = _layernorm(ALPHA * xs[0][...] + r, xs[1][...], xs[2][...])
        outs[0][...] = r
        outs[1][...] = h
        hb = h.astype(BF)
        outs[2][...] = hb
        outs[3][...] = hb.T

    row = pl.BlockSpec((tm, D), lambda i: (i, 0))
    par = pl.BlockSpec((1, D), lambda i: (0, 0))
    return _mm_rows(mixin, w_out, name="mm_out_ln1", tm=tm, extras=[(x, row), (g, par), (b, par)],
                    out_shape=(_sds((T, D)), _sds((T, D)), _sds((T, D), BF), _sds((D, T), BF)),
                    out_specs=(row, row, row, pl.BlockSpec((D, tm), lambda i: (0, i))), epi=epi)


def _dh1_ln1_bwd(du_u, wup_u, du_g, wup_g, dgpre, w_gate, dffn, x, mix, g, b):
    T, D = x.shape
    tm = _pick(T, (512, 256))

    def epi(r, xs, outs):
        dug_ref, wg_ref, dgp_ref, wgate_ref, dffn_ref, x_ref, mix_ref, g_ref, b_ref = xs
        acc_ref = outs[2]

        @pl.when(pl.program_id(0) == 0)
        def _():
            acc_ref[...] = jnp.zeros_like(acc_ref)

        dh = (r + _dot(dug_ref[...], wg_ref[...], _NT) + _dot(dgp_ref[...], wgate_ref[...], _NT)
              + ALPHA * dffn_ref[...])
        f = lambda xx, mm, gg, bb: _layernorm(ALPHA * xx + mm, gg, bb)
        _, vjp = jax.vjp(f, x_ref[...], mix_ref[...], g_ref[...], b_ref[...])
        dx, dm, dg, db = vjp(dh)
        outs[0][...] = dx
        outs[1][...] = dm.astype(BF)
        acc_ref[0:1, :] += dg
        acc_ref[1:2, :] += db

    row = pl.BlockSpec((tm, D), lambda i: (i, 0))
    par = pl.BlockSpec((1, D), lambda i: (0, 0))
    whole = lambda a: pl.BlockSpec(a.shape, lambda i: (0, 0))
    extras = [(du_g, pl.BlockSpec((tm, du_g.shape[1]), lambda i: (i, 0))), (wup_g, whole(wup_g)), (dgpre, row),
              (w_gate, whole(w_gate)), (dffn, row), (x, row), (mix, row), (g, par), (b, par)]
    return _mm_rows(du_u, wup_u, name="mm_dh1_ln1_bwd", tm=tm, mode="nt", sem="arbitrary", extras=extras,
                    out_shape=(_sds((T, D)), _sds((T, D), BF), _sds((8, D))),
                    out_specs=(row, row, pl.BlockSpec((8, D), lambda i: (0, 0))), epi=epi)


def _ffn_up_act_fwd(h1b, wup_g, wup_u, wg, wu, bg, bu, Bl, S):
    T, D = h1b.shape
    C = wup_g.shape[1]
    cb = _pick(C, (256, 128))
    kk = wg.shape[0]

    def body(h_ref, pg_ref, pu_ref, wg_ref, wu_ref, bg_ref, bu_ref, ug_ref, uu_ref, o_ref, ot_ref):
        h = h_ref[...]
        ug = _dot(h, pg_ref[...], _NN)
        uu = _dot(h, pu_ref[...], _NN)
        ug_ref[...] = ug
        uu_ref[...] = uu
        cg = _conv(ug, wg_ref[...]) + bg_ref[...]
        cu = _conv(uu, wu_ref[...]) + bu_ref[...]
        a = (_silu(cg) * cu).astype(o_ref.dtype)
        o_ref[...] = a
        ot_ref[...] = a.T

    blk = pl.BlockSpec((S, cb), lambda b, j: (b, j))
    pblk = pl.BlockSpec((D, cb), lambda b, j: (0, j))
    wblk = pl.BlockSpec((kk, cb), lambda b, j: (0, j))
    bblk = pl.BlockSpec((1, cb), lambda b, j: (0, j))
    return _pc(body, name="ffn_up_act_fwd", out_shape=(_sds((T, C)), _sds((T, C)), _sds((T, C), BF), _sds((C, T), BF)),
               grid=(Bl, C // cb), in_specs=[pl.BlockSpec((S, D), lambda b, j: (b, 0)), pblk, pblk, wblk, wblk, bblk, bblk],
               out_specs=(blk, blk, blk, pl.BlockSpec((cb, S), lambda b, j: (j, b))),
               sem=("parallel", "parallel"))(h1b, wup_g, wup_u, wg, wu, bg, bu)


def _ffn_act_bwd(ug, uu, wg, wu, bg, bu, dffnb, w_down, Bl, S):
    T, C = ug.shape
    D = dffnb.shape[1]
    cb = _pick(C, (256, 128))
    kk = wg.shape[0]

    def body(g_ref, u_ref, wg_ref, wu_ref, bg_ref, bu_ref, d_ref, wd_ref,
             dg_ref, du_ref, dwg_ref, dwu_ref, dbg_ref, dbu_ref):
        first = pl.program_id(1) == 0
        wgv, wuv = wg_ref[...], wu_ref[...]
        tg, tu = _taps(g_ref[...], kk), _taps(u_ref[...], kk)
        cg = _conv(None, wgv, tg) + bg_ref[...]
        cu = _conv(None, wuv, tu) + bu_ref[...]
        _, vjp = jax.vjp(lambda a, b: _silu(a) * b, cg, cu)
        dcg, dcu = vjp(_dot(d_ref[...], wd_ref[...], _NT))
        dg_ref[...] = _conv_bwd_x(dcg, wgv).astype(dg_ref.dtype)
        du_ref[...] = _conv_bwd_x(dcu, wuv).astype(du_ref.dtype)
        _conv_bwd_w(tg, dcg, dwg_ref, first)
        _conv_bwd_w(tu, dcu, dwu_ref, first)
        dbg_ref[...] = jnp.where(first, 0.0, dbg_ref[...]) + jnp.sum(dcg, axis=0, keepdims=True)
        dbu_ref[...] = jnp.where(first, 0.0, dbu_ref[...]) + jnp.sum(dcu, axis=0, keepdims=True)

    blk = pl.BlockSpec((S, cb), lambda j, b: (b, j))
    wblk = pl.BlockSpec((kk, cb), lambda j, b: (0, j))
    bblk = pl.BlockSpec((1, cb), lambda j, b: (0, j))
    out_shape = (_sds((T, C), BF), _sds((T, C), BF), _sds((kk, C)), _sds((kk, C)), _sds((1, C)), _sds((1, C)))
    return _pc(body, name="ffn_act_bwd", out_shape=out_shape, grid=(C // cb, Bl),
               in_specs=[blk, blk, wblk, wblk, bblk, bblk, pl.BlockSpec((S, D), lambda j, b: (b, 0)),
                         pl.BlockSpec((cb, D), lambda j, b: (j, 0))],
               out_specs=(blk, blk, wblk, wblk, bblk, bblk), sem=("parallel", "arbitrary"))(
        ug, uu, wg, wu, bg, bu, dffnb, w_down)


def _head(act, w_down, h1, h1b, w_gate, pb, w_proj, tgt, bg, g2, b2):
    T, D = h1.shape
    tm = _pick(T, (512, 256))

    def epi(ffn, xs, outs):
        h_ref, hb_ref, wg_ref, pb_ref, wp_ref, t_ref, bg_ref, g2_ref, b2_ref = xs
        acc_ref = outs[4]

        @pl.when(pl.program_id(0) == 0)
        def _():
            acc_ref[...] = jnp.zeros_like(acc_ref)

        h, tg = h_ref[...], t_ref[...]
        gpre = _dot(hb_ref[...], wg_ref[...], _NN)
        pp = _dot(pb_ref[...], wp_ref[...], _NN)

        def loss_fn(f, gp, p_, bgv, g2v, b2v):
            pre = ALPHA * h + f + jax.nn.sigmoid(gp + bgv) * p_
            err = _layernorm(pre, g2v, b2v) - tg
            return 0.5 * jnp.sum(jnp.mean(err * err, axis=-1, keepdims=True))

        loss, grads = jax.value_and_grad(loss_fn, argnums=(0, 1, 2, 3, 4, 5))(
            ffn, gpre, pp, bg_ref[...], g2_ref[...], b2_ref[...])
        outs[0][...] = grads[0]
        outs[1][...] = grads[0].astype(BF)
        outs[2][...] = grads[1].astype(BF)
        outs[3][...] = grads[2].astype(BF)
        acc_ref[0:1, :] += grads[3]
        acc_ref[1:2, :] += grads[4]
        acc_ref[2:3, :] += grads[5]
        acc_ref[3:4, :] += jnp.broadcast_to(loss, (1, D))

    row = pl.BlockSpec((tm, D), lambda i: (i, 0))
    par = pl.BlockSpec((1, D), lambda i: (0, 0))
    whole = lambda a: pl.BlockSpec(a.shape, lambda i: (0, 0))
    o_bf = _sds((T, D), BF)
    extras = [(h1, row), (h1b, row), (w_gate, whole(w_gate)), (pb, pl.BlockSpec((tm, pb.shape[1]), lambda i: (i, 0))),
              (w_proj, whole(w_proj)), (tgt, row), (bg, par), (g2, par), (b2, par)]
    return _mm_rows(act, w_down, name="mm_down_loss_head", tm=tm, sem="arbitrary", extras=extras,
                    out_shape=(_sds((T, D)), o_bf, o_bf, o_bf, _sds((8, D))),
                    out_specs=(row, row, row, row, pl.BlockSpec((8, D), lambda i: (0, 0))), epi=epi)


def _adamw(parts, w, m, v, name):
    R, C = w.shape
    tr = R if R <= 512 else _pick(R, (256,))

    def body(p_ref, w_ref, m_ref, v_ref, g_ref, d_ref, nm_ref, nv_ref):
        g = p_ref[0].astype(F32)
        for j in range(1, N_DEV):
            g = g + p_ref[j].astype(F32)
        g_ref[...] = g
        d_ref[...], nm_ref[...], nv_ref[...] = _adam_math(g, w_ref[...], m_ref[...], v_ref[...])

    blk = pl.BlockSpec((tr, C), lambda i: (i, 0))
    o = _sds((R, C))
    return _pc(body, name=name, out_shape=(o, o, o, o), grid=(R // tr,),
               in_specs=[pl.BlockSpec((N_DEV, tr, C), lambda i: (0, i, 0)), blk, blk, blk],
               out_specs=(blk, blk, blk, blk), sem=("parallel",))(parts, w, m, v)


def _adam_math(g, w, m, v):
    mm = ADAM_B1 * m + (1.0 - ADAM_B1) * g
    vv = ADAM_B2 * v + (1.0 - ADAM_B2) * jnp.square(g)
    m_hat = mm / (1.0 - ADAM_B1 ** ADAM_STEP)
    v_hat = vv / (1.0 - ADAM_B2 ** ADAM_STEP)
    return -ADAM_LR * (m_hat / (jnp.sqrt(v_hat) + ADAM_EPS) + ADAM_WD * w), mm, vv


def _small_layout(sizes):
    offs, off = {}, 0
    for n in SMALL_ORDER:
        offs[n] = off
        off += -(-sizes[n] // 128) * 128
    return offs, off


def _pack_small(d):
    return jnp.concatenate([_pad_lanes(d[n].reshape(1, -1), -(-d[n].size // 128) * 128) for n in SMALL_ORDER], axis=1)


def _adamw_small(parts, ws, ms, vs):
    k = len(SMALL_ORDER)
    sizes = {n: ws[n].shape[1] for n in SMALL_ORDER}
    offs, _ = _small_layout(sizes)

    def body(p_ref, *refs):
        ins, outs = refs[:3 * k], refs[3 * k:]
        for i, n in enumerate(SMALL_ORDER):
            lo, hi = offs[n], offs[n] + sizes[n]
            g = p_ref[0, :, lo:hi]
            for j in range(1, N_DEV):
                g = g + p_ref[j, :, lo:hi]
            d, mm, vv = _adam_math(g, ins[i][...], ins[k + i][...], ins[2 * k + i][...])
            outs[4 * i][...], outs[4 * i + 1][...], outs[4 * i + 2][...], outs[4 * i + 3][...] = g, d, mm, vv

    out_shape = tuple(_sds((1, sizes[n])) for n in SMALL_ORDER for _ in range(4))
    args = [ws[n] for n in SMALL_ORDER] + [ms[n] for n in SMALL_ORDER] + [vs[n] for n in SMALL_ORDER]
    res = _pc(body, name="adamw_small", out_shape=out_shape)(parts, *args)
    return {n: tuple(res[4 * i:4 * i + 4]) for i, n in enumerate(SMALL_ORDER)}


def _prep(x, p, xchg=None):
    T, D = x.shape
    Dp = p.shape[1]
    tt = _pick(T, (512, 256))

    def body(x_ref, p_ref, xb_ref, xt_ref, pb_ref, pt_ref):
        xb = x_ref[...].astype(BF)
        xb_ref[...] = xb
        xt_ref[...] = xb.T
        pb = p_ref[...].astype(BF)
        pb_ref[...] = pb
        pt_ref[...] = pb.T

    row = lambda w: pl.BlockSpec((tt, w), lambda i: (i, 0))
    col = lambda w: pl.BlockSpec((w, tt), lambda i: (0, i))
    return _pc(body, name="prep_inputs",
               out_shape=(_sds((T, D), BF), _sds((D, T), BF), _sds((T, Dp), BF), _sds((Dp, T), BF)),
               grid=(T // tt,), in_specs=[row(D), row(Dp)], out_specs=(row(D), col(D), row(Dp), col(Dp)),
               sem=("parallel",), xchg=xchg)(x, p)


SMALL_ORDER = ("gdn_a_log", "gdn_dt_bias", "gdn_norm_g", "mla_q_norm_g", "mla_kv_norm_g", "ln1_g", "ln1_b",
               "ffn_conv_b", "ple_b_gate", "ln2_g", "ln2_b")
EARLY = ("w_in", "gdn_conv_w")
LATE = ("mla_w_q_up", "mla_w_kv_up", "w_out", "ffn_w_up", "ffn_conv_w", "ffn_w_down", "ple_w_gate", "ple_w_proj")
GRADS_EARLY = ("w_out", "ffn_w_up", "ffn_conv_w", "ffn_w_down", "ple_w_gate", "ple_w_proj")
GRADS_MID = ("gdn_conv_w", "mla_w_q_up", "mla_w_kv_up")
GRADS_LATE = ("w_in",)


def _pad_lanes(v, n=128):
    return jnp.pad(v, ((0, 0), (0, n - v.shape[1])))


def _local_step(x, p, tgt, W, sp, Bl, S, early_weights=None, late_weights=None, early_grads=None, mid_grads=None,
                late_grads=None):
    T = Bl * S
    W = dict(W)
    prep = _prep(x, p, xchg=None if early_weights is None else early_weights[:2])
    xb, xT, pb, pT = prep[:4]
    if early_weights is not None:
        W.update(early_weights[2](prep[4:]))
    w_in = W["w_in"]
    wqkv, wz = w_in[:, :1536], w_in[:, 1536:2048]
    z64 = jnp.zeros((w_in.shape[0], 64), w_in.dtype)
    z124 = jnp.zeros((w_in.shape[0], 124), w_in.dtype)
    wsm = jnp.concatenate([w_in[:, 2056:2440], w_in[:, 2696:2760], z64, w_in[:, 2440:2696],
                           w_in[:, 2048:2052], z124, w_in[:, 2052:2056], z124], axis=1)
    alog, dtb = _pad_lanes(sp["gdn_a_log"]), _pad_lanes(sp["gdn_dt_bias"])
    cos, sin = _rope_tables(S)

    qkv, qkvn = _gdn_proj_pre_fwd(xb, wqkv, W["gdn_conv_w"], Bl, S)
    z = _mm(xb, wz, "nn", name="mm_z")
    sm = _mm(xb, wsm, "nn", name="mm_sm")
    gdn_out = _gdn_fwd(qkvn, sm, alog, dtb, Bl, S, xchg=None if late_weights is None else late_weights[:2])
    o, tinv_s, sst_s, w_s, u_s, gc_s, beta_s = gdn_out[:7]
    if late_weights is not None:
        W.update(late_weights[2](gdn_out[7:]))
    wq = jnp.pad(W["mla_w_q_up"].reshape(-1, MLA_HEADS, 192), ((0, 0), (0, 0), (0, 64))).reshape(-1, 1024)
    wkv = W["mla_w_kv_up"]
    C = W["ffn_w_down"].shape[0]
    wup_g, wup_u = W["ffn_w_up"][:, :C], W["ffn_w_up"][:, C:]
    cw_g, cw_u = W["ffn_conv_w"][:, :C], W["ffn_conv_w"][:, C:]
    cb_g, cb_u = sp["ffn_conv_b"][:, :C], sp["ffn_conv_b"][:, C:]
    mixin, mixin_t = _gdn_post(o, z, sp["gdn_norm_g"], T)
    cqn, cqn_t = _rms_fwd(sm, sp["mla_q_norm_g"], 0, "rms_q_fwd")
    ckvn, ckvn_t = _rms_fwd(sm, sp["mla_kv_norm_g"], 2, "rms_kv_fwd")
    qcat = _qup_rope(cqn, wq, cos, sin, S)
    kv, kcat = _kvup_rope(ckvn, wkv, sm, cos, sin, S)
    mixin, lse, mixin_t = _attn_fwd(mixin, mixin_t, qcat, kcat, kv, Bl, S)
    mix, h1, h1b, h1T = _out_ln1(mixin, W["w_out"], x, sp["ln1_g"], sp["ln1_b"])
    ug, uu, act, act_t = _ffn_up_act_fwd(h1b, wup_g, wup_u, cw_g, cw_u, cb_g, cb_u, Bl, S)

    dffn, dffnb, dgpre, dpp, hacc = _head(act, W["ffn_w_down"], h1, h1b, W["ple_w_gate"], pb, W["ple_w_proj"], tgt,
                                          sp["ple_b_gate"], sp["ln2_g"], sp["ln2_b"])
    loss = hacc[3, 0]
    gW, gs = {}, {}
    gs["ple_b_gate"], gs["ln2_g"], gs["ln2_b"] = hacc[0:1], hacc[1:2], hacc[2:3]
    gW["ple_w_proj"] = _mm(pT, dpp, "nn", name="mm_dproj", out_dtype=BF)
    gW["ple_w_gate"] = _mm(h1T, dgpre, "nn", name="mm_dgate", out_dtype=BF)
    gW["ffn_w_down"] = _mm(act_t, dffnb, "nn", name="mm_ddown", out_dtype=BF)
    du_g, du_u, dcw_g, dcw_u, dcb_g, dcb_u = _ffn_act_bwd(ug, uu, cw_g, cw_u, cb_g, cb_u, dffnb, W["ffn_w_down"],
                                                          Bl, S)
    gW["ffn_conv_w"] = jnp.concatenate([dcw_g, dcw_u], axis=1)
    gs["ffn_conv_b"] = jnp.concatenate([dcb_g, dcb_u], axis=1)
    gW["ffn_w_up"] = jnp.concatenate([_mm(h1T, du_g, "nn", name="mm_dup_gate", out_dtype=BF),
                                      _mm(h1T, du_u, "nn", name="mm_dup_up", out_dtype=BF)], axis=1)
    dxa, dmix, acc1 = _dh1_ln1_bwd(du_u, wup_u, du_g, wup_g, dgpre, W["ple_w_gate"], dffn, x, mix,
                                   sp["ln1_g"], sp["ln1_b"])
    gs["ln1_g"], gs["ln1_b"] = acc1[0:1], acc1[1:2]
    gW["w_out"] = _mm(mixin_t, dmix, "nn", name="mm_dwout", out_dtype=BF)
    dmixin = _mm(dmix, W["w_out"], "nt", name="mm_dmixin")
    do, dcat, gacc = _gdn_post_bwd(o, z, sp["gdn_norm_g"], dmixin, T)
    gs["gdn_norm_g"] = gacc[0:1]
    bwd_out = _gdn_bwd(qkvn, sm, alog, dtb, gc_s, beta_s, tinv_s, sst_s, w_s, u_s, do, dcat, Bl, S,
                       xchg=None if early_grads is None else early_grads({n: gW[n] for n in GRADS_EARLY}))
    dqkvn, dcat, cacc = bwd_out[:3]
    early_recv = bwd_out[3:]
    gs["gdn_a_log"], gs["gdn_dt_bias"] = cacc[0:1, :GDN_HEADS], cacc[1:2, :GDN_HEADS]
    dcat, gW["gdn_conv_w"] = _gdn_pre_bwd(qkv, W["gdn_conv_w"], dqkvn, dcat, Bl, S)
    dqc, dkv, dkrr = _attn_bwd(qcat, kcat, kv, lse, dmixin, Bl, S)
    dqraw, dcat = _rope_bwd(dqc, dcat, dkrr, cos, sin, S)
    gwq = _mm(cqn_t, dqraw, "nn", name="mm_dwq", out_dtype=BF)
    gW["mla_w_q_up"] = gwq.reshape(-1, MLA_HEADS, 256)[:, :, :192].reshape(-1, MLA_HEADS * 192)
    gW["mla_w_kv_up"] = _mm(ckvn_t, dkv, "nn", name="mm_dwkv", out_dtype=BF)
    dcqn = _mm(dqraw, wq, "nt", name="mm_dcqn")
    dckvn = _mm(dkv, wkv, "nt", name="mm_dckvn")
    dcat, qacc = _rms_bwd(sm, sp["mla_q_norm_g"], dcqn, dcat, 0, "rms_q_bwd")
    dcat, kacc = _rms_bwd(sm, sp["mla_kv_norm_g"], dckvn, dcat, 2, "rms_kv_bwd")
    gs["mla_q_norm_g"], gs["mla_kv_norm_g"] = qacc[0:1], kacc[0:1]
    gcat = _mm(xT, dcat, "nn", name="mm_dwin", out_dtype=BF,
               xchg=None if mid_grads is None else mid_grads(gW, gs))
    mid_recv = ()
    if mid_grads is not None:
        gcat, mid_recv = gcat[0], gcat[1:]
    gsm, gqkv, gz = gcat[:, :DCAT_QKV], gcat[:, DCAT_QKV:DCAT_Z], gcat[:, DCAT_Z:]
    gW["w_in"] = jnp.concatenate([gqkv, gz, gsm[:, 768:772], gsm[:, 896:900], gsm[:, 0:384], gsm[:, 512:768],
                                  gsm[:, 384:448]], axis=1)
    wcat = jnp.concatenate([wsm, wqkv, wz], axis=1)
    dx = _mm(dcat, wcat, "nt", name="mm_dx", add=dxa, xchg=None if late_grads is None else late_grads(gW))
    late_recv = ()
    if late_grads is not None:
        dx, late_recv = dx[0], dx[1:]
    return loss, dx, gW, gs, early_recv, mid_recv, late_recv


COL_SHARDED = ("w_in", "mla_w_q_up", "mla_w_kv_up", "ffn_w_up", "ple_w_proj", "gdn_conv_w", "ffn_conv_w")
SHARDED = EARLY + LATE
WEIGHTS = ("w_in", "gdn_conv_w", "gdn_a_log", "gdn_dt_bias", "gdn_norm_g", "mla_q_norm_g", "mla_w_q_up",
           "mla_kv_norm_g", "mla_w_kv_up", "w_out", "ln1_g", "ln1_b", "ffn_w_up", "ffn_conv_w", "ffn_conv_b",
           "ffn_w_down", "ple_w_gate", "ple_b_gate", "ple_w_proj", "ln2_g", "ln2_b")
CONV = ("gdn_conv_w", "ffn_conv_w")


def _gathered_to_full(name, g):
    if name in COL_SHARDED:
        return jnp.transpose(g, (1, 0, 2)).reshape(g.shape[1], -1)
    return g.reshape(-1, g.shape[-1])


def _full_to_blocks(name, gfull, shard_shape):
    r, c = shard_shape
    if name in COL_SHARDED:
        return jnp.transpose(gfull.reshape(r, N_DEV, c), (1, 0, 2))
    return gfull.reshape(N_DEV, r, c)


def kernel(x, p, w_in, gdn_conv_w, gdn_a_log, gdn_dt_bias, gdn_norm_g, mla_q_norm_g, mla_w_q_up, mla_kv_norm_g, mla_w_kv_up, w_out, ln1_g, ln1_b, ffn_w_up, ffn_conv_w, ffn_conv_b, ffn_w_down, ple_w_gate, ple_b_gate, ple_w_proj, ln2_g, ln2_b, loss_target, m_w_in, m_gdn_conv_w, m_gdn_a_log, m_gdn_dt_bias, m_gdn_norm_g, m_mla_q_norm_g, m_mla_w_q_up, m_mla_kv_norm_g, m_mla_w_kv_up, m_w_out, m_ln1_g, m_ln1_b, m_ffn_w_up, m_ffn_conv_w, m_ffn_conv_b, m_ffn_w_down, m_ple_w_gate, m_ple_b_gate, m_ple_w_proj, m_ln2_g, m_ln2_b, v_w_in, v_gdn_conv_w, v_gdn_a_log, v_gdn_dt_bias, v_gdn_norm_g, v_mla_q_norm_g, v_mla_w_q_up, v_mla_kv_norm_g, v_mla_w_kv_up, v_w_out, v_ln1_g, v_ln1_b, v_ffn_w_up, v_ffn_conv_w, v_ffn_conv_b, v_ffn_w_down, v_ple_w_gate, v_ple_b_gate, v_ple_w_proj, v_ln2_g, v_ln2_b):
    loc = dict(locals())
    wts = {n: loc[n] for n in WEIGHTS}
    ms = {n: loc["m_" + n] for n in WEIGHTS}
    vs = {n: loc["v_" + n] for n in WEIGHTS}
    Bl, S, D = x.shape
    T = Bl * S

    wire = lambda n: wts[n][0] if n in CONV else wts[n][0].astype(BF)
    gather = lambda names, mode: ([wire(n) for n in names], [mode] * len(names),
                                  lambda res: {n: _gathered_to_full(n, g) for n, g in zip(names, res)})
    blocks = lambda g, names: [_full_to_blocks(n, g[n], wts[n].shape[1:]) for n in names]
    pack_early = lambda g: (blocks(g, GRADS_EARLY), ["a2a"] * len(GRADS_EARLY))
    pack_mid = lambda g, gs: (blocks(g, GRADS_MID) + [_pack_small(gs)], ["a2a"] * len(GRADS_MID) + ["ag"])
    pack_late = lambda g: (blocks(g, GRADS_LATE), ["a2a"] * len(GRADS_LATE))
    sp = {n: wts[n].reshape(1, -1) for n in SMALL_ORDER}

    loss, dx, gW, gs, early_recv, mid_recv, late_recv = _local_step(
        x.reshape(T, D), p.reshape(T, -1), loss_target.reshape(T, D), {}, sp, Bl, S,
        early_weights=gather(EARLY, "ag2"), late_weights=gather(LATE, "ag2"), early_grads=pack_early,
        mid_grads=pack_mid, late_grads=pack_late)

    res = {}
    for n, parts in (list(zip(GRADS_EARLY, early_recv)) + list(zip(GRADS_MID, mid_recv[:-1]))
                     + list(zip(GRADS_LATE, late_recv))):
        res[n] = tuple(t[None] for t in _adamw(parts, wts[n][0], ms[n][0], vs[n][0], "adamw_" + n))
    res.update(_adamw_small(mid_recv[-1], wts, ms, vs))

    loss = lax.psum(loss, ("x", "y", "c"))
    outs = [loss, dx.reshape(Bl, S, D)]
    for k in range(4):
        outs += [res[n][k] for n in WEIGHTS]
    return tuple(outs)
```
